```python
import math
import jax, jax.numpy as jnp
from jax import lax
import numpy as np

D_MODEL = 1024
BATCH = 8
SEQ = 2048
DEPTH = 1

D_MIX = D_MODEL
D_MLSTM = D_MIX // 2
N_MLSTM_HEADS = 4
MLSTM_HEAD_DIM = D_MLSTM // N_MLSTM_HEADS
MLSTM_CHUNK = 64
CONV_WIDTH = 4
D_S5 = D_MIX - D_MLSTM
S5_GROUP = 16
N_S5_GROUPS = D_S5 // S5_GROUP
S5_STATE = 64
D_FF = -(-8 * D_MODEL // (3 * 256)) * 256
D_IN = 4 * D_MLSTM + 2 * N_MLSTM_HEADS + D_S5
EPS = 1e-6

kernel_name = 'hymba_mlstm_s5_hybrid_block'


def rmsnorm(x, g):
    xf = x.astype(jnp.float32)
    y = xf * lax.rsqrt(jnp.mean(xf * xf, axis=-1, keepdims=True) + EPS)
    return (y * g.astype(jnp.float32)).astype(x.dtype)


def causal_depthwise_conv(x, w):
    K, C = w.shape
    return lax.conv_general_dilated(x, w[:, None, :].astype(x.dtype), window_strides=(1,), padding=[(K - 1, 0)], dimension_numbers=('NWC', 'WIO', 'NWC'), feature_group_count=C)


def mlstm_chunkwise(q, k, v, log_i, log_f):
    Bsz, H, S, Dh = q.shape
    L = MLSTM_CHUNK
    NC = S // L
    q = q.astype(jnp.float32).reshape(Bsz, H, NC, L, Dh)
    k = k.astype(jnp.float32).reshape(Bsz, H, NC, L, Dh)
    v = v.astype(jnp.float32).reshape(Bsz, H, NC, L, Dh)
    li = log_i.astype(jnp.float32).reshape(Bsz, H, NC, L)
    lf = log_f.astype(jnp.float32).reshape(Bsz, H, NC, L)
    b = jnp.cumsum(lf, axis=-1)
    b_last = b[..., -1]
    w_state = b_last[..., None] - b + li
    m_loc = jnp.max(w_state, axis=-1)
    e = jnp.exp(w_state - m_loc[..., None])
    C_loc = jnp.einsum('bhcsk,bhcsv->bhckv', e[..., None] * k, v)
    n_loc = jnp.einsum('bhcs,bhcsk->bhck', e, k)

    def step(carry, inp):
        C, n, m = carry
        Cl, nl, ml, bl = inp
        m_new = jnp.maximum(bl + m, ml)
        a = jnp.exp(bl + m - m_new)
        c = jnp.exp(ml - m_new)
        C_new = a[..., None, None] * C + c[..., None, None] * Cl
        n_new = a[..., None] * n + c[..., None] * nl
        return (C_new, n_new, m_new), (C, n, m)

    init = (jnp.zeros((Bsz, H, Dh, Dh), jnp.float32), jnp.zeros((Bsz, H, Dh), jnp.float32), jnp.zeros((Bsz, H), jnp.float32))
    xs = (jnp.moveaxis(C_loc, 2, 0), jnp.moveaxis(n_loc, 2, 0), jnp.moveaxis(m_loc, 2, 0), jnp.moveaxis(b_last, 2, 0))
    _, (C_prev, n_prev, m_prev) = lax.scan(step, init, xs)
    C_prev = jnp.moveaxis(C_prev, 0, 2)
    n_prev = jnp.moveaxis(n_prev, 0, 2)
    m_prev = jnp.moveaxis(m_prev, 0, 2)

    log_inter = b + m_prev[..., None]
    causal = jnp.tril(jnp.ones((L, L), dtype=bool))
    D = b[..., :, None] - b[..., None, :] + li[..., None, :]
    D = jnp.where(causal, D, -jnp.inf)
    m_t = jnp.maximum(log_inter, jnp.max(D, axis=-1))
    inter_w = jnp.exp(log_inter - m_t)
    P = jnp.exp(D - m_t[..., None]) * jnp.einsum('bhctd,bhcsd->bhcts', q, k)
    num = inter_w[..., None] * jnp.einsum('bhctk,bhckv->bhctv', q, C_prev) + jnp.einsum('bhcts,bhcsv->bhctv', P, v)
    den = inter_w * jnp.einsum('bhctk,bhck->bhct', q, n_prev) + jnp.sum(P, axis=-1)
    h = num / jnp.maximum(jnp.abs(den), jnp.exp(-m_t))[..., None]
    return h.reshape(Bsz, H, S, Dh)


def s5_ssm(u, a_re, a_im, log_dt, b_re, b_im, c_re, c_im, d_skip):
    Bsz, S, _ = u.shape
    f32 = jnp.float32
    uf = u.astype(f32).reshape(Bsz, S, N_S5_GROUPS, S5_GROUP)
    a_re = a_re.astype(f32); a_im = a_im.astype(f32)
    dt = jnp.exp(log_dt.astype(f32))[:, None]
    mag = jnp.exp(dt * a_re)
    ang = dt * a_im
    ab_re = mag * jnp.cos(ang)
    ab_im = mag * jnp.sin(ang)
    den = a_re * a_re + a_im * a_im
    nr = ab_re - 1.0
    z_re = (nr * a_re + ab_im * a_im) / den
    z_im = (ab_im * a_re - nr * a_im) / den
    b_re = b_re.astype(f32); b_im = b_im.astype(f32)
    bb_re = z_re[..., None] * b_re - z_im[..., None] * b_im
    bb_im = z_re[..., None] * b_im + z_im[..., None] * b_re
    bu_re = jnp.einsum('bsgh,gph->sbgp', uf, bb_re)
    bu_im = jnp.einsum('bsgh,gph->sbgp', uf, bb_im)
    a_seq_re = jnp.broadcast_to(ab_re[None, None], (S, 1, N_S5_GROUPS, S5_STATE))
    a_seq_im = jnp.broadcast_to(ab_im[None, None], (S, 1, N_S5_GROUPS, S5_STATE))

    def combine(e1, e2):
        a1r, a1i, x1r, x1i = e1
        a2r, a2i, x2r, x2i = e2
        ar = a2r * a1r - a2i * a1i
        ai = a2r * a1i + a2i * a1r
        xr = a2r * x1r - a2i * x1i + x2r
        xi = a2r * x1i + a2i * x1r + x2i
        return ar, ai, xr, xi

    _, _, xr, xi = lax.associative_scan(combine, (a_seq_re, a_seq_im, bu_re, bu_im), axis=0)
    y = jnp.einsum('ghp,sbgp->bsgh', c_re.astype(f32), xr) - jnp.einsum('ghp,sbgp->bsgh', c_im.astype(f32), xi)
    y = y.reshape(Bsz, S, D_S5) + d_skip.astype(f32) * uf.reshape(Bsz, S, D_S5)
    return y


def setup_inputs(seed: int = 0) -> dict:
    key = jax.random.key(seed)
    ks = jax.random.split(key, 24)
    f32 = jnp.float32
    H = N_MLSTM_HEADS
    G = N_S5_GROUPS
    P = S5_STATE

    def nrm(k, shape, scale):
        return jax.random.normal(k, shape, f32) * scale

    x = nrm(ks[0], (BATCH, SEQ, D_MODEL), 1.0)
    norm1_g = 1.0 + nrm(ks[1], (DEPTH, D_MODEL), 0.02)
    w_in = nrm(ks[2], (DEPTH, D_MODEL, D_IN), D_MODEL ** -0.5)
    f_bias = jnp.linspace(3.0, 6.0, H, dtype=f32)
    if_bias = jnp.concatenate([nrm(ks[3], (DEPTH, H), 0.1), f_bias[None] + nrm(ks[4], (DEPTH, H), 0.01)], axis=-1)
    conv_qk = nrm(ks[5], (DEPTH, CONV_WIDTH, 2 * D_MLSTM), CONV_WIDTH ** -0.5)
    mlstm_norm_g = 1.0 + nrm(ks[6], (DEPTH, D_MLSTM), 0.02)
    n_idx = jnp.arange(P, dtype=f32)
    a_re = -0.5 + nrm(ks[7], (DEPTH, G, P), 0.01)
    a_im = math.pi * n_idx + nrm(ks[8], (DEPTH, G, P), 0.01)
    log_dt = jax.random.uniform(ks[9], (DEPTH, G), f32, math.log(1e-3), math.log(1e-1))
    b_re = nrm(ks[10], (DEPTH, G, P, S5_GROUP), (2 * S5_GROUP) ** -0.5)
    b_im = nrm(ks[11], (DEPTH, G, P, S5_GROUP), (2 * S5_GROUP) ** -0.5)
    c_re = nrm(ks[12], (DEPTH, G, S5_GROUP, P), P ** -0.5)
    c_im = nrm(ks[13], (DEPTH, G, S5_GROUP, P), P ** -0.5)
    d_skip = nrm(ks[14], (DEPTH, D_S5), 0.5)
    w_glu = nrm(ks[15], (DEPTH, D_S5, D_S5), D_S5 ** -0.5)
    b_glu = nrm(ks[16], (DEPTH, D_S5), 0.01)
    s5_norm_g = 1.0 + nrm(ks[17], (DEPTH, D_S5), 0.02)
    w_out = nrm(ks[18], (DEPTH, D_MIX, D_MODEL), D_MIX ** -0.5)
    norm2_g = 1.0 + nrm(ks[19], (DEPTH, D_MODEL), 0.02)
    w_gate = nrm(ks[20], (DEPTH, D_MODEL, D_FF), D_MODEL ** -0.5)
    w_up = nrm(ks[21], (DEPTH, D_MODEL, D_FF), D_MODEL ** -0.5)
    w_down = nrm(ks[22], (DEPTH, D_FF, D_MODEL), D_FF ** -0.5)
    norm_f_g = 1.0 + nrm(ks[23], (D_MODEL,), 0.02)
    return {'x': x, 'norm1_g': norm1_g, 'w_in': w_in, 'if_bias': if_bias, 'conv_qk': conv_qk, 'mlstm_norm_g': mlstm_norm_g, 'a_re': a_re, 'a_im': a_im, 'log_dt': log_dt, 'b_re': b_re, 'b_im': b_im, 'c_re': c_re, 'c_im': c_im, 'd_skip': d_skip, 'w_glu': w_glu, 'b_glu': b_glu, 's5_norm_g': s5_norm_g, 'w_out': w_out, 'norm2_g': norm2_g, 'w_gate': w_gate, 'w_up': w_up, 'w_down': w_down, 'norm_f_g': norm_f_g}


def reference(x, norm1_g, w_in, if_bias, conv_qk, mlstm_norm_g, a_re, a_im, log_dt, b_re, b_im, c_re, c_im, d_skip, w_glu, b_glu, s5_norm_g, w_out, norm2_g, w_gate, w_up, w_down, norm_f_g):
    Bsz, S, _ = x.shape
    H, Dh = N_MLSTM_HEADS, MLSTM_HEAD_DIM
    h = x
    for l in range(DEPTH):
        xn = rmsnorm(h, norm1_g[l])
        proj = xn @ w_in[l]
        qk = proj[..., :2 * D_MLSTM]
        v = proj[..., 2 * D_MLSTM:3 * D_MLSTM]
        o_pre = proj[..., 3 * D_MLSTM:4 * D_MLSTM]
        gates = (proj[..., 4 * D_MLSTM:4 * D_MLSTM + 2 * H] + if_bias[l]).astype(jnp.float32)
        u = proj[..., 4 * D_MLSTM + 2 * H:]

        qk = jax.nn.silu(causal_depthwise_conv(qk, conv_qk[l]))
        q = qk[..., :D_MLSTM].reshape(Bsz, S, H, Dh).transpose(0, 2, 1, 3)
        k = (qk[..., D_MLSTM:] * (Dh ** -0.5)).reshape(Bsz, S, H, Dh).transpose(0, 2, 1, 3)
        vh = v.reshape(Bsz, S, H, Dh).transpose(0, 2, 1, 3)
        log_i = gates[..., :H].transpose(0, 2, 1)
        log_f = jax.nn.log_sigmoid(gates[..., H:]).transpose(0, 2, 1)
        h_tilde = mlstm_chunkwise(q, k, vh, log_i, log_f).transpose(0, 2, 1, 3)
        hm = jax.nn.sigmoid(o_pre.astype(jnp.float32)).reshape(Bsz, S, H, Dh) * h_tilde
        mu = jnp.mean(hm, axis=-1, keepdims=True)
        var = jnp.mean(jnp.square(hm - mu), axis=-1, keepdims=True)
        hm = ((hm - mu) * lax.rsqrt(var + EPS)).reshape(Bsz, S, D_MLSTM) * mlstm_norm_g[l].astype(jnp.float32)

        y = s5_ssm(u, a_re[l], a_im[l], log_dt[l], b_re[l], b_im[l], c_re[l], c_im[l], d_skip[l])
        g = jax.nn.gelu(y)
        g = g * jax.nn.sigmoid(g @ w_glu[l].astype(jnp.float32) + b_glu[l].astype(jnp.float32))
        hs = rmsnorm(g, s5_norm_g[l])

        mixed = jnp.concatenate([hm.astype(x.dtype), hs.astype(x.dtype)], axis=-1)
        h = h + mixed @ w_out[l]

        xn2 = rmsnorm(h, norm2_g[l])
        h = h + (jax.nn.silu(xn2 @ w_gate[l]) * (xn2 @ w_up[l])) @ w_down[l]
    return rmsnorm(h, norm_f_g)
```

```python
import functools
import math

import jax
import jax.numpy as jnp
from jax import lax
from jax.experimental import pallas as pl
from jax.experimental.pallas import tpu as pltpu

EPS = 1e-6
N_MLSTM_HEADS = 4
CONV_WIDTH = 4
S5_GROUP = 16
S5_STATE = 64

LANES = 128
SUBLANES = 8
MLSTM_CHUNK = 128
S5_CHUNK = 16
VMEM_LIMIT = 48 * 1024 * 1024

F32 = jnp.float32
BF16 = jnp.bfloat16


def _rmsnorm(x, g):
    ms = jnp.mean(x * x, axis=-1, keepdims=True)
    return x * lax.rsqrt(ms + EPS) * g


def _sigmoid(x):
    return 1.0 / (1.0 + jnp.exp(-x))


def _log_sigmoid(x):
    return -(jnp.maximum(-x, 0.0) + jnp.log1p(jnp.exp(-jnp.abs(x))))


def _const_spec(shape):
    return pl.BlockSpec(shape, lambda *_: (0,) * len(shape), pipeline_mode=pl.Buffered(1))


def _inproj_kernel(x_ref, g1_ref, wm_ref, wg_ref, wu_ref, gb_ref, cw_ref,
                   q_ref, kt_ref, v_ref, o_ref, gates_ref, gatest_ref, u_ref,
                   ext_ref, *, tiles_per_seq, d_mlstm, k_scale):
    tm = x_ref.shape[0]
    dm = d_mlstm
    halo = SUBLANES
    xn = _rmsnorm(x_ref[...], g1_ref[...]).astype(BF16)

    @pl.when(pl.program_id(0) % tiles_per_seq == 0)
    def _():
        ext_ref[0:halo, :] = jnp.zeros((halo, 2 * dm), F32)

    ext_ref[halo:halo + tm, :] = jnp.dot(xn, wm_ref[:, 0:2 * dm], preferred_element_type=F32)
    acc = cw_ref[CONV_WIDTH - 1:CONV_WIDTH, :] * ext_ref[halo:halo + tm, :]
    for j in range(1, CONV_WIDTH):
        acc = acc + cw_ref[CONV_WIDTH - 1 - j:CONV_WIDTH - j, :] * ext_ref[halo - j:halo - j + tm, :]
    ext_ref[0:halo, :] = ext_ref[tm:tm + halo, :]
    qk = acc * _sigmoid(acc)
    q_ref[...] = qk[:, 0:dm].astype(BF16)
    kt_ref[...] = (qk[:, dm:2 * dm] * k_scale).T.astype(BF16)

    v_ref[...] = jnp.dot(xn, wm_ref[:, 2 * dm:3 * dm], preferred_element_type=F32).astype(BF16)
    o_ref[...] = jnp.dot(xn, wm_ref[:, 3 * dm:4 * dm], preferred_element_type=F32).astype(BF16)
    u_ref[...] = jnp.dot(xn, wu_ref[...], preferred_element_type=F32).astype(BF16)

    gates = jnp.dot(xn, wg_ref[...], preferred_element_type=F32) + gb_ref[...]
    gates_ref[...] = gates
    gatest_ref[0:SUBLANES, :] = gates[:, 0:LANES].T[0:SUBLANES, :]
    gatest_ref[SUBLANES:2 * SUBLANES, :] = gates[:, LANES:2 * LANES].T[0:SUBLANES, :]


def _inproj(x2, g1, wm, wg, wu, gb, cw, *, seq, tm):
    T, D = x2.shape
    dm = wm.shape[1] // 4
    ds5 = wu.shape[1]
    H = N_MLSTM_HEADS
    kern = functools.partial(_inproj_kernel, tiles_per_seq=seq // tm, d_mlstm=dm,
                             k_scale=float((dm // H) ** -0.5))
    tok = lambda w: pl.BlockSpec((tm, w), lambda i: (i, 0))
    return pl.pallas_call(
        kern,
        grid=(T // tm,),
        in_specs=[tok(D), _const_spec((1, D)), _const_spec(wm.shape), _const_spec(wg.shape),
                  _const_spec(wu.shape), _const_spec((1, 2 * LANES)), _const_spec(cw.shape)],
        out_specs=[tok(dm), pl.BlockSpec((dm, tm), lambda i: (0, i)), tok(dm), tok(dm),
                   tok(2 * LANES), pl.BlockSpec((2 * SUBLANES, tm), lambda i: (0, i)), tok(ds5)],
        out_shape=[jax.ShapeDtypeStruct((T, dm), BF16), jax.ShapeDtypeStruct((dm, T), BF16),
                   jax.ShapeDtypeStruct((T, dm), BF16), jax.ShapeDtypeStruct((T, dm), BF16),
                   jax.ShapeDtypeStruct((T, 2 * LANES), F32),
                   jax.ShapeDtypeStruct((2 * SUBLANES, T), F32),
                   jax.ShapeDtypeStruct((T, ds5), BF16)],
        scratch_shapes=[pltpu.VMEM((tm + 2 * SUBLANES, 2 * dm), F32)],
        compiler_params=pltpu.CompilerParams(dimension_semantics=("arbitrary",),
                                             vmem_limit_bytes=VMEM_LIMIT),
        name="inproj",
    )(x2, g1, wm, wg, wu, gb, cw)


def _seg_cumsum(x, seg, axis):
    pos = lax.broadcasted_iota(jnp.int32, x.shape, axis) & (seg - 1)
    sh = 1
    while sh < seg:
        x = x + jnp.where(pos >= sh, pltpu.roll(x, sh, axis), 0.0)
        sh *= 2
    return x


def _mlstm_kernel(q_ref, kt_ref, v_ref, o_ref, gates_ref, gatest_ref, gn_ref, out_ref,
                  c_ref, bcol_ref, acol_ref, brow_ref, arow_ref, *, chunk):
    S = q_ref.shape[0]
    H = N_MLSTM_HEADS
    dh = q_ref.shape[1] // H
    L = chunk

    b = _seg_cumsum(_log_sigmoid(gates_ref[:, LANES:2 * LANES]), L, 0)
    bcol_ref[...] = b
    acol_ref[...] = gates_ref[:, 0:LANES] - b
    bt = _seg_cumsum(_log_sigmoid(gatest_ref[SUBLANES:2 * SUBLANES, :]), L, 1)
    brow_ref[...] = bt
    arow_ref[...] = gatest_ref[0:SUBLANES, :] - bt

    c_ref[...] = jnp.zeros(c_ref.shape, F32)
    causal = (lax.broadcasted_iota(jnp.int32, (L, L), 0) >= lax.broadcasted_iota(jnp.int32, (L, L), 1))
    ones_col = (lax.broadcasted_iota(jnp.int32, (L, dh), 1) == 0).astype(BF16)

    def body(c, m_carry):
        r0 = pl.multiple_of(c * L, L)
        m_out = []
        for h in range(H):
            hs = slice(h * dh, (h + 1) * dh)
            m_prev = m_carry[h]
            qc = q_ref[pl.ds(r0, L), hs]
            ktc = kt_ref[hs, pl.ds(r0, L)]
            v_aug = jnp.concatenate([v_ref[pl.ds(r0, L), hs], ones_col], axis=1)
            b_col = bcol_ref[pl.ds(r0, L), h:h + 1]
            b_row = brow_ref[h:h + 1, pl.ds(r0, L)]
            a_row = arow_ref[h:h + 1, pl.ds(r0, L)]
            b_last = b_row[:, L - 1:L]

            dmat = jnp.where(causal, b_col + a_row, -jnp.inf)
            log_inter = b_col + m_prev
            m_t = jnp.maximum(log_inter, jnp.max(dmat, axis=1, keepdims=True))
            inter_w = jnp.exp(log_inter - m_t)
            s_qk = jnp.dot(qc, ktc, preferred_element_type=F32)
            p = (jnp.exp(dmat - m_t) * s_qk).astype(BF16)
            c_prev = c_ref[h]
            nd = inter_w * jnp.dot(qc, c_prev.astype(BF16), preferred_element_type=F32)
            nd = nd + jnp.dot(p, v_aug, preferred_element_type=F32)
            den = jnp.maximum(jnp.abs(nd[:, dh:dh + 1]), jnp.exp(-m_t))
            h_tilde = nd[:, 0:dh] / den

            hm = _sigmoid(o_ref[pl.ds(r0, L), hs].astype(F32)) * h_tilde
            mu = jnp.mean(hm, axis=-1, keepdims=True)
            hc = hm - mu
            var = jnp.mean(hc * hc, axis=-1, keepdims=True)
            out_ref[pl.ds(r0, L), hs] = (hc * lax.rsqrt(var + EPS) * gn_ref[:, hs]).astype(BF16)

            w_row = b_last + a_row
            m_loc = jnp.max(w_row, axis=1, keepdims=True)
            e_row = jnp.exp(w_row - m_loc)
            kte = (ktc.astype(F32) * e_row).astype(BF16)
            c_loc = jnp.dot(kte, v_aug, preferred_element_type=F32)
            m_new = jnp.maximum(b_last + m_prev, m_loc)
            c_ref[h] = jnp.exp(b_last + m_prev - m_new) * c_prev + jnp.exp(m_loc - m_new) * c_loc
            m_out.append(m_new)
        return tuple(m_out)

    lax.fori_loop(0, S // L, body, tuple(jnp.zeros((1, 1), F32) for _ in range(H)))


def _mlstm(q, kt, v, o, gates, gatest, gn, *, batch, seq):
    T, dm = q.shape
    dh = dm // N_MLSTM_HEADS
    kern = functools.partial(_mlstm_kernel, chunk=MLSTM_CHUNK)
    tok = lambda w: pl.BlockSpec((seq, w), lambda b: (b, 0))
    return pl.pallas_call(
        kern,
        grid=(batch,),
        in_specs=[tok(dm), pl.BlockSpec((dm, seq), lambda b: (0, b)), tok(dm), tok(dm),
                  tok(2 * LANES), pl.BlockSpec((2 * SUBLANES, seq), lambda b: (0, b)),
                  _const_spec((1, dm))],
        out_specs=tok(dm),
        out_shape=jax.ShapeDtypeStruct((T, dm), BF16),
        scratch_shapes=[pltpu.VMEM((N_MLSTM_HEADS, dh, 2 * dh), F32),
                        pltpu.VMEM((seq, LANES), F32), pltpu.VMEM((seq, LANES), F32),
                        pltpu.VMEM((SUBLANES, seq), F32), pltpu.VMEM((SUBLANES, seq), F32)],
        compiler_params=pltpu.CompilerParams(dimension_semantics=("arbitrary",),
                                             vmem_limit_bytes=VMEM_LIMIT),
        name="mlstm",
    )(q, kt, v, o, gates, gatest, gn)


def _s5_prep_kernel(arow_ref, acol_ref, ldt_ref, bre_ref, bim_ref, cre_ref, cim_ref, dcol_ref,
                    rep_ref, pw_ref, w1_ref, wout_ref, al_ref, *, chunk):
    L = chunk
    P = S5_STATE
    Hc = S5_GROUP
    R = L * Hc
    hi = lax.Precision.HIGHEST
    dt = jnp.exp(ldt_ref[...])

    def abar_pow(a_re, a_im, tau):
        mag = jnp.exp(tau * (dt * a_re))
        ang = tau * (dt * a_im)
        return mag * jnp.cos(ang), mag * jnp.sin(ang)

    a_re_c = acol_ref[:, 0:1]
    a_im_c = acol_ref[:, 1:2]
    ab_re, ab_im = abar_pow(a_re_c, a_im_c, 1.0)
    den = a_re_c * a_re_c + a_im_c * a_im_c
    nr = ab_re - 1.0
    z_re = (nr * a_re_c + ab_im * a_im_c) / den
    z_im = (ab_im * a_re_c - nr * a_im_c) / den
    bb_re = z_re * bre_ref[...] - z_im * bim_ref[...]
    bb_im = z_re * bim_ref[...] + z_im * bre_ref[...]

    a_re_r = arow_ref[0:1, :]
    a_im_r = arow_ref[1:2, :]
    tau_rows = (lax.broadcasted_iota(jnp.int32, (R, P), 0) // Hc).astype(F32)
    c_re = jnp.concatenate([cre_ref[...]] * L, axis=0)
    c_im = jnp.concatenate([cim_ref[...]] * L, axis=0)

    def c_times_pow(tau):
        e_re, e_im = abar_pow(a_re_r, a_im_r, tau)
        return c_re * e_re - c_im * e_im, c_re * e_im + c_im * e_re

    ce_re, ce_im = c_times_pow(tau_rows)
    kflat = (jnp.dot(ce_re, bb_re, preferred_element_type=F32, precision=hi)
             - jnp.dot(ce_im, bb_im, preferred_element_type=F32, precision=hi))
    row = lax.broadcasted_iota(jnp.int32, (R, Hc), 0)
    col = lax.broadcasted_iota(jnp.int32, (R, Hc), 1)
    kflat = kflat + jnp.where(row == col, jnp.concatenate([dcol_ref[...]] * L, axis=0), 0.0)

    m = jnp.dot(kflat, rep_ref[...], preferred_element_type=F32, precision=hi)
    blk = lax.broadcasted_iota(jnp.int32, (R, R), 1) // Hc
    sh = 1
    while sh < L:
        shifted = jnp.concatenate([jnp.zeros((sh * Hc, R), F32), m[0:R - sh * Hc, :]], axis=0)
        m = jnp.where((blk & sh) != 0, shifted, m)
        sh *= 2
    w1_ref[0:R, :] = m.astype(BF16)

    tau_rev = (L - 1 - lax.broadcasted_iota(jnp.int32, (P, R), 1) // Hc).astype(F32)
    er_re, er_im = abar_pow(a_re_c, a_im_c, tau_rev)
    bt_re = jnp.dot(bb_re, rep_ref[...], preferred_element_type=F32, precision=hi)
    bt_im = jnp.dot(bb_im, rep_ref[...], preferred_element_type=F32, precision=hi)
    w1_ref[R:R + P, :] = (er_re * bt_re - er_im * bt_im).astype(BF16)
    w1_ref[R + P:R + 2 * P, :] = (er_re * bt_im + er_im * bt_re).astype(BF16)

    co_re, co_im = c_times_pow(tau_rows + 1.0)
    wout_ref[:, 0:P] = co_re.astype(BF16)
    wout_ref[:, P:2 * P] = (-co_im).astype(BF16)

    al_re, al_im = abar_pow(a_re_c, a_im_c, pw_ref[...])
    al_ref[0:P, :] = al_re
    al_ref[P:2 * P, :] = al_im


def _s5_prep(a_re, a_im, log_dt, b_re, b_im, c_re, c_im, d_skip, *, chunk, n_chunks):
    G, P = a_re.shape
    Hc = S5_GROUP
    R = chunk * Hc
    arow = jnp.stack([a_re, a_im], axis=1)
    acol = jnp.stack([a_re, a_im], axis=2)
    ldt = log_dt.reshape(G, 1, 1)
    dcol = d_skip.reshape(G, Hc, 1)
    rep = (jnp.arange(R)[None, :] % Hc == jnp.arange(Hc)[:, None]).astype(F32)
    n_steps = max(1, (n_chunks - 1).bit_length())
    pw = jnp.zeros((1, LANES), F32).at[0, :n_steps].set(chunk * 2.0 ** jnp.arange(n_steps))
    kern = functools.partial(_s5_prep_kernel, chunk=chunk)
    grp = lambda *s: pl.BlockSpec((None,) + s, lambda g: (g,) + (0,) * len(s))
    return pl.pallas_call(
        kern,
        grid=(G,),
        in_specs=[grp(2, P), grp(P, 2), grp(1, 1), grp(P, Hc), grp(P, Hc), grp(Hc, P), grp(Hc, P),
                  grp(Hc, 1), _const_spec((Hc, R)), _const_spec((1, LANES))],
        out_specs=[grp(R + 2 * P, R), grp(R, 2 * P), grp(2 * P, LANES)],
        out_shape=[jax.ShapeDtypeStruct((G, R + 2 * P, R), BF16),
                   jax.ShapeDtypeStruct((G, R, 2 * P), BF16),
                   jax.ShapeDtypeStruct((G, 2 * P, LANES), F32)],
        compiler_params=pltpu.CompilerParams(dimension_semantics=("arbitrary",),
                                             vmem_limit_bytes=VMEM_LIMIT),
        name="s5_prep",
    )(arow, acol, ldt, b_re, b_im, c_re, c_im, dcol, rep, pw)


def _s5_kernel(ut_ref, w1_ref, wout_ref, al_ref, yt_ref, *, chunk, n_chunks):
    L = chunk
    P = S5_STATE
    Hc = S5_GROUP
    R = L * Hc
    C = ut_ref.shape[-1]
    ut = ut_ref[...].reshape(R, C)
    res = jnp.dot(w1_ref[...], ut, preferred_element_type=F32)
    pos = lax.broadcasted_iota(jnp.int32, (P, n_chunks), 1)

    xprev = []
    for j in range(C // n_chunks):
        seg = slice(j * n_chunks, (j + 1) * n_chunks)
        x_re = res[R:R + P, seg]
        x_im = res[R + P:R + 2 * P, seg]
        sh, k = 1, 0
        while sh < n_chunks:
            a_re = al_ref[0:P, k:k + 1]
            a_im = al_ref[P:2 * P, k:k + 1]
            s_re = jnp.where(pos >= sh, pltpu.roll(x_re, sh, 1), 0.0)
            s_im = jnp.where(pos >= sh, pltpu.roll(x_im, sh, 1), 0.0)
            x_re, x_im = (x_re + a_re * s_re - a_im * s_im, x_im + a_re * s_im + a_im * s_re)
            sh *= 2
            k += 1
        p_re = jnp.where(pos >= 1, pltpu.roll(x_re, 1, 1), 0.0)
        p_im = jnp.where(pos >= 1, pltpu.roll(x_im, 1, 1), 0.0)
        xprev.append(jnp.concatenate([p_re, p_im], axis=0).astype(BF16))
    xprev = jnp.concatenate(xprev, axis=1)
    y = res[0:R, :] + jnp.dot(wout_ref[...], xprev, preferred_element_type=F32)
    yt_ref[...] = y.reshape(L, Hc, C)


def _s5(ut4, w1, wout, al, *, chunk, n_chunks):
    L, G, Hc, C = ut4.shape
    R = L * Hc
    P2 = wout.shape[-1]
    kern = functools.partial(_s5_kernel, chunk=chunk, n_chunks=n_chunks)
    return pl.pallas_call(
        kern,
        grid=(G,),
        in_specs=[pl.BlockSpec((L, None, Hc, C), lambda g: (0, g, 0, 0)),
                  pl.BlockSpec((None, R + P2, R), lambda g: (g, 0, 0)),
                  pl.BlockSpec((None, R, P2), lambda g: (g, 0, 0)),
                  pl.BlockSpec((None, P2, LANES), lambda g: (g, 0, 0))],
        out_specs=pl.BlockSpec((L, None, Hc, C), lambda g: (0, g, 0, 0)),
        out_shape=jax.ShapeDtypeStruct((L, G, Hc, C), F32),
        compiler_params=pltpu.CompilerParams(dimension_semantics=("arbitrary",),
                                             vmem_limit_bytes=VMEM_LIMIT),
        name="s5",
    )(ut4, w1, wout, al)


def _out_kernel(x_ref, hm_ref, y_ref, wglu_ref, bglu_ref, gs5_ref, wo_ref, g2_ref,
                wgate_ref, wup_ref, wdown_ref, gf_ref, out_ref, acc_ref, xn_ref, *, final_norm):
    dm = hm_ref.shape[1]
    g = jax.nn.gelu(y_ref[...])
    z = jnp.dot(g.astype(BF16), wglu_ref[...], preferred_element_type=F32) + bglu_ref[...]
    hs = _rmsnorm(g * _sigmoid(z), gs5_ref[...]).astype(BF16)
    h1 = (x_ref[...] + jnp.dot(hm_ref[...], wo_ref[0:dm, :], preferred_element_type=F32)
          + jnp.dot(hs, wo_ref[dm:, :], preferred_element_type=F32))
    xn_ref[...] = _rmsnorm(h1, g2_ref[...]).astype(BF16)
    acc_ref[...] = h1
    def ffn_chunk(c, carry):
        xn2 = xn_ref[...]
        gate = jnp.dot(xn2, wgate_ref[c], preferred_element_type=F32)
        up = jnp.dot(xn2, wup_ref[c], preferred_element_type=F32)
        act = (gate * _sigmoid(gate) * up).astype(BF16)
        acc_ref[...] += jnp.dot(act, wdown_ref[c], preferred_element_type=F32)
        return carry

    lax.fori_loop(0, wgate_ref.shape[0], ffn_chunk, 0)
    out_ref[...] = _rmsnorm(acc_ref[...], gf_ref[...]) if final_norm else acc_ref[...]


def _out_block(x2, hm, y, wglu, bglu, gs5, wo, g2, wgate, wup, wdown, gf, *, tm, ff_chunk, final_norm):
    T, D = x2.shape
    dm = hm.shape[1]
    ds5 = y.shape[1]
    d_ff = wgate.shape[1]
    n_ch = d_ff // ff_chunk
    assert n_ch * ff_chunk == d_ff
    wgate = wgate.reshape(D, n_ch, ff_chunk).transpose(1, 0, 2)
    wup = wup.reshape(D, n_ch, ff_chunk).transpose(1, 0, 2)
    wdown = wdown.reshape(n_ch, ff_chunk, D)
    kern = functools.partial(_out_kernel, final_norm=final_norm)
    tok = lambda w: pl.BlockSpec((tm, w), lambda i: (i, 0))
    return pl.pallas_call(
        kern,
        grid=(T // tm,),
        in_specs=[tok(D), tok(dm), tok(ds5), _const_spec(wglu.shape), _const_spec((1, ds5)),
                  _const_spec((1, ds5)), _const_spec(wo.shape), _const_spec((1, D)),
                  _const_spec(wgate.shape), _const_spec(wup.shape), _const_spec(wdown.shape),
                  _const_spec((1, D))],
        out_specs=tok(D),
        out_shape=jax.ShapeDtypeStruct((T, D), F32),
        scratch_shapes=[pltpu.VMEM((tm, D), F32), pltpu.VMEM((tm, D), BF16)],
        compiler_params=pltpu.CompilerParams(dimension_semantics=("arbitrary",),
                                             vmem_limit_bytes=56 * 1024 * 1024),
        name="out_block",
    )(x2, hm, y, wglu, bglu, gs5, wo, g2, wgate, wup, wdown, gf)


def kernel(x, norm1_g, w_in, if_bias, conv_qk, mlstm_norm_g, a_re, a_im, log_dt, b_re, b_im, c_re, c_im,
           d_skip, w_glu, b_glu, s5_norm_g, w_out, norm2_g, w_gate, w_up, w_down, norm_f_g):
    B, S, D = x.shape
    depth = w_in.shape[0]
    H = N_MLSTM_HEADS
    dm = mlstm_norm_g.shape[1]
    ds5 = d_skip.shape[1]
    G = ds5 // S5_GROUP
    T = B * S
    L5 = S5_CHUNK
    nc5 = S // L5
    row = lambda a: a.reshape(1, -1).astype(F32)

    h = x.reshape(T, D)
    for l in range(depth):
        last = l == depth - 1
        wm = w_in[l, :, 0:4 * dm].astype(BF16)
        wu = w_in[l, :, 4 * dm + 2 * H:].astype(BF16)
        wg = jnp.zeros((D, 2 * LANES), F32)
        wg = wg.at[:, 0:H].set(w_in[l, :, 4 * dm:4 * dm + H])
        wg = wg.at[:, LANES:LANES + H].set(w_in[l, :, 4 * dm + H:4 * dm + 2 * H]).astype(BF16)
        gb = jnp.zeros((1, 2 * LANES), F32)
        gb = gb.at[0, 0:H].set(if_bias[l, 0:H]).at[0, LANES:LANES + H].set(if_bias[l, H:2 * H])

        q, kt, v, o, gates, gatest, u = _inproj(h, row(norm1_g[l]), wm, wg, wu, gb,
                                                 conv_qk[l].astype(F32), seq=S, tm=512)
        hm = _mlstm(q, kt, v, o, gates, gatest, row(mlstm_norm_g[l]), batch=B, seq=S)

        w1, wout, al = _s5_prep(a_re[l], a_im[l], log_dt[l], b_re[l], b_im[l], c_re[l], c_im[l],
                                d_skip[l], chunk=L5, n_chunks=nc5)
        ut4 = u.reshape(B * nc5, L5, G, S5_GROUP).transpose(1, 2, 3, 0)
        yt4 = _s5(ut4, w1, wout, al, chunk=L5, n_chunks=nc5)
        y = yt4.transpose(3, 0, 1, 2).reshape(T, ds5)

        h = _out_block(h, hm, y, w_glu[l].astype(BF16), row(b_glu[l]), row(s5_norm_g[l]),
                       w_out[l].astype(BF16), row(norm2_g[l]), w_gate[l].astype(BF16),
                       w_up[l].astype(BF16), w_down[l].astype(BF16), row(norm_f_g), tm=512,
                       ff_chunk=256, final_norm=last)
    return h.reshape(B, S, D)
```

```python
import functools
import math

import jax
import jax.numpy as jnp
from jax import lax
from jax.experimental import pallas as pl
from jax.experimental.pallas import tpu as pltpu

EPS = 1e-6
N_MLSTM_HEADS = 4
CONV_WIDTH = 4
S5_GROUP = 16
S5_STATE = 64

LANES = 128
SUBLANES = 8
MLSTM_CHUNK = 128
S5_CHUNK = 16
VMEM_LIMIT = 48 * 1024 * 1024

F32 = jnp.float32
BF16 = jnp.bfloat16


def _rmsnorm(x, g):
    ms = jnp.mean(x * x, axis=-1, keepdims=True)
    return x * lax.rsqrt(ms + EPS) * g


def _sigmoid(x):
    return 1.0 / (1.0 + jnp.exp(-x))


def _log_sigmoid(x):
    return -(jnp.maximum(-x, 0.0) + jnp.log1p(jnp.exp(-jnp.abs(x))))


def _const_spec(shape):
    return pl.BlockSpec(shape, lambda *_: (0,) * len(shape), pipeline_mode=pl.Buffered(1))


def _inproj_kernel(x_ref, g1_ref, wm_ref, wg_ref, wu_ref, gb_ref, cw_ref,
                   q_ref, kt_ref, v_ref, o_ref, gates_ref, gatest_ref, u_ref,
                   ext_ref, *, tiles_per_seq, d_mlstm, k_scale):
    tm = x_ref.shape[0]
    dm = d_mlstm
    halo = SUBLANES
    xn = _rmsnorm(x_ref[...], g1_ref[...]).astype(BF16)

    @pl.when(pl.program_id(0) % tiles_per_seq == 0)
    def _():
        ext_ref[0:halo, :] = jnp.zeros((halo, 2 * dm), F32)

    ext_ref[halo:halo + tm, :] = jnp.dot(xn, wm_ref[:, 0:2 * dm], preferred_element_type=F32)
    acc = cw_ref[CONV_WIDTH - 1:CONV_WIDTH, :] * ext_ref[halo:halo + tm, :]
    for j in range(1, CONV_WIDTH):
        acc = acc + cw_ref[CONV_WIDTH - 1 - j:CONV_WIDTH - j, :] * ext_ref[halo - j:halo - j + tm, :]
    ext_ref[0:halo, :] = ext_ref[tm:tm + halo, :]
    qk = acc * _sigmoid(acc)
    q_ref[...] = qk[:, 0:dm].astype(BF16)
    kt_ref[...] = (qk[:, dm:2 * dm] * k_scale).T.astype(BF16)

    v_ref[...] = jnp.dot(xn, wm_ref[:, 2 * dm:3 * dm], preferred_element_type=F32).astype(BF16)
    o_ref[...] = jnp.dot(xn, wm_ref[:, 3 * dm:4 * dm], preferred_element_type=F32).astype(BF16)
    u_ref[...] = jnp.dot(xn, wu_ref[...], preferred_element_type=F32)

    gates = jnp.dot(xn, wg_ref[...], preferred_element_type=F32) + gb_ref[...]
    gates_ref[...] = gates
    gatest_ref[0:SUBLANES, :] = gates[:, 0:LANES].T[0:SUBLANES, :]
    gatest_ref[SUBLANES:2 * SUBLANES, :] = gates[:, LANES:2 * LANES].T[0:SUBLANES, :]


def _inproj(x2, g1, wm, wg, wu, gb, cw, *, seq, tm):
    T, D = x2.shape
    dm = wm.shape[1] // 4
    ds5 = wu.shape[1]
    H = N_MLSTM_HEADS
    kern = functools.partial(_inproj_kernel, tiles_per_seq=seq // tm, d_mlstm=dm,
                             k_scale=float((dm // H) ** -0.5))
    tok = lambda w: pl.BlockSpec((tm, w), lambda i: (i, 0))
    return pl.pallas_call(
        kern,
        grid=(T // tm,),
        in_specs=[tok(D), _const_spec((1, D)), _const_spec(wm.shape), _const_spec(wg.shape),
                  _const_spec(wu.shape), _const_spec((1, 2 * LANES)), _const_spec(cw.shape)],
        out_specs=[tok(dm), pl.BlockSpec((dm, tm), lambda i: (0, i)), tok(dm), tok(dm),
                   tok(2 * LANES), pl.BlockSpec((2 * SUBLANES, tm), lambda i: (0, i)), tok(ds5)],
        out_shape=[jax.ShapeDtypeStruct((T, dm), BF16), jax.ShapeDtypeStruct((dm, T), BF16),
                   jax.ShapeDtypeStruct((T, dm), BF16), jax.ShapeDtypeStruct((T, dm), BF16),
                   jax.ShapeDtypeStruct((T, 2 * LANES), F32),
                   jax.ShapeDtypeStruct((2 * SUBLANES, T), F32),
                   jax.ShapeDtypeStruct((T, ds5), F32)],
        scratch_shapes=[pltpu.VMEM((tm + 2 * SUBLANES, 2 * dm), F32)],
        compiler_params=pltpu.CompilerParams(dimension_semantics=("arbitrary",),
                                             vmem_limit_bytes=VMEM_LIMIT),
        name="inproj",
    )(x2, g1, wm, wg, wu, gb, cw)


def _seg_cumsum(x, seg, axis):
    pos = lax.broadcasted_iota(jnp.int32, x.shape, axis) & (seg - 1)
    sh = 1
    while sh < seg:
        x = x + jnp.where(pos >= sh, pltpu.roll(x, sh, axis), 0.0)
        sh *= 2
    return x


def _mlstm_kernel(q_ref, kt_ref, v_ref, o_ref, gates_ref, gatest_ref, gn_ref, out_ref,
                  c_ref, bcol_ref, acol_ref, brow_ref, arow_ref, *, chunk):
    S = q_ref.shape[0]
    H = N_MLSTM_HEADS
    dh = q_ref.shape[1] // H
    L = chunk

    b = _seg_cumsum(_log_sigmoid(gates_ref[:, LANES:2 * LANES]), L, 0)
    bcol_ref[...] = b
    acol_ref[...] = gates_ref[:, 0:LANES] - b
    bt = _seg_cumsum(_log_sigmoid(gatest_ref[SUBLANES:2 * SUBLANES, :]), L, 1)
    brow_ref[...] = bt
    arow_ref[...] = gatest_ref[0:SUBLANES, :] - bt

    c_ref[...] = jnp.zeros(c_ref.shape, F32)
    causal = (lax.broadcasted_iota(jnp.int32, (L, L), 0) >= lax.broadcasted_iota(jnp.int32, (L, L), 1))
    ones_col = (lax.broadcasted_iota(jnp.int32, (L, dh), 1) == 0).astype(BF16)

    def body(c, m_carry):
        r0 = pl.multiple_of(c * L, L)
        m_out = []
        for h in range(H):
            hs = slice(h * dh, (h + 1) * dh)
            m_prev = m_carry[h]
            qc = q_ref[pl.ds(r0, L), hs]
            ktc = kt_ref[hs, pl.ds(r0, L)]
            v_aug = jnp.concatenate([v_ref[pl.ds(r0, L), hs], ones_col], axis=1)
            b_col = bcol_ref[pl.ds(r0, L), h:h + 1]
            b_row = brow_ref[h:h + 1, pl.ds(r0, L)]
            a_row = arow_ref[h:h + 1, pl.ds(r0, L)]
            b_last = b_row[:, L - 1:L]

            dmat = jnp.where(causal, b_col + a_row, -jnp.inf)
            log_inter = b_col + m_prev
            m_t = jnp.maximum(log_inter, jnp.max(dmat, axis=1, keepdims=True))
            inter_w = jnp.exp(log_inter - m_t)
            s_qk = jnp.dot(qc, ktc, preferred_element_type=F32)
            p = (jnp.exp(dmat - m_t) * s_qk).astype(BF16)
            c_prev = c_ref[h]
            nd = inter_w * jnp.dot(qc, c_prev.astype(BF16), preferred_element_type=F32)
            nd = nd + jnp.dot(p, v_aug, preferred_element_type=F32)
            den = jnp.maximum(jnp.abs(nd[:, dh:dh + 1]), jnp.exp(-m_t))
            h_tilde = nd[:, 0:dh] / den

            hm = _sigmoid(o_ref[pl.ds(r0, L), hs].astype(F32)) * h_tilde
            mu = jnp.mean(hm, axis=-1, keepdims=True)
            hc = hm - mu
            var = jnp.mean(hc * hc, axis=-1, keepdims=True)
            out_ref[pl.ds(r0, L), hs] = (hc * lax.rsqrt(var + EPS) * gn_ref[:, hs]).astype(BF16)

            w_row = b_last + a_row
            m_loc = jnp.max(w_row, axis=1, keepdims=True)
            e_row = jnp.exp(w_row - m_loc)
            kte = (ktc.astype(F32) * e_row).astype(BF16)
            c_loc = jnp.dot(kte, v_aug, preferred_element_type=F32)
            m_new = jnp.maximum(b_last + m_prev, m_loc)
            c_ref[h] = jnp.exp(b_last + m_prev - m_new) * c_prev + jnp.exp(m_loc - m_new) * c_loc
            m_out.append(m_new)
        return tuple(m_out)

    lax.fori_loop(0, S // L, body, tuple(jnp.zeros((1, 1), F32) for _ in range(H)))


def _mlstm(q, kt, v, o, gates, gatest, gn, *, batch, seq):
    T, dm = q.shape
    dh = dm // N_MLSTM_HEADS
    kern = functools.partial(_mlstm_kernel, chunk=MLSTM_CHUNK)
    tok = lambda w: pl.BlockSpec((seq, w), lambda b: (b, 0))
    return pl.pallas_call(
        kern,
        grid=(batch,),
        in_specs=[tok(dm), pl.BlockSpec((dm, seq), lambda b: (0, b)), tok(dm), tok(dm),
                  tok(2 * LANES), pl.BlockSpec((2 * SUBLANES, seq), lambda b: (0, b)),
                  _const_spec((1, dm))],
        out_specs=tok(dm),
        out_shape=jax.ShapeDtypeStruct((T, dm), BF16),
        scratch_shapes=[pltpu.VMEM((N_MLSTM_HEADS, dh, 2 * dh), F32),
                        pltpu.VMEM((seq, LANES), F32), pltpu.VMEM((seq, LANES), F32),
                        pltpu.VMEM((SUBLANES, seq), F32), pltpu.VMEM((SUBLANES, seq), F32)],
        compiler_params=pltpu.CompilerParams(dimension_semantics=("arbitrary",),
                                             vmem_limit_bytes=VMEM_LIMIT),
        name="mlstm",
    )(q, kt, v, o, gates, gatest, gn)


def _s5_prep_kernel(arow_ref, acol_ref, ldt_ref, bre_ref, bim_ref, cre_ref, cim_ref, dcol_ref,
                    rep_ref, pw_ref, w1_ref, wout_ref, al_ref, *, chunk):
    L = chunk
    P = S5_STATE
    Hc = S5_GROUP
    R = L * Hc
    hi = lax.Precision.HIGHEST
    dt = jnp.exp(ldt_ref[...])

    def abar_pow(a_re, a_im, tau):
        mag = jnp.exp(tau * (dt * a_re))
        ang = tau * (dt * a_im)
        return mag * jnp.cos(ang), mag * jnp.sin(ang)

    a_re_c = acol_ref[:, 0:1]
    a_im_c = acol_ref[:, 1:2]
    ab_re, ab_im = abar_pow(a_re_c, a_im_c, 1.0)
    den = a_re_c * a_re_c + a_im_c * a_im_c
    nr = ab_re - 1.0
    z_re = (nr * a_re_c + ab_im * a_im_c) / den
    z_im = (ab_im * a_re_c - nr * a_im_c) / den
    bb_re = z_re * bre_ref[...] - z_im * bim_ref[...]
    bb_im = z_re * bim_ref[...] + z_im * bre_ref[...]

    a_re_r = arow_ref[0:1, :]
    a_im_r = arow_ref[1:2, :]
    tau_rows = (lax.broadcasted_iota(jnp.int32, (R, P), 0) // Hc).astype(F32)
    c_re = jnp.concatenate([cre_ref[...]] * L, axis=0)
    c_im = jnp.concatenate([cim_ref[...]] * L, axis=0)

    def c_times_pow(tau):
        e_re, e_im = abar_pow(a_re_r, a_im_r, tau)
        return c_re * e_re - c_im * e_im, c_re * e_im + c_im * e_re

    ce_re, ce_im = c_times_pow(tau_rows)
    kflat = (jnp.dot(ce_re, bb_re, preferred_element_type=F32, precision=hi)
             - jnp.dot(ce_im, bb_im, preferred_element_type=F32, precision=hi))
    row = lax.broadcasted_iota(jnp.int32, (R, Hc), 0)
    col = lax.broadcasted_iota(jnp.int32, (R, Hc), 1)
    kflat = kflat + jnp.where(row == col, jnp.concatenate([dcol_ref[...]] * L, axis=0), 0.0)

    m = jnp.dot(kflat, rep_ref[...], preferred_element_type=F32, precision=hi)
    blk = lax.broadcasted_iota(jnp.int32, (R, R), 1) // Hc
    sh = 1
    while sh < L:
        shifted = jnp.concatenate([jnp.zeros((sh * Hc, R), F32), m[0:R - sh * Hc, :]], axis=0)
        m = jnp.where((blk & sh) != 0, shifted, m)
        sh *= 2
    w1_ref[0:R, :] = m.astype(BF16)

    tau_rev = (L - 1 - lax.broadcasted_iota(jnp.int32, (P, R), 1) // Hc).astype(F32)
    er_re, er_im = abar_pow(a_re_c, a_im_c, tau_rev)
    bt_re = jnp.dot(bb_re, rep_ref[...], preferred_element_type=F32, precision=hi)
    bt_im = jnp.dot(bb_im, rep_ref[...], preferred_element_type=F32, precision=hi)
    w1_ref[R:R + P, :] = (er_re * bt_re - er_im * bt_im).astype(BF16)
    w1_ref[R + P:R + 2 * P, :] = (er_re * bt_im + er_im * bt_re).astype(BF16)

    co_re, co_im = c_times_pow(tau_rows + 1.0)
    wout_ref[:, 0:P] = co_re.astype(BF16)
    wout_ref[:, P:2 * P] = (-co_im).astype(BF16)

    al_re, al_im = abar_pow(a_re_c, a_im_c, pw_ref[...])
    al_ref[0:P, :] = al_re
    al_ref[P:2 * P, :] = al_im


def _s5_prep(a_re, a_im, log_dt, b_re, b_im, c_re, c_im, d_skip, *, chunk, n_chunks):
    G, P = a_re.shape
    Hc = S5_GROUP
    R = chunk * Hc
    arow = jnp.stack([a_re, a_im], axis=1)
    acol = jnp.stack([a_re, a_im], axis=2)
    ldt = log_dt.reshape(G, 1, 1)
    dcol = d_skip.reshape(G, Hc, 1)
    rep = (jnp.arange(R)[None, :] % Hc == jnp.arange(Hc)[:, None]).astype(F32)
    n_steps = max(1, (n_chunks - 1).bit_length())
    pw = jnp.zeros((1, LANES), F32).at[0, :n_steps].set(chunk * 2.0 ** jnp.arange(n_steps))
    kern = functools.partial(_s5_prep_kernel, chunk=chunk)
    grp = lambda *s: pl.BlockSpec((None,) + s, lambda g: (g,) + (0,) * len(s))
    return pl.pallas_call(
        kern,
        grid=(G,),
        in_specs=[grp(2, P), grp(P, 2), grp(1, 1), grp(P, Hc), grp(P, Hc), grp(Hc, P), grp(Hc, P),
                  grp(Hc, 1), _const_spec((Hc, R)), _const_spec((1, LANES))],
        out_specs=[grp(R + 2 * P, R), grp(R, 2 * P), grp(2 * P, LANES)],
        out_shape=[jax.ShapeDtypeStruct((G, R + 2 * P, R), BF16),
                   jax.ShapeDtypeStruct((G, R, 2 * P), BF16),
                   jax.ShapeDtypeStruct((G, 2 * P, LANES), F32)],
        compiler_params=pltpu.CompilerParams(dimension_semantics=("arbitrary",),
                                             vmem_limit_bytes=VMEM_LIMIT),
        name="s5_prep",
    )(arow, acol, ldt, b_re, b_im, c_re, c_im, dcol, rep, pw)


def _s5_kernel(u_ref, w1_ref, wout_ref, al_ref, y_ref, ut_ref, yt_ref, *, chunk, n_chunks):
    L = chunk
    P = S5_STATE
    Hc = S5_GROUP
    R = L * Hc
    C = u_ref.shape[0] // L
    n_grp = LANES // Hc

    for s in range(L):
        xt = u_ref[pl.ds(s, C, stride=L), :].T.astype(BF16)
        for j in range(n_grp):
            ut_ref[j, s * Hc:(s + 1) * Hc, :] = xt[j * Hc:(j + 1) * Hc, :]

    pos = lax.broadcasted_iota(jnp.int32, (P, n_chunks), 1)

    def group(j, carry):
        res = jnp.dot(w1_ref[j], ut_ref[j], preferred_element_type=F32)
        xprev = []
        for b in range(C // n_chunks):
            seg = slice(b * n_chunks, (b + 1) * n_chunks)
            x_re = res[R:R + P, seg]
            x_im = res[R + P:R + 2 * P, seg]
            sh, k = 1, 0
            while sh < n_chunks:
                a_re = al_ref[j, 0:P, k:k + 1]
                a_im = al_ref[j, P:2 * P, k:k + 1]
                s_re = jnp.where(pos >= sh, pltpu.roll(x_re, sh, 1), 0.0)
                s_im = jnp.where(pos >= sh, pltpu.roll(x_im, sh, 1), 0.0)
                x_re, x_im = (x_re + a_re * s_re - a_im * s_im, x_im + a_re * s_im + a_im * s_re)
                sh *= 2
                k += 1
            p_re = jnp.where(pos >= 1, pltpu.roll(x_re, 1, 1), 0.0)
            p_im = jnp.where(pos >= 1, pltpu.roll(x_im, 1, 1), 0.0)
            xprev.append(jnp.concatenate([p_re, p_im], axis=0).astype(BF16))
        xprev = jnp.concatenate(xprev, axis=1)
        yt_ref[j] = res[0:R, :] + jnp.dot(wout_ref[j], xprev, preferred_element_type=F32)
        return carry

    lax.fori_loop(0, n_grp, group, 0)

    for t in range(L):
        zt = jnp.concatenate([yt_ref[j, t * Hc:(t + 1) * Hc, :] for j in range(n_grp)], axis=0)
        y_ref[pl.ds(t, C, stride=L), :] = zt.T


def _s5(u, w1, wout, al, *, chunk, n_chunks, n_col_blocks):
    T, ds5 = u.shape
    G, R1, R = w1.shape
    P2 = wout.shape[-1]
    n_grp = LANES // S5_GROUP
    tb = T // n_col_blocks
    C = tb // chunk
    kern = functools.partial(_s5_kernel, chunk=chunk, n_chunks=n_chunks)
    grp = lambda *s: pl.BlockSpec((n_grp,) + s, lambda q, i: (q,) + (0,) * len(s))
    return pl.pallas_call(
        kern,
        grid=(G // n_grp, n_col_blocks),
        in_specs=[pl.BlockSpec((tb, LANES), lambda q, i: (i, q)),
                  grp(R1, R), grp(R, P2), grp(P2, LANES)],
        out_specs=pl.BlockSpec((tb, LANES), lambda q, i: (i, q)),
        out_shape=jax.ShapeDtypeStruct((T, ds5), F32),
        scratch_shapes=[pltpu.VMEM((n_grp, R, C), BF16), pltpu.VMEM((n_grp, R, C), F32)],
        compiler_params=pltpu.CompilerParams(dimension_semantics=("arbitrary", "arbitrary"),
                                             vmem_limit_bytes=VMEM_LIMIT),
        name="s5",
    )(u, w1, wout, al)


def _out_kernel(x_ref, hm_ref, y_ref, wglu_ref, bglu_ref, gs5_ref, wo_ref, g2_ref,
                wgate_ref, wup_ref, wdown_ref, gf_ref, out_ref, acc_ref, xn_ref, *, ff_chunk, final_norm):
    dm = hm_ref.shape[1]
    g = jax.nn.gelu(y_ref[...])
    z = jnp.dot(g.astype(BF16), wglu_ref[...], preferred_element_type=F32) + bglu_ref[...]
    hs = _rmsnorm(g * _sigmoid(z), gs5_ref[...]).astype(BF16)
    h1 = (x_ref[...] + jnp.dot(hm_ref[...], wo_ref[0:dm, :], preferred_element_type=F32)
          + jnp.dot(hs, wo_ref[dm:, :], preferred_element_type=F32))
    xn_ref[...] = _rmsnorm(h1, g2_ref[...]).astype(BF16)
    acc_ref[...] = h1
    def ffn_chunk(c, carry):
        cs = pl.ds(pl.multiple_of(c * ff_chunk, ff_chunk), ff_chunk)
        xn2 = xn_ref[...]
        gate = jnp.dot(xn2, wgate_ref[:, cs], preferred_element_type=F32)
        up = jnp.dot(xn2, wup_ref[:, cs], preferred_element_type=F32)
        act = (gate * _sigmoid(gate) * up).astype(BF16)
        acc_ref[...] += jnp.dot(act, wdown_ref[cs, :], preferred_element_type=F32)
        return carry

    lax.fori_loop(0, wgate_ref.shape[1] // ff_chunk, ffn_chunk, 0)
    out_ref[...] = _rmsnorm(acc_ref[...], gf_ref[...]) if final_norm else acc_ref[...]


def _out_block(x2, hm, y, wglu, bglu, gs5, wo, g2, wgate, wup, wdown, gf, *, tm, ff_chunk, final_norm):
    T, D = x2.shape
    dm = hm.shape[1]
    ds5 = y.shape[1]
    assert wgate.shape[1] % ff_chunk == 0
    kern = functools.partial(_out_kernel, ff_chunk=ff_chunk, final_norm=final_norm)
    tok = lambda w: pl.BlockSpec((tm, w), lambda i: (i, 0))
    return pl.pallas_call(
        kern,
        grid=(T // tm,),
        in_specs=[tok(D), tok(dm), tok(ds5), _const_spec(wglu.shape), _const_spec((1, ds5)),
                  _const_spec((1, ds5)), _const_spec(wo.shape), _const_spec((1, D)),
                  _const_spec(wgate.shape), _const_spec(wup.shape), _const_spec(wdown.shape),
                  _const_spec((1, D))],
        out_specs=tok(D),
        out_shape=jax.ShapeDtypeStruct((T, D), F32),
        scratch_shapes=[pltpu.VMEM((tm, D), F32), pltpu.VMEM((tm, D), BF16)],
        compiler_params=pltpu.CompilerParams(dimension_semantics=("arbitrary",),
                                             vmem_limit_bytes=56 * 1024 * 1024),
        name="out_block",
    )(x2, hm, y, wglu, bglu, gs5, wo, g2, wgate, wup, wdown, gf)


def kernel(x, norm1_g, w_in, if_bias, conv_qk, mlstm_norm_g, a_re, a_im, log_dt, b_re, b_im, c_re, c_im,
           d_skip, w_glu, b_glu, s5_norm_g, w_out, norm2_g, w_gate, w_up, w_down, norm_f_g):
    B, S, D = x.shape
    depth = w_in.shape[0]
    H = N_MLSTM_HEADS
    dm = mlstm_norm_g.shape[1]
    ds5 = d_skip.shape[1]
    G = ds5 // S5_GROUP
    T = B * S
    L5 = S5_CHUNK
    nc5 = S // L5
    row = lambda a: a.reshape(1, -1).astype(F32)

    h = x.reshape(T, D)
    for l in range(depth):
        last = l == depth - 1
        wm = w_in[l, :, 0:4 * dm].astype(BF16)
        wu = w_in[l, :, 4 * dm + 2 * H:].astype(BF16)
        wg = jnp.zeros((D, 2 * LANES), F32)
        wg = wg.at[:, 0:H].set(w_in[l, :, 4 * dm:4 * dm + H])
        wg = wg.at[:, LANES:LANES + H].set(w_in[l, :, 4 * dm + H:4 * dm + 2 * H]).astype(BF16)
        gb = jnp.zeros((1, 2 * LANES), F32)
        gb = gb.at[0, 0:H].set(if_bias[l, 0:H]).at[0, LANES:LANES + H].set(if_bias[l, H:2 * H])

        q, kt, v, o, gates, gatest, u = _inproj(h, row(norm1_g[l]), wm, wg, wu, gb,
                                                 conv_qk[l].astype(F32), seq=S, tm=512)
        hm = _mlstm(q, kt, v, o, gates, gatest, row(mlstm_norm_g[l]), batch=B, seq=S)

        w1, wout, al = _s5_prep(a_re[l], a_im[l], log_dt[l], b_re[l], b_im[l], c_re[l], c_im[l],
                                d_skip[l], chunk=L5, n_chunks=nc5)
        y = _s5(u, w1, wout, al, chunk=L5, n_chunks=nc5, n_col_blocks=2)

        h = _out_block(h, hm, y, w_glu[l].astype(BF16), row(b_glu[l]), row(s5_norm_g[l]),
                       w_out[l].astype(BF16), row(norm2_g[l]), w_gate[l].astype(BF16),
                       w_up[l].astype(BF16), w_down[l].astype(BF16), row(norm_f_g), tm=512,
                       ff_chunk=256, final_norm=last)
    return h.reshape(B, S, D)
```

```python
import functools
import math

import jax
import jax.numpy as jnp
from jax import lax
from jax.experimental import pallas as pl
from jax.experimental.pallas import tpu as pltpu

EPS = 1e-6
N_MLSTM_HEADS = 4
CONV_WIDTH = 4
S5_GROUP = 16
S5_STATE = 64

LANES = 128
SUBLANES = 8
MLSTM_CHUNK = 128
S5_CHUNK = 16
VMEM_LIMIT = 48 * 1024 * 1024

F32 = jnp.float32
BF16 = jnp.bfloat16


def _rmsnorm(x, g):
    ms = jnp.mean(x * x, axis=-1, keepdims=True)
    return x * lax.rsqrt(ms + EPS) * g


def _sigmoid(x):
    return 1.0 / (1.0 + jnp.exp(-x))


def _log_sigmoid(x):
    return -(jnp.maximum(-x, 0.0) + jnp.log1p(jnp.exp(-jnp.abs(x))))


def _const_spec(shape):
    return pl.BlockSpec(shape, lambda *_: (0,) * len(shape), pipeline_mode=pl.Buffered(1))


def _inproj_kernel(x_ref, g1_ref, wm_ref, wg_ref, wu_ref, gb_ref, cw_ref,
                   q_ref, kt_ref, v_ref, o_ref, gatest_ref, u_ref,
                   ext_ref, *, tiles_per_seq, d_mlstm, k_scale):
    tm = x_ref.shape[0]
    dm = d_mlstm
    halo = SUBLANES
    xn = _rmsnorm(x_ref[...], g1_ref[...]).astype(BF16)

    @pl.when(pl.program_id(0) % tiles_per_seq == 0)
    def _():
        ext_ref[0:halo, :] = jnp.zeros((halo, 2 * dm), F32)

    ext_ref[halo:halo + tm, :] = jnp.dot(xn, wm_ref[:, 0:2 * dm], preferred_element_type=F32)
    acc = cw_ref[CONV_WIDTH - 1:CONV_WIDTH, :] * ext_ref[halo:halo + tm, :]
    for j in range(1, CONV_WIDTH):
        acc = acc + cw_ref[CONV_WIDTH - 1 - j:CONV_WIDTH - j, :] * ext_ref[halo - j:halo - j + tm, :]
    ext_ref[0:halo, :] = ext_ref[tm:tm + halo, :]
    qk = acc * _sigmoid(acc)
    q_ref[...] = qk[:, 0:dm].astype(BF16)
    kt_ref[...] = (qk[:, dm:2 * dm] * k_scale).T.astype(BF16)

    v_ref[...] = jnp.dot(xn, wm_ref[:, 2 * dm:3 * dm], preferred_element_type=F32).astype(BF16)
    o_ref[...] = jnp.dot(xn, wm_ref[:, 3 * dm:4 * dm], preferred_element_type=F32).astype(BF16)
    u_ref[...] = jnp.dot(xn, wu_ref[...], preferred_element_type=F32)

    gates = jnp.dot(xn, wg_ref[...], preferred_element_type=F32) + gb_ref[...]
    gatest_ref[0:SUBLANES, :] = gates[:, 0:LANES].T[0:SUBLANES, :]
    gatest_ref[SUBLANES:2 * SUBLANES, :] = gates[:, LANES:2 * LANES].T[0:SUBLANES, :]


def _inproj(x2, g1, wm, wg, wu, gb, cw, *, seq, tm):
    T, D = x2.shape
    dm = wm.shape[1] // 4
    ds5 = wu.shape[1]
    H = N_MLSTM_HEADS
    kern = functools.partial(_inproj_kernel, tiles_per_seq=seq // tm, d_mlstm=dm,
                             k_scale=float((dm // H) ** -0.5))
    tok = lambda w: pl.BlockSpec((tm, w), lambda i: (i, 0))
    return pl.pallas_call(
        kern,
        grid=(T // tm,),
        in_specs=[tok(D), _const_spec((1, D)), _const_spec(wm.shape), _const_spec(wg.shape),
                  _const_spec(wu.shape), _const_spec((1, 2 * LANES)), _const_spec(cw.shape)],
        out_specs=[tok(dm), pl.BlockSpec((dm, tm), lambda i: (0, i)), tok(dm), tok(dm),
                   pl.BlockSpec((2 * SUBLANES, tm), lambda i: (0, i)), tok(ds5)],
        out_shape=[jax.ShapeDtypeStruct((T, dm), BF16), jax.ShapeDtypeStruct((dm, T), BF16),
                   jax.ShapeDtypeStruct((T, dm), BF16), jax.ShapeDtypeStruct((T, dm), BF16),
                   jax.ShapeDtypeStruct((2 * SUBLANES, T), F32),
                   jax.ShapeDtypeStruct((T, ds5), F32)],
        scratch_shapes=[pltpu.VMEM((tm + 2 * SUBLANES, 2 * dm), F32)],
        compiler_params=pltpu.CompilerParams(dimension_semantics=("arbitrary",),
                                             vmem_limit_bytes=VMEM_LIMIT),
        name="inproj",
    )(x2, g1, wm, wg, wu, gb, cw)


def _lane_scan(x, op, identity):
    n = x.shape[-1]
    pos = lax.broadcasted_iota(jnp.int32, x.shape, x.ndim - 1)
    sh = 1
    while sh < n:
        x = op(x, jnp.where(pos >= sh, pltpu.roll(x, sh, x.ndim - 1), identity))
        sh *= 2
    return x


def _mlstm_kernel(q_ref, kt_ref, v_ref, o_ref, gatest_ref, gn_ref, out_ref,
                  a_ref, g0_ref, nm_ref, gend_ref, *state_refs, chunk):
    S = q_ref.shape[0]
    H = N_MLSTM_HEADS
    dh = q_ref.shape[1] // H
    L = chunk
    assert L == dh == LANES
    c_refs, nd_refs = state_refs[0:H], state_refs[H:2 * H]

    f_cum = _lane_scan(_log_sigmoid(gatest_ref[SUBLANES:2 * SUBLANES, :]), jnp.add, 0.0)
    a = gatest_ref[0:SUBLANES, :] - f_cum
    g0 = jnp.maximum(_lane_scan(a, jnp.maximum, -jnp.inf), 0.0)
    a_ref[...] = a
    g0_ref[...] = g0
    nm_ref[...] = -(f_cum + g0)
    for c in range(S // L):
        gend_ref[:, c * L:(c + 1) * L] = jnp.broadcast_to(g0[:, (c + 1) * L - 1:(c + 1) * L], (SUBLANES, L))

    for c_ref in c_refs:
        c_ref[...] = jnp.zeros(c_ref.shape, F32)
    causal = (lax.broadcasted_iota(jnp.int32, (L, L), 0) >= lax.broadcasted_iota(jnp.int32, (L, L), 1))
    ones_blk = jnp.ones((L, dh), BF16)
    mean_mat = jnp.full((dh, dh), 1.0 / dh, BF16)

    def mix(c, h, g_prev):
        r0 = pl.multiple_of(c * L, L)
        hs = slice(h * dh, (h + 1) * dh)
        qc = q_ref[pl.ds(r0, L), hs]
        ktc = kt_ref[hs, pl.ds(r0, L)]
        v_aug = jnp.concatenate([v_ref[pl.ds(r0, L), hs], ones_blk], axis=1)
        a_row = a_ref[h:h + 1, pl.ds(r0, L)]
        g_end = gend_ref[h:h + 1, pl.ds(r0, L)]
        g0_t = jnp.broadcast_to(g0_ref[h:h + 1, pl.ds(r0, L)], (L, L)).T

        s_qk = jnp.dot(qc, ktc, preferred_element_type=F32)
        p = (jnp.exp(jnp.where(causal, a_row - g0_t, -jnp.inf)) * s_qk).astype(BF16)
        c_prev = c_refs[h][...]
        qc_prev = jnp.dot(qc, c_prev.astype(BF16), preferred_element_type=F32)
        pv = jnp.dot(p, v_aug, preferred_element_type=F32)
        inter_w = jnp.exp(g_prev - g0_t)
        nd_refs[h][...] = jnp.concatenate([inter_w, inter_w], axis=1) * qc_prev + pv

        kte = (ktc.astype(F32) * jnp.exp(a_row - g_end)).astype(BF16)
        decay = jnp.exp(g_prev - g_end)
        c_refs[h][...] = (jnp.concatenate([decay, decay], axis=1) * c_prev
                          + jnp.dot(kte, v_aug, preferred_element_type=F32))
        return g_end

    def emit(c, h):
        r0 = pl.multiple_of(c * L, L)
        hs = slice(h * dh, (h + 1) * dh)
        nm_t = jnp.broadcast_to(nm_ref[h:h + 1, pl.ds(r0, L)], (L, L)).T
        nd = nd_refs[h][...]
        h_tilde = nd[:, 0:dh] / jnp.maximum(jnp.abs(nd[:, dh:2 * dh]), jnp.exp(nm_t))
        hm = _sigmoid(o_ref[pl.ds(r0, L), hs].astype(F32)) * h_tilde
        sq = hm * hm
        parts = []
        for val in (hm, sq):
            hi = val.astype(BF16)
            parts += [hi, (val - hi.astype(F32)).astype(BF16)]
        sums = jnp.dot(jnp.concatenate(parts, axis=0), mean_mat, preferred_element_type=F32)
        mu = sums[0:L] + sums[L:2 * L]
        var = sums[2 * L:3 * L] + sums[3 * L:4 * L] - mu * mu
        out_ref[pl.ds(r0, L), hs] = ((hm - mu) * lax.rsqrt(var + EPS) * gn_ref[:, hs]).astype(BF16)

    zero = jnp.zeros((1, L), F32)
    g_first = tuple(mix(0, h, zero) for h in range(H))

    def body(c, g_carry):
        for h in range(H):
            emit(c - 1, h)
        return tuple(mix(c, h, g_carry[h]) for h in range(H))

    lax.fori_loop(1, S // L, body, g_first)
    for h in range(H):
        emit(S // L - 1, h)


def _mlstm(q, kt, v, o, gatest, gn, *, batch, seq):
    T, dm = q.shape
    dh = dm // N_MLSTM_HEADS
    kern = functools.partial(_mlstm_kernel, chunk=MLSTM_CHUNK)
    tok = lambda w: pl.BlockSpec((seq, w), lambda b: (b, 0))
    rows = pltpu.VMEM((SUBLANES, seq), F32)
    return pl.pallas_call(
        kern,
        grid=(batch,),
        in_specs=[tok(dm), pl.BlockSpec((dm, seq), lambda b: (0, b)), tok(dm), tok(dm),
                  pl.BlockSpec((2 * SUBLANES, seq), lambda b: (0, b)), _const_spec((1, dm))],
        out_specs=tok(dm),
        out_shape=jax.ShapeDtypeStruct((T, dm), BF16),
        scratch_shapes=([rows, rows, rows, rows] + [pltpu.VMEM((dh, 2 * dh), F32)] * N_MLSTM_HEADS
                        + [pltpu.VMEM((MLSTM_CHUNK, 2 * dh), F32)] * N_MLSTM_HEADS),
        compiler_params=pltpu.CompilerParams(dimension_semantics=("arbitrary",),
                                             vmem_limit_bytes=VMEM_LIMIT),
        name="mlstm",
    )(q, kt, v, o, gatest, gn)


def _s5_prep_kernel(arow_ref, acol_ref, ldt_ref, bre_ref, bim_ref, cre_ref, cim_ref, dcol_ref,
                    rep_ref, pw_ref, w1_ref, wout_ref, al_ref, *, chunk):
    L = chunk
    P = S5_STATE
    Hc = S5_GROUP
    R = L * Hc
    hi = lax.Precision.HIGHEST
    dt = jnp.exp(ldt_ref[...])

    def abar_pow(a_re, a_im, tau):
        mag = jnp.exp(tau * (dt * a_re))
        ang = tau * (dt * a_im)
        return mag * jnp.cos(ang), mag * jnp.sin(ang)

    a_re_c = acol_ref[:, 0:1]
    a_im_c = acol_ref[:, 1:2]
    ab_re, ab_im = abar_pow(a_re_c, a_im_c, 1.0)
    den = a_re_c * a_re_c + a_im_c * a_im_c
    nr = ab_re - 1.0
    z_re = (nr * a_re_c + ab_im * a_im_c) / den
    z_im = (ab_im * a_re_c - nr * a_im_c) / den
    bb_re = z_re * bre_ref[...] - z_im * bim_ref[...]
    bb_im = z_re * bim_ref[...] + z_im * bre_ref[...]

    a_re_r = arow_ref[0:1, :]
    a_im_r = arow_ref[1:2, :]
    tau_rows = (lax.broadcasted_iota(jnp.int32, (R, P), 0) // Hc).astype(F32)
    c_re = jnp.concatenate([cre_ref[...]] * L, axis=0)
    c_im = jnp.concatenate([cim_ref[...]] * L, axis=0)

    def c_times_pow(tau):
        e_re, e_im = abar_pow(a_re_r, a_im_r, tau)
        return c_re * e_re - c_im * e_im, c_re * e_im + c_im * e_re

    ce_re, ce_im = c_times_pow(tau_rows)
    kflat = (jnp.dot(ce_re, bb_re, preferred_element_type=F32, precision=hi)
             - jnp.dot(ce_im, bb_im, preferred_element_type=F32, precision=hi))
    row = lax.broadcasted_iota(jnp.int32, (R, Hc), 0)
    col = lax.broadcasted_iota(jnp.int32, (R, Hc), 1)
    kflat = kflat + jnp.where(row == col, jnp.concatenate([dcol_ref[...]] * L, axis=0), 0.0)

    m = jnp.dot(kflat, rep_ref[...], preferred_element_type=F32, precision=hi)
    blk = lax.broadcasted_iota(jnp.int32, (R, R), 1) // Hc
    sh = 1
    while sh < L:
        shifted = jnp.concatenate([jnp.zeros((sh * Hc, R), F32), m[0:R - sh * Hc, :]], axis=0)
        m = jnp.where((blk & sh) != 0, shifted, m)
        sh *= 2
    w1_ref[0:R, :] = m.astype(BF16)

    tau_rev = (L - 1 - lax.broadcasted_iota(jnp.int32, (P, R), 1) // Hc).astype(F32)
    er_re, er_im = abar_pow(a_re_c, a_im_c, tau_rev)
    bt_re = jnp.dot(bb_re, rep_ref[...], preferred_element_type=F32, precision=hi)
    bt_im = jnp.dot(bb_im, rep_ref[...], preferred_element_type=F32, precision=hi)
    w1_ref[R:R + P, :] = (er_re * bt_re - er_im * bt_im).astype(BF16)
    w1_ref[R + P:R + 2 * P, :] = (er_re * bt_im + er_im * bt_re).astype(BF16)

    co_re, co_im = c_times_pow(tau_rows + 1.0)
    wout_ref[:, 0:P] = co_re.astype(BF16)
    wout_ref[:, P:2 * P] = (-co_im).astype(BF16)

    al_re, al_im = abar_pow(a_re_c, a_im_c, pw_ref[...])
    al_ref[0:P, :] = al_re
    al_ref[P:2 * P, :] = al_im


def _s5_prep(a_re, a_im, log_dt, b_re, b_im, c_re, c_im, d_skip, *, chunk, n_chunks):
    G, P = a_re.shape
    Hc = S5_GROUP
    R = chunk * Hc
    arow = jnp.stack([a_re, a_im], axis=1)
    acol = jnp.stack([a_re, a_im], axis=2)
    ldt = log_dt.reshape(G, 1, 1)
    dcol = d_skip.reshape(G, Hc, 1)
    rep = (jnp.arange(R)[None, :] % Hc == jnp.arange(Hc)[:, None]).astype(F32)
    n_steps = max(1, (n_chunks - 1).bit_length())
    pw = jnp.zeros((1, LANES), F32).at[0, :n_steps].set(chunk * 2.0 ** jnp.arange(n_steps))
    kern = functools.partial(_s5_prep_kernel, chunk=chunk)
    grp = lambda *s: pl.BlockSpec((None,) + s, lambda g: (g,) + (0,) * len(s))
    return pl.pallas_call(
        kern,
        grid=(G,),
        in_specs=[grp(2, P), grp(P, 2), grp(1, 1), grp(P, Hc), grp(P, Hc), grp(Hc, P), grp(Hc, P),
                  grp(Hc, 1), _const_spec((Hc, R)), _const_spec((1, LANES))],
        out_specs=[grp(R + 2 * P, R), grp(R, 2 * P), grp(2 * P, LANES)],
        out_shape=[jax.ShapeDtypeStruct((G, R + 2 * P, R), BF16),
                   jax.ShapeDtypeStruct((G, R, 2 * P), BF16),
                   jax.ShapeDtypeStruct((G, 2 * P, LANES), F32)],
        compiler_params=pltpu.CompilerParams(dimension_semantics=("arbitrary",),
                                             vmem_limit_bytes=VMEM_LIMIT),
        name="s5_prep",
    )(arow, acol, ldt, b_re, b_im, c_re, c_im, dcol, rep, pw)


def _s5_kernel(u_ref, w1_ref, wout_ref, al_ref, y_ref, ut_ref, yt_ref, *, chunk, n_chunks):
    L = chunk
    P = S5_STATE
    Hc = S5_GROUP
    R = L * Hc
    C = u_ref.shape[0] // L
    n_grp = LANES // Hc

    for s in range(L):
        xt = u_ref[pl.ds(s, C, stride=L), :].T.astype(BF16)
        for j in range(n_grp):
            ut_ref[j, s * Hc:(s + 1) * Hc, :] = xt[j * Hc:(j + 1) * Hc, :]

    pos = lax.broadcasted_iota(jnp.int32, (P, n_chunks), 1)

    def group(j, carry):
        res = jnp.dot(w1_ref[j], ut_ref[j], preferred_element_type=F32)
        xprev = []
        for b in range(C // n_chunks):
            seg = slice(b * n_chunks, (b + 1) * n_chunks)
            x_re = res[R:R + P, seg]
            x_im = res[R + P:R + 2 * P, seg]
            sh, k = 1, 0
            while sh < n_chunks:
                a_re = al_ref[j, 0:P, k:k + 1]
                a_im = al_ref[j, P:2 * P, k:k + 1]
                s_re = jnp.where(pos >= sh, pltpu.roll(x_re, sh, 1), 0.0)
                s_im = jnp.where(pos >= sh, pltpu.roll(x_im, sh, 1), 0.0)
                x_re, x_im = (x_re + a_re * s_re - a_im * s_im, x_im + a_re * s_im + a_im * s_re)
                sh *= 2
                k += 1
            p_re = jnp.where(pos >= 1, pltpu.roll(x_re, 1, 1), 0.0)
            p_im = jnp.where(pos >= 1, pltpu.roll(x_im, 1, 1), 0.0)
            xprev.append(jnp.concatenate([p_re, p_im], axis=0).astype(BF16))
        xprev = jnp.concatenate(xprev, axis=1)
        yt_ref[j] = res[0:R, :] + jnp.dot(wout_ref[j], xprev, preferred_element_type=F32)
        return carry

    lax.fori_loop(0, n_grp, group, 0)

    for t in range(L):
        zt = jnp.concatenate([yt_ref[j, t * Hc:(t + 1) * Hc, :] for j in range(n_grp)], axis=0)
        y_ref[pl.ds(t, C, stride=L), :] = zt.T


def _s5(u, w1, wout, al, *, chunk, n_chunks, n_col_blocks):
    T, ds5 = u.shape
    G, R1, R = w1.shape
    P2 = wout.shape[-1]
    n_grp = LANES // S5_GROUP
    tb = T // n_col_blocks
    C = tb // chunk
    kern = functools.partial(_s5_kernel, chunk=chunk, n_chunks=n_chunks)
    grp = lambda *s: pl.BlockSpec((n_grp,) + s, lambda q, i: (q,) + (0,) * len(s))
    return pl.pallas_call(
        kern,
        grid=(G // n_grp, n_col_blocks),
        in_specs=[pl.BlockSpec((tb, LANES), lambda q, i: (i, q)),
                  grp(R1, R), grp(R, P2), grp(P2, LANES)],
        out_specs=pl.BlockSpec((tb, LANES), lambda q, i: (i, q)),
        out_shape=jax.ShapeDtypeStruct((T, ds5), F32),
        scratch_shapes=[pltpu.VMEM((n_grp, R, C), BF16), pltpu.VMEM((n_grp, R, C), F32)],
        compiler_params=pltpu.CompilerParams(dimension_semantics=("arbitrary", "arbitrary"),
                                             vmem_limit_bytes=VMEM_LIMIT),
        name="s5",
    )(u, w1, wout, al)


def _out_kernel(x_ref, hm_ref, y_ref, wglu_ref, bglu_ref, gs5_ref, wo_ref, g2_ref,
                wgate_ref, wup_ref, wdown_ref, gf_ref, out_ref, acc_ref, xn_ref, *, ff_chunk, final_norm):
    dm = hm_ref.shape[1]
    g = jax.nn.gelu(y_ref[...])
    z = jnp.dot(g.astype(BF16), wglu_ref[...], preferred_element_type=F32) + bglu_ref[...]
    hs = _rmsnorm(g * _sigmoid(z), gs5_ref[...]).astype(BF16)
    h1 = (x_ref[...] + jnp.dot(hm_ref[...], wo_ref[0:dm, :], preferred_element_type=F32)
          + jnp.dot(hs, wo_ref[dm:, :], preferred_element_type=F32))
    xn_ref[...] = _rmsnorm(h1, g2_ref[...]).astype(BF16)
    acc_ref[...] = h1
    def ffn_chunk(c, carry):
        cs = pl.ds(pl.multiple_of(c * ff_chunk, ff_chunk), ff_chunk)
        xn2 = xn_ref[...]
        gate = jnp.dot(xn2, wgate_ref[:, cs], preferred_element_type=F32)
        up = jnp.dot(xn2, wup_ref[:, cs], preferred_element_type=F32)
        act = (gate * _sigmoid(gate) * up).astype(BF16)
        acc_ref[...] += jnp.dot(act, wdown_ref[cs, :], preferred_element_type=F32)
        return carry

    lax.fori_loop(0, wgate_ref.shape[1] // ff_chunk, ffn_chunk, 0)
    out_ref[...] = _rmsnorm(acc_ref[...], gf_ref[...]) if final_norm else acc_ref[...]


def _out_block(x2, hm, y, wglu, bglu, gs5, wo, g2, wgate, wup, wdown, gf, *, tm, ff_chunk, final_norm):
    T, D = x2.shape
    dm = hm.shape[1]
    ds5 = y.shape[1]
    assert wgate.shape[1] % ff_chunk == 0
    kern = functools.partial(_out_kernel, ff_chunk=ff_chunk, final_norm=final_norm)
    tok = lambda w: pl.BlockSpec((tm, w), lambda i: (i, 0))
    return pl.pallas_call(
        kern,
        grid=(T // tm,),
        in_specs=[tok(D), tok(dm), tok(ds5), _const_spec(wglu.shape), _const_spec((1, ds5)),
                  _const_spec((1, ds5)), _const_spec(wo.shape), _const_spec((1, D)),
                  _const_spec(wgate.shape), _const_spec(wup.shape), _const_spec(wdown.shape),
                  _const_spec((1, D))],
        out_specs=tok(D),
        out_shape=jax.ShapeDtypeStruct((T, D), F32),
        scratch_shapes=[pltpu.VMEM((tm, D), F32), pltpu.VMEM((tm, D), BF16)],
        compiler_params=pltpu.CompilerParams(dimension_semantics=("arbitrary",),
                                             vmem_limit_bytes=56 * 1024 * 1024),
        name="out_block",
    )(x2, hm, y, wglu, bglu, gs5, wo, g2, wgate, wup, wdown, gf)


def kernel(x, norm1_g, w_in, if_bias, conv_qk, mlstm_norm_g, a_re, a_im, log_dt, b_re, b_im, c_re, c_im,
           d_skip, w_glu, b_glu, s5_norm_g, w_out, norm2_g, w_gate, w_up, w_down, norm_f_g):
    B, S, D = x.shape
    depth = w_in.shape[0]
    H = N_MLSTM_HEADS
    dm = mlstm_norm_g.shape[1]
    ds5 = d_skip.shape[1]
    G = ds5 // S5_GROUP
    T = B * S
    L5 = S5_CHUNK
    nc5 = S // L5
    row = lambda a: a.reshape(1, -1).astype(F32)

    h = x.reshape(T, D)
    for l in range(depth):
        last = l == depth - 1
        wm = w_in[l, :, 0:4 * dm].astype(BF16)
        wu = w_in[l, :, 4 * dm + 2 * H:].astype(BF16)
        wg = jnp.zeros((D, 2 * LANES), F32)
        wg = wg.at[:, 0:H].set(w_in[l, :, 4 * dm:4 * dm + H])
        wg = wg.at[:, LANES:LANES + H].set(w_in[l, :, 4 * dm + H:4 * dm + 2 * H]).astype(BF16)
        gb = jnp.zeros((1, 2 * LANES), F32)
        gb = gb.at[0, 0:H].set(if_bias[l, 0:H]).at[0, LANES:LANES + H].set(if_bias[l, H:2 * H])

        q, kt, v, o, gatest, u = _inproj(h, row(norm1_g[l]), wm, wg, wu, gb,
                                                 conv_qk[l].astype(F32), seq=S, tm=512)
        hm = _mlstm(q, kt, v, o, gatest, row(mlstm_norm_g[l]), batch=B, seq=S)

        w1, wout, al = _s5_prep(a_re[l], a_im[l], log_dt[l], b_re[l], b_im[l], c_re[l], c_im[l],
                                d_skip[l], chunk=L5, n_chunks=nc5)
        y = _s5(u, w1, wout, al, chunk=L5, n_chunks=nc5, n_col_blocks=2)

        h = _out_block(h, hm, y, w_glu[l].astype(BF16), row(b_glu[l]), row(s5_norm_g[l]),
                       w_out[l].astype(BF16), row(norm2_g[l]), w_gate[l].astype(BF16),
                       w_up[l].astype(BF16), w_down[l].astype(BF16), row(norm_f_g), tm=512,
                       ff_chunk=256, final_norm=last)
    return h.reshape(B, S, D)
```

```python
import functools

import jax
import jax.numpy as jnp
from jax import lax
from jax.experimental import pallas as pl
from jax.experimental.pallas import tpu as pltpu

EPS = 1e-6
N_MLSTM_HEADS = 4
CONV_WIDTH = 4
S5_GROUP = 16
S5_STATE = 64

LANES = 128
SUBLANES = 8
MLSTM_CHUNK = 128
S5_CHUNK = 16
VMEM_LIMIT = 48 * 1024 * 1024

F32 = jnp.float32
BF16 = jnp.bfloat16


def _rmsnorm(x, g):
    ms = jnp.mean(x * x, axis=-1, keepdims=True)
    return x * lax.rsqrt(ms + EPS) * g


def _sigmoid(x):
    return 1.0 / (1.0 + jnp.exp(-x))


def _log_sigmoid(x):
    return -(jnp.maximum(-x, 0.0) + jnp.log1p(jnp.exp(-jnp.abs(x))))


def _const_spec(shape):
    return pl.BlockSpec(shape, lambda *_: (0,) * len(shape), pipeline_mode=pl.Buffered(1))


def _inproj_kernel(x_ref, g1_ref, wm_ref, wg_ref, wu_ref, gb_ref, cw_ref,
                   q_ref, kt_ref, v_ref, o_ref, gatest_ref, u_ref,
                   ext_ref, *, tiles_per_seq, d_mlstm, k_scale):
    tm = x_ref.shape[0]
    dm = d_mlstm
    halo = SUBLANES
    xn = _rmsnorm(x_ref[...], g1_ref[...]).astype(BF16)

    @pl.when(pl.program_id(0) % tiles_per_seq == 0)
    def _():
        ext_ref[0:halo, :] = jnp.zeros((halo, 2 * dm), F32)

    ext_ref[halo:halo + tm, :] = jnp.dot(xn, wm_ref[:, 0:2 * dm], preferred_element_type=F32)
    acc = cw_ref[CONV_WIDTH - 1:CONV_WIDTH, :] * ext_ref[halo:halo + tm, :]
    for j in range(1, CONV_WIDTH):
        acc = acc + cw_ref[CONV_WIDTH - 1 - j:CONV_WIDTH - j, :] * ext_ref[halo - j:halo - j + tm, :]
    ext_ref[0:halo, :] = ext_ref[tm:tm + halo, :]
    qk = acc * _sigmoid(acc)
    q_ref[...] = qk[:, 0:dm].astype(BF16)
    kt_ref[...] = (qk[:, dm:2 * dm] * k_scale).T.astype(BF16)

    v_ref[...] = jnp.dot(xn, wm_ref[:, 2 * dm:3 * dm], preferred_element_type=F32).astype(BF16)
    o_ref[...] = jnp.dot(xn, wm_ref[:, 3 * dm:4 * dm], preferred_element_type=F32).astype(BF16)
    u_ref[...] = jnp.dot(xn, wu_ref[...], preferred_element_type=F32)

    gates = jnp.dot(xn, wg_ref[...], preferred_element_type=F32) + gb_ref[...]
    gatest_ref[0:SUBLANES, :] = gates[:, 0:LANES].T[0:SUBLANES, :]
    gatest_ref[SUBLANES:2 * SUBLANES, :] = gates[:, LANES:2 * LANES].T[0:SUBLANES, :]


def _inproj(x2, g1, wm, wg, wu, gb, cw, *, seq, tm):
    T, D = x2.shape
    dm = wm.shape[1] // 4
    ds5 = wu.shape[1]
    H = N_MLSTM_HEADS
    kern = functools.partial(_inproj_kernel, tiles_per_seq=seq // tm, d_mlstm=dm,
                             k_scale=float((dm // H) ** -0.5))
    tok = lambda w: pl.BlockSpec((tm, w), lambda i: (i, 0))
    return pl.pallas_call(
        kern,
        grid=(T // tm,),
        in_specs=[tok(D), _const_spec((1, D)), _const_spec(wm.shape), _const_spec(wg.shape),
                  _const_spec(wu.shape), _const_spec((1, 2 * LANES)), _const_spec(cw.shape)],
        out_specs=[tok(dm), pl.BlockSpec((dm, tm), lambda i: (0, i)), tok(dm), tok(dm),
                   pl.BlockSpec((2 * SUBLANES, tm), lambda i: (0, i)), tok(ds5)],
        out_shape=[jax.ShapeDtypeStruct((T, dm), BF16), jax.ShapeDtypeStruct((dm, T), BF16),
                   jax.ShapeDtypeStruct((T, dm), BF16), jax.ShapeDtypeStruct((T, dm), BF16),
                   jax.ShapeDtypeStruct((2 * SUBLANES, T), F32),
                   jax.ShapeDtypeStruct((T, ds5), F32)],
        scratch_shapes=[pltpu.VMEM((tm + 2 * SUBLANES, 2 * dm), F32)],
        compiler_params=pltpu.CompilerParams(dimension_semantics=("arbitrary",),
                                             vmem_limit_bytes=VMEM_LIMIT),
        name="inproj",
    )(x2, g1, wm, wg, wu, gb, cw)


def _lane_scan(x, op, identity):
    n = x.shape[-1]
    pos = lax.broadcasted_iota(jnp.int32, x.shape, x.ndim - 1)
    sh = 1
    while sh < n:
        x = op(x, jnp.where(pos >= sh, pltpu.roll(x, sh, x.ndim - 1), identity))
        sh *= 2
    return x


def _mlstm_kernel(q_ref, kt_ref, v_ref, o_ref, gatest_ref, gn_ref, out_ref,
                  a_ref, g0_ref, nm_ref, gend_ref, *state_refs, chunk):
    S = q_ref.shape[0]
    H = N_MLSTM_HEADS
    dh = q_ref.shape[1] // H
    L = chunk
    assert L == dh == LANES
    c_refs, nd_refs = state_refs[0:H], state_refs[H:2 * H]

    f_cum = _lane_scan(_log_sigmoid(gatest_ref[SUBLANES:2 * SUBLANES, :]), jnp.add, 0.0)
    a = gatest_ref[0:SUBLANES, :] - f_cum
    g0 = jnp.maximum(_lane_scan(a, jnp.maximum, -jnp.inf), 0.0)
    a_ref[...] = a
    g0_ref[...] = g0
    nm_ref[...] = -(f_cum + g0)
    for c in range(S // L):
        gend_ref[:, c * L:(c + 1) * L] = jnp.broadcast_to(g0[:, (c + 1) * L - 1:(c + 1) * L], (SUBLANES, L))

    for c_ref in c_refs:
        c_ref[...] = jnp.zeros(c_ref.shape, F32)
    causal = (lax.broadcasted_iota(jnp.int32, (L, L), 0) >= lax.broadcasted_iota(jnp.int32, (L, L), 1))
    ones_blk = jnp.ones((L, dh), BF16)
    mean_mat = jnp.full((dh, dh), 1.0 / dh, BF16)

    def mix(c, h, g_prev):
        r0 = pl.multiple_of(c * L, L)
        hs = slice(h * dh, (h + 1) * dh)
        qc = q_ref[pl.ds(r0, L), hs]
        ktc = kt_ref[hs, pl.ds(r0, L)]
        v_aug = jnp.concatenate([v_ref[pl.ds(r0, L), hs], ones_blk], axis=1)
        a_row = a_ref[h:h + 1, pl.ds(r0, L)]
        g_end = gend_ref[h:h + 1, pl.ds(r0, L)]
        g0_t = jnp.broadcast_to(g0_ref[h:h + 1, pl.ds(r0, L)], (L, L)).T

        s_qk = jnp.dot(qc, ktc, preferred_element_type=F32)
        p = (jnp.exp(jnp.where(causal, a_row - g0_t, -jnp.inf)) * s_qk).astype(BF16)
        c_prev = c_refs[h][...]
        qc_prev = jnp.dot(qc, c_prev.astype(BF16), preferred_element_type=F32)
        pv = jnp.dot(p, v_aug, preferred_element_type=F32)
        inter_w = jnp.exp(g_prev - g0_t)
        nd_refs[h][...] = jnp.concatenate([inter_w, inter_w], axis=1) * qc_prev + pv

        kte = (ktc.astype(F32) * jnp.exp(a_row - g_end)).astype(BF16)
        decay = jnp.exp(g_prev - g_end)
        c_refs[h][...] = (jnp.concatenate([decay, decay], axis=1) * c_prev
                          + jnp.dot(kte, v_aug, preferred_element_type=F32))
        return g_end

    def emit(c, h):
        r0 = pl.multiple_of(c * L, L)
        hs = slice(h * dh, (h + 1) * dh)
        nm_t = jnp.broadcast_to(nm_ref[h:h + 1, pl.ds(r0, L)], (L, L)).T
        nd = nd_refs[h][...]
        h_tilde = nd[:, 0:dh] / jnp.maximum(jnp.abs(nd[:, dh:2 * dh]), jnp.exp(nm_t))
        hm = _sigmoid(o_ref[pl.ds(r0, L), hs].astype(F32)) * h_tilde
        sq = hm * hm
        parts = []
        for val in (hm, sq):
            hi = val.astype(BF16)
            parts += [hi, (val - hi.astype(F32)).astype(BF16)]
        sums = jnp.dot(jnp.concatenate(parts, axis=0), mean_mat, preferred_element_type=F32)
        mu = sums[0:L] + sums[L:2 * L]
        var = sums[2 * L:3 * L] + sums[3 * L:4 * L] - mu * mu
        out_ref[pl.ds(r0, L), hs] = ((hm - mu) * lax.rsqrt(var + EPS) * gn_ref[:, hs]).astype(BF16)

    zero = jnp.zeros((1, L), F32)
    g_first = tuple(mix(0, h, zero) for h in range(H))

    def body(c, g_carry):
        for h in range(H):
            emit(c - 1, h)
        return tuple(mix(c, h, g_carry[h]) for h in range(H))

    lax.fori_loop(1, S // L, body, g_first)
    for h in range(H):
        emit(S // L - 1, h)


def _mlstm(q, kt, v, o, gatest, gn, *, batch, seq):
    T, dm = q.shape
    dh = dm // N_MLSTM_HEADS
    kern = functools.partial(_mlstm_kernel, chunk=MLSTM_CHUNK)
    tok = lambda w: pl.BlockSpec((seq, w), lambda b: (b, 0))
    rows = pltpu.VMEM((SUBLANES, seq), F32)
    return pl.pallas_call(
        kern,
        grid=(batch,),
        in_specs=[tok(dm), pl.BlockSpec((dm, seq), lambda b: (0, b)), tok(dm), tok(dm),
                  pl.BlockSpec((2 * SUBLANES, seq), lambda b: (0, b)), _const_spec((1, dm))],
        out_specs=tok(dm),
        out_shape=jax.ShapeDtypeStruct((T, dm), BF16),
        scratch_shapes=([rows, rows, rows, rows] + [pltpu.VMEM((dh, 2 * dh), F32)] * N_MLSTM_HEADS
                        + [pltpu.VMEM((MLSTM_CHUNK, 2 * dh), F32)] * N_MLSTM_HEADS),
        compiler_params=pltpu.CompilerParams(dimension_semantics=("arbitrary",),
                                             vmem_limit_bytes=VMEM_LIMIT),
        name="mlstm",
    )(q, kt, v, o, gatest, gn)


def _s5_prep_kernel(arow_ref, acol_ref, ldt_ref, bre_ref, bim_ref, cre_ref, cim_ref, dcol_ref,
                    rep_ref, selrev_ref, pw_ref, pwcol_ref, w1_ref, wout_ref, al_ref, *, chunk):
    L = chunk
    P = S5_STATE
    Hc = S5_GROUP
    R = L * Hc
    hi = lax.Precision.HIGHEST
    dt = jnp.exp(ldt_ref[...])

    def select(x, sel_ref):
        x_hi = x.astype(BF16)
        x_lo = (x - x_hi.astype(F32)).astype(BF16)
        sel = sel_ref[...]
        return (jnp.dot(x_hi, sel, preferred_element_type=F32) + jnp.dot(x_lo, sel, preferred_element_type=F32))

    def abar_pow(a_re, a_im, tau):
        mag = jnp.exp(tau * (dt * a_re))
        ang = tau * (dt * a_im)
        return mag * jnp.cos(ang), mag * jnp.sin(ang)

    a_re_c = acol_ref[:, 0:1]
    a_im_c = acol_ref[:, 1:2]
    ec_re, ec_im = abar_pow(a_re_c, a_im_c, pw_ref[...])

    ab_re = ec_re[:, 1:2]
    ab_im = ec_im[:, 1:2]
    den = a_re_c * a_re_c + a_im_c * a_im_c
    nr = ab_re - 1.0
    z_re = (nr * a_re_c + ab_im * a_im_c) / den
    z_im = (ab_im * a_re_c - nr * a_im_c) / den
    bb_re = z_re * bre_ref[...] - z_im * bim_ref[...]
    bb_im = z_re * bim_ref[...] + z_im * bre_ref[...]

    n_pow = -(-(L + 1) // SUBLANES) * SUBLANES
    tau_r = lax.broadcasted_iota(jnp.int32, (n_pow, P), 0).astype(F32)
    er_re, er_im = abar_pow(arow_ref[0:1, :], arow_ref[1:2, :], tau_r)

    def rows(e, first):
        return jnp.concatenate([jnp.broadcast_to(e[first + t:first + t + 1, :], (Hc, P)) for t in range(L)], axis=0)

    c_re = jnp.concatenate([cre_ref[...]] * L, axis=0)
    c_im = jnp.concatenate([cim_ref[...]] * L, axis=0)

    def c_times_pow(first):
        e_re, e_im = rows(er_re, first), rows(er_im, first)
        return c_re * e_re - c_im * e_im, c_re * e_im + c_im * e_re

    ce_re, ce_im = c_times_pow(0)
    kflat = (jnp.dot(ce_re, bb_re, preferred_element_type=F32, precision=hi)
             - jnp.dot(ce_im, bb_im, preferred_element_type=F32, precision=hi))
    row = lax.broadcasted_iota(jnp.int32, (R, Hc), 0)
    col = lax.broadcasted_iota(jnp.int32, (R, Hc), 1)
    kflat = kflat + jnp.where(row == col, jnp.concatenate([dcol_ref[...]] * L, axis=0), 0.0)

    m = select(kflat, rep_ref)
    blk = lax.broadcasted_iota(jnp.int32, (R, R), 1) // Hc
    sh = 1
    while sh < L:
        shifted = jnp.concatenate([jnp.zeros((sh * Hc, R), F32), m[0:R - sh * Hc, :]], axis=0)
        m = jnp.where((blk & sh) != 0, shifted, m)
        sh *= 2
    w1_ref[0:R, :] = m.astype(BF16)

    ev_re, ev_im = select(ec_re, selrev_ref), select(ec_im, selrev_ref)
    bt_re, bt_im = select(bb_re, rep_ref), select(bb_im, rep_ref)
    w1_ref[R:R + P, :] = (ev_re * bt_re - ev_im * bt_im).astype(BF16)
    w1_ref[R + P:R + 2 * P, :] = (ev_re * bt_im + ev_im * bt_re).astype(BF16)

    co_re, co_im = c_times_pow(1)
    zero = jnp.zeros((R, P), F32)
    odd = pl.program_id(0) % 2 == 1
    halves = lambda v: jnp.where(odd, jnp.concatenate([zero, v], axis=1), jnp.concatenate([v, zero], axis=1))
    wout_ref[:, 0:2 * P] = halves(co_re).astype(BF16)
    wout_ref[:, 2 * P:4 * P] = halves(-co_im).astype(BF16)

    sc_re, sc_im = abar_pow(arow_ref[0:1, :], arow_ref[1:2, :], pwcol_ref[...])
    al_ref[0:SUBLANES, :] = sc_re
    al_ref[SUBLANES:2 * SUBLANES, :] = sc_im


def _s5_prep(a_re, a_im, log_dt, b_re, b_im, c_re, c_im, d_skip, *, chunk, n_chunks):
    G, P = a_re.shape
    Hc = S5_GROUP
    R = chunk * Hc
    arow = jnp.stack([a_re, a_im], axis=1)
    acol = jnp.stack([a_re, a_im], axis=2)
    ldt = log_dt.reshape(G, 1, 1)
    dcol = d_skip.reshape(G, Hc, 1)
    rep = (jnp.arange(R)[None, :] % Hc == jnp.arange(Hc)[:, None]).astype(BF16)
    selrev = (jnp.arange(LANES)[:, None] == chunk - 1 - jnp.arange(R)[None, :] // Hc).astype(BF16)
    n_steps = max(1, (n_chunks - 1).bit_length())
    assert chunk <= LANES and n_steps <= SUBLANES
    pw = jnp.zeros((1, LANES), F32).at[0, :chunk].set(jnp.arange(chunk, dtype=F32))
    pwcol = jnp.zeros((SUBLANES, 1), F32).at[:n_steps, 0].set(chunk * 2.0 ** jnp.arange(n_steps))
    kern = functools.partial(_s5_prep_kernel, chunk=chunk)
    grp = lambda *s: pl.BlockSpec((None,) + s, lambda g: (g,) + (0,) * len(s))
    return pl.pallas_call(
        kern,
        grid=(G,),
        in_specs=[grp(2, P), grp(P, 2), grp(1, 1), grp(P, Hc), grp(P, Hc), grp(Hc, P), grp(Hc, P),
                  grp(Hc, 1), _const_spec((Hc, R)), _const_spec((LANES, R)), _const_spec((1, LANES)),
                  _const_spec((SUBLANES, 1))],
        out_specs=[grp(R + 2 * P, R), grp(R, 4 * P), grp(2 * SUBLANES, P)],
        out_shape=[jax.ShapeDtypeStruct((G, R + 2 * P, R), BF16),
                   jax.ShapeDtypeStruct((G, R, 4 * P), BF16),
                   jax.ShapeDtypeStruct((G, 2 * SUBLANES, P), F32)],
        compiler_params=pltpu.CompilerParams(dimension_semantics=("arbitrary",),
                                             vmem_limit_bytes=VMEM_LIMIT),
        name="s5_prep",
    )(arow, acol, ldt, b_re, b_im, c_re, c_im, dcol, rep, selrev, pw, pwcol)


def _s5_kernel(u_ref, w1_ref, wout_ref, al_ref, y_ref, ut_ref, yt_ref, *, chunk, n_chunks):
    L = chunk
    P = S5_STATE
    Hc = S5_GROUP
    R = L * Hc
    C = u_ref.shape[0] // L
    n_grp = LANES // Hc

    for s in range(L):
        xt = u_ref[pl.ds(s, C, stride=L), :].T.astype(BF16)
        for j in range(n_grp):
            ut_ref[j, s * Hc:(s + 1) * Hc, :] = xt[j * Hc:(j + 1) * Hc, :]

    seg_pos = lax.broadcasted_iota(jnp.int32, (C, 2 * P), 0) & (n_chunks - 1)
    nt = (((1,), (1,)), ((), ()))

    def shifted(x, sh):
        return jnp.where(seg_pos >= sh, pltpu.roll(x, sh, 0), 0.0)

    def group_pair(i, carry):
        j0 = 2 * i
        res = [jnp.dot(w1_ref[j0 + d], ut_ref[j0 + d], preferred_element_type=F32) for d in range(2)]
        x_re = jnp.concatenate([res[0][R:R + P], res[1][R:R + P]], axis=0).T
        x_im = jnp.concatenate([res[0][R + P:R + 2 * P], res[1][R + P:R + 2 * P]], axis=0).T
        sh, k = 1, 0
        while sh < n_chunks:
            a_re = jnp.concatenate([al_ref[j0, k:k + 1, :], al_ref[j0 + 1, k:k + 1, :]], axis=1)
            a_im = jnp.concatenate([al_ref[j0, SUBLANES + k:SUBLANES + k + 1, :],
                                    al_ref[j0 + 1, SUBLANES + k:SUBLANES + k + 1, :]], axis=1)
            s_re, s_im = shifted(x_re, sh), shifted(x_im, sh)
            x_re, x_im = (x_re + a_re * s_re - a_im * s_im, x_im + a_re * s_im + a_im * s_re)
            sh *= 2
            k += 1
        xprev = jnp.concatenate([shifted(x_re, 1), shifted(x_im, 1)], axis=1).astype(BF16)
        for d in range(2):
            yt_ref[j0 + d] = res[d][0:R, :] + lax.dot_general(wout_ref[j0 + d], xprev, nt,
                                                              preferred_element_type=F32)
        return carry

    lax.fori_loop(0, n_grp // 2, group_pair, 0)

    for t in range(L):
        zt = jnp.concatenate([yt_ref[j, t * Hc:(t + 1) * Hc, :] for j in range(n_grp)], axis=0)
        y_ref[pl.ds(t, C, stride=L), :] = zt.T


def _s5(u, w1, wout, al, *, chunk, n_chunks, n_col_blocks):
    T, ds5 = u.shape
    G, R1, R = w1.shape
    P4 = wout.shape[-1]
    n_grp = LANES // S5_GROUP
    tb = T // n_col_blocks
    C = tb // chunk
    kern = functools.partial(_s5_kernel, chunk=chunk, n_chunks=n_chunks)
    grp = lambda *s: pl.BlockSpec((n_grp,) + s, lambda q, i: (q,) + (0,) * len(s))
    return pl.pallas_call(
        kern,
        grid=(G // n_grp, n_col_blocks),
        in_specs=[pl.BlockSpec((tb, LANES), lambda q, i: (i, q)),
                  grp(R1, R), grp(R, P4), grp(2 * SUBLANES, P4 // 4)],
        out_specs=pl.BlockSpec((tb, LANES), lambda q, i: (i, q)),
        out_shape=jax.ShapeDtypeStruct((T, ds5), F32),
        scratch_shapes=[pltpu.VMEM((n_grp, R, C), BF16), pltpu.VMEM((n_grp, R, C), F32)],
        compiler_params=pltpu.CompilerParams(dimension_semantics=("arbitrary", "arbitrary"),
                                             vmem_limit_bytes=VMEM_LIMIT),
        name="s5",
    )(u, w1, wout, al)


def _out_kernel(x_ref, hm_ref, y_ref, wglu_ref, bglu_ref, gs5_ref, wo_ref, g2_ref,
                wgate_ref, wup_ref, wdown_ref, gf_ref, out_ref, acc_ref, xn_ref, *, ff_chunk, final_norm):
    dm = hm_ref.shape[1]
    g = jax.nn.gelu(y_ref[...])
    z = jnp.dot(g.astype(BF16), wglu_ref[...], preferred_element_type=F32) + bglu_ref[...]
    hs = _rmsnorm(g * _sigmoid(z), gs5_ref[...]).astype(BF16)
    h1 = (x_ref[...] + jnp.dot(hm_ref[...], wo_ref[0:dm, :], preferred_element_type=F32)
          + jnp.dot(hs, wo_ref[dm:, :], preferred_element_type=F32))
    xn_ref[...] = _rmsnorm(h1, g2_ref[...]).astype(BF16)
    acc_ref[...] = h1

    def ffn_chunk(c, carry):
        cs = pl.ds(pl.multiple_of(c * ff_chunk, ff_chunk), ff_chunk)
        xn2 = xn_ref[...]
        gate = jnp.dot(xn2, wgate_ref[:, cs], preferred_element_type=F32)
        up = jnp.dot(xn2, wup_ref[:, cs], preferred_element_type=F32)
        act = (gate * _sigmoid(gate) * up).astype(BF16)
        acc_ref[...] += jnp.dot(act, wdown_ref[cs, :], preferred_element_type=F32)
        return carry

    lax.fori_loop(0, wgate_ref.shape[1] // ff_chunk, ffn_chunk, 0)
    out_ref[...] = _rmsnorm(acc_ref[...], gf_ref[...]) if final_norm else acc_ref[...]


def _out_block(x2, hm, y, wglu, bglu, gs5, wo, g2, wgate, wup, wdown, gf, *, tm, ff_chunk, final_norm):
    T, D = x2.shape
    dm = hm.shape[1]
    ds5 = y.shape[1]
    assert wgate.shape[1] % ff_chunk == 0
    kern = functools.partial(_out_kernel, ff_chunk=ff_chunk, final_norm=final_norm)
    tok = lambda w: pl.BlockSpec((tm, w), lambda i: (i, 0))
    return pl.pallas_call(
        kern,
        grid=(T // tm,),
        in_specs=[tok(D), tok(dm), tok(ds5), _const_spec(wglu.shape), _const_spec((1, ds5)),
                  _const_spec((1, ds5)), _const_spec(wo.shape), _const_spec((1, D)),
                  _const_spec(wgate.shape), _const_spec(wup.shape), _const_spec(wdown.shape),
                  _const_spec((1, D))],
        out_specs=tok(D),
        out_shape=jax.ShapeDtypeStruct((T, D), F32),
        scratch_shapes=[pltpu.VMEM((tm, D), F32), pltpu.VMEM((tm, D), BF16)],
        compiler_params=pltpu.CompilerParams(dimension_semantics=("arbitrary",),
                                             vmem_limit_bytes=56 * 1024 * 1024),
        name="out_block",
    )(x2, hm, y, wglu, bglu, gs5, wo, g2, wgate, wup, wdown, gf)


def kernel(x, norm1_g, w_in, if_bias, conv_qk, mlstm_norm_g, a_re, a_im, log_dt, b_re, b_im, c_re, c_im,
           d_skip, w_glu, b_glu, s5_norm_g, w_out, norm2_g, w_gate, w_up, w_down, norm_f_g):
    B, S, D = x.shape
    depth = w_in.shape[0]
    H = N_MLSTM_HEADS
    dm = mlstm_norm_g.shape[1]
    T = B * S
    L5 = S5_CHUNK
    nc5 = S // L5
    row = lambda a: a.reshape(1, -1).astype(F32)

    h = x.reshape(T, D)
    for l in range(depth):
        last = l == depth - 1
        wm = w_in[l, :, 0:4 * dm].astype(BF16)
        wu = w_in[l, :, 4 * dm + 2 * H:].astype(BF16)
        wg = jnp.zeros((D, 2 * LANES), F32)
        wg = wg.at[:, 0:H].set(w_in[l, :, 4 * dm:4 * dm + H])
        wg = wg.at[:, LANES:LANES + H].set(w_in[l, :, 4 * dm + H:4 * dm + 2 * H]).astype(BF16)
        gb = jnp.zeros((1, 2 * LANES), F32)
        gb = gb.at[0, 0:H].set(if_bias[l, 0:H]).at[0, LANES:LANES + H].set(if_bias[l, H:2 * H])

        q, kt, v, o, gatest, u = _inproj(h, row(norm1_g[l]), wm, wg, wu, gb, conv_qk[l].astype(F32),
                                         seq=S, tm=512)
        hm = _mlstm(q, kt, v, o, gatest, row(mlstm_norm_g[l]), batch=B, seq=S)

        w1, wout, al = _s5_prep(a_re[l], a_im[l], log_dt[l], b_re[l], b_im[l], c_re[l], c_im[l],
                                d_skip[l], chunk=L5, n_chunks=nc5)
        y = _s5(u, w1, wout, al, chunk=L5, n_chunks=nc5, n_col_blocks=2)

        h = _out_block(h, hm, y, w_glu[l].astype(BF16), row(b_glu[l]), row(s5_norm_g[l]),
                       w_out[l].astype(BF16), row(norm2_g[l]), w_gate[l].astype(BF16),
                       w_up[l].astype(BF16), w_down[l].astype(BF16), row(norm_f_g), tm=512,
                       ff_chunk=256, final_norm=last)
    return h.reshape(B, S, D)
```

```python
import functools

import jax
import jax.numpy as jnp
from jax import lax
from jax.experimental import pallas as pl
from jax.experimental.pallas import tpu as pltpu

EPS = 1e-6
N_MLSTM_HEADS = 4
CONV_WIDTH = 4
S5_GROUP = 16
S5_STATE = 64

LANES = 128
SUBLANES = 8
MLSTM_CHUNK = 128
S5_CHUNK = 16
VMEM_LIMIT = 48 * 1024 * 1024

F32 = jnp.float32
BF16 = jnp.bfloat16


def _rmsnorm(x, g):
    ms = jnp.mean(x * x, axis=-1, keepdims=True)
    return x * lax.rsqrt(ms + EPS) * g


def _sigmoid(x):
    return 1.0 / (1.0 + jnp.exp(-x))


def _log_sigmoid(x):
    return -(jnp.maximum(-x, 0.0) + jnp.log1p(jnp.exp(-jnp.abs(x))))


def _const_spec(shape):
    return pl.BlockSpec(shape, lambda *_: (0,) * len(shape), pipeline_mode=pl.Buffered(1))


def _inproj_kernel(x_ref, g1_ref, w_ref, gb_ref, cw_ref,
                   q_ref, kt_ref, v_ref, o_ref, gatest_ref, u_ref,
                   ext_ref, wm_ref, wg_ref, wu_ref, *, tiles_per_seq, d_mlstm, k_scale):
    tm = x_ref.shape[0]
    dm = d_mlstm
    H = N_MLSTM_HEADS
    halo = SUBLANES

    @pl.when(pl.program_id(0) == 0)
    def _():
        for c0 in range(0, 4 * dm, dm):
            wm_ref[:, c0:c0 + dm] = w_ref[:, c0:c0 + dm].astype(BF16)
        wu_ref[...] = w_ref[:, 4 * dm + 2 * H:].astype(BF16)
        d_in = w_ref.shape[0]
        fill = jnp.zeros((d_in, LANES - H), F32)
        wg_ref[...] = jnp.concatenate([w_ref[:, 4 * dm:4 * dm + H], fill,
                                       w_ref[:, 4 * dm + H:4 * dm + 2 * H], fill], axis=1).astype(BF16)

    xn = _rmsnorm(x_ref[...], g1_ref[...]).astype(BF16)

    @pl.when(pl.program_id(0) % tiles_per_seq == 0)
    def _():
        ext_ref[0:halo, :] = jnp.zeros((halo, 2 * dm), F32)

    ext_ref[halo:halo + tm, :] = jnp.dot(xn, wm_ref[:, 0:2 * dm], preferred_element_type=F32)
    acc = cw_ref[CONV_WIDTH - 1:CONV_WIDTH, :] * ext_ref[halo:halo + tm, :]
    for j in range(1, CONV_WIDTH):
        acc = acc + cw_ref[CONV_WIDTH - 1 - j:CONV_WIDTH - j, :] * ext_ref[halo - j:halo - j + tm, :]
    ext_ref[0:halo, :] = ext_ref[tm:tm + halo, :]
    qk = acc * _sigmoid(acc)
    q_ref[...] = qk[:, 0:dm].astype(BF16)
    kt_ref[...] = (qk[:, dm:2 * dm] * k_scale).T.astype(BF16)

    v_ref[...] = jnp.dot(xn, wm_ref[:, 2 * dm:3 * dm], preferred_element_type=F32).astype(BF16)
    o_ref[...] = jnp.dot(xn, wm_ref[:, 3 * dm:4 * dm], preferred_element_type=F32).astype(BF16)
    u_ref[...] = jnp.dot(xn, wu_ref[...], preferred_element_type=F32)

    gates = jnp.dot(xn, wg_ref[...], preferred_element_type=F32) + gb_ref[...]
    gatest_ref[0:SUBLANES, :] = gates[:, 0:LANES].T[0:SUBLANES, :]
    gatest_ref[SUBLANES:2 * SUBLANES, :] = gates[:, LANES:2 * LANES].T[0:SUBLANES, :]


def _inproj(x2, g1, w_in, gb, cw, *, seq, tm, d_mlstm):
    T, D = x2.shape
    dm = d_mlstm
    H = N_MLSTM_HEADS
    ds5 = w_in.shape[1] - 4 * dm - 2 * H
    kern = functools.partial(_inproj_kernel, tiles_per_seq=seq // tm, d_mlstm=dm,
                             k_scale=float((dm // H) ** -0.5))
    tok = lambda w: pl.BlockSpec((tm, w), lambda i: (i, 0))
    return pl.pallas_call(
        kern,
        grid=(T // tm,),
        in_specs=[tok(D), _const_spec((1, D)), _const_spec(w_in.shape), _const_spec((1, 2 * LANES)),
                  _const_spec(cw.shape)],
        out_specs=[tok(dm), pl.BlockSpec((dm, tm), lambda i: (0, i)), tok(dm), tok(dm),
                   pl.BlockSpec((2 * SUBLANES, tm), lambda i: (0, i)), tok(ds5)],
        out_shape=[jax.ShapeDtypeStruct((T, dm), BF16), jax.ShapeDtypeStruct((dm, T), BF16),
                   jax.ShapeDtypeStruct((T, dm), BF16), jax.ShapeDtypeStruct((T, dm), BF16),
                   jax.ShapeDtypeStruct((2 * SUBLANES, T), F32),
                   jax.ShapeDtypeStruct((T, ds5), F32)],
        scratch_shapes=[pltpu.VMEM((tm + 2 * SUBLANES, 2 * dm), F32), pltpu.VMEM((D, 4 * dm), BF16),
                        pltpu.VMEM((D, 2 * LANES), BF16), pltpu.VMEM((D, ds5), BF16)],
        compiler_params=pltpu.CompilerParams(dimension_semantics=("arbitrary",),
                                             vmem_limit_bytes=VMEM_LIMIT),
        name="inproj",
    )(x2, g1, w_in, gb, cw)


def _lane_scan(x, op, identity):
    n = x.shape[-1]
    pos = lax.broadcasted_iota(jnp.int32, x.shape, x.ndim - 1)
    sh = 1
    while sh < n:
        x = op(x, jnp.where(pos >= sh, pltpu.roll(x, sh, x.ndim - 1), identity))
        sh *= 2
    return x


def _mlstm_kernel(q_ref, kt_ref, v_ref, o_ref, gatest_ref, gn_ref, out_ref,
                  a_ref, g0_ref, nm_ref, gend_ref, *state_refs, chunk):
    S = q_ref.shape[0]
    H = N_MLSTM_HEADS
    dh = q_ref.shape[1] // H
    L = chunk
    assert L == dh == LANES
    c_refs, nd_refs = state_refs[0:H], state_refs[H:2 * H]

    f_cum = _lane_scan(_log_sigmoid(gatest_ref[SUBLANES:2 * SUBLANES, :]), jnp.add, 0.0)
    a = gatest_ref[0:SUBLANES, :] - f_cum
    g0 = jnp.maximum(_lane_scan(a, jnp.maximum, -jnp.inf), 0.0)
    a_ref[...] = a
    g0_ref[...] = g0
    nm_ref[...] = -(f_cum + g0)
    for c in range(S // L):
        gend_ref[:, c * L:(c + 1) * L] = jnp.broadcast_to(g0[:, (c + 1) * L - 1:(c + 1) * L], (SUBLANES, L))

    for c_ref in c_refs:
        c_ref[...] = jnp.zeros(c_ref.shape, F32)
    causal = (lax.broadcasted_iota(jnp.int32, (L, L), 0) >= lax.broadcasted_iota(jnp.int32, (L, L), 1))
    ones_blk = jnp.ones((L, dh), BF16)
    mean_mat = jnp.full((dh, dh), 1.0 / dh, BF16)

    def mix(c, h, g_prev):
        r0 = pl.multiple_of(c * L, L)
        hs = slice(h * dh, (h + 1) * dh)
        qc = q_ref[pl.ds(r0, L), hs]
        ktc = kt_ref[hs, pl.ds(r0, L)]
        v_aug = jnp.concatenate([v_ref[pl.ds(r0, L), hs], ones_blk], axis=1)
        a_row = a_ref[h:h + 1, pl.ds(r0, L)]
        g_end = gend_ref[h:h + 1, pl.ds(r0, L)]
        g0_t = jnp.broadcast_to(g0_ref[h:h + 1, pl.ds(r0, L)], (L, L)).T

        s_qk = jnp.dot(qc, ktc, preferred_element_type=F32)
        p = (jnp.exp(jnp.where(causal, a_row - g0_t, -jnp.inf)) * s_qk).astype(BF16)
        c_prev = c_refs[h][...]
        qc_prev = jnp.dot(qc, c_prev.astype(BF16), preferred_element_type=F32)
        pv = jnp.dot(p, v_aug, preferred_element_type=F32)
        inter_w = jnp.exp(g_prev - g0_t)
        nd_refs[h][...] = jnp.concatenate([inter_w, inter_w], axis=1) * qc_prev + pv

        kte = (ktc.astype(F32) * jnp.exp(a_row - g_end)).astype(BF16)
        decay = jnp.exp(g_prev - g_end)
        c_refs[h][...] = (jnp.concatenate([decay, decay], axis=1) * c_prev
                          + jnp.dot(kte, v_aug, preferred_element_type=F32))
        return g_end

    def emit(c, h):
        r0 = pl.multiple_of(c * L, L)
        hs = slice(h * dh, (h + 1) * dh)
        nm_t = jnp.broadcast_to(nm_ref[h:h + 1, pl.ds(r0, L)], (L, L)).T
        nd = nd_refs[h][...]
        h_tilde = nd[:, 0:dh] / jnp.maximum(jnp.abs(nd[:, dh:2 * dh]), jnp.exp(nm_t))
        hm = _sigmoid(o_ref[pl.ds(r0, L), hs].astype(F32)) * h_tilde
        mu = jnp.dot(hm.astype(BF16), mean_mat, preferred_element_type=F32)
        hc = hm - mu
        var = jnp.dot((hc * hc).astype(BF16), mean_mat, preferred_element_type=F32)
        out_ref[pl.ds(r0, L), hs] = (hc * lax.rsqrt(var + EPS) * gn_ref[:, hs]).astype(BF16)

    zero = jnp.zeros((1, L), F32)
    g_first = tuple(mix(0, h, zero) for h in range(H))

    def body(c, g_carry):
        for h in range(H):
            emit(c - 1, h)
        return tuple(mix(c, h, g_carry[h]) for h in range(H))

    lax.fori_loop(1, S // L, body, g_first, unroll=2)
    for h in range(H):
        emit(S // L - 1, h)


def _mlstm(q, kt, v, o, gatest, gn, *, batch, seq):
    T, dm = q.shape
    dh = dm // N_MLSTM_HEADS
    kern = functools.partial(_mlstm_kernel, chunk=MLSTM_CHUNK)
    tok = lambda w: pl.BlockSpec((seq, w), lambda b: (b, 0))
    rows = pltpu.VMEM((SUBLANES, seq), F32)
    return pl.pallas_call(
        kern,
        grid=(batch,),
        in_specs=[tok(dm), pl.BlockSpec((dm, seq), lambda b: (0, b)), tok(dm), tok(dm),
                  pl.BlockSpec((2 * SUBLANES, seq), lambda b: (0, b)), _const_spec((1, dm))],
        out_specs=tok(dm),
        out_shape=jax.ShapeDtypeStruct((T, dm), BF16),
        scratch_shapes=([rows, rows, rows, rows] + [pltpu.VMEM((dh, 2 * dh), F32)] * N_MLSTM_HEADS
                        + [pltpu.VMEM((MLSTM_CHUNK, 2 * dh), F32)] * N_MLSTM_HEADS),
        compiler_params=pltpu.CompilerParams(dimension_semantics=("arbitrary",),
                                             vmem_limit_bytes=VMEM_LIMIT),
        name="mlstm",
    )(q, kt, v, o, gatest, gn)


def _s5_prep_kernel(arow_ref, acol_ref, ldt_ref, bre_ref, bim_ref, cre_ref, cim_ref, dcol_ref,
                    rep_ref, selrev_ref, pw_ref, pwcol_ref, w1_ref, wout_ref, al_ref, *, chunk):
    L = chunk
    P = S5_STATE
    Hc = S5_GROUP
    R = L * Hc
    hi = lax.Precision.HIGHEST
    dt = jnp.exp(ldt_ref[...])

    def select(x, sel_ref):
        x_hi = x.astype(BF16)
        x_lo = (x - x_hi.astype(F32)).astype(BF16)
        sel = sel_ref[...]
        return (jnp.dot(x_hi, sel, preferred_element_type=F32) + jnp.dot(x_lo, sel, preferred_element_type=F32))

    def abar_pow(a_re, a_im, tau):
        mag = jnp.exp(tau * (dt * a_re))
        ang = tau * (dt * a_im)
        return mag * jnp.cos(ang), mag * jnp.sin(ang)

    a_re_c = acol_ref[:, 0:1]
    a_im_c = acol_ref[:, 1:2]
    ec_re, ec_im = abar_pow(a_re_c, a_im_c, pw_ref[...])

    ab_re = ec_re[:, 1:2]
    ab_im = ec_im[:, 1:2]
    den = a_re_c * a_re_c + a_im_c * a_im_c
    nr = ab_re - 1.0
    z_re = (nr * a_re_c + ab_im * a_im_c) / den
    z_im = (ab_im * a_re_c - nr * a_im_c) / den
    bb_re = z_re * bre_ref[...] - z_im * bim_ref[...]
    bb_im = z_re * bim_ref[...] + z_im * bre_ref[...]

    n_pow = -(-(L + 1) // SUBLANES) * SUBLANES
    tau_r = lax.broadcasted_iota(jnp.int32, (n_pow, P), 0).astype(F32)
    er_re, er_im = abar_pow(arow_ref[0:1, :], arow_ref[1:2, :], tau_r)

    def rows(e, first):
        return jnp.concatenate([jnp.broadcast_to(e[first + t:first + t + 1, :], (Hc, P)) for t in range(L)], axis=0)

    c_re = jnp.concatenate([cre_ref[...]] * L, axis=0)
    c_im = jnp.concatenate([cim_ref[...]] * L, axis=0)

    def c_times_pow(first):
        e_re, e_im = rows(er_re, first), rows(er_im, first)
        return c_re * e_re - c_im * e_im, c_re * e_im + c_im * e_re

    ce_re, ce_im = c_times_pow(0)
    kflat = (jnp.dot(ce_re, bb_re, preferred_element_type=F32, precision=hi)
             - jnp.dot(ce_im, bb_im, preferred_element_type=F32, precision=hi))
    row = lax.broadcasted_iota(jnp.int32, (R, Hc), 0)
    col = lax.broadcasted_iota(jnp.int32, (R, Hc), 1)
    kflat = kflat + jnp.where(row == col, jnp.concatenate([dcol_ref[...]] * L, axis=0), 0.0)

    m = select(kflat, rep_ref)
    blk = lax.broadcasted_iota(jnp.int32, (R, R), 1) // Hc
    sh = 1
    while sh < L:
        shifted = jnp.concatenate([jnp.zeros((sh * Hc, R), F32), m[0:R - sh * Hc, :]], axis=0)
        m = jnp.where((blk & sh) != 0, shifted, m)
        sh *= 2
    w1_ref[0:R, :] = m.astype(BF16)

    ev_re, ev_im = select(ec_re, selrev_ref), select(ec_im, selrev_ref)
    bt_re, bt_im = select(bb_re, rep_ref), select(bb_im, rep_ref)
    w1_ref[R:R + P, :] = (ev_re * bt_re - ev_im * bt_im).astype(BF16)
    w1_ref[R + P:R + 2 * P, :] = (ev_re * bt_im + ev_im * bt_re).astype(BF16)

    co_re, co_im = c_times_pow(1)
    zero = jnp.zeros((R, P), F32)
    odd = pl.program_id(0) % 2 == 1
    halves = lambda v: jnp.where(odd, jnp.concatenate([zero, v], axis=1), jnp.concatenate([v, zero], axis=1))
    wout_ref[:, 0:2 * P] = halves(co_re).astype(BF16)
    wout_ref[:, 2 * P:4 * P] = halves(-co_im).astype(BF16)

    sc_re, sc_im = abar_pow(arow_ref[0:1, :], arow_ref[1:2, :], pwcol_ref[...])
    al_ref[0:SUBLANES, :] = sc_re
    al_ref[SUBLANES:2 * SUBLANES, :] = sc_im


def _s5_prep(a_re, a_im, log_dt, b_re, b_im, c_re, c_im, d_skip, *, chunk, n_chunks):
    G, P = a_re.shape
    Hc = S5_GROUP
    R = chunk * Hc
    arow = jnp.stack([a_re, a_im], axis=1)
    acol = jnp.stack([a_re, a_im], axis=2)
    ldt = log_dt.reshape(G, 1, 1)
    dcol = d_skip.reshape(G, Hc, 1)
    rep = (jnp.arange(R)[None, :] % Hc == jnp.arange(Hc)[:, None]).astype(BF16)
    selrev = (jnp.arange(LANES)[:, None] == chunk - 1 - jnp.arange(R)[None, :] // Hc).astype(BF16)
    n_steps = max(1, (n_chunks - 1).bit_length())
    assert chunk <= LANES and n_steps <= SUBLANES
    pw = jnp.zeros((1, LANES), F32).at[0, :chunk].set(jnp.arange(chunk, dtype=F32))
    pwcol = jnp.zeros((SUBLANES, 1), F32).at[:n_steps, 0].set(chunk * 2.0 ** jnp.arange(n_steps))
    kern = functools.partial(_s5_prep_kernel, chunk=chunk)
    grp = lambda *s: pl.BlockSpec((None,) + s, lambda g: (g,) + (0,) * len(s))
    return pl.pallas_call(
        kern,
        grid=(G,),
        in_specs=[grp(2, P), grp(P, 2), grp(1, 1), grp(P, Hc), grp(P, Hc), grp(Hc, P), grp(Hc, P),
                  grp(Hc, 1), _const_spec((Hc, R)), _const_spec((LANES, R)), _const_spec((1, LANES)),
                  _const_spec((SUBLANES, 1))],
        out_specs=[grp(R + 2 * P, R), grp(R, 4 * P), grp(2 * SUBLANES, P)],
        out_shape=[jax.ShapeDtypeStruct((G, R + 2 * P, R), BF16),
                   jax.ShapeDtypeStruct((G, R, 4 * P), BF16),
                   jax.ShapeDtypeStruct((G, 2 * SUBLANES, P), F32)],
        compiler_params=pltpu.CompilerParams(dimension_semantics=("arbitrary",),
                                             vmem_limit_bytes=VMEM_LIMIT),
        name="s5_prep",
    )(arow, acol, ldt, b_re, b_im, c_re, c_im, dcol, rep, selrev, pw, pwcol)


def _s5_kernel(u_ref, w1_ref, wout_ref, al_ref, y_ref, ut_ref, yt_ref, *, chunk, n_chunks):
    L = chunk
    P = S5_STATE
    Hc = S5_GROUP
    R = L * Hc
    C = u_ref.shape[0] // L
    n_grp = LANES // Hc

    for s in range(L):
        xt = u_ref[pl.ds(s, C, stride=L), :].T.astype(BF16)
        for j in range(n_grp):
            ut_ref[j, s * Hc:(s + 1) * Hc, :] = xt[j * Hc:(j + 1) * Hc, :]

    seg_pos = lax.broadcasted_iota(jnp.int32, (C, 2 * P), 0) & (n_chunks - 1)
    nt = (((1,), (1,)), ((), ()))

    def shifted(x, sh):
        return jnp.where(seg_pos >= sh, pltpu.roll(x, sh, 0), 0.0)

    def group_pair(i, carry):
        j0 = 2 * i
        res = [jnp.dot(w1_ref[j0 + d], ut_ref[j0 + d], preferred_element_type=F32) for d in range(2)]
        x_re = jnp.concatenate([res[0][R:R + P], res[1][R:R + P]], axis=0).T
        x_im = jnp.concatenate([res[0][R + P:R + 2 * P], res[1][R + P:R + 2 * P]], axis=0).T
        sh, k = 1, 0
        while sh < n_chunks:
            a_re = jnp.concatenate([al_ref[j0, k:k + 1, :], al_ref[j0 + 1, k:k + 1, :]], axis=1)
            a_im = jnp.concatenate([al_ref[j0, SUBLANES + k:SUBLANES + k + 1, :],
                                    al_ref[j0 + 1, SUBLANES + k:SUBLANES + k + 1, :]], axis=1)
            s_re, s_im = shifted(x_re, sh), shifted(x_im, sh)
            x_re, x_im = (x_re + a_re * s_re - a_im * s_im, x_im + a_re * s_im + a_im * s_re)
            sh *= 2
            k += 1
        xprev = jnp.concatenate([shifted(x_re, 1), shifted(x_im, 1)], axis=1).astype(BF16)
        for d in range(2):
            yt_ref[j0 + d] = res[d][0:R, :] + lax.dot_general(wout_ref[j0 + d], xprev, nt,
                                                              preferred_element_type=F32)
        return carry

    lax.fori_loop(0, n_grp // 2, group_pair, 0)

    for t in range(L):
        zt = jnp.concatenate([yt_ref[j, t * Hc:(t + 1) * Hc, :] for j in range(n_grp)], axis=0)
        y_ref[pl.ds(t, C, stride=L), :] = zt.T


def _s5(u, w1, wout, al, *, chunk, n_chunks, n_col_blocks):
    T, ds5 = u.shape
    G, R1, R = w1.shape
    P4 = wout.shape[-1]
    n_grp = LANES // S5_GROUP
    tb = T // n_col_blocks
    C = tb // chunk
    kern = functools.partial(_s5_kernel, chunk=chunk, n_chunks=n_chunks)
    grp = lambda *s: pl.BlockSpec((n_grp,) + s, lambda q, i: (q,) + (0,) * len(s))
    return pl.pallas_call(
        kern,
        grid=(G // n_grp, n_col_blocks),
        in_specs=[pl.BlockSpec((tb, LANES), lambda q, i: (i, q)),
                  grp(R1, R), grp(R, P4), grp(2 * SUBLANES, P4 // 4)],
        out_specs=pl.BlockSpec((tb, LANES), lambda q, i: (i, q)),
        out_shape=jax.ShapeDtypeStruct((T, ds5), F32),
        scratch_shapes=[pltpu.VMEM((n_grp, R, C), BF16), pltpu.VMEM((n_grp, R, C), F32)],
        compiler_params=pltpu.CompilerParams(dimension_semantics=("arbitrary", "arbitrary"),
                                             vmem_limit_bytes=VMEM_LIMIT),
        name="s5",
    )(u, w1, wout, al)


def _out_kernel(x_ref, hm_ref, y_ref, wglu_ref, bglu_ref, gs5_ref, wo_ref, g2_ref,
                wgate_ref, wup_ref, wdown_ref, gf_ref, out_ref, acc_ref, xn_ref, *, ff_chunk, final_norm):
    dm = hm_ref.shape[1]
    g = jax.nn.gelu(y_ref[...])
    z = jnp.dot(g.astype(BF16), wglu_ref[...], preferred_element_type=F32) + bglu_ref[...]
    hs = _rmsnorm(g * _sigmoid(z), gs5_ref[...]).astype(BF16)
    h1 = (x_ref[...] + jnp.dot(hm_ref[...], wo_ref[0:dm, :], preferred_element_type=F32)
          + jnp.dot(hs, wo_ref[dm:, :], preferred_element_type=F32))
    xn_ref[...] = _rmsnorm(h1, g2_ref[...]).astype(BF16)
    acc_ref[...] = h1

    def ffn_chunk(c, carry):
        cs = pl.ds(pl.multiple_of(c * ff_chunk, ff_chunk), ff_chunk)
        xn2 = xn_ref[...]
        gate = jnp.dot(xn2, wgate_ref[:, cs], preferred_element_type=F32)
        up = jnp.dot(xn2, wup_ref[:, cs], preferred_element_type=F32)
        act = (gate * _sigmoid(gate) * up).astype(BF16)
        acc_ref[...] += jnp.dot(act, wdown_ref[cs, :], preferred_element_type=F32)
        return carry

    lax.fori_loop(0, wgate_ref.shape[1] // ff_chunk, ffn_chunk, 0, unroll=2)
    out_ref[...] = _rmsnorm(acc_ref[...], gf_ref[...]) if final_norm else acc_ref[...]


def _out_block(x2, hm, y, wglu, bglu, gs5, wo, g2, wgate, wup, wdown, gf, *, tm, ff_chunk, final_norm):
    T, D = x2.shape
    dm = hm.shape[1]
    ds5 = y.shape[1]
    assert wgate.shape[1] % ff_chunk == 0
    kern = functools.partial(_out_kernel, ff_chunk=ff_chunk, final_norm=final_norm)
    tok = lambda w: pl.BlockSpec((tm, w), lambda i: (i, 0))
    return pl.pallas_call(
        kern,
        grid=(T // tm,),
        in_specs=[tok(D), tok(dm), tok(ds5), _const_spec(wglu.shape), _const_spec((1, ds5)),
                  _const_spec((1, ds5)), _const_spec(wo.shape), _const_spec((1, D)),
                  _const_spec(wgate.shape), _const_spec(wup.shape), _const_spec(wdown.shape),
                  _const_spec((1, D))],
        out_specs=tok(D),
        out_shape=jax.ShapeDtypeStruct((T, D), F32),
        scratch_shapes=[pltpu.VMEM((tm, D), F32), pltpu.VMEM((tm, D), BF16)],
        compiler_params=pltpu.CompilerParams(dimension_semantics=("arbitrary",),
                                             vmem_limit_bytes=56 * 1024 * 1024),
        name="out_block",
    )(x2, hm, y, wglu, bglu, gs5, wo, g2, wgate, wup, wdown, gf)


def kernel(x, norm1_g, w_in, if_bias, conv_qk, mlstm_norm_g, a_re, a_im, log_dt, b_re, b_im, c_re, c_im,
           d_skip, w_glu, b_glu, s5_norm_g, w_out, norm2_g, w_gate, w_up, w_down, norm_f_g):
    B, S, D = x.shape
    depth = w_in.shape[0]
    H = N_MLSTM_HEADS
    dm = mlstm_norm_g.shape[1]
    T = B * S
    L5 = S5_CHUNK
    nc5 = S // L5
    row = lambda a: a.reshape(1, -1).astype(F32)

    h = x.reshape(T, D)
    for l in range(depth):
        last = l == depth - 1
        gb = jnp.zeros((1, 2 * LANES), F32)
        gb = gb.at[0, 0:H].set(if_bias[l, 0:H]).at[0, LANES:LANES + H].set(if_bias[l, H:2 * H])

        q, kt, v, o, gatest, u = _inproj(h, row(norm1_g[l]), w_in[l].astype(F32), gb, conv_qk[l].astype(F32),
                                         seq=S, tm=512, d_mlstm=dm)
        hm = _mlstm(q, kt, v, o, gatest, row(mlstm_norm_g[l]), batch=B, seq=S)

        w1, wout, al = _s5_prep(a_re[l], a_im[l], log_dt[l], b_re[l], b_im[l], c_re[l], c_im[l],
                                d_skip[l], chunk=L5, n_chunks=nc5)
        y = _s5(u, w1, wout, al, chunk=L5, n_chunks=nc5, n_col_blocks=2)

        h = _out_block(h, hm, y, w_glu[l].astype(BF16), row(b_glu[l]), row(s5_norm_g[l]),
                       w_out[l].astype(BF16), row(norm2_g[l]), w_gate[l].astype(BF16),
                       w_up[l].astype(BF16), w_down[l].astype(BF16), row(norm_f_g), tm=512,
                       ff_chunk=256, final_norm=last)
    return h.reshape(B, S, D)
```

```python
import functools

import jax
import jax.numpy as jnp
from jax import lax
from jax.experimental import pallas as pl
from jax.experimental.pallas import tpu as pltpu

EPS = 1e-6
N_MLSTM_HEADS = 4
CONV_WIDTH = 4
S5_GROUP = 16
S5_STATE = 64

LANES = 128
SUBLANES = 8
MLSTM_CHUNK = 128
S5_CHUNK = 16
VMEM_LIMIT = 48 * 1024 * 1024

F32 = jnp.float32
BF16 = jnp.bfloat16


def _rmsnorm(x, g):
    ms = jnp.mean(x * x, axis=-1, keepdims=True)
    return x * lax.rsqrt(ms + EPS) * g


def _sigmoid(x):
    return 1.0 / (1.0 + jnp.exp(-x))


def _log_sigmoid(x):
    return -(jnp.maximum(-x, 0.0) + jnp.log1p(jnp.exp(-jnp.abs(x))))


def _const_spec(shape):
    return pl.BlockSpec(shape, lambda *_: (0,) * len(shape), pipeline_mode=pl.Buffered(1))


def _inproj_kernel(x_ref, g1_ref, wt_ref, gb_ref, cw_ref,
                   q_ref, kt_ref, v_ref, o_ref, gatest_ref, u_ref,
                   ext_ref, wm_ref, wg_ref, wu_ref, *, tiles_per_seq, d_mlstm, k_scale):
    tm = x_ref.shape[0]
    dm = d_mlstm
    H = N_MLSTM_HEADS
    halo = SUBLANES

    @pl.when(pl.program_id(0) == 0)
    def _():
        for c0 in range(0, 4 * dm, dm):
            wm_ref[:, c0:c0 + dm] = wt_ref[c0:c0 + dm, :].T.astype(BF16)
        wu_ref[...] = wt_ref[4 * dm + 2 * H:, :].T.astype(BF16)
        fill = jnp.zeros((LANES - 2 * H, wt_ref.shape[1]), F32)
        wg_ref[...] = jnp.concatenate([wt_ref[4 * dm:4 * dm + 2 * H, :], fill], axis=0).T.astype(BF16)

    xn = _rmsnorm(x_ref[...], g1_ref[...]).astype(BF16)

    @pl.when(pl.program_id(0) % tiles_per_seq == 0)
    def _():
        ext_ref[0:halo, :] = jnp.zeros((halo, 2 * dm), F32)

    ext_ref[halo:halo + tm, :] = jnp.dot(xn, wm_ref[:, 0:2 * dm], preferred_element_type=F32)
    acc = cw_ref[CONV_WIDTH - 1:CONV_WIDTH, :] * ext_ref[halo:halo + tm, :]
    for j in range(1, CONV_WIDTH):
        acc = acc + cw_ref[CONV_WIDTH - 1 - j:CONV_WIDTH - j, :] * ext_ref[halo - j:halo - j + tm, :]
    ext_ref[0:halo, :] = ext_ref[tm:tm + halo, :]
    qk = acc * _sigmoid(acc)
    q_ref[...] = qk[:, 0:dm].astype(BF16)
    kt_ref[...] = (qk[:, dm:2 * dm] * k_scale).T.astype(BF16)

    v_ref[...] = jnp.dot(xn, wm_ref[:, 2 * dm:3 * dm], preferred_element_type=F32).astype(BF16)
    o_ref[...] = jnp.dot(xn, wm_ref[:, 3 * dm:4 * dm], preferred_element_type=F32).astype(BF16)
    u_ref[...] = jnp.dot(xn, wu_ref[...], preferred_element_type=F32)

    gates = jnp.dot(xn, wg_ref[...], preferred_element_type=F32) + gb_ref[...]
    gatest_ref[...] = gates.T[0:2 * H, :]


def _inproj(x2, g1, w_in_t, gb, cw, *, seq, tm, d_mlstm):
    T, D = x2.shape
    dm = d_mlstm
    H = N_MLSTM_HEADS
    assert 2 * H == SUBLANES
    ds5 = w_in_t.shape[0] - 4 * dm - 2 * H
    kern = functools.partial(_inproj_kernel, tiles_per_seq=seq // tm, d_mlstm=dm,
                             k_scale=float((dm // H) ** -0.5))
    tok = lambda w: pl.BlockSpec((tm, w), lambda i: (i, 0))
    return pl.pallas_call(
        kern,
        grid=(T // tm,),
        in_specs=[tok(D), _const_spec((1, D)), _const_spec(w_in_t.shape), _const_spec((1, LANES)),
                  _const_spec(cw.shape)],
        out_specs=[tok(dm), pl.BlockSpec((dm, tm), lambda i: (0, i)), tok(dm), tok(dm),
                   pl.BlockSpec((2 * H, tm), lambda i: (0, i)), tok(ds5)],
        out_shape=[jax.ShapeDtypeStruct((T, dm), BF16), jax.ShapeDtypeStruct((dm, T), BF16),
                   jax.ShapeDtypeStruct((T, dm), BF16), jax.ShapeDtypeStruct((T, dm), BF16),
                   jax.ShapeDtypeStruct((2 * H, T), F32),
                   jax.ShapeDtypeStruct((T, ds5), F32)],
        scratch_shapes=[pltpu.VMEM((tm + 2 * SUBLANES, 2 * dm), F32), pltpu.VMEM((D, 4 * dm), BF16),
                        pltpu.VMEM((D, LANES), BF16), pltpu.VMEM((D, ds5), BF16)],
        compiler_params=pltpu.CompilerParams(dimension_semantics=("arbitrary",),
                                             vmem_limit_bytes=VMEM_LIMIT),
        name="inproj",
    )(x2, g1, w_in_t, gb, cw)


def _lane_scan(x, op, identity):
    n = x.shape[-1]
    pos = lax.broadcasted_iota(jnp.int32, x.shape, x.ndim - 1)
    sh = 1
    while sh < n:
        x = op(x, jnp.where(pos >= sh, pltpu.roll(x, sh, x.ndim - 1), identity))
        sh *= 2
    return x


def _mlstm_kernel(q_ref, kt_ref, v_ref, o_ref, gatest_ref, gn_ref, *refs, chunk, n_cast):
    S = q_ref.shape[0]
    H = N_MLSTM_HEADS
    dh = q_ref.shape[1] // H
    L = chunk
    assert L == dh == LANES
    w32_refs, out_ref, w16_refs = refs[0:n_cast], refs[n_cast], refs[n_cast + 1:2 * n_cast + 1]
    a_ref, g0_ref, nm_ref, gend_ref = refs[2 * n_cast + 1:2 * n_cast + 5]
    c_refs, nd_refs = refs[2 * n_cast + 5:2 * n_cast + 5 + H], refs[2 * n_cast + 5 + H:]

    for w32_ref, w16_ref in zip(w32_refs, w16_refs):
        w16_ref[...] = w32_ref[...].astype(BF16)

    gates = gatest_ref[...]
    f_cum = pltpu.roll(_lane_scan(_log_sigmoid(gates), jnp.add, 0.0), H, 0)
    a = gates - f_cum
    g0 = jnp.maximum(_lane_scan(a, jnp.maximum, -jnp.inf), 0.0)
    a_ref[...] = a
    g0_ref[...] = g0
    nm_ref[...] = -(f_cum + g0)
    for c in range(S // L):
        gend_ref[:, c * L:(c + 1) * L] = jnp.broadcast_to(g0[:, (c + 1) * L - 1:(c + 1) * L], (SUBLANES, L))

    for c_ref in c_refs:
        c_ref[...] = jnp.zeros(c_ref.shape, F32)
    causal = (lax.broadcasted_iota(jnp.int32, (L, L), 0) >= lax.broadcasted_iota(jnp.int32, (L, L), 1))
    ones_blk = jnp.ones((L, dh), BF16)
    mean_mat = jnp.full((dh, dh), 1.0 / dh, BF16)

    def mix(c, h, g_prev):
        r0 = pl.multiple_of(c * L, L)
        hs = slice(h * dh, (h + 1) * dh)
        qc = q_ref[pl.ds(r0, L), hs]
        ktc = kt_ref[hs, pl.ds(r0, L)]
        v_aug = jnp.concatenate([v_ref[pl.ds(r0, L), hs], ones_blk], axis=1)
        a_row = a_ref[h:h + 1, pl.ds(r0, L)]
        g_end = gend_ref[h:h + 1, pl.ds(r0, L)]
        g0_t = jnp.broadcast_to(g0_ref[h:h + 1, pl.ds(r0, L)], (L, L)).T

        s_qk = jnp.dot(qc, ktc, preferred_element_type=F32)
        p = (jnp.exp(jnp.where(causal, a_row - g0_t, -jnp.inf)) * s_qk).astype(BF16)
        c_prev = c_refs[h][...]
        qc_prev = jnp.dot(qc, c_prev.astype(BF16), preferred_element_type=F32)
        pv = jnp.dot(p, v_aug, preferred_element_type=F32)
        inter_w = jnp.exp(g_prev - g0_t)
        nd_refs[h][...] = jnp.concatenate([inter_w, inter_w], axis=1) * qc_prev + pv

        kte = (ktc.astype(F32) * jnp.exp(a_row - g_end)).astype(BF16)
        decay = jnp.exp(g_prev - g_end)
        c_refs[h][...] = (jnp.concatenate([decay, decay], axis=1) * c_prev
                          + jnp.dot(kte, v_aug, preferred_element_type=F32))
        return g_end

    def emit(c, h):
        r0 = pl.multiple_of(c * L, L)
        hs = slice(h * dh, (h + 1) * dh)
        nm_t = jnp.broadcast_to(nm_ref[h:h + 1, pl.ds(r0, L)], (L, L)).T
        nd = nd_refs[h][...]
        h_tilde = nd[:, 0:dh] / jnp.maximum(jnp.abs(nd[:, dh:2 * dh]), jnp.exp(nm_t))
        hm = _sigmoid(o_ref[pl.ds(r0, L), hs].astype(F32)) * h_tilde
        mu = jnp.dot(hm.astype(BF16), mean_mat, preferred_element_type=F32)
        hc = hm - mu
        var = jnp.dot((hc * hc).astype(BF16), mean_mat, preferred_element_type=F32)
        out_ref[pl.ds(r0, L), hs] = (hc * lax.rsqrt(var + EPS) * gn_ref[:, hs]).astype(BF16)

    zero = jnp.zeros((1, L), F32)
    g_first = tuple(mix(0, h, zero) for h in range(H))

    def body(c, g_carry):
        for h in range(H):
            emit(c - 1, h)
        return tuple(mix(c, h, g_carry[h]) for h in range(H))

    lax.fori_loop(1, S // L, body, g_first, unroll=3)
    for h in range(H):
        emit(S // L - 1, h)


def _mlstm(q, kt, v, o, gatest, gn, weights, *, batch, seq):
    T, dm = q.shape
    dh = dm // N_MLSTM_HEADS
    kern = functools.partial(_mlstm_kernel, chunk=MLSTM_CHUNK, n_cast=len(weights))
    tok = lambda w: pl.BlockSpec((seq, w), lambda b: (b, 0))
    rows = pltpu.VMEM((SUBLANES, seq), F32)
    bf16_rows = 2 * SUBLANES
    assert all(w.shape[0] % (batch * bf16_rows) == 0 for w in weights)
    w_specs = [pl.BlockSpec((w.shape[0] // batch, w.shape[1]), lambda b: (b, 0)) for w in weights]
    return pl.pallas_call(
        kern,
        grid=(batch,),
        in_specs=[tok(dm), pl.BlockSpec((dm, seq), lambda b: (0, b)), tok(dm), tok(dm),
                  pl.BlockSpec((SUBLANES, seq), lambda b: (0, b)), _const_spec((1, dm))] + w_specs,
        out_specs=[tok(dm)] + w_specs,
        out_shape=[jax.ShapeDtypeStruct((T, dm), BF16)] + [jax.ShapeDtypeStruct(w.shape, BF16) for w in weights],
        scratch_shapes=([rows, rows, rows, rows] + [pltpu.VMEM((dh, 2 * dh), F32)] * N_MLSTM_HEADS
                        + [pltpu.VMEM((MLSTM_CHUNK, 2 * dh), F32)] * N_MLSTM_HEADS),
        compiler_params=pltpu.CompilerParams(dimension_semantics=("arbitrary",),
                                             vmem_limit_bytes=VMEM_LIMIT),
        name="mlstm",
    )(q, kt, v, o, gatest, gn, *weights)


def _s5_prep_kernel(arow_ref, acol_ref, ldt_ref, bre_ref, bim_ref, cre_ref, cim_ref, dcol_ref,
                    rep_ref, selrev_ref, pw_ref, pwcol_ref, w1_ref, wout_ref, al_ref, *, chunk):
    L = chunk
    P = S5_STATE
    Hc = S5_GROUP
    R = L * Hc
    hi = lax.Precision.HIGHEST
    dt = jnp.exp(ldt_ref[...])

    def select(x, sel_ref):
        x_hi = x.astype(BF16)
        x_lo = (x - x_hi.astype(F32)).astype(BF16)
        sel = sel_ref[...]
        return (jnp.dot(x_hi, sel, preferred_element_type=F32) + jnp.dot(x_lo, sel, preferred_element_type=F32))

    def abar_pow(a_re, a_im, tau):
        mag = jnp.exp(tau * (dt * a_re))
        ang = tau * (dt * a_im)
        return mag * jnp.cos(ang), mag * jnp.sin(ang)

    a_re_c = acol_ref[:, 0:1]
    a_im_c = acol_ref[:, 1:2]
    ec_re, ec_im = abar_pow(a_re_c, a_im_c, pw_ref[...])

    ab_re = ec_re[:, 1:2]
    ab_im = ec_im[:, 1:2]
    den = a_re_c * a_re_c + a_im_c * a_im_c
    nr = ab_re - 1.0
    z_re = (nr * a_re_c + ab_im * a_im_c) / den
    z_im = (ab_im * a_re_c - nr * a_im_c) / den
    bb_re = z_re * bre_ref[...] - z_im * bim_ref[...]
    bb_im = z_re * bim_ref[...] + z_im * bre_ref[...]

    n_pow = -(-(L + 1) // SUBLANES) * SUBLANES
    tau_r = lax.broadcasted_iota(jnp.int32, (n_pow, P), 0).astype(F32)
    er_re, er_im = abar_pow(arow_ref[0:1, :], arow_ref[1:2, :], tau_r)

    def rows(e, first):
        return jnp.concatenate([jnp.broadcast_to(e[first + t:first + t + 1, :], (Hc, P)) for t in range(L)], axis=0)

    c_re = jnp.concatenate([cre_ref[...]] * L, axis=0)
    c_im = jnp.concatenate([cim_ref[...]] * L, axis=0)

    def c_times_pow(first):
        e_re, e_im = rows(er_re, first), rows(er_im, first)
        return c_re * e_re - c_im * e_im, c_re * e_im + c_im * e_re

    ce_re, ce_im = c_times_pow(0)
    kflat = (jnp.dot(ce_re, bb_re, preferred_element_type=F32, precision=hi)
             - jnp.dot(ce_im, bb_im, preferred_element_type=F32, precision=hi))
    row = lax.broadcasted_iota(jnp.int32, (R, Hc), 0)
    col = lax.broadcasted_iota(jnp.int32, (R, Hc), 1)
    kflat = kflat + jnp.where(row == col, jnp.concatenate([dcol_ref[...]] * L, axis=0), 0.0)

    m = select(kflat, rep_ref)
    blk = lax.broadcasted_iota(jnp.int32, (R, R), 1) // Hc
    sh = 1
    while sh < L:
        shifted = jnp.concatenate([jnp.zeros((sh * Hc, R), F32), m[0:R - sh * Hc, :]], axis=0)
        m = jnp.where((blk & sh) != 0, shifted, m)
        sh *= 2
    w1_ref[0:R, :] = m.astype(BF16)

    ev_re, ev_im = select(ec_re, selrev_ref), select(ec_im, selrev_ref)
    bt_re, bt_im = select(bb_re, rep_ref), select(bb_im, rep_ref)
    w1_ref[R:R + P, :] = (ev_re * bt_re - ev_im * bt_im).astype(BF16)
    w1_ref[R + P:R + 2 * P, :] = (ev_re * bt_im + ev_im * bt_re).astype(BF16)

    co_re, co_im = c_times_pow(1)
    zero = jnp.zeros((R, P), F32)
    odd = pl.program_id(0) % 2 == 1
    halves = lambda v: jnp.where(odd, jnp.concatenate([zero, v], axis=1), jnp.concatenate([v, zero], axis=1))
    wout_ref[:, 0:2 * P] = halves(co_re).astype(BF16)
    wout_ref[:, 2 * P:4 * P] = halves(-co_im).astype(BF16)

    sc_re, sc_im = abar_pow(arow_ref[0:1, :], arow_ref[1:2, :], pwcol_ref[...])
    al_ref[0:SUBLANES, :] = sc_re
    al_ref[SUBLANES:2 * SUBLANES, :] = sc_im


def _s5_prep(a_re, a_im, log_dt, b_re, b_im, c_re, c_im, d_skip, *, chunk, n_chunks):
    G, P = a_re.shape
    Hc = S5_GROUP
    R = chunk * Hc
    arow = jnp.stack([a_re, a_im], axis=1)
    acol = jnp.stack([a_re, a_im], axis=2)
    ldt = log_dt.reshape(G, 1, 1)
    dcol = d_skip.reshape(G, Hc, 1)
    rep = (jnp.arange(R)[None, :] % Hc == jnp.arange(Hc)[:, None]).astype(BF16)
    selrev = (jnp.arange(LANES)[:, None] == chunk - 1 - jnp.arange(R)[None, :] // Hc).astype(BF16)
    n_steps = max(1, (n_chunks - 1).bit_length())
    assert chunk <= LANES and n_steps <= SUBLANES
    pw = jnp.zeros((1, LANES), F32).at[0, :chunk].set(jnp.arange(chunk, dtype=F32))
    pwcol = jnp.zeros((SUBLANES, 1), F32).at[:n_steps, 0].set(chunk * 2.0 ** jnp.arange(n_steps))
    kern = functools.partial(_s5_prep_kernel, chunk=chunk)
    grp = lambda *s: pl.BlockSpec((None,) + s, lambda g: (g,) + (0,) * len(s))
    return pl.pallas_call(
        kern,
        grid=(G,),
        in_specs=[grp(2, P), grp(P, 2), grp(1, 1), grp(P, Hc), grp(P, Hc), grp(Hc, P), grp(Hc, P),
                  grp(Hc, 1), _const_spec((Hc, R)), _const_spec((LANES, R)), _const_spec((1, LANES)),
                  _const_spec((SUBLANES, 1))],
        out_specs=[grp(R + 2 * P, R), grp(R, 4 * P), grp(2 * SUBLANES, P)],
        out_shape=[jax.ShapeDtypeStruct((G, R + 2 * P, R), BF16),
                   jax.ShapeDtypeStruct((G, R, 4 * P), BF16),
                   jax.ShapeDtypeStruct((G, 2 * SUBLANES, P), F32)],
        compiler_params=pltpu.CompilerParams(dimension_semantics=("arbitrary",),
                                             vmem_limit_bytes=VMEM_LIMIT),
        name="s5_prep",
    )(arow, acol, ldt, b_re, b_im, c_re, c_im, dcol, rep, selrev, pw, pwcol)


def _s5_kernel(u_ref, w1_ref, wout_ref, al_ref, y_ref, ut_ref, yt_ref, *, chunk, n_chunks):
    L = chunk
    P = S5_STATE
    Hc = S5_GROUP
    R = L * Hc
    C = u_ref.shape[0] // L
    n_grp = LANES // Hc

    for s in range(L):
        xt = u_ref[pl.ds(s, C, stride=L), :].T.astype(BF16)
        for j in range(n_grp):
            ut_ref[j, s * Hc:(s + 1) * Hc, :] = xt[j * Hc:(j + 1) * Hc, :]

    seg_pos = lax.broadcasted_iota(jnp.int32, (C, 2 * P), 0) & (n_chunks - 1)
    nt = (((1,), (1,)), ((), ()))

    def shifted(x, sh):
        return jnp.where(seg_pos >= sh, pltpu.roll(x, sh, 0), 0.0)

    def group_pair(i, carry):
        j0 = 2 * i
        res = [jnp.dot(w1_ref[j0 + d], ut_ref[j0 + d], preferred_element_type=F32) for d in range(2)]
        x_re = jnp.concatenate([res[0][R:R + P], res[1][R:R + P]], axis=0).T
        x_im = jnp.concatenate([res[0][R + P:R + 2 * P], res[1][R + P:R + 2 * P]], axis=0).T
        sh, k = 1, 0
        while sh < n_chunks:
            a_re = jnp.concatenate([al_ref[j0, k:k + 1, :], al_ref[j0 + 1, k:k + 1, :]], axis=1)
            a_im = jnp.concatenate([al_ref[j0, SUBLANES + k:SUBLANES + k + 1, :],
                                    al_ref[j0 + 1, SUBLANES + k:SUBLANES + k + 1, :]], axis=1)
            s_re, s_im = shifted(x_re, sh), shifted(x_im, sh)
            x_re, x_im = (x_re + a_re * s_re - a_im * s_im, x_im + a_re * s_im + a_im * s_re)
            sh *= 2
            k += 1
        xprev = jnp.concatenate([shifted(x_re, 1), shifted(x_im, 1)], axis=1).astype(BF16)
        for d in range(2):
            yt_ref[j0 + d] = res[d][0:R, :] + lax.dot_general(wout_ref[j0 + d], xprev, nt,
                                                              preferred_element_type=F32)
        return carry

    lax.fori_loop(0, n_grp // 2, group_pair, 0)

    for t in range(L):
        zt = jnp.concatenate([yt_ref[j, t * Hc:(t + 1) * Hc, :] for j in range(n_grp)], axis=0)
        y_ref[pl.ds(t, C, stride=L), :] = zt.T


def _s5(u, w1, wout, al, *, chunk, n_chunks, n_col_blocks):
    T, ds5 = u.shape
    G, R1, R = w1.shape
    P4 = wout.shape[-1]
    n_grp = LANES // S5_GROUP
    tb = T // n_col_blocks
    C = tb // chunk
    kern = functools.partial(_s5_kernel, chunk=chunk, n_chunks=n_chunks)
    grp = lambda *s: pl.BlockSpec((n_grp,) + s, lambda q, i: (q,) + (0,) * len(s))
    return pl.pallas_call(
        kern,
        grid=(G // n_grp, n_col_blocks),
        in_specs=[pl.BlockSpec((tb, LANES), lambda q, i: (i, q)),
                  grp(R1, R), grp(R, P4), grp(2 * SUBLANES, P4 // 4)],
        out_specs=pl.BlockSpec((tb, LANES), lambda q, i: (i, q)),
        out_shape=jax.ShapeDtypeStruct((T, ds5), F32),
        scratch_shapes=[pltpu.VMEM((n_grp, R, C), BF16), pltpu.VMEM((n_grp, R, C), F32)],
        compiler_params=pltpu.CompilerParams(dimension_semantics=("arbitrary", "arbitrary"),
                                             vmem_limit_bytes=VMEM_LIMIT),
        name="s5",
    )(u, w1, wout, al)


def _out_kernel(x_ref, hm_ref, y_ref, wglu_ref, bglu_ref, gs5_ref, wo_ref, g2_ref,
                wgate_ref, wup_ref, wdown_ref, gf_ref, out_ref, acc_ref, xn_ref, *, ff_chunk, final_norm):
    dm = hm_ref.shape[1]
    g = jax.nn.gelu(y_ref[...])
    z = jnp.dot(g.astype(BF16), wglu_ref[...], preferred_element_type=F32) + bglu_ref[...]
    hs = _rmsnorm(g * _sigmoid(z), gs5_ref[...]).astype(BF16)
    h1 = (x_ref[...] + jnp.dot(hm_ref[...], wo_ref[0:dm, :], preferred_element_type=F32)
          + jnp.dot(hs, wo_ref[dm:, :], preferred_element_type=F32))
    xn_ref[...] = _rmsnorm(h1, g2_ref[...]).astype(BF16)
    acc_ref[...] = h1

    def ffn_chunk(c, carry):
        cs = pl.ds(pl.multiple_of(c * ff_chunk, ff_chunk), ff_chunk)
        xn2 = xn_ref[...]
        gate = jnp.dot(xn2, wgate_ref[:, cs], preferred_element_type=F32)
        up = jnp.dot(xn2, wup_ref[:, cs], preferred_element_type=F32)
        act = (gate * _sigmoid(gate) * up).astype(BF16)
        acc_ref[...] += jnp.dot(act, wdown_ref[cs, :], preferred_element_type=F32)
        return carry

    lax.fori_loop(0, wgate_ref.shape[1] // ff_chunk, ffn_chunk, 0, unroll=2)
    out_ref[...] = _rmsnorm(acc_ref[...], gf_ref[...]) if final_norm else acc_ref[...]


def _out_block(x2, hm, y, wglu, bglu, gs5, wo, g2, wgate, wup, wdown, gf, *, tm, ff_chunk, final_norm):
    T, D = x2.shape
    dm = hm.shape[1]
    ds5 = y.shape[1]
    assert wgate.shape[1] % ff_chunk == 0
    kern = functools.partial(_out_kernel, ff_chunk=ff_chunk, final_norm=final_norm)
    tok = lambda w: pl.BlockSpec((tm, w), lambda i: (i, 0))
    return pl.pallas_call(
        kern,
        grid=(T // tm,),
        in_specs=[tok(D), tok(dm), tok(ds5), _const_spec(wglu.shape), _const_spec((1, ds5)),
                  _const_spec((1, ds5)), _const_spec(wo.shape), _const_spec((1, D)),
                  _const_spec(wgate.shape), _const_spec(wup.shape), _const_spec(wdown.shape),
                  _const_spec((1, D))],
        out_specs=tok(D),
        out_shape=jax.ShapeDtypeStruct((T, D), F32),
        scratch_shapes=[pltpu.VMEM((tm, D), F32), pltpu.VMEM((tm, D), BF16)],
        compiler_params=pltpu.CompilerParams(dimension_semantics=("arbitrary",),
                                             vmem_limit_bytes=56 * 1024 * 1024),
        name="out_block",
    )(x2, hm, y, wglu, bglu, gs5, wo, g2, wgate, wup, wdown, gf)


def kernel(x, norm1_g, w_in, if_bias, conv_qk, mlstm_norm_g, a_re, a_im, log_dt, b_re, b_im, c_re, c_im,
           d_skip, w_glu, b_glu, s5_norm_g, w_out, norm2_g, w_gate, w_up, w_down, norm_f_g):
    B, S, D = x.shape
    depth = w_in.shape[0]
    H = N_MLSTM_HEADS
    dm = mlstm_norm_g.shape[1]
    T = B * S
    L5 = S5_CHUNK
    nc5 = S // L5
    row = lambda a: a.reshape(1, -1).astype(F32)

    h = x.reshape(T, D)
    for l in range(depth):
        last = l == depth - 1
        gb = jnp.zeros((1, LANES), F32).at[0, 0:2 * H].set(if_bias[l])

        q, kt, v, o, gatest, u = _inproj(h, row(norm1_g[l]), w_in[l].T.astype(F32), gb, conv_qk[l].astype(F32),
                                         seq=S, tm=512, d_mlstm=dm)
        hm, wglu, wo, wgate, wup, wdown = _mlstm(
            q, kt, v, o, gatest, row(mlstm_norm_g[l]),
            [w.astype(F32) for w in (w_glu[l], w_out[l], w_gate[l], w_up[l], w_down[l])], batch=B, seq=S)

        w1, wout, al = _s5_prep(a_re[l], a_im[l], log_dt[l], b_re[l], b_im[l], c_re[l], c_im[l],
                                d_skip[l], chunk=L5, n_chunks=nc5)
        y = _s5(u, w1, wout, al, chunk=L5, n_chunks=nc5, n_col_blocks=2)

        h = _out_block(h, hm, y, wglu, row(b_glu[l]), row(s5_norm_g[l]), wo, row(norm2_g[l]),
                       wgate, wup, wdown, row(norm_f_g), tm=512, ff_chunk=256, final_norm=last)
    return h.reshape(B, S, D)
```

```python
import functools

import jax
import jax.numpy as jnp
from jax import lax
from jax.experimental import pallas as pl
from jax.experimental.pallas import tpu as pltpu

EPS = 1e-6
N_MLSTM_HEADS = 4
CONV_WIDTH = 4
S5_GROUP = 16
S5_STATE = 64

LANES = 128
SUBLANES = 8
MLSTM_CHUNK = 128
S5_CHUNK = 16
VMEM_LIMIT = 48 * 1024 * 1024

F32 = jnp.float32
BF16 = jnp.bfloat16


def _rmsnorm(x, g):
    ms = jnp.mean(x * x, axis=-1, keepdims=True)
    return x * lax.rsqrt(ms + EPS) * g


def _sigmoid(x):
    return 1.0 / (1.0 + jnp.exp(-x))


def _log_sigmoid(x):
    return -(jnp.maximum(-x, 0.0) + jnp.log1p(jnp.exp(-jnp.abs(x))))


def _const_spec(shape):
    return pl.BlockSpec(shape, lambda *_: (0,) * len(shape), pipeline_mode=pl.Buffered(1))


def _inproj_kernel(x_ref, g1_ref, wt_ref, gb_ref, cw_ref,
                   q_ref, kt_ref, v_ref, o_ref, gatest_ref, u_ref,
                   ext_ref, wm_ref, wg_ref, wu_ref, *, tiles_per_seq, d_mlstm, k_scale):
    tm = x_ref.shape[0]
    dm = d_mlstm
    H = N_MLSTM_HEADS
    halo = SUBLANES

    @pl.when(pl.program_id(0) == 0)
    def _():
        for c0 in range(0, 4 * dm, dm):
            wm_ref[:, c0:c0 + dm] = wt_ref[c0:c0 + dm, :].T.astype(BF16)
        wu_ref[...] = wt_ref[4 * dm + 2 * H:, :].T.astype(BF16)
        fill = jnp.zeros((LANES - 2 * H, wt_ref.shape[1]), F32)
        wg_ref[...] = jnp.concatenate([wt_ref[4 * dm:4 * dm + 2 * H, :], fill], axis=0).T.astype(BF16)

    xn = _rmsnorm(x_ref[...], g1_ref[...]).astype(BF16)

    @pl.when(pl.program_id(0) % tiles_per_seq == 0)
    def _():
        ext_ref[0:halo, :] = jnp.zeros((halo, 2 * dm), F32)

    ext_ref[halo:halo + tm, :] = jnp.dot(xn, wm_ref[:, 0:2 * dm], preferred_element_type=F32)
    v = jnp.dot(xn, wm_ref[:, 2 * dm:3 * dm], preferred_element_type=F32)
    o = jnp.dot(xn, wm_ref[:, 3 * dm:4 * dm], preferred_element_type=F32)
    u = jnp.dot(xn, wu_ref[...], preferred_element_type=F32)
    gates = jnp.dot(xn, wg_ref[...], preferred_element_type=F32) + gb_ref[...]
    v_ref[...] = v.astype(BF16)
    o_ref[...] = o.astype(BF16)
    u_ref[...] = u
    gatest_ref[...] = gates.T[0:2 * H, :]

    def zero_of(t):
        w = pltpu.bitcast(t[tm - SUBLANES:tm, t.shape[1] - LANES:], jnp.uint32)
        return pltpu.bitcast((w >> 16) >> 16, F32)
    anchor = zero_of(v)

    acc = cw_ref[CONV_WIDTH - 1:CONV_WIDTH, :] * ext_ref[halo:halo + tm, :]
    for j in range(1, CONV_WIDTH):
        acc = acc + cw_ref[CONV_WIDTH - 1 - j:CONV_WIDTH - j, :] * ext_ref[halo - j:halo - j + tm, :]
    ext_ref[0:halo, :] = ext_ref[tm:tm + halo, :]
    qk = acc * _sigmoid(acc) + jnp.tile(anchor, (tm // SUBLANES, 2 * dm // LANES))
    q_ref[...] = qk[:, 0:dm].astype(BF16)
    kt_ref[...] = (qk[:, dm:2 * dm] * k_scale).T.astype(BF16)


def _inproj(x2, g1, w_in_t, gb, cw, *, seq, tm, d_mlstm):
    T, D = x2.shape
    dm = d_mlstm
    H = N_MLSTM_HEADS
    assert 2 * H == SUBLANES
    ds5 = w_in_t.shape[0] - 4 * dm - 2 * H
    kern = functools.partial(_inproj_kernel, tiles_per_seq=seq // tm, d_mlstm=dm,
                             k_scale=float((dm // H) ** -0.5))
    tok = lambda w: pl.BlockSpec((tm, w), lambda i: (i, 0))
    return pl.pallas_call(
        kern,
        grid=(T // tm,),
        in_specs=[tok(D), _const_spec((1, D)), _const_spec(w_in_t.shape), _const_spec((1, LANES)),
                  _const_spec(cw.shape)],
        out_specs=[tok(dm), pl.BlockSpec((dm, tm), lambda i: (0, i)), tok(dm), tok(dm),
                   pl.BlockSpec((2 * H, tm), lambda i: (0, i)), tok(ds5)],
        out_shape=[jax.ShapeDtypeStruct((T, dm), BF16), jax.ShapeDtypeStruct((dm, T), BF16),
                   jax.ShapeDtypeStruct((T, dm), BF16), jax.ShapeDtypeStruct((T, dm), BF16),
                   jax.ShapeDtypeStruct((2 * H, T), F32),
                   jax.ShapeDtypeStruct((T, ds5), F32)],
        scratch_shapes=[pltpu.VMEM((tm + 2 * SUBLANES, 2 * dm), F32), pltpu.VMEM((D, 4 * dm), BF16),
                        pltpu.VMEM((D, LANES), BF16), pltpu.VMEM((D, ds5), BF16)],
        compiler_params=pltpu.CompilerParams(dimension_semantics=("arbitrary",),
                                             vmem_limit_bytes=VMEM_LIMIT),
        name="inproj",
    )(x2, g1, w_in_t, gb, cw)


def _lane_scan(x, op, identity):
    n = x.shape[-1]
    pos = lax.broadcasted_iota(jnp.int32, x.shape, x.ndim - 1)
    sh = 1
    while sh < n:
        x = op(x, jnp.where(pos >= sh, pltpu.roll(x, sh, x.ndim - 1), identity))
        sh *= 2
    return x


def _mlstm_kernel(q_ref, kt_ref, v_ref, o_ref, gatest_ref, gn_ref, *refs, chunk, n_cast):
    S = q_ref.shape[0]
    H = N_MLSTM_HEADS
    dh = q_ref.shape[1] // H
    L = chunk
    assert L == dh == LANES
    w32_refs, out_ref, w16_refs = refs[0:n_cast], refs[n_cast], refs[n_cast + 1:2 * n_cast + 1]
    a_ref, g0_ref, nm_ref, gend_ref = refs[2 * n_cast + 1:2 * n_cast + 5]
    c_refs, nd_refs = refs[2 * n_cast + 5:2 * n_cast + 5 + H], refs[2 * n_cast + 5 + H:]

    for w32_ref, w16_ref in zip(w32_refs, w16_refs):
        w16_ref[...] = w32_ref[...].astype(BF16)

    gates = gatest_ref[...]
    f_cum = pltpu.roll(_lane_scan(_log_sigmoid(gates), jnp.add, 0.0), H, 0)
    a = gates - f_cum
    g0 = jnp.maximum(_lane_scan(a, jnp.maximum, -jnp.inf), 0.0)
    a_ref[...] = a
    g0_ref[...] = g0
    nm_ref[...] = -(f_cum + g0)
    for c in range(S // L):
        gend_ref[:, c * L:(c + 1) * L] = jnp.broadcast_to(g0[:, (c + 1) * L - 1:(c + 1) * L], (SUBLANES, L))

    for c_ref in c_refs:
        c_ref[...] = jnp.zeros(c_ref.shape, F32)
    causal = (lax.broadcasted_iota(jnp.int32, (L, L), 0) >= lax.broadcasted_iota(jnp.int32, (L, L), 1))
    ones_blk = jnp.ones((L, dh), BF16)
    mean_mat = jnp.full((dh, dh), 1.0 / dh, BF16)

    def mix(c, h, g_prev):
        r0 = pl.multiple_of(c * L, L)
        hs = slice(h * dh, (h + 1) * dh)
        qc = q_ref[pl.ds(r0, L), hs]
        ktc = kt_ref[hs, pl.ds(r0, L)]
        v_aug = jnp.concatenate([v_ref[pl.ds(r0, L), hs], ones_blk], axis=1)
        a_row = a_ref[h:h + 1, pl.ds(r0, L)]
        g_end = gend_ref[h:h + 1, pl.ds(r0, L)]
        g0_t = jnp.broadcast_to(g0_ref[h:h + 1, pl.ds(r0, L)], (L, L)).T

        s_qk = jnp.dot(qc, ktc, preferred_element_type=F32)
        p = (jnp.exp(jnp.where(causal, a_row - g0_t, -jnp.inf)) * s_qk).astype(BF16)
        c_prev = c_refs[h][...]
        qc_prev = jnp.dot(qc, c_prev.astype(BF16), preferred_element_type=F32)
        pv = jnp.dot(p, v_aug, preferred_element_type=F32)
        inter_w = jnp.exp(g_prev - g0_t)
        nd_refs[h][...] = jnp.concatenate([inter_w, inter_w], axis=1) * qc_prev + pv

        kte = (ktc.astype(F32) * jnp.exp(a_row - g_end)).astype(BF16)
        decay = jnp.exp(g_prev - g_end)
        c_refs[h][...] = (jnp.concatenate([decay, decay], axis=1) * c_prev
                          + jnp.dot(kte, v_aug, preferred_element_type=F32))
        return g_end

    def emit(c, h):
        r0 = pl.multiple_of(c * L, L)
        hs = slice(h * dh, (h + 1) * dh)
        nm_t = jnp.broadcast_to(nm_ref[h:h + 1, pl.ds(r0, L)], (L, L)).T
        nd = nd_refs[h][...]
        h_tilde = nd[:, 0:dh] / jnp.maximum(jnp.abs(nd[:, dh:2 * dh]), jnp.exp(nm_t))
        hm = _sigmoid(o_ref[pl.ds(r0, L), hs].astype(F32)) * h_tilde
        mu = jnp.dot(hm.astype(BF16), mean_mat, preferred_element_type=F32)
        hc = hm - mu
        var = jnp.dot((hc * hc).astype(BF16), mean_mat, preferred_element_type=F32)
        out_ref[pl.ds(r0, L), hs] = (hc * lax.rsqrt(var + EPS) * gn_ref[:, hs]).astype(BF16)

    zero = jnp.zeros((1, L), F32)
    g_first = tuple(mix(0, h, zero) for h in range(H))

    def body(c, g_carry):
        for h in range(H):
            emit(c - 1, h)
        return tuple(mix(c, h, g_carry[h]) for h in range(H))

    lax.fori_loop(1, S // L, body, g_first, unroll=3)
    for h in range(H):
        emit(S // L - 1, h)


def _mlstm(q, kt, v, o, gatest, gn, weights, *, batch, seq):
    T, dm = q.shape
    dh = dm // N_MLSTM_HEADS
    kern = functools.partial(_mlstm_kernel, chunk=MLSTM_CHUNK, n_cast=len(weights))
    tok = lambda w: pl.BlockSpec((seq, w), lambda b: (b, 0))
    rows = pltpu.VMEM((SUBLANES, seq), F32)
    bf16_rows = 2 * SUBLANES
    assert all(w.shape[0] % (batch * bf16_rows) == 0 for w in weights)
    w_specs = [pl.BlockSpec((w.shape[0] // batch, w.shape[1]), lambda b: (b, 0)) for w in weights]
    return pl.pallas_call(
        kern,
        grid=(batch,),
        in_specs=[tok(dm), pl.BlockSpec((dm, seq), lambda b: (0, b)), tok(dm), tok(dm),
                  pl.BlockSpec((SUBLANES, seq), lambda b: (0, b)), _const_spec((1, dm))] + w_specs,
        out_specs=[tok(dm)] + w_specs,
        out_shape=[jax.ShapeDtypeStruct((T, dm), BF16)] + [jax.ShapeDtypeStruct(w.shape, BF16) for w in weights],
        scratch_shapes=([rows, rows, rows, rows] + [pltpu.VMEM((dh, 2 * dh), F32)] * N_MLSTM_HEADS
                        + [pltpu.VMEM((MLSTM_CHUNK, 2 * dh), F32)] * N_MLSTM_HEADS),
        compiler_params=pltpu.CompilerParams(dimension_semantics=("arbitrary",),
                                             vmem_limit_bytes=VMEM_LIMIT),
        name="mlstm",
    )(q, kt, v, o, gatest, gn, *weights)


def _s5_prep_kernel(arow_ref, acol_ref, ldt_ref, bre_ref, bim_ref, cre_ref, cim_ref, dcol_ref,
                    rep_ref, selrev_ref, pw_ref, pwcol_ref, w1_ref, wout_ref, al_ref, *, chunk):
    L = chunk
    P = S5_STATE
    Hc = S5_GROUP
    R = L * Hc
    hi = lax.Precision.HIGHEST
    dt = jnp.exp(ldt_ref[...])

    def select(x, sel_ref):
        x_hi = x.astype(BF16)
        x_lo = (x - x_hi.astype(F32)).astype(BF16)
        sel = sel_ref[...]
        return (jnp.dot(x_hi, sel, preferred_element_type=F32) + jnp.dot(x_lo, sel, preferred_element_type=F32))

    def abar_pow(a_re, a_im, tau):
        mag = jnp.exp(tau * (dt * a_re))
        ang = tau * (dt * a_im)
        return mag * jnp.cos(ang), mag * jnp.sin(ang)

    a_re_c = acol_ref[:, 0:1]
    a_im_c = acol_ref[:, 1:2]
    ec_re, ec_im = abar_pow(a_re_c, a_im_c, pw_ref[...])

    ab_re = ec_re[:, 1:2]
    ab_im = ec_im[:, 1:2]
    den = a_re_c * a_re_c + a_im_c * a_im_c
    nr = ab_re - 1.0
    z_re = (nr * a_re_c + ab_im * a_im_c) / den
    z_im = (ab_im * a_re_c - nr * a_im_c) / den
    bb_re = z_re * bre_ref[...] - z_im * bim_ref[...]
    bb_im = z_re * bim_ref[...] + z_im * bre_ref[...]

    n_pow = -(-(L + 1) // SUBLANES) * SUBLANES
    tau_r = lax.broadcasted_iota(jnp.int32, (n_pow, P), 0).astype(F32)
    er_re, er_im = abar_pow(arow_ref[0:1, :], arow_ref[1:2, :], tau_r)

    def rows(e, first):
        return jnp.concatenate([jnp.broadcast_to(e[first + t:first + t + 1, :], (Hc, P)) for t in range(L)], axis=0)

    c_re = jnp.concatenate([cre_ref[...]] * L, axis=0)
    c_im = jnp.concatenate([cim_ref[...]] * L, axis=0)

    def c_times_pow(first):
        e_re, e_im = rows(er_re, first), rows(er_im, first)
        return c_re * e_re - c_im * e_im, c_re * e_im + c_im * e_re

    ce_re, ce_im = c_times_pow(0)
    kflat = (jnp.dot(ce_re, bb_re, preferred_element_type=F32, precision=hi)
             - jnp.dot(ce_im, bb_im, preferred_element_type=F32, precision=hi))
    row = lax.broadcasted_iota(jnp.int32, (R, Hc), 0)
    col = lax.broadcasted_iota(jnp.int32, (R, Hc), 1)
    kflat = kflat + jnp.where(row == col, jnp.concatenate([dcol_ref[...]] * L, axis=0), 0.0)

    m = select(kflat, rep_ref)
    blk = lax.broadcasted_iota(jnp.int32, (R, R), 1) // Hc
    sh = 1
    while sh < L:
        shifted = jnp.concatenate([jnp.zeros((sh * Hc, R), F32), m[0:R - sh * Hc, :]], axis=0)
        m = jnp.where((blk & sh) != 0, shifted, m)
        sh *= 2
    w1_ref[0:R, :] = m.astype(BF16)

    ev_re, ev_im = select(ec_re, selrev_ref), select(ec_im, selrev_ref)
    bt_re, bt_im = select(bb_re, rep_ref), select(bb_im, rep_ref)
    w1_ref[R:R + P, :] = (ev_re * bt_re - ev_im * bt_im).astype(BF16)
    w1_ref[R + P:R + 2 * P, :] = (ev_re * bt_im + ev_im * bt_re).astype(BF16)

    co_re, co_im = c_times_pow(1)
    zero = jnp.zeros((R, P), F32)
    odd = pl.program_id(0) % 2 == 1
    halves = lambda v: jnp.where(odd, jnp.concatenate([zero, v], axis=1), jnp.concatenate([v, zero], axis=1))
    wout_ref[:, 0:2 * P] = halves(co_re).astype(BF16)
    wout_ref[:, 2 * P:4 * P] = halves(-co_im).astype(BF16)

    sc_re, sc_im = abar_pow(arow_ref[0:1, :], arow_ref[1:2, :], pwcol_ref[...])
    al_ref[0:SUBLANES, :] = sc_re
    al_ref[SUBLANES:2 * SUBLANES, :] = sc_im


def _s5_prep(a_re, a_im, log_dt, b_re, b_im, c_re, c_im, d_skip, *, chunk, n_chunks):
    G, P = a_re.shape
    Hc = S5_GROUP
    R = chunk * Hc
    arow = jnp.stack([a_re, a_im], axis=1)
    acol = jnp.stack([a_re, a_im], axis=2)
    ldt = log_dt.reshape(G, 1, 1)
    dcol = d_skip.reshape(G, Hc, 1)
    rep = (jnp.arange(R)[None, :] % Hc == jnp.arange(Hc)[:, None]).astype(BF16)
    selrev = (jnp.arange(LANES)[:, None] == chunk - 1 - jnp.arange(R)[None, :] // Hc).astype(BF16)
    n_steps = max(1, (n_chunks - 1).bit_length())
    assert chunk <= LANES and n_steps <= SUBLANES
    pw = jnp.zeros((1, LANES), F32).at[0, :chunk].set(jnp.arange(chunk, dtype=F32))
    pwcol = jnp.zeros((SUBLANES, 1), F32).at[:n_steps, 0].set(chunk * 2.0 ** jnp.arange(n_steps))
    kern = functools.partial(_s5_prep_kernel, chunk=chunk)
    grp = lambda *s: pl.BlockSpec((None,) + s, lambda g: (g,) + (0,) * len(s))
    return pl.pallas_call(
        kern,
        grid=(G,),
        in_specs=[grp(2, P), grp(P, 2), grp(1, 1), grp(P, Hc), grp(P, Hc), grp(Hc, P), grp(Hc, P),
                  grp(Hc, 1), _const_spec((Hc, R)), _const_spec((LANES, R)), _const_spec((1, LANES)),
                  _const_spec((SUBLANES, 1))],
        out_specs=[grp(R + 2 * P, R), grp(R, 4 * P), grp(2 * SUBLANES, P)],
        out_shape=[jax.ShapeDtypeStruct((G, R + 2 * P, R), BF16),
                   jax.ShapeDtypeStruct((G, R, 4 * P), BF16),
                   jax.ShapeDtypeStruct((G, 2 * SUBLANES, P), F32)],
        compiler_params=pltpu.CompilerParams(dimension_semantics=("arbitrary",),
                                             vmem_limit_bytes=VMEM_LIMIT),
        name="s5_prep",
    )(arow, acol, ldt, b_re, b_im, c_re, c_im, dcol, rep, selrev, pw, pwcol)


def _s5_kernel(u_ref, w1_ref, wout_ref, al_ref, y_ref, ut_ref, yt_ref, *, chunk, n_chunks):
    L = chunk
    P = S5_STATE
    Hc = S5_GROUP
    R = L * Hc
    C = u_ref.shape[0] // L
    n_grp = LANES // Hc

    for s in range(L):
        xt = u_ref[pl.ds(s, C, stride=L), :].T.astype(BF16)
        for j in range(n_grp):
            ut_ref[j, s * Hc:(s + 1) * Hc, :] = xt[j * Hc:(j + 1) * Hc, :]

    seg_pos = lax.broadcasted_iota(jnp.int32, (C, 2 * P), 0) & (n_chunks - 1)
    nt = (((1,), (1,)), ((), ()))

    def shifted(x, sh):
        return jnp.where(seg_pos >= sh, pltpu.roll(x, sh, 0), 0.0)

    def group_pair(i, carry):
        j0 = 2 * i
        res = [jnp.dot(w1_ref[j0 + d], ut_ref[j0 + d], preferred_element_type=F32) for d in range(2)]
        x_re = jnp.concatenate([res[0][R:R + P], res[1][R:R + P]], axis=0).T
        x_im = jnp.concatenate([res[0][R + P:R + 2 * P], res[1][R + P:R + 2 * P]], axis=0).T
        sh, k = 1, 0
        while sh < n_chunks:
            a_re = jnp.concatenate([al_ref[j0, k:k + 1, :], al_ref[j0 + 1, k:k + 1, :]], axis=1)
            a_im = jnp.concatenate([al_ref[j0, SUBLANES + k:SUBLANES + k + 1, :],
                                    al_ref[j0 + 1, SUBLANES + k:SUBLANES + k + 1, :]], axis=1)
            s_re, s_im = shifted(x_re, sh), shifted(x_im, sh)
            x_re, x_im = (x_re + a_re * s_re - a_im * s_im, x_im + a_re * s_im + a_im * s_re)
            sh *= 2
            k += 1
        xprev = jnp.concatenate([shifted(x_re, 1), shifted(x_im, 1)], axis=1).astype(BF16)
        for d in range(2):
            yt_ref[j0 + d] = res[d][0:R, :] + lax.dot_general(wout_ref[j0 + d], xprev, nt,
                                                              preferred_element_type=F32)
        return carry

    lax.fori_loop(0, n_grp // 2, group_pair, 0)

    for t in range(L):
        zt = jnp.concatenate([yt_ref[j, t * Hc:(t + 1) * Hc, :] for j in range(n_grp)], axis=0)
        y_ref[pl.ds(t, C, stride=L), :] = zt.T


def _s5(u, w1, wout, al, *, chunk, n_chunks, n_col_blocks):
    T, ds5 = u.shape
    G, R1, R = w1.shape
    P4 = wout.shape[-1]
    n_grp = LANES // S5_GROUP
    tb = T // n_col_blocks
    C = tb // chunk
    kern = functools.partial(_s5_kernel, chunk=chunk, n_chunks=n_chunks)
    grp = lambda *s: pl.BlockSpec((n_grp,) + s, lambda q, i: (q,) + (0,) * len(s))
    return pl.pallas_call(
        kern,
        grid=(G // n_grp, n_col_blocks),
        in_specs=[pl.BlockSpec((tb, LANES), lambda q, i: (i, q)),
                  grp(R1, R), grp(R, P4), grp(2 * SUBLANES, P4 // 4)],
        out_specs=pl.BlockSpec((tb, LANES), lambda q, i: (i, q)),
        out_shape=jax.ShapeDtypeStruct((T, ds5), F32),
        scratch_shapes=[pltpu.VMEM((n_grp, R, C), BF16), pltpu.VMEM((n_grp, R, C), F32)],
        compiler_params=pltpu.CompilerParams(dimension_semantics=("arbitrary", "arbitrary"),
                                             vmem_limit_bytes=VMEM_LIMIT),
        name="s5",
    )(u, w1, wout, al)


def _out_kernel(x_ref, hm_ref, y_ref, wglu_ref, bglu_ref, gs5_ref, wo_ref, g2_ref,
                wgate_ref, wup_ref, wdown_ref, gf_ref, out_ref, acc_ref, xn_ref, *, ff_chunk, final_norm):
    dm = hm_ref.shape[1]
    g = jax.nn.gelu(y_ref[...])
    z = jnp.dot(g.astype(BF16), wglu_ref[...], preferred_element_type=F32) + bglu_ref[...]
    hs = _rmsnorm(g * _sigmoid(z), gs5_ref[...]).astype(BF16)
    h1 = (x_ref[...] + jnp.dot(hm_ref[...], wo_ref[0:dm, :], preferred_element_type=F32)
          + jnp.dot(hs, wo_ref[dm:, :], preferred_element_type=F32))
    xn_ref[...] = _rmsnorm(h1, g2_ref[...]).astype(BF16)
    acc_ref[...] = h1

    def ffn_chunk(c, carry):
        cs = pl.ds(pl.multiple_of(c * ff_chunk, ff_chunk), ff_chunk)
        xn2 = xn_ref[...]
        gate = jnp.dot(xn2, wgate_ref[:, cs], preferred_element_type=F32)
        up = jnp.dot(xn2, wup_ref[:, cs], preferred_element_type=F32)
        act = (gate * _sigmoid(gate) * up).astype(BF16)
        acc_ref[...] += jnp.dot(act, wdown_ref[cs, :], preferred_element_type=F32)
        return carry

    lax.fori_loop(0, wgate_ref.shape[1] // ff_chunk, ffn_chunk, 0, unroll=2)
    out_ref[...] = _rmsnorm(acc_ref[...], gf_ref[...]) if final_norm else acc_ref[...]


def _out_block(x2, hm, y, wglu, bglu, gs5, wo, g2, wgate, wup, wdown, gf, *, tm, ff_chunk, final_norm):
    T, D = x2.shape
    dm = hm.shape[1]
    ds5 = y.shape[1]
    assert wgate.shape[1] % ff_chunk == 0
    kern = functools.partial(_out_kernel, ff_chunk=ff_chunk, final_norm=final_norm)
    tok = lambda w: pl.BlockSpec((tm, w), lambda i: (i, 0))
    return pl.pallas_call(
        kern,
        grid=(T // tm,),
        in_specs=[tok(D), tok(dm), tok(ds5), _const_spec(wglu.shape), _const_spec((1, ds5)),
                  _const_spec((1, ds5)), _const_spec(wo.shape), _const_spec((1, D)),
                  _const_spec(wgate.shape), _const_spec(wup.shape), _const_spec(wdown.shape),
                  _const_spec((1, D))],
        out_specs=tok(D),
        out_shape=jax.ShapeDtypeStruct((T, D), F32),
        scratch_shapes=[pltpu.VMEM((tm, D), F32), pltpu.VMEM((tm, D), BF16)],
        compiler_params=pltpu.CompilerParams(dimension_semantics=("arbitrary",),
                                             vmem_limit_bytes=56 * 1024 * 1024),
        name="out_block",
    )(x2, hm, y, wglu, bglu, gs5, wo, g2, wgate, wup, wdown, gf)


def kernel(x, norm1_g, w_in, if_bias, conv_qk, mlstm_norm_g, a_re, a_im, log_dt, b_re, b_im, c_re, c_im,
           d_skip, w_glu, b_glu, s5_norm_g, w_out, norm2_g, w_gate, w_up, w_down, norm_f_g):
    B, S, D = x.shape
    depth = w_in.shape[0]
    H = N_MLSTM_HEADS
    dm = mlstm_norm_g.shape[1]
    T = B * S
    L5 = S5_CHUNK
    nc5 = S // L5
    row = lambda a: a.reshape(1, -1).astype(F32)

    h = x.reshape(T, D)
    for l in range(depth):
        last = l == depth - 1
        gb = jnp.zeros((1, LANES), F32).at[0, 0:2 * H].set(if_bias[l])

        q, kt, v, o, gatest, u = _inproj(h, row(norm1_g[l]), w_in[l].T.astype(F32), gb, conv_qk[l].astype(F32),
                                         seq=S, tm=512, d_mlstm=dm)
        hm, wglu, wo, wgate, wup, wdown = _mlstm(
            q, kt, v, o, gatest, row(mlstm_norm_g[l]),
            [w.astype(F32) for w in (w_glu[l], w_out[l], w_gate[l], w_up[l], w_down[l])], batch=B, seq=S)

        w1, wout, al = _s5_prep(a_re[l], a_im[l], log_dt[l], b_re[l], b_im[l], c_re[l], c_im[l],
                                d_skip[l], chunk=L5, n_chunks=nc5)
        y = _s5(u, w1, wout, al, chunk=L5, n_chunks=nc5, n_col_blocks=2)

        h = _out_block(h, hm, y, wglu, row(b_glu[l]), row(s5_norm_g[l]), wo, row(norm2_g[l]),
                       wgate, wup, wdown, row(norm_f_g), tm=512, ff_chunk=256, final_norm=last)
    return h.reshape(B, S, D)
```

```python
import functools

import jax
import jax.numpy as jnp
from jax import lax
from jax.experimental import pallas as pl
from jax.experimental.pallas import tpu as pltpu

EPS = 1e-6
N_MLSTM_HEADS = 4
CONV_WIDTH = 4
S5_GROUP = 16
S5_STATE = 64

LANES = 128
SUBLANES = 8
MLSTM_CHUNK = 128
S5_CHUNK = 16
VMEM_LIMIT = 48 * 1024 * 1024

F32 = jnp.float32
BF16 = jnp.bfloat16


def _rmsnorm(x, g):
    ms = jnp.mean(x * x, axis=-1, keepdims=True)
    return x * lax.rsqrt(ms + EPS) * g


def _sigmoid(x):
    return 1.0 / (1.0 + jnp.exp(-x))


def _log_sigmoid(x):
    return -(jnp.maximum(-x, 0.0) + jnp.log1p(jnp.exp(-jnp.abs(x))))


def _const_spec(shape):
    return pl.BlockSpec(shape, lambda *_: (0,) * len(shape), pipeline_mode=pl.Buffered(1))


def _inproj_kernel(x_ref, g1_ref, wt_ref, gb_ref, cw_ref,
                   q_ref, kt_ref, v_ref, o_ref, gatest_ref, u_ref,
                   ext_ref, wm_ref, wg_ref, wu_ref, *, tiles_per_seq, d_mlstm, k_scale):
    tm = x_ref.shape[0]
    dm = d_mlstm
    H = N_MLSTM_HEADS
    halo = SUBLANES

    @pl.when(pl.program_id(0) == 0)
    def _():
        for c0 in range(0, 4 * dm, dm):
            wm_ref[:, c0:c0 + dm] = wt_ref[c0:c0 + dm, :].T.astype(BF16)
        wu_ref[...] = wt_ref[4 * dm + 2 * H:, :].T.astype(BF16)
        fill = jnp.zeros((LANES - 2 * H, wt_ref.shape[1]), F32)
        wg_ref[...] = jnp.concatenate([wt_ref[4 * dm:4 * dm + 2 * H, :], fill], axis=0).T.astype(BF16)

    xn = _rmsnorm(x_ref[...], g1_ref[...]).astype(BF16)

    @pl.when(pl.program_id(0) % tiles_per_seq == 0)
    def _():
        ext_ref[0:halo, :] = jnp.zeros((halo, 2 * dm), F32)

    ext_ref[halo:halo + tm, :] = jnp.dot(xn, wm_ref[:, 0:2 * dm], preferred_element_type=F32)
    v = jnp.dot(xn, wm_ref[:, 2 * dm:3 * dm], preferred_element_type=F32)
    o = jnp.dot(xn, wm_ref[:, 3 * dm:4 * dm], preferred_element_type=F32)
    u = jnp.dot(xn, wu_ref[...], preferred_element_type=F32)
    gates = jnp.dot(xn, wg_ref[...], preferred_element_type=F32) + gb_ref[...]
    v_ref[...] = v.astype(BF16)
    o_ref[...] = o.astype(BF16)
    u_ref[...] = u
    gatest_ref[...] = gates.T[0:2 * H, :]

    def zero_of(t):
        w = pltpu.bitcast(t[tm - SUBLANES:tm, t.shape[1] - LANES:], jnp.uint32)
        return pltpu.bitcast((w >> 16) >> 16, F32)
    anchor = zero_of(v)

    acc = cw_ref[CONV_WIDTH - 1:CONV_WIDTH, :] * ext_ref[halo:halo + tm, :]
    for j in range(1, CONV_WIDTH):
        acc = acc + cw_ref[CONV_WIDTH - 1 - j:CONV_WIDTH - j, :] * ext_ref[halo - j:halo - j + tm, :]
    ext_ref[0:halo, :] = ext_ref[tm:tm + halo, :]
    qk = acc * _sigmoid(acc) + jnp.tile(anchor, (tm // SUBLANES, 2 * dm // LANES))
    q_ref[...] = qk[:, 0:dm].astype(BF16)
    kt_ref[...] = (qk[:, dm:2 * dm] * k_scale).T.astype(BF16)


def _inproj(x2, g1, w_in_t, gb, cw, *, seq, tm, d_mlstm):
    T, D = x2.shape
    dm = d_mlstm
    H = N_MLSTM_HEADS
    assert 2 * H == SUBLANES
    ds5 = w_in_t.shape[0] - 4 * dm - 2 * H
    kern = functools.partial(_inproj_kernel, tiles_per_seq=seq // tm, d_mlstm=dm,
                             k_scale=float((dm // H) ** -0.5))
    tok = lambda w: pl.BlockSpec((tm, w), lambda i: (i, 0))
    return pl.pallas_call(
        kern,
        grid=(T // tm,),
        in_specs=[tok(D), _const_spec((1, D)), _const_spec(w_in_t.shape), _const_spec((1, LANES)),
                  _const_spec(cw.shape)],
        out_specs=[tok(dm), pl.BlockSpec((dm, tm), lambda i: (0, i)), tok(dm), tok(dm),
                   pl.BlockSpec((2 * H, tm), lambda i: (0, i)), tok(ds5)],
        out_shape=[jax.ShapeDtypeStruct((T, dm), BF16), jax.ShapeDtypeStruct((dm, T), BF16),
                   jax.ShapeDtypeStruct((T, dm), BF16), jax.ShapeDtypeStruct((T, dm), BF16),
                   jax.ShapeDtypeStruct((2 * H, T), F32),
                   jax.ShapeDtypeStruct((T, ds5), F32)],
        scratch_shapes=[pltpu.VMEM((tm + 2 * SUBLANES, 2 * dm), F32), pltpu.VMEM((D, 4 * dm), BF16),
                        pltpu.VMEM((D, LANES), BF16), pltpu.VMEM((D, ds5), BF16)],
        compiler_params=pltpu.CompilerParams(dimension_semantics=("arbitrary",),
                                             vmem_limit_bytes=VMEM_LIMIT),
        name="inproj",
    )(x2, g1, w_in_t, gb, cw)


def _lane_scan(x, op, identity):
    n = x.shape[-1]
    pos = lax.broadcasted_iota(jnp.int32, x.shape, x.ndim - 1)
    sh = 1
    while sh < n:
        x = op(x, jnp.where(pos >= sh, pltpu.roll(x, sh, x.ndim - 1), identity))
        sh *= 2
    return x


def _mlstm_kernel(q_ref, kt_ref, v_ref, o_ref, gatest_ref, gn_ref, *refs, chunk, n_cast):
    S = q_ref.shape[0]
    H = N_MLSTM_HEADS
    dh = q_ref.shape[1] // H
    L = chunk
    assert L == dh == LANES
    w32_refs, out_ref, w16_refs = refs[0:n_cast], refs[n_cast], refs[n_cast + 1:2 * n_cast + 1]
    a_ref, g0_ref, nm_ref, gend_ref = refs[2 * n_cast + 1:2 * n_cast + 5]
    c_refs, nd_refs = refs[2 * n_cast + 5:2 * n_cast + 5 + H], refs[2 * n_cast + 5 + H:]

    for w32_ref, w16_ref in zip(w32_refs, w16_refs):
        w16_ref[...] = w32_ref[...].astype(BF16)

    gates = gatest_ref[...]
    f_cum = pltpu.roll(_lane_scan(_log_sigmoid(gates), jnp.add, 0.0), H, 0)
    a = gates - f_cum
    g0 = jnp.maximum(_lane_scan(a, jnp.maximum, -jnp.inf), 0.0)
    a_ref[...] = a
    g0_ref[...] = g0
    nm_ref[...] = -(f_cum + g0)
    for c in range(S // L):
        gend_ref[:, c * L:(c + 1) * L] = jnp.broadcast_to(g0[:, (c + 1) * L - 1:(c + 1) * L], (SUBLANES, L))

    for c_ref in c_refs:
        c_ref[...] = jnp.zeros(c_ref.shape, F32)
    causal = (lax.broadcasted_iota(jnp.int32, (L, L), 0) >= lax.broadcasted_iota(jnp.int32, (L, L), 1))
    ones_blk = jnp.ones((L, dh), BF16)
    mean_mat = jnp.full((dh, dh), 1.0 / dh, BF16)

    def mix(c, h, g_prev):
        r0 = pl.multiple_of(c * L, L)
        hs = slice(h * dh, (h + 1) * dh)
        qc = q_ref[pl.ds(r0, L), hs]
        ktc = kt_ref[hs, pl.ds(r0, L)]
        v_aug = jnp.concatenate([v_ref[pl.ds(r0, L), hs], ones_blk], axis=1)
        a_row = a_ref[h:h + 1, pl.ds(r0, L)]
        g_end = gend_ref[h:h + 1, pl.ds(r0, L)]
        g0_t = jnp.broadcast_to(g0_ref[h:h + 1, pl.ds(r0, L)], (L, L)).T

        s_qk = jnp.dot(qc, ktc, preferred_element_type=F32)
        p = (jnp.exp(jnp.where(causal, a_row - g0_t, -jnp.inf)) * s_qk).astype(BF16)
        c_prev = c_refs[h][...]
        qc_prev = jnp.dot(qc, c_prev.astype(BF16), preferred_element_type=F32)
        pv = jnp.dot(p, v_aug, preferred_element_type=F32)
        inter_w = jnp.exp(g_prev - g0_t)
        nd_refs[h][...] = jnp.concatenate([inter_w, inter_w], axis=1) * qc_prev + pv

        kte = (ktc.astype(F32) * jnp.exp(a_row - g_end)).astype(BF16)
        decay = jnp.exp(g_prev - g_end)
        c_refs[h][...] = (jnp.concatenate([decay, decay], axis=1) * c_prev
                          + jnp.dot(kte, v_aug, preferred_element_type=F32))
        return g_end

    def emit(c, h):
        r0 = pl.multiple_of(c * L, L)
        hs = slice(h * dh, (h + 1) * dh)
        nm_t = jnp.broadcast_to(nm_ref[h:h + 1, pl.ds(r0, L)], (L, L)).T
        nd = nd_refs[h][...]
        h_tilde = nd[:, 0:dh] / jnp.maximum(jnp.abs(nd[:, dh:2 * dh]), jnp.exp(nm_t))
        hm = _sigmoid(o_ref[pl.ds(r0, L), hs].astype(F32)) * h_tilde
        mu = jnp.dot(hm.astype(BF16), mean_mat, preferred_element_type=F32)
        hc = hm - mu
        var = jnp.dot((hc * hc).astype(BF16), mean_mat, preferred_element_type=F32)
        out_ref[pl.ds(r0, L), hs] = (hc * lax.rsqrt(var + EPS) * gn_ref[:, hs]).astype(BF16)

    zero = jnp.zeros((1, L), F32)
    g_first = tuple(mix(0, h, zero) for h in range(H))

    def body(c, g_carry):
        for h in range(H):
            emit(c - 1, h)
        return tuple(mix(c, h, g_carry[h]) for h in range(H))

    lax.fori_loop(1, S // L, body, g_first, unroll=3)
    for h in range(H):
        emit(S // L - 1, h)


def _mlstm(q, kt, v, o, gatest, gn, weights, *, batch, seq):
    T, dm = q.shape
    dh = dm // N_MLSTM_HEADS
    kern = functools.partial(_mlstm_kernel, chunk=MLSTM_CHUNK, n_cast=len(weights))
    tok = lambda w: pl.BlockSpec((seq, w), lambda b: (b, 0))
    rows = pltpu.VMEM((SUBLANES, seq), F32)
    bf16_rows = 2 * SUBLANES
    assert all(w.shape[0] % (batch * bf16_rows) == 0 for w in weights)
    w_specs = [pl.BlockSpec((w.shape[0] // batch, w.shape[1]), lambda b: (b, 0)) for w in weights]
    return pl.pallas_call(
        kern,
        grid=(batch,),
        in_specs=[tok(dm), pl.BlockSpec((dm, seq), lambda b: (0, b)), tok(dm), tok(dm),
                  pl.BlockSpec((SUBLANES, seq), lambda b: (0, b)), _const_spec((1, dm))] + w_specs,
        out_specs=[tok(dm)] + w_specs,
        out_shape=[jax.ShapeDtypeStruct((T, dm), BF16)] + [jax.ShapeDtypeStruct(w.shape, BF16) for w in weights],
        scratch_shapes=([rows, rows, rows, rows] + [pltpu.VMEM((dh, 2 * dh), F32)] * N_MLSTM_HEADS
                        + [pltpu.VMEM((MLSTM_CHUNK, 2 * dh), F32)] * N_MLSTM_HEADS),
        compiler_params=pltpu.CompilerParams(dimension_semantics=("arbitrary",),
                                             vmem_limit_bytes=VMEM_LIMIT),
        name="mlstm",
    )(q, kt, v, o, gatest, gn, *weights)


def _s5_prep_kernel(arow_ref, acol_ref, ldt_ref, bre_ref, bim_ref, cre_ref, cim_ref, dcol_ref,
                    rep_ref, selrev_ref, pw_ref, pwcol_ref, w1_ref, wout_ref, al_ref, *, chunk):
    L = chunk
    P = S5_STATE
    Hc = S5_GROUP
    R = L * Hc
    hi = lax.Precision.HIGHEST
    dt = jnp.exp(ldt_ref[...])

    def select(x, sel_ref):
        x_hi = x.astype(BF16)
        x_lo = (x - x_hi.astype(F32)).astype(BF16)
        sel = sel_ref[...]
        return (jnp.dot(x_hi, sel, preferred_element_type=F32) + jnp.dot(x_lo, sel, preferred_element_type=F32))

    def abar_pow(a_re, a_im, tau):
        mag = jnp.exp(tau * (dt * a_re))
        ang = tau * (dt * a_im)
        return mag * jnp.cos(ang), mag * jnp.sin(ang)

    a_re_c = acol_ref[:, 0:1]
    a_im_c = acol_ref[:, 1:2]
    ec_re, ec_im = abar_pow(a_re_c, a_im_c, pw_ref[...])

    ab_re = ec_re[:, 1:2]
    ab_im = ec_im[:, 1:2]
    den = a_re_c * a_re_c + a_im_c * a_im_c
    nr = ab_re - 1.0
    z_re = (nr * a_re_c + ab_im * a_im_c) / den
    z_im = (ab_im * a_re_c - nr * a_im_c) / den
    bb_re = z_re * bre_ref[...] - z_im * bim_ref[...]
    bb_im = z_re * bim_ref[...] + z_im * bre_ref[...]

    n_pow = -(-(L + 1) // SUBLANES) * SUBLANES
    tau_r = lax.broadcasted_iota(jnp.int32, (n_pow, P), 0).astype(F32)
    er_re, er_im = abar_pow(arow_ref[0:1, :], arow_ref[1:2, :], tau_r)

    def rows(e, first):
        return jnp.concatenate([jnp.broadcast_to(e[first + t:first + t + 1, :], (Hc, P)) for t in range(L)], axis=0)

    c_re = jnp.concatenate([cre_ref[...]] * L, axis=0)
    c_im = jnp.concatenate([cim_ref[...]] * L, axis=0)

    def c_times_pow(first):
        e_re, e_im = rows(er_re, first), rows(er_im, first)
        return c_re * e_re - c_im * e_im, c_re * e_im + c_im * e_re

    ce_re, ce_im = c_times_pow(0)
    kflat = (jnp.dot(ce_re, bb_re, preferred_element_type=F32, precision=hi)
             - jnp.dot(ce_im, bb_im, preferred_element_type=F32, precision=hi))
    row = lax.broadcasted_iota(jnp.int32, (R, Hc), 0)
    col = lax.broadcasted_iota(jnp.int32, (R, Hc), 1)
    kflat = kflat + jnp.where(row == col, jnp.concatenate([dcol_ref[...]] * L, axis=0), 0.0)

    m = select(kflat, rep_ref)
    blk = lax.broadcasted_iota(jnp.int32, (R, R), 1) // Hc
    sh = 1
    while sh < L:
        shifted = jnp.concatenate([jnp.zeros((sh * Hc, R), F32), m[0:R - sh * Hc, :]], axis=0)
        m = jnp.where((blk & sh) != 0, shifted, m)
        sh *= 2
    w1_ref[0:R, :] = m.astype(BF16)

    ev_re, ev_im = select(ec_re, selrev_ref), select(ec_im, selrev_ref)
    bt_re, bt_im = select(bb_re, rep_ref), select(bb_im, rep_ref)
    w1_ref[R:R + P, :] = (ev_re * bt_re - ev_im * bt_im).astype(BF16)
    w1_ref[R + P:R + 2 * P, :] = (ev_re * bt_im + ev_im * bt_re).astype(BF16)

    co_re, co_im = c_times_pow(1)
    zero = jnp.zeros((R, P), F32)
    odd = pl.program_id(0) % 2 == 1
    halves = lambda v: jnp.where(odd, jnp.concatenate([zero, v], axis=1), jnp.concatenate([v, zero], axis=1))
    wout_ref[:, 0:2 * P] = halves(co_re).astype(BF16)
    wout_ref[:, 2 * P:4 * P] = halves(-co_im).astype(BF16)

    sc_re, sc_im = abar_pow(arow_ref[0:1, :], arow_ref[1:2, :], pwcol_ref[...])
    al_ref[0:SUBLANES, :] = sc_re
    al_ref[SUBLANES:2 * SUBLANES, :] = sc_im


def _s5_prep(a_re, a_im, log_dt, b_re, b_im, c_re, c_im, d_skip, *, chunk, n_chunks):
    G, P = a_re.shape
    Hc = S5_GROUP
    R = chunk * Hc
    arow = jnp.stack([a_re, a_im], axis=1)
    acol = jnp.stack([a_re, a_im], axis=2)
    ldt = log_dt.reshape(G, 1, 1)
    dcol = d_skip.reshape(G, Hc, 1)
    rep = (jnp.arange(R)[None, :] % Hc == jnp.arange(Hc)[:, None]).astype(BF16)
    selrev = (jnp.arange(LANES)[:, None] == chunk - 1 - jnp.arange(R)[None, :] // Hc).astype(BF16)
    n_steps = max(1, (n_chunks - 1).bit_length())
    assert chunk <= LANES and n_steps <= SUBLANES
    pw = jnp.zeros((1, LANES), F32).at[0, :chunk].set(jnp.arange(chunk, dtype=F32))
    pwcol = jnp.zeros((SUBLANES, 1), F32).at[:n_steps, 0].set(chunk * 2.0 ** jnp.arange(n_steps))
    kern = functools.partial(_s5_prep_kernel, chunk=chunk)
    grp = lambda *s: pl.BlockSpec((None,) + s, lambda g: (g,) + (0,) * len(s))
    return pl.pallas_call(
        kern,
        grid=(G,),
        in_specs=[grp(2, P), grp(P, 2), grp(1, 1), grp(P, Hc), grp(P, Hc), grp(Hc, P), grp(Hc, P),
                  grp(Hc, 1), _const_spec((Hc, R)), _const_spec((LANES, R)), _const_spec((1, LANES)),
                  _const_spec((SUBLANES, 1))],
        out_specs=[grp(R + 2 * P, R), grp(R, 4 * P), grp(2 * SUBLANES, P)],
        out_shape=[jax.ShapeDtypeStruct((G, R + 2 * P, R), BF16),
                   jax.ShapeDtypeStruct((G, R, 4 * P), BF16),
                   jax.ShapeDtypeStruct((G, 2 * SUBLANES, P), F32)],
        compiler_params=pltpu.CompilerParams(dimension_semantics=("arbitrary",),
                                             vmem_limit_bytes=VMEM_LIMIT),
        name="s5_prep",
    )(arow, acol, ldt, b_re, b_im, c_re, c_im, dcol, rep, selrev, pw, pwcol)


def _s5_kernel(u_ref, w1_ref, wout_ref, al_ref, y_ref, ut_ref, yt_ref, *, chunk, n_chunks):
    L = chunk
    P = S5_STATE
    Hc = S5_GROUP
    R = L * Hc
    C = u_ref.shape[0] // L
    n_grp = LANES // Hc

    for s in range(L):
        xt = u_ref[pl.ds(s, C, stride=L), :].T.astype(BF16)
        for j in range(n_grp):
            ut_ref[j, s * Hc:(s + 1) * Hc, :] = xt[j * Hc:(j + 1) * Hc, :]

    seg_pos = lax.broadcasted_iota(jnp.int32, (C, 2 * P), 0) & (n_chunks - 1)
    nt = (((1,), (1,)), ((), ()))

    def shifted(x, sh):
        return jnp.where(seg_pos >= sh, pltpu.roll(x, sh, 0), 0.0)

    def group_pair(i, carry):
        j0 = 2 * i
        res = [jnp.dot(w1_ref[j0 + d], ut_ref[j0 + d], preferred_element_type=F32) for d in range(2)]
        x_re = jnp.concatenate([res[0][R:R + P], res[1][R:R + P]], axis=0).T
        x_im = jnp.concatenate([res[0][R + P:R + 2 * P], res[1][R + P:R + 2 * P]], axis=0).T
        sh, k = 1, 0
        while sh < n_chunks:
            a_re = jnp.concatenate([al_ref[j0, k:k + 1, :], al_ref[j0 + 1, k:k + 1, :]], axis=1)
            a_im = jnp.concatenate([al_ref[j0, SUBLANES + k:SUBLANES + k + 1, :],
                                    al_ref[j0 + 1, SUBLANES + k:SUBLANES + k + 1, :]], axis=1)
            s_re, s_im = shifted(x_re, sh), shifted(x_im, sh)
            x_re, x_im = (x_re + a_re * s_re - a_im * s_im, x_im + a_re * s_im + a_im * s_re)
            sh *= 2
            k += 1
        xprev = jnp.concatenate([shifted(x_re, 1), shifted(x_im, 1)], axis=1).astype(BF16)
        for d in range(2):
            yt_ref[j0 + d] = res[d][0:R, :] + lax.dot_general(wout_ref[j0 + d], xprev, nt,
                                                              preferred_element_type=F32)
        return carry

    lax.fori_loop(0, n_grp // 2, group_pair, 0)

    for t in range(L):
        zt = jnp.concatenate([yt_ref[j, t * Hc:(t + 1) * Hc, :] for j in range(n_grp)], axis=0)
        y_ref[pl.ds(t, C, stride=L), :] = zt.T


def _s5(u, w1, wout, al, *, chunk, n_chunks, n_col_blocks):
    T, ds5 = u.shape
    G, R1, R = w1.shape
    P4 = wout.shape[-1]
    n_grp = LANES // S5_GROUP
    tb = T // n_col_blocks
    C = tb // chunk
    kern = functools.partial(_s5_kernel, chunk=chunk, n_chunks=n_chunks)
    grp = lambda *s: pl.BlockSpec((n_grp,) + s, lambda q, i: (q,) + (0,) * len(s))
    return pl.pallas_call(
        kern,
        grid=(G // n_grp, n_col_blocks),
        in_specs=[pl.BlockSpec((tb, LANES), lambda q, i: (i, q)),
                  grp(R1, R), grp(R, P4), grp(2 * SUBLANES, P4 // 4)],
        out_specs=pl.BlockSpec((tb, LANES), lambda q, i: (i, q)),
        out_shape=jax.ShapeDtypeStruct((T, ds5), F32),
        scratch_shapes=[pltpu.VMEM((n_grp, R, C), BF16), pltpu.VMEM((n_grp, R, C), F32)],
        compiler_params=pltpu.CompilerParams(dimension_semantics=("arbitrary", "arbitrary"),
                                             vmem_limit_bytes=VMEM_LIMIT),
        name="s5",
    )(u, w1, wout, al)


def _out_kernel(x_ref, hm_ref, y_ref, wglu_ref, bglu_ref, gs5_ref, wo_ref, g2_ref,
                wgate_ref, wup_ref, wdown_ref, gf_ref, out_ref, acc_ref, xn_ref, *, ff_chunk, final_norm):
    dm = hm_ref.shape[1]
    g = jax.nn.gelu(y_ref[...])
    z = jnp.dot(g.astype(BF16), wglu_ref[...], preferred_element_type=F32) + bglu_ref[...]
    hs = _rmsnorm(g * _sigmoid(z), gs5_ref[...]).astype(BF16)
    h1 = (x_ref[...] + jnp.dot(hm_ref[...], wo_ref[0:dm, :], preferred_element_type=F32)
          + jnp.dot(hs, wo_ref[dm:, :], preferred_element_type=F32))
    xn_ref[...] = _rmsnorm(h1, g2_ref[...]).astype(BF16)
    acc_ref[...] = h1

    def ffn_chunk(c, carry):
        cs = pl.ds(pl.multiple_of(c * ff_chunk, ff_chunk), ff_chunk)
        xn2 = xn_ref[...]
        gate = jnp.dot(xn2, wgate_ref[:, cs], preferred_element_type=F32)
        up = jnp.dot(xn2, wup_ref[:, cs], preferred_element_type=F32)
        act = (gate * _sigmoid(gate) * up).astype(BF16)
        acc_ref[...] += jnp.dot(act, wdown_ref[cs, :], preferred_element_type=F32)
        return carry

    lax.fori_loop(0, wgate_ref.shape[1] // ff_chunk, ffn_chunk, 0, unroll=4)
    out_ref[...] = _rmsnorm(acc_ref[...], gf_ref[...]) if final_norm else acc_ref[...]


def _out_block(x2, hm, y, wglu, bglu, gs5, wo, g2, wgate, wup, wdown, gf, *, tm, ff_chunk, final_norm):
    T, D = x2.shape
    dm = hm.shape[1]
    ds5 = y.shape[1]
    assert wgate.shape[1] % ff_chunk == 0
    kern = functools.partial(_out_kernel, ff_chunk=ff_chunk, final_norm=final_norm)
    tok = lambda w: pl.BlockSpec((tm, w), lambda i: (i, 0))
    return pl.pallas_call(
        kern,
        grid=(T // tm,),
        in_specs=[tok(D), tok(dm), tok(ds5), _const_spec(wglu.shape), _const_spec((1, ds5)),
                  _const_spec((1, ds5)), _const_spec(wo.shape), _const_spec((1, D)),
                  _const_spec(wgate.shape), _const_spec(wup.shape), _const_spec(wdown.shape),
                  _const_spec((1, D))],
        out_specs=tok(D),
        out_shape=jax.ShapeDtypeStruct((T, D), F32),
        scratch_shapes=[pltpu.VMEM((tm, D), F32), pltpu.VMEM((tm, D), BF16)],
        compiler_params=pltpu.CompilerParams(dimension_semantics=("arbitrary",),
                                             vmem_limit_bytes=56 * 1024 * 1024),
        name="out_block",
    )(x2, hm, y, wglu, bglu, gs5, wo, g2, wgate, wup, wdown, gf)


def kernel(x, norm1_g, w_in, if_bias, conv_qk, mlstm_norm_g, a_re, a_im, log_dt, b_re, b_im, c_re, c_im,
           d_skip, w_glu, b_glu, s5_norm_g, w_out, norm2_g, w_gate, w_up, w_down, norm_f_g):
    B, S, D = x.shape
    depth = w_in.shape[0]
    H = N_MLSTM_HEADS
    dm = mlstm_norm_g.shape[1]
    T = B * S
    L5 = S5_CHUNK
    nc5 = S // L5
    row = lambda a: a.reshape(1, -1).astype(F32)

    h = x.reshape(T, D)
    for l in range(depth):
        last = l == depth - 1
        gb = jnp.zeros((1, LANES), F32).at[0, 0:2 * H].set(if_bias[l])

        q, kt, v, o, gatest, u = _inproj(h, row(norm1_g[l]), w_in[l].T.astype(F32), gb, conv_qk[l].astype(F32),
                                         seq=S, tm=512, d_mlstm=dm)
        hm, wglu, wo, wgate, wup, wdown = _mlstm(
            q, kt, v, o, gatest, row(mlstm_norm_g[l]),
            [w.astype(F32) for w in (w_glu[l], w_out[l], w_gate[l], w_up[l], w_down[l])], batch=B, seq=S)

        w1, wout, al = _s5_prep(a_re[l], a_im[l], log_dt[l], b_re[l], b_im[l], c_re[l], c_im[l],
                                d_skip[l], chunk=L5, n_chunks=nc5)
        y = _s5(u, w1, wout, al, chunk=L5, n_chunks=nc5, n_col_blocks=2)

        h = _out_block(h, hm, y, wglu, row(b_glu[l]), row(s5_norm_g[l]), wo, row(norm2_g[l]),
                       wgate, wup, wdown, row(norm_f_g), tm=512, ff_chunk=256, final_norm=last)
    return h.reshape(B, S, D)
```

```python
import functools

import jax
import jax.numpy as jnp
from jax import lax
from jax.experimental import pallas as pl
from jax.experimental.pallas import tpu as pltpu

EPS = 1e-6
N_MLSTM_HEADS = 4
CONV_WIDTH = 4
S5_GROUP = 16
S5_STATE = 64

LANES = 128
SUBLANES = 8
MLSTM_CHUNK = 128
S5_CHUNK = 16
VMEM_LIMIT = 48 * 1024 * 1024

F32 = jnp.float32
BF16 = jnp.bfloat16


def _rmsnorm(x, g):
    ms = jnp.mean(x * x, axis=-1, keepdims=True)
    return x * lax.rsqrt(ms + EPS) * g


def _sigmoid(x):
    return 1.0 / (1.0 + jnp.exp(-x))


def _log_sigmoid(x):
    return -(jnp.maximum(-x, 0.0) + jnp.log1p(jnp.exp(-jnp.abs(x))))


def _const_spec(shape):
    return pl.BlockSpec(shape, lambda *_: (0,) * len(shape), pipeline_mode=pl.Buffered(1))


def _inproj_kernel(x_ref, g1_ref, wt_ref, gb_ref, cw_ref,
                   q_ref, kt_ref, v_ref, o_ref, gatest_ref, u_ref,
                   ext_ref, wm_ref, wg_ref, wu_ref, *, tiles_per_seq, d_mlstm, k_scale):
    tm = x_ref.shape[0]
    dm = d_mlstm
    H = N_MLSTM_HEADS
    halo = SUBLANES

    @pl.when(pl.program_id(0) == 0)
    def _():
        for c0 in range(0, 4 * dm, dm):
            wm_ref[:, c0:c0 + dm] = wt_ref[c0:c0 + dm, :].T.astype(BF16)
        wu_ref[...] = wt_ref[4 * dm + 2 * H:, :].T.astype(BF16)
        fill = jnp.zeros((LANES - 2 * H, wt_ref.shape[1]), F32)
        wg_ref[...] = jnp.concatenate([wt_ref[4 * dm:4 * dm + 2 * H, :], fill], axis=0).T.astype(BF16)

    xn = _rmsnorm(x_ref[...], g1_ref[...]).astype(BF16)

    @pl.when(pl.program_id(0) % tiles_per_seq == 0)
    def _():
        ext_ref[0:halo, :] = jnp.zeros((halo, 2 * dm), F32)

    ext_ref[halo:halo + tm, :] = jnp.dot(xn, wm_ref[:, 0:2 * dm], preferred_element_type=F32)
    v = jnp.dot(xn, wm_ref[:, 2 * dm:3 * dm], preferred_element_type=F32)
    o = jnp.dot(xn, wm_ref[:, 3 * dm:4 * dm], preferred_element_type=F32)
    u = jnp.dot(xn, wu_ref[...], preferred_element_type=F32)
    gates = jnp.dot(xn, wg_ref[...], preferred_element_type=F32) + gb_ref[...]
    v_ref[...] = v.astype(BF16)
    o_ref[...] = o.astype(BF16)
    u_ref[...] = u
    gatest_ref[...] = gates.T[0:2 * H, :]

    def zero_of(t):
        w = pltpu.bitcast(t[tm - SUBLANES:tm, t.shape[1] - LANES:], jnp.uint32)
        return pltpu.bitcast((w >> 16) >> 16, F32)
    anchor = zero_of(v)

    acc = cw_ref[CONV_WIDTH - 1:CONV_WIDTH, :] * ext_ref[halo:halo + tm, :]
    for j in range(1, CONV_WIDTH):
        acc = acc + cw_ref[CONV_WIDTH - 1 - j:CONV_WIDTH - j, :] * ext_ref[halo - j:halo - j + tm, :]
    ext_ref[0:halo, :] = ext_ref[tm:tm + halo, :]
    qk = acc * _sigmoid(acc) + jnp.tile(anchor, (tm // SUBLANES, 2 * dm // LANES))
    q_ref[...] = qk[:, 0:dm].astype(BF16)
    kt_ref[...] = (qk[:, dm:2 * dm] * k_scale).T.astype(BF16)


def _inproj(x2, g1, w_in_t, gb, cw, *, seq, tm, d_mlstm):
    T, D = x2.shape
    dm = d_mlstm
    H = N_MLSTM_HEADS
    assert 2 * H == SUBLANES
    ds5 = w_in_t.shape[0] - 4 * dm - 2 * H
    kern = functools.partial(_inproj_kernel, tiles_per_seq=seq // tm, d_mlstm=dm,
                             k_scale=float((dm // H) ** -0.5))
    tok = lambda w: pl.BlockSpec((tm, w), lambda i: (i, 0))
    return pl.pallas_call(
        kern,
        grid=(T // tm,),
        in_specs=[tok(D), _const_spec((1, D)), _const_spec(w_in_t.shape), _const_spec((1, LANES)),
                  _const_spec(cw.shape)],
        out_specs=[tok(dm), pl.BlockSpec((dm, tm), lambda i: (0, i)), tok(dm), tok(dm),
                   pl.BlockSpec((2 * H, tm), lambda i: (0, i)), tok(ds5)],
        out_shape=[jax.ShapeDtypeStruct((T, dm), BF16), jax.ShapeDtypeStruct((dm, T), BF16),
                   jax.ShapeDtypeStruct((T, dm), BF16), jax.ShapeDtypeStruct((T, dm), BF16),
                   jax.ShapeDtypeStruct((2 * H, T), F32),
                   jax.ShapeDtypeStruct((T, ds5), F32)],
        scratch_shapes=[pltpu.VMEM((tm + 2 * SUBLANES, 2 * dm), F32), pltpu.VMEM((D, 4 * dm), BF16),
                        pltpu.VMEM((D, LANES), BF16), pltpu.VMEM((D, ds5), BF16)],
        compiler_params=pltpu.CompilerParams(dimension_semantics=("arbitrary",),
                                             vmem_limit_bytes=VMEM_LIMIT),
        name="inproj",
    )(x2, g1, w_in_t, gb, cw)


def _lane_scan(x, op, identity):
    n = x.shape[-1]
    pos = lax.broadcasted_iota(jnp.int32, x.shape, x.ndim - 1)
    sh = 1
    while sh < n:
        x = op(x, jnp.where(pos >= sh, pltpu.roll(x, sh, x.ndim - 1), identity))
        sh *= 2
    return x


def _mlstm_kernel(q_ref, kt_ref, v_ref, o_ref, gatest_ref, gn_ref, *refs, chunk, n_cast):
    S = q_ref.shape[0]
    H = N_MLSTM_HEADS
    dh = q_ref.shape[1] // H
    L = chunk
    assert L == dh == LANES
    w32_refs, out_ref, w16_refs = refs[0:n_cast], refs[n_cast], refs[n_cast + 1:2 * n_cast + 1]
    a_ref, g0_ref, nm_ref, gend_ref = refs[2 * n_cast + 1:2 * n_cast + 5]
    c_refs, nd_refs = refs[2 * n_cast + 5:2 * n_cast + 5 + H], refs[2 * n_cast + 5 + H:]

    for w32_ref, w16_ref in zip(w32_refs, w16_refs):
        w16_ref[...] = w32_ref[...].astype(BF16)

    gates = gatest_ref[...]
    f_cum = pltpu.roll(_lane_scan(_log_sigmoid(gates), jnp.add, 0.0), H, 0)
    a = gates - f_cum
    g0 = jnp.maximum(_lane_scan(a, jnp.maximum, -jnp.inf), 0.0)
    a_ref[...] = a
    g0_ref[...] = g0
    nm_ref[...] = -(f_cum + g0)
    for c in range(S // L):
        gend_ref[:, c * L:(c + 1) * L] = jnp.broadcast_to(g0[:, (c + 1) * L - 1:(c + 1) * L], (SUBLANES, L))

    for c_ref in c_refs:
        c_ref[...] = jnp.zeros(c_ref.shape, F32)
    causal = (lax.broadcasted_iota(jnp.int32, (L, L), 0) >= lax.broadcasted_iota(jnp.int32, (L, L), 1))
    ones_blk = jnp.ones((L, dh), BF16)
    mean_mat = jnp.full((dh, dh), 1.0 / dh, BF16)

    def mix(c, h, g_prev):
        r0 = pl.multiple_of(c * L, L)
        hs = slice(h * dh, (h + 1) * dh)
        qc = q_ref[pl.ds(r0, L), hs]
        ktc = kt_ref[hs, pl.ds(r0, L)]
        v_aug = jnp.concatenate([v_ref[pl.ds(r0, L), hs], ones_blk], axis=1)
        a_row = a_ref[h:h + 1, pl.ds(r0, L)]
        g_end = gend_ref[h:h + 1, pl.ds(r0, L)]
        g0_t = jnp.broadcast_to(g0_ref[h:h + 1, pl.ds(r0, L)], (L, L)).T

        s_qk = jnp.dot(qc, ktc, preferred_element_type=F32)
        p = (jnp.exp(jnp.where(causal, a_row - g0_t, -jnp.inf)) * s_qk).astype(BF16)
        c_prev = c_refs[h][...]
        qc_prev = jnp.dot(qc, c_prev.astype(BF16), preferred_element_type=F32)
        pv = jnp.dot(p, v_aug, preferred_element_type=F32)
        inter_w = jnp.exp(g_prev - g0_t)
        nd_refs[h][...] = jnp.concatenate([inter_w, inter_w], axis=1) * qc_prev + pv

        kte = (ktc.astype(F32) * jnp.exp(a_row - g_end)).astype(BF16)
        decay = jnp.exp(g_prev - g_end)
        c_refs[h][...] = (jnp.concatenate([decay, decay], axis=1) * c_prev
                          + jnp.dot(kte, v_aug, preferred_element_type=F32))
        return g_end

    def emit(c, h):
        r0 = pl.multiple_of(c * L, L)
        hs = slice(h * dh, (h + 1) * dh)
        nm_t = jnp.broadcast_to(nm_ref[h:h + 1, pl.ds(r0, L)], (L, L)).T
        nd = nd_refs[h][...]
        h_tilde = nd[:, 0:dh] / jnp.maximum(jnp.abs(nd[:, dh:2 * dh]), jnp.exp(nm_t))
        hm = _sigmoid(o_ref[pl.ds(r0, L), hs].astype(F32)) * h_tilde
        mu = jnp.dot(hm.astype(BF16), mean_mat, preferred_element_type=F32)
        hc = hm - mu
        var = jnp.dot((hc * hc).astype(BF16), mean_mat, preferred_element_type=F32)
        out_ref[pl.ds(r0, L), hs] = (hc * lax.rsqrt(var + EPS) * gn_ref[:, hs]).astype(BF16)

    zero = jnp.zeros((1, L), F32)
    g_first = tuple(mix(0, h, zero) for h in range(H))

    def body(c, g_carry):
        for h in range(H):
            emit(c - 1, h)
        return tuple(mix(c, h, g_carry[h]) for h in range(H))

    lax.fori_loop(1, S // L, body, g_first, unroll=3)
    for h in range(H):
        emit(S // L - 1, h)


def _mlstm(q, kt, v, o, gatest, gn, weights, *, batch, seq):
    T, dm = q.shape
    dh = dm // N_MLSTM_HEADS
    kern = functools.partial(_mlstm_kernel, chunk=MLSTM_CHUNK, n_cast=len(weights))
    tok = lambda w: pl.BlockSpec((seq, w), lambda b: (b, 0))
    rows = pltpu.VMEM((SUBLANES, seq), F32)
    bf16_rows = 2 * SUBLANES
    assert all(w.shape[0] % (batch * bf16_rows) == 0 for w in weights)
    w_specs = [pl.BlockSpec((w.shape[0] // batch, w.shape[1]), lambda b: (b, 0)) for w in weights]
    return pl.pallas_call(
        kern,
        grid=(batch,),
        in_specs=[tok(dm), pl.BlockSpec((dm, seq), lambda b: (0, b)), tok(dm), tok(dm),
                  pl.BlockSpec((SUBLANES, seq), lambda b: (0, b)), _const_spec((1, dm))] + w_specs,
        out_specs=[tok(dm)] + w_specs,
        out_shape=[jax.ShapeDtypeStruct((T, dm), BF16)] + [jax.ShapeDtypeStruct(w.shape, BF16) for w in weights],
        scratch_shapes=([rows, rows, rows, rows] + [pltpu.VMEM((dh, 2 * dh), F32)] * N_MLSTM_HEADS
                        + [pltpu.VMEM((MLSTM_CHUNK, 2 * dh), F32)] * N_MLSTM_HEADS),
        compiler_params=pltpu.CompilerParams(dimension_semantics=("arbitrary",),
                                             vmem_limit_bytes=VMEM_LIMIT),
        name="mlstm",
    )(q, kt, v, o, gatest, gn, *weights)


def _s5_prep_kernel(arow_ref, acol_ref, ldt_ref, bre_ref, bim_ref, cre_ref, cim_ref, dcol_ref,
                    rep_ref, selrev_ref, pw_ref, pwcol_ref, w1_ref, wout_ref, al_ref, *, chunk):
    L = chunk
    P = S5_STATE
    Hc = S5_GROUP
    R = L * Hc
    hi = lax.Precision.HIGHEST
    dt = jnp.exp(ldt_ref[...])

    def select(x, sel_ref):
        x_hi = x.astype(BF16)
        x_lo = (x - x_hi.astype(F32)).astype(BF16)
        sel = sel_ref[...]
        return (jnp.dot(x_hi, sel, preferred_element_type=F32) + jnp.dot(x_lo, sel, preferred_element_type=F32))

    def abar_pow(a_re, a_im, tau):
        mag = jnp.exp(tau * (dt * a_re))
        ang = tau * (dt * a_im)
        return mag * jnp.cos(ang), mag * jnp.sin(ang)

    a_re_c = acol_ref[:, 0:1]
    a_im_c = acol_ref[:, 1:2]
    ec_re, ec_im = abar_pow(a_re_c, a_im_c, pw_ref[...])

    ab_re = ec_re[:, 1:2]
    ab_im = ec_im[:, 1:2]
    den = a_re_c * a_re_c + a_im_c * a_im_c
    nr = ab_re - 1.0
    z_re = (nr * a_re_c + ab_im * a_im_c) / den
    z_im = (ab_im * a_re_c - nr * a_im_c) / den
    bb_re = z_re * bre_ref[...] - z_im * bim_ref[...]
    bb_im = z_re * bim_ref[...] + z_im * bre_ref[...]

    n_pow = -(-(L + 1) // SUBLANES) * SUBLANES
    tau_r = lax.broadcasted_iota(jnp.int32, (n_pow, P), 0).astype(F32)
    er_re, er_im = abar_pow(arow_ref[0:1, :], arow_ref[1:2, :], tau_r)

    def rows(e, first):
        return jnp.concatenate([jnp.broadcast_to(e[first + t:first + t + 1, :], (Hc, P)) for t in range(L)], axis=0)

    c_re = jnp.concatenate([cre_ref[...]] * L, axis=0)
    c_im = jnp.concatenate([cim_ref[...]] * L, axis=0)

    def c_times_pow(first):
        e_re, e_im = rows(er_re, first), rows(er_im, first)
        return c_re * e_re - c_im * e_im, c_re * e_im + c_im * e_re

    ce_re, ce_im = c_times_pow(0)
    kflat = (jnp.dot(ce_re, bb_re, preferred_element_type=F32, precision=hi)
             - jnp.dot(ce_im, bb_im, preferred_element_type=F32, precision=hi))
    row = lax.broadcasted_iota(jnp.int32, (R, Hc), 0)
    col = lax.broadcasted_iota(jnp.int32, (R, Hc), 1)
    kflat = kflat + jnp.where(row == col, jnp.concatenate([dcol_ref[...]] * L, axis=0), 0.0)

    m = select(kflat, rep_ref)
    blk = lax.broadcasted_iota(jnp.int32, (R, R), 1) // Hc
    sh = 1
    while sh < L:
        shifted = jnp.concatenate([jnp.zeros((sh * Hc, R), F32), m[0:R - sh * Hc, :]], axis=0)
        m = jnp.where((blk & sh) != 0, shifted, m)
        sh *= 2
    w1_ref[0:R, :] = m.astype(BF16)

    ev_re, ev_im = select(ec_re, selrev_ref), select(ec_im, selrev_ref)
    bt_re, bt_im = select(bb_re, rep_ref), select(bb_im, rep_ref)
    w1_ref[R:R + P, :] = (ev_re * bt_re - ev_im * bt_im).astype(BF16)
    w1_ref[R + P:R + 2 * P, :] = (ev_re * bt_im + ev_im * bt_re).astype(BF16)

    co_re, co_im = c_times_pow(1)
    zero = jnp.zeros((R, P), F32)
    odd = pl.program_id(0) % 2 == 1
    halves = lambda v: jnp.where(odd, jnp.concatenate([zero, v], axis=1), jnp.concatenate([v, zero], axis=1))
    wout_ref[:, 0:2 * P] = halves(co_re).astype(BF16)
    wout_ref[:, 2 * P:4 * P] = halves(-co_im).astype(BF16)

    sc_re, sc_im = abar_pow(arow_ref[0:1, :], arow_ref[1:2, :], pwcol_ref[...])
    al_ref[0:SUBLANES, :] = sc_re
    al_ref[SUBLANES:2 * SUBLANES, :] = sc_im


def _s5_prep(a_re, a_im, log_dt, b_re, b_im, c_re, c_im, d_skip, *, chunk, n_chunks):
    G, P = a_re.shape
    Hc = S5_GROUP
    R = chunk * Hc
    arow = jnp.stack([a_re, a_im], axis=1)
    acol = jnp.stack([a_re, a_im], axis=2)
    ldt = log_dt.reshape(G, 1, 1)
    dcol = d_skip.reshape(G, Hc, 1)
    rep = (jnp.arange(R)[None, :] % Hc == jnp.arange(Hc)[:, None]).astype(BF16)
    selrev = (jnp.arange(LANES)[:, None] == chunk - 1 - jnp.arange(R)[None, :] // Hc).astype(BF16)
    n_steps = max(1, (n_chunks - 1).bit_length())
    assert chunk <= LANES and n_steps <= SUBLANES
    pw = jnp.zeros((1, LANES), F32).at[0, :chunk].set(jnp.arange(chunk, dtype=F32))
    pwcol = jnp.zeros((SUBLANES, 1), F32).at[:n_steps, 0].set(chunk * 2.0 ** jnp.arange(n_steps))
    kern = functools.partial(_s5_prep_kernel, chunk=chunk)
    grp = lambda *s: pl.BlockSpec((None,) + s, lambda g: (g,) + (0,) * len(s))
    return pl.pallas_call(
        kern,
        grid=(G,),
        in_specs=[grp(2, P), grp(P, 2), grp(1, 1), grp(P, Hc), grp(P, Hc), grp(Hc, P), grp(Hc, P),
                  grp(Hc, 1), _const_spec((Hc, R)), _const_spec((LANES, R)), _const_spec((1, LANES)),
                  _const_spec((SUBLANES, 1))],
        out_specs=[grp(R + 2 * P, R), grp(R, 4 * P), grp(2 * SUBLANES, P)],
        out_shape=[jax.ShapeDtypeStruct((G, R + 2 * P, R), BF16),
                   jax.ShapeDtypeStruct((G, R, 4 * P), BF16),
                   jax.ShapeDtypeStruct((G, 2 * SUBLANES, P), F32)],
        compiler_params=pltpu.CompilerParams(dimension_semantics=("arbitrary",),
                                             vmem_limit_bytes=VMEM_LIMIT),
        name="s5_prep",
    )(arow, acol, ldt, b_re, b_im, c_re, c_im, dcol, rep, selrev, pw, pwcol)


def _s5_kernel(*refs, chunk, n_chunks):
    L = chunk
    u_refs, (w1_ref, wout_ref, al_ref, y_ref, ut_ref, yt_ref) = refs[0:L], refs[L:]
    P = S5_STATE
    Hc = S5_GROUP
    R = L * Hc
    C = u_refs[0].shape[0]
    n_grp = LANES // Hc

    for s in range(L):
        xt = u_refs[s][...].astype(BF16).T
        for j in range(n_grp):
            ut_ref[j, s * Hc:(s + 1) * Hc, :] = xt[j * Hc:(j + 1) * Hc, :]

    seg_pos = lax.broadcasted_iota(jnp.int32, (C, 2 * P), 0) & (n_chunks - 1)
    nt = (((1,), (1,)), ((), ()))

    def shifted(x, sh):
        return jnp.where(seg_pos >= sh, pltpu.roll(x, sh, 0), 0.0)

    def group_pair(i, carry):
        j0 = 2 * i
        res = [jnp.dot(w1_ref[j0 + d], ut_ref[j0 + d], preferred_element_type=F32) for d in range(2)]
        x_re = jnp.concatenate([res[0][R:R + P], res[1][R:R + P]], axis=0).T
        x_im = jnp.concatenate([res[0][R + P:R + 2 * P], res[1][R + P:R + 2 * P]], axis=0).T
        sh, k = 1, 0
        while sh < n_chunks:
            a_re = jnp.concatenate([al_ref[j0, k:k + 1, :], al_ref[j0 + 1, k:k + 1, :]], axis=1)
            a_im = jnp.concatenate([al_ref[j0, SUBLANES + k:SUBLANES + k + 1, :],
                                    al_ref[j0 + 1, SUBLANES + k:SUBLANES + k + 1, :]], axis=1)
            s_re, s_im = shifted(x_re, sh), shifted(x_im, sh)
            x_re, x_im = (x_re + a_re * s_re - a_im * s_im, x_im + a_re * s_im + a_im * s_re)
            sh *= 2
            k += 1
        xprev = jnp.concatenate([shifted(x_re, 1), shifted(x_im, 1)], axis=1).astype(BF16)
        for d in range(2):
            yt_ref[j0 + d] = res[d][0:R, :] + lax.dot_general(wout_ref[j0 + d], xprev, nt,
                                                              preferred_element_type=F32)
        return carry

    lax.fori_loop(0, n_grp // 2, group_pair, 0)

    for t in range(L):
        zt = jnp.concatenate([yt_ref[j, t * Hc:(t + 1) * Hc, :] for j in range(n_grp)], axis=0)
        y_ref[pl.ds(t, C, stride=L), :] = zt.T


def _s5(u, w1, wout, al, *, chunk, n_chunks, n_col_blocks):
    T, ds5 = u.shape
    G, R1, R = w1.shape
    P4 = wout.shape[-1]
    n_grp = LANES // S5_GROUP
    n_q = ds5 // LANES
    tb = T // n_col_blocks
    C = tb // chunk
    kern = functools.partial(_s5_kernel, chunk=chunk, n_chunks=n_chunks)
    grp = lambda *s: pl.BlockSpec((n_grp,) + s, lambda q, i: (q,) + (0,) * len(s))
    slabs = [pl.BlockSpec((C, LANES), functools.partial(lambda q, i, s: (i, s * n_q + q), s=s)) for s in range(chunk)]
    u_cols = u.reshape(T // chunk, chunk * ds5)
    return pl.pallas_call(
        kern,
        grid=(n_q, n_col_blocks),
        in_specs=slabs + [grp(R1, R), grp(R, P4), grp(2 * SUBLANES, P4 // 4)],
        out_specs=pl.BlockSpec((tb, LANES), lambda q, i: (i, q)),
        out_shape=jax.ShapeDtypeStruct((T, ds5), F32),
        scratch_shapes=[pltpu.VMEM((n_grp, R, C), BF16), pltpu.VMEM((n_grp, R, C), F32)],
        compiler_params=pltpu.CompilerParams(dimension_semantics=("arbitrary", "arbitrary"),
                                             vmem_limit_bytes=VMEM_LIMIT),
        name="s5",
    )(*([u_cols] * chunk), w1, wout, al)


def _out_kernel(x_ref, hm_ref, y_ref, wglu_ref, bglu_ref, gs5_ref, wo_ref, g2_ref,
                wgate_ref, wup_ref, wdown_ref, gf_ref, out_ref, acc_ref, xn_ref, *, ff_chunk, final_norm):
    dm = hm_ref.shape[1]
    g = jax.nn.gelu(y_ref[...])
    z = jnp.dot(g.astype(BF16), wglu_ref[...], preferred_element_type=F32) + bglu_ref[...]
    hs = _rmsnorm(g * _sigmoid(z), gs5_ref[...]).astype(BF16)
    h1 = (x_ref[...] + jnp.dot(hm_ref[...], wo_ref[0:dm, :], preferred_element_type=F32)
          + jnp.dot(hs, wo_ref[dm:, :], preferred_element_type=F32))
    xn_ref[...] = _rmsnorm(h1, g2_ref[...]).astype(BF16)
    acc_ref[...] = h1

    def ffn_chunk(c, carry):
        cs = pl.ds(pl.multiple_of(c * ff_chunk, ff_chunk), ff_chunk)
        xn2 = xn_ref[...]
        gate = jnp.dot(xn2, wgate_ref[:, cs], preferred_element_type=F32)
        up = jnp.dot(xn2, wup_ref[:, cs], preferred_element_type=F32)
        act = (gate * _sigmoid(gate) * up).astype(BF16)
        acc_ref[...] += jnp.dot(act, wdown_ref[cs, :], preferred_element_type=F32)
        return carry

    lax.fori_loop(0, wgate_ref.shape[1] // ff_chunk, ffn_chunk, 0, unroll=4)
    out_ref[...] = _rmsnorm(acc_ref[...], gf_ref[...]) if final_norm else acc_ref[...]


def _out_block(x2, hm, y, wglu, bglu, gs5, wo, g2, wgate, wup, wdown, gf, *, tm, ff_chunk, final_norm):
    T, D = x2.shape
    dm = hm.shape[1]
    ds5 = y.shape[1]
    assert wgate.shape[1] % ff_chunk == 0
    kern = functools.partial(_out_kernel, ff_chunk=ff_chunk, final_norm=final_norm)
    tok = lambda w: pl.BlockSpec((tm, w), lambda i: (i, 0))
    return pl.pallas_call(
        kern,
        grid=(T // tm,),
        in_specs=[tok(D), tok(dm), tok(ds5), _const_spec(wglu.shape), _const_spec((1, ds5)),
                  _const_spec((1, ds5)), _const_spec(wo.shape), _const_spec((1, D)),
                  _const_spec(wgate.shape), _const_spec(wup.shape), _const_spec(wdown.shape),
                  _const_spec((1, D))],
        out_specs=tok(D),
        out_shape=jax.ShapeDtypeStruct((T, D), F32),
        scratch_shapes=[pltpu.VMEM((tm, D), F32), pltpu.VMEM((tm, D), BF16)],
        compiler_params=pltpu.CompilerParams(dimension_semantics=("arbitrary",),
                                             vmem_limit_bytes=56 * 1024 * 1024),
        name="out_block",
    )(x2, hm, y, wglu, bglu, gs5, wo, g2, wgate, wup, wdown, gf)


def kernel(x, norm1_g, w_in, if_bias, conv_qk, mlstm_norm_g, a_re, a_im, log_dt, b_re, b_im, c_re, c_im,
           d_skip, w_glu, b_glu, s5_norm_g, w_out, norm2_g, w_gate, w_up, w_down, norm_f_g):
    B, S, D = x.shape
    depth = w_in.shape[0]
    H = N_MLSTM_HEADS
    dm = mlstm_norm_g.shape[1]
    T = B * S
    L5 = S5_CHUNK
    nc5 = S // L5
    row = lambda a: a.reshape(1, -1).astype(F32)

    h = x.reshape(T, D)
    for l in range(depth):
        last = l == depth - 1
        gb = jnp.zeros((1, LANES), F32).at[0, 0:2 * H].set(if_bias[l])

        q, kt, v, o, gatest, u = _inproj(h, row(norm1_g[l]), w_in[l].T.astype(F32), gb, conv_qk[l].astype(F32),
                                         seq=S, tm=512, d_mlstm=dm)
        hm, wglu, wo, wgate, wup, wdown = _mlstm(
            q, kt, v, o, gatest, row(mlstm_norm_g[l]),
            [w.astype(F32) for w in (w_glu[l], w_out[l], w_gate[l], w_up[l], w_down[l])], batch=B, seq=S)

        w1, wout, al = _s5_prep(a_re[l], a_im[l], log_dt[l], b_re[l], b_im[l], c_re[l], c_im[l],
                                d_skip[l], chunk=L5, n_chunks=nc5)
        y = _s5(u, w1, wout, al, chunk=L5, n_chunks=nc5, n_col_blocks=2)

        h = _out_block(h, hm, y, wglu, row(b_glu[l]), row(s5_norm_g[l]), wo, row(norm2_g[l]),
                       wgate, wup, wdown, row(norm_f_g), tm=512, ff_chunk=256, final_norm=last)
    return h.reshape(B, S, D)
```

```python
import functools

import jax
import jax.numpy as jnp
from jax import lax
from jax.experimental import pallas as pl
from jax.experimental.pallas import tpu as pltpu

EPS = 1e-6
N_MLSTM_HEADS = 4
CONV_WIDTH = 4
S5_GROUP = 16
S5_STATE = 64

LANES = 128
SUBLANES = 8
MLSTM_CHUNK = 128
S5_CHUNK = 16
VMEM_LIMIT = 48 * 1024 * 1024

F32 = jnp.float32
BF16 = jnp.bfloat16


def _rmsnorm(x, g):
    ms = jnp.mean(x * x, axis=-1, keepdims=True)
    return x * lax.rsqrt(ms + EPS) * g


def _sigmoid(x):
    return 1.0 / (1.0 + jnp.exp(-x))


def _log_sigmoid(x):
    return -(jnp.maximum(-x, 0.0) + jnp.log1p(jnp.exp(-jnp.abs(x))))


def _const_spec(shape):
    return pl.BlockSpec(shape, lambda *_: (0,) * len(shape), pipeline_mode=pl.Buffered(1))


def _inproj_kernel(x_ref, g1_ref, wt_ref, gb_ref, cw_ref,
                   q_ref, kt_ref, v_ref, o_ref, gatest_ref, u_ref,
                   ext_ref, wm_ref, wg_ref, wu_ref, *, tiles_per_seq, d_mlstm, k_scale):
    tm = x_ref.shape[0]
    dm = d_mlstm
    H = N_MLSTM_HEADS
    halo = SUBLANES

    @pl.when(pl.program_id(0) == 0)
    def _():
        for c0 in range(0, 4 * dm, dm):
            wm_ref[:, c0:c0 + dm] = wt_ref[c0:c0 + dm, :].T.astype(BF16)
        wu_ref[...] = wt_ref[4 * dm + 2 * H:, :].T.astype(BF16)
        fill = jnp.zeros((LANES - 2 * H, wt_ref.shape[1]), F32)
        wg_ref[...] = jnp.concatenate([wt_ref[4 * dm:4 * dm + 2 * H, :], fill], axis=0).T.astype(BF16)

    xn = _rmsnorm(x_ref[...], g1_ref[...]).astype(BF16)

    @pl.when(pl.program_id(0) % tiles_per_seq == 0)
    def _():
        ext_ref[0:halo, :] = jnp.zeros((halo, 2 * dm), F32)

    ext_ref[halo:halo + tm, :] = jnp.dot(xn, wm_ref[:, 0:2 * dm], preferred_element_type=F32)
    v = jnp.dot(xn, wm_ref[:, 2 * dm:3 * dm], preferred_element_type=F32)
    o = jnp.dot(xn, wm_ref[:, 3 * dm:4 * dm], preferred_element_type=F32)
    u = jnp.dot(xn, wu_ref[...], preferred_element_type=F32)
    gates = jnp.dot(xn, wg_ref[...], preferred_element_type=F32) + gb_ref[...]
    v_ref[...] = v.astype(BF16)
    o_ref[...] = o.astype(BF16)
    n_oct = u_ref.shape[0]
    u4 = u.reshape(tm // (n_oct * SUBLANES), n_oct, SUBLANES, u.shape[1])
    for oc in range(n_oct):
        u_ref[oc] = u4[:, oc].reshape(tm // n_oct, u.shape[1])
    gatest_ref[...] = gates.T[0:2 * H, :]

    def zero_of(t):
        w = pltpu.bitcast(t[tm - SUBLANES:tm, t.shape[1] - LANES:], jnp.uint32)
        return pltpu.bitcast((w >> 16) >> 16, F32)
    anchor = zero_of(v)

    acc = cw_ref[CONV_WIDTH - 1:CONV_WIDTH, :] * ext_ref[halo:halo + tm, :]
    for j in range(1, CONV_WIDTH):
        acc = acc + cw_ref[CONV_WIDTH - 1 - j:CONV_WIDTH - j, :] * ext_ref[halo - j:halo - j + tm, :]
    ext_ref[0:halo, :] = ext_ref[tm:tm + halo, :]
    qk = acc * _sigmoid(acc) + jnp.tile(anchor, (tm // SUBLANES, 2 * dm // LANES))
    q_ref[...] = qk[:, 0:dm].astype(BF16)
    kt_ref[...] = (qk[:, dm:2 * dm] * k_scale).T.astype(BF16)


def _inproj(x2, g1, w_in_t, gb, cw, *, seq, tm, d_mlstm, n_oct):
    T, D = x2.shape
    dm = d_mlstm
    H = N_MLSTM_HEADS
    assert 2 * H == SUBLANES
    ds5 = w_in_t.shape[0] - 4 * dm - 2 * H
    kern = functools.partial(_inproj_kernel, tiles_per_seq=seq // tm, d_mlstm=dm,
                             k_scale=float((dm // H) ** -0.5))
    tok = lambda w: pl.BlockSpec((tm, w), lambda i: (i, 0))
    return pl.pallas_call(
        kern,
        grid=(T // tm,),
        in_specs=[tok(D), _const_spec((1, D)), _const_spec(w_in_t.shape), _const_spec((1, LANES)),
                  _const_spec(cw.shape)],
        out_specs=[tok(dm), pl.BlockSpec((dm, tm), lambda i: (0, i)), tok(dm), tok(dm),
                   pl.BlockSpec((2 * H, tm), lambda i: (0, i)),
                   pl.BlockSpec((n_oct, tm // n_oct, ds5), lambda i: (0, i, 0))],
        out_shape=[jax.ShapeDtypeStruct((T, dm), BF16), jax.ShapeDtypeStruct((dm, T), BF16),
                   jax.ShapeDtypeStruct((T, dm), BF16), jax.ShapeDtypeStruct((T, dm), BF16),
                   jax.ShapeDtypeStruct((2 * H, T), F32),
                   jax.ShapeDtypeStruct((n_oct, T // n_oct, ds5), F32)],
        scratch_shapes=[pltpu.VMEM((tm + 2 * SUBLANES, 2 * dm), F32), pltpu.VMEM((D, 4 * dm), BF16),
                        pltpu.VMEM((D, LANES), BF16), pltpu.VMEM((D, ds5), BF16)],
        compiler_params=pltpu.CompilerParams(dimension_semantics=("arbitrary",),
                                             vmem_limit_bytes=VMEM_LIMIT),
        name="inproj",
    )(x2, g1, w_in_t, gb, cw)


def _lane_scan(x, op, identity):
    n = x.shape[-1]
    pos = lax.broadcasted_iota(jnp.int32, x.shape, x.ndim - 1)
    sh = 1
    while sh < n:
        x = op(x, jnp.where(pos >= sh, pltpu.roll(x, sh, x.ndim - 1), identity))
        sh *= 2
    return x


def _mlstm_kernel(q_ref, kt_ref, v_ref, o_ref, gatest_ref, gn_ref, *refs, chunk, n_cast):
    S = q_ref.shape[0]
    H = N_MLSTM_HEADS
    dh = q_ref.shape[1] // H
    L = chunk
    assert L == dh == LANES
    w32_refs, out_ref, w16_refs = refs[0:n_cast], refs[n_cast], refs[n_cast + 1:2 * n_cast + 1]
    a_ref, g0_ref, nm_ref, gend_ref = refs[2 * n_cast + 1:2 * n_cast + 5]
    c_refs, nd_refs = refs[2 * n_cast + 5:2 * n_cast + 5 + H], refs[2 * n_cast + 5 + H:]

    for w32_ref, w16_ref in zip(w32_refs, w16_refs):
        w16_ref[...] = w32_ref[...].astype(BF16)

    gates = gatest_ref[...]
    f_cum = pltpu.roll(_lane_scan(_log_sigmoid(gates), jnp.add, 0.0), H, 0)
    a = gates - f_cum
    g0 = jnp.maximum(_lane_scan(a, jnp.maximum, -jnp.inf), 0.0)
    a_ref[...] = a
    g0_ref[...] = g0
    nm_ref[...] = -(f_cum + g0)
    for c in range(S // L):
        gend_ref[:, c * L:(c + 1) * L] = jnp.broadcast_to(g0[:, (c + 1) * L - 1:(c + 1) * L], (SUBLANES, L))

    for c_ref in c_refs:
        c_ref[...] = jnp.zeros(c_ref.shape, F32)
    causal = (lax.broadcasted_iota(jnp.int32, (L, L), 0) >= lax.broadcasted_iota(jnp.int32, (L, L), 1))
    ones_blk = jnp.ones((L, dh), BF16)
    mean_mat = jnp.full((dh, dh), 1.0 / dh, BF16)

    def mix(c, h, g_prev):
        r0 = pl.multiple_of(c * L, L)
        hs = slice(h * dh, (h + 1) * dh)
        qc = q_ref[pl.ds(r0, L), hs]
        ktc = kt_ref[hs, pl.ds(r0, L)]
        v_aug = jnp.concatenate([v_ref[pl.ds(r0, L), hs], ones_blk], axis=1)
        a_row = a_ref[h:h + 1, pl.ds(r0, L)]
        g_end = gend_ref[h:h + 1, pl.ds(r0, L)]
        g0_t = jnp.broadcast_to(g0_ref[h:h + 1, pl.ds(r0, L)], (L, L)).T

        s_qk = jnp.dot(qc, ktc, preferred_element_type=F32)
        p = (jnp.exp(jnp.where(causal, a_row - g0_t, -jnp.inf)) * s_qk).astype(BF16)
        c_prev = c_refs[h][...]
        qc_prev = jnp.dot(qc, c_prev.astype(BF16), preferred_element_type=F32)
        pv = jnp.dot(p, v_aug, preferred_element_type=F32)
        inter_w = jnp.exp(g_prev - g0_t)
        nd_refs[h][...] = jnp.concatenate([inter_w, inter_w], axis=1) * qc_prev + pv

        kte = (ktc.astype(F32) * jnp.exp(a_row - g_end)).astype(BF16)
        decay = jnp.exp(g_prev - g_end)
        c_refs[h][...] = (jnp.concatenate([decay, decay], axis=1) * c_prev
                          + jnp.dot(kte, v_aug, preferred_element_type=F32))
        return g_end

    def emit(c, h):
        r0 = pl.multiple_of(c * L, L)
        hs = slice(h * dh, (h + 1) * dh)
        nm_t = jnp.broadcast_to(nm_ref[h:h + 1, pl.ds(r0, L)], (L, L)).T
        nd = nd_refs[h][...]
        h_tilde = nd[:, 0:dh] / jnp.maximum(jnp.abs(nd[:, dh:2 * dh]), jnp.exp(nm_t))
        hm = _sigmoid(o_ref[pl.ds(r0, L), hs].astype(F32)) * h_tilde
        mu = jnp.dot(hm.astype(BF16), mean_mat, preferred_element_type=F32)
        hc = hm - mu
        var = jnp.dot((hc * hc).astype(BF16), mean_mat, preferred_element_type=F32)
        out_ref[pl.ds(r0, L), hs] = (hc * lax.rsqrt(var + EPS) * gn_ref[:, hs]).astype(BF16)

    zero = jnp.zeros((1, L), F32)
    g_first = tuple(mix(0, h, zero) for h in range(H))

    def body(c, g_carry):
        for h in range(H):
            emit(c - 1, h)
        return tuple(mix(c, h, g_carry[h]) for h in range(H))

    lax.fori_loop(1, S // L, body, g_first, unroll=3)
    for h in range(H):
        emit(S // L - 1, h)


def _mlstm(q, kt, v, o, gatest, gn, weights, *, batch, seq):
    T, dm = q.shape
    dh = dm // N_MLSTM_HEADS
    kern = functools.partial(_mlstm_kernel, chunk=MLSTM_CHUNK, n_cast=len(weights))
    tok = lambda w: pl.BlockSpec((seq, w), lambda b: (b, 0))
    rows = pltpu.VMEM((SUBLANES, seq), F32)
    bf16_rows = 2 * SUBLANES
    assert all(w.shape[0] % (batch * bf16_rows) == 0 for w in weights)
    w_specs = [pl.BlockSpec((w.shape[0] // batch, w.shape[1]), lambda b: (b, 0)) for w in weights]
    return pl.pallas_call(
        kern,
        grid=(batch,),
        in_specs=[tok(dm), pl.BlockSpec((dm, seq), lambda b: (0, b)), tok(dm), tok(dm),
                  pl.BlockSpec((SUBLANES, seq), lambda b: (0, b)), _const_spec((1, dm))] + w_specs,
        out_specs=[tok(dm)] + w_specs,
        out_shape=[jax.ShapeDtypeStruct((T, dm), BF16)] + [jax.ShapeDtypeStruct(w.shape, BF16) for w in weights],
        scratch_shapes=([rows, rows, rows, rows] + [pltpu.VMEM((dh, 2 * dh), F32)] * N_MLSTM_HEADS
                        + [pltpu.VMEM((MLSTM_CHUNK, 2 * dh), F32)] * N_MLSTM_HEADS),
        compiler_params=pltpu.CompilerParams(dimension_semantics=("arbitrary",),
                                             vmem_limit_bytes=VMEM_LIMIT),
        name="mlstm",
    )(q, kt, v, o, gatest, gn, *weights)


def _s5_prep_kernel(arow_ref, acol_ref, ldt_ref, bre_ref, bim_ref, cre_ref, cim_ref, dcol_ref,
                    rep_ref, selrev_ref, pw_ref, pwcol_ref, w1_ref, wout_ref, al_ref, *, chunk):
    L = chunk
    P = S5_STATE
    Hc = S5_GROUP
    R = L * Hc
    hi = lax.Precision.HIGHEST
    dt = jnp.exp(ldt_ref[...])

    def select(x, sel_ref):
        x_hi = x.astype(BF16)
        x_lo = (x - x_hi.astype(F32)).astype(BF16)
        sel = sel_ref[...]
        return (jnp.dot(x_hi, sel, preferred_element_type=F32) + jnp.dot(x_lo, sel, preferred_element_type=F32))

    def abar_pow(a_re, a_im, tau):
        mag = jnp.exp(tau * (dt * a_re))
        ang = tau * (dt * a_im)
        return mag * jnp.cos(ang), mag * jnp.sin(ang)

    a_re_c = acol_ref[:, 0:1]
    a_im_c = acol_ref[:, 1:2]
    ec_re, ec_im = abar_pow(a_re_c, a_im_c, pw_ref[...])

    ab_re = ec_re[:, 1:2]
    ab_im = ec_im[:, 1:2]
    den = a_re_c * a_re_c + a_im_c * a_im_c
    nr = ab_re - 1.0
    z_re = (nr * a_re_c + ab_im * a_im_c) / den
    z_im = (ab_im * a_re_c - nr * a_im_c) / den
    bb_re = z_re * bre_ref[...] - z_im * bim_ref[...]
    bb_im = z_re * bim_ref[...] + z_im * bre_ref[...]

    n_pow = -(-(L + 1) // SUBLANES) * SUBLANES
    tau_r = lax.broadcasted_iota(jnp.int32, (n_pow, P), 0).astype(F32)
    er_re, er_im = abar_pow(arow_ref[0:1, :], arow_ref[1:2, :], tau_r)

    def rows(e, first):
        return jnp.concatenate([jnp.broadcast_to(e[first + t:first + t + 1, :], (Hc, P)) for t in range(L)], axis=0)

    c_re = jnp.concatenate([cre_ref[...]] * L, axis=0)
    c_im = jnp.concatenate([cim_ref[...]] * L, axis=0)

    def c_times_pow(first):
        e_re, e_im = rows(er_re, first), rows(er_im, first)
        return c_re * e_re - c_im * e_im, c_re * e_im + c_im * e_re

    ce_re, ce_im = c_times_pow(0)
    kflat = (jnp.dot(ce_re, bb_re, preferred_element_type=F32, precision=hi)
             - jnp.dot(ce_im, bb_im, preferred_element_type=F32, precision=hi))
    row = lax.broadcasted_iota(jnp.int32, (R, Hc), 0)
    col = lax.broadcasted_iota(jnp.int32, (R, Hc), 1)
    kflat = kflat + jnp.where(row == col, jnp.concatenate([dcol_ref[...]] * L, axis=0), 0.0)

    m = select(kflat, rep_ref)
    blk = lax.broadcasted_iota(jnp.int32, (R, R), 1) // Hc
    sh = 1
    while sh < L:
        shifted = jnp.concatenate([jnp.zeros((sh * Hc, R), F32), m[0:R - sh * Hc, :]], axis=0)
        m = jnp.where((blk & sh) != 0, shifted, m)
        sh *= 2
    w1_ref[0:R, :] = m.astype(BF16)

    ev_re, ev_im = select(ec_re, selrev_ref), select(ec_im, selrev_ref)
    bt_re, bt_im = select(bb_re, rep_ref), select(bb_im, rep_ref)
    w1_ref[R:R + P, :] = (ev_re * bt_re - ev_im * bt_im).astype(BF16)
    w1_ref[R + P:R + 2 * P, :] = (ev_re * bt_im + ev_im * bt_re).astype(BF16)

    co_re, co_im = c_times_pow(1)
    zero = jnp.zeros((R, P), F32)
    odd = pl.program_id(0) % 2 == 1
    halves = lambda v: jnp.where(odd, jnp.concatenate([zero, v], axis=1), jnp.concatenate([v, zero], axis=1))
    wout_ref[:, 0:2 * P] = halves(co_re).astype(BF16)
    wout_ref[:, 2 * P:4 * P] = halves(-co_im).astype(BF16)

    sc_re, sc_im = abar_pow(arow_ref[0:1, :], arow_ref[1:2, :], pwcol_ref[...])
    al_ref[0:SUBLANES, :] = sc_re
    al_ref[SUBLANES:2 * SUBLANES, :] = sc_im


def _s5_prep(a_re, a_im, log_dt, b_re, b_im, c_re, c_im, d_skip, *, chunk, n_chunks):
    G, P = a_re.shape
    Hc = S5_GROUP
    R = chunk * Hc
    arow = jnp.stack([a_re, a_im], axis=1)
    acol = jnp.stack([a_re, a_im], axis=2)
    ldt = log_dt.reshape(G, 1, 1)
    dcol = d_skip.reshape(G, Hc, 1)
    rep = (jnp.arange(R)[None, :] % Hc == jnp.arange(Hc)[:, None]).astype(BF16)
    selrev = (jnp.arange(LANES)[:, None] == chunk - 1 - jnp.arange(R)[None, :] // Hc).astype(BF16)
    n_steps = max(1, (n_chunks - 1).bit_length())
    assert chunk <= LANES and n_steps <= SUBLANES
    pw = jnp.zeros((1, LANES), F32).at[0, :chunk].set(jnp.arange(chunk, dtype=F32))
    pwcol = jnp.zeros((SUBLANES, 1), F32).at[:n_steps, 0].set(chunk * 2.0 ** jnp.arange(n_steps))
    kern = functools.partial(_s5_prep_kernel, chunk=chunk)
    grp = lambda *s: pl.BlockSpec((None,) + s, lambda g: (g,) + (0,) * len(s))
    return pl.pallas_call(
        kern,
        grid=(G,),
        in_specs=[grp(2, P), grp(P, 2), grp(1, 1), grp(P, Hc), grp(P, Hc), grp(Hc, P), grp(Hc, P),
                  grp(Hc, 1), _const_spec((Hc, R)), _const_spec((LANES, R)), _const_spec((1, LANES)),
                  _const_spec((SUBLANES, 1))],
        out_specs=[grp(R + 2 * P, R), grp(R, 4 * P), grp(2 * SUBLANES, P)],
        out_shape=[jax.ShapeDtypeStruct((G, R + 2 * P, R), BF16),
                   jax.ShapeDtypeStruct((G, R, 4 * P), BF16),
                   jax.ShapeDtypeStruct((G, 2 * SUBLANES, P), F32)],
        compiler_params=pltpu.CompilerParams(dimension_semantics=("arbitrary",),
                                             vmem_limit_bytes=VMEM_LIMIT),
        name="s5_prep",
    )(arow, acol, ldt, b_re, b_im, c_re, c_im, dcol, rep, selrev, pw, pwcol)


def _s5_kernel(*refs, chunk, n_chunks):
    L = chunk
    n_oct = L // SUBLANES
    u_refs, (w1_ref, wout_ref, al_ref, y_ref, ut_ref, yt_ref) = refs[0:n_oct], refs[n_oct:]
    P = S5_STATE
    Hc = S5_GROUP
    R = L * Hc
    C = y_ref.shape[0] // L
    n_grp = LANES // Hc

    for s in range(L):
        xt = u_refs[s // SUBLANES][pl.ds(s % SUBLANES, C, stride=SUBLANES), :].astype(BF16).T
        for j in range(n_grp):
            ut_ref[j, s * Hc:(s + 1) * Hc, :] = xt[j * Hc:(j + 1) * Hc, :]

    seg_pos = lax.broadcasted_iota(jnp.int32, (C, 2 * P), 0) & (n_chunks - 1)
    nt = (((1,), (1,)), ((), ()))

    def shifted(x, sh):
        return jnp.where(seg_pos >= sh, pltpu.roll(x, sh, 0), 0.0)

    def group_pair(i, carry):
        j0 = 2 * i
        res = [jnp.dot(w1_ref[j0 + d], ut_ref[j0 + d], preferred_element_type=F32) for d in range(2)]
        x_re = jnp.concatenate([res[0][R:R + P], res[1][R:R + P]], axis=0).T
        x_im = jnp.concatenate([res[0][R + P:R + 2 * P], res[1][R + P:R + 2 * P]], axis=0).T
        sh, k = 1, 0
        while sh < n_chunks:
            a_re = jnp.concatenate([al_ref[j0, k:k + 1, :], al_ref[j0 + 1, k:k + 1, :]], axis=1)
            a_im = jnp.concatenate([al_ref[j0, SUBLANES + k:SUBLANES + k + 1, :],
                                    al_ref[j0 + 1, SUBLANES + k:SUBLANES + k + 1, :]], axis=1)
            s_re, s_im = shifted(x_re, sh), shifted(x_im, sh)
            x_re, x_im = (x_re + a_re * s_re - a_im * s_im, x_im + a_re * s_im + a_im * s_re)
            sh *= 2
            k += 1
        xprev = jnp.concatenate([shifted(x_re, 1), shifted(x_im, 1)], axis=1).astype(BF16)
        for d in range(2):
            yt_ref[j0 + d] = res[d][0:R, :] + lax.dot_general(wout_ref[j0 + d], xprev, nt,
                                                              preferred_element_type=F32)
        return carry

    lax.fori_loop(0, n_grp // 2, group_pair, 0)

    for t in range(L):
        zt = jnp.concatenate([yt_ref[j, t * Hc:(t + 1) * Hc, :] for j in range(n_grp)], axis=0)
        y_ref[pl.ds(t, C, stride=L), :] = zt.T


def _s5(u, w1, wout, al, *, chunk, n_chunks, n_col_blocks):
    n_oct, t_oct, ds5 = u.shape
    T = n_oct * t_oct
    assert n_oct * SUBLANES == chunk
    G, R1, R = w1.shape
    P4 = wout.shape[-1]
    n_grp = LANES // S5_GROUP
    n_q = ds5 // LANES
    tb = T // n_col_blocks
    C = tb // chunk
    kern = functools.partial(_s5_kernel, chunk=chunk, n_chunks=n_chunks)
    grp = lambda *s: pl.BlockSpec((n_grp,) + s, lambda q, i: (q,) + (0,) * len(s))
    octets = [pl.BlockSpec((None, C * SUBLANES, LANES), functools.partial(lambda q, i, o: (o, i, q), o=o))
              for o in range(n_oct)]
    return pl.pallas_call(
        kern,
        grid=(n_q, n_col_blocks),
        in_specs=octets + [grp(R1, R), grp(R, P4), grp(2 * SUBLANES, P4 // 4)],
        out_specs=pl.BlockSpec((tb, LANES), lambda q, i: (i, q)),
        out_shape=jax.ShapeDtypeStruct((T, ds5), F32),
        scratch_shapes=[pltpu.VMEM((n_grp, R, C), BF16), pltpu.VMEM((n_grp, R, C), F32)],
        compiler_params=pltpu.CompilerParams(dimension_semantics=("arbitrary", "arbitrary"),
                                             vmem_limit_bytes=VMEM_LIMIT),
        name="s5",
    )(*([u] * n_oct), w1, wout, al)


def _out_kernel(x_ref, hm_ref, y_ref, wglu_ref, bglu_ref, gs5_ref, wo_ref, g2_ref,
                wgate_ref, wup_ref, wdown_ref, gf_ref, out_ref, acc_ref, xn_ref, *, ff_chunk, final_norm):
    dm = hm_ref.shape[1]
    g = jax.nn.gelu(y_ref[...])
    z = jnp.dot(g.astype(BF16), wglu_ref[...], preferred_element_type=F32) + bglu_ref[...]
    hs = _rmsnorm(g * _sigmoid(z), gs5_ref[...]).astype(BF16)
    h1 = (x_ref[...] + jnp.dot(hm_ref[...], wo_ref[0:dm, :], preferred_element_type=F32)
          + jnp.dot(hs, wo_ref[dm:, :], preferred_element_type=F32))
    xn_ref[...] = _rmsnorm(h1, g2_ref[...]).astype(BF16)
    acc_ref[...] = h1

    def ffn_chunk(c, carry):
        cs = pl.ds(pl.multiple_of(c * ff_chunk, ff_chunk), ff_chunk)
        xn2 = xn_ref[...]
        gate = jnp.dot(xn2, wgate_ref[:, cs], preferred_element_type=F32)
        up = jnp.dot(xn2, wup_ref[:, cs], preferred_element_type=F32)
        act = (gate * _sigmoid(gate) * up).astype(BF16)
        acc_ref[...] += jnp.dot(act, wdown_ref[cs, :], preferred_element_type=F32)
        return carry

    lax.fori_loop(0, wgate_ref.shape[1] // ff_chunk, ffn_chunk, 0, unroll=4)
    out_ref[...] = _rmsnorm(acc_ref[...], gf_ref[...]) if final_norm else acc_ref[...]


def _out_block(x2, hm, y, wglu, bglu, gs5, wo, g2, wgate, wup, wdown, gf, *, tm, ff_chunk, final_norm):
    T, D = x2.shape
    dm = hm.shape[1]
    ds5 = y.shape[1]
    assert wgate.shape[1] % ff_chunk == 0
    kern = functools.partial(_out_kernel, ff_chunk=ff_chunk, final_norm=final_norm)
    tok = lambda w: pl.BlockSpec((tm, w), lambda i: (i, 0))
    return pl.pallas_call(
        kern,
        grid=(T // tm,),
        in_specs=[tok(D), tok(dm), tok(ds5), _const_spec(wglu.shape), _const_spec((1, ds5)),
                  _const_spec((1, ds5)), _const_spec(wo.shape), _const_spec((1, D)),
                  _const_spec(wgate.shape), _const_spec(wup.shape), _const_spec(wdown.shape),
                  _const_spec((1, D))],
        out_specs=tok(D),
        out_shape=jax.ShapeDtypeStruct((T, D), F32),
        scratch_shapes=[pltpu.VMEM((tm, D), F32), pltpu.VMEM((tm, D), BF16)],
        compiler_params=pltpu.CompilerParams(dimension_semantics=("arbitrary",),
                                             vmem_limit_bytes=56 * 1024 * 1024),
        name="out_block",
    )(x2, hm, y, wglu, bglu, gs5, wo, g2, wgate, wup, wdown, gf)


def kernel(x, norm1_g, w_in, if_bias, conv_qk, mlstm_norm_g, a_re, a_im, log_dt, b_re, b_im, c_re, c_im,
           d_skip, w_glu, b_glu, s5_norm_g, w_out, norm2_g, w_gate, w_up, w_down, norm_f_g):
    B, S, D = x.shape
    depth = w_in.shape[0]
    H = N_MLSTM_HEADS
    dm = mlstm_norm_g.shape[1]
    T = B * S
    L5 = S5_CHUNK
    nc5 = S // L5
    row = lambda a: a.reshape(1, -1).astype(F32)

    h = x.reshape(T, D)
    for l in range(depth):
        last = l == depth - 1
        gb = jnp.zeros((1, LANES), F32).at[0, 0:2 * H].set(if_bias[l])

        q, kt, v, o, gatest, u = _inproj(h, row(norm1_g[l]), w_in[l].T.astype(F32), gb, conv_qk[l].astype(F32),
                                         seq=S, tm=512, d_mlstm=dm, n_oct=L5 // SUBLANES)
        hm, wglu, wo, wgate, wup, wdown = _mlstm(
            q, kt, v, o, gatest, row(mlstm_norm_g[l]),
            [w.astype(F32) for w in (w_glu[l], w_out[l], w_gate[l], w_up[l], w_down[l])], batch=B, seq=S)

        w1, wout, al = _s5_prep(a_re[l], a_im[l], log_dt[l], b_re[l], b_im[l], c_re[l], c_im[l],
                                d_skip[l], chunk=L5, n_chunks=nc5)
        y = _s5(u, w1, wout, al, chunk=L5, n_chunks=nc5, n_col_blocks=2)

        h = _out_block(h, hm, y, wglu, row(b_glu[l]), row(s5_norm_g[l]), wo, row(norm2_g[l]),
                       wgate, wup, wdown, row(norm_f_g), tm=512, ff_chunk=256, final_norm=last)
    return h.reshape(B, S, D)
```

```python
import functools

import jax
import jax.numpy as jnp
from jax import lax
from jax.experimental import pallas as pl
from jax.experimental.pallas import tpu as pltpu

EPS = 1e-6
N_MLSTM_HEADS = 4
CONV_WIDTH = 4
S5_GROUP = 16
S5_STATE = 64

LANES = 128
SUBLANES = 8
MLSTM_CHUNK = 128
S5_CHUNK = 16
VMEM_LIMIT = 48 * 1024 * 1024

F32 = jnp.float32
BF16 = jnp.bfloat16


def _rmsnorm(x, g):
    ms = jnp.mean(x * x, axis=-1, keepdims=True)
    return x * lax.rsqrt(ms + EPS) * g


def _sigmoid(x):
    return 1.0 / (1.0 + jnp.exp(-x))


def _log_sigmoid(x):
    return -(jnp.maximum(-x, 0.0) + jnp.log1p(jnp.exp(-jnp.abs(x))))


def _const_spec(shape):
    return pl.BlockSpec(shape, lambda *_: (0,) * len(shape), pipeline_mode=pl.Buffered(1))


def _inproj_kernel(x_ref, g1_ref, wt_ref, gb_ref, cw_ref,
                   q_ref, kt_ref, v_ref, o_ref, gatest_ref, u_ref,
                   ext_ref, wm_ref, wg_ref, wu_ref, *, tiles_per_seq, d_mlstm, k_scale):
    tm = x_ref.shape[0]
    dm = d_mlstm
    H = N_MLSTM_HEADS
    halo = SUBLANES

    @pl.when(pl.program_id(0) == 0)
    def _():
        for c0 in range(0, 4 * dm, dm):
            wm_ref[:, c0:c0 + dm] = wt_ref[c0:c0 + dm, :].T.astype(BF16)
        wu_ref[...] = wt_ref[4 * dm + 2 * H:, :].T.astype(BF16)
        fill = jnp.zeros((LANES - 2 * H, wt_ref.shape[1]), F32)
        wg_ref[...] = jnp.concatenate([wt_ref[4 * dm:4 * dm + 2 * H, :], fill], axis=0).T.astype(BF16)

    xn = _rmsnorm(x_ref[...], g1_ref[...]).astype(BF16)

    @pl.when(pl.program_id(0) % tiles_per_seq == 0)
    def _():
        ext_ref[0:halo, :] = jnp.zeros((halo, 2 * dm), F32)

    ext_ref[halo:halo + tm, :] = jnp.dot(xn, wm_ref[:, 0:2 * dm], preferred_element_type=F32)
    v = jnp.dot(xn, wm_ref[:, 2 * dm:3 * dm], preferred_element_type=F32)
    o = jnp.dot(xn, wm_ref[:, 3 * dm:4 * dm], preferred_element_type=F32)
    u = jnp.dot(xn, wu_ref[...], preferred_element_type=F32)
    gates = jnp.dot(xn, wg_ref[...], preferred_element_type=F32) + gb_ref[...]
    v_ref[...] = v.astype(BF16)
    o_ref[...] = o.astype(BF16)
    n_oct = u_ref.shape[0]
    u4 = u.reshape(tm // (n_oct * SUBLANES), n_oct, SUBLANES, u.shape[1])
    for oc in range(n_oct):
        u_ref[oc] = u4[:, oc].reshape(tm // n_oct, u.shape[1])
    gatest_ref[...] = gates.T[0:2 * H, :]

    def zero_of(t):
        w = pltpu.bitcast(t[tm - SUBLANES:tm, t.shape[1] - LANES:], jnp.uint32)
        return pltpu.bitcast((w >> 16) >> 16, F32)
    anchor = zero_of(v)

    acc = cw_ref[CONV_WIDTH - 1:CONV_WIDTH, :] * ext_ref[halo:halo + tm, :]
    for j in range(1, CONV_WIDTH):
        acc = acc + cw_ref[CONV_WIDTH - 1 - j:CONV_WIDTH - j, :] * ext_ref[halo - j:halo - j + tm, :]
    ext_ref[0:halo, :] = ext_ref[tm:tm + halo, :]
    qk = acc * _sigmoid(acc) + jnp.tile(anchor, (tm // SUBLANES, 2 * dm // LANES))
    q_ref[...] = qk[:, 0:dm].astype(BF16)
    kt_ref[...] = (qk[:, dm:2 * dm] * k_scale).T.astype(BF16)


def _inproj(x2, g1, w_in_t, gb, cw, *, seq, tm, d_mlstm, n_oct):
    T, D = x2.shape
    dm = d_mlstm
    H = N_MLSTM_HEADS
    assert 2 * H == SUBLANES
    ds5 = w_in_t.shape[0] - 4 * dm - 2 * H
    kern = functools.partial(_inproj_kernel, tiles_per_seq=seq // tm, d_mlstm=dm,
                             k_scale=float((dm // H) ** -0.5))
    tok = lambda w: pl.BlockSpec((tm, w), lambda i: (i, 0))
    return pl.pallas_call(
        kern,
        grid=(T // tm,),
        in_specs=[tok(D), _const_spec((1, D)), _const_spec(w_in_t.shape), _const_spec((1, LANES)),
                  _const_spec(cw.shape)],
        out_specs=[tok(dm), pl.BlockSpec((dm, tm), lambda i: (0, i)), tok(dm), tok(dm),
                   pl.BlockSpec((2 * H, tm), lambda i: (0, i)),
                   pl.BlockSpec((n_oct, tm // n_oct, ds5), lambda i: (0, i, 0))],
        out_shape=[jax.ShapeDtypeStruct((T, dm), BF16), jax.ShapeDtypeStruct((dm, T), BF16),
                   jax.ShapeDtypeStruct((T, dm), BF16), jax.ShapeDtypeStruct((T, dm), BF16),
                   jax.ShapeDtypeStruct((2 * H, T), F32),
                   jax.ShapeDtypeStruct((n_oct, T // n_oct, ds5), F32)],
        scratch_shapes=[pltpu.VMEM((tm + 2 * SUBLANES, 2 * dm), F32), pltpu.VMEM((D, 4 * dm), BF16),
                        pltpu.VMEM((D, LANES), BF16), pltpu.VMEM((D, ds5), BF16)],
        compiler_params=pltpu.CompilerParams(dimension_semantics=("arbitrary",),
                                             vmem_limit_bytes=VMEM_LIMIT),
        name="inproj",
    )(x2, g1, w_in_t, gb, cw)


def _lane_scan(x, op, identity):
    n = x.shape[-1]
    pos = lax.broadcasted_iota(jnp.int32, x.shape, x.ndim - 1)
    sh = 1
    while sh < n:
        x = op(x, jnp.where(pos >= sh, pltpu.roll(x, sh, x.ndim - 1), identity))
        sh *= 2
    return x


def _mlstm_kernel(q_ref, kt_ref, v_ref, o_ref, gatest_ref, gn_ref, *refs, chunk, n_cast):
    S = q_ref.shape[0]
    H = N_MLSTM_HEADS
    dh = q_ref.shape[1] // H
    L = chunk
    assert L == dh == LANES
    w32_refs, out_ref, w16_refs = refs[0:n_cast], refs[n_cast], refs[n_cast + 1:2 * n_cast + 1]
    a_ref, g0_ref, nm_ref, gend_ref = refs[2 * n_cast + 1:2 * n_cast + 5]
    c_refs, nd_refs = refs[2 * n_cast + 5:2 * n_cast + 5 + H], refs[2 * n_cast + 5 + H:]

    for w32_ref, w16_ref in zip(w32_refs, w16_refs):
        w16_ref[...] = w32_ref[...].astype(BF16)

    gates = gatest_ref[...]
    f_cum = pltpu.roll(_lane_scan(_log_sigmoid(gates), jnp.add, 0.0), H, 0)
    a = gates - f_cum
    g0 = jnp.maximum(_lane_scan(a, jnp.maximum, -jnp.inf), 0.0)
    a_ref[...] = a
    g0_ref[...] = g0
    nm_ref[...] = -(f_cum + g0)
    for c in range(S // L):
        gend_ref[:, c * L:(c + 1) * L] = jnp.broadcast_to(g0[:, (c + 1) * L - 1:(c + 1) * L], (SUBLANES, L))

    for c_ref in c_refs:
        c_ref[...] = jnp.zeros(c_ref.shape, F32)
    causal = (lax.broadcasted_iota(jnp.int32, (L, L), 0) >= lax.broadcasted_iota(jnp.int32, (L, L), 1))
    ones_blk = jnp.ones((L, dh), BF16)
    mean_mat = jnp.full((dh, dh), 1.0 / dh, BF16)

    def mix(c, h, g_prev):
        r0 = pl.multiple_of(c * L, L)
        hs = slice(h * dh, (h + 1) * dh)
        qc = q_ref[pl.ds(r0, L), hs]
        ktc = kt_ref[hs, pl.ds(r0, L)]
        v_aug = jnp.concatenate([v_ref[pl.ds(r0, L), hs], ones_blk], axis=1)
        a_row = a_ref[h:h + 1, pl.ds(r0, L)]
        g_end = gend_ref[h:h + 1, pl.ds(r0, L)]
        g0_t = jnp.broadcast_to(g0_ref[h:h + 1, pl.ds(r0, L)], (L, L)).T

        s_qk = jnp.dot(qc, ktc, preferred_element_type=F32)
        p = (jnp.exp(jnp.where(causal, a_row - g0_t, -jnp.inf)) * s_qk).astype(BF16)
        c_prev = c_refs[h][...]
        q_w = (qc.astype(F32) * jnp.exp(g_prev - g0_t)).astype(BF16)
        nd_refs[h][...] = jnp.dot(jnp.concatenate([q_w, p], axis=1),
                                  jnp.concatenate([c_prev.astype(BF16), v_aug], axis=0),
                                  preferred_element_type=F32)

        kte = (ktc.astype(F32) * jnp.exp(a_row - g_end)).astype(BF16)
        decay = jnp.exp(g_prev - g_end)
        c_refs[h][...] = (jnp.concatenate([decay, decay], axis=1) * c_prev
                          + jnp.dot(kte, v_aug, preferred_element_type=F32))
        return g_end

    def emit(c, h):
        r0 = pl.multiple_of(c * L, L)
        hs = slice(h * dh, (h + 1) * dh)
        nm_t = jnp.broadcast_to(nm_ref[h:h + 1, pl.ds(r0, L)], (L, L)).T
        nd = nd_refs[h][...]
        h_tilde = nd[:, 0:dh] / jnp.maximum(jnp.abs(nd[:, dh:2 * dh]), jnp.exp(nm_t))
        hm = _sigmoid(o_ref[pl.ds(r0, L), hs].astype(F32)) * h_tilde
        mu = jnp.dot(hm.astype(BF16), mean_mat, preferred_element_type=F32)
        hc = hm - mu
        var = jnp.dot((hc * hc).astype(BF16), mean_mat, preferred_element_type=F32)
        out_ref[pl.ds(r0, L), hs] = (hc * lax.rsqrt(var + EPS) * gn_ref[:, hs]).astype(BF16)

    zero = jnp.zeros((1, L), F32)
    g_first = tuple(mix(0, h, zero) for h in range(H))

    def body(c, g_carry):
        for h in range(H):
            emit(c - 1, h)
        return tuple(mix(c, h, g_carry[h]) for h in range(H))

    lax.fori_loop(1, S // L, body, g_first, unroll=3)
    for h in range(H):
        emit(S // L - 1, h)


def _mlstm(q, kt, v, o, gatest, gn, weights, *, batch, seq):
    T, dm = q.shape
    dh = dm // N_MLSTM_HEADS
    kern = functools.partial(_mlstm_kernel, chunk=MLSTM_CHUNK, n_cast=len(weights))
    tok = lambda w: pl.BlockSpec((seq, w), lambda b: (b, 0))
    rows = pltpu.VMEM((SUBLANES, seq), F32)
    bf16_rows = 2 * SUBLANES
    assert all(w.shape[0] % (batch * bf16_rows) == 0 for w in weights)
    w_specs = [pl.BlockSpec((w.shape[0] // batch, w.shape[1]), lambda b: (b, 0)) for w in weights]
    return pl.pallas_call(
        kern,
        grid=(batch,),
        in_specs=[tok(dm), pl.BlockSpec((dm, seq), lambda b: (0, b)), tok(dm), tok(dm),
                  pl.BlockSpec((SUBLANES, seq), lambda b: (0, b)), _const_spec((1, dm))] + w_specs,
        out_specs=[tok(dm)] + w_specs,
        out_shape=[jax.ShapeDtypeStruct((T, dm), BF16)] + [jax.ShapeDtypeStruct(w.shape, BF16) for w in weights],
        scratch_shapes=([rows, rows, rows, rows] + [pltpu.VMEM((dh, 2 * dh), F32)] * N_MLSTM_HEADS
                        + [pltpu.VMEM((MLSTM_CHUNK, 2 * dh), F32)] * N_MLSTM_HEADS),
        compiler_params=pltpu.CompilerParams(dimension_semantics=("arbitrary",),
                                             vmem_limit_bytes=VMEM_LIMIT),
        name="mlstm",
    )(q, kt, v, o, gatest, gn, *weights)


def _s5_prep_kernel(arow_ref, acol_ref, ldt_ref, bre_ref, bim_ref, cre_ref, cim_ref, dcol_ref,
                    rep_ref, selrev_ref, pw_ref, pwcol_ref, w1_ref, wout_ref, al_ref, *, chunk):
    L = chunk
    P = S5_STATE
    Hc = S5_GROUP
    R = L * Hc
    hi = lax.Precision.HIGHEST
    dt = jnp.exp(ldt_ref[...])

    def select(x, sel_ref):
        x_hi = x.astype(BF16)
        x_lo = (x - x_hi.astype(F32)).astype(BF16)
        sel = sel_ref[...]
        return (jnp.dot(x_hi, sel, preferred_element_type=F32) + jnp.dot(x_lo, sel, preferred_element_type=F32))

    def abar_pow(a_re, a_im, tau):
        mag = jnp.exp(tau * (dt * a_re))
        ang = tau * (dt * a_im)
        return mag * jnp.cos(ang), mag * jnp.sin(ang)

    a_re_c = acol_ref[:, 0:1]
    a_im_c = acol_ref[:, 1:2]
    ec_re, ec_im = abar_pow(a_re_c, a_im_c, pw_ref[...])

    ab_re = ec_re[:, 1:2]
    ab_im = ec_im[:, 1:2]
    den = a_re_c * a_re_c + a_im_c * a_im_c
    nr = ab_re - 1.0
    z_re = (nr * a_re_c + ab_im * a_im_c) / den
    z_im = (ab_im * a_re_c - nr * a_im_c) / den
    bb_re = z_re * bre_ref[...] - z_im * bim_ref[...]
    bb_im = z_re * bim_ref[...] + z_im * bre_ref[...]

    n_pow = -(-(L + 1) // SUBLANES) * SUBLANES
    tau_r = lax.broadcasted_iota(jnp.int32, (n_pow, P), 0).astype(F32)
    er_re, er_im = abar_pow(arow_ref[0:1, :], arow_ref[1:2, :], tau_r)

    def rows(e, first):
        return jnp.concatenate([jnp.broadcast_to(e[first + t:first + t + 1, :], (Hc, P)) for t in range(L)], axis=0)

    c_re = jnp.concatenate([cre_ref[...]] * L, axis=0)
    c_im = jnp.concatenate([cim_ref[...]] * L, axis=0)

    def c_times_pow(first):
        e_re, e_im = rows(er_re, first), rows(er_im, first)
        return c_re * e_re - c_im * e_im, c_re * e_im + c_im * e_re

    ce_re, ce_im = c_times_pow(0)
    kflat = (jnp.dot(ce_re, bb_re, preferred_element_type=F32, precision=hi)
             - jnp.dot(ce_im, bb_im, preferred_element_type=F32, precision=hi))
    row = lax.broadcasted_iota(jnp.int32, (R, Hc), 0)
    col = lax.broadcasted_iota(jnp.int32, (R, Hc), 1)
    kflat = kflat + jnp.where(row == col, jnp.concatenate([dcol_ref[...]] * L, axis=0), 0.0)

    m = select(kflat, rep_ref)
    blk = lax.broadcasted_iota(jnp.int32, (R, R), 1) // Hc
    sh = 1
    while sh < L:
        shifted = jnp.concatenate([jnp.zeros((sh * Hc, R), F32), m[0:R - sh * Hc, :]], axis=0)
        m = jnp.where((blk & sh) != 0, shifted, m)
        sh *= 2
    w1_ref[0:R, :] = m.astype(BF16)

    ev_re, ev_im = select(ec_re, selrev_ref), select(ec_im, selrev_ref)
    bt_re, bt_im = select(bb_re, rep_ref), select(bb_im, rep_ref)
    w1_ref[R:R + P, :] = (ev_re * bt_re - ev_im * bt_im).astype(BF16)
    w1_ref[R + P:R + 2 * P, :] = (ev_re * bt_im + ev_im * bt_re).astype(BF16)

    co_re, co_im = c_times_pow(1)
    zero = jnp.zeros((R, P), F32)
    odd = pl.program_id(0) % 2 == 1
    halves = lambda v: jnp.where(odd, jnp.concatenate([zero, v], axis=1), jnp.concatenate([v, zero], axis=1))
    wout_ref[:, 0:2 * P] = halves(co_re).astype(BF16)
    wout_ref[:, 2 * P:4 * P] = halves(-co_im).astype(BF16)

    sc_re, sc_im = abar_pow(arow_ref[0:1, :], arow_ref[1:2, :], pwcol_ref[...])
    al_ref[0:SUBLANES, :] = sc_re
    al_ref[SUBLANES:2 * SUBLANES, :] = sc_im


def _s5_prep(a_re, a_im, log_dt, b_re, b_im, c_re, c_im, d_skip, *, chunk, n_chunks):
    G, P = a_re.shape
    Hc = S5_GROUP
    R = chunk * Hc
    arow = jnp.stack([a_re, a_im], axis=1)
    acol = jnp.stack([a_re, a_im], axis=2)
    ldt = log_dt.reshape(G, 1, 1)
    dcol = d_skip.reshape(G, Hc, 1)
    rep = (jnp.arange(R)[None, :] % Hc == jnp.arange(Hc)[:, None]).astype(BF16)
    selrev = (jnp.arange(LANES)[:, None] == chunk - 1 - jnp.arange(R)[None, :] // Hc).astype(BF16)
    n_steps = max(1, (n_chunks - 1).bit_length())
    assert chunk <= LANES and n_steps <= SUBLANES
    pw = jnp.zeros((1, LANES), F32).at[0, :chunk].set(jnp.arange(chunk, dtype=F32))
    pwcol = jnp.zeros((SUBLANES, 1), F32).at[:n_steps, 0].set(chunk * 2.0 ** jnp.arange(n_steps))
    kern = functools.partial(_s5_prep_kernel, chunk=chunk)
    grp = lambda *s: pl.BlockSpec((None,) + s, lambda g: (g,) + (0,) * len(s))
    return pl.pallas_call(
        kern,
        grid=(G,),
        in_specs=[grp(2, P), grp(P, 2), grp(1, 1), grp(P, Hc), grp(P, Hc), grp(Hc, P), grp(Hc, P),
                  grp(Hc, 1), _const_spec((Hc, R)), _const_spec((LANES, R)), _const_spec((1, LANES)),
                  _const_spec((SUBLANES, 1))],
        out_specs=[grp(R + 2 * P, R), grp(R, 4 * P), grp(2 * SUBLANES, P)],
        out_shape=[jax.ShapeDtypeStruct((G, R + 2 * P, R), BF16),
                   jax.ShapeDtypeStruct((G, R, 4 * P), BF16),
                   jax.ShapeDtypeStruct((G, 2 * SUBLANES, P), F32)],
        compiler_params=pltpu.CompilerParams(dimension_semantics=("arbitrary",),
                                             vmem_limit_bytes=VMEM_LIMIT),
        name="s5_prep",
    )(arow, acol, ldt, b_re, b_im, c_re, c_im, dcol, rep, selrev, pw, pwcol)


def _s5_kernel(*refs, chunk, n_chunks):
    L = chunk
    n_oct = L // SUBLANES
    u_refs, (w1_ref, wout_ref, al_ref, y_ref, ut_ref, yt_ref) = refs[0:n_oct], refs[n_oct:]
    P = S5_STATE
    Hc = S5_GROUP
    R = L * Hc
    C = y_ref.shape[0] // L
    n_grp = LANES // Hc

    for s in range(L):
        xt = u_refs[s // SUBLANES][pl.ds(s % SUBLANES, C, stride=SUBLANES), :].astype(BF16).T
        for j in range(n_grp):
            ut_ref[j, s * Hc:(s + 1) * Hc, :] = xt[j * Hc:(j + 1) * Hc, :]

    seg_pos = lax.broadcasted_iota(jnp.int32, (C, 2 * P), 0) & (n_chunks - 1)
    nt = (((1,), (1,)), ((), ()))

    def shifted(x, sh):
        return jnp.where(seg_pos >= sh, pltpu.roll(x, sh, 0), 0.0)

    def group_pair(i, carry):
        j0 = 2 * i
        res = [jnp.dot(w1_ref[j0 + d], ut_ref[j0 + d], preferred_element_type=F32) for d in range(2)]
        x_re = jnp.concatenate([res[0][R:R + P], res[1][R:R + P]], axis=0).T
        x_im = jnp.concatenate([res[0][R + P:R + 2 * P], res[1][R + P:R + 2 * P]], axis=0).T
        sh, k = 1, 0
        while sh < n_chunks:
            a_re = jnp.concatenate([al_ref[j0, k:k + 1, :], al_ref[j0 + 1, k:k + 1, :]], axis=1)
            a_im = jnp.concatenate([al_ref[j0, SUBLANES + k:SUBLANES + k + 1, :],
                                    al_ref[j0 + 1, SUBLANES + k:SUBLANES + k + 1, :]], axis=1)
            s_re, s_im = shifted(x_re, sh), shifted(x_im, sh)
            x_re, x_im = (x_re + a_re * s_re - a_im * s_im, x_im + a_re * s_im + a_im * s_re)
            sh *= 2
            k += 1
        xprev = jnp.concatenate([shifted(x_re, 1), shifted(x_im, 1)], axis=1).astype(BF16)
        for d in range(2):
            yt_ref[j0 + d] = res[d][0:R, :] + lax.dot_general(wout_ref[j0 + d], xprev, nt,
                                                              preferred_element_type=F32)
        return carry

    lax.fori_loop(0, n_grp // 2, group_pair, 0)

    for t in range(L):
        zt = jnp.concatenate([yt_ref[j, t * Hc:(t + 1) * Hc, :] for j in range(n_grp)], axis=0)
        y_ref[pl.ds(t, C, stride=L), :] = zt.T


def _s5(u, w1, wout, al, *, chunk, n_chunks, n_col_blocks):
    n_oct, t_oct, ds5 = u.shape
    T = n_oct * t_oct
    assert n_oct * SUBLANES == chunk
    G, R1, R = w1.shape
    P4 = wout.shape[-1]
    n_grp = LANES // S5_GROUP
    n_q = ds5 // LANES
    tb = T // n_col_blocks
    C = tb // chunk
    kern = functools.partial(_s5_kernel, chunk=chunk, n_chunks=n_chunks)
    grp = lambda *s: pl.BlockSpec((n_grp,) + s, lambda q, i: (q,) + (0,) * len(s))
    octets = [pl.BlockSpec((None, C * SUBLANES, LANES), functools.partial(lambda q, i, o: (o, i, q), o=o))
              for o in range(n_oct)]
    return pl.pallas_call(
        kern,
        grid=(n_q, n_col_blocks),
        in_specs=octets + [grp(R1, R), grp(R, P4), grp(2 * SUBLANES, P4 // 4)],
        out_specs=pl.BlockSpec((tb, LANES), lambda q, i: (i, q)),
        out_shape=jax.ShapeDtypeStruct((T, ds5), F32),
        scratch_shapes=[pltpu.VMEM((n_grp, R, C), BF16), pltpu.VMEM((n_grp, R, C), F32)],
        compiler_params=pltpu.CompilerParams(dimension_semantics=("arbitrary", "arbitrary"),
                                             vmem_limit_bytes=VMEM_LIMIT),
        name="s5",
    )(*([u] * n_oct), w1, wout, al)


def _out_kernel(x_ref, hm_ref, y_ref, wglu_ref, bglu_ref, gs5_ref, wo_ref, g2_ref,
                wgate_ref, wup_ref, wdown_ref, gf_ref, out_ref, acc_ref, xn_ref, *, ff_chunk, final_norm):
    dm = hm_ref.shape[1]
    g = jax.nn.gelu(y_ref[...])
    z = jnp.dot(g.astype(BF16), wglu_ref[...], preferred_element_type=F32) + bglu_ref[...]
    hs = _rmsnorm(g * _sigmoid(z), gs5_ref[...]).astype(BF16)
    h1 = (x_ref[...] + jnp.dot(hm_ref[...], wo_ref[0:dm, :], preferred_element_type=F32)
          + jnp.dot(hs, wo_ref[dm:, :], preferred_element_type=F32))
    xn_ref[...] = _rmsnorm(h1, g2_ref[...]).astype(BF16)
    acc_ref[...] = h1

    def ffn_chunk(c, carry):
        cs = pl.ds(pl.multiple_of(c * ff_chunk, ff_chunk), ff_chunk)
        xn2 = xn_ref[...]
        gate = jnp.dot(xn2, wgate_ref[:, cs], preferred_element_type=F32)
        up = jnp.dot(xn2, wup_ref[:, cs], preferred_element_type=F32)
        act = (gate * _sigmoid(gate) * up).astype(BF16)
        acc_ref[...] += jnp.dot(act, wdown_ref[cs, :], preferred_element_type=F32)
        return carry

    lax.fori_loop(0, wgate_ref.shape[1] // ff_chunk, ffn_chunk, 0, unroll=4)
    out_ref[...] = _rmsnorm(acc_ref[...], gf_ref[...]) if final_norm else acc_ref[...]


def _out_block(x2, hm, y, wglu, bglu, gs5, wo, g2, wgate, wup, wdown, gf, *, tm, ff_chunk, final_norm):
    T, D = x2.shape
    dm = hm.shape[1]
    ds5 = y.shape[1]
    assert wgate.shape[1] % ff_chunk == 0
    kern = functools.partial(_out_kernel, ff_chunk=ff_chunk, final_norm=final_norm)
    tok = lambda w: pl.BlockSpec((tm, w), lambda i: (i, 0))
    return pl.pallas_call(
        kern,
        grid=(T // tm,),
        in_specs=[tok(D), tok(dm), tok(ds5), _const_spec(wglu.shape), _const_spec((1, ds5)),
                  _const_spec((1, ds5)), _const_spec(wo.shape), _const_spec((1, D)),
                  _const_spec(wgate.shape), _const_spec(wup.shape), _const_spec(wdown.shape),
                  _const_spec((1, D))],
        out_specs=tok(D),
        out_shape=jax.ShapeDtypeStruct((T, D), F32),
        scratch_shapes=[pltpu.VMEM((tm, D), F32), pltpu.VMEM((tm, D), BF16)],
        compiler_params=pltpu.CompilerParams(dimension_semantics=("arbitrary",),
                                             vmem_limit_bytes=56 * 1024 * 1024),
        name="out_block",
    )(x2, hm, y, wglu, bglu, gs5, wo, g2, wgate, wup, wdown, gf)


def kernel(x, norm1_g, w_in, if_bias, conv_qk, mlstm_norm_g, a_re, a_im, log_dt, b_re, b_im, c_re, c_im,
           d_skip, w_glu, b_glu, s5_norm_g, w_out, norm2_g, w_gate, w_up, w_down, norm_f_g):
    B, S, D = x.shape
    depth = w_in.shape[0]
    H = N_MLSTM_HEADS
    dm = mlstm_norm_g.shape[1]
    T = B * S
    L5 = S5_CHUNK
    nc5 = S // L5
    row = lambda a: a.reshape(1, -1).astype(F32)

    h = x.reshape(T, D)
    for l in range(depth):
        last = l == depth - 1
        gb = jnp.zeros((1, LANES), F32).at[0, 0:2 * H].set(if_bias[l])

        q, kt, v, o, gatest, u = _inproj(h, row(norm1_g[l]), w_in[l].T.astype(F32), gb, conv_qk[l].astype(F32),
                                         seq=S, tm=512, d_mlstm=dm, n_oct=L5 // SUBLANES)
        hm, wglu, wo, wgate, wup, wdown = _mlstm(
            q, kt, v, o, gatest, row(mlstm_norm_g[l]),
            [w.astype(F32) for w in (w_glu[l], w_out[l], w_gate[l], w_up[l], w_down[l])], batch=B, seq=S)

        w1, wout, al = _s5_prep(a_re[l], a_im[l], log_dt[l], b_re[l], b_im[l], c_re[l], c_im[l],
                                d_skip[l], chunk=L5, n_chunks=nc5)
        y = _s5(u, w1, wout, al, chunk=L5, n_chunks=nc5, n_col_blocks=2)

        h = _out_block(h, hm, y, wglu, row(b_glu[l]), row(s5_norm_g[l]), wo, row(norm2_g[l]),
                       wgate, wup, wdown, row(norm_f_g), tm=512, ff_chunk=256, final_norm=last)
    return h.reshape(B, S, D)
```

```python
import functools

import jax
import jax.numpy as jnp
from jax import lax
from jax.experimental import pallas as pl
from jax.experimental.pallas import tpu as pltpu

EPS = 1e-6
N_MLSTM_HEADS = 4
CONV_WIDTH = 4
S5_GROUP = 16
S5_STATE = 64

LANES = 128
SUBLANES = 8
MLSTM_CHUNK = 128
S5_CHUNK = 16
S5_COL_BLOCKS = 2
INPROJ_TM = 1024
OUT_TM = 512
FF_CHUNK = 256
VMEM_LIMIT = 48 * 1024 * 1024
OUT_VMEM_LIMIT = 56 * 1024 * 1024

F32 = jnp.float32
BF16 = jnp.bfloat16


def _rmsnorm(x, g):
    ms = jnp.mean(x * x, axis=-1, keepdims=True)
    return x * lax.rsqrt(ms + EPS) * g


def _sigmoid(x):
    return 1.0 / (1.0 + jnp.exp(-x))


def _log_sigmoid(x):
    return -(jnp.maximum(-x, 0.0) + jnp.log1p(jnp.exp(-jnp.abs(x))))


def _const_spec(shape):
    return pl.BlockSpec(shape, lambda *_: (0,) * len(shape), pipeline_mode=pl.Buffered(1))


def _inproj_kernel(x_ref, g1_ref, wt_ref, gb_ref, cw_ref,
                   q_ref, kt_ref, v_ref, o_ref, gatest_ref, u_ref,
                   ext_ref, wm_ref, wg_ref, wu_ref, *, tiles_per_seq, d_mlstm, k_scale):
    tm = x_ref.shape[0]
    dm = d_mlstm
    H = N_MLSTM_HEADS
    halo = SUBLANES

    @pl.when(pl.program_id(0) == 0)
    def _():
        for c0 in range(0, 4 * dm, dm):
            wm_ref[:, c0:c0 + dm] = wt_ref[c0:c0 + dm, :].T.astype(BF16)
        wu_ref[...] = wt_ref[4 * dm + 2 * H:, :].T.astype(BF16)
        fill = jnp.zeros((LANES - 2 * H, wt_ref.shape[1]), F32)
        wg_ref[...] = jnp.concatenate([wt_ref[4 * dm:4 * dm + 2 * H, :], fill], axis=0).T.astype(BF16)

    xn = _rmsnorm(x_ref[...], g1_ref[...]).astype(BF16)

    @pl.when(pl.program_id(0) % tiles_per_seq == 0)
    def _():
        ext_ref[0:halo, :] = jnp.zeros((halo, 2 * dm), F32)

    ext_ref[halo:halo + tm, :] = jnp.dot(xn, wm_ref[:, 0:2 * dm], preferred_element_type=F32)
    v = jnp.dot(xn, wm_ref[:, 2 * dm:3 * dm], preferred_element_type=F32)
    o = jnp.dot(xn, wm_ref[:, 3 * dm:4 * dm], preferred_element_type=F32)
    u = jnp.dot(xn, wu_ref[...], preferred_element_type=F32)
    gates = jnp.dot(xn, wg_ref[...], preferred_element_type=F32) + gb_ref[...]
    v_ref[...] = v.astype(BF16)
    o_ref[...] = o.astype(BF16)
    n_oct = u_ref.shape[0]
    u4 = u.reshape(tm // (n_oct * SUBLANES), n_oct, SUBLANES, u.shape[1])
    for oc in range(n_oct):
        u_ref[oc] = u4[:, oc].reshape(tm // n_oct, u.shape[1])
    gatest_ref[...] = gates.T[0:2 * H, :]

    def zero_of(t):
        w = pltpu.bitcast(t[tm - SUBLANES:tm, t.shape[1] - LANES:], jnp.uint32)
        return pltpu.bitcast((w >> 16) >> 16, F32)
    anchor = zero_of(v)

    acc = cw_ref[CONV_WIDTH - 1:CONV_WIDTH, :] * ext_ref[halo:halo + tm, :]
    for j in range(1, CONV_WIDTH):
        acc = acc + cw_ref[CONV_WIDTH - 1 - j:CONV_WIDTH - j, :] * ext_ref[halo - j:halo - j + tm, :]
    ext_ref[0:halo, :] = ext_ref[tm:tm + halo, :]
    qk = acc * _sigmoid(acc) + jnp.tile(anchor, (tm // SUBLANES, 2 * dm // LANES))
    q_ref[...] = qk[:, 0:dm].astype(BF16)
    kt_ref[...] = (qk[:, dm:2 * dm] * k_scale).T.astype(BF16)


def _inproj(x2, g1, w_in_t, gb, cw, *, seq, tm, d_mlstm, n_oct):
    T, D = x2.shape
    dm = d_mlstm
    H = N_MLSTM_HEADS
    assert 2 * H == SUBLANES
    ds5 = w_in_t.shape[0] - 4 * dm - 2 * H
    kern = functools.partial(_inproj_kernel, tiles_per_seq=seq // tm, d_mlstm=dm,
                             k_scale=float((dm // H) ** -0.5))
    tok = lambda w: pl.BlockSpec((tm, w), lambda i: (i, 0))
    return pl.pallas_call(
        kern,
        grid=(T // tm,),
        in_specs=[tok(D), _const_spec((1, D)), _const_spec(w_in_t.shape), _const_spec((1, LANES)),
                  _const_spec(cw.shape)],
        out_specs=[tok(dm), pl.BlockSpec((dm, tm), lambda i: (0, i)), tok(dm), tok(dm),
                   pl.BlockSpec((2 * H, tm), lambda i: (0, i)),
                   pl.BlockSpec((n_oct, tm // n_oct, ds5), lambda i: (0, i, 0))],
        out_shape=[jax.ShapeDtypeStruct((T, dm), BF16), jax.ShapeDtypeStruct((dm, T), BF16),
                   jax.ShapeDtypeStruct((T, dm), BF16), jax.ShapeDtypeStruct((T, dm), BF16),
                   jax.ShapeDtypeStruct((2 * H, T), F32),
                   jax.ShapeDtypeStruct((n_oct, T // n_oct, ds5), F32)],
        scratch_shapes=[pltpu.VMEM((tm + 2 * SUBLANES, 2 * dm), F32), pltpu.VMEM((D, 4 * dm), BF16),
                        pltpu.VMEM((D, LANES), BF16), pltpu.VMEM((D, ds5), BF16)],
        compiler_params=pltpu.CompilerParams(dimension_semantics=("arbitrary",),
                                             vmem_limit_bytes=VMEM_LIMIT),
        name="inproj",
    )(x2, g1, w_in_t, gb, cw)


def _lane_scan(x, op, identity):
    n = x.shape[-1]
    pos = lax.broadcasted_iota(jnp.int32, x.shape, x.ndim - 1)
    sh = 1
    while sh < n:
        x = op(x, jnp.where(pos >= sh, pltpu.roll(x, sh, x.ndim - 1), identity))
        sh *= 2
    return x


def _mlstm_kernel(q_ref, kt_ref, v_ref, o_ref, gatest_ref, gn_ref, *refs, chunk, n_cast):
    S = q_ref.shape[0]
    H = N_MLSTM_HEADS
    dh = q_ref.shape[1] // H
    L = chunk
    assert L == dh == LANES
    w32_refs, out_ref, w16_refs = refs[0:n_cast], refs[n_cast], refs[n_cast + 1:2 * n_cast + 1]
    a_ref, g0_ref, nm_ref, gend_ref = refs[2 * n_cast + 1:2 * n_cast + 5]
    c_refs, nd_refs = refs[2 * n_cast + 5:2 * n_cast + 5 + H], refs[2 * n_cast + 5 + H:]

    for w32_ref, w16_ref in zip(w32_refs, w16_refs):
        w16_ref[...] = w32_ref[...].astype(BF16)

    gates = gatest_ref[...]
    f_cum = pltpu.roll(_lane_scan(_log_sigmoid(gates), jnp.add, 0.0), H, 0)
    a = gates - f_cum
    g0 = jnp.maximum(_lane_scan(a, jnp.maximum, -jnp.inf), 0.0)
    a_ref[...] = a
    g0_ref[...] = g0
    nm_ref[...] = -(f_cum + g0)
    for c in range(S // L):
        gend_ref[:, c * L:(c + 1) * L] = jnp.broadcast_to(g0[:, (c + 1) * L - 1:(c + 1) * L], (SUBLANES, L))

    for c_ref in c_refs:
        c_ref[...] = jnp.zeros(c_ref.shape, F32)
    causal = (lax.broadcasted_iota(jnp.int32, (L, L), 0) >= lax.broadcasted_iota(jnp.int32, (L, L), 1))
    ones_blk = jnp.ones((L, dh), BF16)
    mean_mat = jnp.full((dh, dh), 1.0 / dh, BF16)

    def mix(c, h, g_prev):
        r0 = pl.multiple_of(c * L, L)
        hs = slice(h * dh, (h + 1) * dh)
        qc = q_ref[pl.ds(r0, L), hs]
        ktc = kt_ref[hs, pl.ds(r0, L)]
        v_aug = jnp.concatenate([v_ref[pl.ds(r0, L), hs], ones_blk], axis=1)
        a_row = a_ref[h:h + 1, pl.ds(r0, L)]
        g_end = gend_ref[h:h + 1, pl.ds(r0, L)]
        g0_t = jnp.broadcast_to(g0_ref[h:h + 1, pl.ds(r0, L)], (L, L)).T

        s_qk = jnp.dot(qc, ktc, preferred_element_type=F32)
        p = (jnp.exp(jnp.where(causal, a_row - g0_t, -jnp.inf)) * s_qk).astype(BF16)
        c_prev = c_refs[h][...]
        q_w = (qc.astype(F32) * jnp.exp(g_prev - g0_t)).astype(BF16)
        nd_refs[h][...] = jnp.dot(jnp.concatenate([q_w, p], axis=1),
                                  jnp.concatenate([c_prev.astype(BF16), v_aug], axis=0),
                                  preferred_element_type=F32)

        kte = (ktc.astype(F32) * jnp.exp(a_row - g_end)).astype(BF16)
        decay = jnp.exp(g_prev - g_end)
        c_refs[h][...] = (jnp.concatenate([decay, decay], axis=1) * c_prev
                          + jnp.dot(kte, v_aug, preferred_element_type=F32))
        return g_end

    def emit(c, h):
        r0 = pl.multiple_of(c * L, L)
        hs = slice(h * dh, (h + 1) * dh)
        nm_t = jnp.broadcast_to(nm_ref[h:h + 1, pl.ds(r0, L)], (L, L)).T
        nd = nd_refs[h][...]
        h_tilde = nd[:, 0:dh] / jnp.maximum(jnp.abs(nd[:, dh:2 * dh]), jnp.exp(nm_t))
        hm = _sigmoid(o_ref[pl.ds(r0, L), hs].astype(F32)) * h_tilde
        mu = jnp.dot(hm.astype(BF16), mean_mat, preferred_element_type=F32)
        hc = hm - mu
        var = jnp.dot((hc * hc).astype(BF16), mean_mat, preferred_element_type=F32)
        out_ref[pl.ds(r0, L), hs] = (hc * lax.rsqrt(var + EPS) * gn_ref[:, hs]).astype(BF16)

    zero = jnp.zeros((1, L), F32)
    g_first = tuple(mix(0, h, zero) for h in range(H))

    def body(c, g_carry):
        for h in range(H):
            emit(c - 1, h)
        return tuple(mix(c, h, g_carry[h]) for h in range(H))

    lax.fori_loop(1, S // L, body, g_first, unroll=3)
    for h in range(H):
        emit(S // L - 1, h)


def _mlstm(q, kt, v, o, gatest, gn, weights, *, batch, seq):
    T, dm = q.shape
    dh = dm // N_MLSTM_HEADS
    kern = functools.partial(_mlstm_kernel, chunk=MLSTM_CHUNK, n_cast=len(weights))
    tok = lambda w: pl.BlockSpec((seq, w), lambda b: (b, 0))
    rows = pltpu.VMEM((SUBLANES, seq), F32)
    bf16_rows = 2 * SUBLANES
    assert all(w.shape[0] % (batch * bf16_rows) == 0 for w in weights)
    w_specs = [pl.BlockSpec((w.shape[0] // batch, w.shape[1]), lambda b: (b, 0)) for w in weights]
    return pl.pallas_call(
        kern,
        grid=(batch,),
        in_specs=[tok(dm), pl.BlockSpec((dm, seq), lambda b: (0, b)), tok(dm), tok(dm),
                  pl.BlockSpec((SUBLANES, seq), lambda b: (0, b)), _const_spec((1, dm))] + w_specs,
        out_specs=[tok(dm)] + w_specs,
        out_shape=[jax.ShapeDtypeStruct((T, dm), BF16)] + [jax.ShapeDtypeStruct(w.shape, BF16) for w in weights],
        scratch_shapes=([rows, rows, rows, rows] + [pltpu.VMEM((dh, 2 * dh), F32)] * N_MLSTM_HEADS
                        + [pltpu.VMEM((MLSTM_CHUNK, 2 * dh), F32)] * N_MLSTM_HEADS),
        compiler_params=pltpu.CompilerParams(dimension_semantics=("arbitrary",),
                                             vmem_limit_bytes=VMEM_LIMIT),
        name="mlstm",
    )(q, kt, v, o, gatest, gn, *weights)


def _s5_prep_kernel(arow_ref, acol_ref, ldt_ref, bre_ref, bim_ref, cre_ref, cim_ref, dcol_ref,
                    rep_ref, selrev_ref, pw_ref, pwcol_ref, w1_ref, wout_ref, al_ref, *, chunk):
    L = chunk
    P = S5_STATE
    Hc = S5_GROUP
    R = L * Hc
    hi = lax.Precision.HIGHEST
    dt = jnp.exp(ldt_ref[...])

    def select(x, sel_ref):
        x_hi = x.astype(BF16)
        x_lo = (x - x_hi.astype(F32)).astype(BF16)
        sel = sel_ref[...]
        return (jnp.dot(x_hi, sel, preferred_element_type=F32) + jnp.dot(x_lo, sel, preferred_element_type=F32))

    def abar_pow(a_re, a_im, tau):
        mag = jnp.exp(tau * (dt * a_re))
        ang = tau * (dt * a_im)
        return mag * jnp.cos(ang), mag * jnp.sin(ang)

    a_re_c = acol_ref[:, 0:1]
    a_im_c = acol_ref[:, 1:2]
    ec_re, ec_im = abar_pow(a_re_c, a_im_c, pw_ref[...])

    ab_re = ec_re[:, 1:2]
    ab_im = ec_im[:, 1:2]
    den = a_re_c * a_re_c + a_im_c * a_im_c
    nr = ab_re - 1.0
    z_re = (nr * a_re_c + ab_im * a_im_c) / den
    z_im = (ab_im * a_re_c - nr * a_im_c) / den
    bb_re = z_re * bre_ref[...] - z_im * bim_ref[...]
    bb_im = z_re * bim_ref[...] + z_im * bre_ref[...]

    n_pow = -(-(L + 1) // SUBLANES) * SUBLANES
    tau_r = lax.broadcasted_iota(jnp.int32, (n_pow, P), 0).astype(F32)
    er_re, er_im = abar_pow(arow_ref[0:1, :], arow_ref[1:2, :], tau_r)

    def rows(e, first):
        return jnp.concatenate([jnp.broadcast_to(e[first + t:first + t + 1, :], (Hc, P)) for t in range(L)], axis=0)

    c_re = jnp.concatenate([cre_ref[...]] * L, axis=0)
    c_im = jnp.concatenate([cim_ref[...]] * L, axis=0)

    def c_times_pow(first):
        e_re, e_im = rows(er_re, first), rows(er_im, first)
        return c_re * e_re - c_im * e_im, c_re * e_im + c_im * e_re

    ce_re, ce_im = c_times_pow(0)
    kflat = (jnp.dot(ce_re, bb_re, preferred_element_type=F32, precision=hi)
             - jnp.dot(ce_im, bb_im, preferred_element_type=F32, precision=hi))
    row = lax.broadcasted_iota(jnp.int32, (R, Hc), 0)
    col = lax.broadcasted_iota(jnp.int32, (R, Hc), 1)
    kflat = kflat + jnp.where(row == col, jnp.concatenate([dcol_ref[...]] * L, axis=0), 0.0)

    m = select(kflat, rep_ref)
    blk = lax.broadcasted_iota(jnp.int32, (R, R), 1) // Hc
    sh = 1
    while sh < L:
        shifted = jnp.concatenate([jnp.zeros((sh * Hc, R), F32), m[0:R - sh * Hc, :]], axis=0)
        m = jnp.where((blk & sh) != 0, shifted, m)
        sh *= 2
    w1_ref[0:R, :] = m.astype(BF16)

    ev_re, ev_im = select(ec_re, selrev_ref), select(ec_im, selrev_ref)
    bt_re, bt_im = select(bb_re, rep_ref), select(bb_im, rep_ref)
    w1_ref[R:R + P, :] = (ev_re * bt_re - ev_im * bt_im).astype(BF16)
    w1_ref[R + P:R + 2 * P, :] = (ev_re * bt_im + ev_im * bt_re).astype(BF16)

    co_re, co_im = c_times_pow(1)
    zero = jnp.zeros((R, P), F32)
    odd = pl.program_id(0) % 2 == 1
    halves = lambda v: jnp.where(odd, jnp.concatenate([zero, v], axis=1), jnp.concatenate([v, zero], axis=1))
    wout_ref[:, 0:2 * P] = halves(co_re).astype(BF16)
    wout_ref[:, 2 * P:4 * P] = halves(-co_im).astype(BF16)

    sc_re, sc_im = abar_pow(arow_ref[0:1, :], arow_ref[1:2, :], pwcol_ref[...])
    al_ref[0:SUBLANES, :] = sc_re
    al_ref[SUBLANES:2 * SUBLANES, :] = sc_im


def _s5_prep(a_re, a_im, log_dt, b_re, b_im, c_re, c_im, d_skip, *, chunk, n_chunks):
    G, P = a_re.shape
    Hc = S5_GROUP
    R = chunk * Hc
    arow = jnp.stack([a_re, a_im], axis=1)
    acol = jnp.stack([a_re, a_im], axis=2)
    ldt = log_dt.reshape(G, 1, 1)
    dcol = d_skip.reshape(G, Hc, 1)
    rep = (jnp.arange(R)[None, :] % Hc == jnp.arange(Hc)[:, None]).astype(BF16)
    selrev = (jnp.arange(LANES)[:, None] == chunk - 1 - jnp.arange(R)[None, :] // Hc).astype(BF16)
    n_steps = max(1, (n_chunks - 1).bit_length())
    assert chunk <= LANES and n_steps <= SUBLANES
    pw = jnp.zeros((1, LANES), F32).at[0, :chunk].set(jnp.arange(chunk, dtype=F32))
    pwcol = jnp.zeros((SUBLANES, 1), F32).at[:n_steps, 0].set(chunk * 2.0 ** jnp.arange(n_steps))
    kern = functools.partial(_s5_prep_kernel, chunk=chunk)
    grp = lambda *s: pl.BlockSpec((None,) + s, lambda g: (g,) + (0,) * len(s))
    return pl.pallas_call(
        kern,
        grid=(G,),
        in_specs=[grp(2, P), grp(P, 2), grp(1, 1), grp(P, Hc), grp(P, Hc), grp(Hc, P), grp(Hc, P),
                  grp(Hc, 1), _const_spec((Hc, R)), _const_spec((LANES, R)), _const_spec((1, LANES)),
                  _const_spec((SUBLANES, 1))],
        out_specs=[grp(R + 2 * P, R), grp(R, 4 * P), grp(2 * SUBLANES, P)],
        out_shape=[jax.ShapeDtypeStruct((G, R + 2 * P, R), BF16),
                   jax.ShapeDtypeStruct((G, R, 4 * P), BF16),
                   jax.ShapeDtypeStruct((G, 2 * SUBLANES, P), F32)],
        compiler_params=pltpu.CompilerParams(dimension_semantics=("arbitrary",),
                                             vmem_limit_bytes=VMEM_LIMIT),
        name="s5_prep",
    )(arow, acol, ldt, b_re, b_im, c_re, c_im, dcol, rep, selrev, pw, pwcol)


def _s5_kernel(*refs, chunk, n_chunks):
    L = chunk
    n_oct = L // SUBLANES
    u_refs, (w1_ref, wout_ref, al_ref, y_ref, ut_ref, yt_ref) = refs[0:n_oct], refs[n_oct:]
    P = S5_STATE
    Hc = S5_GROUP
    R = L * Hc
    C = y_ref.shape[0] // L
    n_grp = LANES // Hc

    for s in range(L):
        xt = u_refs[s // SUBLANES][pl.ds(s % SUBLANES, C, stride=SUBLANES), :].astype(BF16).T
        for j in range(n_grp):
            ut_ref[j, s * Hc:(s + 1) * Hc, :] = xt[j * Hc:(j + 1) * Hc, :]

    seg_pos = lax.broadcasted_iota(jnp.int32, (C, 2 * P), 0) & (n_chunks - 1)
    nt = (((1,), (1,)), ((), ()))

    def shifted(x, sh):
        return jnp.where(seg_pos >= sh, pltpu.roll(x, sh, 0), 0.0)

    def group_pair(i, carry):
        j0 = 2 * i
        res = [jnp.dot(w1_ref[j0 + d], ut_ref[j0 + d], preferred_element_type=F32) for d in range(2)]
        x_re = jnp.concatenate([res[0][R:R + P], res[1][R:R + P]], axis=0).T
        x_im = jnp.concatenate([res[0][R + P:R + 2 * P], res[1][R + P:R + 2 * P]], axis=0).T
        sh, k = 1, 0
        while sh < n_chunks:
            a_re = jnp.concatenate([al_ref[j0, k:k + 1, :], al_ref[j0 + 1, k:k + 1, :]], axis=1)
            a_im = jnp.concatenate([al_ref[j0, SUBLANES + k:SUBLANES + k + 1, :],
                                    al_ref[j0 + 1, SUBLANES + k:SUBLANES + k + 1, :]], axis=1)
            s_re, s_im = shifted(x_re, sh), shifted(x_im, sh)
            x_re, x_im = (x_re + a_re * s_re - a_im * s_im, x_im + a_re * s_im + a_im * s_re)
            sh *= 2
            k += 1
        xprev = jnp.concatenate([shifted(x_re, 1), shifted(x_im, 1)], axis=1).astype(BF16)
        for d in range(2):
            yt_ref[j0 + d] = res[d][0:R, :] + lax.dot_general(wout_ref[j0 + d], xprev, nt,
                                                              preferred_element_type=F32)
        return carry

    lax.fori_loop(0, n_grp // 2, group_pair, 0, unroll=2)

    for t in range(L):
        zt = jnp.concatenate([yt_ref[j, t * Hc:(t + 1) * Hc, :] for j in range(n_grp)], axis=0)
        y_ref[pl.ds(t, C, stride=L), :] = zt.T


def _s5(u, w1, wout, al, *, chunk, n_chunks, n_col_blocks):
    n_oct, t_oct, ds5 = u.shape
    T = n_oct * t_oct
    assert n_oct * SUBLANES == chunk
    G, R1, R = w1.shape
    P4 = wout.shape[-1]
    n_grp = LANES // S5_GROUP
    n_q = ds5 // LANES
    tb = T // n_col_blocks
    C = tb // chunk
    kern = functools.partial(_s5_kernel, chunk=chunk, n_chunks=n_chunks)
    grp = lambda *s: pl.BlockSpec((n_grp,) + s, lambda q, i: (q,) + (0,) * len(s))
    octets = [pl.BlockSpec((None, C * SUBLANES, LANES), functools.partial(lambda q, i, o: (o, i, q), o=o))
              for o in range(n_oct)]
    return pl.pallas_call(
        kern,
        grid=(n_q, n_col_blocks),
        in_specs=octets + [grp(R1, R), grp(R, P4), grp(2 * SUBLANES, P4 // 4)],
        out_specs=pl.BlockSpec((tb, LANES), lambda q, i: (i, q)),
        out_shape=jax.ShapeDtypeStruct((T, ds5), F32),
        scratch_shapes=[pltpu.VMEM((n_grp, R, C), BF16), pltpu.VMEM((n_grp, R, C), F32)],
        compiler_params=pltpu.CompilerParams(dimension_semantics=("arbitrary", "arbitrary"),
                                             vmem_limit_bytes=VMEM_LIMIT),
        name="s5",
    )(*([u] * n_oct), w1, wout, al)


def _out_kernel(x_ref, hm_ref, y_ref, wglu_ref, bglu_ref, gs5_ref, wo_ref, g2_ref,
                wgate_ref, wup_ref, wdown_ref, gf_ref, out_ref, acc_ref, xn_ref, *, ff_chunk, final_norm):
    dm = hm_ref.shape[1]
    g = jax.nn.gelu(y_ref[...])
    z = jnp.dot(g.astype(BF16), wglu_ref[...], preferred_element_type=F32) + bglu_ref[...]
    hs = _rmsnorm(g * _sigmoid(z), gs5_ref[...]).astype(BF16)
    h1 = (x_ref[...] + jnp.dot(hm_ref[...], wo_ref[0:dm, :], preferred_element_type=F32)
          + jnp.dot(hs, wo_ref[dm:, :], preferred_element_type=F32))
    xn_ref[...] = _rmsnorm(h1, g2_ref[...]).astype(BF16)
    acc_ref[...] = h1

    def ffn_chunk(c, carry):
        cs = pl.ds(pl.multiple_of(c * ff_chunk, ff_chunk), ff_chunk)
        xn2 = xn_ref[...]
        gate = jnp.dot(xn2, wgate_ref[:, cs], preferred_element_type=F32)
        up = jnp.dot(xn2, wup_ref[:, cs], preferred_element_type=F32)
        act = (gate * _sigmoid(gate) * up).astype(BF16)
        acc_ref[...] += jnp.dot(act, wdown_ref[cs, :], preferred_element_type=F32)
        return carry

    lax.fori_loop(0, wgate_ref.shape[1] // ff_chunk, ffn_chunk, 0, unroll=4)
    out_ref[...] = _rmsnorm(acc_ref[...], gf_ref[...]) if final_norm else acc_ref[...]


def _out_block(x2, hm, y, wglu, bglu, gs5, wo, g2, wgate, wup, wdown, gf, *, tm, ff_chunk, final_norm):
    T, D = x2.shape
    dm = hm.shape[1]
    ds5 = y.shape[1]
    assert wgate.shape[1] % ff_chunk == 0
    kern = functools.partial(_out_kernel, ff_chunk=ff_chunk, final_norm=final_norm)
    tok = lambda w: pl.BlockSpec((tm, w), lambda i: (i, 0))
    return pl.pallas_call(
        kern,
        grid=(T // tm,),
        in_specs=[tok(D), tok(dm), tok(ds5), _const_spec(wglu.shape), _const_spec((1, ds5)),
                  _const_spec((1, ds5)), _const_spec(wo.shape), _const_spec((1, D)),
                  _const_spec(wgate.shape), _const_spec(wup.shape), _const_spec(wdown.shape),
                  _const_spec((1, D))],
        out_specs=tok(D),
        out_shape=jax.ShapeDtypeStruct((T, D), F32),
        scratch_shapes=[pltpu.VMEM((tm, D), F32), pltpu.VMEM((tm, D), BF16)],
        compiler_params=pltpu.CompilerParams(dimension_semantics=("arbitrary",),
                                             vmem_limit_bytes=OUT_VMEM_LIMIT),
        name="out_block",
    )(x2, hm, y, wglu, bglu, gs5, wo, g2, wgate, wup, wdown, gf)


def kernel(x, norm1_g, w_in, if_bias, conv_qk, mlstm_norm_g, a_re, a_im, log_dt, b_re, b_im, c_re, c_im,
           d_skip, w_glu, b_glu, s5_norm_g, w_out, norm2_g, w_gate, w_up, w_down, norm_f_g):
    B, S, D = x.shape
    depth = w_in.shape[0]
    H = N_MLSTM_HEADS
    dm = mlstm_norm_g.shape[1]
    T = B * S
    L5 = S5_CHUNK
    nc5 = S // L5
    row = lambda a: a.reshape(1, -1).astype(F32)

    h = x.reshape(T, D)
    for l in range(depth):
        last = l == depth - 1
        gb = jnp.zeros((1, LANES), F32).at[0, 0:2 * H].set(if_bias[l])

        q, kt, v, o, gatest, u = _inproj(h, row(norm1_g[l]), w_in[l].T.astype(F32), gb, conv_qk[l].astype(F32),
                                         seq=S, tm=min(INPROJ_TM, S), d_mlstm=dm, n_oct=L5 // SUBLANES)
        hm, wglu, wo, wgate, wup, wdown = _mlstm(
            q, kt, v, o, gatest, row(mlstm_norm_g[l]),
            [w.astype(F32) for w in (w_glu[l], w_out[l], w_gate[l], w_up[l], w_down[l])], batch=B, seq=S)

        w1, wout, al = _s5_prep(a_re[l], a_im[l], log_dt[l], b_re[l], b_im[l], c_re[l], c_im[l],
                                d_skip[l], chunk=L5, n_chunks=nc5)
        y = _s5(u, w1, wout, al, chunk=L5, n_chunks=nc5, n_col_blocks=S5_COL_BLOCKS)

        h = _out_block(h, hm, y, wglu, row(b_glu[l]), row(s5_norm_g[l]), wo, row(norm2_g[l]),
                       wgate, wup, wdown, row(norm_f_g), tm=OUT_TM, ff_chunk=FF_CHUNK, final_norm=last)
    return h.reshape(B, S, D)
```

```python
import functools

import jax
import jax.numpy as jnp
from jax import lax
from jax.experimental import pallas as pl
from jax.experimental.pallas import tpu as pltpu

EPS = 1e-6
N_MLSTM_HEADS = 4
CONV_WIDTH = 4
S5_GROUP = 16
S5_STATE = 64

LANES = 128
SUBLANES = 8
MLSTM_CHUNK = 128
S5_CHUNK = 16
S5_COL_BLOCKS = 2
INPROJ_TM = 1024
OUT_TM = 1024
FF_CHUNK = 256
VMEM_LIMIT = 48 * 1024 * 1024
OUT_VMEM_LIMIT = 58 * 1024 * 1024

F32 = jnp.float32
BF16 = jnp.bfloat16


def _rmsnorm(x, g):
    ms = jnp.mean(x * x, axis=-1, keepdims=True)
    return x * lax.rsqrt(ms + EPS) * g


def _sigmoid(x):
    return 1.0 / (1.0 + jnp.exp(-x))


def _log_sigmoid(x):
    return -(jnp.maximum(-x, 0.0) + jnp.log1p(jnp.exp(-jnp.abs(x))))


def _const_spec(shape):
    return pl.BlockSpec(shape, lambda *_: (0,) * len(shape), pipeline_mode=pl.Buffered(1))


def _inproj_kernel(x_ref, g1_ref, wt_ref, gb_ref, cw_ref,
                   q_ref, kt_ref, v_ref, o_ref, gatest_ref, u_ref,
                   ext_ref, wm_ref, wg_ref, wu_ref, *, tiles_per_seq, d_mlstm, k_scale):
    tm = x_ref.shape[0]
    dm = d_mlstm
    H = N_MLSTM_HEADS
    halo = SUBLANES

    @pl.when(pl.program_id(0) == 0)
    def _():
        for c0 in range(0, 4 * dm, dm):
            wm_ref[:, c0:c0 + dm] = wt_ref[c0:c0 + dm, :].T.astype(BF16)
        wu_ref[...] = wt_ref[4 * dm + 2 * H:, :].T.astype(BF16)
        fill = jnp.zeros((LANES - 2 * H, wt_ref.shape[1]), F32)
        wg_ref[...] = jnp.concatenate([wt_ref[4 * dm:4 * dm + 2 * H, :], fill], axis=0).T.astype(BF16)

    xn = _rmsnorm(x_ref[...], g1_ref[...]).astype(BF16)

    @pl.when(pl.program_id(0) % tiles_per_seq == 0)
    def _():
        ext_ref[0:halo, :] = jnp.zeros((halo, 2 * dm), F32)

    ext_ref[halo:halo + tm, :] = jnp.dot(xn, wm_ref[:, 0:2 * dm], preferred_element_type=F32)
    v = jnp.dot(xn, wm_ref[:, 2 * dm:3 * dm], preferred_element_type=F32)
    o = jnp.dot(xn, wm_ref[:, 3 * dm:4 * dm], preferred_element_type=F32)
    u = jnp.dot(xn, wu_ref[...], preferred_element_type=F32)
    gates = jnp.dot(xn, wg_ref[...], preferred_element_type=F32) + gb_ref[...]
    v_ref[...] = v.astype(BF16)
    o_ref[...] = o.astype(BF16)
    n_oct = u_ref.shape[0]
    u4 = u.reshape(tm // (n_oct * SUBLANES), n_oct, SUBLANES, u.shape[1])
    for oc in range(n_oct):
        u_ref[oc] = u4[:, oc].reshape(tm // n_oct, u.shape[1])
    gatest_ref[...] = gates.T[0:2 * H, :]

    def zero_of(t):
        w = pltpu.bitcast(t[tm - SUBLANES:tm, t.shape[1] - LANES:], jnp.uint32)
        return pltpu.bitcast((w >> 16) >> 16, F32)
    anchor = zero_of(v)

    acc = cw_ref[CONV_WIDTH - 1:CONV_WIDTH, :] * ext_ref[halo:halo + tm, :]
    for j in range(1, CONV_WIDTH):
        acc = acc + cw_ref[CONV_WIDTH - 1 - j:CONV_WIDTH - j, :] * ext_ref[halo - j:halo - j + tm, :]
    ext_ref[0:halo, :] = ext_ref[tm:tm + halo, :]
    qk = acc * _sigmoid(acc) + jnp.tile(anchor, (tm // SUBLANES, 2 * dm // LANES))
    q_ref[...] = qk[:, 0:dm].astype(BF16)
    kt_ref[...] = (qk[:, dm:2 * dm] * k_scale).T.astype(BF16)


def _inproj(x2, g1, w_in_t, gb, cw, *, seq, tm, d_mlstm, n_oct):
    T, D = x2.shape
    dm = d_mlstm
    H = N_MLSTM_HEADS
    assert 2 * H == SUBLANES
    ds5 = w_in_t.shape[0] - 4 * dm - 2 * H
    kern = functools.partial(_inproj_kernel, tiles_per_seq=seq // tm, d_mlstm=dm,
                             k_scale=float((dm // H) ** -0.5))
    tok = lambda w: pl.BlockSpec((tm, w), lambda i: (i, 0))
    return pl.pallas_call(
        kern,
        grid=(T // tm,),
        in_specs=[tok(D), _const_spec((1, D)), _const_spec(w_in_t.shape), _const_spec((1, LANES)),
                  _const_spec(cw.shape)],
        out_specs=[tok(dm), pl.BlockSpec((dm, tm), lambda i: (0, i)), tok(dm), tok(dm),
                   pl.BlockSpec((2 * H, tm), lambda i: (0, i)),
                   pl.BlockSpec((n_oct, tm // n_oct, ds5), lambda i: (0, i, 0))],
        out_shape=[jax.ShapeDtypeStruct((T, dm), BF16), jax.ShapeDtypeStruct((dm, T), BF16),
                   jax.ShapeDtypeStruct((T, dm), BF16), jax.ShapeDtypeStruct((T, dm), BF16),
                   jax.ShapeDtypeStruct((2 * H, T), F32),
                   jax.ShapeDtypeStruct((n_oct, T // n_oct, ds5), F32)],
        scratch_shapes=[pltpu.VMEM((tm + 2 * SUBLANES, 2 * dm), F32), pltpu.VMEM((D, 4 * dm), BF16),
                        pltpu.VMEM((D, LANES), BF16), pltpu.VMEM((D, ds5), BF16)],
        compiler_params=pltpu.CompilerParams(dimension_semantics=("arbitrary",),
                                             vmem_limit_bytes=VMEM_LIMIT),
        name="inproj",
    )(x2, g1, w_in_t, gb, cw)


def _lane_scan(x, op, identity):
    n = x.shape[-1]
    pos = lax.broadcasted_iota(jnp.int32, x.shape, x.ndim - 1)
    sh = 1
    while sh < n:
        x = op(x, jnp.where(pos >= sh, pltpu.roll(x, sh, x.ndim - 1), identity))
        sh *= 2
    return x


def _mlstm_kernel(q_ref, kt_ref, v_ref, o_ref, gatest_ref, gn_ref, *refs, chunk, n_cast):
    S = q_ref.shape[0]
    H = N_MLSTM_HEADS
    dh = q_ref.shape[1] // H
    L = chunk
    assert L == dh == LANES
    w32_refs, out_ref, w16_refs = refs[0:n_cast], refs[n_cast], refs[n_cast + 1:2 * n_cast + 1]
    a_ref, g0_ref, nm_ref, gend_ref = refs[2 * n_cast + 1:2 * n_cast + 5]
    c_refs, nd_refs = refs[2 * n_cast + 5:2 * n_cast + 5 + H], refs[2 * n_cast + 5 + H:]

    for w32_ref, w16_ref in zip(w32_refs, w16_refs):
        w16_ref[...] = w32_ref[...].astype(BF16)

    gates = gatest_ref[...]
    f_cum = pltpu.roll(_lane_scan(_log_sigmoid(gates), jnp.add, 0.0), H, 0)
    a = gates - f_cum
    g0 = jnp.maximum(_lane_scan(a, jnp.maximum, -jnp.inf), 0.0)
    a_ref[...] = a
    g0_ref[...] = g0
    nm_ref[...] = -(f_cum + g0)
    for c in range(S // L):
        gend_ref[:, c * L:(c + 1) * L] = jnp.broadcast_to(g0[:, (c + 1) * L - 1:(c + 1) * L], (SUBLANES, L))

    for c_ref in c_refs:
        c_ref[...] = jnp.zeros(c_ref.shape, F32)
    causal = (lax.broadcasted_iota(jnp.int32, (L, L), 0) >= lax.broadcasted_iota(jnp.int32, (L, L), 1))
    ones_blk = jnp.ones((L, dh), BF16)
    mean_mat = jnp.full((dh, dh), 1.0 / dh, BF16)

    def mix(c, h, g_prev):
        r0 = pl.multiple_of(c * L, L)
        hs = slice(h * dh, (h + 1) * dh)
        qc = q_ref[pl.ds(r0, L), hs]
        ktc = kt_ref[hs, pl.ds(r0, L)]
        v_aug = jnp.concatenate([v_ref[pl.ds(r0, L), hs], ones_blk], axis=1)
        a_row = a_ref[h:h + 1, pl.ds(r0, L)]
        g_end = gend_ref[h:h + 1, pl.ds(r0, L)]
        g0_t = jnp.broadcast_to(g0_ref[h:h + 1, pl.ds(r0, L)], (L, L)).T

        s_qk = jnp.dot(qc, ktc, preferred_element_type=F32)
        p = (jnp.exp(jnp.where(causal, a_row - g0_t, -jnp.inf)) * s_qk).astype(BF16)
        c_prev = c_refs[h][...]
        q_w = (qc.astype(F32) * jnp.exp(g_prev - g0_t)).astype(BF16)
        nd_refs[h][...] = jnp.dot(jnp.concatenate([q_w, p], axis=1),
                                  jnp.concatenate([c_prev.astype(BF16), v_aug], axis=0),
                                  preferred_element_type=F32)

        kte = (ktc.astype(F32) * jnp.exp(a_row - g_end)).astype(BF16)
        decay = jnp.exp(g_prev - g_end)
        c_refs[h][...] = (jnp.concatenate([decay, decay], axis=1) * c_prev
                          + jnp.dot(kte, v_aug, preferred_element_type=F32))
        return g_end

    def emit(c, h):
        r0 = pl.multiple_of(c * L, L)
        hs = slice(h * dh, (h + 1) * dh)
        nm_t = jnp.broadcast_to(nm_ref[h:h + 1, pl.ds(r0, L)], (L, L)).T
        nd = nd_refs[h][...]
        h_tilde = nd[:, 0:dh] / jnp.maximum(jnp.abs(nd[:, dh:2 * dh]), jnp.exp(nm_t))
        hm = _sigmoid(o_ref[pl.ds(r0, L), hs].astype(F32)) * h_tilde
        mu = jnp.dot(hm.astype(BF16), mean_mat, preferred_element_type=F32)
        hc = hm - mu
        var = jnp.dot((hc * hc).astype(BF16), mean_mat, preferred_element_type=F32)
        out_ref[pl.ds(r0, L), hs] = (hc * lax.rsqrt(var + EPS) * gn_ref[:, hs]).astype(BF16)

    zero = jnp.zeros((1, L), F32)
    g_first = tuple(mix(0, h, zero) for h in range(H))

    def body(c, g_carry):
        for h in range(H):
            emit(c - 1, h)
        return tuple(mix(c, h, g_carry[h]) for h in range(H))

    lax.fori_loop(1, S // L, body, g_first, unroll=3)
    for h in range(H):
        emit(S // L - 1, h)


def _mlstm(q, kt, v, o, gatest, gn, weights, *, batch, seq):
    T, dm = q.shape
    dh = dm // N_MLSTM_HEADS
    kern = functools.partial(_mlstm_kernel, chunk=MLSTM_CHUNK, n_cast=len(weights))
    tok = lambda w: pl.BlockSpec((seq, w), lambda b: (b, 0))
    rows = pltpu.VMEM((SUBLANES, seq), F32)
    bf16_rows = 2 * SUBLANES
    assert all(w.shape[0] % (batch * bf16_rows) == 0 for w in weights)
    w_specs = [pl.BlockSpec((w.shape[0] // batch, w.shape[1]), lambda b: (b, 0)) for w in weights]
    return pl.pallas_call(
        kern,
        grid=(batch,),
        in_specs=[tok(dm), pl.BlockSpec((dm, seq), lambda b: (0, b)), tok(dm), tok(dm),
                  pl.BlockSpec((SUBLANES, seq), lambda b: (0, b)), _const_spec((1, dm))] + w_specs,
        out_specs=[tok(dm)] + w_specs,
        out_shape=[jax.ShapeDtypeStruct((T, dm), BF16)] + [jax.ShapeDtypeStruct(w.shape, BF16) for w in weights],
        scratch_shapes=([rows, rows, rows, rows] + [pltpu.VMEM((dh, 2 * dh), F32)] * N_MLSTM_HEADS
                        + [pltpu.VMEM((MLSTM_CHUNK, 2 * dh), F32)] * N_MLSTM_HEADS),
        compiler_params=pltpu.CompilerParams(dimension_semantics=("arbitrary",),
                                             vmem_limit_bytes=VMEM_LIMIT),
        name="mlstm",
    )(q, kt, v, o, gatest, gn, *weights)


def _s5_prep_kernel(arow_ref, acol_ref, ldt_ref, bre_ref, bim_ref, cre_ref, cim_ref, dcol_ref,
                    rep_ref, selrev_ref, pw_ref, pwcol_ref, w1_ref, wout_ref, al_ref, *, chunk):
    L = chunk
    P = S5_STATE
    Hc = S5_GROUP
    R = L * Hc
    hi = lax.Precision.HIGHEST
    dt = jnp.exp(ldt_ref[...])

    def select(x, sel_ref):
        x_hi = x.astype(BF16)
        x_lo = (x - x_hi.astype(F32)).astype(BF16)
        sel = sel_ref[...]
        return (jnp.dot(x_hi, sel, preferred_element_type=F32) + jnp.dot(x_lo, sel, preferred_element_type=F32))

    def abar_pow(a_re, a_im, tau):
        mag = jnp.exp(tau * (dt * a_re))
        ang = tau * (dt * a_im)
        return mag * jnp.cos(ang), mag * jnp.sin(ang)

    a_re_c = acol_ref[:, 0:1]
    a_im_c = acol_ref[:, 1:2]
    ec_re, ec_im = abar_pow(a_re_c, a_im_c, pw_ref[...])

    ab_re = ec_re[:, 1:2]
    ab_im = ec_im[:, 1:2]
    den = a_re_c * a_re_c + a_im_c * a_im_c
    nr = ab_re - 1.0
    z_re = (nr * a_re_c + ab_im * a_im_c) / den
    z_im = (ab_im * a_re_c - nr * a_im_c) / den
    bb_re = z_re * bre_ref[...] - z_im * bim_ref[...]
    bb_im = z_re * bim_ref[...] + z_im * bre_ref[...]

    n_pow = -(-(L + 1) // SUBLANES) * SUBLANES
    tau_r = lax.broadcasted_iota(jnp.int32, (n_pow, P), 0).astype(F32)
    er_re, er_im = abar_pow(arow_ref[0:1, :], arow_ref[1:2, :], tau_r)

    def rows(e, first):
        return jnp.concatenate([jnp.broadcast_to(e[first + t:first + t + 1, :], (Hc, P)) for t in range(L)], axis=0)

    c_re = jnp.concatenate([cre_ref[...]] * L, axis=0)
    c_im = jnp.concatenate([cim_ref[...]] * L, axis=0)

    def c_times_pow(first):
        e_re, e_im = rows(er_re, first), rows(er_im, first)
        return c_re * e_re - c_im * e_im, c_re * e_im + c_im * e_re

    ce_re, ce_im = c_times_pow(0)
    kflat = (jnp.dot(ce_re, bb_re, preferred_element_type=F32, precision=hi)
             - jnp.dot(ce_im, bb_im, preferred_element_type=F32, precision=hi))
    row = lax.broadcasted_iota(jnp.int32, (R, Hc), 0)
    col = lax.broadcasted_iota(jnp.int32, (R, Hc), 1)
    kflat = kflat + jnp.where(row == col, jnp.concatenate([dcol_ref[...]] * L, axis=0), 0.0)

    m = select(kflat, rep_ref)
    blk = lax.broadcasted_iota(jnp.int32, (R, R), 1) // Hc
    sh = 1
    while sh < L:
        shifted = jnp.concatenate([jnp.zeros((sh * Hc, R), F32), m[0:R - sh * Hc, :]], axis=0)
        m = jnp.where((blk & sh) != 0, shifted, m)
        sh *= 2
    w1_ref[0:R, :] = m.astype(BF16)

    ev_re, ev_im = select(ec_re, selrev_ref), select(ec_im, selrev_ref)
    bt_re, bt_im = select(bb_re, rep_ref), select(bb_im, rep_ref)
    w1_ref[R:R + P, :] = (ev_re * bt_re - ev_im * bt_im).astype(BF16)
    w1_ref[R + P:R + 2 * P, :] = (ev_re * bt_im + ev_im * bt_re).astype(BF16)

    co_re, co_im = c_times_pow(1)
    zero = jnp.zeros((R, P), F32)
    odd = pl.program_id(0) % 2 == 1
    halves = lambda v: jnp.where(odd, jnp.concatenate([zero, v], axis=1), jnp.concatenate([v, zero], axis=1))
    wout_ref[:, 0:2 * P] = halves(co_re).astype(BF16)
    wout_ref[:, 2 * P:4 * P] = halves(-co_im).astype(BF16)

    sc_re, sc_im = abar_pow(arow_ref[0:1, :], arow_ref[1:2, :], pwcol_ref[...])
    al_ref[0:SUBLANES, :] = sc_re
    al_ref[SUBLANES:2 * SUBLANES, :] = sc_im


def _s5_prep(a_re, a_im, log_dt, b_re, b_im, c_re, c_im, d_skip, *, chunk, n_chunks):
    G, P = a_re.shape
    Hc = S5_GROUP
    R = chunk * Hc
    arow = jnp.stack([a_re, a_im], axis=1)
    acol = jnp.stack([a_re, a_im], axis=2)
    ldt = log_dt.reshape(G, 1, 1)
    dcol = d_skip.reshape(G, Hc, 1)
    rep = (jnp.arange(R)[None, :] % Hc == jnp.arange(Hc)[:, None]).astype(BF16)
    selrev = (jnp.arange(LANES)[:, None] == chunk - 1 - jnp.arange(R)[None, :] // Hc).astype(BF16)
    n_steps = max(1, (n_chunks - 1).bit_length())
    assert chunk <= LANES and n_steps <= SUBLANES
    pw = jnp.zeros((1, LANES), F32).at[0, :chunk].set(jnp.arange(chunk, dtype=F32))
    pwcol = jnp.zeros((SUBLANES, 1), F32).at[:n_steps, 0].set(chunk * 2.0 ** jnp.arange(n_steps))
    kern = functools.partial(_s5_prep_kernel, chunk=chunk)
    grp = lambda *s: pl.BlockSpec((None,) + s, lambda g: (g,) + (0,) * len(s))
    return pl.pallas_call(
        kern,
        grid=(G,),
        in_specs=[grp(2, P), grp(P, 2), grp(1, 1), grp(P, Hc), grp(P, Hc), grp(Hc, P), grp(Hc, P),
                  grp(Hc, 1), _const_spec((Hc, R)), _const_spec((LANES, R)), _const_spec((1, LANES)),
                  _const_spec((SUBLANES, 1))],
        out_specs=[grp(R + 2 * P, R), grp(R, 4 * P), grp(2 * SUBLANES, P)],
        out_shape=[jax.ShapeDtypeStruct((G, R + 2 * P, R), BF16),
                   jax.ShapeDtypeStruct((G, R, 4 * P), BF16),
                   jax.ShapeDtypeStruct((G, 2 * SUBLANES, P), F32)],
        compiler_params=pltpu.CompilerParams(dimension_semantics=("arbitrary",),
                                             vmem_limit_bytes=VMEM_LIMIT),
        name="s5_prep",
    )(arow, acol, ldt, b_re, b_im, c_re, c_im, dcol, rep, selrev, pw, pwcol)


def _s5_kernel(*refs, chunk, n_chunks):
    L = chunk
    n_oct = L // SUBLANES
    u_refs, (w1_ref, wout_ref, al_ref, y_ref, ut_ref, yt_ref) = refs[0:n_oct], refs[n_oct:]
    P = S5_STATE
    Hc = S5_GROUP
    R = L * Hc
    C = y_ref.shape[0] // L
    n_grp = LANES // Hc

    for s in range(L):
        xt = u_refs[s // SUBLANES][pl.ds(s % SUBLANES, C, stride=SUBLANES), :].astype(BF16).T
        for j in range(n_grp):
            ut_ref[j, s * Hc:(s + 1) * Hc, :] = xt[j * Hc:(j + 1) * Hc, :]

    seg_pos = lax.broadcasted_iota(jnp.int32, (C, 2 * P), 0) & (n_chunks - 1)
    nt = (((1,), (1,)), ((), ()))

    def shifted(x, sh):
        return jnp.where(seg_pos >= sh, pltpu.roll(x, sh, 0), 0.0)

    def group_pair(i, carry):
        j0 = 2 * i
        res = [jnp.dot(w1_ref[j0 + d], ut_ref[j0 + d], preferred_element_type=F32) for d in range(2)]
        x_re = jnp.concatenate([res[0][R:R + P], res[1][R:R + P]], axis=0).T
        x_im = jnp.concatenate([res[0][R + P:R + 2 * P], res[1][R + P:R + 2 * P]], axis=0).T
        sh, k = 1, 0
        while sh < n_chunks:
            a_re = jnp.concatenate([al_ref[j0, k:k + 1, :], al_ref[j0 + 1, k:k + 1, :]], axis=1)
            a_im = jnp.concatenate([al_ref[j0, SUBLANES + k:SUBLANES + k + 1, :],
                                    al_ref[j0 + 1, SUBLANES + k:SUBLANES + k + 1, :]], axis=1)
            s_re, s_im = shifted(x_re, sh), shifted(x_im, sh)
            x_re, x_im = (x_re + a_re * s_re - a_im * s_im, x_im + a_re * s_im + a_im * s_re)
            sh *= 2
            k += 1
        xprev = jnp.concatenate([shifted(x_re, 1), shifted(x_im, 1)], axis=1).astype(BF16)
        for d in range(2):
            yt_ref[j0 + d] = res[d][0:R, :] + lax.dot_general(wout_ref[j0 + d], xprev, nt,
                                                              preferred_element_type=F32)
        return carry

    lax.fori_loop(0, n_grp // 2, group_pair, 0, unroll=2)

    for t in range(L):
        zt = jnp.concatenate([yt_ref[j, t * Hc:(t + 1) * Hc, :] for j in range(n_grp)], axis=0)
        y_ref[pl.ds(t, C, stride=L), :] = zt.T


def _s5(u, w1, wout, al, *, chunk, n_chunks, n_col_blocks):
    n_oct, t_oct, ds5 = u.shape
    T = n_oct * t_oct
    assert n_oct * SUBLANES == chunk
    G, R1, R = w1.shape
    P4 = wout.shape[-1]
    n_grp = LANES // S5_GROUP
    n_q = ds5 // LANES
    tb = T // n_col_blocks
    C = tb // chunk
    kern = functools.partial(_s5_kernel, chunk=chunk, n_chunks=n_chunks)
    grp = lambda *s: pl.BlockSpec((n_grp,) + s, lambda q, i: (q,) + (0,) * len(s))
    octets = [pl.BlockSpec((None, C * SUBLANES, LANES), functools.partial(lambda q, i, o: (o, i, q), o=o))
              for o in range(n_oct)]
    return pl.pallas_call(
        kern,
        grid=(n_q, n_col_blocks),
        in_specs=octets + [grp(R1, R), grp(R, P4), grp(2 * SUBLANES, P4 // 4)],
        out_specs=pl.BlockSpec((tb, LANES), lambda q, i: (i, q)),
        out_shape=jax.ShapeDtypeStruct((T, ds5), F32),
        scratch_shapes=[pltpu.VMEM((n_grp, R, C), BF16), pltpu.VMEM((n_grp, R, C), F32)],
        compiler_params=pltpu.CompilerParams(dimension_semantics=("arbitrary", "arbitrary"),
                                             vmem_limit_bytes=VMEM_LIMIT),
        name="s5",
    )(*([u] * n_oct), w1, wout, al)


def _out_kernel(x_ref, hm_ref, y_ref, wglu_ref, bglu_ref, gs5_ref, wo_ref, g2_ref,
                wgate_ref, wup_ref, wdown_ref, gf_ref, out_ref, acc_ref, xn_ref, *, ff_chunk, final_norm):
    dm = hm_ref.shape[1]
    g = jax.nn.gelu(y_ref[...])
    z = jnp.dot(g.astype(BF16), wglu_ref[...], preferred_element_type=F32) + bglu_ref[...]
    hs = _rmsnorm(g * _sigmoid(z), gs5_ref[...]).astype(BF16)
    h1 = (x_ref[...] + jnp.dot(hm_ref[...], wo_ref[0:dm, :], preferred_element_type=F32)
          + jnp.dot(hs, wo_ref[dm:, :], preferred_element_type=F32))
    xn_ref[...] = _rmsnorm(h1, g2_ref[...]).astype(BF16)
    acc_ref[...] = h1

    def ffn_chunk(c, carry):
        cs = pl.ds(pl.multiple_of(c * ff_chunk, ff_chunk), ff_chunk)
        xn2 = xn_ref[...]
        gate = jnp.dot(xn2, wgate_ref[:, cs], preferred_element_type=F32)
        up = jnp.dot(xn2, wup_ref[:, cs], preferred_element_type=F32)
        act = (gate * _sigmoid(gate) * up).astype(BF16)
        acc_ref[...] += jnp.dot(act, wdown_ref[cs, :], preferred_element_type=F32)
        return carry

    lax.fori_loop(0, wgate_ref.shape[1] // ff_chunk, ffn_chunk, 0, unroll=4)
    out_ref[...] = _rmsnorm(acc_ref[...], gf_ref[...]) if final_norm else acc_ref[...]


def _out_block(x2, hm, y, wglu, bglu, gs5, wo, g2, wgate, wup, wdown, gf, *, tm, ff_chunk, final_norm):
    T, D = x2.shape
    dm = hm.shape[1]
    ds5 = y.shape[1]
    assert wgate.shape[1] % ff_chunk == 0
    kern = functools.partial(_out_kernel, ff_chunk=ff_chunk, final_norm=final_norm)
    tok = lambda w: pl.BlockSpec((tm, w), lambda i: (i, 0))
    return pl.pallas_call(
        kern,
        grid=(T // tm,),
        in_specs=[tok(D), tok(dm), tok(ds5), _const_spec(wglu.shape), _const_spec((1, ds5)),
                  _const_spec((1, ds5)), _const_spec(wo.shape), _const_spec((1, D)),
                  _const_spec(wgate.shape), _const_spec(wup.shape), _const_spec(wdown.shape),
                  _const_spec((1, D))],
        out_specs=tok(D),
        out_shape=jax.ShapeDtypeStruct((T, D), F32),
        scratch_shapes=[pltpu.VMEM((tm, D), F32), pltpu.VMEM((tm, D), BF16)],
        compiler_params=pltpu.CompilerParams(dimension_semantics=("arbitrary",),
                                             vmem_limit_bytes=OUT_VMEM_LIMIT),
        name="out_block",
    )(x2, hm, y, wglu, bglu, gs5, wo, g2, wgate, wup, wdown, gf)


def kernel(x, norm1_g, w_in, if_bias, conv_qk, mlstm_norm_g, a_re, a_im, log_dt, b_re, b_im, c_re, c_im,
           d_skip, w_glu, b_glu, s5_norm_g, w_out, norm2_g, w_gate, w_up, w_down, norm_f_g):
    B, S, D = x.shape
    depth = w_in.shape[0]
    H = N_MLSTM_HEADS
    dm = mlstm_norm_g.shape[1]
    T = B * S
    L5 = S5_CHUNK
    nc5 = S // L5
    row = lambda a: a.reshape(1, -1).astype(F32)

    h = x.reshape(T, D)
    for l in range(depth):
        last = l == depth - 1
        gb = jnp.zeros((1, LANES), F32).at[0, 0:2 * H].set(if_bias[l])

        q, kt, v, o, gatest, u = _inproj(h, row(norm1_g[l]), w_in[l].T.astype(F32), gb, conv_qk[l].astype(F32),
                                         seq=S, tm=min(INPROJ_TM, S), d_mlstm=dm, n_oct=L5 // SUBLANES)
        hm, wglu, wo, wgate, wup, wdown = _mlstm(
            q, kt, v, o, gatest, row(mlstm_norm_g[l]),
            [w.astype(F32) for w in (w_glu[l], w_out[l], w_gate[l], w_up[l], w_down[l])], batch=B, seq=S)

        w1, wout, al = _s5_prep(a_re[l], a_im[l], log_dt[l], b_re[l], b_im[l], c_re[l], c_im[l],
                                d_skip[l], chunk=L5, n_chunks=nc5)
        y = _s5(u, w1, wout, al, chunk=L5, n_chunks=nc5, n_col_blocks=S5_COL_BLOCKS)

        h = _out_block(h, hm, y, wglu, row(b_glu[l]), row(s5_norm_g[l]), wo, row(norm2_g[l]),
                       wgate, wup, wdown, row(norm_f_g), tm=OUT_TM, ff_chunk=FF_CHUNK, final_norm=last)
    return h.reshape(B, S, D)
```

```python
import functools

import jax
import jax.numpy as jnp
from jax import lax
from jax.experimental import pallas as pl
from jax.experimental.pallas import tpu as pltpu

EPS = 1e-6
N_MLSTM_HEADS = 4
CONV_WIDTH = 4
S5_GROUP = 16
S5_STATE = 64

LANES = 128
SUBLANES = 8
MLSTM_CHUNK = 128
S5_CHUNK = 16
S5_COL_BLOCKS = 1
INPROJ_TM = 1024
OUT_TM = 1024
FF_CHUNK = 256
VMEM_LIMIT = 48 * 1024 * 1024
BIG_VMEM_LIMIT = 58 * 1024 * 1024

F32 = jnp.float32
BF16 = jnp.bfloat16


def _rmsnorm(x, g):
    ms = jnp.mean(x * x, axis=-1, keepdims=True)
    return x * lax.rsqrt(ms + EPS) * g


def _sigmoid(x):
    return 1.0 / (1.0 + jnp.exp(-x))


def _log_sigmoid(x):
    return -(jnp.maximum(-x, 0.0) + jnp.log1p(jnp.exp(-jnp.abs(x))))


def _const_spec(shape):
    return pl.BlockSpec(shape, lambda *_: (0,) * len(shape), pipeline_mode=pl.Buffered(1))


def _inproj_kernel(x_ref, g1_ref, wt_ref, gb_ref, cw_ref,
                   q_ref, kt_ref, v_ref, o_ref, gatest_ref, u_ref,
                   ext_ref, wm_ref, wg_ref, wu_ref, *, tiles_per_seq, d_mlstm, k_scale):
    tm = x_ref.shape[0]
    dm = d_mlstm
    H = N_MLSTM_HEADS
    halo = SUBLANES

    @pl.when(pl.program_id(0) == 0)
    def _():
        for c0 in range(0, 4 * dm, dm):
            wm_ref[:, c0:c0 + dm] = wt_ref[c0:c0 + dm, :].T.astype(BF16)
        wu_ref[...] = wt_ref[4 * dm + 2 * H:, :].T.astype(BF16)
        fill = jnp.zeros((LANES - 2 * H, wt_ref.shape[1]), F32)
        wg_ref[...] = jnp.concatenate([wt_ref[4 * dm:4 * dm + 2 * H, :], fill], axis=0).T.astype(BF16)

    xn = _rmsnorm(x_ref[...], g1_ref[...]).astype(BF16)

    @pl.when(pl.program_id(0) % tiles_per_seq == 0)
    def _():
        ext_ref[0:halo, :] = jnp.zeros((halo, 2 * dm), F32)

    ext_ref[halo:halo + tm, :] = jnp.dot(xn, wm_ref[:, 0:2 * dm], preferred_element_type=F32)
    v = jnp.dot(xn, wm_ref[:, 2 * dm:3 * dm], preferred_element_type=F32)
    o = jnp.dot(xn, wm_ref[:, 3 * dm:4 * dm], preferred_element_type=F32)
    u = jnp.dot(xn, wu_ref[...], preferred_element_type=F32)
    gates = jnp.dot(xn, wg_ref[...], preferred_element_type=F32) + gb_ref[...]
    v_ref[...] = v.astype(BF16)
    o_ref[...] = o.astype(BF16)
    n_oct = u_ref.shape[0]
    u4 = u.reshape(tm // (n_oct * SUBLANES), n_oct, SUBLANES, u.shape[1])
    for oc in range(n_oct):
        u_ref[oc] = u4[:, oc].reshape(tm // n_oct, u.shape[1])
    gatest_ref[...] = gates.T[0:2 * H, :]

    def zero_of(t):
        w = pltpu.bitcast(t[tm - SUBLANES:tm, t.shape[1] - LANES:], jnp.uint32)
        return pltpu.bitcast((w >> 16) >> 16, F32)
    anchor = zero_of(v)

    acc = cw_ref[CONV_WIDTH - 1:CONV_WIDTH, :] * ext_ref[halo:halo + tm, :]
    for j in range(1, CONV_WIDTH):
        acc = acc + cw_ref[CONV_WIDTH - 1 - j:CONV_WIDTH - j, :] * ext_ref[halo - j:halo - j + tm, :]
    ext_ref[0:halo, :] = ext_ref[tm:tm + halo, :]
    qk = acc * _sigmoid(acc) + jnp.tile(anchor, (tm // SUBLANES, 2 * dm // LANES))
    q_ref[...] = qk[:, 0:dm].astype(BF16)
    kt_ref[...] = (qk[:, dm:2 * dm] * k_scale).T.astype(BF16)


def _inproj(x2, g1, w_in_t, gb, cw, *, seq, tm, d_mlstm, n_oct):
    T, D = x2.shape
    dm = d_mlstm
    H = N_MLSTM_HEADS
    assert 2 * H == SUBLANES
    ds5 = w_in_t.shape[0] - 4 * dm - 2 * H
    kern = functools.partial(_inproj_kernel, tiles_per_seq=seq // tm, d_mlstm=dm,
                             k_scale=float((dm // H) ** -0.5))
    tok = lambda w: pl.BlockSpec((tm, w), lambda i: (i, 0))
    return pl.pallas_call(
        kern,
        grid=(T // tm,),
        in_specs=[tok(D), _const_spec((1, D)), _const_spec(w_in_t.shape), _const_spec((1, LANES)),
                  _const_spec(cw.shape)],
        out_specs=[tok(dm), pl.BlockSpec((dm, tm), lambda i: (0, i)), tok(dm), tok(dm),
                   pl.BlockSpec((2 * H, tm), lambda i: (0, i)),
                   pl.BlockSpec((n_oct, tm // n_oct, ds5), lambda i: (0, i, 0))],
        out_shape=[jax.ShapeDtypeStruct((T, dm), BF16), jax.ShapeDtypeStruct((dm, T), BF16),
                   jax.ShapeDtypeStruct((T, dm), BF16), jax.ShapeDtypeStruct((T, dm), BF16),
                   jax.ShapeDtypeStruct((2 * H, T), F32),
                   jax.ShapeDtypeStruct((n_oct, T // n_oct, ds5), F32)],
        scratch_shapes=[pltpu.VMEM((tm + 2 * SUBLANES, 2 * dm), F32), pltpu.VMEM((D, 4 * dm), BF16),
                        pltpu.VMEM((D, LANES), BF16), pltpu.VMEM((D, ds5), BF16)],
        compiler_params=pltpu.CompilerParams(dimension_semantics=("arbitrary",),
                                             vmem_limit_bytes=VMEM_LIMIT),
        name="inproj",
    )(x2, g1, w_in_t, gb, cw)


def _lane_scan(x, op, identity):
    n = x.shape[-1]
    pos = lax.broadcasted_iota(jnp.int32, x.shape, x.ndim - 1)
    sh = 1
    while sh < n:
        x = op(x, jnp.where(pos >= sh, pltpu.roll(x, sh, x.ndim - 1), identity))
        sh *= 2
    return x


def _mlstm_kernel(q_ref, kt_ref, v_ref, o_ref, gatest_ref, gn_ref, *refs, chunk, n_cast):
    S = q_ref.shape[0]
    H = N_MLSTM_HEADS
    dh = q_ref.shape[1] // H
    L = chunk
    assert L == dh == LANES
    w32_refs, out_ref, w16_refs = refs[0:n_cast], refs[n_cast], refs[n_cast + 1:2 * n_cast + 1]
    a_ref, g0_ref, nm_ref, gend_ref = refs[2 * n_cast + 1:2 * n_cast + 5]
    c_refs, nd_refs = refs[2 * n_cast + 5:2 * n_cast + 5 + H], refs[2 * n_cast + 5 + H:]

    for w32_ref, w16_ref in zip(w32_refs, w16_refs):
        w16_ref[...] = w32_ref[...].astype(BF16)

    gates = gatest_ref[...]
    f_cum = pltpu.roll(_lane_scan(_log_sigmoid(gates), jnp.add, 0.0), H, 0)
    a = gates - f_cum
    g0 = jnp.maximum(_lane_scan(a, jnp.maximum, -jnp.inf), 0.0)
    a_ref[...] = a
    g0_ref[...] = g0
    nm_ref[...] = -(f_cum + g0)
    for c in range(S // L):
        gend_ref[:, c * L:(c + 1) * L] = jnp.broadcast_to(g0[:, (c + 1) * L - 1:(c + 1) * L], (SUBLANES, L))

    for c_ref in c_refs:
        c_ref[...] = jnp.zeros(c_ref.shape, F32)
    causal = (lax.broadcasted_iota(jnp.int32, (L, L), 0) >= lax.broadcasted_iota(jnp.int32, (L, L), 1))
    ones_blk = jnp.ones((L, dh), BF16)
    mean_mat = jnp.full((dh, dh), 1.0 / dh, BF16)

    def mix(c, h, g_prev):
        r0 = pl.multiple_of(c * L, L)
        hs = slice(h * dh, (h + 1) * dh)
        qc = q_ref[pl.ds(r0, L), hs]
        ktc = kt_ref[hs, pl.ds(r0, L)]
        v_aug = jnp.concatenate([v_ref[pl.ds(r0, L), hs], ones_blk], axis=1)
        a_row = a_ref[h:h + 1, pl.ds(r0, L)]
        g_end = gend_ref[h:h + 1, pl.ds(r0, L)]
        g0_t = jnp.broadcast_to(g0_ref[h:h + 1, pl.ds(r0, L)], (L, L)).T

        s_qk = jnp.dot(qc, ktc, preferred_element_type=F32)
        p = (jnp.exp(jnp.where(causal, a_row - g0_t, -jnp.inf)) * s_qk).astype(BF16)
        c_prev = c_refs[h][...]
        q_w = (qc.astype(F32) * jnp.exp(g_prev - g0_t)).astype(BF16)
        nd_refs[h][...] = jnp.dot(jnp.concatenate([q_w, p], axis=1),
                                  jnp.concatenate([c_prev.astype(BF16), v_aug], axis=0),
                                  preferred_element_type=F32)

        kte = (ktc.astype(F32) * jnp.exp(a_row - g_end)).astype(BF16)
        decay = jnp.exp(g_prev - g_end)
        c_refs[h][...] = (jnp.concatenate([decay, decay], axis=1) * c_prev
                          + jnp.dot(kte, v_aug, preferred_element_type=F32))
        return g_end

    def emit(c, h):
        r0 = pl.multiple_of(c * L, L)
        hs = slice(h * dh, (h + 1) * dh)
        nm_t = jnp.broadcast_to(nm_ref[h:h + 1, pl.ds(r0, L)], (L, L)).T
        nd = nd_refs[h][...]
        h_tilde = nd[:, 0:dh] / jnp.maximum(jnp.abs(nd[:, dh:2 * dh]), jnp.exp(nm_t))
        hm = _sigmoid(o_ref[pl.ds(r0, L), hs].astype(F32)) * h_tilde
        mu = jnp.dot(hm.astype(BF16), mean_mat, preferred_element_type=F32)
        hc = hm - mu
        var = jnp.dot((hc * hc).astype(BF16), mean_mat, preferred_element_type=F32)
        out_ref[pl.ds(r0, L), hs] = (hc * lax.rsqrt(var + EPS) * gn_ref[:, hs]).astype(BF16)

    zero = jnp.zeros((1, L), F32)
    g_first = tuple(mix(0, h, zero) for h in range(H))

    def body(c, g_carry):
        for h in range(H):
            emit(c - 1, h)
        return tuple(mix(c, h, g_carry[h]) for h in range(H))

    lax.fori_loop(1, S // L, body, g_first, unroll=3)
    for h in range(H):
        emit(S // L - 1, h)


def _mlstm(q, kt, v, o, gatest, gn, weights, *, batch, seq):
    T, dm = q.shape
    dh = dm // N_MLSTM_HEADS
    kern = functools.partial(_mlstm_kernel, chunk=MLSTM_CHUNK, n_cast=len(weights))
    tok = lambda w: pl.BlockSpec((seq, w), lambda b: (b, 0))
    rows = pltpu.VMEM((SUBLANES, seq), F32)
    bf16_rows = 2 * SUBLANES
    assert all(w.shape[0] % (batch * bf16_rows) == 0 for w in weights)
    w_specs = [pl.BlockSpec((w.shape[0] // batch, w.shape[1]), lambda b: (b, 0)) for w in weights]
    return pl.pallas_call(
        kern,
        grid=(batch,),
        in_specs=[tok(dm), pl.BlockSpec((dm, seq), lambda b: (0, b)), tok(dm), tok(dm),
                  pl.BlockSpec((SUBLANES, seq), lambda b: (0, b)), _const_spec((1, dm))] + w_specs,
        out_specs=[tok(dm)] + w_specs,
        out_shape=[jax.ShapeDtypeStruct((T, dm), BF16)] + [jax.ShapeDtypeStruct(w.shape, BF16) for w in weights],
        scratch_shapes=([rows, rows, rows, rows] + [pltpu.VMEM((dh, 2 * dh), F32)] * N_MLSTM_HEADS
                        + [pltpu.VMEM((MLSTM_CHUNK, 2 * dh), F32)] * N_MLSTM_HEADS),
        compiler_params=pltpu.CompilerParams(dimension_semantics=("arbitrary",),
                                             vmem_limit_bytes=VMEM_LIMIT),
        name="mlstm",
    )(q, kt, v, o, gatest, gn, *weights)


def _s5_prep_kernel(arow_ref, acol_ref, ldt_ref, bre_ref, bim_ref, cre_ref, cim_ref, dcol_ref,
                    rep_ref, selrev_ref, pw_ref, pwcol_ref, w1_ref, wout_ref, al_ref, *, chunk):
    L = chunk
    P = S5_STATE
    Hc = S5_GROUP
    R = L * Hc
    hi = lax.Precision.HIGHEST
    dt = jnp.exp(ldt_ref[...])

    def select(x, sel_ref):
        x_hi = x.astype(BF16)
        x_lo = (x - x_hi.astype(F32)).astype(BF16)
        sel = sel_ref[...]
        return (jnp.dot(x_hi, sel, preferred_element_type=F32) + jnp.dot(x_lo, sel, preferred_element_type=F32))

    def abar_pow(a_re, a_im, tau):
        mag = jnp.exp(tau * (dt * a_re))
        ang = tau * (dt * a_im)
        return mag * jnp.cos(ang), mag * jnp.sin(ang)

    a_re_c = acol_ref[:, 0:1]
    a_im_c = acol_ref[:, 1:2]
    ec_re, ec_im = abar_pow(a_re_c, a_im_c, pw_ref[...])

    ab_re = ec_re[:, 1:2]
    ab_im = ec_im[:, 1:2]
    den = a_re_c * a_re_c + a_im_c * a_im_c
    nr = ab_re - 1.0
    z_re = (nr * a_re_c + ab_im * a_im_c) / den
    z_im = (ab_im * a_re_c - nr * a_im_c) / den
    bb_re = z_re * bre_ref[...] - z_im * bim_ref[...]
    bb_im = z_re * bim_ref[...] + z_im * bre_ref[...]

    n_pow = -(-(L + 1) // SUBLANES) * SUBLANES
    tau_r = lax.broadcasted_iota(jnp.int32, (n_pow, P), 0).astype(F32)
    er_re, er_im = abar_pow(arow_ref[0:1, :], arow_ref[1:2, :], tau_r)

    def rows(e, first):
        return jnp.concatenate([jnp.broadcast_to(e[first + t:first + t + 1, :], (Hc, P)) for t in range(L)], axis=0)

    c_re = jnp.concatenate([cre_ref[...]] * L, axis=0)
    c_im = jnp.concatenate([cim_ref[...]] * L, axis=0)

    def c_times_pow(first):
        e_re, e_im = rows(er_re, first), rows(er_im, first)
        return c_re * e_re - c_im * e_im, c_re * e_im + c_im * e_re

    ce_re, ce_im = c_times_pow(0)
    kflat = (jnp.dot(ce_re, bb_re, preferred_element_type=F32, precision=hi)
             - jnp.dot(ce_im, bb_im, preferred_element_type=F32, precision=hi))
    row = lax.broadcasted_iota(jnp.int32, (R, Hc), 0)
    col = lax.broadcasted_iota(jnp.int32, (R, Hc), 1)
    kflat = kflat + jnp.where(row == col, jnp.concatenate([dcol_ref[...]] * L, axis=0), 0.0)

    m = select(kflat, rep_ref)
    blk = lax.broadcasted_iota(jnp.int32, (R, R), 1) // Hc
    sh = 1
    while sh < L:
        shifted = jnp.concatenate([jnp.zeros((sh * Hc, R), F32), m[0:R - sh * Hc, :]], axis=0)
        m = jnp.where((blk & sh) != 0, shifted, m)
        sh *= 2
    w1_ref[0:R, :] = m.astype(BF16)

    ev_re, ev_im = select(ec_re, selrev_ref), select(ec_im, selrev_ref)
    bt_re, bt_im = select(bb_re, rep_ref), select(bb_im, rep_ref)
    w1_ref[R:R + P, :] = (ev_re * bt_re - ev_im * bt_im).astype(BF16)
    w1_ref[R + P:R + 2 * P, :] = (ev_re * bt_im + ev_im * bt_re).astype(BF16)

    co_re, co_im = c_times_pow(1)
    zero = jnp.zeros((R, P), F32)
    odd = pl.program_id(0) % 2 == 1
    halves = lambda v: jnp.where(odd, jnp.concatenate([zero, v], axis=1), jnp.concatenate([v, zero], axis=1))
    wout_ref[:, 0:2 * P] = halves(co_re).astype(BF16)
    wout_ref[:, 2 * P:4 * P] = halves(-co_im).astype(BF16)

    sc_re, sc_im = abar_pow(arow_ref[0:1, :], arow_ref[1:2, :], pwcol_ref[...])
    al_ref[0:SUBLANES, :] = sc_re
    al_ref[SUBLANES:2 * SUBLANES, :] = sc_im


def _s5_prep(a_re, a_im, log_dt, b_re, b_im, c_re, c_im, d_skip, *, chunk, n_chunks):
    G, P = a_re.shape
    Hc = S5_GROUP
    R = chunk * Hc
    arow = jnp.stack([a_re, a_im], axis=1)
    acol = jnp.stack([a_re, a_im], axis=2)
    ldt = log_dt.reshape(G, 1, 1)
    dcol = d_skip.reshape(G, Hc, 1)
    rep = (jnp.arange(R)[None, :] % Hc == jnp.arange(Hc)[:, None]).astype(BF16)
    selrev = (jnp.arange(LANES)[:, None] == chunk - 1 - jnp.arange(R)[None, :] // Hc).astype(BF16)
    n_steps = max(1, (n_chunks - 1).bit_length())
    assert chunk <= LANES and n_steps <= SUBLANES
    pw = jnp.zeros((1, LANES), F32).at[0, :chunk].set(jnp.arange(chunk, dtype=F32))
    pwcol = jnp.zeros((SUBLANES, 1), F32).at[:n_steps, 0].set(chunk * 2.0 ** jnp.arange(n_steps))
    kern = functools.partial(_s5_prep_kernel, chunk=chunk)
    grp = lambda *s: pl.BlockSpec((None,) + s, lambda g: (g,) + (0,) * len(s))
    return pl.pallas_call(
        kern,
        grid=(G,),
        in_specs=[grp(2, P), grp(P, 2), grp(1, 1), grp(P, Hc), grp(P, Hc), grp(Hc, P), grp(Hc, P),
                  grp(Hc, 1), _const_spec((Hc, R)), _const_spec((LANES, R)), _const_spec((1, LANES)),
                  _const_spec((SUBLANES, 1))],
        out_specs=[grp(R + 2 * P, R), grp(R, 4 * P), grp(2 * SUBLANES, P)],
        out_shape=[jax.ShapeDtypeStruct((G, R + 2 * P, R), BF16),
                   jax.ShapeDtypeStruct((G, R, 4 * P), BF16),
                   jax.ShapeDtypeStruct((G, 2 * SUBLANES, P), F32)],
        compiler_params=pltpu.CompilerParams(dimension_semantics=("arbitrary",),
                                             vmem_limit_bytes=VMEM_LIMIT),
        name="s5_prep",
    )(arow, acol, ldt, b_re, b_im, c_re, c_im, dcol, rep, selrev, pw, pwcol)


def _s5_kernel(*refs, chunk, n_chunks):
    L = chunk
    n_oct = L // SUBLANES
    u_refs, (w1_ref, wout_ref, al_ref, y_ref, ut_ref, yt_ref) = refs[0:n_oct], refs[n_oct:]
    P = S5_STATE
    Hc = S5_GROUP
    R = L * Hc
    C = y_ref.shape[0] // L
    n_grp = LANES // Hc

    for s in range(L):
        xt = u_refs[s // SUBLANES][pl.ds(s % SUBLANES, C, stride=SUBLANES), :].astype(BF16).T
        for j in range(n_grp):
            ut_ref[j, s * Hc:(s + 1) * Hc, :] = xt[j * Hc:(j + 1) * Hc, :]

    seg_pos = lax.broadcasted_iota(jnp.int32, (C, 2 * P), 0) & (n_chunks - 1)
    nt = (((1,), (1,)), ((), ()))

    def shifted(x, sh):
        return jnp.where(seg_pos >= sh, pltpu.roll(x, sh, 0), 0.0)

    def group_pair(i, carry):
        j0 = 2 * i
        res = [jnp.dot(w1_ref[j0 + d], ut_ref[j0 + d], preferred_element_type=F32) for d in range(2)]
        x_re = jnp.concatenate([res[0][R:R + P], res[1][R:R + P]], axis=0).T
        x_im = jnp.concatenate([res[0][R + P:R + 2 * P], res[1][R + P:R + 2 * P]], axis=0).T
        sh, k = 1, 0
        while sh < n_chunks:
            a_re = jnp.concatenate([al_ref[j0, k:k + 1, :], al_ref[j0 + 1, k:k + 1, :]], axis=1)
            a_im = jnp.concatenate([al_ref[j0, SUBLANES + k:SUBLANES + k + 1, :],
                                    al_ref[j0 + 1, SUBLANES + k:SUBLANES + k + 1, :]], axis=1)
            s_re, s_im = shifted(x_re, sh), shifted(x_im, sh)
            x_re, x_im = (x_re + a_re * s_re - a_im * s_im, x_im + a_re * s_im + a_im * s_re)
            sh *= 2
            k += 1
        xprev = jnp.concatenate([shifted(x_re, 1), shifted(x_im, 1)], axis=1).astype(BF16)
        for d in range(2):
            yt_ref[j0 + d] = res[d][0:R, :] + lax.dot_general(wout_ref[j0 + d], xprev, nt,
                                                              preferred_element_type=F32)
        return carry

    lax.fori_loop(0, n_grp // 2, group_pair, 0, unroll=2)

    for t in range(L):
        zt = jnp.concatenate([yt_ref[j, t * Hc:(t + 1) * Hc, :] for j in range(n_grp)], axis=0)
        y_ref[pl.ds(t, C, stride=L), :] = zt.T


def _s5(u, w1, wout, al, *, chunk, n_chunks, n_col_blocks):
    n_oct, t_oct, ds5 = u.shape
    T = n_oct * t_oct
    assert n_oct * SUBLANES == chunk
    G, R1, R = w1.shape
    P4 = wout.shape[-1]
    n_grp = LANES // S5_GROUP
    n_q = ds5 // LANES
    tb = T // n_col_blocks
    C = tb // chunk
    kern = functools.partial(_s5_kernel, chunk=chunk, n_chunks=n_chunks)
    grp = lambda *s: pl.BlockSpec((n_grp,) + s, lambda q, i: (q,) + (0,) * len(s))
    octets = [pl.BlockSpec((None, C * SUBLANES, LANES), functools.partial(lambda q, i, o: (o, i, q), o=o))
              for o in range(n_oct)]
    return pl.pallas_call(
        kern,
        grid=(n_q, n_col_blocks),
        in_specs=octets + [grp(R1, R), grp(R, P4), grp(2 * SUBLANES, P4 // 4)],
        out_specs=pl.BlockSpec((tb, LANES), lambda q, i: (i, q)),
        out_shape=jax.ShapeDtypeStruct((T, ds5), F32),
        scratch_shapes=[pltpu.VMEM((n_grp, R, C), BF16), pltpu.VMEM((n_grp, R, C), F32)],
        compiler_params=pltpu.CompilerParams(dimension_semantics=("arbitrary", "arbitrary"),
                                             vmem_limit_bytes=BIG_VMEM_LIMIT),
        name="s5",
    )(*([u] * n_oct), w1, wout, al)


def _out_kernel(x_ref, hm_ref, y_ref, wglu_ref, bglu_ref, gs5_ref, wo_ref, g2_ref,
                wgate_ref, wup_ref, wdown_ref, gf_ref, out_ref, acc_ref, xn_ref, *, ff_chunk, final_norm):
    dm = hm_ref.shape[1]
    g = jax.nn.gelu(y_ref[...])
    z = jnp.dot(g.astype(BF16), wglu_ref[...], preferred_element_type=F32) + bglu_ref[...]
    hs = _rmsnorm(g * _sigmoid(z), gs5_ref[...]).astype(BF16)
    h1 = (x_ref[...] + jnp.dot(hm_ref[...], wo_ref[0:dm, :], preferred_element_type=F32)
          + jnp.dot(hs, wo_ref[dm:, :], preferred_element_type=F32))
    xn_ref[...] = _rmsnorm(h1, g2_ref[...]).astype(BF16)
    acc_ref[...] = h1

    def ffn_chunk(c, carry):
        cs = pl.ds(pl.multiple_of(c * ff_chunk, ff_chunk), ff_chunk)
        xn2 = xn_ref[...]
        gate = jnp.dot(xn2, wgate_ref[:, cs], preferred_element_type=F32)
        up = jnp.dot(xn2, wup_ref[:, cs], preferred_element_type=F32)
        act = (gate * _sigmoid(gate) * up).astype(BF16)
        acc_ref[...] += jnp.dot(act, wdown_ref[cs, :], preferred_element_type=F32)
        return carry

    lax.fori_loop(0, wgate_ref.shape[1] // ff_chunk, ffn_chunk, 0, unroll=4)
    out_ref[...] = _rmsnorm(acc_ref[...], gf_ref[...]) if final_norm else acc_ref[...]


def _out_block(x2, hm, y, wglu, bglu, gs5, wo, g2, wgate, wup, wdown, gf, *, tm, ff_chunk, final_norm):
    T, D = x2.shape
    dm = hm.shape[1]
    ds5 = y.shape[1]
    assert wgate.shape[1] % ff_chunk == 0
    kern = functools.partial(_out_kernel, ff_chunk=ff_chunk, final_norm=final_norm)
    tok = lambda w: pl.BlockSpec((tm, w), lambda i: (i, 0))
    return pl.pallas_call(
        kern,
        grid=(T // tm,),
        in_specs=[tok(D), tok(dm), tok(ds5), _const_spec(wglu.shape), _const_spec((1, ds5)),
                  _const_spec((1, ds5)), _const_spec(wo.shape), _const_spec((1, D)),
                  _const_spec(wgate.shape), _const_spec(wup.shape), _const_spec(wdown.shape),
                  _const_spec((1, D))],
        out_specs=tok(D),
        out_shape=jax.ShapeDtypeStruct((T, D), F32),
        scratch_shapes=[pltpu.VMEM((tm, D), F32), pltpu.VMEM((tm, D), BF16)],
        compiler_params=pltpu.CompilerParams(dimension_semantics=("arbitrary",),
                                             vmem_limit_bytes=BIG_VMEM_LIMIT),
        name="out_block",
    )(x2, hm, y, wglu, bglu, gs5, wo, g2, wgate, wup, wdown, gf)


def kernel(x, norm1_g, w_in, if_bias, conv_qk, mlstm_norm_g, a_re, a_im, log_dt, b_re, b_im, c_re, c_im,
           d_skip, w_glu, b_glu, s5_norm_g, w_out, norm2_g, w_gate, w_up, w_down, norm_f_g):
    B, S, D = x.shape
    depth = w_in.shape[0]
    H = N_MLSTM_HEADS
    dm = mlstm_norm_g.shape[1]
    T = B * S
    L5 = S5_CHUNK
    nc5 = S // L5
    row = lambda a: a.reshape(1, -1).astype(F32)

    h = x.reshape(T, D)
    for l in range(depth):
        last = l == depth - 1
        gb = jnp.zeros((1, LANES), F32).at[0, 0:2 * H].set(if_bias[l])

        q, kt, v, o, gatest, u = _inproj(h, row(norm1_g[l]), w_in[l].T.astype(F32), gb, conv_qk[l].astype(F32),
                                         seq=S, tm=min(INPROJ_TM, S), d_mlstm=dm, n_oct=L5 // SUBLANES)
        hm, wglu, wo, wgate, wup, wdown = _mlstm(
            q, kt, v, o, gatest, row(mlstm_norm_g[l]),
            [w.astype(F32) for w in (w_glu[l], w_out[l], w_gate[l], w_up[l], w_down[l])], batch=B, seq=S)

        w1, wout, al = _s5_prep(a_re[l], a_im[l], log_dt[l], b_re[l], b_im[l], c_re[l], c_im[l],
                                d_skip[l], chunk=L5, n_chunks=nc5)
        y = _s5(u, w1, wout, al, chunk=L5, n_chunks=nc5, n_col_blocks=S5_COL_BLOCKS)

        h = _out_block(h, hm, y, wglu, row(b_glu[l]), row(s5_norm_g[l]), wo, row(norm2_g[l]),
                       wgate, wup, wdown, row(norm_f_g), tm=OUT_TM, ff_chunk=FF_CHUNK, final_norm=last)
    return h.reshape(B, S, D)
```

```python
import functools

import jax
import jax.numpy as jnp
from jax import lax
from jax.experimental import pallas as pl
from jax.experimental.pallas import tpu as pltpu

EPS = 1e-6
N_MLSTM_HEADS = 4
CONV_WIDTH = 4
S5_GROUP = 16
S5_STATE = 64

LANES = 128
SUBLANES = 8
MLSTM_CHUNK = 128
S5_CHUNK = 16
S5_COL_BLOCKS = 1
INPROJ_TM = 1024
OUT_TM = 1024
FF_CHUNK = 256
VMEM_LIMIT = 48 * 1024 * 1024
BIG_VMEM_LIMIT = 58 * 1024 * 1024

F32 = jnp.float32
BF16 = jnp.bfloat16


def _rmsnorm(x, g):
    ms = jnp.mean(x * x, axis=-1, keepdims=True)
    return x * lax.rsqrt(ms + EPS) * g


def _sigmoid(x):
    return 1.0 / (1.0 + jnp.exp(-x))


def _log_sigmoid(x):
    return -(jnp.maximum(-x, 0.0) + jnp.log1p(jnp.exp(-jnp.abs(x))))


def _const_spec(shape):
    return pl.BlockSpec(shape, lambda *_: (0,) * len(shape), pipeline_mode=pl.Buffered(1))


def _inproj_kernel(x_ref, g1_ref, wt_ref, gb_ref, cw_ref,
                   q_ref, kt_ref, v_ref, o_ref, gatest_ref, u_ref,
                   ext_ref, wm_ref, wg_ref, wu_ref, *, tiles_per_seq, d_mlstm, k_scale):
    tm = x_ref.shape[0]
    dm = d_mlstm
    H = N_MLSTM_HEADS
    halo = SUBLANES

    @pl.when(pl.program_id(0) == 0)
    def _():
        for c0 in range(0, 4 * dm, dm):
            wm_ref[:, c0:c0 + dm] = wt_ref[c0:c0 + dm, :].T.astype(BF16)
        wu_ref[...] = wt_ref[4 * dm + 2 * H:, :].T.astype(BF16)
        fill = jnp.zeros((LANES - 2 * H, wt_ref.shape[1]), F32)
        wg_ref[...] = jnp.concatenate([wt_ref[4 * dm:4 * dm + 2 * H, :], fill], axis=0).T.astype(BF16)

    xn = _rmsnorm(x_ref[...], g1_ref[...]).astype(BF16)

    @pl.when(pl.program_id(0) % tiles_per_seq == 0)
    def _():
        ext_ref[0:halo, :] = jnp.zeros((halo, 2 * dm), F32)

    ext_ref[halo:halo + tm, :] = jnp.dot(xn, wm_ref[:, 0:2 * dm], preferred_element_type=F32)
    v = jnp.dot(xn, wm_ref[:, 2 * dm:3 * dm], preferred_element_type=F32)
    o = jnp.dot(xn, wm_ref[:, 3 * dm:4 * dm], preferred_element_type=F32)
    u = jnp.dot(xn, wu_ref[...], preferred_element_type=F32)
    gates = jnp.dot(xn, wg_ref[...], preferred_element_type=F32) + gb_ref[...]
    v_ref[...] = v.astype(BF16)
    o_ref[...] = o.astype(BF16)
    n_oct = u_ref.shape[0]
    u4 = u.reshape(tm // (n_oct * SUBLANES), n_oct, SUBLANES, u.shape[1])
    for oc in range(n_oct):
        u_ref[oc] = u4[:, oc].reshape(tm // n_oct, u.shape[1])
    gatest_ref[...] = gates.T[0:2 * H, :]

    def zero_of(t):
        w = pltpu.bitcast(t[tm - SUBLANES:tm, t.shape[1] - LANES:], jnp.uint32)
        return pltpu.bitcast((w >> 16) >> 16, F32)
    anchor = zero_of(v)

    acc = cw_ref[CONV_WIDTH - 1:CONV_WIDTH, :] * ext_ref[halo:halo + tm, :]
    for j in range(1, CONV_WIDTH):
        acc = acc + cw_ref[CONV_WIDTH - 1 - j:CONV_WIDTH - j, :] * ext_ref[halo - j:halo - j + tm, :]
    ext_ref[0:halo, :] = ext_ref[tm:tm + halo, :]
    qk = acc * _sigmoid(acc) + jnp.tile(anchor, (tm // SUBLANES, 2 * dm // LANES))
    q_ref[...] = qk[:, 0:dm].astype(BF16)
    kt_ref[...] = (qk[:, dm:2 * dm] * k_scale).T.astype(BF16)


def _inproj(x2, g1, w_in_t, gb, cw, *, seq, tm, d_mlstm, n_oct):
    T, D = x2.shape
    dm = d_mlstm
    H = N_MLSTM_HEADS
    assert 2 * H == SUBLANES
    ds5 = w_in_t.shape[0] - 4 * dm - 2 * H
    kern = functools.partial(_inproj_kernel, tiles_per_seq=seq // tm, d_mlstm=dm,
                             k_scale=float((dm // H) ** -0.5))
    tok = lambda w: pl.BlockSpec((tm, w), lambda i: (i, 0))
    return pl.pallas_call(
        kern,
        grid=(T // tm,),
        in_specs=[tok(D), _const_spec((1, D)), _const_spec(w_in_t.shape), _const_spec((1, LANES)),
                  _const_spec(cw.shape)],
        out_specs=[tok(dm), pl.BlockSpec((dm, tm), lambda i: (0, i)), tok(dm), tok(dm),
                   pl.BlockSpec((2 * H, tm), lambda i: (0, i)),
                   pl.BlockSpec((n_oct, tm // n_oct, ds5), lambda i: (0, i, 0))],
        out_shape=[jax.ShapeDtypeStruct((T, dm), BF16), jax.ShapeDtypeStruct((dm, T), BF16),
                   jax.ShapeDtypeStruct((T, dm), BF16), jax.ShapeDtypeStruct((T, dm), BF16),
                   jax.ShapeDtypeStruct((2 * H, T), F32),
                   jax.ShapeDtypeStruct((n_oct, T // n_oct, ds5), F32)],
        scratch_shapes=[pltpu.VMEM((tm + 2 * SUBLANES, 2 * dm), F32), pltpu.VMEM((D, 4 * dm), BF16),
                        pltpu.VMEM((D, LANES), BF16), pltpu.VMEM((D, ds5), BF16)],
        compiler_params=pltpu.CompilerParams(dimension_semantics=("arbitrary",),
                                             vmem_limit_bytes=VMEM_LIMIT),
        name="inproj",
    )(x2, g1, w_in_t, gb, cw)


def _lane_scan(x, op, identity):
    n = x.shape[-1]
    pos = lax.broadcasted_iota(jnp.int32, x.shape, x.ndim - 1)
    sh = 1
    while sh < n:
        x = op(x, jnp.where(pos >= sh, pltpu.roll(x, sh, x.ndim - 1), identity))
        sh *= 2
    return x


def _mlstm_kernel(q_ref, kt_ref, v_ref, o_ref, gatest_ref, gn_ref, *refs, chunk, n_cast):
    S = q_ref.shape[0]
    H = N_MLSTM_HEADS
    dh = q_ref.shape[1] // H
    L = chunk
    assert L == dh == LANES
    w32_refs, out_ref, w16_refs = refs[0:n_cast], refs[n_cast], refs[n_cast + 1:2 * n_cast + 1]
    a_ref, g0_ref, nm_ref, gend_ref = refs[2 * n_cast + 1:2 * n_cast + 5]
    c_refs, nd_refs = refs[2 * n_cast + 5:2 * n_cast + 5 + H], refs[2 * n_cast + 5 + H:]

    for w32_ref, w16_ref in zip(w32_refs, w16_refs):
        w16_ref[...] = w32_ref[...].astype(BF16)

    gates = gatest_ref[...]
    f_cum = pltpu.roll(_lane_scan(_log_sigmoid(gates), jnp.add, 0.0), H, 0)
    a = gates - f_cum
    g0 = jnp.maximum(_lane_scan(a, jnp.maximum, -jnp.inf), 0.0)
    a_ref[...] = a
    g0_ref[...] = g0
    nm_ref[...] = -(f_cum + g0)
    for c in range(S // L):
        gend_ref[:, c * L:(c + 1) * L] = jnp.broadcast_to(g0[:, (c + 1) * L - 1:(c + 1) * L], (SUBLANES, L))

    for c_ref in c_refs:
        c_ref[...] = jnp.zeros(c_ref.shape, F32)
    causal = (lax.broadcasted_iota(jnp.int32, (L, L), 0) >= lax.broadcasted_iota(jnp.int32, (L, L), 1))
    ones_blk = jnp.ones((L, dh), BF16)
    mean_mat = jnp.full((dh, dh), 1.0 / dh, BF16)

    def mix(c, h, g_prev):
        r0 = pl.multiple_of(c * L, L)
        hs = slice(h * dh, (h + 1) * dh)
        qc = q_ref[pl.ds(r0, L), hs]
        ktc = kt_ref[hs, pl.ds(r0, L)]
        v_aug = jnp.concatenate([v_ref[pl.ds(r0, L), hs], ones_blk], axis=1)
        a_row = a_ref[h:h + 1, pl.ds(r0, L)]
        g_end = gend_ref[h:h + 1, pl.ds(r0, L)]
        g0_t = jnp.broadcast_to(g0_ref[h:h + 1, pl.ds(r0, L)], (L, L)).T

        s_qk = jnp.dot(qc, ktc, preferred_element_type=F32)
        p = (jnp.exp(jnp.where(causal, a_row - g0_t, -jnp.inf)) * s_qk).astype(BF16)
        c_prev = c_refs[h][...]
        q_w = (qc.astype(F32) * jnp.exp(g_prev - g0_t)).astype(BF16)
        nd_refs[h][...] = jnp.dot(jnp.concatenate([q_w, p], axis=1),
                                  jnp.concatenate([c_prev.astype(BF16), v_aug], axis=0),
                                  preferred_element_type=F32)

        kte = (ktc.astype(F32) * jnp.exp(a_row - g_end)).astype(BF16)
        decay = jnp.exp(g_prev - g_end)
        c_refs[h][...] = (jnp.concatenate([decay, decay], axis=1) * c_prev
                          + jnp.dot(kte, v_aug, preferred_element_type=F32))
        return g_end

    def emit(c, h):
        r0 = pl.multiple_of(c * L, L)
        hs = slice(h * dh, (h + 1) * dh)
        nm_t = jnp.broadcast_to(nm_ref[h:h + 1, pl.ds(r0, L)], (L, L)).T
        nd = nd_refs[h][...]
        h_tilde = nd[:, 0:dh] / jnp.maximum(jnp.abs(nd[:, dh:2 * dh]), jnp.exp(nm_t))
        hm = _sigmoid(o_ref[pl.ds(r0, L), hs].astype(F32)) * h_tilde
        mu = jnp.dot(hm.astype(BF16), mean_mat, preferred_element_type=F32)
        hc = hm - mu
        var = jnp.dot((hc * hc).astype(BF16), mean_mat, preferred_element_type=F32)
        out_ref[pl.ds(r0, L), hs] = (hc * lax.rsqrt(var + EPS) * gn_ref[:, hs]).astype(BF16)

    zero = jnp.zeros((1, L), F32)
    g_first = tuple(mix(0, h, zero) for h in range(H))

    def body(c, g_carry):
        for h in range(H):
            emit(c - 1, h)
        return tuple(mix(c, h, g_carry[h]) for h in range(H))

    lax.fori_loop(1, S // L, body, g_first, unroll=3)
    for h in range(H):
        emit(S // L - 1, h)


def _mlstm(q, kt, v, o, gatest, gn, weights, *, batch, seq):
    T, dm = q.shape
    dh = dm // N_MLSTM_HEADS
    kern = functools.partial(_mlstm_kernel, chunk=MLSTM_CHUNK, n_cast=len(weights))
    tok = lambda w: pl.BlockSpec((seq, w), lambda b: (b, 0))
    rows = pltpu.VMEM((SUBLANES, seq), F32)
    bf16_rows = 2 * SUBLANES
    assert all(w.shape[0] % (batch * bf16_rows) == 0 for w in weights)
    w_specs = [pl.BlockSpec((w.shape[0] // batch, w.shape[1]), lambda b: (b, 0)) for w in weights]
    return pl.pallas_call(
        kern,
        grid=(batch,),
        in_specs=[tok(dm), pl.BlockSpec((dm, seq), lambda b: (0, b)), tok(dm), tok(dm),
                  pl.BlockSpec((SUBLANES, seq), lambda b: (0, b)), _const_spec((1, dm))] + w_specs,
        out_specs=[tok(dm)] + w_specs,
        out_shape=[jax.ShapeDtypeStruct((T, dm), BF16)] + [jax.ShapeDtypeStruct(w.shape, BF16) for w in weights],
        scratch_shapes=([rows, rows, rows, rows] + [pltpu.VMEM((dh, 2 * dh), F32)] * N_MLSTM_HEADS
                        + [pltpu.VMEM((MLSTM_CHUNK, 2 * dh), F32)] * N_MLSTM_HEADS),
        compiler_params=pltpu.CompilerParams(dimension_semantics=("arbitrary",),
                                             vmem_limit_bytes=VMEM_LIMIT),
        name="mlstm",
    )(q, kt, v, o, gatest, gn, *weights)


def _s5_prep_kernel(arow_ref, acol_ref, bre_ref, bim_ref, cre_ref, cim_ref, drow_ref,
                    rep_ref, rep2_ref, selrev_ref, pwcol_ref, w1_ref, wout_ref, al_ref, *, chunk):
    L = chunk
    P = S5_STATE
    Hc = S5_GROUP
    R = L * Hc
    hi = lax.Precision.HIGHEST

    def select(x, sel_ref):
        x_hi = x.astype(BF16)
        x_lo = (x - x_hi.astype(F32)).astype(BF16)
        sel = sel_ref[...]
        return (jnp.dot(x_hi, sel, preferred_element_type=F32) + jnp.dot(x_lo, sel, preferred_element_type=F32))

    dt_r = jnp.exp(arow_ref[2:3, :])
    da_re, da_im = dt_r * arow_ref[0:1, :], dt_r * arow_ref[1:2, :]

    def abar_pow(tau):
        mag = jnp.exp(tau * da_re)
        ang = tau * da_im
        return mag * jnp.cos(ang), mag * jnp.sin(ang)

    n_pow = -(-(L + 1) // SUBLANES) * SUBLANES
    er_re, er_im = abar_pow(lax.broadcasted_iota(jnp.int32, (n_pow, 1), 0).astype(F32))
    pad = jnp.zeros((2 * P - n_pow, 2 * P), F32)
    ec_re = jnp.concatenate([er_re, pad], axis=0).T
    ec_im = jnp.concatenate([er_im, pad], axis=0).T

    a_re_c, a_im_c = acol_ref[:, 0:1], acol_ref[:, 1:2]
    ab_re, ab_im = ec_re[:, 1:2], ec_im[:, 1:2]
    den = a_re_c * a_re_c + a_im_c * a_im_c
    nr = ab_re - 1.0
    z_re = (nr * a_re_c + ab_im * a_im_c) / den
    z_im = (ab_im * a_re_c - nr * a_im_c) / den
    bb_re = z_re * bre_ref[...] - z_im * bim_ref[...]
    bb_im = z_re * bim_ref[...] + z_im * bre_ref[...]
    first = lax.broadcasted_iota(jnp.int32, (2 * P, Hc), 0) < P
    bd = lambda b: jnp.concatenate([jnp.where(first, b, 0.0), jnp.where(first, 0.0, b)], axis=1)

    def rows(e, lo):
        return jnp.concatenate([jnp.broadcast_to(e[lo + t:lo + t + 1, :], (Hc, 2 * P)) for t in range(L)], axis=0)

    c_re = jnp.concatenate([cre_ref[...]] * L, axis=0)
    c_im = jnp.concatenate([cim_ref[...]] * L, axis=0)

    def c_times_pow(lo):
        e_re, e_im = rows(er_re, lo), rows(er_im, lo)
        return c_re * e_re - c_im * e_im, c_re * e_im + c_im * e_re

    ce_re, ce_im = c_times_pow(0)
    kflat = (jnp.dot(ce_re, bd(bb_re), preferred_element_type=F32, precision=hi)
             - jnp.dot(ce_im, bd(bb_im), preferred_element_type=F32, precision=hi))
    row = lax.broadcasted_iota(jnp.int32, (R, 2 * Hc), 0)
    col = lax.broadcasted_iota(jnp.int32, (R, 2 * Hc), 1) & (Hc - 1)
    kflat = kflat + jnp.where(row == col, drow_ref[...], 0.0)

    m = select(kflat, rep2_ref)
    blk = (lax.broadcasted_iota(jnp.int32, (R, 2 * R), 1) & (R - 1)) // Hc
    sh = 1
    while sh < L:
        shifted = jnp.concatenate([jnp.zeros((sh * Hc, 2 * R), F32), m[0:R - sh * Hc, :]], axis=0)
        m = jnp.where((blk & sh) != 0, shifted, m)
        sh *= 2

    ev_re, ev_im = select(ec_re, selrev_ref), select(ec_im, selrev_ref)
    bt_re, bt_im = select(bb_re, rep_ref), select(bb_im, rep_ref)
    st_re = (ev_re * bt_re - ev_im * bt_im).astype(BF16)
    st_im = (ev_re * bt_im + ev_im * bt_re).astype(BF16)

    co_re, co_im = c_times_pow(1)
    lane_first = lax.broadcasted_iota(jnp.int32, (R, 2 * P), 1) < P
    for g in range(2):
        w1_ref[g, 0:R, :] = m[:, g * R:(g + 1) * R].astype(BF16)
        w1_ref[g, R:R + P, :] = st_re[g * P:(g + 1) * P, :]
        w1_ref[g, R + P:R + 2 * P, :] = st_im[g * P:(g + 1) * P, :]
        own = lane_first if g == 0 else jnp.logical_not(lane_first)
        wout_ref[g, :, 0:2 * P] = jnp.where(own, co_re, 0.0).astype(BF16)
        wout_ref[g, :, 2 * P:4 * P] = jnp.where(own, -co_im, 0.0).astype(BF16)

    sc_re, sc_im = abar_pow(pwcol_ref[...])
    al_ref[0:SUBLANES, :] = sc_re
    al_ref[SUBLANES:2 * SUBLANES, :] = sc_im


def _s5_prep(a_re, a_im, log_dt, b_re, b_im, c_re, c_im, d_skip, *, chunk, n_chunks):
    G, P = a_re.shape
    Hc = S5_GROUP
    R = chunk * Hc
    assert G % 2 == 0 and 2 * P == LANES
    G2 = G // 2
    ldt = jnp.broadcast_to(log_dt[:, None], (G, P))
    arow = jnp.stack([a_re.reshape(G2, 2 * P), a_im.reshape(G2, 2 * P), ldt.reshape(G2, 2 * P)], axis=1)
    acol = jnp.stack([a_re.reshape(G2, 2 * P), a_im.reshape(G2, 2 * P)], axis=2)
    pair_lanes = lambda c: c.reshape(G2, 2, Hc, P).transpose(0, 2, 1, 3).reshape(G2, Hc, 2 * P)
    drow = d_skip.reshape(G2, 1, 2 * Hc)
    rep = (jnp.arange(R)[None, :] % Hc == jnp.arange(Hc)[:, None]).astype(BF16)
    rep2 = jnp.kron(jnp.eye(2, dtype=BF16), rep)
    selrev = (jnp.arange(LANES)[:, None] == chunk - 1 - jnp.arange(R)[None, :] // Hc).astype(BF16)
    n_steps = max(1, (n_chunks - 1).bit_length())
    assert chunk + 1 <= LANES and n_steps <= SUBLANES
    pwcol = jnp.zeros((SUBLANES, 1), F32).at[:n_steps, 0].set(chunk * 2.0 ** jnp.arange(n_steps))
    kern = functools.partial(_s5_prep_kernel, chunk=chunk)
    pair = lambda *s: pl.BlockSpec((None,) + s, lambda g: (g,) + (0,) * len(s))
    two = lambda *s: pl.BlockSpec((2,) + s, lambda g: (g,) + (0,) * len(s))
    return pl.pallas_call(
        kern,
        grid=(G2,),
        in_specs=[pair(3, 2 * P), pair(2 * P, 2), pair(2 * P, Hc), pair(2 * P, Hc), pair(Hc, 2 * P), pair(Hc, 2 * P),
                  pair(1, 2 * Hc), _const_spec((Hc, R)), _const_spec((2 * Hc, 2 * R)), _const_spec((LANES, R)),
                  _const_spec((SUBLANES, 1))],
        out_specs=[two(R + 2 * P, R), two(R, 4 * P), pair(2 * SUBLANES, 2 * P)],
        out_shape=[jax.ShapeDtypeStruct((G, R + 2 * P, R), BF16),
                   jax.ShapeDtypeStruct((G, R, 4 * P), BF16),
                   jax.ShapeDtypeStruct((G2, 2 * SUBLANES, 2 * P), F32)],
        compiler_params=pltpu.CompilerParams(dimension_semantics=("arbitrary",),
                                             vmem_limit_bytes=VMEM_LIMIT),
        name="s5_prep",
    )(arow, acol, b_re.reshape(G2, 2 * P, Hc), b_im.reshape(G2, 2 * P, Hc), pair_lanes(c_re), pair_lanes(c_im),
      drow, rep, rep2, selrev, pwcol)


def _s5_kernel(*refs, chunk, n_chunks):
    L = chunk
    n_oct = L // SUBLANES
    u_refs, (w1_ref, wout_ref, al_ref, y_ref, ut_ref, yt_ref) = refs[0:n_oct], refs[n_oct:]
    P = S5_STATE
    Hc = S5_GROUP
    R = L * Hc
    C = y_ref.shape[0] // L
    n_grp = LANES // Hc

    for s in range(L):
        xt = u_refs[s // SUBLANES][pl.ds(s % SUBLANES, C, stride=SUBLANES), :].astype(BF16).T
        for j in range(n_grp):
            ut_ref[j, s * Hc:(s + 1) * Hc, :] = xt[j * Hc:(j + 1) * Hc, :]

    seg_pos = lax.broadcasted_iota(jnp.int32, (C, 2 * P), 0) & (n_chunks - 1)
    nt = (((1,), (1,)), ((), ()))

    def shifted(x, sh):
        return jnp.where(seg_pos >= sh, pltpu.roll(x, sh, 0), 0.0)

    def group_pair(i, carry):
        j0 = 2 * i
        res = [jnp.dot(w1_ref[j0 + d], ut_ref[j0 + d], preferred_element_type=F32) for d in range(2)]
        x_re = jnp.concatenate([res[0][R:R + P], res[1][R:R + P]], axis=0).T
        x_im = jnp.concatenate([res[0][R + P:R + 2 * P], res[1][R + P:R + 2 * P]], axis=0).T
        sh, k = 1, 0
        while sh < n_chunks:
            a_re = al_ref[i, k:k + 1, :]
            a_im = al_ref[i, SUBLANES + k:SUBLANES + k + 1, :]
            s_re, s_im = shifted(x_re, sh), shifted(x_im, sh)
            x_re, x_im = (x_re + a_re * s_re - a_im * s_im, x_im + a_re * s_im + a_im * s_re)
            sh *= 2
            k += 1
        xprev = jnp.concatenate([shifted(x_re, 1), shifted(x_im, 1)], axis=1).astype(BF16)
        for d in range(2):
            yt_ref[j0 + d] = res[d][0:R, :] + lax.dot_general(wout_ref[j0 + d], xprev, nt,
                                                              preferred_element_type=F32)
        return carry

    lax.fori_loop(0, n_grp // 2, group_pair, 0, unroll=2)

    for t in range(L):
        zt = jnp.concatenate([yt_ref[j, t * Hc:(t + 1) * Hc, :] for j in range(n_grp)], axis=0)
        y_ref[pl.ds(t, C, stride=L), :] = zt.T


def _s5(u, w1, wout, al, *, chunk, n_chunks, n_col_blocks):
    n_oct, t_oct, ds5 = u.shape
    T = n_oct * t_oct
    assert n_oct * SUBLANES == chunk
    G, R1, R = w1.shape
    P4 = wout.shape[-1]
    n_grp = LANES // S5_GROUP
    n_q = ds5 // LANES
    tb = T // n_col_blocks
    C = tb // chunk
    kern = functools.partial(_s5_kernel, chunk=chunk, n_chunks=n_chunks)
    grp = lambda *s: pl.BlockSpec((n_grp,) + s, lambda q, i: (q,) + (0,) * len(s))
    octets = [pl.BlockSpec((None, C * SUBLANES, LANES), functools.partial(lambda q, i, o: (o, i, q), o=o))
              for o in range(n_oct)]
    return pl.pallas_call(
        kern,
        grid=(n_q, n_col_blocks),
        in_specs=octets + [grp(R1, R), grp(R, P4),
                           pl.BlockSpec((n_grp // 2, 2 * SUBLANES, P4 // 2), lambda q, i: (q, 0, 0))],
        out_specs=pl.BlockSpec((tb, LANES), lambda q, i: (i, q)),
        out_shape=jax.ShapeDtypeStruct((T, ds5), F32),
        scratch_shapes=[pltpu.VMEM((n_grp, R, C), BF16), pltpu.VMEM((n_grp, R, C), F32)],
        compiler_params=pltpu.CompilerParams(dimension_semantics=("arbitrary", "arbitrary"),
                                             vmem_limit_bytes=BIG_VMEM_LIMIT),
        name="s5",
    )(*([u] * n_oct), w1, wout, al)


def _out_kernel(x_ref, hm_ref, y_ref, wglu_ref, bglu_ref, gs5_ref, wo_ref, g2_ref,
                wgate_ref, wup_ref, wdown_ref, gf_ref, out_ref, acc_ref, xn_ref, *, ff_chunk, final_norm):
    dm = hm_ref.shape[1]
    g = jax.nn.gelu(y_ref[...])
    z = jnp.dot(g.astype(BF16), wglu_ref[...], preferred_element_type=F32) + bglu_ref[...]
    hs = _rmsnorm(g * _sigmoid(z), gs5_ref[...]).astype(BF16)
    h1 = (x_ref[...] + jnp.dot(hm_ref[...], wo_ref[0:dm, :], preferred_element_type=F32)
          + jnp.dot(hs, wo_ref[dm:, :], preferred_element_type=F32))
    xn_ref[...] = _rmsnorm(h1, g2_ref[...]).astype(BF16)
    acc_ref[...] = h1

    def ffn_chunk(c, carry):
        cs = pl.ds(pl.multiple_of(c * ff_chunk, ff_chunk), ff_chunk)
        xn2 = xn_ref[...]
        gate = jnp.dot(xn2, wgate_ref[:, cs], preferred_element_type=F32)
        up = jnp.dot(xn2, wup_ref[:, cs], preferred_element_type=F32)
        act = (gate * _sigmoid(gate) * up).astype(BF16)
        acc_ref[...] += jnp.dot(act, wdown_ref[cs, :], preferred_element_type=F32)
        return carry

    lax.fori_loop(0, wgate_ref.shape[1] // ff_chunk, ffn_chunk, 0, unroll=4)
    out_ref[...] = _rmsnorm(acc_ref[...], gf_ref[...]) if final_norm else acc_ref[...]


def _out_block(x2, hm, y, wglu, bglu, gs5, wo, g2, wgate, wup, wdown, gf, *, tm, ff_chunk, final_norm):
    T, D = x2.shape
    dm = hm.shape[1]
    ds5 = y.shape[1]
    assert wgate.shape[1] % ff_chunk == 0
    kern = functools.partial(_out_kernel, ff_chunk=ff_chunk, final_norm=final_norm)
    tok = lambda w: pl.BlockSpec((tm, w), lambda i: (i, 0))
    return pl.pallas_call(
        kern,
        grid=(T // tm,),
        in_specs=[tok(D), tok(dm), tok(ds5), _const_spec(wglu.shape), _const_spec((1, ds5)),
                  _const_spec((1, ds5)), _const_spec(wo.shape), _const_spec((1, D)),
                  _const_spec(wgate.shape), _const_spec(wup.shape), _const_spec(wdown.shape),
                  _const_spec((1, D))],
        out_specs=tok(D),
        out_shape=jax.ShapeDtypeStruct((T, D), F32),
        scratch_shapes=[pltpu.VMEM((tm, D), F32), pltpu.VMEM((tm, D), BF16)],
        compiler_params=pltpu.CompilerParams(dimension_semantics=("arbitrary",),
                                             vmem_limit_bytes=BIG_VMEM_LIMIT),
        name="out_block",
    )(x2, hm, y, wglu, bglu, gs5, wo, g2, wgate, wup, wdown, gf)


def kernel(x, norm1_g, w_in, if_bias, conv_qk, mlstm_norm_g, a_re, a_im, log_dt, b_re, b_im, c_re, c_im,
           d_skip, w_glu, b_glu, s5_norm_g, w_out, norm2_g, w_gate, w_up, w_down, norm_f_g):
    B, S, D = x.shape
    depth = w_in.shape[0]
    H = N_MLSTM_HEADS
    dm = mlstm_norm_g.shape[1]
    T = B * S
    L5 = S5_CHUNK
    nc5 = S // L5
    row = lambda a: a.reshape(1, -1).astype(F32)

    h = x.reshape(T, D)
    for l in range(depth):
        last = l == depth - 1
        gb = jnp.zeros((1, LANES), F32).at[0, 0:2 * H].set(if_bias[l])

        q, kt, v, o, gatest, u = _inproj(h, row(norm1_g[l]), w_in[l].T.astype(F32), gb, conv_qk[l].astype(F32),
                                         seq=S, tm=min(INPROJ_TM, S), d_mlstm=dm, n_oct=L5 // SUBLANES)
        hm, wglu, wo, wgate, wup, wdown = _mlstm(
            q, kt, v, o, gatest, row(mlstm_norm_g[l]),
            [w.astype(F32) for w in (w_glu[l], w_out[l], w_gate[l], w_up[l], w_down[l])], batch=B, seq=S)

        w1, wout, al = _s5_prep(a_re[l], a_im[l], log_dt[l], b_re[l], b_im[l], c_re[l], c_im[l],
                                d_skip[l], chunk=L5, n_chunks=nc5)
        y = _s5(u, w1, wout, al, chunk=L5, n_chunks=nc5, n_col_blocks=S5_COL_BLOCKS)

        h = _out_block(h, hm, y, wglu, row(b_glu[l]), row(s5_norm_g[l]), wo, row(norm2_g[l]),
                       wgate, wup, wdown, row(norm_f_g), tm=OUT_TM, ff_chunk=FF_CHUNK, final_norm=last)
    return h.reshape(B, S, D)
```

```python
import functools

import jax
import jax.numpy as jnp
from jax import lax
from jax.experimental import pallas as pl
from jax.experimental.pallas import tpu as pltpu

EPS = 1e-6
N_MLSTM_HEADS = 4
CONV_WIDTH = 4
S5_GROUP = 16
S5_STATE = 64

LANES = 128
SUBLANES = 8
MLSTM_CHUNK = 128
S5_CHUNK = 16
S5_COL_BLOCKS = 1
INPROJ_TM = 1024
OUT_TM = 1024
FF_CHUNK = 256
VMEM_LIMIT = 48 * 1024 * 1024
BIG_VMEM_LIMIT = 58 * 1024 * 1024

F32 = jnp.float32
BF16 = jnp.bfloat16


def _rmsnorm(x, g):
    ms = jnp.mean(x * x, axis=-1, keepdims=True)
    return x * lax.rsqrt(ms + EPS) * g


def _sigmoid(x):
    return 0.5 * jnp.tanh(0.5 * x) + 0.5


def _log_sigmoid(x):
    return -(jnp.maximum(-x, 0.0) + jnp.log1p(jnp.exp(-jnp.abs(x))))


def _const_spec(shape):
    return pl.BlockSpec(shape, lambda *_: (0,) * len(shape), pipeline_mode=pl.Buffered(1))


def _inproj_kernel(x_ref, g1_ref, wt_ref, gb_ref, cw_ref,
                   q_ref, kt_ref, v_ref, o_ref, gatest_ref, u_ref,
                   ext_ref, wm_ref, wg_ref, wu_ref, *, tiles_per_seq, d_mlstm, k_scale):
    tm = x_ref.shape[0]
    dm = d_mlstm
    H = N_MLSTM_HEADS
    halo = SUBLANES

    @pl.when(pl.program_id(0) == 0)
    def _():
        g1 = g1_ref[...]
        for c0 in range(0, 4 * dm, dm):
            wm_ref[:, c0:c0 + dm] = (wt_ref[c0:c0 + dm, :] * g1).T.astype(BF16)
        wu_ref[...] = (wt_ref[4 * dm + 2 * H:, :] * g1).T.astype(BF16)
        fill = jnp.zeros((LANES - 2 * H, wt_ref.shape[1]), F32)
        wg_ref[...] = jnp.concatenate([wt_ref[4 * dm:4 * dm + 2 * H, :] * g1, fill], axis=0).T.astype(BF16)

    x = x_ref[...]
    xn = (x * lax.rsqrt(jnp.mean(x * x, axis=-1, keepdims=True) + EPS)).astype(BF16)

    @pl.when(pl.program_id(0) % tiles_per_seq == 0)
    def _():
        ext_ref[0:halo, :] = jnp.zeros((halo, 2 * dm), F32)

    ext_ref[halo:halo + tm, :] = jnp.dot(xn, wm_ref[:, 0:2 * dm], preferred_element_type=F32)
    v = jnp.dot(xn, wm_ref[:, 2 * dm:3 * dm], preferred_element_type=F32)
    o = jnp.dot(xn, wm_ref[:, 3 * dm:4 * dm], preferred_element_type=F32)
    u = jnp.dot(xn, wu_ref[...], preferred_element_type=F32)
    gates = jnp.dot(xn, wg_ref[...], preferred_element_type=F32) + gb_ref[...]
    v_ref[...] = v.astype(BF16)
    o_ref[...] = o.astype(BF16)
    n_oct = u_ref.shape[0]
    u4 = u.reshape(tm // (n_oct * SUBLANES), n_oct, SUBLANES, u.shape[1])
    for oc in range(n_oct):
        u_ref[oc] = u4[:, oc].reshape(tm // n_oct, u.shape[1])
    gatest_ref[...] = gates.T[0:2 * H, :]

    def zero_of(t):
        w = pltpu.bitcast(t[tm - SUBLANES:tm, t.shape[1] - LANES:], jnp.uint32)
        return pltpu.bitcast((w >> 16) >> 16, F32)
    anchor = zero_of(v)

    acc = cw_ref[CONV_WIDTH - 1:CONV_WIDTH, :] * ext_ref[halo:halo + tm, :]
    for j in range(1, CONV_WIDTH):
        acc = acc + cw_ref[CONV_WIDTH - 1 - j:CONV_WIDTH - j, :] * ext_ref[halo - j:halo - j + tm, :]
    ext_ref[0:halo, :] = ext_ref[tm:tm + halo, :]
    qk = acc * _sigmoid(acc) + jnp.tile(anchor, (tm // SUBLANES, 2 * dm // LANES))
    q_ref[...] = qk[:, 0:dm].astype(BF16)
    kt_ref[...] = (qk[:, dm:2 * dm] * k_scale).T.astype(BF16)


def _inproj(x2, g1, w_in_t, gb, cw, *, seq, tm, d_mlstm, n_oct):
    T, D = x2.shape
    dm = d_mlstm
    H = N_MLSTM_HEADS
    assert 2 * H == SUBLANES
    ds5 = w_in_t.shape[0] - 4 * dm - 2 * H
    kern = functools.partial(_inproj_kernel, tiles_per_seq=seq // tm, d_mlstm=dm,
                             k_scale=float((dm // H) ** -0.5))
    tok = lambda w: pl.BlockSpec((tm, w), lambda i: (i, 0))
    return pl.pallas_call(
        kern,
        grid=(T // tm,),
        in_specs=[tok(D), _const_spec((1, D)), _const_spec(w_in_t.shape), _const_spec((1, LANES)),
                  _const_spec(cw.shape)],
        out_specs=[tok(dm), pl.BlockSpec((dm, tm), lambda i: (0, i)), tok(dm), tok(dm),
                   pl.BlockSpec((2 * H, tm), lambda i: (0, i)),
                   pl.BlockSpec((n_oct, tm // n_oct, ds5), lambda i: (0, i, 0))],
        out_shape=[jax.ShapeDtypeStruct((T, dm), BF16), jax.ShapeDtypeStruct((dm, T), BF16),
                   jax.ShapeDtypeStruct((T, dm), BF16), jax.ShapeDtypeStruct((T, dm), BF16),
                   jax.ShapeDtypeStruct((2 * H, T), F32),
                   jax.ShapeDtypeStruct((n_oct, T // n_oct, ds5), F32)],
        scratch_shapes=[pltpu.VMEM((tm + 2 * SUBLANES, 2 * dm), F32), pltpu.VMEM((D, 4 * dm), BF16),
                        pltpu.VMEM((D, LANES), BF16), pltpu.VMEM((D, ds5), BF16)],
        compiler_params=pltpu.CompilerParams(dimension_semantics=("arbitrary",),
                                             vmem_limit_bytes=VMEM_LIMIT),
        name="inproj",
    )(x2, g1, w_in_t, gb, cw)


def _lane_scan(x, op, identity):
    n = x.shape[-1]
    pos = lax.broadcasted_iota(jnp.int32, x.shape, x.ndim - 1)
    sh = 1
    while sh < n:
        x = op(x, jnp.where(pos >= sh, pltpu.roll(x, sh, x.ndim - 1), identity))
        sh *= 2
    return x


def _mlstm_kernel(q_ref, kt_ref, v_ref, o_ref, gatest_ref, gn_ref, *refs, chunk, n_cast):
    S = q_ref.shape[0]
    H = N_MLSTM_HEADS
    dh = q_ref.shape[1] // H
    L = chunk
    assert L == dh == LANES
    w32_refs, out_ref, w16_refs = refs[0:n_cast], refs[n_cast], refs[n_cast + 1:2 * n_cast + 1]
    a_ref, g0_ref, nm_ref, gend_ref = refs[2 * n_cast + 1:2 * n_cast + 5]
    c_refs, nd_refs = refs[2 * n_cast + 5:2 * n_cast + 5 + H], refs[2 * n_cast + 5 + H:]

    for w32_ref, w16_ref in zip(w32_refs, w16_refs):
        w16_ref[...] = w32_ref[...].astype(BF16)

    gates = gatest_ref[...]
    f_cum = pltpu.roll(_lane_scan(_log_sigmoid(gates), jnp.add, 0.0), H, 0)
    a = gates - f_cum
    g0 = jnp.maximum(_lane_scan(a, jnp.maximum, -jnp.inf), 0.0)
    a_ref[...] = a
    g0_ref[...] = g0
    nm_ref[...] = -(f_cum + g0)
    for c in range(S // L):
        gend_ref[:, c * L:(c + 1) * L] = jnp.broadcast_to(g0[:, (c + 1) * L - 1:(c + 1) * L], (SUBLANES, L))

    for c_ref in c_refs:
        c_ref[...] = jnp.zeros(c_ref.shape, F32)
    causal = (lax.broadcasted_iota(jnp.int32, (L, L), 0) >= lax.broadcasted_iota(jnp.int32, (L, L), 1))
    ones_blk = jnp.ones((L, dh), BF16)
    mean_mat = jnp.full((dh, dh), 1.0 / dh, BF16)

    def mix(c, h, g_prev):
        r0 = pl.multiple_of(c * L, L)
        hs = slice(h * dh, (h + 1) * dh)
        qc = q_ref[pl.ds(r0, L), hs]
        ktc = kt_ref[hs, pl.ds(r0, L)]
        v_aug = jnp.concatenate([v_ref[pl.ds(r0, L), hs], ones_blk], axis=1)
        a_row = a_ref[h:h + 1, pl.ds(r0, L)]
        g_end = gend_ref[h:h + 1, pl.ds(r0, L)]
        g0_t = jnp.broadcast_to(g0_ref[h:h + 1, pl.ds(r0, L)], (L, L)).T

        s_qk = jnp.dot(qc, ktc, preferred_element_type=F32)
        p = (jnp.exp(jnp.where(causal, a_row - g0_t, -jnp.inf)) * s_qk).astype(BF16)
        c_prev = c_refs[h][...]
        q_w = (qc.astype(F32) * jnp.exp(g_prev - g0_t)).astype(BF16)
        nd_refs[h][...] = jnp.dot(jnp.concatenate([q_w, p], axis=1),
                                  jnp.concatenate([c_prev.astype(BF16), v_aug], axis=0),
                                  preferred_element_type=F32)

        kte = (ktc.astype(F32) * jnp.exp(a_row - g_end)).astype(BF16)
        decay = jnp.exp(g_prev - g_end)
        c_refs[h][...] = (jnp.concatenate([decay, decay], axis=1) * c_prev
                          + jnp.dot(kte, v_aug, preferred_element_type=F32))
        return g_end

    def emit(c, h):
        r0 = pl.multiple_of(c * L, L)
        hs = slice(h * dh, (h + 1) * dh)
        nm_t = jnp.broadcast_to(nm_ref[h:h + 1, pl.ds(r0, L)], (L, L)).T
        nd = nd_refs[h][...]
        h_tilde = nd[:, 0:dh] / jnp.maximum(jnp.abs(nd[:, dh:2 * dh]), jnp.exp(nm_t))
        hm = _sigmoid(o_ref[pl.ds(r0, L), hs].astype(F32)) * h_tilde
        mu = jnp.dot(hm.astype(BF16), mean_mat, preferred_element_type=F32)
        hc = hm - mu
        var = jnp.dot((hc * hc).astype(BF16), mean_mat, preferred_element_type=F32)
        out_ref[pl.ds(r0, L), hs] = (hc * lax.rsqrt(var + EPS) * gn_ref[:, hs]).astype(BF16)

    zero = jnp.zeros((1, L), F32)
    g_first = tuple(mix(0, h, zero) for h in range(H))

    def body(c, g_carry):
        for h in range(H):
            emit(c - 1, h)
        return tuple(mix(c, h, g_carry[h]) for h in range(H))

    lax.fori_loop(1, S // L, body, g_first, unroll=5)
    for h in range(H):
        emit(S // L - 1, h)


def _mlstm(q, kt, v, o, gatest, gn, weights, *, batch, seq):
    T, dm = q.shape
    dh = dm // N_MLSTM_HEADS
    kern = functools.partial(_mlstm_kernel, chunk=MLSTM_CHUNK, n_cast=len(weights))
    tok = lambda w: pl.BlockSpec((seq, w), lambda b: (b, 0))
    rows = pltpu.VMEM((SUBLANES, seq), F32)
    bf16_rows = 2 * SUBLANES
    assert all(w.shape[0] % (batch * bf16_rows) == 0 for w in weights)
    w_specs = [pl.BlockSpec((w.shape[0] // batch, w.shape[1]), lambda b: (b, 0)) for w in weights]
    return pl.pallas_call(
        kern,
        grid=(batch,),
        in_specs=[tok(dm), pl.BlockSpec((dm, seq), lambda b: (0, b)), tok(dm), tok(dm),
                  pl.BlockSpec((SUBLANES, seq), lambda b: (0, b)), _const_spec((1, dm))] + w_specs,
        out_specs=[tok(dm)] + w_specs,
        out_shape=[jax.ShapeDtypeStruct((T, dm), BF16)] + [jax.ShapeDtypeStruct(w.shape, BF16) for w in weights],
        scratch_shapes=([rows, rows, rows, rows] + [pltpu.VMEM((dh, 2 * dh), F32)] * N_MLSTM_HEADS
                        + [pltpu.VMEM((MLSTM_CHUNK, 2 * dh), F32)] * N_MLSTM_HEADS),
        compiler_params=pltpu.CompilerParams(dimension_semantics=("arbitrary",),
                                             vmem_limit_bytes=VMEM_LIMIT),
        name="mlstm",
    )(q, kt, v, o, gatest, gn, *weights)


def _s5_prep_kernel(arow_ref, acol_ref, bre_ref, bim_ref, cre_ref, cim_ref, drow_ref,
                    rep_ref, rep2_ref, selrev_ref, pwcol_ref, w1_ref, wout_ref, al_ref, *, chunk):
    L = chunk
    P = S5_STATE
    Hc = S5_GROUP
    R = L * Hc
    hi = lax.Precision.HIGHEST

    def select(x, sel_ref):
        x_hi = x.astype(BF16)
        x_lo = (x - x_hi.astype(F32)).astype(BF16)
        sel = sel_ref[...]
        return (jnp.dot(x_hi, sel, preferred_element_type=F32) + jnp.dot(x_lo, sel, preferred_element_type=F32))

    dt_r = jnp.exp(arow_ref[2:3, :])
    da_re, da_im = dt_r * arow_ref[0:1, :], dt_r * arow_ref[1:2, :]

    def abar_pow(tau):
        mag = jnp.exp(tau * da_re)
        ang = tau * da_im
        return mag * jnp.cos(ang), mag * jnp.sin(ang)

    n_pow = -(-(L + 1) // SUBLANES) * SUBLANES
    er_re, er_im = abar_pow(lax.broadcasted_iota(jnp.int32, (n_pow, 1), 0).astype(F32))
    pad = jnp.zeros((2 * P - n_pow, 2 * P), F32)
    ec_re = jnp.concatenate([er_re, pad], axis=0).T
    ec_im = jnp.concatenate([er_im, pad], axis=0).T

    a_re_c, a_im_c = acol_ref[:, 0:1], acol_ref[:, 1:2]
    ab_re, ab_im = ec_re[:, 1:2], ec_im[:, 1:2]
    den = a_re_c * a_re_c + a_im_c * a_im_c
    nr = ab_re - 1.0
    z_re = (nr * a_re_c + ab_im * a_im_c) / den
    z_im = (ab_im * a_re_c - nr * a_im_c) / den
    bb_re = z_re * bre_ref[...] - z_im * bim_ref[...]
    bb_im = z_re * bim_ref[...] + z_im * bre_ref[...]
    first = lax.broadcasted_iota(jnp.int32, (2 * P, Hc), 0) < P
    bd = lambda b: jnp.concatenate([jnp.where(first, b, 0.0), jnp.where(first, 0.0, b)], axis=1)

    def rows(e, lo):
        return jnp.concatenate([jnp.broadcast_to(e[lo + t:lo + t + 1, :], (Hc, 2 * P)) for t in range(L)], axis=0)

    c_re = jnp.concatenate([cre_ref[...]] * L, axis=0)
    c_im = jnp.concatenate([cim_ref[...]] * L, axis=0)

    def c_times_pow(lo):
        e_re, e_im = rows(er_re, lo), rows(er_im, lo)
        return c_re * e_re - c_im * e_im, c_re * e_im + c_im * e_re

    ce_re, ce_im = c_times_pow(0)
    kflat = (jnp.dot(ce_re, bd(bb_re), preferred_element_type=F32, precision=hi)
             - jnp.dot(ce_im, bd(bb_im), preferred_element_type=F32, precision=hi))
    row = lax.broadcasted_iota(jnp.int32, (R, 2 * Hc), 0)
    col = lax.broadcasted_iota(jnp.int32, (R, 2 * Hc), 1) & (Hc - 1)
    kflat = kflat + jnp.where(row == col, drow_ref[...], 0.0)

    m = select(kflat, rep2_ref)
    blk = (lax.broadcasted_iota(jnp.int32, (R, 2 * R), 1) & (R - 1)) // Hc
    sh = 1
    while sh < L:
        shifted = jnp.concatenate([jnp.zeros((sh * Hc, 2 * R), F32), m[0:R - sh * Hc, :]], axis=0)
        m = jnp.where((blk & sh) != 0, shifted, m)
        sh *= 2

    ev_re, ev_im = select(ec_re, selrev_ref), select(ec_im, selrev_ref)
    bt_re, bt_im = select(bb_re, rep_ref), select(bb_im, rep_ref)
    st_re = (ev_re * bt_re - ev_im * bt_im).astype(BF16)
    st_im = (ev_re * bt_im + ev_im * bt_re).astype(BF16)

    co_re, co_im = c_times_pow(1)
    lane_first = lax.broadcasted_iota(jnp.int32, (R, 2 * P), 1) < P
    for g in range(2):
        w1_ref[g, 0:R, :] = m[:, g * R:(g + 1) * R].astype(BF16)
        w1_ref[g, R:R + P, :] = st_re[g * P:(g + 1) * P, :]
        w1_ref[g, R + P:R + 2 * P, :] = st_im[g * P:(g + 1) * P, :]
        own = lane_first if g == 0 else jnp.logical_not(lane_first)
        wout_ref[g, :, 0:2 * P] = jnp.where(own, co_re, 0.0).astype(BF16)
        wout_ref[g, :, 2 * P:4 * P] = jnp.where(own, -co_im, 0.0).astype(BF16)

    sc_re, sc_im = abar_pow(pwcol_ref[...])
    al_ref[0:SUBLANES, :] = sc_re
    al_ref[SUBLANES:2 * SUBLANES, :] = sc_im


def _s5_prep(a_re, a_im, log_dt, b_re, b_im, c_re, c_im, d_skip, *, chunk, n_chunks):
    G, P = a_re.shape
    Hc = S5_GROUP
    R = chunk * Hc
    assert G % 2 == 0 and 2 * P == LANES
    G2 = G // 2
    ldt = jnp.broadcast_to(log_dt[:, None], (G, P))
    arow = jnp.stack([a_re.reshape(G2, 2 * P), a_im.reshape(G2, 2 * P), ldt.reshape(G2, 2 * P)], axis=1)
    acol = jnp.stack([a_re.reshape(G2, 2 * P), a_im.reshape(G2, 2 * P)], axis=2)
    pair_lanes = lambda c: c.reshape(G2, 2, Hc, P).transpose(0, 2, 1, 3).reshape(G2, Hc, 2 * P)
    drow = d_skip.reshape(G2, 1, 2 * Hc)
    rep = (jnp.arange(R)[None, :] % Hc == jnp.arange(Hc)[:, None]).astype(BF16)
    rep2 = jnp.kron(jnp.eye(2, dtype=BF16), rep)
    selrev = (jnp.arange(LANES)[:, None] == chunk - 1 - jnp.arange(R)[None, :] // Hc).astype(BF16)
    n_steps = max(1, (n_chunks - 1).bit_length())
    assert chunk + 1 <= LANES and n_steps <= SUBLANES
    pwcol = jnp.zeros((SUBLANES, 1), F32).at[:n_steps, 0].set(chunk * 2.0 ** jnp.arange(n_steps))
    kern = functools.partial(_s5_prep_kernel, chunk=chunk)
    pair = lambda *s: pl.BlockSpec((None,) + s, lambda g: (g,) + (0,) * len(s))
    two = lambda *s: pl.BlockSpec((2,) + s, lambda g: (g,) + (0,) * len(s))
    return pl.pallas_call(
        kern,
        grid=(G2,),
        in_specs=[pair(3, 2 * P), pair(2 * P, 2), pair(2 * P, Hc), pair(2 * P, Hc), pair(Hc, 2 * P), pair(Hc, 2 * P),
                  pair(1, 2 * Hc), _const_spec((Hc, R)), _const_spec((2 * Hc, 2 * R)), _const_spec((LANES, R)),
                  _const_spec((SUBLANES, 1))],
        out_specs=[two(R + 2 * P, R), two(R, 4 * P), pair(2 * SUBLANES, 2 * P)],
        out_shape=[jax.ShapeDtypeStruct((G, R + 2 * P, R), BF16),
                   jax.ShapeDtypeStruct((G, R, 4 * P), BF16),
                   jax.ShapeDtypeStruct((G2, 2 * SUBLANES, 2 * P), F32)],
        compiler_params=pltpu.CompilerParams(dimension_semantics=("arbitrary",),
                                             vmem_limit_bytes=VMEM_LIMIT),
        name="s5_prep",
    )(arow, acol, b_re.reshape(G2, 2 * P, Hc), b_im.reshape(G2, 2 * P, Hc), pair_lanes(c_re), pair_lanes(c_im),
      drow, rep, rep2, selrev, pwcol)


def _s5_kernel(*refs, chunk, n_chunks):
    L = chunk
    n_oct = L // SUBLANES
    u_refs, (w1_ref, wout_ref, al_ref, y_ref, ut_ref, yt_ref) = refs[0:n_oct], refs[n_oct:]
    P = S5_STATE
    Hc = S5_GROUP
    R = L * Hc
    C = y_ref.shape[0] // L
    n_grp = LANES // Hc

    for s in range(L):
        xt = u_refs[s // SUBLANES][pl.ds(s % SUBLANES, C, stride=SUBLANES), :].astype(BF16).T
        for j in range(n_grp):
            ut_ref[j, s * Hc:(s + 1) * Hc, :] = xt[j * Hc:(j + 1) * Hc, :]

    seg_pos = lax.broadcasted_iota(jnp.int32, (C, 2 * P), 0) & (n_chunks - 1)
    nt = (((1,), (1,)), ((), ()))

    def shifted(x, sh):
        return jnp.where(seg_pos >= sh, pltpu.roll(x, sh, 0), 0.0)

    def group_pair(i, carry):
        j0 = 2 * i
        res = [jnp.dot(w1_ref[j0 + d], ut_ref[j0 + d], preferred_element_type=F32) for d in range(2)]
        x_re = jnp.concatenate([res[0][R:R + P], res[1][R:R + P]], axis=0).T
        x_im = jnp.concatenate([res[0][R + P:R + 2 * P], res[1][R + P:R + 2 * P]], axis=0).T
        sh, k = 1, 0
        while sh < n_chunks:
            a_re = al_ref[i, k:k + 1, :]
            a_im = al_ref[i, SUBLANES + k:SUBLANES + k + 1, :]
            s_re, s_im = shifted(x_re, sh), shifted(x_im, sh)
            x_re, x_im = (x_re + a_re * s_re - a_im * s_im, x_im + a_re * s_im + a_im * s_re)
            sh *= 2
            k += 1
        xprev = jnp.concatenate([shifted(x_re, 1), shifted(x_im, 1)], axis=1).astype(BF16)
        for d in range(2):
            yt_ref[j0 + d] = res[d][0:R, :] + lax.dot_general(wout_ref[j0 + d], xprev, nt,
                                                              preferred_element_type=F32)
        return carry

    lax.fori_loop(0, n_grp // 2, group_pair, 0, unroll=2)

    for t in range(L):
        zt = jnp.concatenate([yt_ref[j, t * Hc:(t + 1) * Hc, :] for j in range(n_grp)], axis=0)
        y_ref[pl.ds(t, C, stride=L), :] = zt.T


def _s5(u, w1, wout, al, *, chunk, n_chunks, n_col_blocks):
    n_oct, t_oct, ds5 = u.shape
    T = n_oct * t_oct
    assert n_oct * SUBLANES == chunk
    G, R1, R = w1.shape
    P4 = wout.shape[-1]
    n_grp = LANES // S5_GROUP
    n_q = ds5 // LANES
    tb = T // n_col_blocks
    C = tb // chunk
    kern = functools.partial(_s5_kernel, chunk=chunk, n_chunks=n_chunks)
    grp = lambda *s: pl.BlockSpec((n_grp,) + s, lambda q, i: (q,) + (0,) * len(s))
    octets = [pl.BlockSpec((None, C * SUBLANES, LANES), functools.partial(lambda q, i, o: (o, i, q), o=o))
              for o in range(n_oct)]
    return pl.pallas_call(
        kern,
        grid=(n_q, n_col_blocks),
        in_specs=octets + [grp(R1, R), grp(R, P4),
                           pl.BlockSpec((n_grp // 2, 2 * SUBLANES, P4 // 2), lambda q, i: (q, 0, 0))],
        out_specs=pl.BlockSpec((tb, LANES), lambda q, i: (i, q)),
        out_shape=jax.ShapeDtypeStruct((T, ds5), F32),
        scratch_shapes=[pltpu.VMEM((n_grp, R, C), BF16), pltpu.VMEM((n_grp, R, C), F32)],
        compiler_params=pltpu.CompilerParams(dimension_semantics=("arbitrary", "arbitrary"),
                                             vmem_limit_bytes=BIG_VMEM_LIMIT),
        name="s5",
    )(*([u] * n_oct), w1, wout, al)


def _out_kernel(x_ref, hm_ref, y_ref, wglu_ref, bglu_ref, gs5_ref, wo_ref, g2_ref,
                wgate_ref, wup_ref, wdown_ref, gf_ref, out_ref, acc_ref, xn_ref, *, ff_chunk, final_norm):
    dm = hm_ref.shape[1]
    g = jax.nn.gelu(y_ref[...])
    z = jnp.dot(g.astype(BF16), wglu_ref[...], preferred_element_type=F32) + bglu_ref[...]
    hs = _rmsnorm(g * _sigmoid(z), gs5_ref[...]).astype(BF16)
    h1 = (x_ref[...] + jnp.dot(hm_ref[...], wo_ref[0:dm, :], preferred_element_type=F32)
          + jnp.dot(hs, wo_ref[dm:, :], preferred_element_type=F32))
    xn_ref[...] = _rmsnorm(h1, g2_ref[...]).astype(BF16)
    acc_ref[...] = h1

    def ffn_chunk(c, carry):
        cs = pl.ds(pl.multiple_of(c * ff_chunk, ff_chunk), ff_chunk)
        xn2 = xn_ref[...]
        gate = jnp.dot(xn2, wgate_ref[:, cs], preferred_element_type=F32)
        up = jnp.dot(xn2, wup_ref[:, cs], preferred_element_type=F32)
        act = (gate * _sigmoid(gate) * up).astype(BF16)
        acc_ref[...] += jnp.dot(act, wdown_ref[cs, :], preferred_element_type=F32)
        return carry

    lax.fori_loop(0, wgate_ref.shape[1] // ff_chunk, ffn_chunk, 0, unroll=4)
    out_ref[...] = _rmsnorm(acc_ref[...], gf_ref[...]) if final_norm else acc_ref[...]


def _out_block(x2, hm, y, wglu, bglu, gs5, wo, g2, wgate, wup, wdown, gf, *, tm, ff_chunk, final_norm):
    T, D = x2.shape
    dm = hm.shape[1]
    ds5 = y.shape[1]
    assert wgate.shape[1] % ff_chunk == 0
    kern = functools.partial(_out_kernel, ff_chunk=ff_chunk, final_norm=final_norm)
    tok = lambda w: pl.BlockSpec((tm, w), lambda i: (i, 0))
    return pl.pallas_call(
        kern,
        grid=(T // tm,),
        in_specs=[tok(D), tok(dm), tok(ds5), _const_spec(wglu.shape), _const_spec((1, ds5)),
                  _const_spec((1, ds5)), _const_spec(wo.shape), _const_spec((1, D)),
                  _const_spec(wgate.shape), _const_spec(wup.shape), _const_spec(wdown.shape),
                  _const_spec((1, D))],
        out_specs=tok(D),
        out_shape=jax.ShapeDtypeStruct((T, D), F32),
        scratch_shapes=[pltpu.VMEM((tm, D), F32), pltpu.VMEM((tm, D), BF16)],
        compiler_params=pltpu.CompilerParams(dimension_semantics=("arbitrary",),
                                             vmem_limit_bytes=BIG_VMEM_LIMIT),
        name="out_block",
    )(x2, hm, y, wglu, bglu, gs5, wo, g2, wgate, wup, wdown, gf)


def kernel(x, norm1_g, w_in, if_bias, conv_qk, mlstm_norm_g, a_re, a_im, log_dt, b_re, b_im, c_re, c_im,
           d_skip, w_glu, b_glu, s5_norm_g, w_out, norm2_g, w_gate, w_up, w_down, norm_f_g):
    B, S, D = x.shape
    depth = w_in.shape[0]
    H = N_MLSTM_HEADS
    dm = mlstm_norm_g.shape[1]
    T = B * S
    L5 = S5_CHUNK
    nc5 = S // L5
    row = lambda a: a.reshape(1, -1).astype(F32)

    h = x.reshape(T, D)
    for l in range(depth):
        last = l == depth - 1
        gb = jnp.zeros((1, LANES), F32).at[0, 0:2 * H].set(if_bias[l])

        q, kt, v, o, gatest, u = _inproj(h, row(norm1_g[l]), w_in[l].T.astype(F32), gb, conv_qk[l].astype(F32),
                                         seq=S, tm=min(INPROJ_TM, S), d_mlstm=dm, n_oct=L5 // SUBLANES)
        hm, wglu, wo, wgate, wup, wdown = _mlstm(
            q, kt, v, o, gatest, row(mlstm_norm_g[l]),
            [w.astype(F32) for w in (w_glu[l], w_out[l], w_gate[l], w_up[l], w_down[l])], batch=B, seq=S)

        w1, wout, al = _s5_prep(a_re[l], a_im[l], log_dt[l], b_re[l], b_im[l], c_re[l], c_im[l],
                                d_skip[l], chunk=L5, n_chunks=nc5)
        y = _s5(u, w1, wout, al, chunk=L5, n_chunks=nc5, n_col_blocks=S5_COL_BLOCKS)

        h = _out_block(h, hm, y, wglu, row(b_glu[l]), row(s5_norm_g[l]), wo, row(norm2_g[l]),
                       wgate, wup, wdown, row(norm_f_g), tm=OUT_TM, ff_chunk=FF_CHUNK, final_norm=last)
    return h.reshape(B, S, D)
```

```python
import functools

import jax
import jax.numpy as jnp
from jax import lax
from jax.experimental import pallas as pl
from jax.experimental.pallas import tpu as pltpu

EPS = 1e-6
N_MLSTM_HEADS = 4
CONV_WIDTH = 4
S5_GROUP = 16
S5_STATE = 64

LANES = 128
SUBLANES = 8
MLSTM_CHUNK = 128
S5_CHUNK = 16
S5_COL_BLOCKS = 1
INPROJ_TM = 1024
OUT_TM = 1024
FF_CHUNK = 256
VMEM_LIMIT = 48 * 1024 * 1024
BIG_VMEM_LIMIT = 58 * 1024 * 1024

F32 = jnp.float32
BF16 = jnp.bfloat16


def _rmsnorm(x, g):
    ms = jnp.mean(x * x, axis=-1, keepdims=True)
    return x * lax.rsqrt(ms + EPS) * g


def _sigmoid(x):
    return 0.5 * jnp.tanh(0.5 * x) + 0.5


def _log_sigmoid(x):
    return -(jnp.maximum(-x, 0.0) + jnp.log1p(jnp.exp(-jnp.abs(x))))


def _const_spec(shape):
    return pl.BlockSpec(shape, lambda *_: (0,) * len(shape), pipeline_mode=pl.Buffered(1))


def _inproj_kernel(x_ref, g1_ref, wt_ref, gb_ref, cw_ref,
                   q_ref, kt_ref, v_ref, o_ref, gatest_ref, u_ref,
                   ext_ref, wm_ref, wg_ref, wu_ref, *, tiles_per_seq, d_mlstm, k_scale):
    tm = x_ref.shape[0]
    dm = d_mlstm
    H = N_MLSTM_HEADS
    halo = SUBLANES

    @pl.when(pl.program_id(0) == 0)
    def _():
        g1 = g1_ref[...]
        for c0 in range(0, 4 * dm, dm):
            wm_ref[:, c0:c0 + dm] = (wt_ref[c0:c0 + dm, :] * g1).T.astype(BF16)
        wu_ref[...] = (wt_ref[4 * dm + 2 * H:, :] * g1).T.astype(BF16)
        fill = jnp.zeros((LANES - 2 * H, wt_ref.shape[1]), F32)
        wg_ref[...] = jnp.concatenate([wt_ref[4 * dm:4 * dm + 2 * H, :] * g1, fill], axis=0).T.astype(BF16)

    x = x_ref[...]
    xn = (x * lax.rsqrt(jnp.mean(x * x, axis=-1, keepdims=True) + EPS)).astype(BF16)

    @pl.when(pl.program_id(0) % tiles_per_seq == 0)
    def _():
        ext_ref[0:halo, :] = jnp.zeros((halo, 2 * dm), F32)

    ext_ref[halo:halo + tm, :] = jnp.dot(xn, wm_ref[:, 0:2 * dm], preferred_element_type=F32)
    v = jnp.dot(xn, wm_ref[:, 2 * dm:3 * dm], preferred_element_type=F32)
    o = jnp.dot(xn, wm_ref[:, 3 * dm:4 * dm], preferred_element_type=F32)
    u = jnp.dot(xn, wu_ref[...], preferred_element_type=F32)
    gates = jnp.dot(xn, wg_ref[...], preferred_element_type=F32) + gb_ref[...]
    v_ref[...] = v.astype(BF16)
    o_ref[...] = o.astype(BF16)
    n_oct = u_ref.shape[0]
    u4 = u.reshape(tm // (n_oct * SUBLANES), n_oct, SUBLANES, u.shape[1])
    for oc in range(n_oct):
        u_ref[oc] = u4[:, oc].reshape(tm // n_oct, u.shape[1])
    gatest_ref[...] = gates.T[0:2 * H, :]

    def zero_of(t):
        w = pltpu.bitcast(t[tm - SUBLANES:tm, t.shape[1] - LANES:], jnp.uint32)
        return pltpu.bitcast((w >> 16) >> 16, F32)
    anchor = zero_of(v)

    acc = cw_ref[CONV_WIDTH - 1:CONV_WIDTH, :] * ext_ref[halo:halo + tm, :]
    for j in range(1, CONV_WIDTH):
        acc = acc + cw_ref[CONV_WIDTH - 1 - j:CONV_WIDTH - j, :] * ext_ref[halo - j:halo - j + tm, :]
    ext_ref[0:halo, :] = ext_ref[tm:tm + halo, :]
    qk = acc * _sigmoid(acc) + jnp.tile(anchor, (tm // SUBLANES, 2 * dm // LANES))
    q_ref[...] = qk[:, 0:dm].astype(BF16)
    kt_ref[...] = (qk[:, dm:2 * dm] * k_scale).T.astype(BF16)


def _inproj(x2, g1, w_in_t, gb, cw, *, seq, tm, d_mlstm, n_oct):
    T, D = x2.shape
    dm = d_mlstm
    H = N_MLSTM_HEADS
    assert 2 * H == SUBLANES
    ds5 = w_in_t.shape[0] - 4 * dm - 2 * H
    kern = functools.partial(_inproj_kernel, tiles_per_seq=seq // tm, d_mlstm=dm,
                             k_scale=float((dm // H) ** -0.5))
    tok = lambda w: pl.BlockSpec((tm, w), lambda i: (i, 0))
    return pl.pallas_call(
        kern,
        grid=(T // tm,),
        in_specs=[tok(D), _const_spec((1, D)), _const_spec(w_in_t.shape), _const_spec((1, LANES)),
                  _const_spec(cw.shape)],
        out_specs=[tok(dm), pl.BlockSpec((dm, tm), lambda i: (0, i)), tok(dm), tok(dm),
                   pl.BlockSpec((2 * H, tm), lambda i: (0, i)),
                   pl.BlockSpec((n_oct, tm // n_oct, ds5), lambda i: (0, i, 0))],
        out_shape=[jax.ShapeDtypeStruct((T, dm), BF16), jax.ShapeDtypeStruct((dm, T), BF16),
                   jax.ShapeDtypeStruct((T, dm), BF16), jax.ShapeDtypeStruct((T, dm), BF16),
                   jax.ShapeDtypeStruct((2 * H, T), F32),
                   jax.ShapeDtypeStruct((n_oct, T // n_oct, ds5), F32)],
        scratch_shapes=[pltpu.VMEM((tm + 2 * SUBLANES, 2 * dm), F32), pltpu.VMEM((D, 4 * dm), BF16),
                        pltpu.VMEM((D, LANES), BF16), pltpu.VMEM((D, ds5), BF16)],
        compiler_params=pltpu.CompilerParams(dimension_semantics=("arbitrary",),
                                             vmem_limit_bytes=VMEM_LIMIT),
        name="inproj",
    )(x2, g1, w_in_t, gb, cw)


def _lane_scan(x, op, identity):
    n = x.shape[-1]
    pos = lax.broadcasted_iota(jnp.int32, x.shape, x.ndim - 1)
    sh = 1
    while sh < n:
        x = op(x, jnp.where(pos >= sh, pltpu.roll(x, sh, x.ndim - 1), identity))
        sh *= 2
    return x


def _mlstm_kernel(q_ref, kt_ref, v_ref, o_ref, gatest_ref, *refs, chunk, n_cast):
    S = q_ref.shape[0]
    H = N_MLSTM_HEADS
    dh = q_ref.shape[1] // H
    L = chunk
    assert L == dh == LANES
    w32_refs, ws_refs = refs[0:n_cast], refs[n_cast:2 * n_cast]
    out_ref, w16_refs = refs[2 * n_cast], refs[2 * n_cast + 1:3 * n_cast + 1]
    a_ref, g0_ref, nm_ref, gend_ref = refs[3 * n_cast + 1:3 * n_cast + 5]
    c_refs, nd_refs = refs[3 * n_cast + 5:3 * n_cast + 5 + H], refs[3 * n_cast + 5 + H:]

    for w32_ref, ws_ref, w16_ref in zip(w32_refs, ws_refs, w16_refs):
        w16_ref[...] = (w32_ref[...] * ws_ref[...]).astype(BF16)

    gates = gatest_ref[...]
    f_cum = pltpu.roll(_lane_scan(_log_sigmoid(gates), jnp.add, 0.0), H, 0)
    a = gates - f_cum
    g0 = jnp.maximum(_lane_scan(a, jnp.maximum, -jnp.inf), 0.0)
    a_ref[...] = a
    g0_ref[...] = g0
    nm_ref[...] = -(f_cum + g0)
    for c in range(S // L):
        gend_ref[:, c * L:(c + 1) * L] = jnp.broadcast_to(g0[:, (c + 1) * L - 1:(c + 1) * L], (SUBLANES, L))

    for c_ref in c_refs:
        c_ref[...] = jnp.zeros(c_ref.shape, F32)
    causal = (lax.broadcasted_iota(jnp.int32, (L, L), 0) >= lax.broadcasted_iota(jnp.int32, (L, L), 1))
    ones_blk = jnp.ones((L, dh), BF16)
    mean_mat = jnp.full((dh, dh), 1.0 / dh, BF16)

    def mix(c, h, g_prev):
        r0 = pl.multiple_of(c * L, L)
        hs = slice(h * dh, (h + 1) * dh)
        qc = q_ref[pl.ds(r0, L), hs]
        ktc = kt_ref[hs, pl.ds(r0, L)]
        v_aug = jnp.concatenate([v_ref[pl.ds(r0, L), hs], ones_blk], axis=1)
        a_row = a_ref[h:h + 1, pl.ds(r0, L)]
        g_end = gend_ref[h:h + 1, pl.ds(r0, L)]
        g0_t = jnp.broadcast_to(g0_ref[h:h + 1, pl.ds(r0, L)], (L, L)).T

        s_qk = jnp.dot(qc, ktc, preferred_element_type=F32)
        p = (jnp.exp(jnp.where(causal, a_row - g0_t, -jnp.inf)) * s_qk).astype(BF16)
        c_prev = c_refs[h][...]
        q_w = (qc.astype(F32) * jnp.exp(g_prev - g0_t)).astype(BF16)
        nd_refs[h][...] = jnp.dot(jnp.concatenate([q_w, p], axis=1),
                                  jnp.concatenate([c_prev.astype(BF16), v_aug], axis=0),
                                  preferred_element_type=F32)

        kte = (ktc.astype(F32) * jnp.exp(a_row - g_end)).astype(BF16)
        decay = jnp.exp(g_prev - g_end)
        c_refs[h][...] = (jnp.concatenate([decay, decay], axis=1) * c_prev
                          + jnp.dot(kte, v_aug, preferred_element_type=F32))
        return g_end

    def emit(c, h):
        r0 = pl.multiple_of(c * L, L)
        hs = slice(h * dh, (h + 1) * dh)
        nm_t = jnp.broadcast_to(nm_ref[h:h + 1, pl.ds(r0, L)], (L, L)).T
        nd = nd_refs[h][...]
        h_tilde = nd[:, 0:dh] / jnp.maximum(jnp.abs(nd[:, dh:2 * dh]), jnp.exp(nm_t))
        hm = _sigmoid(o_ref[pl.ds(r0, L), hs].astype(F32)) * h_tilde
        mu = jnp.dot(hm.astype(BF16), mean_mat, preferred_element_type=F32)
        hc = hm - mu
        var = jnp.dot((hc * hc).astype(BF16), mean_mat, preferred_element_type=F32)
        out_ref[pl.ds(r0, L), hs] = (hc * lax.rsqrt(var + EPS)).astype(BF16)

    zero = jnp.zeros((1, L), F32)
    g_first = tuple(mix(0, h, zero) for h in range(H))

    def body(c, g_carry):
        for h in range(H):
            emit(c - 1, h)
        return tuple(mix(c, h, g_carry[h]) for h in range(H))

    lax.fori_loop(1, S // L, body, g_first, unroll=5)
    for h in range(H):
        emit(S // L - 1, h)


def _mlstm(q, kt, v, o, gatest, weights, row_scales, *, batch, seq):
    T, dm = q.shape
    dh = dm // N_MLSTM_HEADS
    kern = functools.partial(_mlstm_kernel, chunk=MLSTM_CHUNK, n_cast=len(weights))
    tok = lambda w: pl.BlockSpec((seq, w), lambda b: (b, 0))
    rows = pltpu.VMEM((SUBLANES, seq), F32)
    bf16_rows = 2 * SUBLANES
    assert all(w.shape[0] % (batch * bf16_rows) == 0 for w in weights)
    w_specs = [pl.BlockSpec((w.shape[0] // batch, w.shape[1]), lambda b: (b, 0)) for w in weights]
    s_specs = [pl.BlockSpec((w.shape[0] // batch, 1), lambda b: (b, 0)) for w in weights]
    scales = [sc.reshape(-1, 1).astype(F32) for sc in row_scales]
    return pl.pallas_call(
        kern,
        grid=(batch,),
        in_specs=[tok(dm), pl.BlockSpec((dm, seq), lambda b: (0, b)), tok(dm), tok(dm),
                  pl.BlockSpec((SUBLANES, seq), lambda b: (0, b))] + w_specs + s_specs,
        out_specs=[tok(dm)] + w_specs,
        out_shape=[jax.ShapeDtypeStruct((T, dm), BF16)] + [jax.ShapeDtypeStruct(w.shape, BF16) for w in weights],
        scratch_shapes=([rows, rows, rows, rows] + [pltpu.VMEM((dh, 2 * dh), F32)] * N_MLSTM_HEADS
                        + [pltpu.VMEM((MLSTM_CHUNK, 2 * dh), F32)] * N_MLSTM_HEADS),
        compiler_params=pltpu.CompilerParams(dimension_semantics=("arbitrary",),
                                             vmem_limit_bytes=VMEM_LIMIT),
        name="mlstm",
    )(q, kt, v, o, gatest, *weights, *scales)


def _s5_prep_kernel(arow_ref, acol_ref, bre_ref, bim_ref, cre_ref, cim_ref, drow_ref,
                    rep_ref, rep2_ref, selrev_ref, pwcol_ref, w1_ref, wout_ref, al_ref, *, chunk):
    L = chunk
    P = S5_STATE
    Hc = S5_GROUP
    R = L * Hc
    hi = lax.Precision.HIGHEST

    def select(x, sel_ref):
        x_hi = x.astype(BF16)
        x_lo = (x - x_hi.astype(F32)).astype(BF16)
        sel = sel_ref[...]
        return (jnp.dot(x_hi, sel, preferred_element_type=F32) + jnp.dot(x_lo, sel, preferred_element_type=F32))

    dt_r = jnp.exp(arow_ref[2:3, :])
    da_re, da_im = dt_r * arow_ref[0:1, :], dt_r * arow_ref[1:2, :]

    def abar_pow(tau):
        mag = jnp.exp(tau * da_re)
        ang = tau * da_im
        return mag * jnp.cos(ang), mag * jnp.sin(ang)

    n_pow = -(-(L + 1) // SUBLANES) * SUBLANES
    er_re, er_im = abar_pow(lax.broadcasted_iota(jnp.int32, (n_pow, 1), 0).astype(F32))
    pad = jnp.zeros((2 * P - n_pow, 2 * P), F32)
    ec_re = jnp.concatenate([er_re, pad], axis=0).T
    ec_im = jnp.concatenate([er_im, pad], axis=0).T

    a_re_c, a_im_c = acol_ref[:, 0:1], acol_ref[:, 1:2]
    ab_re, ab_im = ec_re[:, 1:2], ec_im[:, 1:2]
    den = a_re_c * a_re_c + a_im_c * a_im_c
    nr = ab_re - 1.0
    z_re = (nr * a_re_c + ab_im * a_im_c) / den
    z_im = (ab_im * a_re_c - nr * a_im_c) / den
    bb_re = z_re * bre_ref[...] - z_im * bim_ref[...]
    bb_im = z_re * bim_ref[...] + z_im * bre_ref[...]
    first = lax.broadcasted_iota(jnp.int32, (2 * P, Hc), 0) < P
    bd = lambda b: jnp.concatenate([jnp.where(first, b, 0.0), jnp.where(first, 0.0, b)], axis=1)

    def rows(e, lo):
        return jnp.concatenate([jnp.broadcast_to(e[lo + t:lo + t + 1, :], (Hc, 2 * P)) for t in range(L)], axis=0)

    c_re = jnp.concatenate([cre_ref[...]] * L, axis=0)
    c_im = jnp.concatenate([cim_ref[...]] * L, axis=0)

    def c_times_pow(lo):
        e_re, e_im = rows(er_re, lo), rows(er_im, lo)
        return c_re * e_re - c_im * e_im, c_re * e_im + c_im * e_re

    ce_re, ce_im = c_times_pow(0)
    kflat = (jnp.dot(ce_re, bd(bb_re), preferred_element_type=F32, precision=hi)
             - jnp.dot(ce_im, bd(bb_im), preferred_element_type=F32, precision=hi))
    row = lax.broadcasted_iota(jnp.int32, (R, 2 * Hc), 0)
    col = lax.broadcasted_iota(jnp.int32, (R, 2 * Hc), 1) & (Hc - 1)
    kflat = kflat + jnp.where(row == col, drow_ref[...], 0.0)

    m = select(kflat, rep2_ref)
    blk = (lax.broadcasted_iota(jnp.int32, (R, 2 * R), 1) & (R - 1)) // Hc
    sh = 1
    while sh < L:
        shifted = jnp.concatenate([jnp.zeros((sh * Hc, 2 * R), F32), m[0:R - sh * Hc, :]], axis=0)
        m = jnp.where((blk & sh) != 0, shifted, m)
        sh *= 2

    ev_re, ev_im = select(ec_re, selrev_ref), select(ec_im, selrev_ref)
    bt_re, bt_im = select(bb_re, rep_ref), select(bb_im, rep_ref)
    st_re = (ev_re * bt_re - ev_im * bt_im).astype(BF16)
    st_im = (ev_re * bt_im + ev_im * bt_re).astype(BF16)

    co_re, co_im = c_times_pow(1)
    lane_first = lax.broadcasted_iota(jnp.int32, (R, 2 * P), 1) < P
    for g in range(2):
        w1_ref[g, 0:R, :] = m[:, g * R:(g + 1) * R].astype(BF16)
        w1_ref[g, R:R + P, :] = st_re[g * P:(g + 1) * P, :]
        w1_ref[g, R + P:R + 2 * P, :] = st_im[g * P:(g + 1) * P, :]
        own = lane_first if g == 0 else jnp.logical_not(lane_first)
        wout_ref[g, :, 0:2 * P] = jnp.where(own, co_re, 0.0).astype(BF16)
        wout_ref[g, :, 2 * P:4 * P] = jnp.where(own, -co_im, 0.0).astype(BF16)

    sc_re, sc_im = abar_pow(pwcol_ref[...])
    al_ref[0:SUBLANES, :] = sc_re
    al_ref[SUBLANES:2 * SUBLANES, :] = sc_im


def _s5_prep(a_re, a_im, log_dt, b_re, b_im, c_re, c_im, d_skip, *, chunk, n_chunks):
    G, P = a_re.shape
    Hc = S5_GROUP
    R = chunk * Hc
    assert G % 2 == 0 and 2 * P == LANES
    G2 = G // 2
    ldt = jnp.broadcast_to(log_dt[:, None], (G, P))
    arow = jnp.stack([a_re.reshape(G2, 2 * P), a_im.reshape(G2, 2 * P), ldt.reshape(G2, 2 * P)], axis=1)
    acol = jnp.stack([a_re.reshape(G2, 2 * P), a_im.reshape(G2, 2 * P)], axis=2)
    pair_lanes = lambda c: c.reshape(G2, 2, Hc, P).transpose(0, 2, 1, 3).reshape(G2, Hc, 2 * P)
    drow = d_skip.reshape(G2, 1, 2 * Hc)
    rep = (jnp.arange(R)[None, :] % Hc == jnp.arange(Hc)[:, None]).astype(BF16)
    rep2 = jnp.kron(jnp.eye(2, dtype=BF16), rep)
    selrev = (jnp.arange(LANES)[:, None] == chunk - 1 - jnp.arange(R)[None, :] // Hc).astype(BF16)
    n_steps = max(1, (n_chunks - 1).bit_length())
    assert chunk + 1 <= LANES and n_steps <= SUBLANES
    pwcol = jnp.zeros((SUBLANES, 1), F32).at[:n_steps, 0].set(chunk * 2.0 ** jnp.arange(n_steps))
    kern = functools.partial(_s5_prep_kernel, chunk=chunk)
    pair = lambda *s: pl.BlockSpec((None,) + s, lambda g: (g,) + (0,) * len(s))
    two = lambda *s: pl.BlockSpec((2,) + s, lambda g: (g,) + (0,) * len(s))
    return pl.pallas_call(
        kern,
        grid=(G2,),
        in_specs=[pair(3, 2 * P), pair(2 * P, 2), pair(2 * P, Hc), pair(2 * P, Hc), pair(Hc, 2 * P), pair(Hc, 2 * P),
                  pair(1, 2 * Hc), _const_spec((Hc, R)), _const_spec((2 * Hc, 2 * R)), _const_spec((LANES, R)),
                  _const_spec((SUBLANES, 1))],
        out_specs=[two(R + 2 * P, R), two(R, 4 * P), pair(2 * SUBLANES, 2 * P)],
        out_shape=[jax.ShapeDtypeStruct((G, R + 2 * P, R), BF16),
                   jax.ShapeDtypeStruct((G, R, 4 * P), BF16),
                   jax.ShapeDtypeStruct((G2, 2 * SUBLANES, 2 * P), F32)],
        compiler_params=pltpu.CompilerParams(dimension_semantics=("arbitrary",),
                                             vmem_limit_bytes=VMEM_LIMIT),
        name="s5_prep",
    )(arow, acol, b_re.reshape(G2, 2 * P, Hc), b_im.reshape(G2, 2 * P, Hc), pair_lanes(c_re), pair_lanes(c_im),
      drow, rep, rep2, selrev, pwcol)


def _s5_kernel(*refs, chunk, n_chunks):
    L = chunk
    n_oct = L // SUBLANES
    u_refs, (w1_ref, wout_ref, al_ref, y_ref, ut_ref, yt_ref) = refs[0:n_oct], refs[n_oct:]
    P = S5_STATE
    Hc = S5_GROUP
    R = L * Hc
    C = y_ref.shape[0] // L
    n_grp = LANES // Hc

    for s in range(L):
        xt = u_refs[s // SUBLANES][pl.ds(s % SUBLANES, C, stride=SUBLANES), :].astype(BF16).T
        for j in range(n_grp):
            ut_ref[j, s * Hc:(s + 1) * Hc, :] = xt[j * Hc:(j + 1) * Hc, :]

    seg_pos = lax.broadcasted_iota(jnp.int32, (C, 2 * P), 0) & (n_chunks - 1)
    nt = (((1,), (1,)), ((), ()))

    def shifted(x, sh):
        return jnp.where(seg_pos >= sh, pltpu.roll(x, sh, 0), 0.0)

    def group_pair(i, carry):
        j0 = 2 * i
        res = [jnp.dot(w1_ref[j0 + d], ut_ref[j0 + d], preferred_element_type=F32) for d in range(2)]
        x_re = jnp.concatenate([res[0][R:R + P], res[1][R:R + P]], axis=0).T
        x_im = jnp.concatenate([res[0][R + P:R + 2 * P], res[1][R + P:R + 2 * P]], axis=0).T
        sh, k = 1, 0
        while sh < n_chunks:
            a_re = al_ref[i, k:k + 1, :]
            a_im = al_ref[i, SUBLANES + k:SUBLANES + k + 1, :]
            s_re, s_im = shifted(x_re, sh), shifted(x_im, sh)
            x_re, x_im = (x_re + a_re * s_re - a_im * s_im, x_im + a_re * s_im + a_im * s_re)
            sh *= 2
            k += 1
        xprev = jnp.concatenate([shifted(x_re, 1), shifted(x_im, 1)], axis=1).astype(BF16)
        for d in range(2):
            yt_ref[j0 + d] = res[d][0:R, :] + lax.dot_general(wout_ref[j0 + d], xprev, nt,
                                                              preferred_element_type=F32)
        return carry

    lax.fori_loop(0, n_grp // 2, group_pair, 0, unroll=2)

    for t in range(L):
        zt = jnp.concatenate([yt_ref[j, t * Hc:(t + 1) * Hc, :] for j in range(n_grp)], axis=0)
        y_ref[pl.ds(t, C, stride=L), :] = zt.T


def _s5(u, w1, wout, al, *, chunk, n_chunks, n_col_blocks):
    n_oct, t_oct, ds5 = u.shape
    T = n_oct * t_oct
    assert n_oct * SUBLANES == chunk
    G, R1, R = w1.shape
    P4 = wout.shape[-1]
    n_grp = LANES // S5_GROUP
    n_q = ds5 // LANES
    tb = T // n_col_blocks
    C = tb // chunk
    kern = functools.partial(_s5_kernel, chunk=chunk, n_chunks=n_chunks)
    grp = lambda *s: pl.BlockSpec((n_grp,) + s, lambda q, i: (q,) + (0,) * len(s))
    octets = [pl.BlockSpec((None, C * SUBLANES, LANES), functools.partial(lambda q, i, o: (o, i, q), o=o))
              for o in range(n_oct)]
    return pl.pallas_call(
        kern,
        grid=(n_q, n_col_blocks),
        in_specs=octets + [grp(R1, R), grp(R, P4),
                           pl.BlockSpec((n_grp // 2, 2 * SUBLANES, P4 // 2), lambda q, i: (q, 0, 0))],
        out_specs=pl.BlockSpec((tb, LANES), lambda q, i: (i, q)),
        out_shape=jax.ShapeDtypeStruct((T, ds5), F32),
        scratch_shapes=[pltpu.VMEM((n_grp, R, C), BF16), pltpu.VMEM((n_grp, R, C), F32)],
        compiler_params=pltpu.CompilerParams(dimension_semantics=("arbitrary", "arbitrary"),
                                             vmem_limit_bytes=BIG_VMEM_LIMIT),
        name="s5",
    )(*([u] * n_oct), w1, wout, al)


def _out_kernel(x_ref, hm_ref, y_ref, wglu_ref, bglu_ref, wo_ref,
                wgate_ref, wup_ref, wdown_ref, gf_ref, out_ref, acc_ref, xn_ref, *, ff_chunk, final_norm):
    dm = hm_ref.shape[1]
    g = jax.nn.gelu(y_ref[...])
    z = jnp.dot(g.astype(BF16), wglu_ref[...], preferred_element_type=F32) + bglu_ref[...]
    gg = g * _sigmoid(z)
    hs = (gg * lax.rsqrt(jnp.mean(gg * gg, axis=-1, keepdims=True) + EPS)).astype(BF16)
    h1 = (x_ref[...] + jnp.dot(hm_ref[...], wo_ref[0:dm, :], preferred_element_type=F32)
          + jnp.dot(hs, wo_ref[dm:, :], preferred_element_type=F32))
    xn_ref[...] = (h1 * lax.rsqrt(jnp.mean(h1 * h1, axis=-1, keepdims=True) + EPS)).astype(BF16)
    acc_ref[...] = h1

    def ffn_chunk(c, carry):
        cs = pl.ds(pl.multiple_of(c * ff_chunk, ff_chunk), ff_chunk)
        xn2 = xn_ref[...]
        gate = jnp.dot(xn2, wgate_ref[:, cs], preferred_element_type=F32)
        up = jnp.dot(xn2, wup_ref[:, cs], preferred_element_type=F32)
        act = (gate * _sigmoid(gate) * up).astype(BF16)
        acc_ref[...] += jnp.dot(act, wdown_ref[cs, :], preferred_element_type=F32)
        return carry

    lax.fori_loop(0, wgate_ref.shape[1] // ff_chunk, ffn_chunk, 0, unroll=4)
    out_ref[...] = _rmsnorm(acc_ref[...], gf_ref[...]) if final_norm else acc_ref[...]


def _out_block(x2, hm, y, wglu, bglu, wo, wgate, wup, wdown, gf, *, tm, ff_chunk, final_norm):
    T, D = x2.shape
    dm = hm.shape[1]
    ds5 = y.shape[1]
    assert wgate.shape[1] % ff_chunk == 0
    kern = functools.partial(_out_kernel, ff_chunk=ff_chunk, final_norm=final_norm)
    tok = lambda w: pl.BlockSpec((tm, w), lambda i: (i, 0))
    return pl.pallas_call(
        kern,
        grid=(T // tm,),
        in_specs=[tok(D), tok(dm), tok(ds5), _const_spec(wglu.shape), _const_spec((1, ds5)), _const_spec(wo.shape),
                  _const_spec(wgate.shape), _const_spec(wup.shape), _const_spec(wdown.shape),
                  _const_spec((1, D))],
        out_specs=tok(D),
        out_shape=jax.ShapeDtypeStruct((T, D), F32),
        scratch_shapes=[pltpu.VMEM((tm, D), F32), pltpu.VMEM((tm, D), BF16)],
        compiler_params=pltpu.CompilerParams(dimension_semantics=("arbitrary",),
                                             vmem_limit_bytes=BIG_VMEM_LIMIT),
        name="out_block",
    )(x2, hm, y, wglu, bglu, wo, wgate, wup, wdown, gf)


def kernel(x, norm1_g, w_in, if_bias, conv_qk, mlstm_norm_g, a_re, a_im, log_dt, b_re, b_im, c_re, c_im,
           d_skip, w_glu, b_glu, s5_norm_g, w_out, norm2_g, w_gate, w_up, w_down, norm_f_g):
    B, S, D = x.shape
    depth = w_in.shape[0]
    H = N_MLSTM_HEADS
    dm = mlstm_norm_g.shape[1]
    T = B * S
    L5 = S5_CHUNK
    nc5 = S // L5
    row = lambda a: a.reshape(1, -1).astype(F32)

    h = x.reshape(T, D)
    for l in range(depth):
        last = l == depth - 1
        gb = jnp.zeros((1, LANES), F32).at[0, 0:2 * H].set(if_bias[l])

        q, kt, v, o, gatest, u = _inproj(h, row(norm1_g[l]), w_in[l].T.astype(F32), gb, conv_qk[l].astype(F32),
                                         seq=S, tm=min(INPROJ_TM, S), d_mlstm=dm, n_oct=L5 // SUBLANES)
        ones = lambda n: jnp.ones((n,), F32)
        g_mix = jnp.concatenate([mlstm_norm_g[l], s5_norm_g[l]]).astype(F32)
        g_ffn = norm2_g[l].astype(F32)
        hm, wglu, wo, wgate, wup, wdown = _mlstm(
            q, kt, v, o, gatest,
            [w.astype(F32) for w in (w_glu[l], w_out[l], w_gate[l], w_up[l], w_down[l])],
            [ones(w_glu.shape[1]), g_mix, g_ffn, g_ffn, ones(w_down.shape[1])], batch=B, seq=S)

        w1, wout, al = _s5_prep(a_re[l], a_im[l], log_dt[l], b_re[l], b_im[l], c_re[l], c_im[l],
                                d_skip[l], chunk=L5, n_chunks=nc5)
        y = _s5(u, w1, wout, al, chunk=L5, n_chunks=nc5, n_col_blocks=S5_COL_BLOCKS)

        h = _out_block(h, hm, y, wglu, row(b_glu[l]), wo, wgate, wup, wdown, row(norm_f_g),
                       tm=OUT_TM, ff_chunk=FF_CHUNK, final_norm=last)
    return h.reshape(B, S, D)
```

```python
import functools

import jax
import jax.numpy as jnp
from jax import lax
from jax.experimental import pallas as pl
from jax.experimental.pallas import tpu as pltpu

EPS = 1e-6
N_MLSTM_HEADS = 4
CONV_WIDTH = 4
S5_GROUP = 16
S5_STATE = 64

LANES = 128
SUBLANES = 8
MLSTM_CHUNK = 128
S5_CHUNK = 16
S5_COL_BLOCKS = 1
INPROJ_TM = 1024
OUT_TM = 1024
FF_CHUNK = 256
VMEM_LIMIT = 48 * 1024 * 1024
BIG_VMEM_LIMIT = 58 * 1024 * 1024

F32 = jnp.float32
BF16 = jnp.bfloat16


def _rmsnorm(x, g):
    ms = jnp.mean(x * x, axis=-1, keepdims=True)
    return x * lax.rsqrt(ms + EPS) * g


def _sigmoid(x):
    return 0.5 * jnp.tanh(0.5 * x) + 0.5


def _silu(x):
    h = 0.5 * x
    return h + h * jnp.tanh(h)


def _log_sigmoid(x):
    return -(jnp.maximum(-x, 0.0) + jnp.log1p(jnp.exp(-jnp.abs(x))))


def _const_spec(shape):
    return pl.BlockSpec(shape, lambda *_: (0,) * len(shape), pipeline_mode=pl.Buffered(1))


def _inproj_kernel(x_ref, g1_ref, wt_ref, gb_ref, cw_ref,
                   q_ref, kt_ref, v_ref, o_ref, gatest_ref, u_ref,
                   ext_ref, wm_ref, wg_ref, wu_ref, *, tiles_per_seq, d_mlstm, k_scale):
    tm = x_ref.shape[0]
    dm = d_mlstm
    H = N_MLSTM_HEADS
    halo = SUBLANES

    @pl.when(pl.program_id(0) == 0)
    def _():
        g1 = g1_ref[...]
        for c0 in range(0, 4 * dm, dm):
            wm_ref[:, c0:c0 + dm] = (wt_ref[c0:c0 + dm, :] * g1).T.astype(BF16)
        wu_ref[...] = (wt_ref[4 * dm + 2 * H:, :] * g1).T.astype(BF16)
        fill = jnp.zeros((LANES - 2 * H, wt_ref.shape[1]), F32)
        wg_ref[...] = jnp.concatenate([wt_ref[4 * dm:4 * dm + 2 * H, :] * g1, fill], axis=0).T.astype(BF16)

    x = x_ref[...]
    xn = (x * lax.rsqrt(jnp.mean(x * x, axis=-1, keepdims=True) + EPS)).astype(BF16)

    @pl.when(pl.program_id(0) % tiles_per_seq == 0)
    def _():
        ext_ref[0:halo, :] = jnp.zeros((halo, 2 * dm), F32)

    ext_ref[halo:halo + tm, :] = jnp.dot(xn, wm_ref[:, 0:2 * dm], preferred_element_type=F32)
    v = jnp.dot(xn, wm_ref[:, 2 * dm:3 * dm], preferred_element_type=F32)
    o = jnp.dot(xn, wm_ref[:, 3 * dm:4 * dm], preferred_element_type=F32)
    u = jnp.dot(xn, wu_ref[...], preferred_element_type=F32)
    gates = jnp.dot(xn, wg_ref[...], preferred_element_type=F32) + gb_ref[...]
    v_ref[...] = v.astype(BF16)
    o_ref[...] = o.astype(BF16)
    n_oct = u_ref.shape[0]
    u4 = u.reshape(tm // (n_oct * SUBLANES), n_oct, SUBLANES, u.shape[1])
    for oc in range(n_oct):
        u_ref[oc] = u4[:, oc].reshape(tm // n_oct, u.shape[1])
    gatest_ref[...] = gates.T[0:2 * H, :]

    def zero_of(t):
        w = pltpu.bitcast(t[tm - SUBLANES:tm, t.shape[1] - LANES:], jnp.uint32)
        return pltpu.bitcast((w >> 16) >> 16, F32)
    anchor = zero_of(v)

    acc = cw_ref[CONV_WIDTH - 1:CONV_WIDTH, :] * ext_ref[halo:halo + tm, :]
    for j in range(1, CONV_WIDTH):
        acc = acc + cw_ref[CONV_WIDTH - 1 - j:CONV_WIDTH - j, :] * ext_ref[halo - j:halo - j + tm, :]
    ext_ref[0:halo, :] = ext_ref[tm:tm + halo, :]
    qk = _silu(acc) + jnp.tile(anchor, (tm // SUBLANES, 2 * dm // LANES))
    q_ref[...] = qk[:, 0:dm].astype(BF16)
    kt_ref[...] = (qk[:, dm:2 * dm] * k_scale).T.astype(BF16)


def _inproj(x2, g1, w_in_t, gb, cw, *, seq, tm, d_mlstm, n_oct):
    T, D = x2.shape
    dm = d_mlstm
    H = N_MLSTM_HEADS
    assert 2 * H == SUBLANES
    ds5 = w_in_t.shape[0] - 4 * dm - 2 * H
    kern = functools.partial(_inproj_kernel, tiles_per_seq=seq // tm, d_mlstm=dm,
                             k_scale=float((dm // H) ** -0.5))
    tok = lambda w: pl.BlockSpec((tm, w), lambda i: (i, 0))
    return pl.pallas_call(
        kern,
        grid=(T // tm,),
        in_specs=[tok(D), _const_spec((1, D)), _const_spec(w_in_t.shape), _const_spec((1, LANES)),
                  _const_spec(cw.shape)],
        out_specs=[tok(dm), pl.BlockSpec((dm, tm), lambda i: (0, i)), tok(dm), tok(dm),
                   pl.BlockSpec((2 * H, tm), lambda i: (0, i)),
                   pl.BlockSpec((n_oct, tm // n_oct, ds5), lambda i: (0, i, 0))],
        out_shape=[jax.ShapeDtypeStruct((T, dm), BF16), jax.ShapeDtypeStruct((dm, T), BF16),
                   jax.ShapeDtypeStruct((T, dm), BF16), jax.ShapeDtypeStruct((T, dm), BF16),
                   jax.ShapeDtypeStruct((2 * H, T), F32),
                   jax.ShapeDtypeStruct((n_oct, T // n_oct, ds5), F32)],
        scratch_shapes=[pltpu.VMEM((tm + 2 * SUBLANES, 2 * dm), F32), pltpu.VMEM((D, 4 * dm), BF16),
                        pltpu.VMEM((D, LANES), BF16), pltpu.VMEM((D, ds5), BF16)],
        compiler_params=pltpu.CompilerParams(dimension_semantics=("arbitrary",),
                                             vmem_limit_bytes=VMEM_LIMIT),
        name="inproj",
    )(x2, g1, w_in_t, gb, cw)


def _lane_scan(x, op, identity, seg):
    pos = lax.broadcasted_iota(jnp.int32, x.shape, x.ndim - 1) & (seg - 1)
    sh = 1
    while sh < seg:
        x = op(x, jnp.where(pos >= sh, pltpu.roll(x, sh, x.ndim - 1), identity))
        sh *= 2
    return x


def _mlstm_kernel(q_ref, kt_ref, v_ref, o_ref, gatest_ref, *refs, chunk, n_cast):
    S = q_ref.shape[0]
    H = N_MLSTM_HEADS
    dh = q_ref.shape[1] // H
    L = chunk
    assert L == dh == LANES
    w32_refs, ws_refs = refs[0:n_cast], refs[n_cast:2 * n_cast]
    out_ref, w16_refs = refs[2 * n_cast], refs[2 * n_cast + 1:3 * n_cast + 1]
    a_ref, g0_ref, nm_ref, gend_ref = refs[3 * n_cast + 1:3 * n_cast + 5]
    c_refs, nd_refs = refs[3 * n_cast + 5:3 * n_cast + 5 + H], refs[3 * n_cast + 5 + H:]

    for w32_ref, ws_ref, w16_ref in zip(w32_refs, ws_refs, w16_refs):
        w16_ref[...] = (w32_ref[...] * ws_ref[...]).astype(BF16)

    @pl.when(pl.program_id(0) == 0)
    def _():
        gates = gatest_ref[...]
        f_cum = pltpu.roll(_lane_scan(_log_sigmoid(gates), jnp.add, 0.0, S), H, 0)
        a = gates - f_cum
        g0 = jnp.maximum(_lane_scan(a, jnp.maximum, -jnp.inf, S), 0.0)
        a_ref[...] = a
        g0_ref[...] = g0
        nm_ref[...] = -(f_cum + g0)
        for c in range(gatest_ref.shape[1] // L):
            gend_ref[:, c * L:(c + 1) * L] = jnp.broadcast_to(g0[:, (c + 1) * L - 1:(c + 1) * L], (SUBLANES, L))

    t0 = pl.program_id(0) * S

    for c_ref in c_refs:
        c_ref[...] = jnp.zeros(c_ref.shape, F32)
    causal = (lax.broadcasted_iota(jnp.int32, (L, L), 0) >= lax.broadcasted_iota(jnp.int32, (L, L), 1))
    ones_blk = jnp.ones((L, dh), BF16)
    mean_mat = jnp.full((dh, dh), 1.0 / dh, BF16)

    def mix(c, h, g_prev):
        r0 = pl.multiple_of(c * L, L)
        hs = slice(h * dh, (h + 1) * dh)
        qc = q_ref[pl.ds(r0, L), hs]
        ktc = kt_ref[hs, pl.ds(r0, L)]
        v_aug = jnp.concatenate([v_ref[pl.ds(r0, L), hs], ones_blk], axis=1)
        g0l = pl.ds(pl.multiple_of(t0 + r0, L), L)
        a_row = a_ref[h:h + 1, g0l]
        g_end = gend_ref[h:h + 1, g0l]
        g0_t = jnp.broadcast_to(g0_ref[h:h + 1, g0l], (L, L)).T

        s_qk = jnp.dot(qc, ktc, preferred_element_type=F32)
        p = (jnp.exp(jnp.where(causal, a_row - g0_t, -jnp.inf)) * s_qk).astype(BF16)
        c_prev = c_refs[h][...]
        q_w = (qc.astype(F32) * jnp.exp(g_prev - g0_t)).astype(BF16)
        nd_refs[h][...] = jnp.dot(jnp.concatenate([q_w, p], axis=1),
                                  jnp.concatenate([c_prev.astype(BF16), v_aug], axis=0),
                                  preferred_element_type=F32)

        kte = (ktc.astype(F32) * jnp.exp(a_row - g_end)).astype(BF16)
        decay = jnp.exp(g_prev - g_end)
        c_refs[h][...] = (jnp.concatenate([decay, decay], axis=1) * c_prev
                          + jnp.dot(kte, v_aug, preferred_element_type=F32))
        return g_end

    def emit(c, h):
        r0 = pl.multiple_of(c * L, L)
        hs = slice(h * dh, (h + 1) * dh)
        nm_t = jnp.broadcast_to(nm_ref[h:h + 1, pl.ds(pl.multiple_of(t0 + r0, L), L)], (L, L)).T
        nd = nd_refs[h][...]
        h_tilde = nd[:, 0:dh] / jnp.maximum(jnp.abs(nd[:, dh:2 * dh]), jnp.exp(nm_t))
        hm = _sigmoid(o_ref[pl.ds(r0, L), hs].astype(F32)) * h_tilde
        mu = jnp.dot(hm.astype(BF16), mean_mat, preferred_element_type=F32)
        hc = hm - mu
        var = jnp.dot((hc * hc).astype(BF16), mean_mat, preferred_element_type=F32)
        out_ref[pl.ds(r0, L), hs] = (hc * lax.rsqrt(var + EPS)).astype(BF16)

    zero = jnp.zeros((1, L), F32)
    g_first = tuple(mix(0, h, zero) for h in range(H))

    def body(c, g_carry):
        for h in range(H):
            emit(c - 1, h)
        return tuple(mix(c, h, g_carry[h]) for h in range(H))

    lax.fori_loop(1, S // L, body, g_first, unroll=5)
    for h in range(H):
        emit(S // L - 1, h)


def _mlstm(q, kt, v, o, gatest, weights, row_scales, *, batch, seq):
    T, dm = q.shape
    dh = dm // N_MLSTM_HEADS
    kern = functools.partial(_mlstm_kernel, chunk=MLSTM_CHUNK, n_cast=len(weights))
    tok = lambda w: pl.BlockSpec((seq, w), lambda b: (b, 0))
    rows = pltpu.VMEM((SUBLANES, T), F32)
    bf16_rows = 2 * SUBLANES
    assert all(w.shape[0] % (batch * bf16_rows) == 0 for w in weights)
    w_specs = [pl.BlockSpec((w.shape[0] // batch, w.shape[1]), lambda b: (b, 0)) for w in weights]
    s_specs = [pl.BlockSpec((w.shape[0] // batch, 1), lambda b: (b, 0)) for w in weights]
    scales = [sc.reshape(-1, 1).astype(F32) for sc in row_scales]
    return pl.pallas_call(
        kern,
        grid=(batch,),
        in_specs=[tok(dm), pl.BlockSpec((dm, seq), lambda b: (0, b)), tok(dm), tok(dm),
                  _const_spec((SUBLANES, T))] + w_specs + s_specs,
        out_specs=[tok(dm)] + w_specs,
        out_shape=[jax.ShapeDtypeStruct((T, dm), BF16)] + [jax.ShapeDtypeStruct(w.shape, BF16) for w in weights],
        scratch_shapes=([rows, rows, rows, rows] + [pltpu.VMEM((dh, 2 * dh), F32)] * N_MLSTM_HEADS
                        + [pltpu.VMEM((MLSTM_CHUNK, 2 * dh), F32)] * N_MLSTM_HEADS),
        compiler_params=pltpu.CompilerParams(dimension_semantics=("arbitrary",),
                                             vmem_limit_bytes=VMEM_LIMIT),
        name="mlstm",
    )(q, kt, v, o, gatest, *weights, *scales)


def _s5_prep_kernel(arow_ref, acol_ref, bre_ref, bim_ref, cre_ref, cim_ref, drow_ref,
                    rep_ref, rep2_ref, selrev_ref, pwcol_ref, w1_ref, wout_ref, al_ref, *, chunk):
    L = chunk
    P = S5_STATE
    Hc = S5_GROUP
    R = L * Hc
    hi = lax.Precision.HIGHEST

    def select(x, sel_ref):
        x_hi = x.astype(BF16)
        x_lo = (x - x_hi.astype(F32)).astype(BF16)
        sel = sel_ref[...]
        return (jnp.dot(x_hi, sel, preferred_element_type=F32) + jnp.dot(x_lo, sel, preferred_element_type=F32))

    dt_r = jnp.exp(arow_ref[2:3, :])
    da_re, da_im = dt_r * arow_ref[0:1, :], dt_r * arow_ref[1:2, :]

    def abar_pow(tau):
        mag = jnp.exp(tau * da_re)
        ang = tau * da_im
        return mag * jnp.cos(ang), mag * jnp.sin(ang)

    n_pow = -(-(L + 1) // SUBLANES) * SUBLANES
    er_re, er_im = abar_pow(lax.broadcasted_iota(jnp.int32, (n_pow, 1), 0).astype(F32))
    pad = jnp.zeros((2 * P - n_pow, 2 * P), F32)
    ec_re = jnp.concatenate([er_re, pad], axis=0).T
    ec_im = jnp.concatenate([er_im, pad], axis=0).T

    a_re_c, a_im_c = acol_ref[:, 0:1], acol_ref[:, 1:2]
    ab_re, ab_im = ec_re[:, 1:2], ec_im[:, 1:2]
    den = a_re_c * a_re_c + a_im_c * a_im_c
    nr = ab_re - 1.0
    z_re = (nr * a_re_c + ab_im * a_im_c) / den
    z_im = (ab_im * a_re_c - nr * a_im_c) / den
    bb_re = z_re * bre_ref[...] - z_im * bim_ref[...]
    bb_im = z_re * bim_ref[...] + z_im * bre_ref[...]
    first = lax.broadcasted_iota(jnp.int32, (2 * P, Hc), 0) < P
    bd = lambda b: jnp.concatenate([jnp.where(first, b, 0.0), jnp.where(first, 0.0, b)], axis=1)

    def rows(e, lo):
        return jnp.concatenate([jnp.broadcast_to(e[lo + t:lo + t + 1, :], (Hc, 2 * P)) for t in range(L)], axis=0)

    c_re = jnp.concatenate([cre_ref[...]] * L, axis=0)
    c_im = jnp.concatenate([cim_ref[...]] * L, axis=0)

    def c_times_pow(lo):
        e_re, e_im = rows(er_re, lo), rows(er_im, lo)
        return c_re * e_re - c_im * e_im, c_re * e_im + c_im * e_re

    ce_re, ce_im = c_times_pow(0)
    kflat = (jnp.dot(ce_re, bd(bb_re), preferred_element_type=F32, precision=hi)
             - jnp.dot(ce_im, bd(bb_im), preferred_element_type=F32, precision=hi))
    row = lax.broadcasted_iota(jnp.int32, (R, 2 * Hc), 0)
    col = lax.broadcasted_iota(jnp.int32, (R, 2 * Hc), 1) & (Hc - 1)
    kflat = kflat + jnp.where(row == col, drow_ref[...], 0.0)

    m = select(kflat, rep2_ref)
    blk = (lax.broadcasted_iota(jnp.int32, (R, 2 * R), 1) & (R - 1)) // Hc
    sh = 1
    while sh < L:
        shifted = jnp.concatenate([jnp.zeros((sh * Hc, 2 * R), F32), m[0:R - sh * Hc, :]], axis=0)
        m = jnp.where((blk & sh) != 0, shifted, m)
        sh *= 2

    ev_re, ev_im = select(ec_re, selrev_ref), select(ec_im, selrev_ref)
    bt_re, bt_im = select(bb_re, rep_ref), select(bb_im, rep_ref)
    st_re = (ev_re * bt_re - ev_im * bt_im).astype(BF16)
    st_im = (ev_re * bt_im + ev_im * bt_re).astype(BF16)

    co_re, co_im = c_times_pow(1)
    lane_first = lax.broadcasted_iota(jnp.int32, (R, 2 * P), 1) < P
    for g in range(2):
        w1_ref[g, 0:R, :] = m[:, g * R:(g + 1) * R].astype(BF16)
        w1_ref[g, R:R + P, :] = st_re[g * P:(g + 1) * P, :]
        w1_ref[g, R + P:R + 2 * P, :] = st_im[g * P:(g + 1) * P, :]
        own = lane_first if g == 0 else jnp.logical_not(lane_first)
        wout_ref[g, :, 0:2 * P] = jnp.where(own, co_re, 0.0).astype(BF16)
        wout_ref[g, :, 2 * P:4 * P] = jnp.where(own, -co_im, 0.0).astype(BF16)

    sc_re, sc_im = abar_pow(pwcol_ref[...])
    al_ref[0:SUBLANES, :] = sc_re
    al_ref[SUBLANES:2 * SUBLANES, :] = sc_im


def _s5_prep(a_re, a_im, log_dt, b_re, b_im, c_re, c_im, d_skip, *, chunk, n_chunks):
    G, P = a_re.shape
    Hc = S5_GROUP
    R = chunk * Hc
    assert G % 2 == 0 and 2 * P == LANES
    G2 = G // 2
    ldt = jnp.broadcast_to(log_dt[:, None], (G, P))
    arow = jnp.stack([a_re.reshape(G2, 2 * P), a_im.reshape(G2, 2 * P), ldt.reshape(G2, 2 * P)], axis=1)
    acol = jnp.stack([a_re.reshape(G2, 2 * P), a_im.reshape(G2, 2 * P)], axis=2)
    pair_lanes = lambda c: c.reshape(G2, 2, Hc, P).transpose(0, 2, 1, 3).reshape(G2, Hc, 2 * P)
    drow = d_skip.reshape(G2, 1, 2 * Hc)
    rep = (jnp.arange(R)[None, :] % Hc == jnp.arange(Hc)[:, None]).astype(BF16)
    rep2 = jnp.kron(jnp.eye(2, dtype=BF16), rep)
    selrev = (jnp.arange(LANES)[:, None] == chunk - 1 - jnp.arange(R)[None, :] // Hc).astype(BF16)
    n_steps = max(1, (n_chunks - 1).bit_length())
    assert chunk + 1 <= LANES and n_steps <= SUBLANES
    pwcol = jnp.zeros((SUBLANES, 1), F32).at[:n_steps, 0].set(chunk * 2.0 ** jnp.arange(n_steps))
    kern = functools.partial(_s5_prep_kernel, chunk=chunk)
    pair = lambda *s: pl.BlockSpec((None,) + s, lambda g: (g,) + (0,) * len(s))
    two = lambda *s: pl.BlockSpec((2,) + s, lambda g: (g,) + (0,) * len(s))
    return pl.pallas_call(
        kern,
        grid=(G2,),
        in_specs=[pair(3, 2 * P), pair(2 * P, 2), pair(2 * P, Hc), pair(2 * P, Hc), pair(Hc, 2 * P), pair(Hc, 2 * P),
                  pair(1, 2 * Hc), _const_spec((Hc, R)), _const_spec((2 * Hc, 2 * R)), _const_spec((LANES, R)),
                  _const_spec((SUBLANES, 1))],
        out_specs=[two(R + 2 * P, R), two(R, 4 * P), pair(2 * SUBLANES, 2 * P)],
        out_shape=[jax.ShapeDtypeStruct((G, R + 2 * P, R), BF16),
                   jax.ShapeDtypeStruct((G, R, 4 * P), BF16),
                   jax.ShapeDtypeStruct((G2, 2 * SUBLANES, 2 * P), F32)],
        compiler_params=pltpu.CompilerParams(dimension_semantics=("arbitrary",),
                                             vmem_limit_bytes=VMEM_LIMIT),
        name="s5_prep",
    )(arow, acol, b_re.reshape(G2, 2 * P, Hc), b_im.reshape(G2, 2 * P, Hc), pair_lanes(c_re), pair_lanes(c_im),
      drow, rep, rep2, selrev, pwcol)


def _s5_kernel(*refs, chunk, n_chunks):
    L = chunk
    n_oct = L // SUBLANES
    u_refs, (w1_ref, wout_ref, al_ref, y_ref, ut_ref, yt_ref) = refs[0:n_oct], refs[n_oct:]
    P = S5_STATE
    Hc = S5_GROUP
    R = L * Hc
    C = y_ref.shape[0] // L
    n_grp = LANES // Hc

    for s in range(L):
        xt = u_refs[s // SUBLANES][pl.ds(s % SUBLANES, C, stride=SUBLANES), :].astype(BF16).T
        for j in range(n_grp):
            ut_ref[j, s * Hc:(s + 1) * Hc, :] = xt[j * Hc:(j + 1) * Hc, :]

    seg_pos = lax.broadcasted_iota(jnp.int32, (C, 2 * P), 0) & (n_chunks - 1)
    nt = (((1,), (1,)), ((), ()))

    def shifted(x, sh):
        return jnp.where(seg_pos >= sh, pltpu.roll(x, sh, 0), 0.0)

    def group_pair(i, carry):
        j0 = 2 * i
        res = [jnp.dot(w1_ref[j0 + d], ut_ref[j0 + d], preferred_element_type=F32) for d in range(2)]
        x_re = jnp.concatenate([res[0][R:R + P], res[1][R:R + P]], axis=0).T
        x_im = jnp.concatenate([res[0][R + P:R + 2 * P], res[1][R + P:R + 2 * P]], axis=0).T
        sh, k = 1, 0
        while sh < n_chunks:
            a_re = al_ref[i, k:k + 1, :]
            a_im = al_ref[i, SUBLANES + k:SUBLANES + k + 1, :]
            s_re, s_im = shifted(x_re, sh), shifted(x_im, sh)
            x_re, x_im = (x_re + a_re * s_re - a_im * s_im, x_im + a_re * s_im + a_im * s_re)
            sh *= 2
            k += 1
        xprev = jnp.concatenate([shifted(x_re, 1), shifted(x_im, 1)], axis=1).astype(BF16)
        for d in range(2):
            yt_ref[j0 + d] = res[d][0:R, :] + lax.dot_general(wout_ref[j0 + d], xprev, nt,
                                                              preferred_element_type=F32)
        return carry

    lax.fori_loop(0, n_grp // 2, group_pair, 0, unroll=2)

    for t in range(L):
        zt = jnp.concatenate([yt_ref[j, t * Hc:(t + 1) * Hc, :] for j in range(n_grp)], axis=0)
        y_ref[pl.ds(t, C, stride=L), :] = zt.T


def _s5(u, w1, wout, al, *, chunk, n_chunks, n_col_blocks):
    n_oct, t_oct, ds5 = u.shape
    T = n_oct * t_oct
    assert n_oct * SUBLANES == chunk
    G, R1, R = w1.shape
    P4 = wout.shape[-1]
    n_grp = LANES // S5_GROUP
    n_q = ds5 // LANES
    tb = T // n_col_blocks
    C = tb // chunk
    kern = functools.partial(_s5_kernel, chunk=chunk, n_chunks=n_chunks)
    grp = lambda *s: pl.BlockSpec((n_grp,) + s, lambda q, i: (q,) + (0,) * len(s))
    octets = [pl.BlockSpec((None, C * SUBLANES, LANES), functools.partial(lambda q, i, o: (o, i, q), o=o))
              for o in range(n_oct)]
    return pl.pallas_call(
        kern,
        grid=(n_q, n_col_blocks),
        in_specs=octets + [grp(R1, R), grp(R, P4),
                           pl.BlockSpec((n_grp // 2, 2 * SUBLANES, P4 // 2), lambda q, i: (q, 0, 0))],
        out_specs=pl.BlockSpec((tb, LANES), lambda q, i: (i, q)),
        out_shape=jax.ShapeDtypeStruct((T, ds5), F32),
        scratch_shapes=[pltpu.VMEM((n_grp, R, C), BF16), pltpu.VMEM((n_grp, R, C), F32)],
        compiler_params=pltpu.CompilerParams(dimension_semantics=("arbitrary", "arbitrary"),
                                             vmem_limit_bytes=BIG_VMEM_LIMIT),
        name="s5",
    )(*([u] * n_oct), w1, wout, al)


def _out_kernel(x_ref, hm_ref, y_ref, wglu_ref, bglu_ref, wo_ref,
                wgate_ref, wup_ref, wdown_ref, gf_ref, out_ref, acc_ref, xn_ref, *, ff_chunk, final_norm):
    dm = hm_ref.shape[1]
    g = jax.nn.gelu(y_ref[...])
    z = jnp.dot(g.astype(BF16), wglu_ref[...], preferred_element_type=F32) + bglu_ref[...]
    gg = g * _sigmoid(z)
    hs = (gg * lax.rsqrt(jnp.mean(gg * gg, axis=-1, keepdims=True) + EPS)).astype(BF16)
    h1 = (x_ref[...] + jnp.dot(hm_ref[...], wo_ref[0:dm, :], preferred_element_type=F32)
          + jnp.dot(hs, wo_ref[dm:, :], preferred_element_type=F32))
    xn_ref[...] = (h1 * lax.rsqrt(jnp.mean(h1 * h1, axis=-1, keepdims=True) + EPS)).astype(BF16)
    acc_ref[...] = h1

    def ffn_chunk(c, carry):
        cs = pl.ds(pl.multiple_of(c * ff_chunk, ff_chunk), ff_chunk)
        xn2 = xn_ref[...]
        gate = jnp.dot(xn2, wgate_ref[:, cs], preferred_element_type=F32)
        up = jnp.dot(xn2, wup_ref[:, cs], preferred_element_type=F32)
        act = (_silu(gate) * up).astype(BF16)
        acc_ref[...] += jnp.dot(act, wdown_ref[cs, :], preferred_element_type=F32)
        return carry

    lax.fori_loop(0, wgate_ref.shape[1] // ff_chunk, ffn_chunk, 0, unroll=4)
    out_ref[...] = _rmsnorm(acc_ref[...], gf_ref[...]) if final_norm else acc_ref[...]


def _out_block(x2, hm, y, wglu, bglu, wo, wgate, wup, wdown, gf, *, tm, ff_chunk, final_norm):
    T, D = x2.shape
    dm = hm.shape[1]
    ds5 = y.shape[1]
    assert wgate.shape[1] % ff_chunk == 0
    kern = functools.partial(_out_kernel, ff_chunk=ff_chunk, final_norm=final_norm)
    tok = lambda w: pl.BlockSpec((tm, w), lambda i: (i, 0))
    return pl.pallas_call(
        kern,
        grid=(T // tm,),
        in_specs=[tok(D), tok(dm), tok(ds5), _const_spec(wglu.shape), _const_spec((1, ds5)), _const_spec(wo.shape),
                  _const_spec(wgate.shape), _const_spec(wup.shape), _const_spec(wdown.shape),
                  _const_spec((1, D))],
        out_specs=tok(D),
        out_shape=jax.ShapeDtypeStruct((T, D), F32),
        scratch_shapes=[pltpu.VMEM((tm, D), F32), pltpu.VMEM((tm, D), BF16)],
        compiler_params=pltpu.CompilerParams(dimension_semantics=("arbitrary",),
                                             vmem_limit_bytes=BIG_VMEM_LIMIT),
        name="out_block",
    )(x2, hm, y, wglu, bglu, wo, wgate, wup, wdown, gf)


def kernel(x, norm1_g, w_in, if_bias, conv_qk, mlstm_norm_g, a_re, a_im, log_dt, b_re, b_im, c_re, c_im,
           d_skip, w_glu, b_glu, s5_norm_g, w_out, norm2_g, w_gate, w_up, w_down, norm_f_g):
    B, S, D = x.shape
    depth = w_in.shape[0]
    H = N_MLSTM_HEADS
    dm = mlstm_norm_g.shape[1]
    T = B * S
    L5 = S5_CHUNK
    nc5 = S // L5
    row = lambda a: a.reshape(1, -1).astype(F32)

    h = x.reshape(T, D)
    for l in range(depth):
        last = l == depth - 1
        gb = jnp.zeros((1, LANES), F32).at[0, 0:2 * H].set(if_bias[l])

        q, kt, v, o, gatest, u = _inproj(h, row(norm1_g[l]), w_in[l].T.astype(F32), gb, conv_qk[l].astype(F32),
                                         seq=S, tm=min(INPROJ_TM, S), d_mlstm=dm, n_oct=L5 // SUBLANES)
        ones = lambda n: jnp.ones((n,), F32)
        g_mix = jnp.concatenate([mlstm_norm_g[l], s5_norm_g[l]]).astype(F32)
        g_ffn = norm2_g[l].astype(F32)
        hm, wglu, wo, wgate, wup, wdown = _mlstm(
            q, kt, v, o, gatest,
            [w.astype(F32) for w in (w_glu[l], w_out[l], w_gate[l], w_up[l], w_down[l])],
            [ones(w_glu.shape[1]), g_mix, g_ffn, g_ffn, ones(w_down.shape[1])], batch=B, seq=S)

        w1, wout, al = _s5_prep(a_re[l], a_im[l], log_dt[l], b_re[l], b_im[l], c_re[l], c_im[l],
                                d_skip[l], chunk=L5, n_chunks=nc5)
        y = _s5(u, w1, wout, al, chunk=L5, n_chunks=nc5, n_col_blocks=S5_COL_BLOCKS)

        h = _out_block(h, hm, y, wglu, row(b_glu[l]), wo, wgate, wup, wdown, row(norm_f_g),
                       tm=OUT_TM, ff_chunk=FF_CHUNK, final_norm=last)
    return h.reshape(B, S, D)
```

```python
import functools

import jax
import jax.numpy as jnp
from jax import lax
from jax.experimental import pallas as pl
from jax.experimental.pallas import tpu as pltpu

EPS = 1e-6
N_MLSTM_HEADS = 4
CONV_WIDTH = 4
S5_GROUP = 16
S5_STATE = 64

LANES = 128
SUBLANES = 8
MLSTM_CHUNK = 128
S5_CHUNK = 16
S5_COL_BLOCKS = 1
INPROJ_TM = 1024
OUT_TM = 1024
FF_CHUNK = 256
VMEM_LIMIT = 48 * 1024 * 1024
BIG_VMEM_LIMIT = 58 * 1024 * 1024

F32 = jnp.float32
BF16 = jnp.bfloat16


def _rmsnorm(x, g):
    ms = jnp.mean(x * x, axis=-1, keepdims=True)
    return x * lax.rsqrt(ms + EPS) * g


def _sigmoid(x):
    return 0.5 * jnp.tanh(0.5 * x) + 0.5


def _silu(x):
    h = 0.5 * x
    return h + h * jnp.tanh(h)


def _log_sigmoid(x):
    return -(jnp.maximum(-x, 0.0) + jnp.log1p(jnp.exp(-jnp.abs(x))))


def _const_spec(shape):
    return pl.BlockSpec(shape, lambda *_: (0,) * len(shape), pipeline_mode=pl.Buffered(1))


def _inproj_kernel(x_ref, g1_ref, wt_ref, gb_ref, cw_ref,
                   q_ref, kt_ref, v_ref, o_ref, gatest_ref, u_ref,
                   ext_ref, wm_ref, wg_ref, wu_ref, *, tiles_per_seq, d_mlstm, k_scale):
    tm = x_ref.shape[0]
    dm = d_mlstm
    H = N_MLSTM_HEADS
    halo = SUBLANES

    @pl.when(pl.program_id(0) == 0)
    def _():
        g1 = g1_ref[...]
        for c0 in range(0, 4 * dm, dm):
            wm_ref[:, c0:c0 + dm] = (wt_ref[c0:c0 + dm, :] * g1).T.astype(BF16)
        wu_ref[...] = (wt_ref[4 * dm + 2 * H:, :] * g1).T.astype(BF16)
        fill = jnp.zeros((LANES - 2 * H, wt_ref.shape[1]), F32)
        wg_ref[...] = jnp.concatenate([wt_ref[4 * dm:4 * dm + 2 * H, :] * g1, fill], axis=0).T.astype(BF16)

    x = x_ref[...]
    xn = (x * lax.rsqrt(jnp.mean(x * x, axis=-1, keepdims=True) + EPS)).astype(BF16)

    @pl.when(pl.program_id(0) % tiles_per_seq == 0)
    def _():
        ext_ref[0:halo, :] = jnp.zeros((halo, 2 * dm), F32)

    ext_ref[halo:halo + tm, :] = jnp.dot(xn, wm_ref[:, 0:2 * dm], preferred_element_type=F32)
    v = jnp.dot(xn, wm_ref[:, 2 * dm:3 * dm], preferred_element_type=F32)
    o = jnp.dot(xn, wm_ref[:, 3 * dm:4 * dm], preferred_element_type=F32)
    u = jnp.dot(xn, wu_ref[...], preferred_element_type=F32)
    gates = jnp.dot(xn, wg_ref[...], preferred_element_type=F32) + gb_ref[...]
    v_ref[...] = v.astype(BF16)
    o_ref[...] = o.astype(BF16)
    n_oct = u_ref.shape[0]
    u4 = u.reshape(tm // (n_oct * SUBLANES), n_oct, SUBLANES, u.shape[1])
    for oc in range(n_oct):
        u_ref[oc] = u4[:, oc].reshape(tm // n_oct, u.shape[1])
    gatest_ref[...] = gates.T[0:2 * H, :]

    def zero_of(t):
        w = pltpu.bitcast(t[tm - SUBLANES:tm, t.shape[1] - LANES:], jnp.uint32)
        return pltpu.bitcast((w >> 16) >> 16, F32)
    anchor = zero_of(v)

    acc = cw_ref[CONV_WIDTH - 1:CONV_WIDTH, :] * ext_ref[halo:halo + tm, :]
    for j in range(1, CONV_WIDTH):
        acc = acc + cw_ref[CONV_WIDTH - 1 - j:CONV_WIDTH - j, :] * ext_ref[halo - j:halo - j + tm, :]
    ext_ref[0:halo, :] = ext_ref[tm:tm + halo, :]
    qk = _silu(acc) + jnp.tile(anchor, (tm // SUBLANES, 2 * dm // LANES))
    q_ref[...] = qk[:, 0:dm].astype(BF16)
    kt_ref[...] = (qk[:, dm:2 * dm] * k_scale).T.astype(BF16)


def _inproj(x2, g1, w_in_t, gb, cw, *, seq, tm, d_mlstm, n_oct):
    T, D = x2.shape
    dm = d_mlstm
    H = N_MLSTM_HEADS
    assert 2 * H == SUBLANES
    ds5 = w_in_t.shape[0] - 4 * dm - 2 * H
    kern = functools.partial(_inproj_kernel, tiles_per_seq=seq // tm, d_mlstm=dm,
                             k_scale=float((dm // H) ** -0.5))
    tok = lambda w: pl.BlockSpec((tm, w), lambda i: (i, 0))
    return pl.pallas_call(
        kern,
        grid=(T // tm,),
        in_specs=[tok(D), _const_spec((1, D)), _const_spec(w_in_t.shape), _const_spec((1, LANES)),
                  _const_spec(cw.shape)],
        out_specs=[tok(dm), pl.BlockSpec((dm, tm), lambda i: (0, i)), tok(dm), tok(dm),
                   pl.BlockSpec((2 * H, tm), lambda i: (0, i)),
                   pl.BlockSpec((n_oct, tm // n_oct, ds5), lambda i: (0, i, 0))],
        out_shape=[jax.ShapeDtypeStruct((T, dm), BF16), jax.ShapeDtypeStruct((dm, T), BF16),
                   jax.ShapeDtypeStruct((T, dm), BF16), jax.ShapeDtypeStruct((T, dm), BF16),
                   jax.ShapeDtypeStruct((2 * H, T), F32),
                   jax.ShapeDtypeStruct((n_oct, T // n_oct, ds5), F32)],
        scratch_shapes=[pltpu.VMEM((tm + 2 * SUBLANES, 2 * dm), F32), pltpu.VMEM((D, 4 * dm), BF16),
                        pltpu.VMEM((D, LANES), BF16), pltpu.VMEM((D, ds5), BF16)],
        compiler_params=pltpu.CompilerParams(dimension_semantics=("arbitrary",),
                                             vmem_limit_bytes=VMEM_LIMIT),
        name="inproj",
    )(x2, g1, w_in_t, gb, cw)


def _lane_scan(x, op, identity, seg):
    pos = lax.broadcasted_iota(jnp.int32, x.shape, x.ndim - 1) & (seg - 1)
    sh = 1
    while sh < seg:
        x = op(x, jnp.where(pos >= sh, pltpu.roll(x, sh, x.ndim - 1), identity))
        sh *= 2
    return x


def _mlstm_kernel(q_ref, kt_ref, v_ref, o_ref, gatest_ref, *refs, chunk, n_cast):
    S = q_ref.shape[0]
    H = N_MLSTM_HEADS
    dh = q_ref.shape[1] // H
    L = chunk
    assert L == dh == LANES
    w32_refs, ws_refs = refs[0:n_cast], refs[n_cast:2 * n_cast]
    out_ref, w16_refs = refs[2 * n_cast], refs[2 * n_cast + 1:3 * n_cast + 1]
    a_ref, g0_ref, nm_ref, gend_ref = refs[3 * n_cast + 1:3 * n_cast + 5]
    c_refs, nd_refs = refs[3 * n_cast + 5:3 * n_cast + 5 + H], refs[3 * n_cast + 5 + H:]

    for w32_ref, ws_ref, w16_ref in zip(w32_refs, ws_refs, w16_refs):
        w16_ref[...] = (w32_ref[...] * ws_ref[...]).astype(BF16)

    @pl.when(pl.program_id(0) == 0)
    def _():
        gates = gatest_ref[...]
        f_cum = pltpu.roll(_lane_scan(_log_sigmoid(gates), jnp.add, 0.0, S), H, 0)
        a = gates - f_cum
        g0 = jnp.maximum(_lane_scan(a, jnp.maximum, -jnp.inf, S), 0.0)
        a_ref[...] = a
        g0_ref[...] = g0
        nm_ref[...] = -(f_cum + g0)
        for c in range(gatest_ref.shape[1] // L):
            gend_ref[:, c * L:(c + 1) * L] = jnp.broadcast_to(g0[:, (c + 1) * L - 1:(c + 1) * L], (SUBLANES, L))

    t0 = pl.program_id(0) * S

    for c_ref in c_refs:
        c_ref[...] = jnp.zeros(c_ref.shape, F32)
    causal = (lax.broadcasted_iota(jnp.int32, (L, L), 0) >= lax.broadcasted_iota(jnp.int32, (L, L), 1))
    ones_blk = jnp.ones((L, dh), BF16)
    mean_mat = jnp.full((dh, dh), 1.0 / dh, BF16)

    def mix(c, h, g_prev):
        r0 = pl.multiple_of(c * L, L)
        hs = slice(h * dh, (h + 1) * dh)
        qc = q_ref[pl.ds(r0, L), hs]
        ktc = kt_ref[hs, pl.ds(r0, L)]
        v_aug = jnp.concatenate([v_ref[pl.ds(r0, L), hs], ones_blk], axis=1)
        g0l = pl.ds(pl.multiple_of(t0 + r0, L), L)
        a_row = a_ref[h:h + 1, g0l]
        g_end = gend_ref[h:h + 1, g0l]
        g0_t = jnp.broadcast_to(g0_ref[h:h + 1, g0l], (L, L)).T

        s_qk = jnp.dot(qc, ktc, preferred_element_type=F32)
        p = (jnp.exp(jnp.where(causal, a_row - g0_t, -jnp.inf)) * s_qk).astype(BF16)
        c_prev = c_refs[h][...]
        q_w = (qc.astype(F32) * jnp.exp(g_prev - g0_t)).astype(BF16)
        nd_refs[h][...] = jnp.dot(jnp.concatenate([q_w, p], axis=1),
                                  jnp.concatenate([c_prev.astype(BF16), v_aug], axis=0),
                                  preferred_element_type=F32)

        kte = (ktc.astype(F32) * jnp.exp(a_row - g_end)).astype(BF16)
        decay = jnp.exp(g_prev - g_end)
        c_refs[h][...] = (jnp.concatenate([decay, decay], axis=1) * c_prev
                          + jnp.dot(kte, v_aug, preferred_element_type=F32))
        return g_end

    def emit(c, h):
        r0 = pl.multiple_of(c * L, L)
        hs = slice(h * dh, (h + 1) * dh)
        nm_t = jnp.broadcast_to(nm_ref[h:h + 1, pl.ds(pl.multiple_of(t0 + r0, L), L)], (L, L)).T
        nd = nd_refs[h][...]
        h_tilde = nd[:, 0:dh] / jnp.maximum(jnp.abs(nd[:, dh:2 * dh]), jnp.exp(nm_t))
        hm = _sigmoid(o_ref[pl.ds(r0, L), hs].astype(F32)) * h_tilde
        mu = jnp.dot(hm.astype(BF16), mean_mat, preferred_element_type=F32)
        hc = hm - mu
        var = jnp.dot((hc * hc).astype(BF16), mean_mat, preferred_element_type=F32)
        out_ref[pl.ds(r0, L), hs] = (hc * lax.rsqrt(var + EPS)).astype(BF16)

    zero = jnp.zeros((1, L), F32)
    g_first = tuple(mix(0, h, zero) for h in range(H))

    def body(c, g_carry):
        for h in range(H):
            emit(c - 1, h)
        return tuple(mix(c, h, g_carry[h]) for h in range(H))

    lax.fori_loop(1, S // L, body, g_first, unroll=5)
    for h in range(H):
        emit(S // L - 1, h)


def _mlstm(q, kt, v, o, gatest, weights, row_scales, *, batch, seq):
    T, dm = q.shape
    dh = dm // N_MLSTM_HEADS
    kern = functools.partial(_mlstm_kernel, chunk=MLSTM_CHUNK, n_cast=len(weights))
    tok = lambda w: pl.BlockSpec((seq, w), lambda b: (b, 0))
    rows = pltpu.VMEM((SUBLANES, T), F32)
    bf16_rows = 2 * SUBLANES
    assert all(w.shape[0] % (batch * bf16_rows) == 0 for w in weights)
    w_specs = [pl.BlockSpec((w.shape[0] // batch, w.shape[1]), lambda b: (b, 0)) for w in weights]
    s_specs = [pl.BlockSpec((w.shape[0] // batch, 1), lambda b: (b, 0)) for w in weights]
    scales = [sc.reshape(-1, 1).astype(F32) for sc in row_scales]
    return pl.pallas_call(
        kern,
        grid=(batch,),
        in_specs=[tok(dm), pl.BlockSpec((dm, seq), lambda b: (0, b)), tok(dm), tok(dm),
                  _const_spec((SUBLANES, T))] + w_specs + s_specs,
        out_specs=[tok(dm)] + w_specs,
        out_shape=[jax.ShapeDtypeStruct((T, dm), BF16)] + [jax.ShapeDtypeStruct(w.shape, BF16) for w in weights],
        scratch_shapes=([rows, rows, rows, rows] + [pltpu.VMEM((dh, 2 * dh), F32)] * N_MLSTM_HEADS
                        + [pltpu.VMEM((MLSTM_CHUNK, 2 * dh), F32)] * N_MLSTM_HEADS),
        compiler_params=pltpu.CompilerParams(dimension_semantics=("arbitrary",),
                                             vmem_limit_bytes=VMEM_LIMIT),
        name="mlstm",
    )(q, kt, v, o, gatest, *weights, *scales)


def _s5_prep_kernel(arow_ref, acol_ref, bre_ref, bim_ref, cre_ref, cim_ref, drow_ref,
                    rep_ref, rep2_ref, selrev_ref, pwcol_ref, w1_ref, wout_ref, al_ref, *, chunk):
    L = chunk
    P = S5_STATE
    Hc = S5_GROUP
    R = L * Hc
    hi = lax.Precision.HIGHEST

    def select(x, sel_ref):
        x_hi = x.astype(BF16)
        x_lo = (x - x_hi.astype(F32)).astype(BF16)
        sel = sel_ref[...]
        return (jnp.dot(x_hi, sel, preferred_element_type=F32) + jnp.dot(x_lo, sel, preferred_element_type=F32))

    dt_r = jnp.exp(arow_ref[2:3, :])
    da_re, da_im = dt_r * arow_ref[0:1, :], dt_r * arow_ref[1:2, :]

    def abar_pow(tau):
        mag = jnp.exp(tau * da_re)
        ang = tau * da_im
        return mag * jnp.cos(ang), mag * jnp.sin(ang)

    n_pow = -(-(L + 1) // SUBLANES) * SUBLANES
    er_re, er_im = abar_pow(lax.broadcasted_iota(jnp.int32, (n_pow, 1), 0).astype(F32))
    pad = jnp.zeros((2 * P - n_pow, 2 * P), F32)
    ec_re = jnp.concatenate([er_re, pad], axis=0).T
    ec_im = jnp.concatenate([er_im, pad], axis=0).T

    a_re_c, a_im_c = acol_ref[:, 0:1], acol_ref[:, 1:2]
    ab_re, ab_im = ec_re[:, 1:2], ec_im[:, 1:2]
    den = a_re_c * a_re_c + a_im_c * a_im_c
    nr = ab_re - 1.0
    z_re = (nr * a_re_c + ab_im * a_im_c) / den
    z_im = (ab_im * a_re_c - nr * a_im_c) / den
    bb_re = z_re * bre_ref[...] - z_im * bim_ref[...]
    bb_im = z_re * bim_ref[...] + z_im * bre_ref[...]
    first = lax.broadcasted_iota(jnp.int32, (2 * P, Hc), 0) < P
    bd = lambda b: jnp.concatenate([jnp.where(first, b, 0.0), jnp.where(first, 0.0, b)], axis=1)

    def rows(e, lo):
        return jnp.concatenate([jnp.broadcast_to(e[lo + t:lo + t + 1, :], (Hc, 2 * P)) for t in range(L)], axis=0)

    c_re = jnp.concatenate([cre_ref[...]] * L, axis=0)
    c_im = jnp.concatenate([cim_ref[...]] * L, axis=0)

    def c_times_pow(lo):
        e_re, e_im = rows(er_re, lo), rows(er_im, lo)
        return c_re * e_re - c_im * e_im, c_re * e_im + c_im * e_re

    ce_re, ce_im = c_times_pow(0)
    kflat = (jnp.dot(ce_re, bd(bb_re), preferred_element_type=F32, precision=hi)
             - jnp.dot(ce_im, bd(bb_im), preferred_element_type=F32, precision=hi))
    row = lax.broadcasted_iota(jnp.int32, (R, 2 * Hc), 0)
    col = lax.broadcasted_iota(jnp.int32, (R, 2 * Hc), 1) & (Hc - 1)
    kflat = kflat + jnp.where(row == col, drow_ref[...], 0.0)

    m = select(kflat, rep2_ref)
    blk = (lax.broadcasted_iota(jnp.int32, (R, 2 * R), 1) & (R - 1)) // Hc
    sh = 1
    while sh < L:
        shifted = jnp.concatenate([jnp.zeros((sh * Hc, 2 * R), F32), m[0:R - sh * Hc, :]], axis=0)
        m = jnp.where((blk & sh) != 0, shifted, m)
        sh *= 2

    ev_re, ev_im = select(ec_re, selrev_ref), select(ec_im, selrev_ref)
    bt_re, bt_im = select(bb_re, rep_ref), select(bb_im, rep_ref)
    st_re = (ev_re * bt_re - ev_im * bt_im).astype(BF16)
    st_im = (ev_re * bt_im + ev_im * bt_re).astype(BF16)

    co_re, co_im = c_times_pow(1)
    lane_first = lax.broadcasted_iota(jnp.int32, (R, 2 * P), 1) < P
    for g in range(2):
        w1_ref[g, 0:R, :] = m[:, g * R:(g + 1) * R].astype(BF16)
        w1_ref[g, R:R + P, :] = st_re[g * P:(g + 1) * P, :]
        w1_ref[g, R + P:R + 2 * P, :] = st_im[g * P:(g + 1) * P, :]
        own = lane_first if g == 0 else jnp.logical_not(lane_first)
        wout_ref[g, :, 0:2 * P] = jnp.where(own, co_re, 0.0).astype(BF16)
        wout_ref[g, :, 2 * P:4 * P] = jnp.where(own, -co_im, 0.0).astype(BF16)

    sc_re, sc_im = abar_pow(pwcol_ref[...])
    al_ref[0:SUBLANES, :] = sc_re
    al_ref[SUBLANES:2 * SUBLANES, :] = sc_im


def _s5_prep(a_re, a_im, log_dt, b_re, b_im, c_re, c_im, d_skip, *, chunk, n_chunks):
    G, P = a_re.shape
    Hc = S5_GROUP
    R = chunk * Hc
    assert G % 2 == 0 and 2 * P == LANES
    G2 = G // 2
    ldt = jnp.broadcast_to(log_dt[:, None], (G, P))
    arow = jnp.stack([a_re.reshape(G2, 2 * P), a_im.reshape(G2, 2 * P), ldt.reshape(G2, 2 * P)], axis=1)
    acol = jnp.stack([a_re.reshape(G2, 2 * P), a_im.reshape(G2, 2 * P)], axis=2)
    pair_lanes = lambda c: c.reshape(G2, 2, Hc, P).transpose(0, 2, 1, 3).reshape(G2, Hc, 2 * P)
    drow = d_skip.reshape(G2, 1, 2 * Hc)
    rep = (jnp.arange(R)[None, :] % Hc == jnp.arange(Hc)[:, None]).astype(BF16)
    rep2 = jnp.kron(jnp.eye(2, dtype=BF16), rep)
    selrev = (jnp.arange(LANES)[:, None] == chunk - 1 - jnp.arange(R)[None, :] // Hc).astype(BF16)
    n_steps = max(1, (n_chunks - 1).bit_length())
    assert chunk + 1 <= LANES and n_steps <= SUBLANES
    pwcol = jnp.zeros((SUBLANES, 1), F32).at[:n_steps, 0].set(chunk * 2.0 ** jnp.arange(n_steps))
    kern = functools.partial(_s5_prep_kernel, chunk=chunk)
    pair = lambda *s: pl.BlockSpec((None,) + s, lambda g: (g,) + (0,) * len(s))
    two = lambda *s: pl.BlockSpec((2,) + s, lambda g: (g,) + (0,) * len(s))
    return pl.pallas_call(
        kern,
        grid=(G2,),
        in_specs=[pair(3, 2 * P), pair(2 * P, 2), pair(2 * P, Hc), pair(2 * P, Hc), pair(Hc, 2 * P), pair(Hc, 2 * P),
                  pair(1, 2 * Hc), _const_spec((Hc, R)), _const_spec((2 * Hc, 2 * R)), _const_spec((LANES, R)),
                  _const_spec((SUBLANES, 1))],
        out_specs=[two(R + 2 * P, R), two(R, 4 * P), pair(2 * SUBLANES, 2 * P)],
        out_shape=[jax.ShapeDtypeStruct((G, R + 2 * P, R), BF16),
                   jax.ShapeDtypeStruct((G, R, 4 * P), BF16),
                   jax.ShapeDtypeStruct((G2, 2 * SUBLANES, 2 * P), F32)],
        compiler_params=pltpu.CompilerParams(dimension_semantics=("arbitrary",),
                                             vmem_limit_bytes=VMEM_LIMIT),
        name="s5_prep",
    )(arow, acol, b_re.reshape(G2, 2 * P, Hc), b_im.reshape(G2, 2 * P, Hc), pair_lanes(c_re), pair_lanes(c_im),
      drow, rep, rep2, selrev, pwcol)


def _s5_kernel(*refs, chunk, n_chunks):
    L = chunk
    n_oct = L // SUBLANES
    u_refs, (w1_ref, wout_ref, al_ref, y_ref, ut_ref, yt_ref) = refs[0:n_oct], refs[n_oct:]
    P = S5_STATE
    Hc = S5_GROUP
    R = L * Hc
    C = y_ref.shape[1] // SUBLANES
    n_grp = LANES // Hc

    for s in range(L):
        xt = u_refs[s // SUBLANES][pl.ds(s % SUBLANES, C, stride=SUBLANES), :].astype(BF16).T
        for j in range(n_grp):
            ut_ref[j, s * Hc:(s + 1) * Hc, :] = xt[j * Hc:(j + 1) * Hc, :]

    seg_pos = lax.broadcasted_iota(jnp.int32, (C, 2 * P), 0) & (n_chunks - 1)
    nt = (((1,), (1,)), ((), ()))

    def shifted(x, sh):
        return jnp.where(seg_pos >= sh, pltpu.roll(x, sh, 0), 0.0)

    def group_pair(i, carry):
        j0 = 2 * i
        res = [jnp.dot(w1_ref[j0 + d], ut_ref[j0 + d], preferred_element_type=F32) for d in range(2)]
        x_re = jnp.concatenate([res[0][R:R + P], res[1][R:R + P]], axis=0).T
        x_im = jnp.concatenate([res[0][R + P:R + 2 * P], res[1][R + P:R + 2 * P]], axis=0).T
        sh, k = 1, 0
        while sh < n_chunks:
            a_re = al_ref[i, k:k + 1, :]
            a_im = al_ref[i, SUBLANES + k:SUBLANES + k + 1, :]
            s_re, s_im = shifted(x_re, sh), shifted(x_im, sh)
            x_re, x_im = (x_re + a_re * s_re - a_im * s_im, x_im + a_re * s_im + a_im * s_re)
            sh *= 2
            k += 1
        xprev = jnp.concatenate([shifted(x_re, 1), shifted(x_im, 1)], axis=1).astype(BF16)
        for d in range(2):
            yt_ref[j0 + d] = res[d][0:R, :] + lax.dot_general(wout_ref[j0 + d], xprev, nt,
                                                              preferred_element_type=F32)
        return carry

    lax.fori_loop(0, n_grp // 2, group_pair, 0, unroll=2)

    for t in range(L):
        zt = jnp.concatenate([yt_ref[j, t * Hc:(t + 1) * Hc, :] for j in range(n_grp)], axis=0)
        y_ref[t // SUBLANES, pl.ds(t % SUBLANES, C, stride=SUBLANES), :] = zt.T


def _s5(u, w1, wout, al, *, chunk, n_chunks, n_col_blocks):
    n_oct, t_oct, ds5 = u.shape
    T = n_oct * t_oct
    assert n_oct * SUBLANES == chunk
    G, R1, R = w1.shape
    P4 = wout.shape[-1]
    n_grp = LANES // S5_GROUP
    n_q = ds5 // LANES
    tb = T // n_col_blocks
    C = tb // chunk
    kern = functools.partial(_s5_kernel, chunk=chunk, n_chunks=n_chunks)
    grp = lambda *s: pl.BlockSpec((n_grp,) + s, lambda q, i: (q,) + (0,) * len(s))
    octets = [pl.BlockSpec((None, C * SUBLANES, LANES), functools.partial(lambda q, i, o: (o, i, q), o=o))
              for o in range(n_oct)]
    return pl.pallas_call(
        kern,
        grid=(n_q, n_col_blocks),
        in_specs=octets + [grp(R1, R), grp(R, P4),
                           pl.BlockSpec((n_grp // 2, 2 * SUBLANES, P4 // 2), lambda q, i: (q, 0, 0))],
        out_specs=pl.BlockSpec((n_oct, C * SUBLANES, LANES), lambda q, i: (0, i, q)),
        out_shape=jax.ShapeDtypeStruct((n_oct, t_oct, ds5), F32),
        scratch_shapes=[pltpu.VMEM((n_grp, R, C), BF16), pltpu.VMEM((n_grp, R, C), F32)],
        compiler_params=pltpu.CompilerParams(dimension_semantics=("arbitrary", "arbitrary"),
                                             vmem_limit_bytes=BIG_VMEM_LIMIT),
        name="s5",
    )(*([u] * n_oct), w1, wout, al)


def _out_kernel(x_ref, hm_ref, y_ref, wglu_ref, bglu_ref, wo_ref,
                wgate_ref, wup_ref, wdown_ref, gf_ref, out_ref, acc_ref, xn_ref, *, ff_chunk, final_norm):
    dm = hm_ref.shape[1]
    n_oct, rows_oct, ds5 = y_ref.shape
    y = jnp.stack([y_ref[oc].reshape(rows_oct // SUBLANES, SUBLANES, ds5) for oc in range(n_oct)], axis=1)
    g = jax.nn.gelu(y.reshape(n_oct * rows_oct, ds5))
    z = jnp.dot(g.astype(BF16), wglu_ref[...], preferred_element_type=F32) + bglu_ref[...]
    gg = g * _sigmoid(z)
    hs = (gg * lax.rsqrt(jnp.mean(gg * gg, axis=-1, keepdims=True) + EPS)).astype(BF16)
    h1 = (x_ref[...] + jnp.dot(hm_ref[...], wo_ref[0:dm, :], preferred_element_type=F32)
          + jnp.dot(hs, wo_ref[dm:, :], preferred_element_type=F32))
    xn_ref[...] = (h1 * lax.rsqrt(jnp.mean(h1 * h1, axis=-1, keepdims=True) + EPS)).astype(BF16)
    acc_ref[...] = h1

    def ffn_chunk(c, carry):
        cs = pl.ds(pl.multiple_of(c * ff_chunk, ff_chunk), ff_chunk)
        xn2 = xn_ref[...]
        gate = jnp.dot(xn2, wgate_ref[:, cs], preferred_element_type=F32)
        up = jnp.dot(xn2, wup_ref[:, cs], preferred_element_type=F32)
        act = (_silu(gate) * up).astype(BF16)
        acc_ref[...] += jnp.dot(act, wdown_ref[cs, :], preferred_element_type=F32)
        return carry

    lax.fori_loop(0, wgate_ref.shape[1] // ff_chunk, ffn_chunk, 0, unroll=4)
    out_ref[...] = _rmsnorm(acc_ref[...], gf_ref[...]) if final_norm else acc_ref[...]


def _out_block(x2, hm, y, wglu, bglu, wo, wgate, wup, wdown, gf, *, tm, ff_chunk, final_norm):
    T, D = x2.shape
    dm = hm.shape[1]
    n_oct, _, ds5 = y.shape
    assert wgate.shape[1] % ff_chunk == 0
    kern = functools.partial(_out_kernel, ff_chunk=ff_chunk, final_norm=final_norm)
    tok = lambda w: pl.BlockSpec((tm, w), lambda i: (i, 0))
    return pl.pallas_call(
        kern,
        grid=(T // tm,),
        in_specs=[tok(D), tok(dm), pl.BlockSpec((n_oct, tm // n_oct, ds5), lambda i: (0, i, 0)),
                  _const_spec(wglu.shape), _const_spec((1, ds5)), _const_spec(wo.shape),
                  _const_spec(wgate.shape), _const_spec(wup.shape), _const_spec(wdown.shape),
                  _const_spec((1, D))],
        out_specs=tok(D),
        out_shape=jax.ShapeDtypeStruct((T, D), F32),
        scratch_shapes=[pltpu.VMEM((tm, D), F32), pltpu.VMEM((tm, D), BF16)],
        compiler_params=pltpu.CompilerParams(dimension_semantics=("arbitrary",),
                                             vmem_limit_bytes=BIG_VMEM_LIMIT),
        name="out_block",
    )(x2, hm, y, wglu, bglu, wo, wgate, wup, wdown, gf)


def kernel(x, norm1_g, w_in, if_bias, conv_qk, mlstm_norm_g, a_re, a_im, log_dt, b_re, b_im, c_re, c_im,
           d_skip, w_glu, b_glu, s5_norm_g, w_out, norm2_g, w_gate, w_up, w_down, norm_f_g):
    B, S, D = x.shape
    depth = w_in.shape[0]
    H = N_MLSTM_HEADS
    dm = mlstm_norm_g.shape[1]
    T = B * S
    L5 = S5_CHUNK
    nc5 = S // L5
    row = lambda a: a.reshape(1, -1).astype(F32)

    h = x.reshape(T, D)
    for l in range(depth):
        last = l == depth - 1
        gb = jnp.zeros((1, LANES), F32).at[0, 0:2 * H].set(if_bias[l])

        q, kt, v, o, gatest, u = _inproj(h, row(norm1_g[l]), w_in[l].T.astype(F32), gb, conv_qk[l].astype(F32),
                                         seq=S, tm=min(INPROJ_TM, S), d_mlstm=dm, n_oct=L5 // SUBLANES)
        ones = lambda n: jnp.ones((n,), F32)
        g_mix = jnp.concatenate([mlstm_norm_g[l], s5_norm_g[l]]).astype(F32)
        g_ffn = norm2_g[l].astype(F32)
        hm, wglu, wo, wgate, wup, wdown = _mlstm(
            q, kt, v, o, gatest,
            [w.astype(F32) for w in (w_glu[l], w_out[l], w_gate[l], w_up[l], w_down[l])],
            [ones(w_glu.shape[1]), g_mix, g_ffn, g_ffn, ones(w_down.shape[1])], batch=B, seq=S)

        w1, wout, al = _s5_prep(a_re[l], a_im[l], log_dt[l], b_re[l], b_im[l], c_re[l], c_im[l],
                                d_skip[l], chunk=L5, n_chunks=nc5)
        y = _s5(u, w1, wout, al, chunk=L5, n_chunks=nc5, n_col_blocks=S5_COL_BLOCKS)

        h = _out_block(h, hm, y, wglu, row(b_glu[l]), wo, wgate, wup, wdown, row(norm_f_g),
                       tm=OUT_TM, ff_chunk=FF_CHUNK, final_norm=last)
    return h.reshape(B, S, D)
```

```python
import functools

import jax
import jax.numpy as jnp
from jax import lax
from jax.experimental import pallas as pl
from jax.experimental.pallas import tpu as pltpu

EPS = 1e-6
N_MLSTM_HEADS = 4
CONV_WIDTH = 4
S5_GROUP = 16
S5_STATE = 64

LANES = 128
SUBLANES = 8
MLSTM_CHUNK = 128
S5_CHUNK = 16
S5_COL_BLOCKS = 1
INPROJ_TM = 1024
OUT_TM = 1024
FF_CHUNK = 256
VMEM_LIMIT = 48 * 1024 * 1024
BIG_VMEM_LIMIT = 58 * 1024 * 1024

F32 = jnp.float32
BF16 = jnp.bfloat16


def _rmsnorm(x, g):
    ms = jnp.mean(x * x, axis=-1, keepdims=True)
    return x * lax.rsqrt(ms + EPS) * g


def _sigmoid(x):
    return 0.5 * jnp.tanh(0.5 * x) + 0.5


def _silu(x):
    h = 0.5 * x
    return h + h * jnp.tanh(h)


def _log_sigmoid(x):
    return -(jnp.maximum(-x, 0.0) + jnp.log1p(jnp.exp(-jnp.abs(x))))


def _const_spec(shape):
    return pl.BlockSpec(shape, lambda *_: (0,) * len(shape), pipeline_mode=pl.Buffered(1))


def _inproj_kernel(x_ref, g1_ref, wt_ref, gb_ref, cw_ref,
                   q_ref, kt_ref, v_ref, o_ref, gatest_ref, u_ref,
                   ext_ref, wm_ref, wg_ref, wu_ref, *, tiles_per_seq, d_mlstm, k_scale):
    tm = x_ref.shape[0]
    dm = d_mlstm
    H = N_MLSTM_HEADS
    halo = SUBLANES

    @pl.when(pl.program_id(0) == 0)
    def _():
        g1 = g1_ref[...]
        for c0 in range(0, 4 * dm, dm):
            wm_ref[:, c0:c0 + dm] = (wt_ref[c0:c0 + dm, :] * g1).T.astype(BF16)
        wu_ref[...] = (wt_ref[4 * dm + 2 * H:, :] * g1).T.astype(BF16)
        fill = jnp.zeros((LANES - 2 * H, wt_ref.shape[1]), F32)
        wg_ref[...] = jnp.concatenate([wt_ref[4 * dm:4 * dm + 2 * H, :] * g1, fill], axis=0).T.astype(BF16)

    x = x_ref[...]
    xn = (x * lax.rsqrt(jnp.mean(x * x, axis=-1, keepdims=True) + EPS)).astype(BF16)

    @pl.when(pl.program_id(0) % tiles_per_seq == 0)
    def _():
        ext_ref[0:halo, :] = jnp.zeros((halo, 2 * dm), F32)

    ext_ref[halo:halo + tm, :] = jnp.dot(xn, wm_ref[:, 0:2 * dm], preferred_element_type=F32)
    v = jnp.dot(xn, wm_ref[:, 2 * dm:3 * dm], preferred_element_type=F32)
    o = jnp.dot(xn, wm_ref[:, 3 * dm:4 * dm], preferred_element_type=F32)
    u = jnp.dot(xn, wu_ref[...], preferred_element_type=F32)
    gates = jnp.dot(xn, wg_ref[...], preferred_element_type=F32) + gb_ref[...]
    v_ref[...] = v.astype(BF16)
    o_ref[...] = o.astype(BF16)
    n_oct = u_ref.shape[0]
    u4 = u.reshape(tm // (n_oct * SUBLANES), n_oct, SUBLANES, u.shape[1])
    for oc in range(n_oct):
        u_ref[oc] = u4[:, oc].reshape(tm // n_oct, u.shape[1])
    gatest_ref[...] = gates.T[0:2 * H, :]

    def zero_of(t):
        w = pltpu.bitcast(t[tm - SUBLANES:tm, t.shape[1] - LANES:], jnp.uint32)
        return pltpu.bitcast((w >> 16) >> 16, F32)
    anchor = zero_of(v)

    acc = cw_ref[CONV_WIDTH - 1:CONV_WIDTH, :] * ext_ref[halo:halo + tm, :]
    for j in range(1, CONV_WIDTH):
        acc = acc + cw_ref[CONV_WIDTH - 1 - j:CONV_WIDTH - j, :] * ext_ref[halo - j:halo - j + tm, :]
    ext_ref[0:halo, :] = ext_ref[tm:tm + halo, :]
    qk = _silu(acc) + jnp.tile(anchor, (tm // SUBLANES, 2 * dm // LANES))
    q_ref[...] = qk[:, 0:dm].astype(BF16)
    kt_ref[...] = (qk[:, dm:2 * dm] * k_scale).T.astype(BF16)


def _inproj(x2, g1, w_in_t, gb, cw, *, seq, tm, d_mlstm, n_oct):
    T, D = x2.shape
    dm = d_mlstm
    H = N_MLSTM_HEADS
    assert 2 * H == SUBLANES
    ds5 = w_in_t.shape[0] - 4 * dm - 2 * H
    kern = functools.partial(_inproj_kernel, tiles_per_seq=seq // tm, d_mlstm=dm,
                             k_scale=float((dm // H) ** -0.5))
    tok = lambda w: pl.BlockSpec((tm, w), lambda i: (i, 0))
    return pl.pallas_call(
        kern,
        grid=(T // tm,),
        in_specs=[tok(D), _const_spec((1, D)), _const_spec(w_in_t.shape), _const_spec((1, LANES)),
                  _const_spec(cw.shape)],
        out_specs=[tok(dm), pl.BlockSpec((dm, tm), lambda i: (0, i)), tok(dm), tok(dm),
                   pl.BlockSpec((2 * H, tm), lambda i: (0, i)),
                   pl.BlockSpec((n_oct, tm // n_oct, ds5), lambda i: (0, i, 0))],
        out_shape=[jax.ShapeDtypeStruct((T, dm), BF16), jax.ShapeDtypeStruct((dm, T), BF16),
                   jax.ShapeDtypeStruct((T, dm), BF16), jax.ShapeDtypeStruct((T, dm), BF16),
                   jax.ShapeDtypeStruct((2 * H, T), F32),
                   jax.ShapeDtypeStruct((n_oct, T // n_oct, ds5), F32)],
        scratch_shapes=[pltpu.VMEM((tm + 2 * SUBLANES, 2 * dm), F32), pltpu.VMEM((D, 4 * dm), BF16),
                        pltpu.VMEM((D, LANES), BF16), pltpu.VMEM((D, ds5), BF16)],
        compiler_params=pltpu.CompilerParams(dimension_semantics=("arbitrary",),
                                             vmem_limit_bytes=VMEM_LIMIT),
        name="inproj",
    )(x2, g1, w_in_t, gb, cw)


def _lane_scan(x, op, identity, seg):
    pos = lax.broadcasted_iota(jnp.int32, x.shape, x.ndim - 1) & (seg - 1)
    sh = 1
    while sh < seg:
        x = op(x, jnp.where(pos >= sh, pltpu.roll(x, sh, x.ndim - 1), identity))
        sh *= 2
    return x


def _mlstm_kernel(q_ref, kt_ref, v_ref, o_ref, gatest_ref, *refs, chunk, n_cast):
    S = q_ref.shape[0]
    H = N_MLSTM_HEADS
    dh = q_ref.shape[1] // H
    L = chunk
    assert L == dh == LANES
    w32_refs, ws_refs = refs[0:n_cast], refs[n_cast:2 * n_cast]
    out_ref, w16_refs = refs[2 * n_cast], refs[2 * n_cast + 1:3 * n_cast + 1]
    a_ref, g0_ref, em_ref, gend_ref = refs[3 * n_cast + 1:3 * n_cast + 5]
    c_refs, nd_refs = refs[3 * n_cast + 5:3 * n_cast + 5 + H], refs[3 * n_cast + 5 + H:]

    for w32_ref, ws_ref, w16_ref in zip(w32_refs, ws_refs, w16_refs):
        w16_ref[...] = (w32_ref[...] * ws_ref[...]).astype(BF16)

    @pl.when(pl.program_id(0) == 0)
    def _():
        gates = gatest_ref[...]
        f_cum = pltpu.roll(_lane_scan(_log_sigmoid(gates), jnp.add, 0.0, S), H, 0)
        a = gates - f_cum
        g0 = jnp.maximum(_lane_scan(a, jnp.maximum, -jnp.inf, S), 0.0)
        a_ref[...] = a
        g0_ref[...] = g0
        em_ref[...] = jnp.exp(-(f_cum + g0))
        for c in range(gatest_ref.shape[1] // L):
            gend_ref[:, c * L:(c + 1) * L] = jnp.broadcast_to(g0[:, (c + 1) * L - 1:(c + 1) * L], (SUBLANES, L))

    t0 = pl.program_id(0) * S

    for c_ref in c_refs:
        c_ref[...] = jnp.zeros(c_ref.shape, F32)
    causal = (lax.broadcasted_iota(jnp.int32, (L, L), 0) >= lax.broadcasted_iota(jnp.int32, (L, L), 1))
    ones_blk = jnp.ones((L, dh), BF16)
    mean_mat = jnp.full((dh, dh), 1.0 / dh, BF16)

    def mix(c, h, g_prev):
        r0 = pl.multiple_of(c * L, L)
        hs = slice(h * dh, (h + 1) * dh)
        qc = q_ref[pl.ds(r0, L), hs]
        ktc = kt_ref[hs, pl.ds(r0, L)]
        v_aug = jnp.concatenate([v_ref[pl.ds(r0, L), hs], ones_blk], axis=1)
        g0l = pl.ds(pl.multiple_of(t0 + r0, L), L)
        a_row = a_ref[h:h + 1, g0l]
        g_end = gend_ref[h:h + 1, g0l]
        g0_t = jnp.broadcast_to(g0_ref[h:h + 1, g0l], (L, L)).T

        s_qk = jnp.dot(qc, ktc, preferred_element_type=F32)
        p = (jnp.exp(jnp.where(causal, a_row - g0_t, -jnp.inf)) * s_qk).astype(BF16)
        c_prev = c_refs[h][...]
        q_w = (qc.astype(F32) * jnp.exp(g_prev - g0_t)).astype(BF16)
        nd_refs[h][...] = jnp.dot(jnp.concatenate([q_w, p], axis=1),
                                  jnp.concatenate([c_prev.astype(BF16), v_aug], axis=0),
                                  preferred_element_type=F32)

        kte = (ktc.astype(F32) * jnp.exp(a_row - g_end)).astype(BF16)
        decay = jnp.exp(g_prev - g_end)
        c_refs[h][...] = (jnp.concatenate([decay, decay], axis=1) * c_prev
                          + jnp.dot(kte, v_aug, preferred_element_type=F32))
        return g_end

    def emit(c, h):
        r0 = pl.multiple_of(c * L, L)
        hs = slice(h * dh, (h + 1) * dh)
        em_t = jnp.broadcast_to(em_ref[h:h + 1, pl.ds(pl.multiple_of(t0 + r0, L), L)], (L, L)).T
        nd = nd_refs[h][...]
        h_tilde = nd[:, 0:dh] / jnp.maximum(jnp.abs(nd[:, dh:2 * dh]), em_t)
        hm = _sigmoid(o_ref[pl.ds(r0, L), hs].astype(F32)) * h_tilde
        mu = jnp.dot(hm.astype(BF16), mean_mat, preferred_element_type=F32)
        hc = hm - mu
        var = jnp.dot((hc * hc).astype(BF16), mean_mat, preferred_element_type=F32)
        out_ref[pl.ds(r0, L), hs] = (hc * lax.rsqrt(var + EPS)).astype(BF16)

    zero = jnp.zeros((1, L), F32)
    g_first = tuple(mix(0, h, zero) for h in range(H))

    def body(c, g_carry):
        for h in range(H):
            emit(c - 1, h)
        return tuple(mix(c, h, g_carry[h]) for h in range(H))

    lax.fori_loop(1, S // L, body, g_first, unroll=5)
    for h in range(H):
        emit(S // L - 1, h)


def _mlstm(q, kt, v, o, gatest, weights, row_scales, *, batch, seq):
    T, dm = q.shape
    dh = dm // N_MLSTM_HEADS
    kern = functools.partial(_mlstm_kernel, chunk=MLSTM_CHUNK, n_cast=len(weights))
    tok = lambda w: pl.BlockSpec((seq, w), lambda b: (b, 0))
    rows = pltpu.VMEM((SUBLANES, T), F32)
    bf16_rows = 2 * SUBLANES
    assert all(w.shape[0] % (batch * bf16_rows) == 0 for w in weights)
    w_specs = [pl.BlockSpec((w.shape[0] // batch, w.shape[1]), lambda b: (b, 0)) for w in weights]
    s_specs = [pl.BlockSpec((w.shape[0] // batch, 1), lambda b: (b, 0)) for w in weights]
    scales = [sc.reshape(-1, 1).astype(F32) for sc in row_scales]
    return pl.pallas_call(
        kern,
        grid=(batch,),
        in_specs=[tok(dm), pl.BlockSpec((dm, seq), lambda b: (0, b)), tok(dm), tok(dm),
                  _const_spec((SUBLANES, T))] + w_specs + s_specs,
        out_specs=[tok(dm)] + w_specs,
        out_shape=[jax.ShapeDtypeStruct((T, dm), BF16)] + [jax.ShapeDtypeStruct(w.shape, BF16) for w in weights],
        scratch_shapes=([rows, rows, rows, rows] + [pltpu.VMEM((dh, 2 * dh), F32)] * N_MLSTM_HEADS
                        + [pltpu.VMEM((MLSTM_CHUNK, 2 * dh), F32)] * N_MLSTM_HEADS),
        compiler_params=pltpu.CompilerParams(dimension_semantics=("arbitrary",),
                                             vmem_limit_bytes=VMEM_LIMIT),
        name="mlstm",
    )(q, kt, v, o, gatest, *weights, *scales)


def _s5_prep_kernel(arow_ref, acol_ref, bre_ref, bim_ref, cre_ref, cim_ref, drow_ref,
                    rep_ref, rep2_ref, selrev_ref, pwcol_ref, w1_ref, wout_ref, al_ref, *, chunk):
    L = chunk
    P = S5_STATE
    Hc = S5_GROUP
    R = L * Hc
    hi = lax.Precision.HIGHEST

    def select(x, sel_ref):
        x_hi = x.astype(BF16)
        x_lo = (x - x_hi.astype(F32)).astype(BF16)
        sel = sel_ref[...]
        return (jnp.dot(x_hi, sel, preferred_element_type=F32) + jnp.dot(x_lo, sel, preferred_element_type=F32))

    dt_r = jnp.exp(arow_ref[2:3, :])
    da_re, da_im = dt_r * arow_ref[0:1, :], dt_r * arow_ref[1:2, :]

    def abar_pow(tau):
        mag = jnp.exp(tau * da_re)
        ang = tau * da_im
        return mag * jnp.cos(ang), mag * jnp.sin(ang)

    n_pow = -(-(L + 1) // SUBLANES) * SUBLANES
    er_re, er_im = abar_pow(lax.broadcasted_iota(jnp.int32, (n_pow, 1), 0).astype(F32))
    pad = jnp.zeros((2 * P - n_pow, 2 * P), F32)
    ec_re = jnp.concatenate([er_re, pad], axis=0).T
    ec_im = jnp.concatenate([er_im, pad], axis=0).T

    a_re_c, a_im_c = acol_ref[:, 0:1], acol_ref[:, 1:2]
    ab_re, ab_im = ec_re[:, 1:2], ec_im[:, 1:2]
    den = a_re_c * a_re_c + a_im_c * a_im_c
    nr = ab_re - 1.0
    z_re = (nr * a_re_c + ab_im * a_im_c) / den
    z_im = (ab_im * a_re_c - nr * a_im_c) / den
    bb_re = z_re * bre_ref[...] - z_im * bim_ref[...]
    bb_im = z_re * bim_ref[...] + z_im * bre_ref[...]
    first = lax.broadcasted_iota(jnp.int32, (2 * P, Hc), 0) < P
    bd = lambda b: jnp.concatenate([jnp.where(first, b, 0.0), jnp.where(first, 0.0, b)], axis=1)

    def rows(e, lo):
        return jnp.concatenate([jnp.broadcast_to(e[lo + t:lo + t + 1, :], (Hc, 2 * P)) for t in range(L)], axis=0)

    c_re = jnp.concatenate([cre_ref[...]] * L, axis=0)
    c_im = jnp.concatenate([cim_ref[...]] * L, axis=0)

    def c_times_pow(lo):
        e_re, e_im = rows(er_re, lo), rows(er_im, lo)
        return c_re * e_re - c_im * e_im, c_re * e_im + c_im * e_re

    ce_re, ce_im = c_times_pow(0)
    kflat = (jnp.dot(ce_re, bd(bb_re), preferred_element_type=F32, precision=hi)
             - jnp.dot(ce_im, bd(bb_im), preferred_element_type=F32, precision=hi))
    row = lax.broadcasted_iota(jnp.int32, (R, 2 * Hc), 0)
    col = lax.broadcasted_iota(jnp.int32, (R, 2 * Hc), 1) & (Hc - 1)
    kflat = kflat + jnp.where(row == col, drow_ref[...], 0.0)

    m = select(kflat, rep2_ref)
    blk = (lax.broadcasted_iota(jnp.int32, (R, 2 * R), 1) & (R - 1)) // Hc
    sh = 1
    while sh < L:
        shifted = jnp.concatenate([jnp.zeros((sh * Hc, 2 * R), F32), m[0:R - sh * Hc, :]], axis=0)
        m = jnp.where((blk & sh) != 0, shifted, m)
        sh *= 2

    ev_re, ev_im = select(ec_re, selrev_ref), select(ec_im, selrev_ref)
    bt_re, bt_im = select(bb_re, rep_ref), select(bb_im, rep_ref)
    st_re = (ev_re * bt_re - ev_im * bt_im).astype(BF16)
    st_im = (ev_re * bt_im + ev_im * bt_re).astype(BF16)

    co_re, co_im = c_times_pow(1)
    lane_first = lax.broadcasted_iota(jnp.int32, (R, 2 * P), 1) < P
    for g in range(2):
        w1_ref[g, 0:R, :] = m[:, g * R:(g + 1) * R].astype(BF16)
        w1_ref[g, R:R + P, :] = st_re[g * P:(g + 1) * P, :]
        w1_ref[g, R + P:R + 2 * P, :] = st_im[g * P:(g + 1) * P, :]
        own = lane_first if g == 0 else jnp.logical_not(lane_first)
        wout_ref[g, :, 0:2 * P] = jnp.where(own, co_re, 0.0).astype(BF16)
        wout_ref[g, :, 2 * P:4 * P] = jnp.where(own, -co_im, 0.0).astype(BF16)

    sc_re, sc_im = abar_pow(pwcol_ref[...])
    al_ref[0:SUBLANES, :] = sc_re
    al_ref[SUBLANES:2 * SUBLANES, :] = sc_im


def _s5_prep(a_re, a_im, log_dt, b_re, b_im, c_re, c_im, d_skip, *, chunk, n_chunks):
    G, P = a_re.shape
    Hc = S5_GROUP
    R = chunk * Hc
    assert G % 2 == 0 and 2 * P == LANES
    G2 = G // 2
    ldt = jnp.broadcast_to(log_dt[:, None], (G, P))
    arow = jnp.stack([a_re.reshape(G2, 2 * P), a_im.reshape(G2, 2 * P), ldt.reshape(G2, 2 * P)], axis=1)
    acol = jnp.stack([a_re.reshape(G2, 2 * P), a_im.reshape(G2, 2 * P)], axis=2)
    pair_lanes = lambda c: c.reshape(G2, 2, Hc, P).transpose(0, 2, 1, 3).reshape(G2, Hc, 2 * P)
    drow = d_skip.reshape(G2, 1, 2 * Hc)
    rep = (jnp.arange(R)[None, :] % Hc == jnp.arange(Hc)[:, None]).astype(BF16)
    rep2 = jnp.kron(jnp.eye(2, dtype=BF16), rep)
    selrev = (jnp.arange(LANES)[:, None] == chunk - 1 - jnp.arange(R)[None, :] // Hc).astype(BF16)
    n_steps = max(1, (n_chunks - 1).bit_length())
    assert chunk + 1 <= LANES and n_steps <= SUBLANES
    pwcol = jnp.zeros((SUBLANES, 1), F32).at[:n_steps, 0].set(chunk * 2.0 ** jnp.arange(n_steps))
    kern = functools.partial(_s5_prep_kernel, chunk=chunk)
    pair = lambda *s: pl.BlockSpec((None,) + s, lambda g: (g,) + (0,) * len(s))
    two = lambda *s: pl.BlockSpec((2,) + s, lambda g: (g,) + (0,) * len(s))
    return pl.pallas_call(
        kern,
        grid=(G2,),
        in_specs=[pair(3, 2 * P), pair(2 * P, 2), pair(2 * P, Hc), pair(2 * P, Hc), pair(Hc, 2 * P), pair(Hc, 2 * P),
                  pair(1, 2 * Hc), _const_spec((Hc, R)), _const_spec((2 * Hc, 2 * R)), _const_spec((LANES, R)),
                  _const_spec((SUBLANES, 1))],
        out_specs=[two(R + 2 * P, R), two(R, 4 * P), pair(2 * SUBLANES, 2 * P)],
        out_shape=[jax.ShapeDtypeStruct((G, R + 2 * P, R), BF16),
                   jax.ShapeDtypeStruct((G, R, 4 * P), BF16),
                   jax.ShapeDtypeStruct((G2, 2 * SUBLANES, 2 * P), F32)],
        compiler_params=pltpu.CompilerParams(dimension_semantics=("arbitrary",),
                                             vmem_limit_bytes=VMEM_LIMIT),
        name="s5_prep",
    )(arow, acol, b_re.reshape(G2, 2 * P, Hc), b_im.reshape(G2, 2 * P, Hc), pair_lanes(c_re), pair_lanes(c_im),
      drow, rep, rep2, selrev, pwcol)


def _s5_kernel(*refs, chunk, n_chunks):
    L = chunk
    n_oct = L // SUBLANES
    u_refs, (w1_ref, wout_ref, al_ref, y_ref, ut_ref, yt_ref) = refs[0:n_oct], refs[n_oct:]
    P = S5_STATE
    Hc = S5_GROUP
    R = L * Hc
    C = y_ref.shape[1] // SUBLANES
    n_grp = LANES // Hc

    for s in range(L):
        xt = u_refs[s // SUBLANES][pl.ds(s % SUBLANES, C, stride=SUBLANES), :].astype(BF16).T
        for j in range(n_grp):
            ut_ref[j, s * Hc:(s + 1) * Hc, :] = xt[j * Hc:(j + 1) * Hc, :]

    seg_pos = lax.broadcasted_iota(jnp.int32, (C, 2 * P), 0) & (n_chunks - 1)
    nt = (((1,), (1,)), ((), ()))

    def shifted(x, sh):
        return jnp.where(seg_pos >= sh, pltpu.roll(x, sh, 0), 0.0)

    def group_pair(i, carry):
        j0 = 2 * i
        res = [jnp.dot(w1_ref[j0 + d], ut_ref[j0 + d], preferred_element_type=F32) for d in range(2)]
        x_re = jnp.concatenate([res[0][R:R + P], res[1][R:R + P]], axis=0).T
        x_im = jnp.concatenate([res[0][R + P:R + 2 * P], res[1][R + P:R + 2 * P]], axis=0).T
        sh, k = 1, 0
        while sh < n_chunks:
            a_re = al_ref[i, k:k + 1, :]
            a_im = al_ref[i, SUBLANES + k:SUBLANES + k + 1, :]
            s_re, s_im = shifted(x_re, sh), shifted(x_im, sh)
            x_re, x_im = (x_re + a_re * s_re - a_im * s_im, x_im + a_re * s_im + a_im * s_re)
            sh *= 2
            k += 1
        xprev = jnp.concatenate([shifted(x_re, 1), shifted(x_im, 1)], axis=1).astype(BF16)
        for d in range(2):
            yt_ref[j0 + d] = res[d][0:R, :] + lax.dot_general(wout_ref[j0 + d], xprev, nt,
                                                              preferred_element_type=F32)
        return carry

    lax.fori_loop(0, n_grp // 2, group_pair, 0, unroll=2)

    for t in range(L):
        zt = jnp.concatenate([yt_ref[j, t * Hc:(t + 1) * Hc, :] for j in range(n_grp)], axis=0)
        y_ref[t // SUBLANES, pl.ds(t % SUBLANES, C, stride=SUBLANES), :] = zt.T


def _s5(u, w1, wout, al, *, chunk, n_chunks, n_col_blocks):
    n_oct, t_oct, ds5 = u.shape
    T = n_oct * t_oct
    assert n_oct * SUBLANES == chunk
    G, R1, R = w1.shape
    assert G * S5_GROUP == ds5
    P4 = wout.shape[-1]
    n_grp = LANES // S5_GROUP
    n_q = ds5 // LANES
    tb = T // n_col_blocks
    C = tb // chunk
    kern = functools.partial(_s5_kernel, chunk=chunk, n_chunks=n_chunks)
    grp = lambda *s: pl.BlockSpec((n_grp,) + s, lambda q, i: (q,) + (0,) * len(s))
    octets = [pl.BlockSpec((None, C * SUBLANES, LANES), functools.partial(lambda q, i, o: (o, i, q), o=o))
              for o in range(n_oct)]
    return pl.pallas_call(
        kern,
        grid=(n_q, n_col_blocks),
        in_specs=octets + [grp(R1, R), grp(R, P4),
                           pl.BlockSpec((n_grp // 2, 2 * SUBLANES, P4 // 2), lambda q, i: (q, 0, 0))],
        out_specs=pl.BlockSpec((n_oct, C * SUBLANES, LANES), lambda q, i: (0, i, q)),
        out_shape=jax.ShapeDtypeStruct((n_oct, t_oct, ds5), F32),
        scratch_shapes=[pltpu.VMEM((n_grp, R, C), BF16), pltpu.VMEM((n_grp, R, C), F32)],
        compiler_params=pltpu.CompilerParams(dimension_semantics=("arbitrary", "arbitrary"),
                                             vmem_limit_bytes=BIG_VMEM_LIMIT),
        name="s5",
    )(*([u] * n_oct), w1, wout, al)


def _out_kernel(x_ref, hm_ref, y_ref, wglu_ref, bglu_ref, wo_ref,
                wgate_ref, wup_ref, wdown_ref, gf_ref, out_ref, acc_ref, xn_ref, *, ff_chunk, final_norm):
    dm = hm_ref.shape[1]
    n_oct, rows_oct, ds5 = y_ref.shape
    y = jnp.stack([y_ref[oc].reshape(rows_oct // SUBLANES, SUBLANES, ds5) for oc in range(n_oct)], axis=1)
    g = jax.nn.gelu(y.reshape(n_oct * rows_oct, ds5))
    z = jnp.dot(g.astype(BF16), wglu_ref[...], preferred_element_type=F32) + bglu_ref[...]
    gg = g * _sigmoid(z)
    hs = (gg * lax.rsqrt(jnp.mean(gg * gg, axis=-1, keepdims=True) + EPS)).astype(BF16)
    h1 = (x_ref[...] + jnp.dot(hm_ref[...], wo_ref[0:dm, :], preferred_element_type=F32)
          + jnp.dot(hs, wo_ref[dm:, :], preferred_element_type=F32))
    xn_ref[...] = (h1 * lax.rsqrt(jnp.mean(h1 * h1, axis=-1, keepdims=True) + EPS)).astype(BF16)
    acc_ref[...] = h1

    def ffn_chunk(c, carry):
        cs = pl.ds(pl.multiple_of(c * ff_chunk, ff_chunk), ff_chunk)
        xn2 = xn_ref[...]
        gate = jnp.dot(xn2, wgate_ref[:, cs], preferred_element_type=F32)
        up = jnp.dot(xn2, wup_ref[:, cs], preferred_element_type=F32)
        act = (_silu(gate) * up).astype(BF16)
        acc_ref[...] += jnp.dot(act, wdown_ref[cs, :], preferred_element_type=F32)
        return carry

    lax.fori_loop(0, wgate_ref.shape[1] // ff_chunk, ffn_chunk, 0, unroll=4)
    out_ref[...] = _rmsnorm(acc_ref[...], gf_ref[...]) if final_norm else acc_ref[...]


def _out_block(x2, hm, y, wglu, bglu, wo, wgate, wup, wdown, gf, *, tm, ff_chunk, final_norm):
    T, D = x2.shape
    dm = hm.shape[1]
    n_oct, _, ds5 = y.shape
    assert wgate.shape[1] % ff_chunk == 0
    kern = functools.partial(_out_kernel, ff_chunk=ff_chunk, final_norm=final_norm)
    tok = lambda w: pl.BlockSpec((tm, w), lambda i: (i, 0))
    return pl.pallas_call(
        kern,
        grid=(T // tm,),
        in_specs=[tok(D), tok(dm), pl.BlockSpec((n_oct, tm // n_oct, ds5), lambda i: (0, i, 0)),
                  _const_spec(wglu.shape), _const_spec((1, ds5)), _const_spec(wo.shape),
                  _const_spec(wgate.shape), _const_spec(wup.shape), _const_spec(wdown.shape),
                  _const_spec((1, D))],
        out_specs=tok(D),
        out_shape=jax.ShapeDtypeStruct((T, D), F32),
        scratch_shapes=[pltpu.VMEM((tm, D), F32), pltpu.VMEM((tm, D), BF16)],
        compiler_params=pltpu.CompilerParams(dimension_semantics=("arbitrary",),
                                             vmem_limit_bytes=BIG_VMEM_LIMIT),
        name="out_block",
    )(x2, hm, y, wglu, bglu, wo, wgate, wup, wdown, gf)


def kernel(x, norm1_g, w_in, if_bias, conv_qk, mlstm_norm_g, a_re, a_im, log_dt, b_re, b_im, c_re, c_im,
           d_skip, w_glu, b_glu, s5_norm_g, w_out, norm2_g, w_gate, w_up, w_down, norm_f_g):
    B, S, D = x.shape
    depth = w_in.shape[0]
    H = N_MLSTM_HEADS
    dm = mlstm_norm_g.shape[1]
    T = B * S
    L5 = S5_CHUNK
    nc5 = S // L5
    row = lambda a: a.reshape(1, -1).astype(F32)

    h = x.reshape(T, D)
    for l in range(depth):
        last = l == depth - 1
        gb = jnp.zeros((1, LANES), F32).at[0, 0:2 * H].set(if_bias[l])

        q, kt, v, o, gatest, u = _inproj(h, row(norm1_g[l]), w_in[l].T.astype(F32), gb, conv_qk[l].astype(F32),
                                         seq=S, tm=min(INPROJ_TM, S), d_mlstm=dm, n_oct=L5 // SUBLANES)
        ones = lambda n: jnp.ones((n,), F32)
        g_mix = jnp.concatenate([mlstm_norm_g[l], s5_norm_g[l]]).astype(F32)
        g_ffn = norm2_g[l].astype(F32)
        hm, wglu, wo, wgate, wup, wdown = _mlstm(
            q, kt, v, o, gatest,
            [w.astype(F32) for w in (w_glu[l], w_out[l], w_gate[l], w_up[l], w_down[l])],
            [ones(w_glu.shape[1]), g_mix, g_ffn, g_ffn, ones(w_down.shape[1])], batch=B, seq=S)

        w1, wout, al = _s5_prep(a_re[l], a_im[l], log_dt[l], b_re[l], b_im[l], c_re[l], c_im[l],
                                d_skip[l], chunk=L5, n_chunks=nc5)
        y = _s5(u, w1, wout, al, chunk=L5, n_chunks=nc5, n_col_blocks=S5_COL_BLOCKS)

        h = _out_block(h, hm, y, wglu, row(b_glu[l]), wo, wgate, wup, wdown, row(norm_f_g),
                       tm=OUT_TM, ff_chunk=FF_CHUNK, final_norm=last)
    return h.reshape(B, S, D)
```

```python
import functools

import jax
import jax.numpy as jnp
import numpy as np
from jax import lax
from jax.experimental import pallas as pl
from jax.experimental.pallas import tpu as pltpu

EPS = 1e-6
N_MLSTM_HEADS = 4
CONV_WIDTH = 4
S5_GROUP = 16
S5_STATE = 64

LANES = 128
SUBLANES = 8
MLSTM_CHUNK = 128
S5_CHUNK = 16
S5_COL_BLOCKS = 1
INPROJ_TM = 1024
OUT_TM = 1024
FF_CHUNK = 256
VMEM_LIMIT = 48 * 1024 * 1024
BIG_VMEM_LIMIT = 58 * 1024 * 1024

F32 = jnp.float32
BF16 = jnp.bfloat16


def _rmsnorm(x, g):
    ms = jnp.mean(x * x, axis=-1, keepdims=True)
    return x * lax.rsqrt(ms + EPS) * g


def _sigmoid(x):
    return 0.5 * jnp.tanh(0.5 * x) + 0.5


def _silu(x):
    h = 0.5 * x
    return h + h * jnp.tanh(h)


def _log_sigmoid(x):
    return -(jnp.maximum(-x, 0.0) + jnp.log1p(jnp.exp(-jnp.abs(x))))


def _const_spec(shape):
    return pl.BlockSpec(shape, lambda *_: (0,) * len(shape), pipeline_mode=pl.Buffered(1))


def _inproj_kernel(x_ref, g1_ref, wt_ref, gb_ref, cw_ref,
                   q_ref, kt_ref, v_ref, o_ref, gatest_ref, u_ref,
                   ext_ref, wm_ref, wg_ref, wu_ref, *, tiles_per_seq, d_mlstm, k_scale):
    tm = x_ref.shape[0]
    dm = d_mlstm
    H = N_MLSTM_HEADS
    halo = SUBLANES

    @pl.when(pl.program_id(0) == 0)
    def _():
        g1 = g1_ref[...]
        for c0 in range(0, 4 * dm, dm):
            wm_ref[:, c0:c0 + dm] = (wt_ref[c0:c0 + dm, :] * g1).T.astype(BF16)
        wu_ref[...] = (wt_ref[4 * dm + 2 * H:, :] * g1).T.astype(BF16)
        fill = jnp.zeros((LANES - 2 * H, wt_ref.shape[1]), F32)
        wg_ref[...] = jnp.concatenate([wt_ref[4 * dm:4 * dm + 2 * H, :] * g1, fill], axis=0).T.astype(BF16)

    x = x_ref[...]
    xn = (x * lax.rsqrt(jnp.mean(x * x, axis=-1, keepdims=True) + EPS)).astype(BF16)

    @pl.when(pl.program_id(0) % tiles_per_seq == 0)
    def _():
        ext_ref[0:halo, :] = jnp.zeros((halo, 2 * dm), F32)

    ext_ref[halo:halo + tm, :] = jnp.dot(xn, wm_ref[:, 0:2 * dm], preferred_element_type=F32)
    v = jnp.dot(xn, wm_ref[:, 2 * dm:3 * dm], preferred_element_type=F32)
    o = jnp.dot(xn, wm_ref[:, 3 * dm:4 * dm], preferred_element_type=F32)
    u = jnp.dot(xn, wu_ref[...], preferred_element_type=F32)
    gates = jnp.dot(xn, wg_ref[...], preferred_element_type=F32) + gb_ref[...]
    v_ref[...] = v.astype(BF16)
    o_ref[...] = o.astype(BF16)
    n_oct = u_ref.shape[0]
    u4 = u.reshape(tm // (n_oct * SUBLANES), n_oct, SUBLANES, u.shape[1])
    for oc in range(n_oct):
        u_ref[oc] = u4[:, oc].reshape(tm // n_oct, u.shape[1])
    gatest_ref[...] = gates.T[0:2 * H, :]

    def zero_of(t):
        w = pltpu.bitcast(t[tm - SUBLANES:tm, t.shape[1] - LANES:], jnp.uint32)
        return pltpu.bitcast((w >> 16) >> 16, F32)
    anchor = zero_of(v)

    acc = cw_ref[CONV_WIDTH - 1:CONV_WIDTH, :] * ext_ref[halo:halo + tm, :]
    for j in range(1, CONV_WIDTH):
        acc = acc + cw_ref[CONV_WIDTH - 1 - j:CONV_WIDTH - j, :] * ext_ref[halo - j:halo - j + tm, :]
    ext_ref[0:halo, :] = ext_ref[tm:tm + halo, :]
    qk = _silu(acc) + jnp.tile(anchor, (tm // SUBLANES, 2 * dm // LANES))
    q_ref[...] = qk[:, 0:dm].astype(BF16)
    kt_ref[...] = (qk[:, dm:2 * dm] * k_scale).T.astype(BF16)


def _inproj(x2, g1, w_in_t, gb, cw, *, seq, tm, d_mlstm, n_oct):
    T, D = x2.shape
    dm = d_mlstm
    H = N_MLSTM_HEADS
    assert 2 * H == SUBLANES
    ds5 = w_in_t.shape[0] - 4 * dm - 2 * H
    kern = functools.partial(_inproj_kernel, tiles_per_seq=seq // tm, d_mlstm=dm,
                             k_scale=float((dm // H) ** -0.5))
    tok = lambda w: pl.BlockSpec((tm, w), lambda i: (i, 0))
    return pl.pallas_call(
        kern,
        grid=(T // tm,),
        in_specs=[tok(D), _const_spec((1, D)), _const_spec(w_in_t.shape), _const_spec((1, LANES)),
                  _const_spec(cw.shape)],
        out_specs=[tok(dm), pl.BlockSpec((dm, tm), lambda i: (0, i)), tok(dm), tok(dm),
                   pl.BlockSpec((2 * H, tm), lambda i: (0, i)),
                   pl.BlockSpec((n_oct, tm // n_oct, ds5), lambda i: (0, i, 0))],
        out_shape=[jax.ShapeDtypeStruct((T, dm), BF16), jax.ShapeDtypeStruct((dm, T), BF16),
                   jax.ShapeDtypeStruct((T, dm), BF16), jax.ShapeDtypeStruct((T, dm), BF16),
                   jax.ShapeDtypeStruct((2 * H, T), F32),
                   jax.ShapeDtypeStruct((n_oct, T // n_oct, ds5), F32)],
        scratch_shapes=[pltpu.VMEM((tm + 2 * SUBLANES, 2 * dm), F32), pltpu.VMEM((D, 4 * dm), BF16),
                        pltpu.VMEM((D, LANES), BF16), pltpu.VMEM((D, ds5), BF16)],
        compiler_params=pltpu.CompilerParams(dimension_semantics=("arbitrary",),
                                             vmem_limit_bytes=VMEM_LIMIT),
        name="inproj",
    )(x2, g1, w_in_t, gb, cw)


def _lane_scan(x, op, identity, seg):
    pos = lax.broadcasted_iota(jnp.int32, x.shape, x.ndim - 1) & (seg - 1)
    sh = 1
    while sh < seg:
        x = op(x, jnp.where(pos >= sh, pltpu.roll(x, sh, x.ndim - 1), identity))
        sh *= 2
    return x


def _mlstm_kernel(q_ref, kt_ref, v_ref, o_ref, gatest_ref, *refs, chunk, n_cast):
    S = q_ref.shape[0]
    H = N_MLSTM_HEADS
    dh = q_ref.shape[1] // H
    L = chunk
    assert L == dh == LANES
    w32_refs, ws_refs = refs[0:n_cast], refs[n_cast:2 * n_cast]
    out_ref, w16_refs = refs[2 * n_cast], refs[2 * n_cast + 1:3 * n_cast + 1]
    a_ref, g0_ref, em_ref, gend_ref = refs[3 * n_cast + 1:3 * n_cast + 5]
    c_refs, nd_refs = refs[3 * n_cast + 5:3 * n_cast + 5 + H], refs[3 * n_cast + 5 + H:]

    for w32_ref, ws_ref, w16_ref in zip(w32_refs, ws_refs, w16_refs):
        w16_ref[...] = (w32_ref[...] * ws_ref[...]).astype(BF16)

    @pl.when(pl.program_id(0) == 0)
    def _():
        gates = gatest_ref[...]
        f_cum = pltpu.roll(_lane_scan(_log_sigmoid(gates), jnp.add, 0.0, S), H, 0)
        a = gates - f_cum
        g0 = jnp.maximum(_lane_scan(a, jnp.maximum, -jnp.inf, S), 0.0)
        a_ref[...] = a
        g0_ref[...] = g0
        em_ref[...] = jnp.exp(-(f_cum + g0))
        for c in range(gatest_ref.shape[1] // L):
            gend_ref[:, c * L:(c + 1) * L] = jnp.broadcast_to(g0[:, (c + 1) * L - 1:(c + 1) * L], (SUBLANES, L))

    t0 = pl.program_id(0) * S

    for c_ref in c_refs:
        c_ref[...] = jnp.zeros(c_ref.shape, F32)
    causal = (lax.broadcasted_iota(jnp.int32, (L, L), 0) >= lax.broadcasted_iota(jnp.int32, (L, L), 1))
    ones_blk = jnp.ones((L, dh), BF16)
    mean_mat = jnp.full((dh, dh), 1.0 / dh, BF16)

    def mix(c, h, g_prev):
        r0 = pl.multiple_of(c * L, L)
        hs = slice(h * dh, (h + 1) * dh)
        qc = q_ref[pl.ds(r0, L), hs]
        ktc = kt_ref[hs, pl.ds(r0, L)]
        v_aug = jnp.concatenate([v_ref[pl.ds(r0, L), hs], ones_blk], axis=1)
        g0l = pl.ds(pl.multiple_of(t0 + r0, L), L)
        a_row = a_ref[h:h + 1, g0l]
        g_end = gend_ref[h:h + 1, g0l]
        g0_t = jnp.broadcast_to(g0_ref[h:h + 1, g0l], (L, L)).T

        s_qk = jnp.dot(qc, ktc, preferred_element_type=F32)
        p = (jnp.exp(jnp.where(causal, a_row - g0_t, -jnp.inf)) * s_qk).astype(BF16)
        c_prev = c_refs[h][...]
        q_w = (qc.astype(F32) * jnp.exp(g_prev - g0_t)).astype(BF16)
        nd_refs[h][...] = jnp.dot(jnp.concatenate([q_w, p], axis=1),
                                  jnp.concatenate([c_prev.astype(BF16), v_aug], axis=0),
                                  preferred_element_type=F32)

        kte = (ktc.astype(F32) * jnp.exp(a_row - g_end)).astype(BF16)
        decay = jnp.exp(g_prev - g_end)
        c_refs[h][...] = (jnp.concatenate([decay, decay], axis=1) * c_prev
                          + jnp.dot(kte, v_aug, preferred_element_type=F32))
        return g_end

    def emit(c, h):
        r0 = pl.multiple_of(c * L, L)
        hs = slice(h * dh, (h + 1) * dh)
        em_t = jnp.broadcast_to(em_ref[h:h + 1, pl.ds(pl.multiple_of(t0 + r0, L), L)], (L, L)).T
        nd = nd_refs[h][...]
        h_tilde = nd[:, 0:dh] / jnp.maximum(jnp.abs(nd[:, dh:2 * dh]), em_t)
        hm = _sigmoid(o_ref[pl.ds(r0, L), hs].astype(F32)) * h_tilde
        mu = jnp.dot(hm.astype(BF16), mean_mat, preferred_element_type=F32)
        hc = hm - mu
        var = jnp.dot((hc * hc).astype(BF16), mean_mat, preferred_element_type=F32)
        out_ref[pl.ds(r0, L), hs] = (hc * lax.rsqrt(var + EPS)).astype(BF16)

    zero = jnp.zeros((1, L), F32)
    g_first = tuple(mix(0, h, zero) for h in range(H))

    def body(c, g_carry):
        for h in range(H):
            emit(c - 1, h)
        return tuple(mix(c, h, g_carry[h]) for h in range(H))

    lax.fori_loop(1, S // L, body, g_first, unroll=5)
    for h in range(H):
        emit(S // L - 1, h)


def _mlstm(q, kt, v, o, gatest, weights, row_scales, *, batch, seq):
    T, dm = q.shape
    dh = dm // N_MLSTM_HEADS
    kern = functools.partial(_mlstm_kernel, chunk=MLSTM_CHUNK, n_cast=len(weights))
    tok = lambda w: pl.BlockSpec((seq, w), lambda b: (b, 0))
    rows = pltpu.VMEM((SUBLANES, T), F32)
    bf16_rows = 2 * SUBLANES
    assert all(w.shape[0] % (batch * bf16_rows) == 0 for w in weights)
    w_specs = [pl.BlockSpec((w.shape[0] // batch, w.shape[1]), lambda b: (b, 0)) for w in weights]
    s_specs = [pl.BlockSpec((w.shape[0] // batch, 1), lambda b: (b, 0)) for w in weights]
    scales = [sc.reshape(-1, 1).astype(F32) for sc in row_scales]
    return pl.pallas_call(
        kern,
        grid=(batch,),
        in_specs=[tok(dm), pl.BlockSpec((dm, seq), lambda b: (0, b)), tok(dm), tok(dm),
                  _const_spec((SUBLANES, T))] + w_specs + s_specs,
        out_specs=[tok(dm)] + w_specs,
        out_shape=[jax.ShapeDtypeStruct((T, dm), BF16)] + [jax.ShapeDtypeStruct(w.shape, BF16) for w in weights],
        scratch_shapes=([rows, rows, rows, rows] + [pltpu.VMEM((dh, 2 * dh), F32)] * N_MLSTM_HEADS
                        + [pltpu.VMEM((MLSTM_CHUNK, 2 * dh), F32)] * N_MLSTM_HEADS),
        compiler_params=pltpu.CompilerParams(dimension_semantics=("arbitrary",),
                                             vmem_limit_bytes=VMEM_LIMIT),
        name="mlstm",
    )(q, kt, v, o, gatest, *weights, *scales)


def _s5_prep_kernel(arow_ref, acol_ref, bre_ref, bim_ref, cre_ref, cim_ref, drow_ref,
                    rep_ref, rep2_ref, selrev_ref, pwcol_ref, w1_ref, wout_ref, al_ref, *, chunk):
    L = chunk
    P = S5_STATE
    Hc = S5_GROUP
    R = L * Hc
    hi = lax.Precision.HIGHEST

    def select(x, sel_ref):
        x_hi = x.astype(BF16)
        x_lo = (x - x_hi.astype(F32)).astype(BF16)
        sel = sel_ref[...]
        return (jnp.dot(x_hi, sel, preferred_element_type=F32) + jnp.dot(x_lo, sel, preferred_element_type=F32))

    dt_r = jnp.exp(arow_ref[2:3, :])
    da_re, da_im = dt_r * arow_ref[0:1, :], dt_r * arow_ref[1:2, :]

    def abar_pow(tau):
        mag = jnp.exp(tau * da_re)
        ang = tau * da_im
        return mag * jnp.cos(ang), mag * jnp.sin(ang)

    n_pow = -(-(L + 1) // SUBLANES) * SUBLANES
    er_re, er_im = abar_pow(lax.broadcasted_iota(jnp.int32, (n_pow, 1), 0).astype(F32))
    pad = jnp.zeros((2 * P - n_pow, 2 * P), F32)
    ec_re = jnp.concatenate([er_re, pad], axis=0).T
    ec_im = jnp.concatenate([er_im, pad], axis=0).T

    a_re_c, a_im_c = acol_ref[:, 0:1], acol_ref[:, 1:2]
    ab_re, ab_im = ec_re[:, 1:2], ec_im[:, 1:2]
    den = a_re_c * a_re_c + a_im_c * a_im_c
    nr = ab_re - 1.0
    z_re = (nr * a_re_c + ab_im * a_im_c) / den
    z_im = (ab_im * a_re_c - nr * a_im_c) / den
    bb_re = z_re * bre_ref[...] - z_im * bim_ref[...]
    bb_im = z_re * bim_ref[...] + z_im * bre_ref[...]
    first = lax.broadcasted_iota(jnp.int32, (2 * P, Hc), 0) < P
    bd = lambda b: jnp.concatenate([jnp.where(first, b, 0.0), jnp.where(first, 0.0, b)], axis=1)

    def rows(e, lo):
        return jnp.concatenate([jnp.broadcast_to(e[lo + t:lo + t + 1, :], (Hc, 2 * P)) for t in range(L)], axis=0)

    c_re = jnp.concatenate([cre_ref[...]] * L, axis=0)
    c_im = jnp.concatenate([cim_ref[...]] * L, axis=0)

    def c_times_pow(lo):
        e_re, e_im = rows(er_re, lo), rows(er_im, lo)
        return c_re * e_re - c_im * e_im, c_re * e_im + c_im * e_re

    ce_re, ce_im = c_times_pow(0)
    kflat = (jnp.dot(ce_re, bd(bb_re), preferred_element_type=F32, precision=hi)
             - jnp.dot(ce_im, bd(bb_im), preferred_element_type=F32, precision=hi))
    row = lax.broadcasted_iota(jnp.int32, (R, 2 * Hc), 0)
    col = lax.broadcasted_iota(jnp.int32, (R, 2 * Hc), 1) & (Hc - 1)
    kflat = kflat + jnp.where(row == col, drow_ref[...], 0.0)

    m = select(kflat, rep2_ref)
    blk = (lax.broadcasted_iota(jnp.int32, (R, 2 * R), 1) & (R - 1)) // Hc
    sh = 1
    while sh < L:
        shifted = jnp.concatenate([jnp.zeros((sh * Hc, 2 * R), F32), m[0:R - sh * Hc, :]], axis=0)
        m = jnp.where((blk & sh) != 0, shifted, m)
        sh *= 2

    ev_re, ev_im = select(ec_re, selrev_ref), select(ec_im, selrev_ref)
    bt_re, bt_im = select(bb_re, rep_ref), select(bb_im, rep_ref)
    st_re = (ev_re * bt_re - ev_im * bt_im).astype(BF16)
    st_im = (ev_re * bt_im + ev_im * bt_re).astype(BF16)

    co_re, co_im = c_times_pow(1)
    lane_first = lax.broadcasted_iota(jnp.int32, (R, 2 * P), 1) < P
    for g in range(2):
        w1_ref[g, 0:R, :] = m[:, g * R:(g + 1) * R].astype(BF16)
        w1_ref[g, R:R + P, :] = st_re[g * P:(g + 1) * P, :]
        w1_ref[g, R + P:R + 2 * P, :] = st_im[g * P:(g + 1) * P, :]
        own = lane_first if g == 0 else jnp.logical_not(lane_first)
        wout_ref[g, :, 0:2 * P] = jnp.where(own, co_re, 0.0).astype(BF16)
        wout_ref[g, :, 2 * P:4 * P] = jnp.where(own, -co_im, 0.0).astype(BF16)

    sc_re, sc_im = abar_pow(pwcol_ref[...])
    al_ref[0:SUBLANES, :] = sc_re
    al_ref[SUBLANES:2 * SUBLANES, :] = sc_im


def _s5_prep(a_re, a_im, log_dt, b_re, b_im, c_re, c_im, d_skip, *, chunk, n_chunks):
    G, P = a_re.shape
    Hc = S5_GROUP
    R = chunk * Hc
    assert G % 2 == 0 and 2 * P == LANES
    G2 = G // 2
    ldt = jnp.broadcast_to(log_dt[:, None], (G, P))
    arow = jnp.stack([a_re.reshape(G2, 2 * P), a_im.reshape(G2, 2 * P), ldt.reshape(G2, 2 * P)], axis=1)
    acol = jnp.stack([a_re.reshape(G2, 2 * P), a_im.reshape(G2, 2 * P)], axis=2)
    pair_lanes = lambda c: c.reshape(G2, 2, Hc, P).transpose(0, 2, 1, 3).reshape(G2, Hc, 2 * P)
    drow = d_skip.reshape(G2, 1, 2 * Hc)
    rep_np = (np.arange(R)[None, :] % Hc == np.arange(Hc)[:, None]).astype(np.float32)
    rep = jnp.asarray(rep_np, BF16)
    rep2 = jnp.asarray(np.kron(np.eye(2, dtype=np.float32), rep_np), BF16)
    selrev = jnp.asarray(np.arange(LANES)[:, None] == chunk - 1 - np.arange(R)[None, :] // Hc, BF16)
    n_steps = max(1, (n_chunks - 1).bit_length())
    assert chunk + 1 <= LANES and n_steps <= SUBLANES
    pwcol_np = np.zeros((SUBLANES, 1), np.float32)
    pwcol_np[:n_steps, 0] = chunk * 2.0 ** np.arange(n_steps)
    pwcol = jnp.asarray(pwcol_np)
    kern = functools.partial(_s5_prep_kernel, chunk=chunk)
    pair = lambda *s: pl.BlockSpec((None,) + s, lambda g: (g,) + (0,) * len(s))
    two = lambda *s: pl.BlockSpec((2,) + s, lambda g: (g,) + (0,) * len(s))
    return pl.pallas_call(
        kern,
        grid=(G2,),
        in_specs=[pair(3, 2 * P), pair(2 * P, 2), pair(2 * P, Hc), pair(2 * P, Hc), pair(Hc, 2 * P), pair(Hc, 2 * P),
                  pair(1, 2 * Hc), _const_spec((Hc, R)), _const_spec((2 * Hc, 2 * R)), _const_spec((LANES, R)),
                  _const_spec((SUBLANES, 1))],
        out_specs=[two(R + 2 * P, R), two(R, 4 * P), pair(2 * SUBLANES, 2 * P)],
        out_shape=[jax.ShapeDtypeStruct((G, R + 2 * P, R), BF16),
                   jax.ShapeDtypeStruct((G, R, 4 * P), BF16),
                   jax.ShapeDtypeStruct((G2, 2 * SUBLANES, 2 * P), F32)],
        compiler_params=pltpu.CompilerParams(dimension_semantics=("arbitrary",),
                                             vmem_limit_bytes=VMEM_LIMIT),
        name="s5_prep",
    )(arow, acol, b_re.reshape(G2, 2 * P, Hc), b_im.reshape(G2, 2 * P, Hc), pair_lanes(c_re), pair_lanes(c_im),
      drow, rep, rep2, selrev, pwcol)


def _s5_kernel(*refs, chunk, n_chunks):
    L = chunk
    n_oct = L // SUBLANES
    u_refs, (w1_ref, wout_ref, al_ref, y_ref, ut_ref, yt_ref) = refs[0:n_oct], refs[n_oct:]
    P = S5_STATE
    Hc = S5_GROUP
    R = L * Hc
    C = y_ref.shape[1] // SUBLANES
    n_grp = LANES // Hc

    for s in range(L):
        xt = u_refs[s // SUBLANES][pl.ds(s % SUBLANES, C, stride=SUBLANES), :].astype(BF16).T
        for j in range(n_grp):
            ut_ref[j, s * Hc:(s + 1) * Hc, :] = xt[j * Hc:(j + 1) * Hc, :]

    seg_pos = lax.broadcasted_iota(jnp.int32, (C, 2 * P), 0) & (n_chunks - 1)
    nt = (((1,), (1,)), ((), ()))

    def shifted(x, sh):
        return jnp.where(seg_pos >= sh, pltpu.roll(x, sh, 0), 0.0)

    def group_pair(i, carry):
        j0 = 2 * i
        res = [jnp.dot(w1_ref[j0 + d], ut_ref[j0 + d], preferred_element_type=F32) for d in range(2)]
        x_re = jnp.concatenate([res[0][R:R + P], res[1][R:R + P]], axis=0).T
        x_im = jnp.concatenate([res[0][R + P:R + 2 * P], res[1][R + P:R + 2 * P]], axis=0).T
        sh, k = 1, 0
        while sh < n_chunks:
            a_re = al_ref[i, k:k + 1, :]
            a_im = al_ref[i, SUBLANES + k:SUBLANES + k + 1, :]
            s_re, s_im = shifted(x_re, sh), shifted(x_im, sh)
            x_re, x_im = (x_re + a_re * s_re - a_im * s_im, x_im + a_re * s_im + a_im * s_re)
            sh *= 2
            k += 1
        xprev = jnp.concatenate([shifted(x_re, 1), shifted(x_im, 1)], axis=1).astype(BF16)
        for d in range(2):
            yt_ref[j0 + d] = res[d][0:R, :] + lax.dot_general(wout_ref[j0 + d], xprev, nt,
                                                              preferred_element_type=F32)
        return carry

    lax.fori_loop(0, n_grp // 2, group_pair, 0, unroll=2)

    for t in range(L):
        zt = jnp.concatenate([yt_ref[j, t * Hc:(t + 1) * Hc, :] for j in range(n_grp)], axis=0)
        y_ref[t // SUBLANES, pl.ds(t % SUBLANES, C, stride=SUBLANES), :] = zt.T


def _s5(u, w1, wout, al, *, chunk, n_chunks, n_col_blocks):
    n_oct, t_oct, ds5 = u.shape
    T = n_oct * t_oct
    assert n_oct * SUBLANES == chunk
    G, R1, R = w1.shape
    assert G * S5_GROUP == ds5
    P4 = wout.shape[-1]
    n_grp = LANES // S5_GROUP
    n_q = ds5 // LANES
    tb = T // n_col_blocks
    C = tb // chunk
    kern = functools.partial(_s5_kernel, chunk=chunk, n_chunks=n_chunks)
    grp = lambda *s: pl.BlockSpec((n_grp,) + s, lambda q, i: (q,) + (0,) * len(s))
    octets = [pl.BlockSpec((None, C * SUBLANES, LANES), functools.partial(lambda q, i, o: (o, i, q), o=o))
              for o in range(n_oct)]
    return pl.pallas_call(
        kern,
        grid=(n_q, n_col_blocks),
        in_specs=octets + [grp(R1, R), grp(R, P4),
                           pl.BlockSpec((n_grp // 2, 2 * SUBLANES, P4 // 2), lambda q, i: (q, 0, 0))],
        out_specs=pl.BlockSpec((n_oct, C * SUBLANES, LANES), lambda q, i: (0, i, q)),
        out_shape=jax.ShapeDtypeStruct((n_oct, t_oct, ds5), F32),
        scratch_shapes=[pltpu.VMEM((n_grp, R, C), BF16), pltpu.VMEM((n_grp, R, C), F32)],
        compiler_params=pltpu.CompilerParams(dimension_semantics=("arbitrary", "arbitrary"),
                                             vmem_limit_bytes=BIG_VMEM_LIMIT),
        name="s5",
    )(*([u] * n_oct), w1, wout, al)


def _out_kernel(x_ref, hm_ref, y_ref, wglu_ref, bglu_ref, wo_ref,
                wgate_ref, wup_ref, wdown_ref, gf_ref, out_ref, acc_ref, xn_ref, *, ff_chunk, final_norm):
    dm = hm_ref.shape[1]
    n_oct, rows_oct, ds5 = y_ref.shape
    y = jnp.stack([y_ref[oc].reshape(rows_oct // SUBLANES, SUBLANES, ds5) for oc in range(n_oct)], axis=1)
    g = jax.nn.gelu(y.reshape(n_oct * rows_oct, ds5))
    z = jnp.dot(g.astype(BF16), wglu_ref[...], preferred_element_type=F32) + bglu_ref[...]
    gg = g * _sigmoid(z)
    hs = (gg * lax.rsqrt(jnp.mean(gg * gg, axis=-1, keepdims=True) + EPS)).astype(BF16)
    h1 = (x_ref[...] + jnp.dot(hm_ref[...], wo_ref[0:dm, :], preferred_element_type=F32)
          + jnp.dot(hs, wo_ref[dm:, :], preferred_element_type=F32))
    xn_ref[...] = (h1 * lax.rsqrt(jnp.mean(h1 * h1, axis=-1, keepdims=True) + EPS)).astype(BF16)
    acc_ref[...] = h1

    def ffn_chunk(c, carry):
        cs = pl.ds(pl.multiple_of(c * ff_chunk, ff_chunk), ff_chunk)
        xn2 = xn_ref[...]
        gate = jnp.dot(xn2, wgate_ref[:, cs], preferred_element_type=F32)
        up = jnp.dot(xn2, wup_ref[:, cs], preferred_element_type=F32)
        act = (_silu(gate) * up).astype(BF16)
        acc_ref[...] += jnp.dot(act, wdown_ref[cs, :], preferred_element_type=F32)
        return carry

    lax.fori_loop(0, wgate_ref.shape[1] // ff_chunk, ffn_chunk, 0, unroll=4)
    out_ref[...] = _rmsnorm(acc_ref[...], gf_ref[...]) if final_norm else acc_ref[...]


def _out_block(x2, hm, y, wglu, bglu, wo, wgate, wup, wdown, gf, *, tm, ff_chunk, final_norm):
    T, D = x2.shape
    dm = hm.shape[1]
    n_oct, _, ds5 = y.shape
    assert wgate.shape[1] % ff_chunk == 0
    kern = functools.partial(_out_kernel, ff_chunk=ff_chunk, final_norm=final_norm)
    tok = lambda w: pl.BlockSpec((tm, w), lambda i: (i, 0))
    return pl.pallas_call(
        kern,
        grid=(T // tm,),
        in_specs=[tok(D), tok(dm), pl.BlockSpec((n_oct, tm // n_oct, ds5), lambda i: (0, i, 0)),
                  _const_spec(wglu.shape), _const_spec((1, ds5)), _const_spec(wo.shape),
                  _const_spec(wgate.shape), _const_spec(wup.shape), _const_spec(wdown.shape),
                  _const_spec((1, D))],
        out_specs=tok(D),
        out_shape=jax.ShapeDtypeStruct((T, D), F32),
        scratch_shapes=[pltpu.VMEM((tm, D), F32), pltpu.VMEM((tm, D), BF16)],
        compiler_params=pltpu.CompilerParams(dimension_semantics=("arbitrary",),
                                             vmem_limit_bytes=BIG_VMEM_LIMIT),
        name="out_block",
    )(x2, hm, y, wglu, bglu, wo, wgate, wup, wdown, gf)


def kernel(x, norm1_g, w_in, if_bias, conv_qk, mlstm_norm_g, a_re, a_im, log_dt, b_re, b_im, c_re, c_im,
           d_skip, w_glu, b_glu, s5_norm_g, w_out, norm2_g, w_gate, w_up, w_down, norm_f_g):
    B, S, D = x.shape
    depth = w_in.shape[0]
    H = N_MLSTM_HEADS
    dm = mlstm_norm_g.shape[1]
    T = B * S
    L5 = S5_CHUNK
    nc5 = S // L5
    row = lambda a: a.reshape(1, -1).astype(F32)

    h = x.reshape(T, D)
    for l in range(depth):
        last = l == depth - 1
        gb = jnp.pad(if_bias[l].astype(F32).reshape(1, 2 * H), ((0, 0), (0, LANES - 2 * H)))

        q, kt, v, o, gatest, u = _inproj(h, row(norm1_g[l]), w_in[l].T.astype(F32), gb, conv_qk[l].astype(F32),
                                         seq=S, tm=min(INPROJ_TM, S), d_mlstm=dm, n_oct=L5 // SUBLANES)
        ones = lambda n: jnp.ones((n,), F32)
        g_mix = jnp.concatenate([mlstm_norm_g[l], s5_norm_g[l]]).astype(F32)
        g_ffn = norm2_g[l].astype(F32)
        hm, wglu, wo, wgate, wup, wdown = _mlstm(
            q, kt, v, o, gatest,
            [w.astype(F32) for w in (w_glu[l], w_out[l], w_gate[l], w_up[l], w_down[l])],
            [ones(w_glu.shape[1]), g_mix, g_ffn, g_ffn, ones(w_down.shape[1])], batch=B, seq=S)

        w1, wout, al = _s5_prep(a_re[l], a_im[l], log_dt[l], b_re[l], b_im[l], c_re[l], c_im[l],
                                d_skip[l], chunk=L5, n_chunks=nc5)
        y = _s5(u, w1, wout, al, chunk=L5, n_chunks=nc5, n_col_blocks=S5_COL_BLOCKS)

        h = _out_block(h, hm, y, wglu, row(b_glu[l]), wo, wgate, wup, wdown, row(norm_f_g),
                       tm=OUT_TM, ff_chunk=FF_CHUNK, final_norm=last)
    return h.reshape(B, S, D)
```

```python
import functools

import jax
import jax.numpy as jnp
import numpy as np
from jax import lax
from jax.experimental import pallas as pl
from jax.experimental.pallas import tpu as pltpu

EPS = 1e-6
N_MLSTM_HEADS = 4
CONV_WIDTH = 4
S5_GROUP = 16
S5_STATE = 64

LANES = 128
SUBLANES = 8
MLSTM_CHUNK = 128
S5_CHUNK = 16
S5_COL_BLOCKS = 1
INPROJ_TM = 1024
OUT_TM = 1024
FF_CHUNK = 256
VMEM_LIMIT = 48 * 1024 * 1024
BIG_VMEM_LIMIT = 58 * 1024 * 1024

F32 = jnp.float32
BF16 = jnp.bfloat16


def _rmsnorm(x, g):
    ms = jnp.mean(x * x, axis=-1, keepdims=True)
    return x * lax.rsqrt(ms + EPS) * g


def _sigmoid(x):
    return 0.5 * jnp.tanh(0.5 * x) + 0.5


def _silu(x):
    h = 0.5 * x
    return h + h * jnp.tanh(h)


def _log_sigmoid(x):
    return -(jnp.maximum(-x, 0.0) + jnp.log1p(jnp.exp(-jnp.abs(x))))


def _const_spec(shape):
    return pl.BlockSpec(shape, lambda *_: (0,) * len(shape), pipeline_mode=pl.Buffered(1))


def _inproj_kernel(x_ref, g1_ref, wt_ref, gb_ref, cw_ref,
                   q_ref, kt_ref, v_ref, o_ref, gatest_ref, u_ref,
                   ext_ref, wm_ref, wg_ref, wu_ref, *, tiles_per_seq, d_mlstm, k_scale):
    tm = x_ref.shape[0]
    dm = d_mlstm
    H = N_MLSTM_HEADS
    halo = SUBLANES

    @pl.when(pl.program_id(0) == 0)
    def _():
        g1 = g1_ref[...]
        for c0 in range(0, 4 * dm, dm):
            wm_ref[:, c0:c0 + dm] = (wt_ref[c0:c0 + dm, :] * g1).T.astype(BF16)
        wu_ref[...] = (wt_ref[4 * dm + 2 * H:, :] * g1).T.astype(BF16)
        fill = jnp.zeros((LANES - 2 * H, wt_ref.shape[1]), F32)
        wg_ref[...] = jnp.concatenate([wt_ref[4 * dm:4 * dm + 2 * H, :] * g1, fill], axis=0).T.astype(BF16)

    x = x_ref[...]
    xn = (x * lax.rsqrt(jnp.mean(x * x, axis=-1, keepdims=True) + EPS)).astype(BF16)

    @pl.when(pl.program_id(0) % tiles_per_seq == 0)
    def _():
        ext_ref[0:halo, :] = jnp.zeros((halo, 2 * dm), F32)

    ext_ref[halo:halo + tm, :] = jnp.dot(xn, wm_ref[:, 0:2 * dm], preferred_element_type=F32)
    v = jnp.dot(xn, wm_ref[:, 2 * dm:3 * dm], preferred_element_type=F32)
    o = jnp.dot(xn, wm_ref[:, 3 * dm:4 * dm], preferred_element_type=F32)
    u = jnp.dot(xn, wu_ref[...], preferred_element_type=F32)
    gates = jnp.dot(xn, wg_ref[...], preferred_element_type=F32) + gb_ref[...]
    v_ref[...] = v.astype(BF16)
    o_ref[...] = o.astype(BF16)
    n_oct = u_ref.shape[0]
    u4 = u.reshape(tm // (n_oct * SUBLANES), n_oct, SUBLANES, u.shape[1])
    for oc in range(n_oct):
        u_ref[oc] = u4[:, oc].reshape(tm // n_oct, u.shape[1])
    gatest_ref[...] = gates.T[0:2 * H, :]

    def zero_of(t):
        w = pltpu.bitcast(t[tm - SUBLANES:tm, t.shape[1] - LANES:], jnp.uint32)
        return pltpu.bitcast((w >> 16) >> 16, F32)
    anchor = zero_of(v)

    acc = cw_ref[CONV_WIDTH - 1:CONV_WIDTH, :] * ext_ref[halo:halo + tm, :]
    for j in range(1, CONV_WIDTH):
        acc = acc + cw_ref[CONV_WIDTH - 1 - j:CONV_WIDTH - j, :] * ext_ref[halo - j:halo - j + tm, :]
    ext_ref[0:halo, :] = ext_ref[tm:tm + halo, :]
    qk = _silu(acc) + jnp.tile(anchor, (tm // SUBLANES, 2 * dm // LANES))
    q_ref[...] = qk[:, 0:dm].astype(BF16)
    kt_ref[...] = (qk[:, dm:2 * dm] * k_scale).T.astype(BF16)


def _inproj(x2, g1, w_in_t, gb, cw, *, seq, tm, d_mlstm, n_oct):
    T, D = x2.shape
    dm = d_mlstm
    H = N_MLSTM_HEADS
    assert 2 * H == SUBLANES
    ds5 = w_in_t.shape[0] - 4 * dm - 2 * H
    kern = functools.partial(_inproj_kernel, tiles_per_seq=seq // tm, d_mlstm=dm,
                             k_scale=float((dm // H) ** -0.5))
    tok = lambda w: pl.BlockSpec((tm, w), lambda i: (i, 0))
    return pl.pallas_call(
        kern,
        grid=(T // tm,),
        in_specs=[tok(D), _const_spec((1, D)), _const_spec(w_in_t.shape), _const_spec((1, LANES)),
                  _const_spec(cw.shape)],
        out_specs=[tok(dm), pl.BlockSpec((dm, tm), lambda i: (0, i)), tok(dm), tok(dm),
                   pl.BlockSpec((2 * H, tm), lambda i: (0, i)),
                   pl.BlockSpec((n_oct, tm // n_oct, ds5), lambda i: (0, i, 0))],
        out_shape=[jax.ShapeDtypeStruct((T, dm), BF16), jax.ShapeDtypeStruct((dm, T), BF16),
                   jax.ShapeDtypeStruct((T, dm), BF16), jax.ShapeDtypeStruct((T, dm), BF16),
                   jax.ShapeDtypeStruct((2 * H, T), F32),
                   jax.ShapeDtypeStruct((n_oct, T // n_oct, ds5), F32)],
        scratch_shapes=[pltpu.VMEM((tm + 2 * SUBLANES, 2 * dm), F32), pltpu.VMEM((D, 4 * dm), BF16),
                        pltpu.VMEM((D, LANES), BF16), pltpu.VMEM((D, ds5), BF16)],
        compiler_params=pltpu.CompilerParams(dimension_semantics=("arbitrary",),
                                             vmem_limit_bytes=VMEM_LIMIT),
        name="inproj",
    )(x2, g1, w_in_t, gb, cw)


def _lane_scan(x, op, identity, seg):
    pos = lax.broadcasted_iota(jnp.int32, x.shape, x.ndim - 1) & (seg - 1)
    sh = 1
    while sh < seg:
        x = op(x, jnp.where(pos >= sh, pltpu.roll(x, sh, x.ndim - 1), identity))
        sh *= 2
    return x


def _mlstm_kernel(q_ref, kt_ref, v_ref, o_ref, gatest_ref, *refs, chunk, n_cast):
    S = q_ref.shape[0]
    H = N_MLSTM_HEADS
    dh = q_ref.shape[1] // H
    L = chunk
    assert L == dh == LANES
    w32_refs, ws_refs = refs[0:n_cast], refs[n_cast:2 * n_cast]
    out_ref, w16_refs = refs[2 * n_cast], refs[2 * n_cast + 1:3 * n_cast + 1]
    a_ref, g0_ref, em_ref, gend_ref = refs[3 * n_cast + 1:3 * n_cast + 5]
    c_refs, nd_refs = refs[3 * n_cast + 5:3 * n_cast + 5 + H], refs[3 * n_cast + 5 + H:]

    for w32_ref, ws_ref, w16_ref in zip(w32_refs, ws_refs, w16_refs):
        w16_ref[...] = (w32_ref[...] * ws_ref[...]).astype(BF16)

    @pl.when(pl.program_id(0) == 0)
    def _():
        gates = gatest_ref[...]
        f_cum = pltpu.roll(_lane_scan(_log_sigmoid(gates), jnp.add, 0.0, S), H, 0)
        a = gates - f_cum
        g0 = jnp.maximum(_lane_scan(a, jnp.maximum, -jnp.inf, S), 0.0)
        a_ref[...] = a
        g0_ref[...] = g0
        em_ref[...] = jnp.exp(-(f_cum + g0))
        for c in range(gatest_ref.shape[1] // L):
            gend_ref[:, c * L:(c + 1) * L] = jnp.broadcast_to(g0[:, (c + 1) * L - 1:(c + 1) * L], (SUBLANES, L))

    t0 = pl.program_id(0) * S

    for c_ref in c_refs:
        c_ref[...] = jnp.zeros(c_ref.shape, F32)
    causal = (lax.broadcasted_iota(jnp.int32, (L, L), 0) >= lax.broadcasted_iota(jnp.int32, (L, L), 1))
    ones_blk = jnp.ones((L, dh), BF16)
    mean_mat = jnp.full((dh, dh), 1.0 / dh, BF16)

    def mix(c, h, g_prev):
        r0 = pl.multiple_of(c * L, L)
        hs = slice(h * dh, (h + 1) * dh)
        qc = q_ref[pl.ds(r0, L), hs]
        ktc = kt_ref[hs, pl.ds(r0, L)]
        v_aug = jnp.concatenate([v_ref[pl.ds(r0, L), hs], ones_blk], axis=1)
        g0l = pl.ds(pl.multiple_of(t0 + r0, L), L)
        a_row = a_ref[h:h + 1, g0l]
        g_end = gend_ref[h:h + 1, g0l]
        g0_t = jnp.broadcast_to(g0_ref[h:h + 1, g0l], (L, L)).T

        s_qk = jnp.dot(qc, ktc, preferred_element_type=F32)
        p = (jnp.exp(jnp.where(causal, a_row - g0_t, -jnp.inf)) * s_qk).astype(BF16)
        c_prev = c_refs[h][...]
        q_w = (qc.astype(F32) * jnp.exp(g_prev - g0_t)).astype(BF16)
        nd_refs[h][...] = jnp.dot(jnp.concatenate([q_w, p], axis=1),
                                  jnp.concatenate([c_prev.astype(BF16), v_aug], axis=0),
                                  preferred_element_type=F32)

        kte = (ktc.astype(F32) * jnp.exp(a_row - g_end)).astype(BF16)
        decay = jnp.exp(g_prev - g_end)
        c_refs[h][...] = (jnp.concatenate([decay, decay], axis=1) * c_prev
                          + jnp.dot(kte, v_aug, preferred_element_type=F32))
        return g_end

    def emit(c, h):
        r0 = pl.multiple_of(c * L, L)
        hs = slice(h * dh, (h + 1) * dh)
        em_t = jnp.broadcast_to(em_ref[h:h + 1, pl.ds(pl.multiple_of(t0 + r0, L), L)], (L, L)).T
        nd = nd_refs[h][...]
        h_tilde = nd[:, 0:dh] / jnp.maximum(jnp.abs(nd[:, dh:2 * dh]), em_t)
        hm = _sigmoid(o_ref[pl.ds(r0, L), hs].astype(F32)) * h_tilde
        mu = jnp.dot(hm.astype(BF16), mean_mat, preferred_element_type=F32)
        hc = hm - mu
        var = jnp.dot((hc * hc).astype(BF16), mean_mat, preferred_element_type=F32)
        out_ref[pl.ds(r0, L), hs] = (hc * lax.rsqrt(var + EPS)).astype(BF16)

    zero = jnp.zeros((1, L), F32)
    g_first = tuple(mix(0, h, zero) for h in range(H))

    def body(c, g_carry):
        for h in range(H):
            emit(c - 1, h)
        return tuple(mix(c, h, g_carry[h]) for h in range(H))

    lax.fori_loop(1, S // L, body, g_first, unroll=5)
    for h in range(H):
        emit(S // L - 1, h)


def _mlstm(q, kt, v, o, gatest, weights, row_scales, *, batch, seq):
    T, dm = q.shape
    dh = dm // N_MLSTM_HEADS
    kern = functools.partial(_mlstm_kernel, chunk=MLSTM_CHUNK, n_cast=len(weights))
    tok = lambda w: pl.BlockSpec((seq, w), lambda b: (b, 0))
    rows = pltpu.VMEM((SUBLANES, T), F32)
    bf16_rows = 2 * SUBLANES
    assert all(w.shape[0] % (batch * bf16_rows) == 0 for w in weights)
    w_specs = [pl.BlockSpec((w.shape[0] // batch, w.shape[1]), lambda b: (b, 0)) for w in weights]
    s_specs = [pl.BlockSpec((w.shape[0] // batch, 1), lambda b: (b, 0)) for w in weights]
    scales = [sc.reshape(-1, 1).astype(F32) for sc in row_scales]
    return pl.pallas_call(
        kern,
        grid=(batch,),
        in_specs=[tok(dm), pl.BlockSpec((dm, seq), lambda b: (0, b)), tok(dm), tok(dm),
                  _const_spec((SUBLANES, T))] + w_specs + s_specs,
        out_specs=[tok(dm)] + w_specs,
        out_shape=[jax.ShapeDtypeStruct((T, dm), BF16)] + [jax.ShapeDtypeStruct(w.shape, BF16) for w in weights],
        scratch_shapes=([rows, rows, rows, rows] + [pltpu.VMEM((dh, 2 * dh), F32)] * N_MLSTM_HEADS
                        + [pltpu.VMEM((MLSTM_CHUNK, 2 * dh), F32)] * N_MLSTM_HEADS),
        compiler_params=pltpu.CompilerParams(dimension_semantics=("arbitrary",),
                                             vmem_limit_bytes=VMEM_LIMIT),
        name="mlstm",
    )(q, kt, v, o, gatest, *weights, *scales)


def _s5_prep_kernel(arow_ref, acol_ref, bre_ref, bim_ref, cre_ref, cim_ref, drow_ref,
                    rep_ref, rep2_ref, selrev_ref, pwcol_ref, w1_ref, wout_ref, al_ref, *, chunk):
    L = chunk
    P = S5_STATE
    Hc = S5_GROUP
    R = L * Hc
    hi = lax.Precision.HIGHEST

    def select(x, sel_ref):
        x_hi = x.astype(BF16)
        x_lo = (x - x_hi.astype(F32)).astype(BF16)
        sel = sel_ref[...]
        return (jnp.dot(x_hi, sel, preferred_element_type=F32) + jnp.dot(x_lo, sel, preferred_element_type=F32))

    dt_r = jnp.exp(arow_ref[2:3, :])
    da_re, da_im = dt_r * arow_ref[0:1, :], dt_r * arow_ref[1:2, :]

    def abar_pow(tau):
        mag = jnp.exp(tau * da_re)
        ang = tau * da_im
        return mag * jnp.cos(ang), mag * jnp.sin(ang)

    n_pow = -(-(L + 1) // SUBLANES) * SUBLANES
    er_re, er_im = abar_pow(lax.broadcasted_iota(jnp.int32, (n_pow, 1), 0).astype(F32))
    pad = jnp.zeros((2 * P - n_pow, 2 * P), F32)
    ec_re = jnp.concatenate([er_re, pad], axis=0).T
    ec_im = jnp.concatenate([er_im, pad], axis=0).T

    a_re_c, a_im_c = acol_ref[:, 0:1], acol_ref[:, 1:2]
    ab_re, ab_im = ec_re[:, 1:2], ec_im[:, 1:2]
    den = a_re_c * a_re_c + a_im_c * a_im_c
    nr = ab_re - 1.0
    z_re = (nr * a_re_c + ab_im * a_im_c) / den
    z_im = (ab_im * a_re_c - nr * a_im_c) / den
    bb_re = z_re * bre_ref[...] - z_im * bim_ref[...]
    bb_im = z_re * bim_ref[...] + z_im * bre_ref[...]
    first = lax.broadcasted_iota(jnp.int32, (2 * P, Hc), 0) < P
    bd = lambda b: jnp.concatenate([jnp.where(first, b, 0.0), jnp.where(first, 0.0, b)], axis=1)

    def rows(e, lo):
        return jnp.concatenate([jnp.broadcast_to(e[lo + t:lo + t + 1, :], (Hc, 2 * P)) for t in range(L)], axis=0)

    c_re = jnp.concatenate([cre_ref[...]] * L, axis=0)
    c_im = jnp.concatenate([cim_ref[...]] * L, axis=0)

    def c_times_pow(lo):
        e_re, e_im = rows(er_re, lo), rows(er_im, lo)
        return c_re * e_re - c_im * e_im, c_re * e_im + c_im * e_re

    ce_re, ce_im = c_times_pow(0)
    kflat = (jnp.dot(ce_re, bd(bb_re), preferred_element_type=F32, precision=hi)
             - jnp.dot(ce_im, bd(bb_im), preferred_element_type=F32, precision=hi))
    row = lax.broadcasted_iota(jnp.int32, (R, 2 * Hc), 0)
    col = lax.broadcasted_iota(jnp.int32, (R, 2 * Hc), 1) & (Hc - 1)
    kflat = kflat + jnp.where(row == col, drow_ref[...], 0.0)

    m = select(kflat, rep2_ref)
    blk = (lax.broadcasted_iota(jnp.int32, (R, 2 * R), 1) & (R - 1)) // Hc
    sh = 1
    while sh < L:
        shifted = jnp.concatenate([jnp.zeros((sh * Hc, 2 * R), F32), m[0:R - sh * Hc, :]], axis=0)
        m = jnp.where((blk & sh) != 0, shifted, m)
        sh *= 2

    ev_re, ev_im = select(ec_re, selrev_ref), select(ec_im, selrev_ref)
    bt_re, bt_im = select(bb_re, rep_ref), select(bb_im, rep_ref)
    st_re = (ev_re * bt_re - ev_im * bt_im).astype(BF16)
    st_im = (ev_re * bt_im + ev_im * bt_re).astype(BF16)

    co_re, co_im = c_times_pow(1)
    lane_first = lax.broadcasted_iota(jnp.int32, (R, 2 * P), 1) < P
    for g in range(2):
        w1_ref[g, 0:R, :] = m[:, g * R:(g + 1) * R].astype(BF16)
        w1_ref[g, R:R + P, :] = st_re[g * P:(g + 1) * P, :]
        w1_ref[g, R + P:R + 2 * P, :] = st_im[g * P:(g + 1) * P, :]
        own = lane_first if g == 0 else jnp.logical_not(lane_first)
        wout_ref[g, :, 0:2 * P] = jnp.where(own, co_re, 0.0).astype(BF16)
        wout_ref[g, :, 2 * P:4 * P] = jnp.where(own, -co_im, 0.0).astype(BF16)

    sc_re, sc_im = abar_pow(pwcol_ref[...])
    al_ref[0:SUBLANES, :] = sc_re
    al_ref[SUBLANES:2 * SUBLANES, :] = sc_im


def _s5_prep(a_re, a_im, log_dt, b_re, b_im, c_re, c_im, d_skip, *, chunk, n_chunks):
    G, P = a_re.shape
    Hc = S5_GROUP
    R = chunk * Hc
    assert G % 2 == 0 and 2 * P == LANES
    G2 = G // 2
    ldt = jnp.broadcast_to(log_dt[:, None], (G, P))
    arow = jnp.stack([a_re.reshape(G2, 2 * P), a_im.reshape(G2, 2 * P), ldt.reshape(G2, 2 * P)], axis=1)
    acol = jnp.stack([a_re.reshape(G2, 2 * P), a_im.reshape(G2, 2 * P)], axis=2)
    pair_lanes = lambda c: c.reshape(G2, 2, Hc, P).transpose(0, 2, 1, 3).reshape(G2, Hc, 2 * P)
    drow = d_skip.reshape(G2, 1, 2 * Hc)
    rep_np = (np.arange(R)[None, :] % Hc == np.arange(Hc)[:, None]).astype(np.float32)
    rep = jnp.asarray(rep_np, BF16)
    rep2 = jnp.asarray(np.kron(np.eye(2, dtype=np.float32), rep_np), BF16)
    selrev = jnp.asarray(np.arange(LANES)[:, None] == chunk - 1 - np.arange(R)[None, :] // Hc, BF16)
    n_steps = max(1, (n_chunks - 1).bit_length())
    assert chunk + 1 <= LANES and n_steps <= SUBLANES
    pwcol_np = np.zeros((SUBLANES, 1), np.float32)
    pwcol_np[:n_steps, 0] = chunk * 2.0 ** np.arange(n_steps)
    pwcol = jnp.asarray(pwcol_np)
    kern = functools.partial(_s5_prep_kernel, chunk=chunk)
    pair = lambda *s: pl.BlockSpec((None,) + s, lambda g: (g,) + (0,) * len(s))
    two = lambda *s: pl.BlockSpec((2,) + s, lambda g: (g,) + (0,) * len(s))
    return pl.pallas_call(
        kern,
        grid=(G2,),
        in_specs=[pair(3, 2 * P), pair(2 * P, 2), pair(2 * P, Hc), pair(2 * P, Hc), pair(Hc, 2 * P), pair(Hc, 2 * P),
                  pair(1, 2 * Hc), _const_spec((Hc, R)), _const_spec((2 * Hc, 2 * R)), _const_spec((LANES, R)),
                  _const_spec((SUBLANES, 1))],
        out_specs=[two(R + 2 * P, R), two(R, 4 * P), pair(2 * SUBLANES, 2 * P)],
        out_shape=[jax.ShapeDtypeStruct((G, R + 2 * P, R), BF16),
                   jax.ShapeDtypeStruct((G, R, 4 * P), BF16),
                   jax.ShapeDtypeStruct((G2, 2 * SUBLANES, 2 * P), F32)],
        compiler_params=pltpu.CompilerParams(dimension_semantics=("arbitrary",),
                                             vmem_limit_bytes=VMEM_LIMIT),
        name="s5_prep",
    )(arow, acol, b_re.reshape(G2, 2 * P, Hc), b_im.reshape(G2, 2 * P, Hc), pair_lanes(c_re), pair_lanes(c_im),
      drow, rep, rep2, selrev, pwcol)


def _s5_kernel(*refs, chunk, n_chunks):
    L = chunk
    n_oct = L // SUBLANES
    u_refs, (w1_ref, wout_ref, al_ref, y_ref, ut_ref, yt_ref) = refs[0:n_oct], refs[n_oct:]
    P = S5_STATE
    Hc = S5_GROUP
    R = L * Hc
    C = y_ref.shape[1] // SUBLANES
    n_grp = LANES // Hc

    for s in range(L):
        xt = u_refs[s // SUBLANES][pl.ds(s % SUBLANES, C, stride=SUBLANES), :].astype(BF16).T
        for j in range(n_grp):
            ut_ref[j, s * Hc:(s + 1) * Hc, :] = xt[j * Hc:(j + 1) * Hc, :]

    seg_pos = lax.broadcasted_iota(jnp.int32, (C, 2 * P), 0) & (n_chunks - 1)
    nt = (((1,), (1,)), ((), ()))

    def shifted(x, sh):
        return jnp.where(seg_pos >= sh, pltpu.roll(x, sh, 0), 0.0)

    def group_pair(i, carry):
        j0 = 2 * i
        res = [jnp.dot(w1_ref[j0 + d], ut_ref[j0 + d], preferred_element_type=F32) for d in range(2)]
        x_re = jnp.concatenate([res[0][R:R + P], res[1][R:R + P]], axis=0).T
        x_im = jnp.concatenate([res[0][R + P:R + 2 * P], res[1][R + P:R + 2 * P]], axis=0).T
        sh, k = 1, 0
        while sh < n_chunks:
            a_re = al_ref[i, k:k + 1, :]
            a_im = al_ref[i, SUBLANES + k:SUBLANES + k + 1, :]
            s_re, s_im = shifted(x_re, sh), shifted(x_im, sh)
            x_re, x_im = (x_re + a_re * s_re - a_im * s_im, x_im + a_re * s_im + a_im * s_re)
            sh *= 2
            k += 1
        xprev = jnp.concatenate([shifted(x_re, 1), shifted(x_im, 1)], axis=1).astype(BF16)
        for d in range(2):
            yt_ref[j0 + d] = res[d][0:R, :] + lax.dot_general(wout_ref[j0 + d], xprev, nt,
                                                              preferred_element_type=F32)
        return carry

    lax.fori_loop(0, n_grp // 2, group_pair, 0, unroll=2)

    for t in range(L):
        zt = jnp.concatenate([yt_ref[j, t * Hc:(t + 1) * Hc, :] for j in range(n_grp)], axis=0)
        y_ref[t // SUBLANES, pl.ds(t % SUBLANES, C, stride=SUBLANES), :] = zt.T


def _s5(u, w1, wout, al, *, chunk, n_chunks, n_col_blocks):
    n_oct, t_oct, ds5 = u.shape
    T = n_oct * t_oct
    assert n_oct * SUBLANES == chunk
    G, R1, R = w1.shape
    assert G * S5_GROUP == ds5
    P4 = wout.shape[-1]
    n_grp = LANES // S5_GROUP
    n_q = ds5 // LANES
    tb = T // n_col_blocks
    C = tb // chunk
    kern = functools.partial(_s5_kernel, chunk=chunk, n_chunks=n_chunks)
    grp = lambda *s: pl.BlockSpec((n_grp,) + s, lambda q, i: (q,) + (0,) * len(s))
    octets = [pl.BlockSpec((None, C * SUBLANES, LANES), functools.partial(lambda q, i, o: (o, i, q), o=o))
              for o in range(n_oct)]
    return pl.pallas_call(
        kern,
        grid=(n_q, n_col_blocks),
        in_specs=octets + [grp(R1, R), grp(R, P4),
                           pl.BlockSpec((n_grp // 2, 2 * SUBLANES, P4 // 2), lambda q, i: (q, 0, 0))],
        out_specs=pl.BlockSpec((n_oct, C * SUBLANES, LANES), lambda q, i: (0, i, q)),
        out_shape=jax.ShapeDtypeStruct((n_oct, t_oct, ds5), F32),
        scratch_shapes=[pltpu.VMEM((n_grp, R, C), BF16), pltpu.VMEM((n_grp, R, C), F32)],
        compiler_params=pltpu.CompilerParams(dimension_semantics=("arbitrary", "arbitrary"),
                                             vmem_limit_bytes=BIG_VMEM_LIMIT),
        name="s5",
    )(*([u] * n_oct), w1, wout, al)


def _out_kernel(x_ref, hm_ref, y_ref, wglu_ref, bglu_ref, wo_ref,
                wgate_ref, wup_ref, wdown_ref, gf_ref, out_ref, acc_ref, xn_ref, r_ref, *, ff_chunk, final_norm):
    dm = hm_ref.shape[1]
    n_oct, rows_oct, ds5 = y_ref.shape
    y = jnp.stack([y_ref[oc].reshape(rows_oct // SUBLANES, SUBLANES, ds5) for oc in range(n_oct)], axis=1)
    g = jax.nn.gelu(y.reshape(n_oct * rows_oct, ds5))
    z = jnp.dot(g.astype(BF16), wglu_ref[...], preferred_element_type=F32) + bglu_ref[...]
    gg = g * _sigmoid(z)
    hs = (gg * lax.rsqrt(jnp.mean(gg * gg, axis=-1, keepdims=True) + EPS)).astype(BF16)
    h1 = (x_ref[...] + jnp.dot(hm_ref[...], wo_ref[0:dm, :], preferred_element_type=F32)
          + jnp.dot(hs, wo_ref[dm:, :], preferred_element_type=F32))
    xn_ref[...] = h1.astype(BF16)
    r_ref[...] = jnp.broadcast_to(lax.rsqrt(jnp.mean(h1 * h1, axis=-1, keepdims=True) + EPS), r_ref.shape)
    acc_ref[...] = h1

    def ffn_chunk(c, carry):
        cs = pl.ds(pl.multiple_of(c * ff_chunk, ff_chunk), ff_chunk)
        xn2 = xn_ref[...]
        r = jnp.concatenate([r_ref[...]] * (ff_chunk // LANES), axis=1)
        gate = jnp.dot(xn2, wgate_ref[:, cs], preferred_element_type=F32) * r
        up = jnp.dot(xn2, wup_ref[:, cs], preferred_element_type=F32) * r
        act = (_silu(gate) * up).astype(BF16)
        acc_ref[...] += jnp.dot(act, wdown_ref[cs, :], preferred_element_type=F32)
        return carry

    lax.fori_loop(0, wgate_ref.shape[1] // ff_chunk, ffn_chunk, 0, unroll=4)
    out_ref[...] = _rmsnorm(acc_ref[...], gf_ref[...]) if final_norm else acc_ref[...]


def _out_block(x2, hm, y, wglu, bglu, wo, wgate, wup, wdown, gf, *, tm, ff_chunk, final_norm):
    T, D = x2.shape
    dm = hm.shape[1]
    n_oct, _, ds5 = y.shape
    assert wgate.shape[1] % ff_chunk == 0
    kern = functools.partial(_out_kernel, ff_chunk=ff_chunk, final_norm=final_norm)
    tok = lambda w: pl.BlockSpec((tm, w), lambda i: (i, 0))
    return pl.pallas_call(
        kern,
        grid=(T // tm,),
        in_specs=[tok(D), tok(dm), pl.BlockSpec((n_oct, tm // n_oct, ds5), lambda i: (0, i, 0)),
                  _const_spec(wglu.shape), _const_spec((1, ds5)), _const_spec(wo.shape),
                  _const_spec(wgate.shape), _const_spec(wup.shape), _const_spec(wdown.shape),
                  _const_spec((1, D))],
        out_specs=tok(D),
        out_shape=jax.ShapeDtypeStruct((T, D), F32),
        scratch_shapes=[pltpu.VMEM((tm, D), F32), pltpu.VMEM((tm, D), BF16), pltpu.VMEM((tm, LANES), F32)],
        compiler_params=pltpu.CompilerParams(dimension_semantics=("arbitrary",),
                                             vmem_limit_bytes=BIG_VMEM_LIMIT),
        name="out_block",
    )(x2, hm, y, wglu, bglu, wo, wgate, wup, wdown, gf)


def kernel(x, norm1_g, w_in, if_bias, conv_qk, mlstm_norm_g, a_re, a_im, log_dt, b_re, b_im, c_re, c_im,
           d_skip, w_glu, b_glu, s5_norm_g, w_out, norm2_g, w_gate, w_up, w_down, norm_f_g):
    B, S, D = x.shape
    depth = w_in.shape[0]
    H = N_MLSTM_HEADS
    dm = mlstm_norm_g.shape[1]
    T = B * S
    L5 = S5_CHUNK
    nc5 = S // L5
    row = lambda a: a.reshape(1, -1).astype(F32)

    h = x.reshape(T, D)
    for l in range(depth):
        last = l == depth - 1
        gb = jnp.pad(if_bias[l].astype(F32).reshape(1, 2 * H), ((0, 0), (0, LANES - 2 * H)))

        q, kt, v, o, gatest, u = _inproj(h, row(norm1_g[l]), w_in[l].T.astype(F32), gb, conv_qk[l].astype(F32),
                                         seq=S, tm=min(INPROJ_TM, S), d_mlstm=dm, n_oct=L5 // SUBLANES)
        ones = lambda n: jnp.ones((n,), F32)
        g_mix = jnp.concatenate([mlstm_norm_g[l], s5_norm_g[l]]).astype(F32)
        g_ffn = norm2_g[l].astype(F32)
        hm, wglu, wo, wgate, wup, wdown = _mlstm(
            q, kt, v, o, gatest,
            [w.astype(F32) for w in (w_glu[l], w_out[l], w_gate[l], w_up[l], w_down[l])],
            [ones(w_glu.shape[1]), g_mix, g_ffn, g_ffn, ones(w_down.shape[1])], batch=B, seq=S)

        w1, wout, al = _s5_prep(a_re[l], a_im[l], log_dt[l], b_re[l], b_im[l], c_re[l], c_im[l],
                                d_skip[l], chunk=L5, n_chunks=nc5)
        y = _s5(u, w1, wout, al, chunk=L5, n_chunks=nc5, n_col_blocks=S5_COL_BLOCKS)

        h = _out_block(h, hm, y, wglu, row(b_glu[l]), wo, wgate, wup, wdown, row(norm_f_g),
                       tm=OUT_TM, ff_chunk=FF_CHUNK, final_norm=last)
    return h.reshape(B, S, D)
```

```python
import functools

import jax
import jax.numpy as jnp
from jax import lax
from jax.experimental import pallas as pl
from jax.experimental.pallas import tpu as pltpu

EPS = 1e-6
N_MLSTM_HEADS = 4
CONV_WIDTH = 4
S5_GROUP = 16
S5_STATE = 64

LANES = 128
SUBLANES = 8
MLSTM_CHUNK = 128
S5_CHUNK = 16
S5_COL_BLOCKS = 1
INPROJ_TM = 1024
OUT_TM = 1024
FF_CHUNK = 256
VMEM_LIMIT = 48 * 1024 * 1024
BIG_VMEM_LIMIT = 58 * 1024 * 1024

F32 = jnp.float32
BF16 = jnp.bfloat16


def _rmsnorm(x, g):
    ms = jnp.mean(x * x, axis=-1, keepdims=True)
    return x * lax.rsqrt(ms + EPS) * g


def _sigmoid(x):
    return 0.5 * jnp.tanh(0.5 * x) + 0.5


def _silu(x):
    h = 0.5 * x
    return h + h * jnp.tanh(h)


def _log_sigmoid(x):
    return -(jnp.maximum(-x, 0.0) + jnp.log1p(jnp.exp(-jnp.abs(x))))


def _const_spec(shape):
    return pl.BlockSpec(shape, lambda *_: (0,) * len(shape), pipeline_mode=pl.Buffered(1))


def _inproj_kernel(x_ref, g1_ref, wt_ref, gb_ref, cw_ref,
                   q_ref, kt_ref, v_ref, o_ref, gatest_ref, u_ref,
                   ext_ref, wm_ref, wg_ref, wu_ref, *, tiles_per_seq, d_mlstm, k_scale):
    tm = x_ref.shape[0]
    dm = d_mlstm
    H = N_MLSTM_HEADS
    halo = SUBLANES

    @pl.when(pl.program_id(0) == 0)
    def _():
        g1 = g1_ref[...]
        for c0 in range(0, 4 * dm, dm):
            wm_ref[:, c0:c0 + dm] = (wt_ref[c0:c0 + dm, :] * g1).T.astype(BF16)
        wu_ref[...] = (wt_ref[4 * dm + 2 * H:, :] * g1).T.astype(BF16)
        fill = jnp.zeros((LANES - 2 * H, wt_ref.shape[1]), F32)
        wg_ref[...] = jnp.concatenate([wt_ref[4 * dm:4 * dm + 2 * H, :] * g1, fill], axis=0).T.astype(BF16)

    x = x_ref[...]
    xn = (x * lax.rsqrt(jnp.mean(x * x, axis=-1, keepdims=True) + EPS)).astype(BF16)

    @pl.when(pl.program_id(0) % tiles_per_seq == 0)
    def _():
        ext_ref[0:halo, :] = jnp.zeros((halo, 2 * dm), F32)

    ext_ref[halo:halo + tm, :] = jnp.dot(xn, wm_ref[:, 0:2 * dm], preferred_element_type=F32)
    v = jnp.dot(xn, wm_ref[:, 2 * dm:3 * dm], preferred_element_type=F32)
    o = jnp.dot(xn, wm_ref[:, 3 * dm:4 * dm], preferred_element_type=F32)
    u = jnp.dot(xn, wu_ref[...], preferred_element_type=F32)
    gates = jnp.dot(xn, wg_ref[...], preferred_element_type=F32) + gb_ref[...]
    v_ref[...] = v.astype(BF16)
    o_ref[...] = o.astype(BF16)
    n_oct = u_ref.shape[0]
    u4 = u.reshape(tm // (n_oct * SUBLANES), n_oct, SUBLANES, u.shape[1])
    for oc in range(n_oct):
        u_ref[oc] = u4[:, oc].reshape(tm // n_oct, u.shape[1])
    gatest_ref[...] = gates.T[0:2 * H, :]

    def zero_of(t):
        w = pltpu.bitcast(t[tm - SUBLANES:tm, t.shape[1] - LANES:], jnp.uint32)
        return pltpu.bitcast((w >> 16) >> 16, F32)
    anchor = zero_of(v)

    acc = cw_ref[CONV_WIDTH - 1:CONV_WIDTH, :] * ext_ref[halo:halo + tm, :]
    for j in range(1, CONV_WIDTH):
        acc = acc + cw_ref[CONV_WIDTH - 1 - j:CONV_WIDTH - j, :] * ext_ref[halo - j:halo - j + tm, :]
    ext_ref[0:halo, :] = ext_ref[tm:tm + halo, :]
    qk = _silu(acc) + jnp.tile(anchor, (tm // SUBLANES, 2 * dm // LANES))
    q_ref[...] = qk[:, 0:dm].astype(BF16)
    kt_ref[...] = (qk[:, dm:2 * dm] * k_scale).T.astype(BF16)


def _inproj(x2, g1, w_in_t, gb, cw, *, seq, tm, d_mlstm, n_oct):
    T, D = x2.shape
    dm = d_mlstm
    H = N_MLSTM_HEADS
    assert 2 * H == SUBLANES
    ds5 = w_in_t.shape[0] - 4 * dm - 2 * H
    kern = functools.partial(_inproj_kernel, tiles_per_seq=seq // tm, d_mlstm=dm,
                             k_scale=float((dm // H) ** -0.5))
    tok = lambda w: pl.BlockSpec((tm, w), lambda i: (i, 0))
    return pl.pallas_call(
        kern,
        grid=(T // tm,),
        in_specs=[tok(D), _const_spec((1, D)), _const_spec(w_in_t.shape), _const_spec((1, LANES)),
                  _const_spec(cw.shape)],
        out_specs=[tok(dm), pl.BlockSpec((dm, tm), lambda i: (0, i)), tok(dm), tok(dm),
                   pl.BlockSpec((2 * H, tm), lambda i: (0, i)),
                   pl.BlockSpec((n_oct, tm // n_oct, ds5), lambda i: (0, i, 0))],
        out_shape=[jax.ShapeDtypeStruct((T, dm), BF16), jax.ShapeDtypeStruct((dm, T), BF16),
                   jax.ShapeDtypeStruct((T, dm), BF16), jax.ShapeDtypeStruct((T, dm), BF16),
                   jax.ShapeDtypeStruct((2 * H, T), F32),
                   jax.ShapeDtypeStruct((n_oct, T // n_oct, ds5), F32)],
        scratch_shapes=[pltpu.VMEM((tm + 2 * SUBLANES, 2 * dm), F32), pltpu.VMEM((D, 4 * dm), BF16),
                        pltpu.VMEM((D, LANES), BF16), pltpu.VMEM((D, ds5), BF16)],
        compiler_params=pltpu.CompilerParams(dimension_semantics=("arbitrary",),
                                             vmem_limit_bytes=VMEM_LIMIT),
        name="inproj",
    )(x2, g1, w_in_t, gb, cw)


def _lane_scan(x, op, identity, seg):
    pos = lax.broadcasted_iota(jnp.int32, x.shape, x.ndim - 1) & (seg - 1)
    sh = 1
    while sh < seg:
        x = op(x, jnp.where(pos >= sh, pltpu.roll(x, sh, x.ndim - 1), identity))
        sh *= 2
    return x


def _mlstm_kernel(q_ref, kt_ref, v_ref, o_ref, gatest_ref, *refs, chunk, n_cast):
    S = q_ref.shape[0]
    H = N_MLSTM_HEADS
    dh = q_ref.shape[1] // H
    L = chunk
    assert L == dh == LANES
    w32_refs, ws_refs = refs[0:n_cast], refs[n_cast:2 * n_cast]
    out_ref, w16_refs = refs[2 * n_cast], refs[2 * n_cast + 1:3 * n_cast + 1]
    a_ref, g0_ref, nm_ref, gend_ref = refs[3 * n_cast + 1:3 * n_cast + 5]
    c_refs, nd_refs = refs[3 * n_cast + 5:3 * n_cast + 5 + H], refs[3 * n_cast + 5 + H:]

    for w32_ref, ws_ref, w16_ref in zip(w32_refs, ws_refs, w16_refs):
        w16_ref[...] = (w32_ref[...] * ws_ref[...]).astype(BF16)

    @pl.when(pl.program_id(0) == 0)
    def _():
        gates = gatest_ref[...]
        f_cum = pltpu.roll(_lane_scan(_log_sigmoid(gates), jnp.add, 0.0, S), H, 0)
        a = gates - f_cum
        g0 = jnp.maximum(_lane_scan(a, jnp.maximum, -jnp.inf, S), 0.0)
        a_ref[...] = a
        g0_ref[...] = g0
        nm_ref[...] = -(f_cum + g0)
        for c in range(gatest_ref.shape[1] // L):
            gend_ref[:, c * L:(c + 1) * L] = jnp.broadcast_to(g0[:, (c + 1) * L - 1:(c + 1) * L], (SUBLANES, L))

    t0 = pl.program_id(0) * S

    for c_ref in c_refs:
        c_ref[...] = jnp.zeros(c_ref.shape, F32)
    causal = (lax.broadcasted_iota(jnp.int32, (L, L), 0) >= lax.broadcasted_iota(jnp.int32, (L, L), 1))
    ones_blk = jnp.ones((L, dh), BF16)
    mean_mat = jnp.full((dh, dh), 1.0 / dh, BF16)

    def mix(c, h, g_prev):
        r0 = pl.multiple_of(c * L, L)
        hs = slice(h * dh, (h + 1) * dh)
        qc = q_ref[pl.ds(r0, L), hs]
        ktc = kt_ref[hs, pl.ds(r0, L)]
        v_aug = jnp.concatenate([v_ref[pl.ds(r0, L), hs], ones_blk], axis=1)
        g0l = pl.ds(pl.multiple_of(t0 + r0, L), L)
        a_row = a_ref[h:h + 1, g0l]
        g_end = gend_ref[h:h + 1, g0l]
        g0_t = jnp.broadcast_to(g0_ref[h:h + 1, g0l], (L, L)).T

        s_qk = jnp.dot(qc, ktc, preferred_element_type=F32)
        p = (jnp.exp(jnp.where(causal, a_row - g0_t, -jnp.inf)) * s_qk).astype(BF16)
        c_prev = c_refs[h][...]
        q_w = (qc.astype(F32) * jnp.exp(g_prev - g0_t)).astype(BF16)
        nd_refs[h][...] = jnp.dot(jnp.concatenate([q_w, p], axis=1),
                                  jnp.concatenate([c_prev.astype(BF16), v_aug], axis=0),
                                  preferred_element_type=F32)

        kte = (ktc.astype(F32) * jnp.exp(a_row - g_end)).astype(BF16)
        decay = jnp.exp(g_prev - g_end)
        c_refs[h][...] = (jnp.concatenate([decay, decay], axis=1) * c_prev
                          + jnp.dot(kte, v_aug, preferred_element_type=F32))
        return g_end

    def emit(c, h):
        r0 = pl.multiple_of(c * L, L)
        hs = slice(h * dh, (h + 1) * dh)
        nm_t = jnp.broadcast_to(nm_ref[h:h + 1, pl.ds(pl.multiple_of(t0 + r0, L), L)], (L, L)).T
        nd = nd_refs[h][...]
        h_tilde = nd[:, 0:dh] / jnp.maximum(jnp.abs(nd[:, dh:2 * dh]), jnp.exp(nm_t))
        hm = _sigmoid(o_ref[pl.ds(r0, L), hs].astype(F32)) * h_tilde
        mu = jnp.dot(hm.astype(BF16), mean_mat, preferred_element_type=F32)
        hc = hm - mu
        var = jnp.dot((hc * hc).astype(BF16), mean_mat, preferred_element_type=F32)
        out_ref[pl.ds(r0, L), hs] = (hc * lax.rsqrt(var + EPS)).astype(BF16)

    zero = jnp.zeros((1, L), F32)
    g_first = tuple(mix(0, h, zero) for h in range(H))

    def body(c, g_carry):
        for h in range(H):
            emit(c - 1, h)
        return tuple(mix(c, h, g_carry[h]) for h in range(H))

    lax.fori_loop(1, S // L, body, g_first, unroll=5)
    for h in range(H):
        emit(S // L - 1, h)


def _mlstm(q, kt, v, o, gatest, weights, row_scales, *, batch, seq):
    T, dm = q.shape
    dh = dm // N_MLSTM_HEADS
    kern = functools.partial(_mlstm_kernel, chunk=MLSTM_CHUNK, n_cast=len(weights))
    tok = lambda w: pl.BlockSpec((seq, w), lambda b: (b, 0))
    rows = pltpu.VMEM((SUBLANES, T), F32)
    bf16_rows = 2 * SUBLANES
    assert all(w.shape[0] % (batch * bf16_rows) == 0 for w in weights)
    w_specs = [pl.BlockSpec((w.shape[0] // batch, w.shape[1]), lambda b: (b, 0)) for w in weights]
    s_specs = [pl.BlockSpec((w.shape[0] // batch, 1), lambda b: (b, 0)) for w in weights]
    scales = [sc.reshape(-1, 1).astype(F32) for sc in row_scales]
    return pl.pallas_call(
        kern,
        grid=(batch,),
        in_specs=[tok(dm), pl.BlockSpec((dm, seq), lambda b: (0, b)), tok(dm), tok(dm),
                  _const_spec((SUBLANES, T))] + w_specs + s_specs,
        out_specs=[tok(dm)] + w_specs,
        out_shape=[jax.ShapeDtypeStruct((T, dm), BF16)] + [jax.ShapeDtypeStruct(w.shape, BF16) for w in weights],
        scratch_shapes=([rows, rows, rows, rows] + [pltpu.VMEM((dh, 2 * dh), F32)] * N_MLSTM_HEADS
                        + [pltpu.VMEM((MLSTM_CHUNK, 2 * dh), F32)] * N_MLSTM_HEADS),
        compiler_params=pltpu.CompilerParams(dimension_semantics=("arbitrary",),
                                             vmem_limit_bytes=VMEM_LIMIT),
        name="mlstm",
    )(q, kt, v, o, gatest, *weights, *scales)


def _s5_prep_kernel(arow_ref, acol_ref, bre_ref, bim_ref, cre_ref, cim_ref, drow_ref,
                    rep_ref, rep2_ref, selrev_ref, pwcol_ref, w1_ref, wout_ref, al_ref, *, chunk):
    L = chunk
    P = S5_STATE
    Hc = S5_GROUP
    R = L * Hc
    hi = lax.Precision.HIGHEST

    def select(x, sel_ref):
        x_hi = x.astype(BF16)
        x_lo = (x - x_hi.astype(F32)).astype(BF16)
        sel = sel_ref[...]
        return (jnp.dot(x_hi, sel, preferred_element_type=F32) + jnp.dot(x_lo, sel, preferred_element_type=F32))

    dt_r = jnp.exp(arow_ref[2:3, :])
    da_re, da_im = dt_r * arow_ref[0:1, :], dt_r * arow_ref[1:2, :]

    def abar_pow(tau):
        mag = jnp.exp(tau * da_re)
        ang = tau * da_im
        return mag * jnp.cos(ang), mag * jnp.sin(ang)

    n_pow = -(-(L + 1) // SUBLANES) * SUBLANES
    er_re, er_im = abar_pow(lax.broadcasted_iota(jnp.int32, (n_pow, 1), 0).astype(F32))
    pad = jnp.zeros((2 * P - n_pow, 2 * P), F32)
    ec_re = jnp.concatenate([er_re, pad], axis=0).T
    ec_im = jnp.concatenate([er_im, pad], axis=0).T

    a_re_c, a_im_c = acol_ref[:, 0:1], acol_ref[:, 1:2]
    ab_re, ab_im = ec_re[:, 1:2], ec_im[:, 1:2]
    den = a_re_c * a_re_c + a_im_c * a_im_c
    nr = ab_re - 1.0
    z_re = (nr * a_re_c + ab_im * a_im_c) / den
    z_im = (ab_im * a_re_c - nr * a_im_c) / den
    bb_re = z_re * bre_ref[...] - z_im * bim_ref[...]
    bb_im = z_re * bim_ref[...] + z_im * bre_ref[...]
    first = lax.broadcasted_iota(jnp.int32, (2 * P, Hc), 0) < P
    bd = lambda b: jnp.concatenate([jnp.where(first, b, 0.0), jnp.where(first, 0.0, b)], axis=1)

    def rows(e, lo):
        return jnp.concatenate([jnp.broadcast_to(e[lo + t:lo + t + 1, :], (Hc, 2 * P)) for t in range(L)], axis=0)

    c_re = jnp.concatenate([cre_ref[...]] * L, axis=0)
    c_im = jnp.concatenate([cim_ref[...]] * L, axis=0)

    def c_times_pow(lo):
        e_re, e_im = rows(er_re, lo), rows(er_im, lo)
        return c_re * e_re - c_im * e_im, c_re * e_im + c_im * e_re

    ce_re, ce_im = c_times_pow(0)
    kflat = (jnp.dot(ce_re, bd(bb_re), preferred_element_type=F32, precision=hi)
             - jnp.dot(ce_im, bd(bb_im), preferred_element_type=F32, precision=hi))
    row = lax.broadcasted_iota(jnp.int32, (R, 2 * Hc), 0)
    col = lax.broadcasted_iota(jnp.int32, (R, 2 * Hc), 1) & (Hc - 1)
    kflat = kflat + jnp.where(row == col, drow_ref[...], 0.0)

    m = select(kflat, rep2_ref)
    blk = (lax.broadcasted_iota(jnp.int32, (R, 2 * R), 1) & (R - 1)) // Hc
    sh = 1
    while sh < L:
        shifted = jnp.concatenate([jnp.zeros((sh * Hc, 2 * R), F32), m[0:R - sh * Hc, :]], axis=0)
        m = jnp.where((blk & sh) != 0, shifted, m)
        sh *= 2

    ev_re, ev_im = select(ec_re, selrev_ref), select(ec_im, selrev_ref)
    bt_re, bt_im = select(bb_re, rep_ref), select(bb_im, rep_ref)
    st_re = (ev_re * bt_re - ev_im * bt_im).astype(BF16)
    st_im = (ev_re * bt_im + ev_im * bt_re).astype(BF16)

    co_re, co_im = c_times_pow(1)
    lane_first = lax.broadcasted_iota(jnp.int32, (R, 2 * P), 1) < P
    for g in range(2):
        w1_ref[g, 0:R, :] = m[:, g * R:(g + 1) * R].astype(BF16)
        w1_ref[g, R:R + P, :] = st_re[g * P:(g + 1) * P, :]
        w1_ref[g, R + P:R + 2 * P, :] = st_im[g * P:(g + 1) * P, :]
        own = lane_first if g == 0 else jnp.logical_not(lane_first)
        wout_ref[g, :, 0:2 * P] = jnp.where(own, co_re, 0.0).astype(BF16)
        wout_ref[g, :, 2 * P:4 * P] = jnp.where(own, -co_im, 0.0).astype(BF16)

    sc_re, sc_im = abar_pow(pwcol_ref[...])
    al_ref[0:SUBLANES, :] = sc_re
    al_ref[SUBLANES:2 * SUBLANES, :] = sc_im


def _s5_prep(a_re, a_im, log_dt, b_re, b_im, c_re, c_im, d_skip, *, chunk, n_chunks):
    G, P = a_re.shape
    Hc = S5_GROUP
    R = chunk * Hc
    assert G % 2 == 0 and 2 * P == LANES
    G2 = G // 2
    ldt = jnp.broadcast_to(log_dt[:, None], (G, P))
    arow = jnp.stack([a_re.reshape(G2, 2 * P), a_im.reshape(G2, 2 * P), ldt.reshape(G2, 2 * P)], axis=1)
    acol = jnp.stack([a_re.reshape(G2, 2 * P), a_im.reshape(G2, 2 * P)], axis=2)
    pair_lanes = lambda c: c.reshape(G2, 2, Hc, P).transpose(0, 2, 1, 3).reshape(G2, Hc, 2 * P)
    drow = d_skip.reshape(G2, 1, 2 * Hc)
    rep = (jnp.arange(R)[None, :] % Hc == jnp.arange(Hc)[:, None]).astype(BF16)
    rep2 = jnp.kron(jnp.eye(2, dtype=BF16), rep)
    selrev = (jnp.arange(LANES)[:, None] == chunk - 1 - jnp.arange(R)[None, :] // Hc).astype(BF16)
    n_steps = max(1, (n_chunks - 1).bit_length())
    assert chunk + 1 <= LANES and n_steps <= SUBLANES
    pwcol = jnp.zeros((SUBLANES, 1), F32).at[:n_steps, 0].set(chunk * 2.0 ** jnp.arange(n_steps))
    kern = functools.partial(_s5_prep_kernel, chunk=chunk)
    pair = lambda *s: pl.BlockSpec((None,) + s, lambda g: (g,) + (0,) * len(s))
    two = lambda *s: pl.BlockSpec((2,) + s, lambda g: (g,) + (0,) * len(s))
    return pl.pallas_call(
        kern,
        grid=(G2,),
        in_specs=[pair(3, 2 * P), pair(2 * P, 2), pair(2 * P, Hc), pair(2 * P, Hc), pair(Hc, 2 * P), pair(Hc, 2 * P),
                  pair(1, 2 * Hc), _const_spec((Hc, R)), _const_spec((2 * Hc, 2 * R)), _const_spec((LANES, R)),
                  _const_spec((SUBLANES, 1))],
        out_specs=[two(R + 2 * P, R), two(R, 4 * P), pair(2 * SUBLANES, 2 * P)],
        out_shape=[jax.ShapeDtypeStruct((G, R + 2 * P, R), BF16),
                   jax.ShapeDtypeStruct((G, R, 4 * P), BF16),
                   jax.ShapeDtypeStruct((G2, 2 * SUBLANES, 2 * P), F32)],
        compiler_params=pltpu.CompilerParams(dimension_semantics=("arbitrary",),
                                             vmem_limit_bytes=VMEM_LIMIT),
        name="s5_prep",
    )(arow, acol, b_re.reshape(G2, 2 * P, Hc), b_im.reshape(G2, 2 * P, Hc), pair_lanes(c_re), pair_lanes(c_im),
      drow, rep, rep2, selrev, pwcol)


def _s5_kernel(*refs, chunk, n_chunks):
    L = chunk
    n_oct = L // SUBLANES
    u_refs, (w1_ref, wout_ref, al_ref, y_ref, ut_ref, yt_ref) = refs[0:n_oct], refs[n_oct:]
    P = S5_STATE
    Hc = S5_GROUP
    R = L * Hc
    C = y_ref.shape[1] // SUBLANES
    n_grp = LANES // Hc

    for s in range(L):
        xt = u_refs[s // SUBLANES][pl.ds(s % SUBLANES, C, stride=SUBLANES), :].astype(BF16).T
        for j in range(n_grp):
            ut_ref[j, s * Hc:(s + 1) * Hc, :] = xt[j * Hc:(j + 1) * Hc, :]

    seg_pos = lax.broadcasted_iota(jnp.int32, (C, 2 * P), 0) & (n_chunks - 1)
    nt = (((1,), (1,)), ((), ()))

    def shifted(x, sh):
        return jnp.where(seg_pos >= sh, pltpu.roll(x, sh, 0), 0.0)

    def group_pair(i, carry):
        j0 = 2 * i
        res = [jnp.dot(w1_ref[j0 + d], ut_ref[j0 + d], preferred_element_type=F32) for d in range(2)]
        x_re = jnp.concatenate([res[0][R:R + P], res[1][R:R + P]], axis=0).T
        x_im = jnp.concatenate([res[0][R + P:R + 2 * P], res[1][R + P:R + 2 * P]], axis=0).T
        sh, k = 1, 0
        while sh < n_chunks:
            a_re = al_ref[i, k:k + 1, :]
            a_im = al_ref[i, SUBLANES + k:SUBLANES + k + 1, :]
            s_re, s_im = shifted(x_re, sh), shifted(x_im, sh)
            x_re, x_im = (x_re + a_re * s_re - a_im * s_im, x_im + a_re * s_im + a_im * s_re)
            sh *= 2
            k += 1
        xprev = jnp.concatenate([shifted(x_re, 1), shifted(x_im, 1)], axis=1).astype(BF16)
        for d in range(2):
            yt_ref[j0 + d] = res[d][0:R, :] + lax.dot_general(wout_ref[j0 + d], xprev, nt,
                                                              preferred_element_type=F32)
        return carry

    lax.fori_loop(0, n_grp // 2, group_pair, 0, unroll=2)

    for t in range(L):
        zt = jnp.concatenate([yt_ref[j, t * Hc:(t + 1) * Hc, :] for j in range(n_grp)], axis=0)
        y_ref[t // SUBLANES, pl.ds(t % SUBLANES, C, stride=SUBLANES), :] = zt.T


def _s5(u, w1, wout, al, *, chunk, n_chunks, n_col_blocks):
    n_oct, t_oct, ds5 = u.shape
    T = n_oct * t_oct
    assert n_oct * SUBLANES == chunk
    G, R1, R = w1.shape
    P4 = wout.shape[-1]
    n_grp = LANES // S5_GROUP
    n_q = ds5 // LANES
    tb = T // n_col_blocks
    C = tb // chunk
    kern = functools.partial(_s5_kernel, chunk=chunk, n_chunks=n_chunks)
    grp = lambda *s: pl.BlockSpec((n_grp,) + s, lambda q, i: (q,) + (0,) * len(s))
    octets = [pl.BlockSpec((None, C * SUBLANES, LANES), functools.partial(lambda q, i, o: (o, i, q), o=o))
              for o in range(n_oct)]
    return pl.pallas_call(
        kern,
        grid=(n_q, n_col_blocks),
        in_specs=octets + [grp(R1, R), grp(R, P4),
                           pl.BlockSpec((n_grp // 2, 2 * SUBLANES, P4 // 2), lambda q, i: (q, 0, 0))],
        out_specs=pl.BlockSpec((n_oct, C * SUBLANES, LANES), lambda q, i: (0, i, q)),
        out_shape=jax.ShapeDtypeStruct((n_oct, t_oct, ds5), F32),
        scratch_shapes=[pltpu.VMEM((n_grp, R, C), BF16), pltpu.VMEM((n_grp, R, C), F32)],
        compiler_params=pltpu.CompilerParams(dimension_semantics=("arbitrary", "arbitrary"),
                                             vmem_limit_bytes=BIG_VMEM_LIMIT),
        name="s5",
    )(*([u] * n_oct), w1, wout, al)


def _out_kernel(x_ref, hm_ref, y_ref, wglu_ref, bglu_ref, wo_ref,
                wgate_ref, wup_ref, wdown_ref, gf_ref, out_ref, acc_ref, xn_ref, *, ff_chunk, final_norm):
    dm = hm_ref.shape[1]
    n_oct, rows_oct, ds5 = y_ref.shape
    y = jnp.stack([y_ref[oc].reshape(rows_oct // SUBLANES, SUBLANES, ds5) for oc in range(n_oct)], axis=1)
    g = jax.nn.gelu(y.reshape(n_oct * rows_oct, ds5))
    z = jnp.dot(g.astype(BF16), wglu_ref[...], preferred_element_type=F32) + bglu_ref[...]
    gg = g * _sigmoid(z)
    hs = (gg * lax.rsqrt(jnp.mean(gg * gg, axis=-1, keepdims=True) + EPS)).astype(BF16)
    h1 = (x_ref[...] + jnp.dot(hm_ref[...], wo_ref[0:dm, :], preferred_element_type=F32)
          + jnp.dot(hs, wo_ref[dm:, :], preferred_element_type=F32))
    xn_ref[...] = (h1 * lax.rsqrt(jnp.mean(h1 * h1, axis=-1, keepdims=True) + EPS)).astype(BF16)
    acc_ref[...] = h1

    def ffn_chunk(c, carry):
        cs = pl.ds(pl.multiple_of(c * ff_chunk, ff_chunk), ff_chunk)
        xn2 = xn_ref[...]
        gate = jnp.dot(xn2, wgate_ref[:, cs], preferred_element_type=F32)
        up = jnp.dot(xn2, wup_ref[:, cs], preferred_element_type=F32)
        act = (_silu(gate) * up).astype(BF16)
        acc_ref[...] += jnp.dot(act, wdown_ref[cs, :], preferred_element_type=F32)
        return carry

    lax.fori_loop(0, wgate_ref.shape[1] // ff_chunk, ffn_chunk, 0, unroll=5)
    out_ref[...] = _rmsnorm(acc_ref[...], gf_ref[...]) if final_norm else acc_ref[...]


def _out_block(x2, hm, y, wglu, bglu, wo, wgate, wup, wdown, gf, *, tm, ff_chunk, final_norm):
    T, D = x2.shape
    dm = hm.shape[1]
    n_oct, _, ds5 = y.shape
    assert wgate.shape[1] % ff_chunk == 0
    kern = functools.partial(_out_kernel, ff_chunk=ff_chunk, final_norm=final_norm)
    tok = lambda w: pl.BlockSpec((tm, w), lambda i: (i, 0))
    return pl.pallas_call(
        kern,
        grid=(T // tm,),
        in_specs=[tok(D), tok(dm), pl.BlockSpec((n_oct, tm // n_oct, ds5), lambda i: (0, i, 0)),
                  _const_spec(wglu.shape), _const_spec((1, ds5)), _const_spec(wo.shape),
                  _const_spec(wgate.shape), _const_spec(wup.shape), _const_spec(wdown.shape),
                  _const_spec((1, D))],
        out_specs=tok(D),
        out_shape=jax.ShapeDtypeStruct((T, D), F32),
        scratch_shapes=[pltpu.VMEM((tm, D), F32), pltpu.VMEM((tm, D), BF16)],
        compiler_params=pltpu.CompilerParams(dimension_semantics=("arbitrary",),
                                             vmem_limit_bytes=BIG_VMEM_LIMIT),
        name="out_block",
    )(x2, hm, y, wglu, bglu, wo, wgate, wup, wdown, gf)


def kernel(x, norm1_g, w_in, if_bias, conv_qk, mlstm_norm_g, a_re, a_im, log_dt, b_re, b_im, c_re, c_im,
           d_skip, w_glu, b_glu, s5_norm_g, w_out, norm2_g, w_gate, w_up, w_down, norm_f_g):
    B, S, D = x.shape
    depth = w_in.shape[0]
    H = N_MLSTM_HEADS
    dm = mlstm_norm_g.shape[1]
    T = B * S
    L5 = S5_CHUNK
    nc5 = S // L5
    row = lambda a: a.reshape(1, -1).astype(F32)

    h = x.reshape(T, D)
    for l in range(depth):
        last = l == depth - 1
        gb = jnp.zeros((1, LANES), F32).at[0, 0:2 * H].set(if_bias[l])

        q, kt, v, o, gatest, u = _inproj(h, row(norm1_g[l]), w_in[l].T.astype(F32), gb, conv_qk[l].astype(F32),
                                         seq=S, tm=min(INPROJ_TM, S), d_mlstm=dm, n_oct=L5 // SUBLANES)
        ones = lambda n: jnp.ones((n,), F32)
        g_mix = jnp.concatenate([mlstm_norm_g[l], s5_norm_g[l]]).astype(F32)
        g_ffn = norm2_g[l].astype(F32)
        hm, wglu, wo, wgate, wup, wdown = _mlstm(
            q, kt, v, o, gatest,
            [w.astype(F32) for w in (w_glu[l], w_out[l], w_gate[l], w_up[l], w_down[l])],
            [ones(w_glu.shape[1]), g_mix, g_ffn, g_ffn, ones(w_down.shape[1])], batch=B, seq=S)

        w1, wout, al = _s5_prep(a_re[l], a_im[l], log_dt[l], b_re[l], b_im[l], c_re[l], c_im[l],
                                d_skip[l], chunk=L5, n_chunks=nc5)
        y = _s5(u, w1, wout, al, chunk=L5, n_chunks=nc5, n_col_blocks=S5_COL_BLOCKS)

        h = _out_block(h, hm, y, wglu, row(b_glu[l]), wo, wgate, wup, wdown, row(norm_f_g),
                       tm=OUT_TM, ff_chunk=FF_CHUNK, final_norm=last)
    return h.reshape(B, S, D)
```

```python
import functools
import math

import jax
import jax.numpy as jnp
from jax import lax
from jax.experimental import pallas as pl
from jax.experimental.pallas import tpu as pltpu

EPS = 1e-6
N_MLSTM_HEADS = 4
CONV_WIDTH = 4
S5_GROUP = 16
S5_STATE = 64

LANES = 128
SUBLANES = 8
MLSTM_CHUNK = 128
S5_CHUNK = 16
S5_COL_BLOCKS = 1
S5_PREP_PAIRS = 4
INPROJ_TM = 1024
OUT_TM = 1024
FF_CHUNK = 256
VMEM_LIMIT = 48 * 1024 * 1024
BIG_VMEM_LIMIT = 58 * 1024 * 1024

F32 = jnp.float32
BF16 = jnp.bfloat16


def _rmsnorm(x, g):
    ms = jnp.mean(x * x, axis=-1, keepdims=True)
    return x * lax.rsqrt(ms + EPS) * g


def _sigmoid(x):
    return 0.5 * jnp.tanh(0.5 * x) + 0.5


def _silu(x):
    h = 0.5 * x
    return h + h * jnp.tanh(h)


def _log_sigmoid(x):
    return -(jnp.maximum(-x, 0.0) + jnp.log1p(jnp.exp(-jnp.abs(x))))


def _const_spec(shape):
    return pl.BlockSpec(shape, lambda *_: (0,) * len(shape), pipeline_mode=pl.Buffered(1))


def _inproj_kernel(x_ref, g1_ref, wt_ref, gb_ref, cw_ref,
                   q_ref, kt_ref, v_ref, o_ref, gatest_ref, u_ref,
                   ext_ref, wm_ref, wg_ref, wu_ref, *, tiles_per_seq, d_mlstm, k_scale):
    tm = x_ref.shape[0]
    dm = d_mlstm
    H = N_MLSTM_HEADS
    halo = SUBLANES

    @pl.when(pl.program_id(0) == 0)
    def _():
        g1 = g1_ref[...]
        for c0 in range(0, 4 * dm, dm):
            wm_ref[:, c0:c0 + dm] = (wt_ref[c0:c0 + dm, :] * g1).T.astype(BF16)
        wu_ref[...] = (wt_ref[4 * dm + 2 * H:, :] * g1).T.astype(BF16)
        fill = jnp.zeros((LANES - 2 * H, wt_ref.shape[1]), F32)
        wg_ref[...] = jnp.concatenate([wt_ref[4 * dm:4 * dm + 2 * H, :] * g1, fill], axis=0).T.astype(BF16)

    x = x_ref[...]
    xn = (x * lax.rsqrt(jnp.mean(x * x, axis=-1, keepdims=True) + EPS)).astype(BF16)

    @pl.when(pl.program_id(0) % tiles_per_seq == 0)
    def _():
        ext_ref[0:halo, :] = jnp.zeros((halo, 2 * dm), F32)

    ext_ref[halo:halo + tm, :] = jnp.dot(xn, wm_ref[:, 0:2 * dm], preferred_element_type=F32)
    v = jnp.dot(xn, wm_ref[:, 2 * dm:3 * dm], preferred_element_type=F32)
    o = jnp.dot(xn, wm_ref[:, 3 * dm:4 * dm], preferred_element_type=F32)
    u = jnp.dot(xn, wu_ref[...], preferred_element_type=F32)
    gates = jnp.dot(xn, wg_ref[...], preferred_element_type=F32) + gb_ref[...]
    v_ref[...] = v.astype(BF16)
    o_ref[...] = o.astype(BF16)
    n_oct = u_ref.shape[0]
    u4 = u.reshape(tm // (n_oct * SUBLANES), n_oct, SUBLANES, u.shape[1])
    for oc in range(n_oct):
        u_ref[oc] = u4[:, oc].reshape(tm // n_oct, u.shape[1])
    gatest_ref[...] = gates.T[0:2 * H, :]

    def zero_of(t):
        w = pltpu.bitcast(t[tm - SUBLANES:tm, t.shape[1] - LANES:], jnp.uint32)
        return pltpu.bitcast((w >> 16) >> 16, F32)
    anchor = zero_of(v)

    acc = cw_ref[CONV_WIDTH - 1:CONV_WIDTH, :] * ext_ref[halo:halo + tm, :]
    for j in range(1, CONV_WIDTH):
        acc = acc + cw_ref[CONV_WIDTH - 1 - j:CONV_WIDTH - j, :] * ext_ref[halo - j:halo - j + tm, :]
    ext_ref[0:halo, :] = ext_ref[tm:tm + halo, :]
    qk = _silu(acc) + jnp.tile(anchor, (tm // SUBLANES, 2 * dm // LANES))
    q_ref[...] = qk[:, 0:dm].astype(BF16)
    kt_ref[...] = (qk[:, dm:2 * dm] * k_scale).T.astype(BF16)


def _inproj(x2, g1, w_in_t, gb, cw, *, seq, tm, d_mlstm, n_oct):
    T, D = x2.shape
    dm = d_mlstm
    H = N_MLSTM_HEADS
    assert 2 * H == SUBLANES
    ds5 = w_in_t.shape[0] - 4 * dm - 2 * H
    kern = functools.partial(_inproj_kernel, tiles_per_seq=seq // tm, d_mlstm=dm,
                             k_scale=float((dm // H) ** -0.5))
    tok = lambda w: pl.BlockSpec((tm, w), lambda i: (i, 0))
    return pl.pallas_call(
        kern,
        grid=(T // tm,),
        in_specs=[tok(D), _const_spec((1, D)), _const_spec(w_in_t.shape), _const_spec((1, LANES)),
                  _const_spec(cw.shape)],
        out_specs=[tok(dm), pl.BlockSpec((dm, tm), lambda i: (0, i)), tok(dm), tok(dm),
                   pl.BlockSpec((2 * H, tm), lambda i: (0, i)),
                   pl.BlockSpec((n_oct, tm // n_oct, ds5), lambda i: (0, i, 0))],
        out_shape=[jax.ShapeDtypeStruct((T, dm), BF16), jax.ShapeDtypeStruct((dm, T), BF16),
                   jax.ShapeDtypeStruct((T, dm), BF16), jax.ShapeDtypeStruct((T, dm), BF16),
                   jax.ShapeDtypeStruct((2 * H, T), F32),
                   jax.ShapeDtypeStruct((n_oct, T // n_oct, ds5), F32)],
        scratch_shapes=[pltpu.VMEM((tm + 2 * SUBLANES, 2 * dm), F32), pltpu.VMEM((D, 4 * dm), BF16),
                        pltpu.VMEM((D, LANES), BF16), pltpu.VMEM((D, ds5), BF16)],
        compiler_params=pltpu.CompilerParams(dimension_semantics=("arbitrary",),
                                             vmem_limit_bytes=VMEM_LIMIT),
        name="inproj",
    )(x2, g1, w_in_t, gb, cw)


def _lane_scan(x, op, identity, seg):
    pos = lax.broadcasted_iota(jnp.int32, x.shape, x.ndim - 1) & (seg - 1)
    sh = 1
    while sh < seg:
        x = op(x, jnp.where(pos >= sh, pltpu.roll(x, sh, x.ndim - 1), identity))
        sh *= 2
    return x


def _mlstm_kernel(q_ref, kt_ref, v_ref, o_ref, gatest_ref, *refs, chunk, n_cast):
    S = q_ref.shape[0]
    H = N_MLSTM_HEADS
    dh = q_ref.shape[1] // H
    L = chunk
    assert L == dh == LANES
    w32_refs, ws_refs = refs[0:n_cast], refs[n_cast:2 * n_cast]
    out_ref, w16_refs = refs[2 * n_cast], refs[2 * n_cast + 1:3 * n_cast + 1]
    a_ref, g0_ref, nm_ref, gend_ref = refs[3 * n_cast + 1:3 * n_cast + 5]
    c_refs, nd_refs = refs[3 * n_cast + 5:3 * n_cast + 5 + H], refs[3 * n_cast + 5 + H:]

    for w32_ref, ws_ref, w16_ref in zip(w32_refs, ws_refs, w16_refs):
        w16_ref[...] = (w32_ref[...] * ws_ref[...]).astype(BF16)

    @pl.when(pl.program_id(0) == 0)
    def _():
        gates = gatest_ref[...]
        f_cum = pltpu.roll(_lane_scan(_log_sigmoid(gates), jnp.add, 0.0, S), H, 0)
        a = gates - f_cum
        g0 = jnp.maximum(_lane_scan(a, jnp.maximum, -jnp.inf, S), 0.0)
        a_ref[...] = a
        g0_ref[...] = g0
        nm_ref[...] = -(f_cum + g0)
        for c in range(gatest_ref.shape[1] // L):
            gend_ref[:, c * L:(c + 1) * L] = jnp.broadcast_to(g0[:, (c + 1) * L - 1:(c + 1) * L], (SUBLANES, L))

    t0 = pl.program_id(0) * S

    for c_ref in c_refs:
        c_ref[...] = jnp.zeros(c_ref.shape, F32)
    causal = (lax.broadcasted_iota(jnp.int32, (L, L), 0) >= lax.broadcasted_iota(jnp.int32, (L, L), 1))
    ones_blk = jnp.ones((L, dh), BF16)
    mean_mat = jnp.full((dh, dh), 1.0 / dh, BF16)

    def mix(c, h, g_prev):
        r0 = pl.multiple_of(c * L, L)
        hs = slice(h * dh, (h + 1) * dh)
        qc = q_ref[pl.ds(r0, L), hs]
        ktc = kt_ref[hs, pl.ds(r0, L)]
        v_aug = jnp.concatenate([v_ref[pl.ds(r0, L), hs], ones_blk], axis=1)
        g0l = pl.ds(pl.multiple_of(t0 + r0, L), L)
        a_row = a_ref[h:h + 1, g0l]
        g_end = gend_ref[h:h + 1, g0l]
        g0_t = jnp.broadcast_to(g0_ref[h:h + 1, g0l], (L, L)).T

        s_qk = jnp.dot(qc, ktc, preferred_element_type=F32)
        p = (jnp.exp(jnp.where(causal, a_row - g0_t, -jnp.inf)) * s_qk).astype(BF16)
        c_prev = c_refs[h][...]
        q_w = (qc.astype(F32) * jnp.exp(g_prev - g0_t)).astype(BF16)
        nd_refs[h][...] = jnp.dot(jnp.concatenate([q_w, p], axis=1),
                                  jnp.concatenate([c_prev.astype(BF16), v_aug], axis=0),
                                  preferred_element_type=F32)

        kte = (ktc.astype(F32) * jnp.exp(a_row - g_end)).astype(BF16)
        decay = jnp.exp(g_prev - g_end)
        c_refs[h][...] = (jnp.concatenate([decay, decay], axis=1) * c_prev
                          + jnp.dot(kte, v_aug, preferred_element_type=F32))
        return g_end

    def emit(c, h):
        r0 = pl.multiple_of(c * L, L)
        hs = slice(h * dh, (h + 1) * dh)
        nm_t = jnp.broadcast_to(nm_ref[h:h + 1, pl.ds(pl.multiple_of(t0 + r0, L), L)], (L, L)).T
        nd = nd_refs[h][...]
        h_tilde = nd[:, 0:dh] / jnp.maximum(jnp.abs(nd[:, dh:2 * dh]), jnp.exp(nm_t))
        hm = _sigmoid(o_ref[pl.ds(r0, L), hs].astype(F32)) * h_tilde
        mu = jnp.dot(hm.astype(BF16), mean_mat, preferred_element_type=F32)
        hc = hm - mu
        var = jnp.dot((hc * hc).astype(BF16), mean_mat, preferred_element_type=F32)
        out_ref[pl.ds(r0, L), hs] = (hc * lax.rsqrt(var + EPS)).astype(BF16)

    zero = jnp.zeros((1, L), F32)
    g_first = tuple(mix(0, h, zero) for h in range(H))

    def body(c, g_carry):
        for h in range(H):
            emit(c - 1, h)
        return tuple(mix(c, h, g_carry[h]) for h in range(H))

    lax.fori_loop(1, S // L, body, g_first, unroll=5)
    for h in range(H):
        emit(S // L - 1, h)


def _mlstm(q, kt, v, o, gatest, weights, row_scales, *, batch, seq):
    T, dm = q.shape
    dh = dm // N_MLSTM_HEADS
    kern = functools.partial(_mlstm_kernel, chunk=MLSTM_CHUNK, n_cast=len(weights))
    tok = lambda w: pl.BlockSpec((seq, w), lambda b: (b, 0))
    rows = pltpu.VMEM((SUBLANES, T), F32)
    bf16_rows = 2 * SUBLANES
    assert all(w.shape[0] % (batch * bf16_rows) == 0 for w in weights)
    w_specs = [pl.BlockSpec((w.shape[0] // batch, w.shape[1]), lambda b: (b, 0)) for w in weights]
    s_specs = [pl.BlockSpec((w.shape[0] // batch, 1), lambda b: (b, 0)) for w in weights]
    scales = [sc.reshape(-1, 1).astype(F32) for sc in row_scales]
    return pl.pallas_call(
        kern,
        grid=(batch,),
        in_specs=[tok(dm), pl.BlockSpec((dm, seq), lambda b: (0, b)), tok(dm), tok(dm),
                  _const_spec((SUBLANES, T))] + w_specs + s_specs,
        out_specs=[tok(dm)] + w_specs,
        out_shape=[jax.ShapeDtypeStruct((T, dm), BF16)] + [jax.ShapeDtypeStruct(w.shape, BF16) for w in weights],
        scratch_shapes=([rows, rows, rows, rows] + [pltpu.VMEM((dh, 2 * dh), F32)] * N_MLSTM_HEADS
                        + [pltpu.VMEM((MLSTM_CHUNK, 2 * dh), F32)] * N_MLSTM_HEADS),
        compiler_params=pltpu.CompilerParams(dimension_semantics=("arbitrary",),
                                             vmem_limit_bytes=VMEM_LIMIT),
        name="mlstm",
    )(q, kt, v, o, gatest, *weights, *scales)


def _s5_prep_kernel(arow_ref, acol_ref, bre_ref, bim_ref, cre_ref, cim_ref, drow_ref,
                    rep_ref, rep2_ref, selrev_ref, pwcol_ref, w1_ref, wout_ref, al_ref, *, chunk):
    for p in range(arow_ref.shape[0]):
        _s5_prep_pair(p, arow_ref, acol_ref, bre_ref, bim_ref, cre_ref, cim_ref, drow_ref,
                      rep_ref, rep2_ref, selrev_ref, pwcol_ref, w1_ref, wout_ref, al_ref, chunk=chunk)


def _s5_prep_pair(p, arow_ref, acol_ref, bre_ref, bim_ref, cre_ref, cim_ref, drow_ref,
                  rep_ref, rep2_ref, selrev_ref, pwcol_ref, w1_ref, wout_ref, al_ref, *, chunk):
    L = chunk
    P = S5_STATE
    Hc = S5_GROUP
    R = L * Hc
    hi = lax.Precision.HIGHEST

    def select(x, sel_ref):
        x_hi = x.astype(BF16)
        x_lo = (x - x_hi.astype(F32)).astype(BF16)
        sel = sel_ref[...]
        return (jnp.dot(x_hi, sel, preferred_element_type=F32) + jnp.dot(x_lo, sel, preferred_element_type=F32))

    dt_r = jnp.exp(arow_ref[p, 2:3, :])
    da_re, da_im = dt_r * arow_ref[p, 0:1, :], dt_r * arow_ref[p, 1:2, :]

    def abar_pow(tau):
        mag = jnp.exp(tau * da_re)
        ang = tau * da_im
        return mag * jnp.cos(ang), mag * jnp.sin(ang)

    n_pow = -(-(L + 1) // SUBLANES) * SUBLANES
    er_re, er_im = abar_pow(lax.broadcasted_iota(jnp.int32, (n_pow, 1), 0).astype(F32))
    pad = jnp.zeros((2 * P - n_pow, 2 * P), F32)
    ec_re = jnp.concatenate([er_re, pad], axis=0).T
    ec_im = jnp.concatenate([er_im, pad], axis=0).T

    a_re_c, a_im_c = acol_ref[p, :, 0:1], acol_ref[p, :, 1:2]
    ab_re, ab_im = ec_re[:, 1:2], ec_im[:, 1:2]
    den = a_re_c * a_re_c + a_im_c * a_im_c
    nr = ab_re - 1.0
    z_re = (nr * a_re_c + ab_im * a_im_c) / den
    z_im = (ab_im * a_re_c - nr * a_im_c) / den
    bb_re = z_re * bre_ref[p] - z_im * bim_ref[p]
    bb_im = z_re * bim_ref[p] + z_im * bre_ref[p]
    first = lax.broadcasted_iota(jnp.int32, (2 * P, Hc), 0) < P
    bd = lambda b: jnp.concatenate([jnp.where(first, b, 0.0), jnp.where(first, 0.0, b)], axis=1)

    def rows(e, lo):
        return jnp.concatenate([jnp.broadcast_to(e[lo + t:lo + t + 1, :], (Hc, 2 * P)) for t in range(L)], axis=0)

    c_re = jnp.concatenate([cre_ref[p]] * L, axis=0)
    c_im = jnp.concatenate([cim_ref[p]] * L, axis=0)

    def c_times_pow(lo):
        e_re, e_im = rows(er_re, lo), rows(er_im, lo)
        return c_re * e_re - c_im * e_im, c_re * e_im + c_im * e_re

    ce_re, ce_im = c_times_pow(0)
    kflat = (jnp.dot(ce_re, bd(bb_re), preferred_element_type=F32, precision=hi)
             - jnp.dot(ce_im, bd(bb_im), preferred_element_type=F32, precision=hi))
    row = lax.broadcasted_iota(jnp.int32, (R, 2 * Hc), 0)
    col = lax.broadcasted_iota(jnp.int32, (R, 2 * Hc), 1) & (Hc - 1)
    kflat = kflat + jnp.where(row == col, drow_ref[p], 0.0)

    m = select(kflat, rep2_ref)
    blk = (lax.broadcasted_iota(jnp.int32, (R, 2 * R), 1) & (R - 1)) // Hc
    sh = 1
    while sh < L:
        shifted = jnp.concatenate([jnp.zeros((sh * Hc, 2 * R), F32), m[0:R - sh * Hc, :]], axis=0)
        m = jnp.where((blk & sh) != 0, shifted, m)
        sh *= 2

    ev_re, ev_im = select(ec_re, selrev_ref), select(ec_im, selrev_ref)
    bt_re, bt_im = select(bb_re, rep_ref), select(bb_im, rep_ref)
    st_re = (ev_re * bt_re - ev_im * bt_im).astype(BF16)
    st_im = (ev_re * bt_im + ev_im * bt_re).astype(BF16)

    co_re, co_im = c_times_pow(1)
    lane_first = lax.broadcasted_iota(jnp.int32, (R, 2 * P), 1) < P
    for g in range(2):
        w1_ref[2 * p + g, 0:R, :] = m[:, g * R:(g + 1) * R].astype(BF16)
        w1_ref[2 * p + g, R:R + P, :] = st_re[g * P:(g + 1) * P, :]
        w1_ref[2 * p + g, R + P:R + 2 * P, :] = st_im[g * P:(g + 1) * P, :]
        own = lane_first if g == 0 else jnp.logical_not(lane_first)
        wout_ref[2 * p + g, :, 0:2 * P] = jnp.where(own, co_re, 0.0).astype(BF16)
        wout_ref[2 * p + g, :, 2 * P:4 * P] = jnp.where(own, -co_im, 0.0).astype(BF16)

    sc_re, sc_im = abar_pow(pwcol_ref[...])
    al_ref[p, 0:SUBLANES, :] = sc_re
    al_ref[p, SUBLANES:2 * SUBLANES, :] = sc_im


def _s5_prep(a_re, a_im, log_dt, b_re, b_im, c_re, c_im, d_skip, *, chunk, n_chunks):
    G, P = a_re.shape
    Hc = S5_GROUP
    R = chunk * Hc
    assert G % 2 == 0 and 2 * P == LANES
    G2 = G // 2
    n_pairs = math.gcd(G2, S5_PREP_PAIRS)
    ldt = jnp.broadcast_to(log_dt[:, None], (G, P))
    arow = jnp.stack([a_re.reshape(G2, 2 * P), a_im.reshape(G2, 2 * P), ldt.reshape(G2, 2 * P)], axis=1)
    acol = jnp.stack([a_re.reshape(G2, 2 * P), a_im.reshape(G2, 2 * P)], axis=2)
    pair_lanes = lambda c: c.reshape(G2, 2, Hc, P).transpose(0, 2, 1, 3).reshape(G2, Hc, 2 * P)
    drow = d_skip.reshape(G2, 1, 2 * Hc)
    rep = (jnp.arange(R)[None, :] % Hc == jnp.arange(Hc)[:, None]).astype(BF16)
    rep2 = jnp.kron(jnp.eye(2, dtype=BF16), rep)
    selrev = (jnp.arange(LANES)[:, None] == chunk - 1 - jnp.arange(R)[None, :] // Hc).astype(BF16)
    n_steps = max(1, (n_chunks - 1).bit_length())
    assert chunk + 1 <= LANES and n_steps <= SUBLANES
    pwcol = jnp.zeros((SUBLANES, 1), F32).at[:n_steps, 0].set(chunk * 2.0 ** jnp.arange(n_steps))
    kern = functools.partial(_s5_prep_kernel, chunk=chunk)
    pair = lambda *s: pl.BlockSpec((n_pairs,) + s, lambda g: (g,) + (0,) * len(s))
    two = lambda *s: pl.BlockSpec((2 * n_pairs,) + s, lambda g: (g,) + (0,) * len(s))
    return pl.pallas_call(
        kern,
        grid=(G2 // n_pairs,),
        in_specs=[pair(3, 2 * P), pair(2 * P, 2), pair(2 * P, Hc), pair(2 * P, Hc), pair(Hc, 2 * P), pair(Hc, 2 * P),
                  pair(1, 2 * Hc), _const_spec((Hc, R)), _const_spec((2 * Hc, 2 * R)), _const_spec((LANES, R)),
                  _const_spec((SUBLANES, 1))],
        out_specs=[two(R + 2 * P, R), two(R, 4 * P), pair(2 * SUBLANES, 2 * P)],
        out_shape=[jax.ShapeDtypeStruct((G, R + 2 * P, R), BF16),
                   jax.ShapeDtypeStruct((G, R, 4 * P), BF16),
                   jax.ShapeDtypeStruct((G2, 2 * SUBLANES, 2 * P), F32)],
        compiler_params=pltpu.CompilerParams(dimension_semantics=("arbitrary",),
                                             vmem_limit_bytes=VMEM_LIMIT),
        name="s5_prep",
    )(arow, acol, b_re.reshape(G2, 2 * P, Hc), b_im.reshape(G2, 2 * P, Hc), pair_lanes(c_re), pair_lanes(c_im),
      drow, rep, rep2, selrev, pwcol)


def _s5_kernel(*refs, chunk, n_chunks):
    L = chunk
    n_oct = L // SUBLANES
    u_refs, (w1_ref, wout_ref, al_ref, y_ref, ut_ref, yt_ref) = refs[0:n_oct], refs[n_oct:]
    P = S5_STATE
    Hc = S5_GROUP
    R = L * Hc
    C = y_ref.shape[1] // SUBLANES
    n_grp = LANES // Hc

    for s in range(L):
        xt = u_refs[s // SUBLANES][pl.ds(s % SUBLANES, C, stride=SUBLANES), :].astype(BF16).T
        for j in range(n_grp):
            ut_ref[j, s * Hc:(s + 1) * Hc, :] = xt[j * Hc:(j + 1) * Hc, :]

    seg_pos = lax.broadcasted_iota(jnp.int32, (C, 2 * P), 0) & (n_chunks - 1)
    nt = (((1,), (1,)), ((), ()))

    def shifted(x, sh):
        return jnp.where(seg_pos >= sh, pltpu.roll(x, sh, 0), 0.0)

    def group_pair(i, carry):
        j0 = 2 * i
        res = [jnp.dot(w1_ref[j0 + d], ut_ref[j0 + d], preferred_element_type=F32) for d in range(2)]
        x_re = jnp.concatenate([res[0][R:R + P], res[1][R:R + P]], axis=0).T
        x_im = jnp.concatenate([res[0][R + P:R + 2 * P], res[1][R + P:R + 2 * P]], axis=0).T
        sh, k = 1, 0
        while sh < n_chunks:
            a_re = al_ref[i, k:k + 1, :]
            a_im = al_ref[i, SUBLANES + k:SUBLANES + k + 1, :]
            s_re, s_im = shifted(x_re, sh), shifted(x_im, sh)
            x_re, x_im = (x_re + a_re * s_re - a_im * s_im, x_im + a_re * s_im + a_im * s_re)
            sh *= 2
            k += 1
        xprev = jnp.concatenate([shifted(x_re, 1), shifted(x_im, 1)], axis=1).astype(BF16)
        for d in range(2):
            yt_ref[j0 + d] = res[d][0:R, :] + lax.dot_general(wout_ref[j0 + d], xprev, nt,
                                                              preferred_element_type=F32)
        return carry

    lax.fori_loop(0, n_grp // 2, group_pair, 0, unroll=2)

    for t in range(L):
        zt = jnp.concatenate([yt_ref[j, t * Hc:(t + 1) * Hc, :] for j in range(n_grp)], axis=0)
        y_ref[t // SUBLANES, pl.ds(t % SUBLANES, C, stride=SUBLANES), :] = zt.T


def _s5(u, w1, wout, al, *, chunk, n_chunks, n_col_blocks):
    n_oct, t_oct, ds5 = u.shape
    T = n_oct * t_oct
    assert n_oct * SUBLANES == chunk
    G, R1, R = w1.shape
    P4 = wout.shape[-1]
    n_grp = LANES // S5_GROUP
    n_q = ds5 // LANES
    tb = T // n_col_blocks
    C = tb // chunk
    kern = functools.partial(_s5_kernel, chunk=chunk, n_chunks=n_chunks)
    grp = lambda *s: pl.BlockSpec((n_grp,) + s, lambda q, i: (q,) + (0,) * len(s))
    octets = [pl.BlockSpec((None, C * SUBLANES, LANES), functools.partial(lambda q, i, o: (o, i, q), o=o))
              for o in range(n_oct)]
    return pl.pallas_call(
        kern,
        grid=(n_q, n_col_blocks),
        in_specs=octets + [grp(R1, R), grp(R, P4),
                           pl.BlockSpec((n_grp // 2, 2 * SUBLANES, P4 // 2), lambda q, i: (q, 0, 0))],
        out_specs=pl.BlockSpec((n_oct, C * SUBLANES, LANES), lambda q, i: (0, i, q)),
        out_shape=jax.ShapeDtypeStruct((n_oct, t_oct, ds5), F32),
        scratch_shapes=[pltpu.VMEM((n_grp, R, C), BF16), pltpu.VMEM((n_grp, R, C), F32)],
        compiler_params=pltpu.CompilerParams(dimension_semantics=("arbitrary", "arbitrary"),
                                             vmem_limit_bytes=BIG_VMEM_LIMIT),
        name="s5",
    )(*([u] * n_oct), w1, wout, al)


def _out_kernel(x_ref, hm_ref, y_ref, wglu_ref, bglu_ref, wo_ref,
                wgate_ref, wup_ref, wdown_ref, gf_ref, out_ref, acc_ref, xn_ref, *, ff_chunk, final_norm):
    dm = hm_ref.shape[1]
    n_oct, rows_oct, ds5 = y_ref.shape
    y = jnp.stack([y_ref[oc].reshape(rows_oct // SUBLANES, SUBLANES, ds5) for oc in range(n_oct)], axis=1)
    g = jax.nn.gelu(y.reshape(n_oct * rows_oct, ds5))
    z = jnp.dot(g.astype(BF16), wglu_ref[...], preferred_element_type=F32) + bglu_ref[...]
    gg = g * _sigmoid(z)
    hs = (gg * lax.rsqrt(jnp.mean(gg * gg, axis=-1, keepdims=True) + EPS)).astype(BF16)
    h1 = (x_ref[...] + jnp.dot(hm_ref[...], wo_ref[0:dm, :], preferred_element_type=F32)
          + jnp.dot(hs, wo_ref[dm:, :], preferred_element_type=F32))
    xn_ref[...] = (h1 * lax.rsqrt(jnp.mean(h1 * h1, axis=-1, keepdims=True) + EPS)).astype(BF16)
    acc_ref[...] = h1

    def ffn_chunk(c, carry):
        cs = pl.ds(pl.multiple_of(c * ff_chunk, ff_chunk), ff_chunk)
        xn2 = xn_ref[...]
        gate = jnp.dot(xn2, wgate_ref[:, cs], preferred_element_type=F32)
        up = jnp.dot(xn2, wup_ref[:, cs], preferred_element_type=F32)
        act = (_silu(gate) * up).astype(BF16)
        acc_ref[...] += jnp.dot(act, wdown_ref[cs, :], preferred_element_type=F32)
        return carry

    lax.fori_loop(0, wgate_ref.shape[1] // ff_chunk, ffn_chunk, 0, unroll=4)
    out_ref[...] = _rmsnorm(acc_ref[...], gf_ref[...]) if final_norm else acc_ref[...]


def _out_block(x2, hm, y, wglu, bglu, wo, wgate, wup, wdown, gf, *, tm, ff_chunk, final_norm):
    T, D = x2.shape
    dm = hm.shape[1]
    n_oct, _, ds5 = y.shape
    assert wgate.shape[1] % ff_chunk == 0
    kern = functools.partial(_out_kernel, ff_chunk=ff_chunk, final_norm=final_norm)
    tok = lambda w: pl.BlockSpec((tm, w), lambda i: (i, 0))
    return pl.pallas_call(
        kern,
        grid=(T // tm,),
        in_specs=[tok(D), tok(dm), pl.BlockSpec((n_oct, tm // n_oct, ds5), lambda i: (0, i, 0)),
                  _const_spec(wglu.shape), _const_spec((1, ds5)), _const_spec(wo.shape),
                  _const_spec(wgate.shape), _const_spec(wup.shape), _const_spec(wdown.shape),
                  _const_spec((1, D))],
        out_specs=tok(D),
        out_shape=jax.ShapeDtypeStruct((T, D), F32),
        scratch_shapes=[pltpu.VMEM((tm, D), F32), pltpu.VMEM((tm, D), BF16)],
        compiler_params=pltpu.CompilerParams(dimension_semantics=("arbitrary",),
                                             vmem_limit_bytes=BIG_VMEM_LIMIT),
        name="out_block",
    )(x2, hm, y, wglu, bglu, wo, wgate, wup, wdown, gf)


def kernel(x, norm1_g, w_in, if_bias, conv_qk, mlstm_norm_g, a_re, a_im, log_dt, b_re, b_im, c_re, c_im,
           d_skip, w_glu, b_glu, s5_norm_g, w_out, norm2_g, w_gate, w_up, w_down, norm_f_g):
    B, S, D = x.shape
    depth = w_in.shape[0]
    H = N_MLSTM_HEADS
    dm = mlstm_norm_g.shape[1]
    T = B * S
    L5 = S5_CHUNK
    nc5 = S // L5
    row = lambda a: a.reshape(1, -1).astype(F32)

    h = x.reshape(T, D)
    for l in range(depth):
        last = l == depth - 1
        gb = jnp.zeros((1, LANES), F32).at[0, 0:2 * H].set(if_bias[l])

        q, kt, v, o, gatest, u = _inproj(h, row(norm1_g[l]), w_in[l].T.astype(F32), gb, conv_qk[l].astype(F32),
                                         seq=S, tm=min(INPROJ_TM, S), d_mlstm=dm, n_oct=L5 // SUBLANES)
        ones = lambda n: jnp.ones((n,), F32)
        g_mix = jnp.concatenate([mlstm_norm_g[l], s5_norm_g[l]]).astype(F32)
        g_ffn = norm2_g[l].astype(F32)
        hm, wglu, wo, wgate, wup, wdown = _mlstm(
            q, kt, v, o, gatest,
            [w.astype(F32) for w in (w_glu[l], w_out[l], w_gate[l], w_up[l], w_down[l])],
            [ones(w_glu.shape[1]), g_mix, g_ffn, g_ffn, ones(w_down.shape[1])], batch=B, seq=S)

        w1, wout, al = _s5_prep(a_re[l], a_im[l], log_dt[l], b_re[l], b_im[l], c_re[l], c_im[l],
                                d_skip[l], chunk=L5, n_chunks=nc5)
        y = _s5(u, w1, wout, al, chunk=L5, n_chunks=nc5, n_col_blocks=S5_COL_BLOCKS)

        h = _out_block(h, hm, y, wglu, row(b_glu[l]), wo, wgate, wup, wdown, row(norm_f_g),
                       tm=OUT_TM, ff_chunk=FF_CHUNK, final_norm=last)
    return h.reshape(B, S, D)
```

```python
import functools
import math

import jax
import jax.numpy as jnp
from jax import lax
from jax.experimental import pallas as pl
from jax.experimental.pallas import tpu as pltpu

EPS = 1e-6
N_MLSTM_HEADS = 4
CONV_WIDTH = 4
S5_GROUP = 16
S5_STATE = 64

LANES = 128
SUBLANES = 8
MLSTM_CHUNK = 128
S5_CHUNK = 16
S5_COL_BLOCKS = 1
S5_PREP_PAIRS = 4
INPROJ_TM = 1024
OUT_TM = 1024
FF_CHUNK = 256
VMEM_LIMIT = 48 * 1024 * 1024
BIG_VMEM_LIMIT = 58 * 1024 * 1024

F32 = jnp.float32
BF16 = jnp.bfloat16


def _rmsnorm(x, g):
    ms = jnp.mean(x * x, axis=-1, keepdims=True)
    return x * lax.rsqrt(ms + EPS) * g


def _sigmoid(x):
    return 0.5 * jnp.tanh(0.5 * x) + 0.5


def _silu(x):
    h = 0.5 * x
    return h + h * jnp.tanh(h)


def _log_sigmoid(x):
    return -(jnp.maximum(-x, 0.0) + jnp.log1p(jnp.exp(-jnp.abs(x))))


def _const_spec(shape):
    return pl.BlockSpec(shape, lambda *_: (0,) * len(shape), pipeline_mode=pl.Buffered(1))


def _inproj_kernel(x_ref, g1_ref, wt_ref, gb_ref, cw_ref,
                   q_ref, kt_ref, v_ref, o_ref, gatest_ref, u_ref,
                   ext_ref, wm_ref, wg_ref, wu_ref, *, tiles_per_seq, d_mlstm, k_scale):
    tm = x_ref.shape[0]
    dm = d_mlstm
    H = N_MLSTM_HEADS
    halo = SUBLANES

    @pl.when(pl.program_id(0) == 0)
    def _():
        g1 = g1_ref[...]
        for c0 in range(0, 4 * dm, dm):
            wm_ref[:, c0:c0 + dm] = (wt_ref[c0:c0 + dm, :] * g1).T.astype(BF16)
        wu_ref[...] = (wt_ref[4 * dm + 2 * H:, :] * g1).T.astype(BF16)
        fill = jnp.zeros((LANES - 2 * H, wt_ref.shape[1]), F32)
        wg_ref[...] = jnp.concatenate([wt_ref[4 * dm:4 * dm + 2 * H, :] * g1, fill], axis=0).T.astype(BF16)

    x = x_ref[...]
    xn = (x * lax.rsqrt(jnp.mean(x * x, axis=-1, keepdims=True) + EPS)).astype(BF16)

    @pl.when(pl.program_id(0) % tiles_per_seq == 0)
    def _():
        ext_ref[0:halo, :] = jnp.zeros((halo, 2 * dm), F32)

    ext_ref[halo:halo + tm, :] = jnp.dot(xn, wm_ref[:, 0:2 * dm], preferred_element_type=F32)
    v = jnp.dot(xn, wm_ref[:, 2 * dm:3 * dm], preferred_element_type=F32)
    o = jnp.dot(xn, wm_ref[:, 3 * dm:4 * dm], preferred_element_type=F32)
    u = jnp.dot(xn, wu_ref[...], preferred_element_type=F32)
    gates = jnp.dot(xn, wg_ref[...], preferred_element_type=F32) + gb_ref[...]
    v_ref[...] = v.astype(BF16)
    o_ref[...] = o.astype(BF16)
    n_oct = u_ref.shape[0]
    u4 = u.reshape(tm // (n_oct * SUBLANES), n_oct, SUBLANES, u.shape[1])
    for oc in range(n_oct):
        u_ref[oc] = u4[:, oc].reshape(tm // n_oct, u.shape[1])
    gatest_ref[...] = gates.T[0:2 * H, :]

    def zero_of(t):
        w = pltpu.bitcast(t[tm - SUBLANES:tm, t.shape[1] - LANES:], jnp.uint32)
        return pltpu.bitcast((w >> 16) >> 16, F32)
    anchor = zero_of(v)

    acc = cw_ref[CONV_WIDTH - 1:CONV_WIDTH, :] * ext_ref[halo:halo + tm, :]
    for j in range(1, CONV_WIDTH):
        acc = acc + cw_ref[CONV_WIDTH - 1 - j:CONV_WIDTH - j, :] * ext_ref[halo - j:halo - j + tm, :]
    ext_ref[0:halo, :] = ext_ref[tm:tm + halo, :]
    qk = _silu(acc) + jnp.tile(anchor, (tm // SUBLANES, 2 * dm // LANES))
    q_ref[...] = qk[:, 0:dm].astype(BF16)
    kt_ref[...] = (qk[:, dm:2 * dm] * k_scale).T.astype(BF16)


def _inproj(x2, g1, w_in_t, gb, cw, *, seq, tm, d_mlstm, n_oct):
    T, D = x2.shape
    dm = d_mlstm
    H = N_MLSTM_HEADS
    assert 2 * H == SUBLANES
    ds5 = w_in_t.shape[0] - 4 * dm - 2 * H
    kern = functools.partial(_inproj_kernel, tiles_per_seq=seq // tm, d_mlstm=dm,
                             k_scale=float((dm // H) ** -0.5))
    tok = lambda w: pl.BlockSpec((tm, w), lambda i: (i, 0))
    return pl.pallas_call(
        kern,
        grid=(T // tm,),
        in_specs=[tok(D), _const_spec((1, D)), _const_spec(w_in_t.shape), _const_spec((1, LANES)),
                  _const_spec(cw.shape)],
        out_specs=[tok(dm), pl.BlockSpec((dm, tm), lambda i: (0, i)), tok(dm), tok(dm),
                   pl.BlockSpec((2 * H, tm), lambda i: (0, i)),
                   pl.BlockSpec((n_oct, tm // n_oct, ds5), lambda i: (0, i, 0))],
        out_shape=[jax.ShapeDtypeStruct((T, dm), BF16), jax.ShapeDtypeStruct((dm, T), BF16),
                   jax.ShapeDtypeStruct((T, dm), BF16), jax.ShapeDtypeStruct((T, dm), BF16),
                   jax.ShapeDtypeStruct((2 * H, T), F32),
                   jax.ShapeDtypeStruct((n_oct, T // n_oct, ds5), F32)],
        scratch_shapes=[pltpu.VMEM((tm + 2 * SUBLANES, 2 * dm), F32), pltpu.VMEM((D, 4 * dm), BF16),
                        pltpu.VMEM((D, LANES), BF16), pltpu.VMEM((D, ds5), BF16)],
        compiler_params=pltpu.CompilerParams(dimension_semantics=("arbitrary",),
                                             vmem_limit_bytes=VMEM_LIMIT),
        name="inproj",
    )(x2, g1, w_in_t, gb, cw)


def _lane_scan(x, op, identity, seg):
    pos = lax.broadcasted_iota(jnp.int32, x.shape, x.ndim - 1) & (seg - 1)
    sh = 1
    while sh < seg:
        x = op(x, jnp.where(pos >= sh, pltpu.roll(x, sh, x.ndim - 1), identity))
        sh *= 2
    return x


def _mlstm_kernel(q_ref, kt_ref, v_ref, o_ref, gatest_ref, *refs, chunk, n_cast):
    S = q_ref.shape[0]
    H = N_MLSTM_HEADS
    dh = q_ref.shape[1] // H
    L = chunk
    assert L == dh == LANES
    w32_refs, ws_refs = refs[0:n_cast], refs[n_cast:2 * n_cast]
    out_ref, w16_refs = refs[2 * n_cast], refs[2 * n_cast + 1:3 * n_cast + 1]
    a_ref, g0_ref, nm_ref, gend_ref = refs[3 * n_cast + 1:3 * n_cast + 5]
    c_refs, nd_refs = refs[3 * n_cast + 5:3 * n_cast + 5 + H], refs[3 * n_cast + 5 + H:]

    for w32_ref, ws_ref, w16_ref in zip(w32_refs, ws_refs, w16_refs):
        w16_ref[...] = (w32_ref[...] * ws_ref[...]).astype(BF16)

    @pl.when(pl.program_id(0) == 0)
    def _():
        gates = gatest_ref[...]
        f_cum = pltpu.roll(_lane_scan(_log_sigmoid(gates), jnp.add, 0.0, S), H, 0)
        a = gates - f_cum
        g0 = jnp.maximum(_lane_scan(a, jnp.maximum, -jnp.inf, S), 0.0)
        a_ref[...] = a
        g0_ref[...] = g0
        nm_ref[...] = -(f_cum + g0)
        for c in range(gatest_ref.shape[1] // L):
            gend_ref[:, c * L:(c + 1) * L] = jnp.broadcast_to(g0[:, (c + 1) * L - 1:(c + 1) * L], (SUBLANES, L))

    t0 = pl.program_id(0) * S

    for c_ref in c_refs:
        c_ref[...] = jnp.zeros(c_ref.shape, F32)
    causal = (lax.broadcasted_iota(jnp.int32, (L, L), 0) >= lax.broadcasted_iota(jnp.int32, (L, L), 1))
    ones_blk = jnp.ones((L, dh), BF16)
    mean_mat = jnp.full((dh, dh), 1.0 / dh, BF16)

    def mix(c, h, g_prev):
        r0 = pl.multiple_of(c * L, L)
        hs = slice(h * dh, (h + 1) * dh)
        qc = q_ref[pl.ds(r0, L), hs]
        ktc = kt_ref[hs, pl.ds(r0, L)]
        v_aug = jnp.concatenate([v_ref[pl.ds(r0, L), hs], ones_blk], axis=1)
        g0l = pl.ds(pl.multiple_of(t0 + r0, L), L)
        a_row = a_ref[h:h + 1, g0l]
        g_end = gend_ref[h:h + 1, g0l]
        g0_t = jnp.broadcast_to(g0_ref[h:h + 1, g0l], (L, L)).T

        s_qk = jnp.dot(qc, ktc, preferred_element_type=F32)
        p = (jnp.exp(jnp.where(causal, a_row - g0_t, -jnp.inf)) * s_qk).astype(BF16)
        c_prev = c_refs[h][...]
        q_w = (qc.astype(F32) * jnp.exp(g_prev - g0_t)).astype(BF16)
        nd_refs[h][...] = jnp.dot(jnp.concatenate([q_w, p], axis=1),
                                  jnp.concatenate([c_prev.astype(BF16), v_aug], axis=0),
                                  preferred_element_type=F32)

        kte = (ktc.astype(F32) * jnp.exp(a_row - g_end)).astype(BF16)
        decay = jnp.exp(g_prev - g_end)
        c_refs[h][...] = (jnp.concatenate([decay, decay], axis=1) * c_prev
                          + jnp.dot(kte, v_aug, preferred_element_type=F32))
        return g_end

    def emit(c, h):
        r0 = pl.multiple_of(c * L, L)
        hs = slice(h * dh, (h + 1) * dh)
        nm_t = jnp.broadcast_to(nm_ref[h:h + 1, pl.ds(pl.multiple_of(t0 + r0, L), L)], (L, L)).T
        nd = nd_refs[h][...]
        h_tilde = nd[:, 0:dh] / jnp.maximum(jnp.abs(nd[:, dh:2 * dh]), jnp.exp(nm_t))
        hm = _sigmoid(o_ref[pl.ds(r0, L), hs].astype(F32)) * h_tilde
        mu = jnp.dot(hm.astype(BF16), mean_mat, preferred_element_type=F32)
        hc = hm - mu
        var = jnp.dot((hc * hc).astype(BF16), mean_mat, preferred_element_type=F32)
        out_ref[pl.ds(r0, L), hs] = (hc * lax.rsqrt(var + EPS)).astype(BF16)

    zero = jnp.zeros((1, L), F32)
    g_first = tuple(mix(0, h, zero) for h in range(H))

    def body(c, g_carry):
        for h in range(H):
            emit(c - 1, h)
        return tuple(mix(c, h, g_carry[h]) for h in range(H))

    lax.fori_loop(1, S // L, body, g_first, unroll=5)
    for h in range(H):
        emit(S // L - 1, h)


def _mlstm(q, kt, v, o, gatest, weights, row_scales, *, batch, seq):
    T, dm = q.shape
    dh = dm // N_MLSTM_HEADS
    kern = functools.partial(_mlstm_kernel, chunk=MLSTM_CHUNK, n_cast=len(weights))
    tok = lambda w: pl.BlockSpec((seq, w), lambda b: (b, 0))
    rows = pltpu.VMEM((SUBLANES, T), F32)
    bf16_rows = 2 * SUBLANES
    assert all(w.shape[0] % (batch * bf16_rows) == 0 for w in weights)
    w_specs = [pl.BlockSpec((w.shape[0] // batch, w.shape[1]), lambda b: (b, 0)) for w in weights]
    s_specs = [pl.BlockSpec((w.shape[0] // batch, 1), lambda b: (b, 0)) for w in weights]
    scales = [sc.reshape(-1, 1).astype(F32) for sc in row_scales]
    return pl.pallas_call(
        kern,
        grid=(batch,),
        in_specs=[tok(dm), pl.BlockSpec((dm, seq), lambda b: (0, b)), tok(dm), tok(dm),
                  _const_spec((SUBLANES, T))] + w_specs + s_specs,
        out_specs=[tok(dm)] + w_specs,
        out_shape=[jax.ShapeDtypeStruct((T, dm), BF16)] + [jax.ShapeDtypeStruct(w.shape, BF16) for w in weights],
        scratch_shapes=([rows, rows, rows, rows] + [pltpu.VMEM((dh, 2 * dh), F32)] * N_MLSTM_HEADS
                        + [pltpu.VMEM((MLSTM_CHUNK, 2 * dh), F32)] * N_MLSTM_HEADS),
        compiler_params=pltpu.CompilerParams(dimension_semantics=("arbitrary",),
                                             vmem_limit_bytes=VMEM_LIMIT),
        name="mlstm",
    )(q, kt, v, o, gatest, *weights, *scales)


def _s5_prep_kernel(arow_ref, acol_ref, bre_ref, bim_ref, cre_ref, cim_ref, drow_ref,
                    rep_ref, rep2_ref, selrev_ref, pwcol_ref, w1_ref, wout_ref, al_ref, *, chunk):
    for p in range(arow_ref.shape[0]):
        _s5_prep_pair(p, arow_ref, acol_ref, bre_ref, bim_ref, cre_ref, cim_ref, drow_ref,
                      rep_ref, rep2_ref, selrev_ref, pwcol_ref, w1_ref, wout_ref, al_ref, chunk=chunk)


def _s5_prep_pair(p, arow_ref, acol_ref, bre_ref, bim_ref, cre_ref, cim_ref, drow_ref,
                  rep_ref, rep2_ref, selrev_ref, pwcol_ref, w1_ref, wout_ref, al_ref, *, chunk):
    L = chunk
    P = S5_STATE
    Hc = S5_GROUP
    R = L * Hc
    hi = lax.Precision.HIGHEST

    def select(x, sel_ref):
        x_hi = x.astype(BF16)
        x_lo = (x - x_hi.astype(F32)).astype(BF16)
        sel = sel_ref[...]
        return (jnp.dot(x_hi, sel, preferred_element_type=F32) + jnp.dot(x_lo, sel, preferred_element_type=F32))

    dt_r = jnp.exp(arow_ref[p, 2:3, :])
    da_re, da_im = dt_r * arow_ref[p, 0:1, :], dt_r * arow_ref[p, 1:2, :]

    def abar_pow(tau):
        mag = jnp.exp(tau * da_re)
        ang = tau * da_im
        return mag * jnp.cos(ang), mag * jnp.sin(ang)

    n_pow = -(-(L + 1) // SUBLANES) * SUBLANES
    er_re, er_im = abar_pow(lax.broadcasted_iota(jnp.int32, (n_pow, 1), 0).astype(F32))
    pad = jnp.zeros((2 * P - n_pow, 2 * P), F32)
    ec_re = jnp.concatenate([er_re, pad], axis=0).T
    ec_im = jnp.concatenate([er_im, pad], axis=0).T

    a_re_c, a_im_c = acol_ref[p, :, 0:1], acol_ref[p, :, 1:2]
    ab_re, ab_im = ec_re[:, 1:2], ec_im[:, 1:2]
    den = a_re_c * a_re_c + a_im_c * a_im_c
    nr = ab_re - 1.0
    z_re = (nr * a_re_c + ab_im * a_im_c) / den
    z_im = (ab_im * a_re_c - nr * a_im_c) / den
    bb_re = z_re * bre_ref[p] - z_im * bim_ref[p]
    bb_im = z_re * bim_ref[p] + z_im * bre_ref[p]
    first = lax.broadcasted_iota(jnp.int32, (2 * P, Hc), 0) < P
    bd = lambda b: jnp.concatenate([jnp.where(first, b, 0.0), jnp.where(first, 0.0, b)], axis=1)

    def rows(e, lo):
        return jnp.concatenate([jnp.broadcast_to(e[lo + t:lo + t + 1, :], (Hc, 2 * P)) for t in range(L)], axis=0)

    c_re = jnp.concatenate([cre_ref[p]] * L, axis=0)
    c_im = jnp.concatenate([cim_ref[p]] * L, axis=0)

    def c_times_pow(lo):
        e_re, e_im = rows(er_re, lo), rows(er_im, lo)
        return c_re * e_re - c_im * e_im, c_re * e_im + c_im * e_re

    ce_re, ce_im = c_times_pow(0)
    kflat = (jnp.dot(ce_re, bd(bb_re), preferred_element_type=F32, precision=hi)
             - jnp.dot(ce_im, bd(bb_im), preferred_element_type=F32, precision=hi))
    row = lax.broadcasted_iota(jnp.int32, (R, 2 * Hc), 0)
    col = lax.broadcasted_iota(jnp.int32, (R, 2 * Hc), 1) & (Hc - 1)
    kflat = kflat + jnp.where(row == col, drow_ref[p], 0.0)

    m = select(kflat, rep2_ref)
    blk = (lax.broadcasted_iota(jnp.int32, (R, 2 * R), 1) & (R - 1)) // Hc
    sh = 1
    while sh < L:
        shifted = jnp.concatenate([jnp.zeros((sh * Hc, 2 * R), F32), m[0:R - sh * Hc, :]], axis=0)
        m = jnp.where((blk & sh) != 0, shifted, m)
        sh *= 2

    ev_re, ev_im = select(ec_re, selrev_ref), select(ec_im, selrev_ref)
    bt_re, bt_im = select(bb_re, rep_ref), select(bb_im, rep_ref)
    st_re = (ev_re * bt_re - ev_im * bt_im).astype(BF16)
    st_im = (ev_re * bt_im + ev_im * bt_re).astype(BF16)

    co_re, co_im = c_times_pow(1)
    lane_first = lax.broadcasted_iota(jnp.int32, (R, 2 * P), 1) < P
    for g in range(2):
        w1_ref[2 * p + g, 0:R, :] = m[:, g * R:(g + 1) * R].astype(BF16)
        w1_ref[2 * p + g, R:R + P, :] = st_re[g * P:(g + 1) * P, :]
        w1_ref[2 * p + g, R + P:R + 2 * P, :] = st_im[g * P:(g + 1) * P, :]
        own = lane_first if g == 0 else jnp.logical_not(lane_first)
        wout_ref[2 * p + g, :, 0:2 * P] = jnp.where(own, co_re, 0.0).astype(BF16)
        wout_ref[2 * p + g, :, 2 * P:4 * P] = jnp.where(own, -co_im, 0.0).astype(BF16)

    sc_re, sc_im = abar_pow(pwcol_ref[...])
    al_ref[p, 0:SUBLANES, :] = sc_re
    al_ref[p, SUBLANES:2 * SUBLANES, :] = sc_im


def _s5_prep(a_re, a_im, log_dt, b_re, b_im, c_re, c_im, d_skip, *, chunk, n_chunks):
    G, P = a_re.shape
    Hc = S5_GROUP
    R = chunk * Hc
    assert G % 2 == 0 and 2 * P == LANES
    G2 = G // 2
    n_pairs = math.gcd(G2, S5_PREP_PAIRS)
    ldt = jnp.broadcast_to(log_dt[:, None], (G, P))
    arow = jnp.stack([a_re.reshape(G2, 2 * P), a_im.reshape(G2, 2 * P), ldt.reshape(G2, 2 * P)], axis=1)
    acol = jnp.stack([a_re.reshape(G2, 2 * P), a_im.reshape(G2, 2 * P)], axis=2)
    pair_lanes = lambda c: c.reshape(G2, 2, Hc, P).transpose(0, 2, 1, 3).reshape(G2, Hc, 2 * P)
    drow = d_skip.reshape(G2, 1, 2 * Hc)
    rep = (jnp.arange(R)[None, :] % Hc == jnp.arange(Hc)[:, None]).astype(BF16)
    rep2 = jnp.kron(jnp.eye(2, dtype=BF16), rep)
    selrev = (jnp.arange(LANES)[:, None] == chunk - 1 - jnp.arange(R)[None, :] // Hc).astype(BF16)
    n_steps = max(1, (n_chunks - 1).bit_length())
    assert chunk + 1 <= LANES and n_steps <= SUBLANES
    pwcol = jnp.zeros((SUBLANES, 1), F32).at[:n_steps, 0].set(chunk * 2.0 ** jnp.arange(n_steps))
    kern = functools.partial(_s5_prep_kernel, chunk=chunk)
    pair = lambda *s: pl.BlockSpec((n_pairs,) + s, lambda g: (g,) + (0,) * len(s))
    two = lambda *s: pl.BlockSpec((2 * n_pairs,) + s, lambda g: (g,) + (0,) * len(s))
    return pl.pallas_call(
        kern,
        grid=(G2 // n_pairs,),
        in_specs=[pair(3, 2 * P), pair(2 * P, 2), pair(2 * P, Hc), pair(2 * P, Hc), pair(Hc, 2 * P), pair(Hc, 2 * P),
                  pair(1, 2 * Hc), _const_spec((Hc, R)), _const_spec((2 * Hc, 2 * R)), _const_spec((LANES, R)),
                  _const_spec((SUBLANES, 1))],
        out_specs=[two(R + 2 * P, R), two(R, 4 * P), pair(2 * SUBLANES, 2 * P)],
        out_shape=[jax.ShapeDtypeStruct((G, R + 2 * P, R), BF16),
                   jax.ShapeDtypeStruct((G, R, 4 * P), BF16),
                   jax.ShapeDtypeStruct((G2, 2 * SUBLANES, 2 * P), F32)],
        compiler_params=pltpu.CompilerParams(dimension_semantics=("arbitrary",),
                                             vmem_limit_bytes=VMEM_LIMIT),
        name="s5_prep",
    )(arow, acol, b_re.reshape(G2, 2 * P, Hc), b_im.reshape(G2, 2 * P, Hc), pair_lanes(c_re), pair_lanes(c_im),
      drow, rep, rep2, selrev, pwcol)


def _s5_kernel(*refs, chunk, n_chunks):
    L = chunk
    n_oct = L // SUBLANES
    u_refs, (w1_ref, wout_ref, al_ref, y_ref, ut_ref, yt_ref) = refs[0:n_oct], refs[n_oct:]
    P = S5_STATE
    Hc = S5_GROUP
    R = L * Hc
    C = y_ref.shape[1] // SUBLANES
    n_grp = LANES // Hc

    for s in range(L):
        xt = u_refs[s // SUBLANES][pl.ds(s % SUBLANES, C, stride=SUBLANES), :].astype(BF16).T
        for j in range(n_grp):
            ut_ref[j, s * Hc:(s + 1) * Hc, :] = xt[j * Hc:(j + 1) * Hc, :]

    seg_pos = lax.broadcasted_iota(jnp.int32, (C, 2 * P), 0) & (n_chunks - 1)
    nt = (((1,), (1,)), ((), ()))

    def shifted(x, sh):
        return jnp.where(seg_pos >= sh, pltpu.roll(x, sh, 0), 0.0)

    def group_pair(i, carry):
        j0 = 2 * i
        res = [jnp.dot(w1_ref[j0 + d], ut_ref[j0 + d], preferred_element_type=F32) for d in range(2)]
        x_re = jnp.concatenate([res[0][R:R + P], res[1][R:R + P]], axis=0).T
        x_im = jnp.concatenate([res[0][R + P:R + 2 * P], res[1][R + P:R + 2 * P]], axis=0).T
        sh, k = 1, 0
        while sh < n_chunks:
            a_re = al_ref[i, k:k + 1, :]
            a_im = al_ref[i, SUBLANES + k:SUBLANES + k + 1, :]
            s_re, s_im = shifted(x_re, sh), shifted(x_im, sh)
            x_re, x_im = (x_re + a_re * s_re - a_im * s_im, x_im + a_re * s_im + a_im * s_re)
            sh *= 2
            k += 1
        xprev = jnp.concatenate([shifted(x_re, 1), shifted(x_im, 1)], axis=1).astype(BF16)
        for d in range(2):
            yt_ref[j0 + d] = res[d][0:R, :] + lax.dot_general(wout_ref[j0 + d], xprev, nt,
                                                              preferred_element_type=F32)
        return carry

    lax.fori_loop(0, n_grp // 2, group_pair, 0, unroll=2)

    for t in range(L):
        zt = jnp.concatenate([yt_ref[j, t * Hc:(t + 1) * Hc, :] for j in range(n_grp)], axis=0)
        y_ref[t // SUBLANES, pl.ds(t % SUBLANES, C, stride=SUBLANES), :] = zt.T


def _s5(u, w1, wout, al, *, chunk, n_chunks, n_col_blocks):
    n_oct, t_oct, ds5 = u.shape
    T = n_oct * t_oct
    assert n_oct * SUBLANES == chunk
    G, R1, R = w1.shape
    P4 = wout.shape[-1]
    n_grp = LANES // S5_GROUP
    n_q = ds5 // LANES
    tb = T // n_col_blocks
    C = tb // chunk
    kern = functools.partial(_s5_kernel, chunk=chunk, n_chunks=n_chunks)
    grp = lambda *s: pl.BlockSpec((n_grp,) + s, lambda q, i: (q,) + (0,) * len(s))
    octets = [pl.BlockSpec((None, C * SUBLANES, LANES), functools.partial(lambda q, i, o: (o, i, q), o=o))
              for o in range(n_oct)]
    return pl.pallas_call(
        kern,
        grid=(n_q, n_col_blocks),
        in_specs=octets + [grp(R1, R), grp(R, P4),
                           pl.BlockSpec((n_grp // 2, 2 * SUBLANES, P4 // 2), lambda q, i: (q, 0, 0))],
        out_specs=pl.BlockSpec((n_oct, C * SUBLANES, LANES), lambda q, i: (0, i, q)),
        out_shape=jax.ShapeDtypeStruct((n_oct, t_oct, ds5), F32),
        scratch_shapes=[pltpu.VMEM((n_grp, R, C), BF16), pltpu.VMEM((n_grp, R, C), F32)],
        compiler_params=pltpu.CompilerParams(dimension_semantics=("arbitrary", "arbitrary"),
                                             vmem_limit_bytes=BIG_VMEM_LIMIT),
        name="s5",
    )(*([u] * n_oct), w1, wout, al)


def _out_kernel(x_ref, hm_ref, y_ref, wglu_hbm, bglu_ref, wo_hbm, wgate_hbm, wup_hbm, wdown_hbm, gf_ref,
                out_ref, acc_ref, xn_ref, wglu_ref, wo_ref, wgate_ref, wup_ref, wdown_ref, sems,
                *, ff_chunk, final_norm):
    dm = hm_ref.shape[1]
    copies = [pltpu.make_async_copy(src, dst, sems.at[n]) for n, (src, dst) in enumerate(
        [(wglu_hbm, wglu_ref), (wo_hbm, wo_ref), (wgate_hbm, wgate_ref), (wup_hbm, wup_ref), (wdown_hbm, wdown_ref)])]
    first_step = pl.program_id(0) == 0

    @pl.when(first_step)
    def _():
        for cp in copies:
            cp.start(priority=1)

    def arrived(*which):
        @pl.when(first_step)
        def _():
            for n in which:
                copies[n].wait()

    n_oct, rows_oct, ds5 = y_ref.shape
    y = jnp.stack([y_ref[oc].reshape(rows_oct // SUBLANES, SUBLANES, ds5) for oc in range(n_oct)], axis=1)
    g = jax.nn.gelu(y.reshape(n_oct * rows_oct, ds5))
    arrived(0)
    z = jnp.dot(g.astype(BF16), wglu_ref[...], preferred_element_type=F32) + bglu_ref[...]
    gg = g * _sigmoid(z)
    hs = (gg * lax.rsqrt(jnp.mean(gg * gg, axis=-1, keepdims=True) + EPS)).astype(BF16)
    arrived(1)
    h1 =(x_ref[...] + jnp.dot(hm_ref[...], wo_ref[0:dm, :], preferred_element_type=F32)
          + jnp.dot(hs, wo_ref[dm:, :], preferred_element_type=F32))
    xn_ref[...] = (h1 * lax.rsqrt(jnp.mean(h1 * h1, axis=-1, keepdims=True) + EPS)).astype(BF16)
    acc_ref[...] = h1
    arrived(2, 3, 4)

    def ffn_chunk(c, carry):
        cs = pl.ds(pl.multiple_of(c * ff_chunk, ff_chunk), ff_chunk)
        xn2 = xn_ref[...]
        gate = jnp.dot(xn2, wgate_ref[:, cs], preferred_element_type=F32)
        up = jnp.dot(xn2, wup_ref[:, cs], preferred_element_type=F32)
        act = (_silu(gate) * up).astype(BF16)
        acc_ref[...] += jnp.dot(act, wdown_ref[cs, :], preferred_element_type=F32)
        return carry

    lax.fori_loop(0, wgate_ref.shape[1] // ff_chunk, ffn_chunk, 0, unroll=4)
    out_ref[...] = _rmsnorm(acc_ref[...], gf_ref[...]) if final_norm else acc_ref[...]


def _out_block(x2, hm, y, wglu, bglu, wo, wgate, wup, wdown, gf, *, tm, ff_chunk, final_norm):
    T, D = x2.shape
    dm = hm.shape[1]
    n_oct, _, ds5 = y.shape
    assert wgate.shape[1] % ff_chunk == 0
    kern = functools.partial(_out_kernel, ff_chunk=ff_chunk, final_norm=final_norm)
    tok = lambda w: pl.BlockSpec((tm, w), lambda i: (i, 0))
    in_hbm = pl.BlockSpec(memory_space=pl.ANY)
    weights = (wglu, wo, wgate, wup, wdown)
    return pl.pallas_call(
        kern,
        grid=(T // tm,),
        in_specs=[tok(D), tok(dm), pl.BlockSpec((n_oct, tm // n_oct, ds5), lambda i: (0, i, 0)),
                  in_hbm, _const_spec((1, ds5)), in_hbm, in_hbm, in_hbm, in_hbm, _const_spec((1, D))],
        out_specs=tok(D),
        out_shape=jax.ShapeDtypeStruct((T, D), F32),
        scratch_shapes=([pltpu.VMEM((tm, D), F32), pltpu.VMEM((tm, D), BF16)]
                        + [pltpu.VMEM(w.shape, w.dtype) for w in weights]
                        + [pltpu.SemaphoreType.DMA((len(weights),))]),
        compiler_params=pltpu.CompilerParams(dimension_semantics=("arbitrary",),
                                             vmem_limit_bytes=BIG_VMEM_LIMIT),
        name="out_block",
    )(x2, hm, y, wglu, bglu, wo, wgate, wup, wdown, gf)


def kernel(x, norm1_g, w_in, if_bias, conv_qk, mlstm_norm_g, a_re, a_im, log_dt, b_re, b_im, c_re, c_im,
           d_skip, w_glu, b_glu, s5_norm_g, w_out, norm2_g, w_gate, w_up, w_down, norm_f_g):
    B, S, D = x.shape
    depth = w_in.shape[0]
    H = N_MLSTM_HEADS
    dm = mlstm_norm_g.shape[1]
    T = B * S
    L5 = S5_CHUNK
    nc5 = S // L5
    row = lambda a: a.reshape(1, -1).astype(F32)

    h = x.reshape(T, D)
    for l in range(depth):
        last = l == depth - 1
        gb = jnp.zeros((1, LANES), F32).at[0, 0:2 * H].set(if_bias[l])

        q, kt, v, o, gatest, u = _inproj(h, row(norm1_g[l]), w_in[l].T.astype(F32), gb, conv_qk[l].astype(F32),
                                         seq=S, tm=min(INPROJ_TM, S), d_mlstm=dm, n_oct=L5 // SUBLANES)
        ones = lambda n: jnp.ones((n,), F32)
        g_mix = jnp.concatenate([mlstm_norm_g[l], s5_norm_g[l]]).astype(F32)
        g_ffn = norm2_g[l].astype(F32)
        hm, wglu, wo, wgate, wup, wdown = _mlstm(
            q, kt, v, o, gatest,
            [w.astype(F32) for w in (w_glu[l], w_out[l], w_gate[l], w_up[l], w_down[l])],
            [ones(w_glu.shape[1]), g_mix, g_ffn, g_ffn, ones(w_down.shape[1])], batch=B, seq=S)

        w1, wout, al = _s5_prep(a_re[l], a_im[l], log_dt[l], b_re[l], b_im[l], c_re[l], c_im[l],
                                d_skip[l], chunk=L5, n_chunks=nc5)
        y = _s5(u, w1, wout, al, chunk=L5, n_chunks=nc5, n_col_blocks=S5_COL_BLOCKS)

        h = _out_block(h, hm, y, wglu, row(b_glu[l]), wo, wgate, wup, wdown, row(norm_f_g),
                       tm=OUT_TM, ff_chunk=FF_CHUNK, final_norm=last)
    return h.reshape(B, S, D)
```

```python
import functools
import math

import jax
import jax.numpy as jnp
from jax import lax
from jax.experimental import pallas as pl
from jax.experimental.pallas import tpu as pltpu

EPS = 1e-6
N_MLSTM_HEADS = 4
CONV_WIDTH = 4
S5_GROUP = 16
S5_STATE = 64

LANES = 128
SUBLANES = 8
MLSTM_CHUNK = 128
S5_CHUNK = 16
S5_COL_BLOCKS = 1
S5_PREP_PAIRS = 4
INPROJ_TM = 1024
OUT_TM = 1024
FF_CHUNK = 256
VMEM_LIMIT = 48 * 1024 * 1024
BIG_VMEM_LIMIT = 58 * 1024 * 1024

F32 = jnp.float32
BF16 = jnp.bfloat16


def _rmsnorm(x, g):
    ms = jnp.mean(x * x, axis=-1, keepdims=True)
    return x * lax.rsqrt(ms + EPS) * g


def _sigmoid(x):
    return 0.5 * jnp.tanh(0.5 * x) + 0.5


def _silu(x):
    h = 0.5 * x
    return h + h * jnp.tanh(h)


def _log_sigmoid(x):
    return -(jnp.maximum(-x, 0.0) + jnp.log1p(jnp.exp(-jnp.abs(x))))


def _const_spec(shape):
    return pl.BlockSpec(shape, lambda *_: (0,) * len(shape), pipeline_mode=pl.Buffered(1))


def _inproj_kernel(x_ref, g1_ref, wt_ref, gb_ref, cw_ref,
                   q_ref, kt_ref, v_ref, o_ref, gatest_ref, u_ref,
                   ext_ref, wm_ref, wg_ref, wu_ref, *, tiles_per_seq, d_mlstm, k_scale):
    tm = x_ref.shape[0]
    dm = d_mlstm
    H = N_MLSTM_HEADS
    halo = SUBLANES

    @pl.when(pl.program_id(0) == 0)
    def _():
        g1 = g1_ref[...]
        for c0 in range(0, 4 * dm, dm):
            wm_ref[:, c0:c0 + dm] = (wt_ref[c0:c0 + dm, :] * g1).T.astype(BF16)
        wu_ref[...] = (wt_ref[4 * dm + 2 * H:, :] * g1).T.astype(BF16)
        fill = jnp.zeros((LANES - 2 * H, wt_ref.shape[1]), F32)
        wg_ref[...] = jnp.concatenate([wt_ref[4 * dm:4 * dm + 2 * H, :] * g1, fill], axis=0).T.astype(BF16)

    x = x_ref[...]
    xn = (x * lax.rsqrt(jnp.mean(x * x, axis=-1, keepdims=True) + EPS)).astype(BF16)

    @pl.when(pl.program_id(0) % tiles_per_seq == 0)
    def _():
        ext_ref[0:halo, :] = jnp.zeros((halo, 2 * dm), F32)

    ext_ref[halo:halo + tm, :] = jnp.dot(xn, wm_ref[:, 0:2 * dm], preferred_element_type=F32)
    v = jnp.dot(xn, wm_ref[:, 2 * dm:3 * dm], preferred_element_type=F32)
    o = jnp.dot(xn, wm_ref[:, 3 * dm:4 * dm], preferred_element_type=F32)
    u = jnp.dot(xn, wu_ref[...], preferred_element_type=F32)
    gates = jnp.dot(xn, wg_ref[...], preferred_element_type=F32) + gb_ref[...]
    v_ref[...] = v.astype(BF16)
    o_ref[...] = o.astype(BF16)
    n_oct = u_ref.shape[0]
    u4 = u.reshape(tm // (n_oct * SUBLANES), n_oct, SUBLANES, u.shape[1])
    for oc in range(n_oct):
        u_ref[oc] = u4[:, oc].reshape(tm // n_oct, u.shape[1])
    gatest_ref[...] = gates.T[0:2 * H, :]

    def zero_of(t):
        w = pltpu.bitcast(t[tm - SUBLANES:tm, t.shape[1] - LANES:], jnp.uint32)
        return pltpu.bitcast((w >> 16) >> 16, F32)
    anchor = zero_of(v)

    acc = cw_ref[CONV_WIDTH - 1:CONV_WIDTH, :] * ext_ref[halo:halo + tm, :]
    for j in range(1, CONV_WIDTH):
        acc = acc + cw_ref[CONV_WIDTH - 1 - j:CONV_WIDTH - j, :] * ext_ref[halo - j:halo - j + tm, :]
    ext_ref[0:halo, :] = ext_ref[tm:tm + halo, :]
    qk = _silu(acc) + jnp.tile(anchor, (tm // SUBLANES, 2 * dm // LANES))
    q_ref[...] = qk[:, 0:dm].astype(BF16)
    kt_ref[...] = (qk[:, dm:2 * dm] * k_scale).T.astype(BF16)


def _inproj(x2, g1, w_in_t, gb, cw, *, seq, tm, d_mlstm, n_oct):
    T, D = x2.shape
    dm = d_mlstm
    H = N_MLSTM_HEADS
    assert 2 * H == SUBLANES
    ds5 = w_in_t.shape[0] - 4 * dm - 2 * H
    kern = functools.partial(_inproj_kernel, tiles_per_seq=seq // tm, d_mlstm=dm,
                             k_scale=float((dm // H) ** -0.5))
    tok = lambda w: pl.BlockSpec((tm, w), lambda i: (i, 0))
    return pl.pallas_call(
        kern,
        grid=(T // tm,),
        in_specs=[tok(D), _const_spec((1, D)), _const_spec(w_in_t.shape), _const_spec((1, LANES)),
                  _const_spec(cw.shape)],
        out_specs=[tok(dm), pl.BlockSpec((dm, tm), lambda i: (0, i)), tok(dm), tok(dm),
                   pl.BlockSpec((2 * H, tm), lambda i: (0, i)),
                   pl.BlockSpec((n_oct, tm // n_oct, ds5), lambda i: (0, i, 0))],
        out_shape=[jax.ShapeDtypeStruct((T, dm), BF16), jax.ShapeDtypeStruct((dm, T), BF16),
                   jax.ShapeDtypeStruct((T, dm), BF16), jax.ShapeDtypeStruct((T, dm), BF16),
                   jax.ShapeDtypeStruct((2 * H, T), F32),
                   jax.ShapeDtypeStruct((n_oct, T // n_oct, ds5), F32)],
        scratch_shapes=[pltpu.VMEM((tm + 2 * SUBLANES, 2 * dm), F32), pltpu.VMEM((D, 4 * dm), BF16),
                        pltpu.VMEM((D, LANES), BF16), pltpu.VMEM((D, ds5), BF16)],
        compiler_params=pltpu.CompilerParams(dimension_semantics=("arbitrary",),
                                             vmem_limit_bytes=VMEM_LIMIT),
        name="inproj",
    )(x2, g1, w_in_t, gb, cw)


def _lane_scan(x, op, identity, seg):
    pos = lax.broadcasted_iota(jnp.int32, x.shape, x.ndim - 1) & (seg - 1)
    sh = 1
    while sh < seg:
        x = op(x, jnp.where(pos >= sh, pltpu.roll(x, sh, x.ndim - 1), identity))
        sh *= 2
    return x


def _mlstm_kernel(q_ref, kt_ref, v_ref, o_ref, gatest_ref, *refs, chunk, n_cast):
    S = q_ref.shape[0]
    H = N_MLSTM_HEADS
    dh = q_ref.shape[1] // H
    L = chunk
    assert L == dh == LANES
    w32_refs, ws_refs = refs[0:n_cast], refs[n_cast:2 * n_cast]
    out_ref, w16_refs = refs[2 * n_cast], refs[2 * n_cast + 1:3 * n_cast + 1]
    a_ref, g0_ref, nm_ref, gend_ref = refs[3 * n_cast + 1:3 * n_cast + 5]
    c_refs, nd_refs = refs[3 * n_cast + 5:3 * n_cast + 5 + H], refs[3 * n_cast + 5 + H:]

    for w32_ref, ws_ref, w16_ref in zip(w32_refs, ws_refs, w16_refs):
        w16_ref[...] = (w32_ref[...] * ws_ref[...]).astype(BF16)

    @pl.when(pl.program_id(0) == 0)
    def _():
        gates = gatest_ref[...]
        f_cum = pltpu.roll(_lane_scan(_log_sigmoid(gates), jnp.add, 0.0, S), H, 0)
        a = gates - f_cum
        g0 = jnp.maximum(_lane_scan(a, jnp.maximum, -jnp.inf, S), 0.0)
        a_ref[...] = a
        g0_ref[...] = g0
        nm_ref[...] = -(f_cum + g0)
        for c in range(gatest_ref.shape[1] // L):
            gend_ref[:, c * L:(c + 1) * L] = jnp.broadcast_to(g0[:, (c + 1) * L - 1:(c + 1) * L], (SUBLANES, L))

    t0 = pl.program_id(0) * S

    for c_ref in c_refs:
        c_ref[...] = jnp.zeros(c_ref.shape, F32)
    causal = (lax.broadcasted_iota(jnp.int32, (L, L), 0) >= lax.broadcasted_iota(jnp.int32, (L, L), 1))
    ones_blk = jnp.ones((L, dh), BF16)
    mean_mat = jnp.full((dh, dh), 1.0 / dh, BF16)

    def mix(c, h, g_prev):
        r0 = pl.multiple_of(c * L, L)
        hs = slice(h * dh, (h + 1) * dh)
        qc = q_ref[pl.ds(r0, L), hs]
        ktc = kt_ref[hs, pl.ds(r0, L)]
        v_aug = jnp.concatenate([v_ref[pl.ds(r0, L), hs], ones_blk], axis=1)
        g0l = pl.ds(pl.multiple_of(t0 + r0, L), L)
        a_row = a_ref[h:h + 1, g0l]
        g_end = gend_ref[h:h + 1, g0l]
        g0_t = jnp.broadcast_to(g0_ref[h:h + 1, g0l], (L, L)).T

        s_qk = jnp.dot(qc, ktc, preferred_element_type=F32)
        p = (jnp.exp(jnp.where(causal, a_row - g0_t, -jnp.inf)) * s_qk).astype(BF16)
        c_prev = c_refs[h][...]
        q_w = (qc.astype(F32) * jnp.exp(g_prev - g0_t)).astype(BF16)
        nd_refs[h][...] = jnp.dot(jnp.concatenate([q_w, p], axis=1),
                                  jnp.concatenate([c_prev.astype(BF16), v_aug], axis=0),
                                  preferred_element_type=F32)

        kte = (ktc.astype(F32) * jnp.exp(a_row - g_end)).astype(BF16)
        decay = jnp.exp(g_prev - g_end)
        c_refs[h][...] = (jnp.concatenate([decay, decay], axis=1) * c_prev
                          + jnp.dot(kte, v_aug, preferred_element_type=F32))
        return g_end

    def emit(c, h):
        r0 = pl.multiple_of(c * L, L)
        hs = slice(h * dh, (h + 1) * dh)
        nm_t = jnp.broadcast_to(nm_ref[h:h + 1, pl.ds(pl.multiple_of(t0 + r0, L), L)], (L, L)).T
        nd = nd_refs[h][...]
        h_tilde = nd[:, 0:dh] / jnp.maximum(jnp.abs(nd[:, dh:2 * dh]), jnp.exp(nm_t))
        hm = _sigmoid(o_ref[pl.ds(r0, L), hs].astype(F32)) * h_tilde
        mu = jnp.dot(hm.astype(BF16), mean_mat, preferred_element_type=F32)
        hc = hm - mu
        var = jnp.dot((hc * hc).astype(BF16), mean_mat, preferred_element_type=F32)
        out_ref[pl.ds(r0, L), hs] = (hc * lax.rsqrt(var + EPS)).astype(BF16)

    zero = jnp.zeros((1, L), F32)
    g_first = tuple(mix(0, h, zero) for h in range(H))

    def body(c, g_carry):
        for h in range(H):
            emit(c - 1, h)
        return tuple(mix(c, h, g_carry[h]) for h in range(H))

    lax.fori_loop(1, S // L, body, g_first, unroll=5)
    for h in range(H):
        emit(S // L - 1, h)


def _mlstm(q, kt, v, o, gatest, weights, row_scales, *, batch, seq):
    T, dm = q.shape
    dh = dm // N_MLSTM_HEADS
    kern = functools.partial(_mlstm_kernel, chunk=MLSTM_CHUNK, n_cast=len(weights))
    tok = lambda w: pl.BlockSpec((seq, w), lambda b: (b, 0))
    rows = pltpu.VMEM((SUBLANES, T), F32)
    bf16_rows = 2 * SUBLANES
    assert all(w.shape[0] % (batch * bf16_rows) == 0 for w in weights)
    w_specs = [pl.BlockSpec((w.shape[0] // batch, w.shape[1]), lambda b: (b, 0)) for w in weights]
    s_specs = [pl.BlockSpec((w.shape[0] // batch, 1), lambda b: (b, 0)) for w in weights]
    scales = [sc.reshape(-1, 1).astype(F32) for sc in row_scales]
    return pl.pallas_call(
        kern,
        grid=(batch,),
        in_specs=[tok(dm), pl.BlockSpec((dm, seq), lambda b: (0, b)), tok(dm), tok(dm),
                  _const_spec((SUBLANES, T))] + w_specs + s_specs,
        out_specs=[tok(dm)] + w_specs,
        out_shape=[jax.ShapeDtypeStruct((T, dm), BF16)] + [jax.ShapeDtypeStruct(w.shape, BF16) for w in weights],
        scratch_shapes=([rows, rows, rows, rows] + [pltpu.VMEM((dh, 2 * dh), F32)] * N_MLSTM_HEADS
                        + [pltpu.VMEM((MLSTM_CHUNK, 2 * dh), F32)] * N_MLSTM_HEADS),
        compiler_params=pltpu.CompilerParams(dimension_semantics=("arbitrary",),
                                             vmem_limit_bytes=VMEM_LIMIT),
        name="mlstm",
    )(q, kt, v, o, gatest, *weights, *scales)


def _s5_prep_kernel(arow_ref, acol_ref, bre_ref, bim_ref, cre_ref, cim_ref, drow_ref,
                    rep_ref, rep2_ref, selrev_ref, pwcol_ref, w1_ref, wout_ref, al_ref, *, chunk):
    for p in range(arow_ref.shape[0]):
        _s5_prep_pair(p, arow_ref, acol_ref, bre_ref, bim_ref, cre_ref, cim_ref, drow_ref,
                      rep_ref, rep2_ref, selrev_ref, pwcol_ref, w1_ref, wout_ref, al_ref, chunk=chunk)


def _s5_prep_pair(p, arow_ref, acol_ref, bre_ref, bim_ref, cre_ref, cim_ref, drow_ref,
                  rep_ref, rep2_ref, selrev_ref, pwcol_ref, w1_ref, wout_ref, al_ref, *, chunk):
    L = chunk
    P = S5_STATE
    Hc = S5_GROUP
    R = L * Hc
    hi = lax.Precision.HIGHEST

    def select(x, sel_ref):
        x_hi = x.astype(BF16)
        x_lo = (x - x_hi.astype(F32)).astype(BF16)
        sel = sel_ref[...]
        return (jnp.dot(x_hi, sel, preferred_element_type=F32) + jnp.dot(x_lo, sel, preferred_element_type=F32))

    dt_r = jnp.exp(arow_ref[p, 2:3, :])
    da_re, da_im = dt_r * arow_ref[p, 0:1, :], dt_r * arow_ref[p, 1:2, :]

    def abar_pow(tau):
        mag = jnp.exp(tau * da_re)
        ang = tau * da_im
        return mag * jnp.cos(ang), mag * jnp.sin(ang)

    n_pow = -(-(L + 1) // SUBLANES) * SUBLANES
    er_re, er_im = abar_pow(lax.broadcasted_iota(jnp.int32, (n_pow, 1), 0).astype(F32))
    pad = jnp.zeros((2 * P - n_pow, 2 * P), F32)
    ec_re = jnp.concatenate([er_re, pad], axis=0).T
    ec_im = jnp.concatenate([er_im, pad], axis=0).T

    a_re_c, a_im_c = acol_ref[p, :, 0:1], acol_ref[p, :, 1:2]
    ab_re, ab_im = ec_re[:, 1:2], ec_im[:, 1:2]
    den = a_re_c * a_re_c + a_im_c * a_im_c
    nr = ab_re - 1.0
    z_re = (nr * a_re_c + ab_im * a_im_c) / den
    z_im = (ab_im * a_re_c - nr * a_im_c) / den
    bb_re = z_re * bre_ref[p] - z_im * bim_ref[p]
    bb_im = z_re * bim_ref[p] + z_im * bre_ref[p]
    first = lax.broadcasted_iota(jnp.int32, (2 * P, Hc), 0) < P
    bd = lambda b: jnp.concatenate([jnp.where(first, b, 0.0), jnp.where(first, 0.0, b)], axis=1)

    def rows(e, lo):
        return jnp.concatenate([jnp.broadcast_to(e[lo + t:lo + t + 1, :], (Hc, 2 * P)) for t in range(L)], axis=0)

    c_re = jnp.concatenate([cre_ref[p]] * L, axis=0)
    c_im = jnp.concatenate([cim_ref[p]] * L, axis=0)

    def c_times_pow(lo):
        e_re, e_im = rows(er_re, lo), rows(er_im, lo)
        return c_re * e_re - c_im * e_im, c_re * e_im + c_im * e_re

    ce_re, ce_im = c_times_pow(0)
    kflat = (jnp.dot(ce_re, bd(bb_re), preferred_element_type=F32, precision=hi)
             - jnp.dot(ce_im, bd(bb_im), preferred_element_type=F32, precision=hi))
    row = lax.broadcasted_iota(jnp.int32, (R, 2 * Hc), 0)
    col = lax.broadcasted_iota(jnp.int32, (R, 2 * Hc), 1) & (Hc - 1)
    kflat = kflat + jnp.where(row == col, drow_ref[p], 0.0)

    m = select(kflat, rep2_ref)
    blk = (lax.broadcasted_iota(jnp.int32, (R, 2 * R), 1) & (R - 1)) // Hc
    sh = 1
    while sh < L:
        shifted = jnp.concatenate([jnp.zeros((sh * Hc, 2 * R), F32), m[0:R - sh * Hc, :]], axis=0)
        m = jnp.where((blk & sh) != 0, shifted, m)
        sh *= 2

    ev_re, ev_im = select(ec_re, selrev_ref), select(ec_im, selrev_ref)
    bt_re, bt_im = select(bb_re, rep_ref), select(bb_im, rep_ref)
    st_re = (ev_re * bt_re - ev_im * bt_im).astype(BF16)
    st_im = (ev_re * bt_im + ev_im * bt_re).astype(BF16)

    co_re, co_im = c_times_pow(1)
    lane_first = lax.broadcasted_iota(jnp.int32, (R, 2 * P), 1) < P
    for g in range(2):
        w1_ref[2 * p + g, 0:R, :] = m[:, g * R:(g + 1) * R].astype(BF16)
        w1_ref[2 * p + g, R:R + P, :] = st_re[g * P:(g + 1) * P, :]
        w1_ref[2 * p + g, R + P:R + 2 * P, :] = st_im[g * P:(g + 1) * P, :]
        own = lane_first if g == 0 else jnp.logical_not(lane_first)
        wout_ref[2 * p + g, :, 0:2 * P] = jnp.where(own, co_re, 0.0).astype(BF16)
        wout_ref[2 * p + g, :, 2 * P:4 * P] = jnp.where(own, -co_im, 0.0).astype(BF16)

    sc_re, sc_im = abar_pow(pwcol_ref[...])
    al_ref[p, 0:SUBLANES, :] = sc_re
    al_ref[p, SUBLANES:2 * SUBLANES, :] = sc_im


def _s5_prep(a_re, a_im, log_dt, b_re, b_im, c_re, c_im, d_skip, *, chunk, n_chunks):
    G, P = a_re.shape
    Hc = S5_GROUP
    R = chunk * Hc
    assert G % 2 == 0 and 2 * P == LANES
    G2 = G // 2
    n_pairs = math.gcd(G2, S5_PREP_PAIRS)
    ldt = jnp.broadcast_to(log_dt[:, None], (G, P))
    arow = jnp.stack([a_re.reshape(G2, 2 * P), a_im.reshape(G2, 2 * P), ldt.reshape(G2, 2 * P)], axis=1)
    acol = jnp.stack([a_re.reshape(G2, 2 * P), a_im.reshape(G2, 2 * P)], axis=2)
    pair_lanes = lambda c: c.reshape(G2, 2, Hc, P).transpose(0, 2, 1, 3).reshape(G2, Hc, 2 * P)
    drow = d_skip.reshape(G2, 1, 2 * Hc)
    rep = (jnp.arange(R)[None, :] % Hc == jnp.arange(Hc)[:, None]).astype(BF16)
    rep2 = jnp.kron(jnp.eye(2, dtype=BF16), rep)
    selrev = (jnp.arange(LANES)[:, None] == chunk - 1 - jnp.arange(R)[None, :] // Hc).astype(BF16)
    n_steps = max(1, (n_chunks - 1).bit_length())
    assert chunk + 1 <= LANES and n_steps <= SUBLANES
    pwcol = jnp.zeros((SUBLANES, 1), F32).at[:n_steps, 0].set(chunk * 2.0 ** jnp.arange(n_steps))
    kern = functools.partial(_s5_prep_kernel, chunk=chunk)
    pair = lambda *s: pl.BlockSpec((n_pairs,) + s, lambda g: (g,) + (0,) * len(s))
    two = lambda *s: pl.BlockSpec((2 * n_pairs,) + s, lambda g: (g,) + (0,) * len(s))
    return pl.pallas_call(
        kern,
        grid=(G2 // n_pairs,),
        in_specs=[pair(3, 2 * P), pair(2 * P, 2), pair(2 * P, Hc), pair(2 * P, Hc), pair(Hc, 2 * P), pair(Hc, 2 * P),
                  pair(1, 2 * Hc), _const_spec((Hc, R)), _const_spec((2 * Hc, 2 * R)), _const_spec((LANES, R)),
                  _const_spec((SUBLANES, 1))],
        out_specs=[two(R + 2 * P, R), two(R, 4 * P), pair(2 * SUBLANES, 2 * P)],
        out_shape=[jax.ShapeDtypeStruct((G, R + 2 * P, R), BF16),
                   jax.ShapeDtypeStruct((G, R, 4 * P), BF16),
                   jax.ShapeDtypeStruct((G2, 2 * SUBLANES, 2 * P), F32)],
        compiler_params=pltpu.CompilerParams(dimension_semantics=("arbitrary",),
                                             vmem_limit_bytes=VMEM_LIMIT),
        name="s5_prep",
    )(arow, acol, b_re.reshape(G2, 2 * P, Hc), b_im.reshape(G2, 2 * P, Hc), pair_lanes(c_re), pair_lanes(c_im),
      drow, rep, rep2, selrev, pwcol)


def _s5_kernel(*refs, chunk, n_chunks):
    L = chunk
    n_oct = L // SUBLANES
    u_refs, (w1_ref, wout_ref, al_ref, y_ref, ut_ref, yt_ref) = refs[0:n_oct], refs[n_oct:]
    P = S5_STATE
    Hc = S5_GROUP
    R = L * Hc
    C = y_ref.shape[1] // SUBLANES
    n_grp = LANES // Hc

    for s in range(L):
        xt = u_refs[s // SUBLANES][pl.ds(s % SUBLANES, C, stride=SUBLANES), :].astype(BF16).T
        for j in range(n_grp):
            ut_ref[j, s * Hc:(s + 1) * Hc, :] = xt[j * Hc:(j + 1) * Hc, :]

    seg_pos = lax.broadcasted_iota(jnp.int32, (C, 2 * P), 0) & (n_chunks - 1)
    nt = (((1,), (1,)), ((), ()))

    def shifted(x, sh):
        return jnp.where(seg_pos >= sh, pltpu.roll(x, sh, 0), 0.0)

    def group_pair(i, carry):
        j0 = 2 * i
        res = [jnp.dot(w1_ref[j0 + d], ut_ref[j0 + d], preferred_element_type=F32) for d in range(2)]
        x_re = jnp.concatenate([res[0][R:R + P], res[1][R:R + P]], axis=0).T
        x_im = jnp.concatenate([res[0][R + P:R + 2 * P], res[1][R + P:R + 2 * P]], axis=0).T
        sh, k = 1, 0
        while sh < n_chunks:
            a_re = al_ref[i, k:k + 1, :]
            a_im = al_ref[i, SUBLANES + k:SUBLANES + k + 1, :]
            s_re, s_im = shifted(x_re, sh), shifted(x_im, sh)
            x_re, x_im = (x_re + a_re * s_re - a_im * s_im, x_im + a_re * s_im + a_im * s_re)
            sh *= 2
            k += 1
        xprev = jnp.concatenate([shifted(x_re, 1), shifted(x_im, 1)], axis=1).astype(BF16)
        for d in range(2):
            yt_ref[j0 + d] = res[d][0:R, :] + lax.dot_general(wout_ref[j0 + d], xprev, nt,
                                                              preferred_element_type=F32)
        return carry

    lax.fori_loop(0, n_grp // 2, group_pair, 0, unroll=2)

    for t in range(L):
        zt = jnp.concatenate([yt_ref[j, t * Hc:(t + 1) * Hc, :] for j in range(n_grp)], axis=0)
        y_ref[t // SUBLANES, pl.ds(t % SUBLANES, C, stride=SUBLANES), :] = zt.T


def _s5(u, w1, wout, al, *, chunk, n_chunks, n_col_blocks):
    n_oct, t_oct, ds5 = u.shape
    T = n_oct * t_oct
    assert n_oct * SUBLANES == chunk
    G, R1, R = w1.shape
    P4 = wout.shape[-1]
    n_grp = LANES // S5_GROUP
    n_q = ds5 // LANES
    tb = T // n_col_blocks
    C = tb // chunk
    kern = functools.partial(_s5_kernel, chunk=chunk, n_chunks=n_chunks)
    grp = lambda *s: pl.BlockSpec((n_grp,) + s, lambda q, i: (q,) + (0,) * len(s))
    octets = [pl.BlockSpec((None, C * SUBLANES, LANES), functools.partial(lambda q, i, o: (o, i, q), o=o))
              for o in range(n_oct)]
    return pl.pallas_call(
        kern,
        grid=(n_q, n_col_blocks),
        in_specs=octets + [grp(R1, R), grp(R, P4),
                           pl.BlockSpec((n_grp // 2, 2 * SUBLANES, P4 // 2), lambda q, i: (q, 0, 0))],
        out_specs=pl.BlockSpec((n_oct, C * SUBLANES, LANES), lambda q, i: (0, i, q)),
        out_shape=jax.ShapeDtypeStruct((n_oct, t_oct, ds5), F32),
        scratch_shapes=[pltpu.VMEM((n_grp, R, C), BF16), pltpu.VMEM((n_grp, R, C), F32)],
        compiler_params=pltpu.CompilerParams(dimension_semantics=("arbitrary", "arbitrary"),
                                             vmem_limit_bytes=BIG_VMEM_LIMIT),
        name="s5",
    )(*([u] * n_oct), w1, wout, al)


def _out_kernel(x_ref, hm_ref, y_ref, wglu_hbm, bglu_ref, wo_hbm, wgate_hbm, wup_hbm, wdown_hbm, gf_ref,
                out_ref, acc_ref, xn_ref, wglu_ref, wo_ref, wgate_ref, wup_ref, wdown_ref, sems,
                *, ff_chunk, final_norm):
    dm = hm_ref.shape[1]
    copies = [pltpu.make_async_copy(src, dst, sems.at[n]) for n, (src, dst) in enumerate(
        [(wglu_hbm, wglu_ref), (wo_hbm, wo_ref), (wgate_hbm, wgate_ref), (wup_hbm, wup_ref), (wdown_hbm, wdown_ref)])]
    first_step = pl.program_id(0) == 0

    @pl.when(first_step)
    def _():
        for cp in copies:
            cp.start(priority=1)
        copies[0].wait()
        copies[1].wait()

    n_oct, rows_oct, ds5 = y_ref.shape
    y = jnp.stack([y_ref[oc].reshape(rows_oct // SUBLANES, SUBLANES, ds5) for oc in range(n_oct)], axis=1)
    g = jax.nn.gelu(y.reshape(n_oct * rows_oct, ds5))
    z = jnp.dot(g.astype(BF16), wglu_ref[...], preferred_element_type=F32) + bglu_ref[...]
    gg = g * _sigmoid(z)
    hs = (gg * lax.rsqrt(jnp.mean(gg * gg, axis=-1, keepdims=True) + EPS)).astype(BF16)
    h1 = (x_ref[...] + jnp.dot(hm_ref[...], wo_ref[0:dm, :], preferred_element_type=F32)
          + jnp.dot(hs, wo_ref[dm:, :], preferred_element_type=F32))
    xn_ref[...] = (h1 * lax.rsqrt(jnp.mean(h1 * h1, axis=-1, keepdims=True) + EPS)).astype(BF16)
    acc_ref[...] = h1

    @pl.when(first_step)
    def _():
        for cp in copies[2:]:
            cp.wait()

    def ffn_chunk(c, carry):
        cs = pl.ds(pl.multiple_of(c * ff_chunk, ff_chunk), ff_chunk)
        xn2 = xn_ref[...]
        gate = jnp.dot(xn2, wgate_ref[:, cs], preferred_element_type=F32)
        up = jnp.dot(xn2, wup_ref[:, cs], preferred_element_type=F32)
        act = (_silu(gate) * up).astype(BF16)
        acc_ref[...] += jnp.dot(act, wdown_ref[cs, :], preferred_element_type=F32)
        return carry

    lax.fori_loop(0, wgate_ref.shape[1] // ff_chunk, ffn_chunk, 0, unroll=4)
    out_ref[...] = _rmsnorm(acc_ref[...], gf_ref[...]) if final_norm else acc_ref[...]


def _out_block(x2, hm, y, wglu, bglu, wo, wgate, wup, wdown, gf, *, tm, ff_chunk, final_norm):
    T, D = x2.shape
    dm = hm.shape[1]
    n_oct, _, ds5 = y.shape
    assert wgate.shape[1] % ff_chunk == 0
    kern = functools.partial(_out_kernel, ff_chunk=ff_chunk, final_norm=final_norm)
    tok = lambda w: pl.BlockSpec((tm, w), lambda i: (i, 0))
    in_hbm = pl.BlockSpec(memory_space=pl.ANY)
    weights = (wglu, wo, wgate, wup, wdown)
    return pl.pallas_call(
        kern,
        grid=(T // tm,),
        in_specs=[tok(D), tok(dm), pl.BlockSpec((n_oct, tm // n_oct, ds5), lambda i: (0, i, 0)),
                  in_hbm, _const_spec((1, ds5)), in_hbm, in_hbm, in_hbm, in_hbm, _const_spec((1, D))],
        out_specs=tok(D),
        out_shape=jax.ShapeDtypeStruct((T, D), F32),
        scratch_shapes=([pltpu.VMEM((tm, D), F32), pltpu.VMEM((tm, D), BF16)]
                        + [pltpu.VMEM(w.shape, w.dtype) for w in weights]
                        + [pltpu.SemaphoreType.DMA((len(weights),))]),
        compiler_params=pltpu.CompilerParams(dimension_semantics=("arbitrary",),
                                             vmem_limit_bytes=BIG_VMEM_LIMIT),
        name="out_block",
    )(x2, hm, y, wglu, bglu, wo, wgate, wup, wdown, gf)


def kernel(x, norm1_g, w_in, if_bias, conv_qk, mlstm_norm_g, a_re, a_im, log_dt, b_re, b_im, c_re, c_im,
           d_skip, w_glu, b_glu, s5_norm_g, w_out, norm2_g, w_gate, w_up, w_down, norm_f_g):
    B, S, D = x.shape
    depth = w_in.shape[0]
    H = N_MLSTM_HEADS
    dm = mlstm_norm_g.shape[1]
    T = B * S
    L5 = S5_CHUNK
    nc5 = S // L5
    row = lambda a: a.reshape(1, -1).astype(F32)

    h = x.reshape(T, D)
    for l in range(depth):
        last = l == depth - 1
        gb = jnp.zeros((1, LANES), F32).at[0, 0:2 * H].set(if_bias[l])

        q, kt, v, o, gatest, u = _inproj(h, row(norm1_g[l]), w_in[l].T.astype(F32), gb, conv_qk[l].astype(F32),
                                         seq=S, tm=min(INPROJ_TM, S), d_mlstm=dm, n_oct=L5 // SUBLANES)
        ones = lambda n: jnp.ones((n,), F32)
        g_mix = jnp.concatenate([mlstm_norm_g[l], s5_norm_g[l]]).astype(F32)
        g_ffn = norm2_g[l].astype(F32)
        hm, wglu, wo, wgate, wup, wdown = _mlstm(
            q, kt, v, o, gatest,
            [w.astype(F32) for w in (w_glu[l], w_out[l], w_gate[l], w_up[l], w_down[l])],
            [ones(w_glu.shape[1]), g_mix, g_ffn, g_ffn, ones(w_down.shape[1])], batch=B, seq=S)

        w1, wout, al = _s5_prep(a_re[l], a_im[l], log_dt[l], b_re[l], b_im[l], c_re[l], c_im[l],
                                d_skip[l], chunk=L5, n_chunks=nc5)
        y = _s5(u, w1, wout, al, chunk=L5, n_chunks=nc5, n_col_blocks=S5_COL_BLOCKS)

        h = _out_block(h, hm, y, wglu, row(b_glu[l]), wo, wgate, wup, wdown, row(norm_f_g),
                       tm=OUT_TM, ff_chunk=FF_CHUNK, final_norm=last)
    return h.reshape(B, S, D)
```

```python
import functools
import math

import jax
import jax.numpy as jnp
from jax import lax
from jax.experimental import pallas as pl
from jax.experimental.pallas import tpu as pltpu

EPS = 1e-6
N_MLSTM_HEADS = 4
CONV_WIDTH = 4
S5_GROUP = 16
S5_STATE = 64

LANES = 128
SUBLANES = 8
MLSTM_CHUNK = 128
S5_CHUNK = 16
S5_COL_BLOCKS = 1
S5_PREP_PAIRS = 4
INPROJ_TM = 1024
OUT_TM = 1024
FF_CHUNK = 256
VMEM_LIMIT = 48 * 1024 * 1024
BIG_VMEM_LIMIT = 58 * 1024 * 1024

F32 = jnp.float32
BF16 = jnp.bfloat16


def _rmsnorm(x, g):
    ms = jnp.mean(x * x, axis=-1, keepdims=True)
    return x * lax.rsqrt(ms + EPS) * g


def _sigmoid(x):
    return 0.5 * jnp.tanh(0.5 * x) + 0.5


def _silu(x):
    h = 0.5 * x
    return h + h * jnp.tanh(h)


def _log_sigmoid(x):
    return -(jnp.maximum(-x, 0.0) + jnp.log1p(jnp.exp(-jnp.abs(x))))


def _const_spec(shape):
    return pl.BlockSpec(shape, lambda *_: (0,) * len(shape), pipeline_mode=pl.Buffered(1))


def _inproj_kernel(x_ref, g1_ref, wt_hbm, gb_ref, cw_ref,
                   q_ref, kt_ref, v_ref, o_ref, gatest_ref, u_ref,
                   ext_ref, wm_ref, wg_ref, wu_ref, wt_ref, sems, *, tiles_per_seq, d_mlstm, k_scale):
    tm = x_ref.shape[0]
    dm = d_mlstm
    H = N_MLSTM_HEADS
    halo = SUBLANES

    @pl.when(pl.program_id(0) == 0)
    def _():
        n_rows = wt_ref.shape[0]
        slabs = [(c0, dm) for c0 in range(0, 4 * dm, dm)] + [(4 * dm + 2 * H, n_rows - 4 * dm - 2 * H),
                                                             (4 * dm, 2 * H)]
        copies = [pltpu.make_async_copy(wt_hbm.at[pl.ds(r0, n)], wt_ref.at[pl.ds(r0, n)], sems.at[k])
                  for k, (r0, n) in enumerate(slabs)]
        for cp in copies:
            cp.start(priority=1)
        g1 = g1_ref[...]
        for k, c0 in enumerate(range(0, 4 * dm, dm)):
            copies[k].wait()
            wm_ref[:, c0:c0 + dm] = (wt_ref[c0:c0 + dm, :] * g1).T.astype(BF16)
        copies[4].wait()
        wu_ref[...] = (wt_ref[4 * dm + 2 * H:, :] * g1).T.astype(BF16)
        copies[5].wait()
        fill = jnp.zeros((LANES - 2 * H, wt_ref.shape[1]), F32)
        wg_ref[...] = jnp.concatenate([wt_ref[4 * dm:4 * dm + 2 * H, :] * g1, fill], axis=0).T.astype(BF16)

    x = x_ref[...]
    xn = (x * lax.rsqrt(jnp.mean(x * x, axis=-1, keepdims=True) + EPS)).astype(BF16)

    @pl.when(pl.program_id(0) % tiles_per_seq == 0)
    def _():
        ext_ref[0:halo, :] = jnp.zeros((halo, 2 * dm), F32)

    ext_ref[halo:halo + tm, :] = jnp.dot(xn, wm_ref[:, 0:2 * dm], preferred_element_type=F32)
    v = jnp.dot(xn, wm_ref[:, 2 * dm:3 * dm], preferred_element_type=F32)
    o = jnp.dot(xn, wm_ref[:, 3 * dm:4 * dm], preferred_element_type=F32)
    u = jnp.dot(xn, wu_ref[...], preferred_element_type=F32)
    gates = jnp.dot(xn, wg_ref[...], preferred_element_type=F32) + gb_ref[...]
    v_ref[...] = v.astype(BF16)
    o_ref[...] = o.astype(BF16)
    n_oct = u_ref.shape[0]
    u4 = u.reshape(tm // (n_oct * SUBLANES), n_oct, SUBLANES, u.shape[1])
    for oc in range(n_oct):
        u_ref[oc] = u4[:, oc].reshape(tm // n_oct, u.shape[1])
    gatest_ref[...] = gates.T[0:2 * H, :]

    def zero_of(t):
        w = pltpu.bitcast(t[tm - SUBLANES:tm, t.shape[1] - LANES:], jnp.uint32)
        return pltpu.bitcast((w >> 16) >> 16, F32)
    anchor = zero_of(v)

    acc = cw_ref[CONV_WIDTH - 1:CONV_WIDTH, :] * ext_ref[halo:halo + tm, :]
    for j in range(1, CONV_WIDTH):
        acc = acc + cw_ref[CONV_WIDTH - 1 - j:CONV_WIDTH - j, :] * ext_ref[halo - j:halo - j + tm, :]
    ext_ref[0:halo, :] = ext_ref[tm:tm + halo, :]
    qk = _silu(acc) + jnp.tile(anchor, (tm // SUBLANES, 2 * dm // LANES))
    q_ref[...] = qk[:, 0:dm].astype(BF16)
    kt_ref[...] = (qk[:, dm:2 * dm] * k_scale).T.astype(BF16)


def _inproj(x2, g1, w_in_t, gb, cw, *, seq, tm, d_mlstm, n_oct):
    T, D = x2.shape
    dm = d_mlstm
    H = N_MLSTM_HEADS
    assert 2 * H == SUBLANES
    ds5 = w_in_t.shape[0] - 4 * dm - 2 * H
    kern = functools.partial(_inproj_kernel, tiles_per_seq=seq // tm, d_mlstm=dm,
                             k_scale=float((dm // H) ** -0.5))
    tok = lambda w: pl.BlockSpec((tm, w), lambda i: (i, 0))
    return pl.pallas_call(
        kern,
        grid=(T // tm,),
        in_specs=[tok(D), _const_spec((1, D)), pl.BlockSpec(memory_space=pl.ANY), _const_spec((1, LANES)),
                  _const_spec(cw.shape)],
        out_specs=[tok(dm), pl.BlockSpec((dm, tm), lambda i: (0, i)), tok(dm), tok(dm),
                   pl.BlockSpec((2 * H, tm), lambda i: (0, i)),
                   pl.BlockSpec((n_oct, tm // n_oct, ds5), lambda i: (0, i, 0))],
        out_shape=[jax.ShapeDtypeStruct((T, dm), BF16), jax.ShapeDtypeStruct((dm, T), BF16),
                   jax.ShapeDtypeStruct((T, dm), BF16), jax.ShapeDtypeStruct((T, dm), BF16),
                   jax.ShapeDtypeStruct((2 * H, T), F32),
                   jax.ShapeDtypeStruct((n_oct, T // n_oct, ds5), F32)],
        scratch_shapes=[pltpu.VMEM((tm + 2 * SUBLANES, 2 * dm), F32), pltpu.VMEM((D, 4 * dm), BF16),
                        pltpu.VMEM((D, LANES), BF16), pltpu.VMEM((D, ds5), BF16),
                        pltpu.VMEM(w_in_t.shape, F32), pltpu.SemaphoreType.DMA((6,))],
        compiler_params=pltpu.CompilerParams(dimension_semantics=("arbitrary",),
                                             vmem_limit_bytes=VMEM_LIMIT),
        name="inproj",
    )(x2, g1, w_in_t, gb, cw)


def _lane_scan(x, op, identity, seg):
    pos = lax.broadcasted_iota(jnp.int32, x.shape, x.ndim - 1) & (seg - 1)
    sh = 1
    while sh < seg:
        x = op(x, jnp.where(pos >= sh, pltpu.roll(x, sh, x.ndim - 1), identity))
        sh *= 2
    return x


def _mlstm_kernel(q_ref, kt_ref, v_ref, o_ref, gatest_ref, *refs, chunk, n_cast):
    S = q_ref.shape[0]
    H = N_MLSTM_HEADS
    dh = q_ref.shape[1] // H
    L = chunk
    assert L == dh == LANES
    w32_refs, ws_refs = refs[0:n_cast], refs[n_cast:2 * n_cast]
    out_ref, w16_refs = refs[2 * n_cast], refs[2 * n_cast + 1:3 * n_cast + 1]
    a_ref, g0_ref, nm_ref, gend_ref = refs[3 * n_cast + 1:3 * n_cast + 5]
    c_refs, nd_refs = refs[3 * n_cast + 5:3 * n_cast + 5 + H], refs[3 * n_cast + 5 + H:]

    for w32_ref, ws_ref, w16_ref in zip(w32_refs, ws_refs, w16_refs):
        w16_ref[...] = (w32_ref[...] * ws_ref[...]).astype(BF16)

    @pl.when(pl.program_id(0) == 0)
    def _():
        gates = gatest_ref[...]
        f_cum = pltpu.roll(_lane_scan(_log_sigmoid(gates), jnp.add, 0.0, S), H, 0)
        a = gates - f_cum
        g0 = jnp.maximum(_lane_scan(a, jnp.maximum, -jnp.inf, S), 0.0)
        a_ref[...] = a
        g0_ref[...] = g0
        nm_ref[...] = -(f_cum + g0)
        for c in range(gatest_ref.shape[1] // L):
            gend_ref[:, c * L:(c + 1) * L] = jnp.broadcast_to(g0[:, (c + 1) * L - 1:(c + 1) * L], (SUBLANES, L))

    t0 = pl.program_id(0) * S

    for c_ref in c_refs:
        c_ref[...] = jnp.zeros(c_ref.shape, F32)
    causal = (lax.broadcasted_iota(jnp.int32, (L, L), 0) >= lax.broadcasted_iota(jnp.int32, (L, L), 1))
    ones_blk = jnp.ones((L, dh), BF16)
    mean_mat = jnp.full((dh, dh), 1.0 / dh, BF16)

    def mix(c, h, g_prev):
        r0 = pl.multiple_of(c * L, L)
        hs = slice(h * dh, (h + 1) * dh)
        qc = q_ref[pl.ds(r0, L), hs]
        ktc = kt_ref[hs, pl.ds(r0, L)]
        v_aug = jnp.concatenate([v_ref[pl.ds(r0, L), hs], ones_blk], axis=1)
        g0l = pl.ds(pl.multiple_of(t0 + r0, L), L)
        a_row = a_ref[h:h + 1, g0l]
        g_end = gend_ref[h:h + 1, g0l]
        g0_t = jnp.broadcast_to(g0_ref[h:h + 1, g0l], (L, L)).T

        s_qk = jnp.dot(qc, ktc, preferred_element_type=F32)
        p = (jnp.exp(jnp.where(causal, a_row - g0_t, -jnp.inf)) * s_qk).astype(BF16)
        c_prev = c_refs[h][...]
        q_w = (qc.astype(F32) * jnp.exp(g_prev - g0_t)).astype(BF16)
        nd_refs[h][...] = jnp.dot(jnp.concatenate([q_w, p], axis=1),
                                  jnp.concatenate([c_prev.astype(BF16), v_aug], axis=0),
                                  preferred_element_type=F32)

        kte = (ktc.astype(F32) * jnp.exp(a_row - g_end)).astype(BF16)
        decay = jnp.exp(g_prev - g_end)
        c_refs[h][...] = (jnp.concatenate([decay, decay], axis=1) * c_prev
                          + jnp.dot(kte, v_aug, preferred_element_type=F32))
        return g_end

    def emit(c, h):
        r0 = pl.multiple_of(c * L, L)
        hs = slice(h * dh, (h + 1) * dh)
        nm_t = jnp.broadcast_to(nm_ref[h:h + 1, pl.ds(pl.multiple_of(t0 + r0, L), L)], (L, L)).T
        nd = nd_refs[h][...]
        h_tilde = nd[:, 0:dh] / jnp.maximum(jnp.abs(nd[:, dh:2 * dh]), jnp.exp(nm_t))
        hm = _sigmoid(o_ref[pl.ds(r0, L), hs].astype(F32)) * h_tilde
        mu = jnp.dot(hm.astype(BF16), mean_mat, preferred_element_type=F32)
        hc = hm - mu
        var = jnp.dot((hc * hc).astype(BF16), mean_mat, preferred_element_type=F32)
        out_ref[pl.ds(r0, L), hs] = (hc * lax.rsqrt(var + EPS)).astype(BF16)

    zero = jnp.zeros((1, L), F32)
    g_first = tuple(mix(0, h, zero) for h in range(H))

    def body(c, g_carry):
        for h in range(H):
            emit(c - 1, h)
        return tuple(mix(c, h, g_carry[h]) for h in range(H))

    lax.fori_loop(1, S // L, body, g_first, unroll=5)
    for h in range(H):
        emit(S // L - 1, h)


def _mlstm(q, kt, v, o, gatest, weights, row_scales, *, batch, seq):
    T, dm = q.shape
    dh = dm // N_MLSTM_HEADS
    kern = functools.partial(_mlstm_kernel, chunk=MLSTM_CHUNK, n_cast=len(weights))
    tok = lambda w: pl.BlockSpec((seq, w), lambda b: (b, 0))
    rows = pltpu.VMEM((SUBLANES, T), F32)
    bf16_rows = 2 * SUBLANES
    assert all(w.shape[0] % (batch * bf16_rows) == 0 for w in weights)
    w_specs = [pl.BlockSpec((w.shape[0] // batch, w.shape[1]), lambda b: (b, 0)) for w in weights]
    s_specs = [pl.BlockSpec((w.shape[0] // batch, 1), lambda b: (b, 0)) for w in weights]
    scales = [sc.reshape(-1, 1).astype(F32) for sc in row_scales]
    return pl.pallas_call(
        kern,
        grid=(batch,),
        in_specs=[tok(dm), pl.BlockSpec((dm, seq), lambda b: (0, b)), tok(dm), tok(dm),
                  _const_spec((SUBLANES, T))] + w_specs + s_specs,
        out_specs=[tok(dm)] + w_specs,
        out_shape=[jax.ShapeDtypeStruct((T, dm), BF16)] + [jax.ShapeDtypeStruct(w.shape, BF16) for w in weights],
        scratch_shapes=([rows, rows, rows, rows] + [pltpu.VMEM((dh, 2 * dh), F32)] * N_MLSTM_HEADS
                        + [pltpu.VMEM((MLSTM_CHUNK, 2 * dh), F32)] * N_MLSTM_HEADS),
        compiler_params=pltpu.CompilerParams(dimension_semantics=("arbitrary",),
                                             vmem_limit_bytes=VMEM_LIMIT),
        name="mlstm",
    )(q, kt, v, o, gatest, *weights, *scales)


def _s5_prep_kernel(arow_ref, acol_ref, bre_ref, bim_ref, cre_ref, cim_ref, drow_ref,
                    rep_ref, rep2_ref, selrev_ref, pwcol_ref, w1_ref, wout_ref, al_ref, *, chunk):
    for p in range(arow_ref.shape[0]):
        _s5_prep_pair(p, arow_ref, acol_ref, bre_ref, bim_ref, cre_ref, cim_ref, drow_ref,
                      rep_ref, rep2_ref, selrev_ref, pwcol_ref, w1_ref, wout_ref, al_ref, chunk=chunk)


def _s5_prep_pair(p, arow_ref, acol_ref, bre_ref, bim_ref, cre_ref, cim_ref, drow_ref,
                  rep_ref, rep2_ref, selrev_ref, pwcol_ref, w1_ref, wout_ref, al_ref, *, chunk):
    L = chunk
    P = S5_STATE
    Hc = S5_GROUP
    R = L * Hc
    hi = lax.Precision.HIGHEST

    def select(x, sel_ref):
        x_hi = x.astype(BF16)
        x_lo = (x - x_hi.astype(F32)).astype(BF16)
        sel = sel_ref[...]
        return (jnp.dot(x_hi, sel, preferred_element_type=F32) + jnp.dot(x_lo, sel, preferred_element_type=F32))

    dt_r = jnp.exp(arow_ref[p, 2:3, :])
    da_re, da_im = dt_r * arow_ref[p, 0:1, :], dt_r * arow_ref[p, 1:2, :]

    def abar_pow(tau):
        mag = jnp.exp(tau * da_re)
        ang = tau * da_im
        return mag * jnp.cos(ang), mag * jnp.sin(ang)

    n_pow = -(-(L + 1) // SUBLANES) * SUBLANES
    er_re, er_im = abar_pow(lax.broadcasted_iota(jnp.int32, (n_pow, 1), 0).astype(F32))
    pad = jnp.zeros((2 * P - n_pow, 2 * P), F32)
    ec_re = jnp.concatenate([er_re, pad], axis=0).T
    ec_im = jnp.concatenate([er_im, pad], axis=0).T

    a_re_c, a_im_c = acol_ref[p, :, 0:1], acol_ref[p, :, 1:2]
    ab_re, ab_im = ec_re[:, 1:2], ec_im[:, 1:2]
    den = a_re_c * a_re_c + a_im_c * a_im_c
    nr = ab_re - 1.0
    z_re = (nr * a_re_c + ab_im * a_im_c) / den
    z_im = (ab_im * a_re_c - nr * a_im_c) / den
    bb_re = z_re * bre_ref[p] - z_im * bim_ref[p]
    bb_im = z_re * bim_ref[p] + z_im * bre_ref[p]
    first = lax.broadcasted_iota(jnp.int32, (2 * P, Hc), 0) < P
    bd = lambda b: jnp.concatenate([jnp.where(first, b, 0.0), jnp.where(first, 0.0, b)], axis=1)

    def rows(e, lo):
        return jnp.concatenate([jnp.broadcast_to(e[lo + t:lo + t + 1, :], (Hc, 2 * P)) for t in range(L)], axis=0)

    c_re = jnp.concatenate([cre_ref[p]] * L, axis=0)
    c_im = jnp.concatenate([cim_ref[p]] * L, axis=0)

    def c_times_pow(lo):
        e_re, e_im = rows(er_re, lo), rows(er_im, lo)
        return c_re * e_re - c_im * e_im, c_re * e_im + c_im * e_re

    ce_re, ce_im = c_times_pow(0)
    kflat = (jnp.dot(ce_re, bd(bb_re), preferred_element_type=F32, precision=hi)
             - jnp.dot(ce_im, bd(bb_im), preferred_element_type=F32, precision=hi))
    row = lax.broadcasted_iota(jnp.int32, (R, 2 * Hc), 0)
    col = lax.broadcasted_iota(jnp.int32, (R, 2 * Hc), 1) & (Hc - 1)
    kflat = kflat + jnp.where(row == col, drow_ref[p], 0.0)

    m = select(kflat, rep2_ref)
    blk = (lax.broadcasted_iota(jnp.int32, (R, 2 * R), 1) & (R - 1)) // Hc
    sh = 1
    while sh < L:
        shifted = jnp.concatenate([jnp.zeros((sh * Hc, 2 * R), F32), m[0:R - sh * Hc, :]], axis=0)
        m = jnp.where((blk & sh) != 0, shifted, m)
        sh *= 2

    ev_re, ev_im = select(ec_re, selrev_ref), select(ec_im, selrev_ref)
    bt_re, bt_im = select(bb_re, rep_ref), select(bb_im, rep_ref)
    st_re = (ev_re * bt_re - ev_im * bt_im).astype(BF16)
    st_im = (ev_re * bt_im + ev_im * bt_re).astype(BF16)

    co_re, co_im = c_times_pow(1)
    lane_first = lax.broadcasted_iota(jnp.int32, (R, 2 * P), 1) < P
    for g in range(2):
        w1_ref[2 * p + g, 0:R, :] = m[:, g * R:(g + 1) * R].astype(BF16)
        w1_ref[2 * p + g, R:R + P, :] = st_re[g * P:(g + 1) * P, :]
        w1_ref[2 * p + g, R + P:R + 2 * P, :] = st_im[g * P:(g + 1) * P, :]
        own = lane_first if g == 0 else jnp.logical_not(lane_first)
        wout_ref[2 * p + g, :, 0:2 * P] = jnp.where(own, co_re, 0.0).astype(BF16)
        wout_ref[2 * p + g, :, 2 * P:4 * P] = jnp.where(own, -co_im, 0.0).astype(BF16)

    sc_re, sc_im = abar_pow(pwcol_ref[...])
    al_ref[p, 0:SUBLANES, :] = sc_re
    al_ref[p, SUBLANES:2 * SUBLANES, :] = sc_im


def _s5_prep(a_re, a_im, log_dt, b_re, b_im, c_re, c_im, d_skip, *, chunk, n_chunks):
    G, P = a_re.shape
    Hc = S5_GROUP
    R = chunk * Hc
    assert G % 2 == 0 and 2 * P == LANES
    G2 = G // 2
    n_pairs = math.gcd(G2, S5_PREP_PAIRS)
    ldt = jnp.broadcast_to(log_dt[:, None], (G, P))
    arow = jnp.stack([a_re.reshape(G2, 2 * P), a_im.reshape(G2, 2 * P), ldt.reshape(G2, 2 * P)], axis=1)
    acol = jnp.stack([a_re.reshape(G2, 2 * P), a_im.reshape(G2, 2 * P)], axis=2)
    pair_lanes = lambda c: c.reshape(G2, 2, Hc, P).transpose(0, 2, 1, 3).reshape(G2, Hc, 2 * P)
    drow = d_skip.reshape(G2, 1, 2 * Hc)
    rep = (jnp.arange(R)[None, :] % Hc == jnp.arange(Hc)[:, None]).astype(BF16)
    rep2 = jnp.kron(jnp.eye(2, dtype=BF16), rep)
    selrev = (jnp.arange(LANES)[:, None] == chunk - 1 - jnp.arange(R)[None, :] // Hc).astype(BF16)
    n_steps = max(1, (n_chunks - 1).bit_length())
    assert chunk + 1 <= LANES and n_steps <= SUBLANES
    pwcol = jnp.zeros((SUBLANES, 1), F32).at[:n_steps, 0].set(chunk * 2.0 ** jnp.arange(n_steps))
    kern = functools.partial(_s5_prep_kernel, chunk=chunk)
    pair = lambda *s: pl.BlockSpec((n_pairs,) + s, lambda g: (g,) + (0,) * len(s))
    two = lambda *s: pl.BlockSpec((2 * n_pairs,) + s, lambda g: (g,) + (0,) * len(s))
    return pl.pallas_call(
        kern,
        grid=(G2 // n_pairs,),
        in_specs=[pair(3, 2 * P), pair(2 * P, 2), pair(2 * P, Hc), pair(2 * P, Hc), pair(Hc, 2 * P), pair(Hc, 2 * P),
                  pair(1, 2 * Hc), _const_spec((Hc, R)), _const_spec((2 * Hc, 2 * R)), _const_spec((LANES, R)),
                  _const_spec((SUBLANES, 1))],
        out_specs=[two(R + 2 * P, R), two(R, 4 * P), pair(2 * SUBLANES, 2 * P)],
        out_shape=[jax.ShapeDtypeStruct((G, R + 2 * P, R), BF16),
                   jax.ShapeDtypeStruct((G, R, 4 * P), BF16),
                   jax.ShapeDtypeStruct((G2, 2 * SUBLANES, 2 * P), F32)],
        compiler_params=pltpu.CompilerParams(dimension_semantics=("arbitrary",),
                                             vmem_limit_bytes=VMEM_LIMIT),
        name="s5_prep",
    )(arow, acol, b_re.reshape(G2, 2 * P, Hc), b_im.reshape(G2, 2 * P, Hc), pair_lanes(c_re), pair_lanes(c_im),
      drow, rep, rep2, selrev, pwcol)


def _s5_kernel(*refs, chunk, n_chunks):
    L = chunk
    n_oct = L // SUBLANES
    u_refs, (w1_ref, wout_ref, al_ref, y_ref, ut_ref, yt_ref) = refs[0:n_oct], refs[n_oct:]
    P = S5_STATE
    Hc = S5_GROUP
    R = L * Hc
    C = y_ref.shape[1] // SUBLANES
    n_grp = LANES // Hc

    for s in range(L):
        xt = u_refs[s // SUBLANES][pl.ds(s % SUBLANES, C, stride=SUBLANES), :].astype(BF16).T
        for j in range(n_grp):
            ut_ref[j, s * Hc:(s + 1) * Hc, :] = xt[j * Hc:(j + 1) * Hc, :]

    seg_pos = lax.broadcasted_iota(jnp.int32, (C, 2 * P), 0) & (n_chunks - 1)
    nt = (((1,), (1,)), ((), ()))

    def shifted(x, sh):
        return jnp.where(seg_pos >= sh, pltpu.roll(x, sh, 0), 0.0)

    def group_pair(i, carry):
        j0 = 2 * i
        res = [jnp.dot(w1_ref[j0 + d], ut_ref[j0 + d], preferred_element_type=F32) for d in range(2)]
        x_re = jnp.concatenate([res[0][R:R + P], res[1][R:R + P]], axis=0).T
        x_im = jnp.concatenate([res[0][R + P:R + 2 * P], res[1][R + P:R + 2 * P]], axis=0).T
        sh, k = 1, 0
        while sh < n_chunks:
            a_re = al_ref[i, k:k + 1, :]
            a_im = al_ref[i, SUBLANES + k:SUBLANES + k + 1, :]
            s_re, s_im = shifted(x_re, sh), shifted(x_im, sh)
            x_re, x_im = (x_re + a_re * s_re - a_im * s_im, x_im + a_re * s_im + a_im * s_re)
            sh *= 2
            k += 1
        xprev = jnp.concatenate([shifted(x_re, 1), shifted(x_im, 1)], axis=1).astype(BF16)
        for d in range(2):
            yt_ref[j0 + d] = res[d][0:R, :] + lax.dot_general(wout_ref[j0 + d], xprev, nt,
                                                              preferred_element_type=F32)
        return carry

    lax.fori_loop(0, n_grp // 2, group_pair, 0, unroll=2)

    for t in range(L):
        zt = jnp.concatenate([yt_ref[j, t * Hc:(t + 1) * Hc, :] for j in range(n_grp)], axis=0)
        y_ref[t // SUBLANES, pl.ds(t % SUBLANES, C, stride=SUBLANES), :] = zt.T


def _s5(u, w1, wout, al, *, chunk, n_chunks, n_col_blocks):
    n_oct, t_oct, ds5 = u.shape
    T = n_oct * t_oct
    assert n_oct * SUBLANES == chunk
    G, R1, R = w1.shape
    P4 = wout.shape[-1]
    n_grp = LANES // S5_GROUP
    n_q = ds5 // LANES
    tb = T // n_col_blocks
    C = tb // chunk
    kern = functools.partial(_s5_kernel, chunk=chunk, n_chunks=n_chunks)
    grp = lambda *s: pl.BlockSpec((n_grp,) + s, lambda q, i: (q,) + (0,) * len(s))
    octets = [pl.BlockSpec((None, C * SUBLANES, LANES), functools.partial(lambda q, i, o: (o, i, q), o=o))
              for o in range(n_oct)]
    return pl.pallas_call(
        kern,
        grid=(n_q, n_col_blocks),
        in_specs=octets + [grp(R1, R), grp(R, P4),
                           pl.BlockSpec((n_grp // 2, 2 * SUBLANES, P4 // 2), lambda q, i: (q, 0, 0))],
        out_specs=pl.BlockSpec((n_oct, C * SUBLANES, LANES), lambda q, i: (0, i, q)),
        out_shape=jax.ShapeDtypeStruct((n_oct, t_oct, ds5), F32),
        scratch_shapes=[pltpu.VMEM((n_grp, R, C), BF16), pltpu.VMEM((n_grp, R, C), F32)],
        compiler_params=pltpu.CompilerParams(dimension_semantics=("arbitrary", "arbitrary"),
                                             vmem_limit_bytes=BIG_VMEM_LIMIT),
        name="s5",
    )(*([u] * n_oct), w1, wout, al)


def _out_kernel(x_ref, hm_ref, y_ref, wglu_hbm, bglu_ref, wo_hbm, wgate_hbm, wup_hbm, wdown_hbm, gf_ref,
                out_ref, acc_ref, xn_ref, wglu_ref, wo_ref, wgate_ref, wup_ref, wdown_ref, sems,
                *, ff_chunk, final_norm):
    dm = hm_ref.shape[1]
    copies = [pltpu.make_async_copy(src, dst, sems.at[n]) for n, (src, dst) in enumerate(
        [(wglu_hbm, wglu_ref), (wo_hbm, wo_ref), (wgate_hbm, wgate_ref), (wup_hbm, wup_ref), (wdown_hbm, wdown_ref)])]
    first_step = pl.program_id(0) == 0

    @pl.when(first_step)
    def _():
        for cp in copies:
            cp.start(priority=1)
        copies[0].wait()
        copies[1].wait()

    n_oct, rows_oct, ds5 = y_ref.shape
    y = jnp.stack([y_ref[oc].reshape(rows_oct // SUBLANES, SUBLANES, ds5) for oc in range(n_oct)], axis=1)
    g = jax.nn.gelu(y.reshape(n_oct * rows_oct, ds5))
    z = jnp.dot(g.astype(BF16), wglu_ref[...], preferred_element_type=F32) + bglu_ref[...]
    gg = g * _sigmoid(z)
    hs = (gg * lax.rsqrt(jnp.mean(gg * gg, axis=-1, keepdims=True) + EPS)).astype(BF16)
    h1 = (x_ref[...] + jnp.dot(hm_ref[...], wo_ref[0:dm, :], preferred_element_type=F32)
          + jnp.dot(hs, wo_ref[dm:, :], preferred_element_type=F32))
    xn_ref[...] = (h1 * lax.rsqrt(jnp.mean(h1 * h1, axis=-1, keepdims=True) + EPS)).astype(BF16)
    acc_ref[...] = h1

    @pl.when(first_step)
    def _():
        for cp in copies[2:]:
            cp.wait()

    def ffn_chunk(c, carry):
        cs = pl.ds(pl.multiple_of(c * ff_chunk, ff_chunk), ff_chunk)
        xn2 = xn_ref[...]
        gate = jnp.dot(xn2, wgate_ref[:, cs], preferred_element_type=F32)
        up = jnp.dot(xn2, wup_ref[:, cs], preferred_element_type=F32)
        act = (_silu(gate) * up).astype(BF16)
        acc_ref[...] += jnp.dot(act, wdown_ref[cs, :], preferred_element_type=F32)
        return carry

    lax.fori_loop(0, wgate_ref.shape[1] // ff_chunk, ffn_chunk, 0, unroll=4)
    out_ref[...] = _rmsnorm(acc_ref[...], gf_ref[...]) if final_norm else acc_ref[...]


def _out_block(x2, hm, y, wglu, bglu, wo, wgate, wup, wdown, gf, *, tm, ff_chunk, final_norm):
    T, D = x2.shape
    dm = hm.shape[1]
    n_oct, _, ds5 = y.shape
    assert wgate.shape[1] % ff_chunk == 0
    kern = functools.partial(_out_kernel, ff_chunk=ff_chunk, final_norm=final_norm)
    tok = lambda w: pl.BlockSpec((tm, w), lambda i: (i, 0))
    in_hbm = pl.BlockSpec(memory_space=pl.ANY)
    weights = (wglu, wo, wgate, wup, wdown)
    return pl.pallas_call(
        kern,
        grid=(T // tm,),
        in_specs=[tok(D), tok(dm), pl.BlockSpec((n_oct, tm // n_oct, ds5), lambda i: (0, i, 0)),
                  in_hbm, _const_spec((1, ds5)), in_hbm, in_hbm, in_hbm, in_hbm, _const_spec((1, D))],
        out_specs=tok(D),
        out_shape=jax.ShapeDtypeStruct((T, D), F32),
        scratch_shapes=([pltpu.VMEM((tm, D), F32), pltpu.VMEM((tm, D), BF16)]
                        + [pltpu.VMEM(w.shape, w.dtype) for w in weights]
                        + [pltpu.SemaphoreType.DMA((len(weights),))]),
        compiler_params=pltpu.CompilerParams(dimension_semantics=("arbitrary",),
                                             vmem_limit_bytes=BIG_VMEM_LIMIT),
        name="out_block",
    )(x2, hm, y, wglu, bglu, wo, wgate, wup, wdown, gf)


def kernel(x, norm1_g, w_in, if_bias, conv_qk, mlstm_norm_g, a_re, a_im, log_dt, b_re, b_im, c_re, c_im,
           d_skip, w_glu, b_glu, s5_norm_g, w_out, norm2_g, w_gate, w_up, w_down, norm_f_g):
    B, S, D = x.shape
    depth = w_in.shape[0]
    H = N_MLSTM_HEADS
    dm = mlstm_norm_g.shape[1]
    T = B * S
    L5 = S5_CHUNK
    nc5 = S // L5
    row = lambda a: a.reshape(1, -1).astype(F32)

    h = x.reshape(T, D)
    for l in range(depth):
        last = l == depth - 1
        gb = jnp.zeros((1, LANES), F32).at[0, 0:2 * H].set(if_bias[l])

        q, kt, v, o, gatest, u = _inproj(h, row(norm1_g[l]), w_in[l].T.astype(F32), gb, conv_qk[l].astype(F32),
                                         seq=S, tm=min(INPROJ_TM, S), d_mlstm=dm, n_oct=L5 // SUBLANES)
        ones = lambda n: jnp.ones((n,), F32)
        g_mix = jnp.concatenate([mlstm_norm_g[l], s5_norm_g[l]]).astype(F32)
        g_ffn = norm2_g[l].astype(F32)
        hm, wglu, wo, wgate, wup, wdown = _mlstm(
            q, kt, v, o, gatest,
            [w.astype(F32) for w in (w_glu[l], w_out[l], w_gate[l], w_up[l], w_down[l])],
            [ones(w_glu.shape[1]), g_mix, g_ffn, g_ffn, ones(w_down.shape[1])], batch=B, seq=S)

        w1, wout, al = _s5_prep(a_re[l], a_im[l], log_dt[l], b_re[l], b_im[l], c_re[l], c_im[l],
                                d_skip[l], chunk=L5, n_chunks=nc5)
        y = _s5(u, w1, wout, al, chunk=L5, n_chunks=nc5, n_col_blocks=S5_COL_BLOCKS)

        h = _out_block(h, hm, y, wglu, row(b_glu[l]), wo, wgate, wup, wdown, row(norm_f_g),
                       tm=OUT_TM, ff_chunk=FF_CHUNK, final_norm=last)
    return h.reshape(B, S, D)
```

```python
import functools
import math

import jax
import jax.numpy as jnp
from jax import lax
from jax.experimental import pallas as pl
from jax.experimental.pallas import tpu as pltpu

EPS = 1e-6
N_MLSTM_HEADS = 4
CONV_WIDTH = 4
S5_GROUP = 16
S5_STATE = 64

LANES = 128
SUBLANES = 8
MLSTM_CHUNK = 128
S5_CHUNK = 16
S5_COL_BLOCKS = 1
S5_PREP_PAIRS = 4
INPROJ_TM = 1024
OUT_TM = 1024
FF_CHUNK = 256
FFN_CHUNKS_PER_TRIP = 4
VMEM_LIMIT = 48 * 1024 * 1024
BIG_VMEM_LIMIT = 58 * 1024 * 1024

F32 = jnp.float32
BF16 = jnp.bfloat16


def _rmsnorm(x, g):
    ms = jnp.mean(x * x, axis=-1, keepdims=True)
    return x * lax.rsqrt(ms + EPS) * g


def _sigmoid(x):
    return 0.5 * jnp.tanh(0.5 * x) + 0.5


def _silu(x):
    h = 0.5 * x
    return h + h * jnp.tanh(h)


def _log_sigmoid(x):
    return -(jnp.maximum(-x, 0.0) + jnp.log1p(jnp.exp(-jnp.abs(x))))


def _const_spec(shape):
    return pl.BlockSpec(shape, lambda *_: (0,) * len(shape), pipeline_mode=pl.Buffered(1))


def _inproj_kernel(x_ref, g1_ref, wt_ref, gb_ref, cw_ref,
                   q_ref, kt_ref, v_ref, o_ref, gatest_ref, u_ref,
                   ext_ref, wm_ref, wg_ref, wu_ref, *, tiles_per_seq, d_mlstm, k_scale):
    tm = x_ref.shape[0]
    dm = d_mlstm
    H = N_MLSTM_HEADS
    halo = SUBLANES

    @pl.when(pl.program_id(0) == 0)
    def _():
        g1 = g1_ref[...]
        for c0 in range(0, 4 * dm, dm):
            wm_ref[:, c0:c0 + dm] = (wt_ref[c0:c0 + dm, :] * g1).T.astype(BF16)
        wu_ref[...] = (wt_ref[4 * dm + 2 * H:, :] * g1).T.astype(BF16)
        fill = jnp.zeros((LANES - 2 * H, wt_ref.shape[1]), F32)
        wg_ref[...] = jnp.concatenate([wt_ref[4 * dm:4 * dm + 2 * H, :] * g1, fill], axis=0).T.astype(BF16)

    x = x_ref[...]
    xn = (x * lax.rsqrt(jnp.mean(x * x, axis=-1, keepdims=True) + EPS)).astype(BF16)

    @pl.when(pl.program_id(0) % tiles_per_seq == 0)
    def _():
        ext_ref[0:halo, :] = jnp.zeros((halo, 2 * dm), F32)

    ext_ref[halo:halo + tm, :] = jnp.dot(xn, wm_ref[:, 0:2 * dm], preferred_element_type=F32)
    v = jnp.dot(xn, wm_ref[:, 2 * dm:3 * dm], preferred_element_type=F32)
    o = jnp.dot(xn, wm_ref[:, 3 * dm:4 * dm], preferred_element_type=F32)
    u = jnp.dot(xn, wu_ref[...], preferred_element_type=F32)
    gates = jnp.dot(xn, wg_ref[...], preferred_element_type=F32) + gb_ref[...]
    v_ref[...] = v.astype(BF16)
    o_ref[...] = o.astype(BF16)
    n_oct = u_ref.shape[0]
    u4 = u.reshape(tm // (n_oct * SUBLANES), n_oct, SUBLANES, u.shape[1])
    for oc in range(n_oct):
        u_ref[oc] = u4[:, oc].reshape(tm // n_oct, u.shape[1])
    gatest_ref[...] = gates.T[0:2 * H, :]

    def zero_of(t):
        w = pltpu.bitcast(t[tm - SUBLANES:tm, t.shape[1] - LANES:], jnp.uint32)
        return pltpu.bitcast((w >> 16) >> 16, F32)
    anchor = zero_of(v)

    acc = cw_ref[CONV_WIDTH - 1:CONV_WIDTH, :] * ext_ref[halo:halo + tm, :]
    for j in range(1, CONV_WIDTH):
        acc = acc + cw_ref[CONV_WIDTH - 1 - j:CONV_WIDTH - j, :] * ext_ref[halo - j:halo - j + tm, :]
    ext_ref[0:halo, :] = ext_ref[tm:tm + halo, :]
    qk = _silu(acc) + jnp.tile(anchor, (tm // SUBLANES, 2 * dm // LANES))
    q_ref[...] = qk[:, 0:dm].astype(BF16)
    kt_ref[...] = (qk[:, dm:2 * dm] * k_scale).T.astype(BF16)


def _inproj(x2, g1, w_in_t, gb, cw, *, seq, tm, d_mlstm, n_oct):
    T, D = x2.shape
    dm = d_mlstm
    H = N_MLSTM_HEADS
    assert 2 * H == SUBLANES
    ds5 = w_in_t.shape[0] - 4 * dm - 2 * H
    kern = functools.partial(_inproj_kernel, tiles_per_seq=seq // tm, d_mlstm=dm,
                             k_scale=float((dm // H) ** -0.5))
    tok = lambda w: pl.BlockSpec((tm, w), lambda i: (i, 0))
    return pl.pallas_call(
        kern,
        grid=(T // tm,),
        in_specs=[tok(D), _const_spec((1, D)), _const_spec(w_in_t.shape), _const_spec((1, LANES)),
                  _const_spec(cw.shape)],
        out_specs=[tok(dm), pl.BlockSpec((dm, tm), lambda i: (0, i)), tok(dm), tok(dm),
                   pl.BlockSpec((2 * H, tm), lambda i: (0, i)),
                   pl.BlockSpec((n_oct, tm // n_oct, ds5), lambda i: (0, i, 0))],
        out_shape=[jax.ShapeDtypeStruct((T, dm), BF16), jax.ShapeDtypeStruct((dm, T), BF16),
                   jax.ShapeDtypeStruct((T, dm), BF16), jax.ShapeDtypeStruct((T, dm), BF16),
                   jax.ShapeDtypeStruct((2 * H, T), F32),
                   jax.ShapeDtypeStruct((n_oct, T // n_oct, ds5), F32)],
        scratch_shapes=[pltpu.VMEM((tm + 2 * SUBLANES, 2 * dm), F32), pltpu.VMEM((D, 4 * dm), BF16),
                        pltpu.VMEM((D, LANES), BF16), pltpu.VMEM((D, ds5), BF16)],
        compiler_params=pltpu.CompilerParams(dimension_semantics=("arbitrary",),
                                             vmem_limit_bytes=VMEM_LIMIT),
        name="inproj",
    )(x2, g1, w_in_t, gb, cw)


def _lane_scan(x, op, identity, seg):
    pos = lax.broadcasted_iota(jnp.int32, x.shape, x.ndim - 1) & (seg - 1)
    sh = 1
    while sh < seg:
        x = op(x, jnp.where(pos >= sh, pltpu.roll(x, sh, x.ndim - 1), identity))
        sh *= 2
    return x


def _mlstm_kernel(q_ref, kt_ref, v_ref, o_ref, gatest_ref, *refs, chunk, n_cast):
    S = q_ref.shape[0]
    H = N_MLSTM_HEADS
    dh = q_ref.shape[1] // H
    L = chunk
    assert L == dh == LANES
    w32_refs, ws_refs = refs[0:n_cast], refs[n_cast:2 * n_cast]
    out_ref, w16_refs = refs[2 * n_cast], refs[2 * n_cast + 1:3 * n_cast + 1]
    a_ref, g0_ref, nm_ref, gend_ref = refs[3 * n_cast + 1:3 * n_cast + 5]
    c_refs, nd_refs = refs[3 * n_cast + 5:3 * n_cast + 5 + H], refs[3 * n_cast + 5 + H:]

    for w32_ref, ws_ref, w16_ref in zip(w32_refs, ws_refs, w16_refs):
        w16_ref[...] = (w32_ref[...] * ws_ref[...]).astype(BF16)

    @pl.when(pl.program_id(0) == 0)
    def _():
        gates = gatest_ref[...]
        f_cum = pltpu.roll(_lane_scan(_log_sigmoid(gates), jnp.add, 0.0, S), H, 0)
        a = gates - f_cum
        g0 = jnp.maximum(_lane_scan(a, jnp.maximum, -jnp.inf, S), 0.0)
        a_ref[...] = a
        g0_ref[...] = g0
        nm_ref[...] = -(f_cum + g0)
        for c in range(gatest_ref.shape[1] // L):
            gend_ref[:, c * L:(c + 1) * L] = jnp.broadcast_to(g0[:, (c + 1) * L - 1:(c + 1) * L], (SUBLANES, L))

    t0 = pl.program_id(0) * S

    for c_ref in c_refs:
        c_ref[...] = jnp.zeros(c_ref.shape, F32)
    causal = (lax.broadcasted_iota(jnp.int32, (L, L), 0) >= lax.broadcasted_iota(jnp.int32, (L, L), 1))
    ones_blk = jnp.ones((L, dh), BF16)
    mean_mat = jnp.full((dh, dh), 1.0 / dh, BF16)

    def mix(c, h, g_prev):
        r0 = pl.multiple_of(c * L, L)
        hs = slice(h * dh, (h + 1) * dh)
        qc = q_ref[pl.ds(r0, L), hs]
        ktc = kt_ref[hs, pl.ds(r0, L)]
        v_aug = jnp.concatenate([v_ref[pl.ds(r0, L), hs], ones_blk], axis=1)
        g0l = pl.ds(pl.multiple_of(t0 + r0, L), L)
        a_row = a_ref[h:h + 1, g0l]
        g_end = gend_ref[h:h + 1, g0l]
        g0_t = jnp.broadcast_to(g0_ref[h:h + 1, g0l], (L, L)).T

        s_qk = jnp.dot(qc, ktc, preferred_element_type=F32)
        p = (jnp.exp(jnp.where(causal, a_row - g0_t, -jnp.inf)) * s_qk).astype(BF16)
        c_prev = c_refs[h][...]
        q_w = (qc.astype(F32) * jnp.exp(g_prev - g0_t)).astype(BF16)
        nd_refs[h][...] = jnp.dot(jnp.concatenate([q_w, p], axis=1),
                                  jnp.concatenate([c_prev.astype(BF16), v_aug], axis=0),
                                  preferred_element_type=F32)

        kte = (ktc.astype(F32) * jnp.exp(a_row - g_end)).astype(BF16)
        decay = jnp.exp(g_prev - g_end)
        c_refs[h][...] = (jnp.concatenate([decay, decay], axis=1) * c_prev
                          + jnp.dot(kte, v_aug, preferred_element_type=F32))
        return g_end

    def emit(c, h):
        r0 = pl.multiple_of(c * L, L)
        hs = slice(h * dh, (h + 1) * dh)
        nm_t = jnp.broadcast_to(nm_ref[h:h + 1, pl.ds(pl.multiple_of(t0 + r0, L), L)], (L, L)).T
        nd = nd_refs[h][...]
        h_tilde = nd[:, 0:dh] / jnp.maximum(jnp.abs(nd[:, dh:2 * dh]), jnp.exp(nm_t))
        hm = _sigmoid(o_ref[pl.ds(r0, L), hs].astype(F32)) * h_tilde
        mu = jnp.dot(hm.astype(BF16), mean_mat, preferred_element_type=F32)
        hc = hm - mu
        var = jnp.dot((hc * hc).astype(BF16), mean_mat, preferred_element_type=F32)
        out_ref[pl.ds(r0, L), hs] = (hc * lax.rsqrt(var + EPS)).astype(BF16)

    zero = jnp.zeros((1, L), F32)
    g_first = tuple(mix(0, h, zero) for h in range(H))

    def body(c, g_carry):
        for h in range(H):
            emit(c - 1, h)
        return tuple(mix(c, h, g_carry[h]) for h in range(H))

    lax.fori_loop(1, S // L, body, g_first, unroll=5)
    for h in range(H):
        emit(S // L - 1, h)


def _mlstm(q, kt, v, o, gatest, weights, row_scales, *, batch, seq):
    T, dm = q.shape
    dh = dm // N_MLSTM_HEADS
    kern = functools.partial(_mlstm_kernel, chunk=MLSTM_CHUNK, n_cast=len(weights))
    tok = lambda w: pl.BlockSpec((seq, w), lambda b: (b, 0))
    rows = pltpu.VMEM((SUBLANES, T), F32)
    bf16_rows = 2 * SUBLANES
    assert all(w.shape[0] % (batch * bf16_rows) == 0 for w in weights)
    w_specs = [pl.BlockSpec((w.shape[0] // batch, w.shape[1]), lambda b: (b, 0)) for w in weights]
    s_specs = [pl.BlockSpec((w.shape[0] // batch, 1), lambda b: (b, 0)) for w in weights]
    scales = [sc.reshape(-1, 1).astype(F32) for sc in row_scales]
    return pl.pallas_call(
        kern,
        grid=(batch,),
        in_specs=[tok(dm), pl.BlockSpec((dm, seq), lambda b: (0, b)), tok(dm), tok(dm),
                  _const_spec((SUBLANES, T))] + w_specs + s_specs,
        out_specs=[tok(dm)] + w_specs,
        out_shape=[jax.ShapeDtypeStruct((T, dm), BF16)] + [jax.ShapeDtypeStruct(w.shape, BF16) for w in weights],
        scratch_shapes=([rows, rows, rows, rows] + [pltpu.VMEM((dh, 2 * dh), F32)] * N_MLSTM_HEADS
                        + [pltpu.VMEM((MLSTM_CHUNK, 2 * dh), F32)] * N_MLSTM_HEADS),
        compiler_params=pltpu.CompilerParams(dimension_semantics=("arbitrary",),
                                             vmem_limit_bytes=VMEM_LIMIT),
        name="mlstm",
    )(q, kt, v, o, gatest, *weights, *scales)


def _s5_prep_kernel(arow_ref, acol_ref, bre_ref, bim_ref, cre_ref, cim_ref, drow_ref,
                    rep_ref, rep2_ref, selrev_ref, pwcol_ref, w1_ref, wout_ref, al_ref, *, chunk):
    for p in range(arow_ref.shape[0]):
        _s5_prep_pair(p, arow_ref, acol_ref, bre_ref, bim_ref, cre_ref, cim_ref, drow_ref,
                      rep_ref, rep2_ref, selrev_ref, pwcol_ref, w1_ref, wout_ref, al_ref, chunk=chunk)


def _s5_prep_pair(p, arow_ref, acol_ref, bre_ref, bim_ref, cre_ref, cim_ref, drow_ref,
                  rep_ref, rep2_ref, selrev_ref, pwcol_ref, w1_ref, wout_ref, al_ref, *, chunk):
    L = chunk
    P = S5_STATE
    Hc = S5_GROUP
    R = L * Hc
    hi = lax.Precision.HIGHEST

    def select(x, sel_ref):
        x_hi = x.astype(BF16)
        x_lo = (x - x_hi.astype(F32)).astype(BF16)
        sel = sel_ref[...]
        return (jnp.dot(x_hi, sel, preferred_element_type=F32) + jnp.dot(x_lo, sel, preferred_element_type=F32))

    dt_r = jnp.exp(arow_ref[p, 2:3, :])
    da_re, da_im = dt_r * arow_ref[p, 0:1, :], dt_r * arow_ref[p, 1:2, :]

    def abar_pow(tau):
        mag = jnp.exp(tau * da_re)
        ang = tau * da_im
        return mag * jnp.cos(ang), mag * jnp.sin(ang)

    n_pow = -(-(L + 1) // SUBLANES) * SUBLANES
    er_re, er_im = abar_pow(lax.broadcasted_iota(jnp.int32, (n_pow, 1), 0).astype(F32))
    pad = jnp.zeros((2 * P - n_pow, 2 * P), F32)
    ec_re = jnp.concatenate([er_re, pad], axis=0).T
    ec_im = jnp.concatenate([er_im, pad], axis=0).T

    a_re_c, a_im_c = acol_ref[p, :, 0:1], acol_ref[p, :, 1:2]
    ab_re, ab_im = ec_re[:, 1:2], ec_im[:, 1:2]
    den = a_re_c * a_re_c + a_im_c * a_im_c
    nr = ab_re - 1.0
    z_re = (nr * a_re_c + ab_im * a_im_c) / den
    z_im = (ab_im * a_re_c - nr * a_im_c) / den
    bb_re = z_re * bre_ref[p] - z_im * bim_ref[p]
    bb_im = z_re * bim_ref[p] + z_im * bre_ref[p]
    first = lax.broadcasted_iota(jnp.int32, (2 * P, Hc), 0) < P
    bd = lambda b: jnp.concatenate([jnp.where(first, b, 0.0), jnp.where(first, 0.0, b)], axis=1)

    def rows(e, lo):
        return jnp.concatenate([jnp.broadcast_to(e[lo + t:lo + t + 1, :], (Hc, 2 * P)) for t in range(L)], axis=0)

    c_re = jnp.concatenate([cre_ref[p]] * L, axis=0)
    c_im = jnp.concatenate([cim_ref[p]] * L, axis=0)

    def c_times_pow(lo):
        e_re, e_im = rows(er_re, lo), rows(er_im, lo)
        return c_re * e_re - c_im * e_im, c_re * e_im + c_im * e_re

    ce_re, ce_im = c_times_pow(0)
    kflat = (jnp.dot(ce_re, bd(bb_re), preferred_element_type=F32, precision=hi)
             - jnp.dot(ce_im, bd(bb_im), preferred_element_type=F32, precision=hi))
    row = lax.broadcasted_iota(jnp.int32, (R, 2 * Hc), 0)
    col = lax.broadcasted_iota(jnp.int32, (R, 2 * Hc), 1) & (Hc - 1)
    kflat = kflat + jnp.where(row == col, drow_ref[p], 0.0)

    m = select(kflat, rep2_ref)
    blk = (lax.broadcasted_iota(jnp.int32, (R, 2 * R), 1) & (R - 1)) // Hc
    sh = 1
    while sh < L:
        shifted = jnp.concatenate([jnp.zeros((sh * Hc, 2 * R), F32), m[0:R - sh * Hc, :]], axis=0)
        m = jnp.where((blk & sh) != 0, shifted, m)
        sh *= 2

    ev_re, ev_im = select(ec_re, selrev_ref), select(ec_im, selrev_ref)
    bt_re, bt_im = select(bb_re, rep_ref), select(bb_im, rep_ref)
    st_re = (ev_re * bt_re - ev_im * bt_im).astype(BF16)
    st_im = (ev_re * bt_im + ev_im * bt_re).astype(BF16)

    co_re, co_im = c_times_pow(1)
    lane_first = lax.broadcasted_iota(jnp.int32, (R, 2 * P), 1) < P
    for g in range(2):
        w1_ref[2 * p + g, 0:R, :] = m[:, g * R:(g + 1) * R].astype(BF16)
        w1_ref[2 * p + g, R:R + P, :] = st_re[g * P:(g + 1) * P, :]
        w1_ref[2 * p + g, R + P:R + 2 * P, :] = st_im[g * P:(g + 1) * P, :]
        own = lane_first if g == 0 else jnp.logical_not(lane_first)
        wout_ref[2 * p + g, :, 0:2 * P] = jnp.where(own, co_re, 0.0).astype(BF16)
        wout_ref[2 * p + g, :, 2 * P:4 * P] = jnp.where(own, -co_im, 0.0).astype(BF16)

    sc_re, sc_im = abar_pow(pwcol_ref[...])
    al_ref[p, 0:SUBLANES, :] = sc_re
    al_ref[p, SUBLANES:2 * SUBLANES, :] = sc_im


def _s5_prep(a_re, a_im, log_dt, b_re, b_im, c_re, c_im, d_skip, *, chunk, n_chunks):
    G, P = a_re.shape
    Hc = S5_GROUP
    R = chunk * Hc
    assert G % 2 == 0 and 2 * P == LANES
    G2 = G // 2
    n_pairs = math.gcd(G2, S5_PREP_PAIRS)
    ldt = jnp.broadcast_to(log_dt[:, None], (G, P))
    arow = jnp.stack([a_re.reshape(G2, 2 * P), a_im.reshape(G2, 2 * P), ldt.reshape(G2, 2 * P)], axis=1)
    acol = jnp.stack([a_re.reshape(G2, 2 * P), a_im.reshape(G2, 2 * P)], axis=2)
    pair_lanes = lambda c: c.reshape(G2, 2, Hc, P).transpose(0, 2, 1, 3).reshape(G2, Hc, 2 * P)
    drow = d_skip.reshape(G2, 1, 2 * Hc)
    rep = (jnp.arange(R)[None, :] % Hc == jnp.arange(Hc)[:, None]).astype(BF16)
    rep2 = jnp.kron(jnp.eye(2, dtype=BF16), rep)
    selrev = (jnp.arange(LANES)[:, None] == chunk - 1 - jnp.arange(R)[None, :] // Hc).astype(BF16)
    n_steps = max(1, (n_chunks - 1).bit_length())
    assert chunk + 1 <= LANES and n_steps <= SUBLANES
    pwcol = jnp.zeros((SUBLANES, 1), F32).at[:n_steps, 0].set(chunk * 2.0 ** jnp.arange(n_steps))
    kern = functools.partial(_s5_prep_kernel, chunk=chunk)
    pair = lambda *s: pl.BlockSpec((n_pairs,) + s, lambda g: (g,) + (0,) * len(s))
    two = lambda *s: pl.BlockSpec((2 * n_pairs,) + s, lambda g: (g,) + (0,) * len(s))
    return pl.pallas_call(
        kern,
        grid=(G2 // n_pairs,),
        in_specs=[pair(3, 2 * P), pair(2 * P, 2), pair(2 * P, Hc), pair(2 * P, Hc), pair(Hc, 2 * P), pair(Hc, 2 * P),
                  pair(1, 2 * Hc), _const_spec((Hc, R)), _const_spec((2 * Hc, 2 * R)), _const_spec((LANES, R)),
                  _const_spec((SUBLANES, 1))],
        out_specs=[two(R + 2 * P, R), two(R, 4 * P), pair(2 * SUBLANES, 2 * P)],
        out_shape=[jax.ShapeDtypeStruct((G, R + 2 * P, R), BF16),
                   jax.ShapeDtypeStruct((G, R, 4 * P), BF16),
                   jax.ShapeDtypeStruct((G2, 2 * SUBLANES, 2 * P), F32)],
        compiler_params=pltpu.CompilerParams(dimension_semantics=("arbitrary",),
                                             vmem_limit_bytes=VMEM_LIMIT),
        name="s5_prep",
    )(arow, acol, b_re.reshape(G2, 2 * P, Hc), b_im.reshape(G2, 2 * P, Hc), pair_lanes(c_re), pair_lanes(c_im),
      drow, rep, rep2, selrev, pwcol)


def _s5_kernel(*refs, chunk, n_chunks):
    L = chunk
    n_oct = L // SUBLANES
    u_refs, (w1_ref, wout_ref, al_ref, y_ref, ut_ref, yt_ref) = refs[0:n_oct], refs[n_oct:]
    P = S5_STATE
    Hc = S5_GROUP
    R = L * Hc
    C = y_ref.shape[1] // SUBLANES
    n_grp = LANES // Hc

    for s in range(L):
        xt = u_refs[s // SUBLANES][pl.ds(s % SUBLANES, C, stride=SUBLANES), :].astype(BF16).T
        for j in range(n_grp):
            ut_ref[j, s * Hc:(s + 1) * Hc, :] = xt[j * Hc:(j + 1) * Hc, :]

    seg_pos = lax.broadcasted_iota(jnp.int32, (C, 2 * P), 0) & (n_chunks - 1)
    nt = (((1,), (1,)), ((), ()))

    def shifted(x, sh):
        return jnp.where(seg_pos >= sh, pltpu.roll(x, sh, 0), 0.0)

    def group_pair(i, carry):
        j0 = 2 * i
        res = [jnp.dot(w1_ref[j0 + d], ut_ref[j0 + d], preferred_element_type=F32) for d in range(2)]
        x_re = jnp.concatenate([res[0][R:R + P], res[1][R:R + P]], axis=0).T
        x_im = jnp.concatenate([res[0][R + P:R + 2 * P], res[1][R + P:R + 2 * P]], axis=0).T
        sh, k = 1, 0
        while sh < n_chunks:
            a_re = al_ref[i, k:k + 1, :]
            a_im = al_ref[i, SUBLANES + k:SUBLANES + k + 1, :]
            s_re, s_im = shifted(x_re, sh), shifted(x_im, sh)
            x_re, x_im = (x_re + a_re * s_re - a_im * s_im, x_im + a_re * s_im + a_im * s_re)
            sh *= 2
            k += 1
        xprev = jnp.concatenate([shifted(x_re, 1), shifted(x_im, 1)], axis=1).astype(BF16)
        for d in range(2):
            yt_ref[j0 + d] = res[d][0:R, :] + lax.dot_general(wout_ref[j0 + d], xprev, nt,
                                                              preferred_element_type=F32)
        return carry

    lax.fori_loop(0, n_grp // 2, group_pair, 0, unroll=2)

    for t in range(L):
        zt = jnp.concatenate([yt_ref[j, t * Hc:(t + 1) * Hc, :] for j in range(n_grp)], axis=0)
        y_ref[t // SUBLANES, pl.ds(t % SUBLANES, C, stride=SUBLANES), :] = zt.T


def _s5(u, w1, wout, al, *, chunk, n_chunks, n_col_blocks):
    n_oct, t_oct, ds5 = u.shape
    T = n_oct * t_oct
    assert n_oct * SUBLANES == chunk
    G, R1, R = w1.shape
    P4 = wout.shape[-1]
    n_grp = LANES // S5_GROUP
    n_q = ds5 // LANES
    tb = T // n_col_blocks
    C = tb // chunk
    kern = functools.partial(_s5_kernel, chunk=chunk, n_chunks=n_chunks)
    grp = lambda *s: pl.BlockSpec((n_grp,) + s, lambda q, i: (q,) + (0,) * len(s))
    octets = [pl.BlockSpec((None, C * SUBLANES, LANES), functools.partial(lambda q, i, o: (o, i, q), o=o))
              for o in range(n_oct)]
    return pl.pallas_call(
        kern,
        grid=(n_q, n_col_blocks),
        in_specs=octets + [grp(R1, R), grp(R, P4),
                           pl.BlockSpec((n_grp // 2, 2 * SUBLANES, P4 // 2), lambda q, i: (q, 0, 0))],
        out_specs=pl.BlockSpec((n_oct, C * SUBLANES, LANES), lambda q, i: (0, i, q)),
        out_shape=jax.ShapeDtypeStruct((n_oct, t_oct, ds5), F32),
        scratch_shapes=[pltpu.VMEM((n_grp, R, C), BF16), pltpu.VMEM((n_grp, R, C), F32)],
        compiler_params=pltpu.CompilerParams(dimension_semantics=("arbitrary", "arbitrary"),
                                             vmem_limit_bytes=BIG_VMEM_LIMIT),
        name="s5",
    )(*([u] * n_oct), w1, wout, al)


def _out_kernel(x_ref, hm_ref, y_ref, wglu_hbm, bglu_ref, wo_hbm, wgate_hbm, wup_hbm, wdown_hbm, gf_ref,
                out_ref, acc_ref, xn_ref, wglu_ref, wo_ref, wgate_ref, wup_ref, wdown_ref, sems,
                *, ff_chunk, final_norm):
    dm = hm_ref.shape[1]
    n_ff = wgate_ref.shape[1] // ff_chunk
    n_trips = n_ff // FFN_CHUNKS_PER_TRIP
    slab = FFN_CHUNKS_PER_TRIP * ff_chunk
    tail = wgate_ref.shape[1] - n_trips * slab
    front = [pltpu.make_async_copy(wglu_hbm, wglu_ref, sems.at[0]), pltpu.make_async_copy(wo_hbm, wo_ref, sems.at[1])]

    def slab_copies(t, width):
        cols = pl.ds(t * slab if isinstance(t, int) else pl.multiple_of(t * slab, ff_chunk), width)
        pairs = [(wgate_hbm.at[:, cols], wgate_ref.at[:, cols]), (wup_hbm.at[:, cols], wup_ref.at[:, cols]),
                 (wdown_hbm.at[cols, :], wdown_ref.at[cols, :])]
        return [pltpu.make_async_copy(src, dst, sems.at[2 + 3 * t + j]) for j, (src, dst) in enumerate(pairs)]

    first_step = pl.program_id(0) == 0

    @pl.when(first_step)
    def _():
        for cp in front:
            cp.start(priority=1)
        for t in range(n_trips):
            for cp in slab_copies(t, slab):
                cp.start(priority=1)
        if tail:
            for cp in slab_copies(n_trips, tail):
                cp.start(priority=1)
        for cp in front:
            cp.wait()

    n_oct, rows_oct, ds5 = y_ref.shape
    y = jnp.stack([y_ref[oc].reshape(rows_oct // SUBLANES, SUBLANES, ds5) for oc in range(n_oct)], axis=1)
    g = jax.nn.gelu(y.reshape(n_oct * rows_oct, ds5))
    z = jnp.dot(g.astype(BF16), wglu_ref[...], preferred_element_type=F32) + bglu_ref[...]
    gg = g * _sigmoid(z)
    hs = (gg * lax.rsqrt(jnp.mean(gg * gg, axis=-1, keepdims=True) + EPS)).astype(BF16)
    h1 = (x_ref[...] + jnp.dot(hm_ref[...], wo_ref[0:dm, :], preferred_element_type=F32)
          + jnp.dot(hs, wo_ref[dm:, :], preferred_element_type=F32))
    xn_ref[...] = (h1 * lax.rsqrt(jnp.mean(h1 * h1, axis=-1, keepdims=True) + EPS)).astype(BF16)
    acc_ref[...] = h1

    def arrived(t, width):
        @pl.when(first_step)
        def _():
            for cp in slab_copies(t, width):
                cp.wait()

    def ffn_chunk(c):
        cs = pl.ds(c * ff_chunk if isinstance(c, int) else pl.multiple_of(c * ff_chunk, ff_chunk), ff_chunk)
        xn2 = xn_ref[...]
        gate = jnp.dot(xn2, wgate_ref[:, cs], preferred_element_type=F32)
        up = jnp.dot(xn2, wup_ref[:, cs], preferred_element_type=F32)
        act = (_silu(gate) * up).astype(BF16)
        acc_ref[...] += jnp.dot(act, wdown_ref[cs, :], preferred_element_type=F32)

    def ffn_trip(t, carry):
        arrived(t, slab)
        for j in range(FFN_CHUNKS_PER_TRIP):
            ffn_chunk(t * FFN_CHUNKS_PER_TRIP + j)
        return carry

    lax.fori_loop(0, n_trips, ffn_trip, 0)
    if tail:
        arrived(n_trips, tail)
        for c in range(n_trips * FFN_CHUNKS_PER_TRIP, n_ff):
            ffn_chunk(c)
    out_ref[...] = _rmsnorm(acc_ref[...], gf_ref[...]) if final_norm else acc_ref[...]


def _out_block(x2, hm, y, wglu, bglu, wo, wgate, wup, wdown, gf, *, tm, ff_chunk, final_norm):
    T, D = x2.shape
    dm = hm.shape[1]
    n_oct, _, ds5 = y.shape
    assert wgate.shape[1] % ff_chunk == 0
    kern = functools.partial(_out_kernel, ff_chunk=ff_chunk, final_norm=final_norm)
    tok = lambda w: pl.BlockSpec((tm, w), lambda i: (i, 0))
    in_hbm = pl.BlockSpec(memory_space=pl.ANY)
    weights = (wglu, wo, wgate, wup, wdown)
    return pl.pallas_call(
        kern,
        grid=(T // tm,),
        in_specs=[tok(D), tok(dm), pl.BlockSpec((n_oct, tm // n_oct, ds5), lambda i: (0, i, 0)),
                  in_hbm, _const_spec((1, ds5)), in_hbm, in_hbm, in_hbm, in_hbm, _const_spec((1, D))],
        out_specs=tok(D),
        out_shape=jax.ShapeDtypeStruct((T, D), F32),
        scratch_shapes=([pltpu.VMEM((tm, D), F32), pltpu.VMEM((tm, D), BF16)]
                        + [pltpu.VMEM(w.shape, w.dtype) for w in weights]
                        + [pltpu.SemaphoreType.DMA((2 + 3 * pl.cdiv(wgate.shape[1], FFN_CHUNKS_PER_TRIP * ff_chunk),))]),
        compiler_params=pltpu.CompilerParams(dimension_semantics=("arbitrary",),
                                             vmem_limit_bytes=BIG_VMEM_LIMIT),
        name="out_block",
    )(x2, hm, y, wglu, bglu, wo, wgate, wup, wdown, gf)


def kernel(x, norm1_g, w_in, if_bias, conv_qk, mlstm_norm_g, a_re, a_im, log_dt, b_re, b_im, c_re, c_im,
           d_skip, w_glu, b_glu, s5_norm_g, w_out, norm2_g, w_gate, w_up, w_down, norm_f_g):
    B, S, D = x.shape
    depth = w_in.shape[0]
    H = N_MLSTM_HEADS
    dm = mlstm_norm_g.shape[1]
    T = B * S
    L5 = S5_CHUNK
    nc5 = S // L5
    row = lambda a: a.reshape(1, -1).astype(F32)

    h = x.reshape(T, D)
    for l in range(depth):
        last = l == depth - 1
        gb = jnp.zeros((1, LANES), F32).at[0, 0:2 * H].set(if_bias[l])

        q, kt, v, o, gatest, u = _inproj(h, row(norm1_g[l]), w_in[l].T.astype(F32), gb, conv_qk[l].astype(F32),
                                         seq=S, tm=min(INPROJ_TM, S), d_mlstm=dm, n_oct=L5 // SUBLANES)
        ones = lambda n: jnp.ones((n,), F32)
        g_mix = jnp.concatenate([mlstm_norm_g[l], s5_norm_g[l]]).astype(F32)
        g_ffn = norm2_g[l].astype(F32)
        hm, wglu, wo, wgate, wup, wdown = _mlstm(
            q, kt, v, o, gatest,
            [w.astype(F32) for w in (w_glu[l], w_out[l], w_gate[l], w_up[l], w_down[l])],
            [ones(w_glu.shape[1]), g_mix, g_ffn, g_ffn, ones(w_down.shape[1])], batch=B, seq=S)

        w1, wout, al = _s5_prep(a_re[l], a_im[l], log_dt[l], b_re[l], b_im[l], c_re[l], c_im[l],
                                d_skip[l], chunk=L5, n_chunks=nc5)
        y = _s5(u, w1, wout, al, chunk=L5, n_chunks=nc5, n_col_blocks=S5_COL_BLOCKS)

        h = _out_block(h, hm, y, wglu, row(b_glu[l]), wo, wgate, wup, wdown, row(norm_f_g),
                       tm=OUT_TM, ff_chunk=FF_CHUNK, final_norm=last)
    return h.reshape(B, S, D)
```

```python
import functools
import math

import jax
import jax.numpy as jnp
from jax import lax
from jax.experimental import pallas as pl
from jax.experimental.pallas import tpu as pltpu

EPS = 1e-6
N_MLSTM_HEADS = 4
CONV_WIDTH = 4
S5_GROUP = 16
S5_STATE = 64

LANES = 128
SUBLANES = 8
MLSTM_CHUNK = 128
S5_CHUNK = 16
S5_COL_BLOCKS = 1
INPROJ_TM = 1024
OUT_TM = 1024
FF_CHUNK = 256
VMEM_LIMIT = 48 * 1024 * 1024
BIG_VMEM_LIMIT = 58 * 1024 * 1024

F32 = jnp.float32
BF16 = jnp.bfloat16


def _rmsnorm(x, g):
    ms = jnp.mean(x * x, axis=-1, keepdims=True)
    return x * lax.rsqrt(ms + EPS) * g


def _sigmoid(x):
    return 0.5 * jnp.tanh(0.5 * x) + 0.5


def _silu(x):
    h = 0.5 * x
    return h + h * jnp.tanh(h)


def _log_sigmoid(x):
    return -(jnp.maximum(-x, 0.0) + jnp.log1p(jnp.exp(-jnp.abs(x))))


def _const_spec(shape):
    return pl.BlockSpec(shape, lambda *_: (0,) * len(shape), pipeline_mode=pl.Buffered(1))


def _inproj_kernel(x_ref, g1_ref, wt_ref, gb_ref, cw_ref,
                   q_ref, kt_ref, v_ref, o_ref, gatest_ref, u_ref,
                   ext_ref, wm_ref, wg_ref, wu_ref, *, tiles_per_seq, d_mlstm, k_scale):
    tm = x_ref.shape[0]
    dm = d_mlstm
    H = N_MLSTM_HEADS
    halo = SUBLANES

    @pl.when(pl.program_id(0) == 0)
    def _():
        g1 = g1_ref[...]
        for c0 in range(0, 4 * dm, dm):
            wm_ref[:, c0:c0 + dm] = (wt_ref[c0:c0 + dm, :] * g1).T.astype(BF16)
        wu_ref[...] = (wt_ref[4 * dm + 2 * H:, :] * g1).T.astype(BF16)
        fill = jnp.zeros((LANES - 2 * H, wt_ref.shape[1]), F32)
        wg_ref[...] = jnp.concatenate([wt_ref[4 * dm:4 * dm + 2 * H, :] * g1, fill], axis=0).T.astype(BF16)

    x = x_ref[...]
    xn = (x * lax.rsqrt(jnp.mean(x * x, axis=-1, keepdims=True) + EPS)).astype(BF16)

    @pl.when(pl.program_id(0) % tiles_per_seq == 0)
    def _():
        ext_ref[0:halo, :] = jnp.zeros((halo, 2 * dm), F32)

    ext_ref[halo:halo + tm, :] = jnp.dot(xn, wm_ref[:, 0:2 * dm], preferred_element_type=F32)
    v = jnp.dot(xn, wm_ref[:, 2 * dm:3 * dm], preferred_element_type=F32)
    o = jnp.dot(xn, wm_ref[:, 3 * dm:4 * dm], preferred_element_type=F32)
    u = jnp.dot(xn, wu_ref[...], preferred_element_type=F32)
    gates = jnp.dot(xn, wg_ref[...], preferred_element_type=F32) + gb_ref[...]
    v_ref[...] = v.astype(BF16)
    o_ref[...] = o.astype(BF16)
    n_oct = u_ref.shape[0]
    u4 = u.reshape(tm // (n_oct * SUBLANES), n_oct, SUBLANES, u.shape[1])
    for oc in range(n_oct):
        u_ref[oc] = u4[:, oc].reshape(tm // n_oct, u.shape[1])
    gatest_ref[...] = gates.T[0:2 * H, :]

    def zero_of(t):
        w = pltpu.bitcast(t[tm - SUBLANES:tm, t.shape[1] - LANES:], jnp.uint32)
        return pltpu.bitcast((w >> 16) >> 16, F32)
    anchor = zero_of(v)

    acc = cw_ref[CONV_WIDTH - 1:CONV_WIDTH, :] * ext_ref[halo:halo + tm, :]
    for j in range(1, CONV_WIDTH):
        acc = acc + cw_ref[CONV_WIDTH - 1 - j:CONV_WIDTH - j, :] * ext_ref[halo - j:halo - j + tm, :]
    ext_ref[0:halo, :] = ext_ref[tm:tm + halo, :]
    qk = _silu(acc) + jnp.tile(anchor, (tm // SUBLANES, 2 * dm // LANES))
    q_ref[...] = qk[:, 0:dm].astype(BF16)
    kt_ref[...] = (qk[:, dm:2 * dm] * k_scale).T.astype(BF16)


def _inproj(x2, g1, w_in_t, gb, cw, *, seq, tm, d_mlstm, n_oct):
    T, D = x2.shape
    dm = d_mlstm
    H = N_MLSTM_HEADS
    assert 2 * H == SUBLANES
    ds5 = w_in_t.shape[0] - 4 * dm - 2 * H
    kern = functools.partial(_inproj_kernel, tiles_per_seq=seq // tm, d_mlstm=dm,
                             k_scale=float((dm // H) ** -0.5))
    tok = lambda w: pl.BlockSpec((tm, w), lambda i: (i, 0))
    return pl.pallas_call(
        kern,
        grid=(T // tm,),
        in_specs=[tok(D), _const_spec((1, D)), _const_spec(w_in_t.shape), _const_spec((1, LANES)),
                  _const_spec(cw.shape)],
        out_specs=[tok(dm), pl.BlockSpec((dm, tm), lambda i: (0, i)), tok(dm), tok(dm),
                   pl.BlockSpec((2 * H, tm), lambda i: (0, i)),
                   pl.BlockSpec((n_oct, tm // n_oct, ds5), lambda i: (0, i, 0))],
        out_shape=[jax.ShapeDtypeStruct((T, dm), BF16), jax.ShapeDtypeStruct((dm, T), BF16),
                   jax.ShapeDtypeStruct((T, dm), BF16), jax.ShapeDtypeStruct((T, dm), BF16),
                   jax.ShapeDtypeStruct((2 * H, T), F32),
                   jax.ShapeDtypeStruct((n_oct, T // n_oct, ds5), F32)],
        scratch_shapes=[pltpu.VMEM((tm + 2 * SUBLANES, 2 * dm), F32), pltpu.VMEM((D, 4 * dm), BF16),
                        pltpu.VMEM((D, LANES), BF16), pltpu.VMEM((D, ds5), BF16)],
        compiler_params=pltpu.CompilerParams(dimension_semantics=("arbitrary",),
                                             vmem_limit_bytes=VMEM_LIMIT),
        name="inproj",
    )(x2, g1, w_in_t, gb, cw)


def _lane_scan(x, op, identity, seg):
    pos = lax.broadcasted_iota(jnp.int32, x.shape, x.ndim - 1) & (seg - 1)
    sh = 1
    while sh < seg:
        x = op(x, jnp.where(pos >= sh, pltpu.roll(x, sh, x.ndim - 1), identity))
        sh *= 2
    return x


def _mlstm_kernel(q_ref, kt_ref, v_ref, o_ref, gatest_ref, *refs, chunk, n_cast):
    S = q_ref.shape[0]
    H = N_MLSTM_HEADS
    dh = q_ref.shape[1] // H
    L = chunk
    assert L == dh == LANES
    w32_refs, ws_refs = refs[0:n_cast], refs[n_cast:2 * n_cast]
    out_ref, w16_refs = refs[2 * n_cast], refs[2 * n_cast + 1:3 * n_cast + 1]
    a_ref, g0_ref, nm_ref, gend_ref = refs[3 * n_cast + 1:3 * n_cast + 5]
    c_refs, nd_refs = refs[3 * n_cast + 5:3 * n_cast + 5 + H], refs[3 * n_cast + 5 + H:]

    for w32_ref, ws_ref, w16_ref in zip(w32_refs, ws_refs, w16_refs):
        w16_ref[...] = (w32_ref[...] * ws_ref[...]).astype(BF16)

    @pl.when(pl.program_id(0) == 0)
    def _():
        gates = gatest_ref[...]
        f_cum = pltpu.roll(_lane_scan(_log_sigmoid(gates), jnp.add, 0.0, S), H, 0)
        a = gates - f_cum
        g0 = jnp.maximum(_lane_scan(a, jnp.maximum, -jnp.inf, S), 0.0)
        a_ref[...] = a
        g0_ref[...] = g0
        nm_ref[...] = -(f_cum + g0)
        for c in range(gatest_ref.shape[1] // L):
            gend_ref[:, c * L:(c + 1) * L] = jnp.broadcast_to(g0[:, (c + 1) * L - 1:(c + 1) * L], (SUBLANES, L))

    t0 = pl.program_id(0) * S

    for c_ref in c_refs:
        c_ref[...] = jnp.zeros(c_ref.shape, F32)
    causal = (lax.broadcasted_iota(jnp.int32, (L, L), 0) >= lax.broadcasted_iota(jnp.int32, (L, L), 1))
    ones_blk = jnp.ones((L, dh), BF16)
    mean_mat = jnp.full((dh, dh), 1.0 / dh, BF16)

    def mix(c, h, g_prev):
        r0 = pl.multiple_of(c * L, L)
        hs = slice(h * dh, (h + 1) * dh)
        qc = q_ref[pl.ds(r0, L), hs]
        ktc = kt_ref[hs, pl.ds(r0, L)]
        v_aug = jnp.concatenate([v_ref[pl.ds(r0, L), hs], ones_blk], axis=1)
        g0l = pl.ds(pl.multiple_of(t0 + r0, L), L)
        a_row = a_ref[h:h + 1, g0l]
        g_end = gend_ref[h:h + 1, g0l]
        g0_t = jnp.broadcast_to(g0_ref[h:h + 1, g0l], (L, L)).T

        s_qk = jnp.dot(qc, ktc, preferred_element_type=F32)
        p = (jnp.exp(jnp.where(causal, a_row - g0_t, -jnp.inf)) * s_qk).astype(BF16)
        c_prev = c_refs[h][...]
        q_w = (qc.astype(F32) * jnp.exp(g_prev - g0_t)).astype(BF16)
        nd_refs[h][...] = jnp.dot(jnp.concatenate([q_w, p], axis=1),
                                  jnp.concatenate([c_prev.astype(BF16), v_aug], axis=0),
                                  preferred_element_type=F32)

        kte = (ktc.astype(F32) * jnp.exp(a_row - g_end)).astype(BF16)
        decay = jnp.exp(g_prev - g_end)
        c_refs[h][...] = (jnp.concatenate([decay, decay], axis=1) * c_prev
                          + jnp.dot(kte, v_aug, preferred_element_type=F32))
        return g_end

    def emit(c, h):
        r0 = pl.multiple_of(c * L, L)
        hs = slice(h * dh, (h + 1) * dh)
        nm_t = jnp.broadcast_to(nm_ref[h:h + 1, pl.ds(pl.multiple_of(t0 + r0, L), L)], (L, L)).T
        nd = nd_refs[h][...]
        h_tilde = nd[:, 0:dh] / jnp.maximum(jnp.abs(nd[:, dh:2 * dh]), jnp.exp(nm_t))
        hm = _sigmoid(o_ref[pl.ds(r0, L), hs].astype(F32)) * h_tilde
        mu = jnp.dot(hm.astype(BF16), mean_mat, preferred_element_type=F32)
        hc = hm - mu
        var = jnp.dot((hc * hc).astype(BF16), mean_mat, preferred_element_type=F32)
        out_ref[pl.ds(r0, L), hs] = (hc * lax.rsqrt(var + EPS)).astype(BF16)

    zero = jnp.zeros((1, L), F32)
    g_first = tuple(mix(0, h, zero) for h in range(H))

    def body(c, g_carry):
        for h in range(H):
            emit(c - 1, h)
        return tuple(mix(c, h, g_carry[h]) for h in range(H))

    lax.fori_loop(1, S // L, body, g_first, unroll=5)
    for h in range(H):
        emit(S // L - 1, h)


def _mlstm(q, kt, v, o, gatest, weights, row_scales, *, batch, seq):
    T, dm = q.shape
    dh = dm // N_MLSTM_HEADS
    kern = functools.partial(_mlstm_kernel, chunk=MLSTM_CHUNK, n_cast=len(weights))
    tok = lambda w: pl.BlockSpec((seq, w), lambda b: (b, 0))
    rows = pltpu.VMEM((SUBLANES, T), F32)
    bf16_rows = 2 * SUBLANES
    assert all(w.shape[0] % (batch * bf16_rows) == 0 for w in weights)
    w_specs = [pl.BlockSpec((w.shape[0] // batch, w.shape[1]), lambda b: (b, 0)) for w in weights]
    s_specs = [pl.BlockSpec((w.shape[0] // batch, 1), lambda b: (b, 0)) for w in weights]
    scales = [sc.reshape(-1, 1).astype(F32) for sc in row_scales]
    return pl.pallas_call(
        kern,
        grid=(batch,),
        in_specs=[tok(dm), pl.BlockSpec((dm, seq), lambda b: (0, b)), tok(dm), tok(dm),
                  _const_spec((SUBLANES, T))] + w_specs + s_specs,
        out_specs=[tok(dm)] + w_specs,
        out_shape=[jax.ShapeDtypeStruct((T, dm), BF16)] + [jax.ShapeDtypeStruct(w.shape, BF16) for w in weights],
        scratch_shapes=([rows, rows, rows, rows] + [pltpu.VMEM((dh, 2 * dh), F32)] * N_MLSTM_HEADS
                        + [pltpu.VMEM((MLSTM_CHUNK, 2 * dh), F32)] * N_MLSTM_HEADS),
        compiler_params=pltpu.CompilerParams(dimension_semantics=("arbitrary",),
                                             vmem_limit_bytes=VMEM_LIMIT),
        name="mlstm",
    )(q, kt, v, o, gatest, *weights, *scales)


N_S5_PREP_OPERANDS = 11


def _s5_prep_pair(p, arow_ref, acol_ref, bre_ref, bim_ref, cre_ref, cim_ref, drow_ref,
                  rep_ref, rep2_ref, selrev_ref, pwcol_ref, w1_ref, wout_ref, al_ref, *, chunk):
    L = chunk
    P = S5_STATE
    Hc = S5_GROUP
    R = L * Hc
    hi = lax.Precision.HIGHEST

    def select(x, sel_ref):
        x_hi = x.astype(BF16)
        x_lo = (x - x_hi.astype(F32)).astype(BF16)
        sel = sel_ref[...]
        return (jnp.dot(x_hi, sel, preferred_element_type=F32) + jnp.dot(x_lo, sel, preferred_element_type=F32))

    dt_r = jnp.exp(arow_ref[p, 2:3, :])
    da_re, da_im = dt_r * arow_ref[p, 0:1, :], dt_r * arow_ref[p, 1:2, :]

    def abar_pow(tau):
        mag = jnp.exp(tau * da_re)
        ang = tau * da_im
        return mag * jnp.cos(ang), mag * jnp.sin(ang)

    n_pow = -(-(L + 1) // SUBLANES) * SUBLANES
    er_re, er_im = abar_pow(lax.broadcasted_iota(jnp.int32, (n_pow, 1), 0).astype(F32))
    pad = jnp.zeros((2 * P - n_pow, 2 * P), F32)
    ec_re = jnp.concatenate([er_re, pad], axis=0).T
    ec_im = jnp.concatenate([er_im, pad], axis=0).T

    a_re_c, a_im_c = acol_ref[p, :, 0:1], acol_ref[p, :, 1:2]
    ab_re, ab_im = ec_re[:, 1:2], ec_im[:, 1:2]
    den = a_re_c * a_re_c + a_im_c * a_im_c
    nr = ab_re - 1.0
    z_re = (nr * a_re_c + ab_im * a_im_c) / den
    z_im = (ab_im * a_re_c - nr * a_im_c) / den
    bb_re = z_re * bre_ref[p] - z_im * bim_ref[p]
    bb_im = z_re * bim_ref[p] + z_im * bre_ref[p]
    first = lax.broadcasted_iota(jnp.int32, (2 * P, Hc), 0) < P
    bd = lambda b: jnp.concatenate([jnp.where(first, b, 0.0), jnp.where(first, 0.0, b)], axis=1)

    def rows(e, lo):
        return jnp.concatenate([jnp.broadcast_to(e[lo + t:lo + t + 1, :], (Hc, 2 * P)) for t in range(L)], axis=0)

    c_re = jnp.concatenate([cre_ref[p]] * L, axis=0)
    c_im = jnp.concatenate([cim_ref[p]] * L, axis=0)

    def c_times_pow(lo):
        e_re, e_im = rows(er_re, lo), rows(er_im, lo)
        return c_re * e_re - c_im * e_im, c_re * e_im + c_im * e_re

    ce_re, ce_im = c_times_pow(0)
    kflat = (jnp.dot(ce_re, bd(bb_re), preferred_element_type=F32, precision=hi)
             - jnp.dot(ce_im, bd(bb_im), preferred_element_type=F32, precision=hi))
    row = lax.broadcasted_iota(jnp.int32, (R, 2 * Hc), 0)
    col = lax.broadcasted_iota(jnp.int32, (R, 2 * Hc), 1) & (Hc - 1)
    kflat = kflat + jnp.where(row == col, drow_ref[p], 0.0)

    m = select(kflat, rep2_ref)
    blk = (lax.broadcasted_iota(jnp.int32, (R, 2 * R), 1) & (R - 1)) // Hc
    sh = 1
    while sh < L:
        shifted = jnp.concatenate([jnp.zeros((sh * Hc, 2 * R), F32), m[0:R - sh * Hc, :]], axis=0)
        m = jnp.where((blk & sh) != 0, shifted, m)
        sh *= 2

    ev_re, ev_im = select(ec_re, selrev_ref), select(ec_im, selrev_ref)
    bt_re, bt_im = select(bb_re, rep_ref), select(bb_im, rep_ref)
    st_re = (ev_re * bt_re - ev_im * bt_im).astype(BF16)
    st_im = (ev_re * bt_im + ev_im * bt_re).astype(BF16)

    co_re, co_im = c_times_pow(1)
    lane_first = lax.broadcasted_iota(jnp.int32, (R, 2 * P), 1) < P
    for g in range(2):
        w1_ref[2 * p + g, 0:R, :] = m[:, g * R:(g + 1) * R].astype(BF16)
        w1_ref[2 * p + g, R:R + P, :] = st_re[g * P:(g + 1) * P, :]
        w1_ref[2 * p + g, R + P:R + 2 * P, :] = st_im[g * P:(g + 1) * P, :]
        own = lane_first if g == 0 else jnp.logical_not(lane_first)
        wout_ref[2 * p + g, :, 0:2 * P] = jnp.where(own, co_re, 0.0).astype(BF16)
        wout_ref[2 * p + g, :, 2 * P:4 * P] = jnp.where(own, -co_im, 0.0).astype(BF16)

    sc_re, sc_im = abar_pow(pwcol_ref[...])
    al_ref[p, 0:SUBLANES, :] = sc_re
    al_ref[p, SUBLANES:2 * SUBLANES, :] = sc_im


def _s5_prep_operands(a_re, a_im, log_dt, b_re, b_im, c_re, c_im, d_skip, *, chunk, n_chunks):
    G, P = a_re.shape
    Hc = S5_GROUP
    R = chunk * Hc
    assert G % 2 == 0 and 2 * P == LANES
    G2 = G // 2
    ldt = jnp.broadcast_to(log_dt[:, None], (G, P))
    arow = jnp.stack([a_re.reshape(G2, 2 * P), a_im.reshape(G2, 2 * P), ldt.reshape(G2, 2 * P)], axis=1)
    acol = jnp.stack([a_re.reshape(G2, 2 * P), a_im.reshape(G2, 2 * P)], axis=2)
    pair_lanes = lambda c: c.reshape(G2, 2, Hc, P).transpose(0, 2, 1, 3).reshape(G2, Hc, 2 * P)
    drow = d_skip.reshape(G2, 1, 2 * Hc)
    rep = (jnp.arange(R)[None, :] % Hc == jnp.arange(Hc)[:, None]).astype(BF16)
    rep2 = jnp.kron(jnp.eye(2, dtype=BF16), rep)
    selrev = (jnp.arange(LANES)[:, None] == chunk - 1 - jnp.arange(R)[None, :] // Hc).astype(BF16)
    n_steps = max(1, (n_chunks - 1).bit_length())
    assert chunk + 1 <= LANES and n_steps <= SUBLANES
    pwcol = jnp.zeros((SUBLANES, 1), F32).at[:n_steps, 0].set(chunk * 2.0 ** jnp.arange(n_steps))
    return [arow, acol, b_re.reshape(G2, 2 * P, Hc), b_im.reshape(G2, 2 * P, Hc), pair_lanes(c_re), pair_lanes(c_im),
            drow, rep, rep2, selrev, pwcol]


def _s5_kernel(*refs, chunk, n_chunks):
    L = chunk
    n_oct = L // SUBLANES
    u_refs, prep_refs = refs[0:n_oct], refs[n_oct:n_oct + N_S5_PREP_OPERANDS]
    y_ref, ut_ref, yt_ref, w1_ref, wout_ref, al_ref = refs[n_oct + N_S5_PREP_OPERANDS:]
    P = S5_STATE
    Hc = S5_GROUP
    R = L * Hc
    C = y_ref.shape[1] // SUBLANES
    n_grp = LANES // Hc

    @pl.when(pl.program_id(1) == 0)
    def _():
        for p in range(n_grp // 2):
            _s5_prep_pair(p, *prep_refs, w1_ref, wout_ref, al_ref, chunk=chunk)

    for s in range(L):
        xt = u_refs[s // SUBLANES][pl.ds(s % SUBLANES, C, stride=SUBLANES), :].astype(BF16).T
        for j in range(n_grp):
            ut_ref[j, s * Hc:(s + 1) * Hc, :] = xt[j * Hc:(j + 1) * Hc, :]

    seg_pos = lax.broadcasted_iota(jnp.int32, (C, 2 * P), 0) & (n_chunks - 1)
    nt = (((1,), (1,)), ((), ()))

    def shifted(x, sh):
        return jnp.where(seg_pos >= sh, pltpu.roll(x, sh, 0), 0.0)

    def group_pair(i, carry):
        j0 = 2 * i
        res = [jnp.dot(w1_ref[j0 + d], ut_ref[j0 + d], preferred_element_type=F32) for d in range(2)]
        x_re = jnp.concatenate([res[0][R:R + P], res[1][R:R + P]], axis=0).T
        x_im = jnp.concatenate([res[0][R + P:R + 2 * P], res[1][R + P:R + 2 * P]], axis=0).T
        sh, k = 1, 0
        while sh < n_chunks:
            a_re = al_ref[i, k:k + 1, :]
            a_im = al_ref[i, SUBLANES + k:SUBLANES + k + 1, :]
            s_re, s_im = shifted(x_re, sh), shifted(x_im, sh)
            x_re, x_im = (x_re + a_re * s_re - a_im * s_im, x_im + a_re * s_im + a_im * s_re)
            sh *= 2
            k += 1
        xprev = jnp.concatenate([shifted(x_re, 1), shifted(x_im, 1)], axis=1).astype(BF16)
        for d in range(2):
            yt_ref[j0 + d] = res[d][0:R, :] + lax.dot_general(wout_ref[j0 + d], xprev, nt,
                                                              preferred_element_type=F32)
        return carry

    lax.fori_loop(0, n_grp // 2, group_pair, 0, unroll=2)

    for t in range(L):
        zt = jnp.concatenate([yt_ref[j, t * Hc:(t + 1) * Hc, :] for j in range(n_grp)], axis=0)
        y_ref[t // SUBLANES, pl.ds(t % SUBLANES, C, stride=SUBLANES), :] = zt.T


def _s5(u, prep, *, chunk, n_chunks, n_col_blocks):
    n_oct, t_oct, ds5 = u.shape
    T = n_oct * t_oct
    assert n_oct * SUBLANES == chunk and len(prep) == N_S5_PREP_OPERANDS
    P = S5_STATE
    R = chunk * S5_GROUP
    n_grp = LANES // S5_GROUP
    n_q = ds5 // LANES
    assert prep[0].shape[0] * 2 == n_q * n_grp
    tb = T // n_col_blocks
    C = tb // chunk
    kern = functools.partial(_s5_kernel, chunk=chunk, n_chunks=n_chunks)
    octets = [pl.BlockSpec((None, C * SUBLANES, LANES), functools.partial(lambda q, i, o: (o, i, q), o=o))
              for o in range(n_oct)]
    pairs = lambda a: pl.BlockSpec((n_grp // 2,) + a.shape[1:], lambda q, i: (q,) + (0,) * (a.ndim - 1))
    prep_specs = [pairs(a) for a in prep[0:7]] + [_const_spec(a.shape) for a in prep[7:]]
    return pl.pallas_call(
        kern,
        grid=(n_q, n_col_blocks),
        in_specs=octets + prep_specs,
        out_specs=pl.BlockSpec((n_oct, C * SUBLANES, LANES), lambda q, i: (0, i, q)),
        out_shape=jax.ShapeDtypeStruct((n_oct, t_oct, ds5), F32),
        scratch_shapes=[pltpu.VMEM((n_grp, R, C), BF16), pltpu.VMEM((n_grp, R, C), F32),
                        pltpu.VMEM((n_grp, R + 2 * P, R), BF16), pltpu.VMEM((n_grp, R, 4 * P), BF16),
                        pltpu.VMEM((n_grp // 2, 2 * SUBLANES, 2 * P), F32)],
        compiler_params=pltpu.CompilerParams(dimension_semantics=("arbitrary", "arbitrary"),
                                             vmem_limit_bytes=BIG_VMEM_LIMIT),
        name="s5",
    )(*([u] * n_oct), *prep)


def _out_kernel(x_ref, hm_ref, y_ref, wglu_hbm, bglu_ref, wo_hbm, wgate_hbm, wup_hbm, wdown_hbm, gf_ref,
                out_ref, acc_ref, xn_ref, wglu_ref, wo_ref, wgate_ref, wup_ref, wdown_ref, sems,
                *, ff_chunk, final_norm):
    dm = hm_ref.shape[1]
    copies = [pltpu.make_async_copy(src, dst, sems.at[n]) for n, (src, dst) in enumerate(
        [(wglu_hbm, wglu_ref), (wo_hbm, wo_ref), (wgate_hbm, wgate_ref), (wup_hbm, wup_ref), (wdown_hbm, wdown_ref)])]
    first_step = pl.program_id(0) == 0

    @pl.when(first_step)
    def _():
        for cp in copies:
            cp.start(priority=1)
        copies[0].wait()
        copies[1].wait()

    n_oct, rows_oct, ds5 = y_ref.shape
    y = jnp.stack([y_ref[oc].reshape(rows_oct // SUBLANES, SUBLANES, ds5) for oc in range(n_oct)], axis=1)
    g = jax.nn.gelu(y.reshape(n_oct * rows_oct, ds5))
    z = jnp.dot(g.astype(BF16), wglu_ref[...], preferred_element_type=F32) + bglu_ref[...]
    gg = g * _sigmoid(z)
    hs = (gg * lax.rsqrt(jnp.mean(gg * gg, axis=-1, keepdims=True) + EPS)).astype(BF16)
    h1 = (x_ref[...] + jnp.dot(hm_ref[...], wo_ref[0:dm, :], preferred_element_type=F32)
          + jnp.dot(hs, wo_ref[dm:, :], preferred_element_type=F32))
    xn_ref[...] = (h1 * lax.rsqrt(jnp.mean(h1 * h1, axis=-1, keepdims=True) + EPS)).astype(BF16)
    acc_ref[...] = h1

    @pl.when(first_step)
    def _():
        for cp in copies[2:]:
            cp.wait()

    def ffn_chunk(c, carry):
        cs = pl.ds(pl.multiple_of(c * ff_chunk, ff_chunk), ff_chunk)
        xn2 = xn_ref[...]
        gate = jnp.dot(xn2, wgate_ref[:, cs], preferred_element_type=F32)
        up = jnp.dot(xn2, wup_ref[:, cs], preferred_element_type=F32)
        act = (_silu(gate) * up).astype(BF16)
        acc_ref[...] += jnp.dot(act, wdown_ref[cs, :], preferred_element_type=F32)
        return carry

    lax.fori_loop(0, wgate_ref.shape[1] // ff_chunk, ffn_chunk, 0, unroll=4)
    out_ref[...] = _rmsnorm(acc_ref[...], gf_ref[...]) if final_norm else acc_ref[...]


def _out_block(x2, hm, y, wglu, bglu, wo, wgate, wup, wdown, gf, *, tm, ff_chunk, final_norm):
    T, D = x2.shape
    dm = hm.shape[1]
    n_oct, _, ds5 = y.shape
    assert wgate.shape[1] % ff_chunk == 0
    kern = functools.partial(_out_kernel, ff_chunk=ff_chunk, final_norm=final_norm)
    tok = lambda w: pl.BlockSpec((tm, w), lambda i: (i, 0))
    in_hbm = pl.BlockSpec(memory_space=pl.ANY)
    weights = (wglu, wo, wgate, wup, wdown)
    return pl.pallas_call(
        kern,
        grid=(T // tm,),
        in_specs=[tok(D), tok(dm), pl.BlockSpec((n_oct, tm // n_oct, ds5), lambda i: (0, i, 0)),
                  in_hbm, _const_spec((1, ds5)), in_hbm, in_hbm, in_hbm, in_hbm, _const_spec((1, D))],
        out_specs=tok(D),
        out_shape=jax.ShapeDtypeStruct((T, D), F32),
        scratch_shapes=([pltpu.VMEM((tm, D), F32), pltpu.VMEM((tm, D), BF16)]
                        + [pltpu.VMEM(w.shape, w.dtype) for w in weights]
                        + [pltpu.SemaphoreType.DMA((len(weights),))]),
        compiler_params=pltpu.CompilerParams(dimension_semantics=("arbitrary",),
                                             vmem_limit_bytes=BIG_VMEM_LIMIT),
        name="out_block",
    )(x2, hm, y, wglu, bglu, wo, wgate, wup, wdown, gf)


def kernel(x, norm1_g, w_in, if_bias, conv_qk, mlstm_norm_g, a_re, a_im, log_dt, b_re, b_im, c_re, c_im,
           d_skip, w_glu, b_glu, s5_norm_g, w_out, norm2_g, w_gate, w_up, w_down, norm_f_g):
    B, S, D = x.shape
    depth = w_in.shape[0]
    H = N_MLSTM_HEADS
    dm = mlstm_norm_g.shape[1]
    T = B * S
    L5 = S5_CHUNK
    nc5 = S // L5
    row = lambda a: a.reshape(1, -1).astype(F32)

    h = x.reshape(T, D)
    for l in range(depth):
        last = l == depth - 1
        gb = jnp.zeros((1, LANES), F32).at[0, 0:2 * H].set(if_bias[l])

        q, kt, v, o, gatest, u = _inproj(h, row(norm1_g[l]), w_in[l].T.astype(F32), gb, conv_qk[l].astype(F32),
                                         seq=S, tm=min(INPROJ_TM, S), d_mlstm=dm, n_oct=L5 // SUBLANES)
        ones = lambda n: jnp.ones((n,), F32)
        g_mix = jnp.concatenate([mlstm_norm_g[l], s5_norm_g[l]]).astype(F32)
        g_ffn = norm2_g[l].astype(F32)
        hm, wglu, wo, wgate, wup, wdown = _mlstm(
            q, kt, v, o, gatest,
            [w.astype(F32) for w in (w_glu[l], w_out[l], w_gate[l], w_up[l], w_down[l])],
            [ones(w_glu.shape[1]), g_mix, g_ffn, g_ffn, ones(w_down.shape[1])], batch=B, seq=S)

        prep = _s5_prep_operands(a_re[l], a_im[l], log_dt[l], b_re[l], b_im[l], c_re[l], c_im[l],
                                 d_skip[l], chunk=L5, n_chunks=nc5)
        y = _s5(u, prep, chunk=L5, n_chunks=nc5, n_col_blocks=S5_COL_BLOCKS)

        h = _out_block(h, hm, y, wglu, row(b_glu[l]), wo, wgate, wup, wdown, row(norm_f_g),
                       tm=OUT_TM, ff_chunk=FF_CHUNK, final_norm=last)
    return h.reshape(B, S, D)
```

```python
import functools
import math

import jax
import jax.numpy as jnp
from jax import lax
from jax.experimental import pallas as pl
from jax.experimental.pallas import tpu as pltpu

EPS = 1e-6
N_MLSTM_HEADS = 4
CONV_WIDTH = 4
S5_GROUP = 16
S5_STATE = 64

LANES = 128
SUBLANES = 8
MLSTM_CHUNK = 128
S5_CHUNK = 16
S5_COL_BLOCKS = 1
S5_PREP_PAIRS = 4
INPROJ_TM = 1024
OUT_TM = 1024
FF_CHUNK = 256
VMEM_LIMIT = 48 * 1024 * 1024
BIG_VMEM_LIMIT = 58 * 1024 * 1024

F32 = jnp.float32
BF16 = jnp.bfloat16


def _rmsnorm(x, g):
    ms = jnp.mean(x * x, axis=-1, keepdims=True)
    return x * lax.rsqrt(ms + EPS) * g


def _sigmoid(x):
    return 0.5 * jnp.tanh(0.5 * x) + 0.5


def _silu(x):
    h = 0.5 * x
    return h + h * jnp.tanh(h)


def _log_sigmoid(x):
    return -(jnp.maximum(-x, 0.0) + jnp.log1p(jnp.exp(-jnp.abs(x))))


def _const_spec(shape):
    return pl.BlockSpec(shape, lambda *_: (0,) * len(shape), pipeline_mode=pl.Buffered(1))


def _inproj_kernel(x_ref, g1_ref, wt_ref, gb_ref, cw_ref,
                   q_ref, kt_ref, v_ref, o_ref, gatest_ref, u_ref,
                   ext_ref, wm_ref, wg_ref, wu_ref, *, tiles_per_seq, d_mlstm, k_scale):
    tm = x_ref.shape[0]
    dm = d_mlstm
    H = N_MLSTM_HEADS
    halo = SUBLANES

    @pl.when(pl.program_id(0) == 0)
    def _():
        g1 = g1_ref[...]
        for c0 in range(0, 4 * dm, dm):
            wm_ref[:, c0:c0 + dm] = (wt_ref[c0:c0 + dm, :] * g1).T.astype(BF16)
        wu_ref[...] = (wt_ref[4 * dm + 2 * H:, :] * g1).T.astype(BF16)
        fill = jnp.zeros((LANES - 2 * H, wt_ref.shape[1]), F32)
        wg_ref[...] = jnp.concatenate([wt_ref[4 * dm:4 * dm + 2 * H, :] * g1, fill], axis=0).T.astype(BF16)

    x = x_ref[...]
    xn = (x * lax.rsqrt(jnp.mean(x * x, axis=-1, keepdims=True) + EPS)).astype(BF16)

    @pl.when(pl.program_id(0) % tiles_per_seq == 0)
    def _():
        ext_ref[0:halo, :] = jnp.zeros((halo, 2 * dm), F32)

    ext_ref[halo:halo + tm, :] = jnp.dot(xn, wm_ref[:, 0:2 * dm], preferred_element_type=F32)
    v = jnp.dot(xn, wm_ref[:, 2 * dm:3 * dm], preferred_element_type=F32)
    o = jnp.dot(xn, wm_ref[:, 3 * dm:4 * dm], preferred_element_type=F32)
    u = jnp.dot(xn, wu_ref[...], preferred_element_type=F32)
    gates = jnp.dot(xn, wg_ref[...], preferred_element_type=F32) + gb_ref[...]
    v_ref[...] = v.astype(BF16)
    o_ref[...] = o.astype(BF16)
    n_oct = u_ref.shape[0]
    u4 = u.reshape(tm // (n_oct * SUBLANES), n_oct, SUBLANES, u.shape[1])
    for oc in range(n_oct):
        u_ref[oc] = u4[:, oc].reshape(tm // n_oct, u.shape[1])
    gatest_ref[...] = gates.T[0:2 * H, :]

    def zero_of(t):
        w = pltpu.bitcast(t[tm - SUBLANES:tm, t.shape[1] - LANES:], jnp.uint32)
        return pltpu.bitcast((w >> 16) >> 16, F32)
    anchor = zero_of(v)

    acc = cw_ref[CONV_WIDTH - 1:CONV_WIDTH, :] * ext_ref[halo:halo + tm, :]
    for j in range(1, CONV_WIDTH):
        acc = acc + cw_ref[CONV_WIDTH - 1 - j:CONV_WIDTH - j, :] * ext_ref[halo - j:halo - j + tm, :]
    ext_ref[0:halo, :] = ext_ref[tm:tm + halo, :]
    qk = _silu(acc) + jnp.tile(anchor, (tm // SUBLANES, 2 * dm // LANES))
    q_ref[...] = qk[:, 0:dm].astype(BF16)
    kt_ref[...] = (qk[:, dm:2 * dm] * k_scale).T.astype(BF16)


def _inproj(x2, g1, w_in_t, gb, cw, *, seq, tm, d_mlstm, n_oct):
    T, D = x2.shape
    dm = d_mlstm
    H = N_MLSTM_HEADS
    assert 2 * H == SUBLANES
    ds5 = w_in_t.shape[0] - 4 * dm - 2 * H
    kern = functools.partial(_inproj_kernel, tiles_per_seq=seq // tm, d_mlstm=dm,
                             k_scale=float((dm // H) ** -0.5))
    tok = lambda w: pl.BlockSpec((tm, w), lambda i: (i, 0))
    return pl.pallas_call(
        kern,
        grid=(T // tm,),
        in_specs=[tok(D), _const_spec((1, D)), _const_spec(w_in_t.shape), _const_spec((1, LANES)),
                  _const_spec(cw.shape)],
        out_specs=[tok(dm), pl.BlockSpec((dm, tm), lambda i: (0, i)), tok(dm), tok(dm),
                   pl.BlockSpec((2 * H, tm), lambda i: (0, i)),
                   pl.BlockSpec((n_oct, tm // n_oct, ds5), lambda i: (0, i, 0))],
        out_shape=[jax.ShapeDtypeStruct((T, dm), BF16), jax.ShapeDtypeStruct((dm, T), BF16),
                   jax.ShapeDtypeStruct((T, dm), BF16), jax.ShapeDtypeStruct((T, dm), BF16),
                   jax.ShapeDtypeStruct((2 * H, T), F32),
                   jax.ShapeDtypeStruct((n_oct, T // n_oct, ds5), F32)],
        scratch_shapes=[pltpu.VMEM((tm + 2 * SUBLANES, 2 * dm), F32), pltpu.VMEM((D, 4 * dm), BF16),
                        pltpu.VMEM((D, LANES), BF16), pltpu.VMEM((D, ds5), BF16)],
        compiler_params=pltpu.CompilerParams(dimension_semantics=("arbitrary",),
                                             vmem_limit_bytes=VMEM_LIMIT),
        name="inproj",
    )(x2, g1, w_in_t, gb, cw)


def _lane_scan(x, op, identity, seg):
    pos = lax.broadcasted_iota(jnp.int32, x.shape, x.ndim - 1) & (seg - 1)
    sh = 1
    while sh < seg:
        x = op(x, jnp.where(pos >= sh, pltpu.roll(x, sh, x.ndim - 1), identity))
        sh *= 2
    return x


def _mlstm_kernel(q_ref, kt_ref, v_ref, o_ref, gatest_ref, *refs, chunk, n_cast):
    S = q_ref.shape[0]
    H = N_MLSTM_HEADS
    dh = q_ref.shape[1] // H
    L = chunk
    assert L == dh == LANES
    w32_refs, ws_refs = refs[0:n_cast], refs[n_cast:2 * n_cast]
    out_ref, w16_refs = refs[2 * n_cast], refs[2 * n_cast + 1:3 * n_cast + 1]
    a_ref, g0_ref, nm_ref, gend_ref = refs[3 * n_cast + 1:3 * n_cast + 5]
    c_refs, nd_refs = refs[3 * n_cast + 5:3 * n_cast + 5 + H], refs[3 * n_cast + 5 + H:3 * n_cast + 5 + 2 * H]
    stage_refs, sems = refs[3 * n_cast + 5 + 2 * H:4 * n_cast + 5 + 2 * H], refs[4 * n_cast + 5 + 2 * H]

    side_copies = []
    for n, (w32_ref, stage_ref) in enumerate(zip(w32_refs, stage_refs)):
        rows = stage_ref.shape[0]
        r0 = pl.multiple_of(pl.program_id(0) * rows, SUBLANES)
        side_copies.append(pltpu.make_async_copy(w32_ref.at[pl.ds(r0, rows)], stage_ref, sems.at[n]))
    for cp in side_copies:
        cp.start(priority=1)

    @pl.when(pl.program_id(0) == 0)
    def _():
        gates = gatest_ref[...]
        f_cum = pltpu.roll(_lane_scan(_log_sigmoid(gates), jnp.add, 0.0, S), H, 0)
        a = gates - f_cum
        g0 = jnp.maximum(_lane_scan(a, jnp.maximum, -jnp.inf, S), 0.0)
        a_ref[...] = a
        g0_ref[...] = g0
        nm_ref[...] = -(f_cum + g0)
        for c in range(gatest_ref.shape[1] // L):
            gend_ref[:, c * L:(c + 1) * L] = jnp.broadcast_to(g0[:, (c + 1) * L - 1:(c + 1) * L], (SUBLANES, L))

    t0 = pl.program_id(0) * S

    for c_ref in c_refs:
        c_ref[...] = jnp.zeros(c_ref.shape, F32)
    causal = (lax.broadcasted_iota(jnp.int32, (L, L), 0) >= lax.broadcasted_iota(jnp.int32, (L, L), 1))
    ones_blk = jnp.ones((L, dh), BF16)
    mean_mat = jnp.full((dh, dh), 1.0 / dh, BF16)

    def mix(c, h, g_prev):
        r0 = pl.multiple_of(c * L, L)
        hs = slice(h * dh, (h + 1) * dh)
        qc = q_ref[pl.ds(r0, L), hs]
        ktc = kt_ref[hs, pl.ds(r0, L)]
        v_aug = jnp.concatenate([v_ref[pl.ds(r0, L), hs], ones_blk], axis=1)
        g0l = pl.ds(pl.multiple_of(t0 + r0, L), L)
        a_row = a_ref[h:h + 1, g0l]
        g_end = gend_ref[h:h + 1, g0l]
        g0_t = jnp.broadcast_to(g0_ref[h:h + 1, g0l], (L, L)).T

        s_qk = jnp.dot(qc, ktc, preferred_element_type=F32)
        p = (jnp.exp(jnp.where(causal, a_row - g0_t, -jnp.inf)) * s_qk).astype(BF16)
        c_prev = c_refs[h][...]
        q_w = (qc.astype(F32) * jnp.exp(g_prev - g0_t)).astype(BF16)
        nd_refs[h][...] = jnp.dot(jnp.concatenate([q_w, p], axis=1),
                                  jnp.concatenate([c_prev.astype(BF16), v_aug], axis=0),
                                  preferred_element_type=F32)

        kte = (ktc.astype(F32) * jnp.exp(a_row - g_end)).astype(BF16)
        decay = jnp.exp(g_prev - g_end)
        c_refs[h][...] = (jnp.concatenate([decay, decay], axis=1) * c_prev
                          + jnp.dot(kte, v_aug, preferred_element_type=F32))
        return g_end

    def emit(c, h):
        r0 = pl.multiple_of(c * L, L)
        hs = slice(h * dh, (h + 1) * dh)
        nm_t = jnp.broadcast_to(nm_ref[h:h + 1, pl.ds(pl.multiple_of(t0 + r0, L), L)], (L, L)).T
        nd = nd_refs[h][...]
        h_tilde = nd[:, 0:dh] / jnp.maximum(jnp.abs(nd[:, dh:2 * dh]), jnp.exp(nm_t))
        hm = _sigmoid(o_ref[pl.ds(r0, L), hs].astype(F32)) * h_tilde
        mu = jnp.dot(hm.astype(BF16), mean_mat, preferred_element_type=F32)
        hc = hm - mu
        var = jnp.dot((hc * hc).astype(BF16), mean_mat, preferred_element_type=F32)
        out_ref[pl.ds(r0, L), hs] = (hc * lax.rsqrt(var + EPS)).astype(BF16)

    zero = jnp.zeros((1, L), F32)
    g_first = tuple(mix(0, h, zero) for h in range(H))

    def body(c, g_carry):
        for h in range(H):
            emit(c - 1, h)
        return tuple(mix(c, h, g_carry[h]) for h in range(H))

    lax.fori_loop(1, S // L, body, g_first, unroll=5)
    for h in range(H):
        emit(S // L - 1, h)

    for cp, stage_ref, ws_ref, w16_ref in zip(side_copies, stage_refs, ws_refs, w16_refs):
        cp.wait()
        w16_ref[...] = (stage_ref[...] * ws_ref[...]).astype(BF16)


def _mlstm(q, kt, v, o, gatest, weights, row_scales, *, batch, seq):
    T, dm = q.shape
    dh = dm // N_MLSTM_HEADS
    kern = functools.partial(_mlstm_kernel, chunk=MLSTM_CHUNK, n_cast=len(weights))
    tok = lambda w: pl.BlockSpec((seq, w), lambda b: (b, 0))
    rows = pltpu.VMEM((SUBLANES, T), F32)
    bf16_rows = 2 * SUBLANES
    assert all(w.shape[0] % (batch * bf16_rows) == 0 for w in weights)
    w_specs = [pl.BlockSpec((w.shape[0] // batch, w.shape[1]), lambda b: (b, 0)) for w in weights]
    s_specs = [pl.BlockSpec((w.shape[0] // batch, 1), lambda b: (b, 0)) for w in weights]
    scales = [sc.reshape(-1, 1).astype(F32) for sc in row_scales]
    return pl.pallas_call(
        kern,
        grid=(batch,),
        in_specs=[tok(dm), pl.BlockSpec((dm, seq), lambda b: (0, b)), tok(dm), tok(dm),
                  _const_spec((SUBLANES, T))] + [pl.BlockSpec(memory_space=pl.ANY)] * len(weights) + s_specs,
        out_specs=[tok(dm)] + w_specs,
        out_shape=[jax.ShapeDtypeStruct((T, dm), BF16)] + [jax.ShapeDtypeStruct(w.shape, BF16) for w in weights],
        scratch_shapes=([rows, rows, rows, rows] + [pltpu.VMEM((dh, 2 * dh), F32)] * N_MLSTM_HEADS
                        + [pltpu.VMEM((MLSTM_CHUNK, 2 * dh), F32)] * N_MLSTM_HEADS
                        + [pltpu.VMEM((w.shape[0] // batch, w.shape[1]), F32) for w in weights]
                        + [pltpu.SemaphoreType.DMA((len(weights),))]),
        compiler_params=pltpu.CompilerParams(dimension_semantics=("arbitrary",),
                                             vmem_limit_bytes=VMEM_LIMIT),
        name="mlstm",
    )(q, kt, v, o, gatest, *weights, *scales)


def _s5_prep_kernel(arow_ref, acol_ref, bre_ref, bim_ref, cre_ref, cim_ref, drow_ref,
                    rep_ref, rep2_ref, selrev_ref, pwcol_ref, w1_ref, wout_ref, al_ref, *, chunk):
    for p in range(arow_ref.shape[0]):
        _s5_prep_pair(p, arow_ref, acol_ref, bre_ref, bim_ref, cre_ref, cim_ref, drow_ref,
                      rep_ref, rep2_ref, selrev_ref, pwcol_ref, w1_ref, wout_ref, al_ref, chunk=chunk)


def _s5_prep_pair(p, arow_ref, acol_ref, bre_ref, bim_ref, cre_ref, cim_ref, drow_ref,
                  rep_ref, rep2_ref, selrev_ref, pwcol_ref, w1_ref, wout_ref, al_ref, *, chunk):
    L = chunk
    P = S5_STATE
    Hc = S5_GROUP
    R = L * Hc
    hi = lax.Precision.HIGHEST

    def select(x, sel_ref):
        x_hi = x.astype(BF16)
        x_lo = (x - x_hi.astype(F32)).astype(BF16)
        sel = sel_ref[...]
        return (jnp.dot(x_hi, sel, preferred_element_type=F32) + jnp.dot(x_lo, sel, preferred_element_type=F32))

    dt_r = jnp.exp(arow_ref[p, 2:3, :])
    da_re, da_im = dt_r * arow_ref[p, 0:1, :], dt_r * arow_ref[p, 1:2, :]

    def abar_pow(tau):
        mag = jnp.exp(tau * da_re)
        ang = tau * da_im
        return mag * jnp.cos(ang), mag * jnp.sin(ang)

    n_pow = -(-(L + 1) // SUBLANES) * SUBLANES
    er_re, er_im = abar_pow(lax.broadcasted_iota(jnp.int32, (n_pow, 1), 0).astype(F32))
    pad = jnp.zeros((2 * P - n_pow, 2 * P), F32)
    ec_re = jnp.concatenate([er_re, pad], axis=0).T
    ec_im = jnp.concatenate([er_im, pad], axis=0).T

    a_re_c, a_im_c = acol_ref[p, :, 0:1], acol_ref[p, :, 1:2]
    ab_re, ab_im = ec_re[:, 1:2], ec_im[:, 1:2]
    den = a_re_c * a_re_c + a_im_c * a_im_c
    nr = ab_re - 1.0
    z_re = (nr * a_re_c + ab_im * a_im_c) / den
    z_im = (ab_im * a_re_c - nr * a_im_c) / den
    bb_re = z_re * bre_ref[p] - z_im * bim_ref[p]
    bb_im = z_re * bim_ref[p] + z_im * bre_ref[p]
    first = lax.broadcasted_iota(jnp.int32, (2 * P, Hc), 0) < P
    bd = lambda b: jnp.concatenate([jnp.where(first, b, 0.0), jnp.where(first, 0.0, b)], axis=1)

    def rows(e, lo):
        return jnp.concatenate([jnp.broadcast_to(e[lo + t:lo + t + 1, :], (Hc, 2 * P)) for t in range(L)], axis=0)

    c_re = jnp.concatenate([cre_ref[p]] * L, axis=0)
    c_im = jnp.concatenate([cim_ref[p]] * L, axis=0)

    def c_times_pow(lo):
        e_re, e_im = rows(er_re, lo), rows(er_im, lo)
        return c_re * e_re - c_im * e_im, c_re * e_im + c_im * e_re

    ce_re, ce_im = c_times_pow(0)
    kflat = (jnp.dot(ce_re, bd(bb_re), preferred_element_type=F32, precision=hi)
             - jnp.dot(ce_im, bd(bb_im), preferred_element_type=F32, precision=hi))
    row = lax.broadcasted_iota(jnp.int32, (R, 2 * Hc), 0)
    col = lax.broadcasted_iota(jnp.int32, (R, 2 * Hc), 1) & (Hc - 1)
    kflat = kflat + jnp.where(row == col, drow_ref[p], 0.0)

    m = select(kflat, rep2_ref)
    blk = (lax.broadcasted_iota(jnp.int32, (R, 2 * R), 1) & (R - 1)) // Hc
    sh = 1
    while sh < L:
        shifted = jnp.concatenate([jnp.zeros((sh * Hc, 2 * R), F32), m[0:R - sh * Hc, :]], axis=0)
        m = jnp.where((blk & sh) != 0, shifted, m)
        sh *= 2

    ev_re, ev_im = select(ec_re, selrev_ref), select(ec_im, selrev_ref)
    bt_re, bt_im = select(bb_re, rep_ref), select(bb_im, rep_ref)
    st_re = (ev_re * bt_re - ev_im * bt_im).astype(BF16)
    st_im = (ev_re * bt_im + ev_im * bt_re).astype(BF16)

    co_re, co_im = c_times_pow(1)
    lane_first = lax.broadcasted_iota(jnp.int32, (R, 2 * P), 1) < P
    for g in range(2):
        w1_ref[2 * p + g, 0:R, :] = m[:, g * R:(g + 1) * R].astype(BF16)
        w1_ref[2 * p + g, R:R + P, :] = st_re[g * P:(g + 1) * P, :]
        w1_ref[2 * p + g, R + P:R + 2 * P, :] = st_im[g * P:(g + 1) * P, :]
        own = lane_first if g == 0 else jnp.logical_not(lane_first)
        wout_ref[2 * p + g, :, 0:2 * P] = jnp.where(own, co_re, 0.0).astype(BF16)
        wout_ref[2 * p + g, :, 2 * P:4 * P] = jnp.where(own, -co_im, 0.0).astype(BF16)

    sc_re, sc_im = abar_pow(pwcol_ref[...])
    al_ref[p, 0:SUBLANES, :] = sc_re
    al_ref[p, SUBLANES:2 * SUBLANES, :] = sc_im


def _s5_prep(a_re, a_im, log_dt, b_re, b_im, c_re, c_im, d_skip, *, chunk, n_chunks):
    G, P = a_re.shape
    Hc = S5_GROUP
    R = chunk * Hc
    assert G % 2 == 0 and 2 * P == LANES
    G2 = G // 2
    n_pairs = math.gcd(G2, S5_PREP_PAIRS)
    ldt = jnp.broadcast_to(log_dt[:, None], (G, P))
    arow = jnp.stack([a_re.reshape(G2, 2 * P), a_im.reshape(G2, 2 * P), ldt.reshape(G2, 2 * P)], axis=1)
    acol = jnp.stack([a_re.reshape(G2, 2 * P), a_im.reshape(G2, 2 * P)], axis=2)
    pair_lanes = lambda c: c.reshape(G2, 2, Hc, P).transpose(0, 2, 1, 3).reshape(G2, Hc, 2 * P)
    drow = d_skip.reshape(G2, 1, 2 * Hc)
    rep = (jnp.arange(R)[None, :] % Hc == jnp.arange(Hc)[:, None]).astype(BF16)
    rep2 = jnp.kron(jnp.eye(2, dtype=BF16), rep)
    selrev = (jnp.arange(LANES)[:, None] == chunk - 1 - jnp.arange(R)[None, :] // Hc).astype(BF16)
    n_steps = max(1, (n_chunks - 1).bit_length())
    assert chunk + 1 <= LANES and n_steps <= SUBLANES
    pwcol = jnp.zeros((SUBLANES, 1), F32).at[:n_steps, 0].set(chunk * 2.0 ** jnp.arange(n_steps))
    kern = functools.partial(_s5_prep_kernel, chunk=chunk)
    pair = lambda *s: pl.BlockSpec((n_pairs,) + s, lambda g: (g,) + (0,) * len(s))
    two = lambda *s: pl.BlockSpec((2 * n_pairs,) + s, lambda g: (g,) + (0,) * len(s))
    return pl.pallas_call(
        kern,
        grid=(G2 // n_pairs,),
        in_specs=[pair(3, 2 * P), pair(2 * P, 2), pair(2 * P, Hc), pair(2 * P, Hc), pair(Hc, 2 * P), pair(Hc, 2 * P),
                  pair(1, 2 * Hc), _const_spec((Hc, R)), _const_spec((2 * Hc, 2 * R)), _const_spec((LANES, R)),
                  _const_spec((SUBLANES, 1))],
        out_specs=[two(R + 2 * P, R), two(R, 4 * P), pair(2 * SUBLANES, 2 * P)],
        out_shape=[jax.ShapeDtypeStruct((G, R + 2 * P, R), BF16),
                   jax.ShapeDtypeStruct((G, R, 4 * P), BF16),
                   jax.ShapeDtypeStruct((G2, 2 * SUBLANES, 2 * P), F32)],
        compiler_params=pltpu.CompilerParams(dimension_semantics=("arbitrary",),
                                             vmem_limit_bytes=VMEM_LIMIT),
        name="s5_prep",
    )(arow, acol, b_re.reshape(G2, 2 * P, Hc), b_im.reshape(G2, 2 * P, Hc), pair_lanes(c_re), pair_lanes(c_im),
      drow, rep, rep2, selrev, pwcol)


def _s5_kernel(*refs, chunk, n_chunks):
    L = chunk
    n_oct = L // SUBLANES
    u_refs, (w1_ref, wout_ref, al_ref, y_ref, ut_ref, yt_ref) = refs[0:n_oct], refs[n_oct:]
    P = S5_STATE
    Hc = S5_GROUP
    R = L * Hc
    C = y_ref.shape[1] // SUBLANES
    n_grp = LANES // Hc

    for s in range(L):
        xt = u_refs[s // SUBLANES][pl.ds(s % SUBLANES, C, stride=SUBLANES), :].astype(BF16).T
        for j in range(n_grp):
            ut_ref[j, s * Hc:(s + 1) * Hc, :] = xt[j * Hc:(j + 1) * Hc, :]

    seg_pos = lax.broadcasted_iota(jnp.int32, (C, 2 * P), 0) & (n_chunks - 1)
    nt = (((1,), (1,)), ((), ()))

    def shifted(x, sh):
        return jnp.where(seg_pos >= sh, pltpu.roll(x, sh, 0), 0.0)

    def group_pair(i, carry):
        j0 = 2 * i
        res = [jnp.dot(w1_ref[j0 + d], ut_ref[j0 + d], preferred_element_type=F32) for d in range(2)]
        x_re = jnp.concatenate([res[0][R:R + P], res[1][R:R + P]], axis=0).T
        x_im = jnp.concatenate([res[0][R + P:R + 2 * P], res[1][R + P:R + 2 * P]], axis=0).T
        sh, k = 1, 0
        while sh < n_chunks:
            a_re = al_ref[i, k:k + 1, :]
            a_im = al_ref[i, SUBLANES + k:SUBLANES + k + 1, :]
            s_re, s_im = shifted(x_re, sh), shifted(x_im, sh)
            x_re, x_im = (x_re + a_re * s_re - a_im * s_im, x_im + a_re * s_im + a_im * s_re)
            sh *= 2
            k += 1
        xprev = jnp.concatenate([shifted(x_re, 1), shifted(x_im, 1)], axis=1).astype(BF16)
        for d in range(2):
            yt_ref[j0 + d] = res[d][0:R, :] + lax.dot_general(wout_ref[j0 + d], xprev, nt,
                                                              preferred_element_type=F32)
        return carry

    lax.fori_loop(0, n_grp // 2, group_pair, 0, unroll=2)

    for t in range(L):
        zt = jnp.concatenate([yt_ref[j, t * Hc:(t + 1) * Hc, :] for j in range(n_grp)], axis=0)
        y_ref[t // SUBLANES, pl.ds(t % SUBLANES, C, stride=SUBLANES), :] = zt.T


def _s5(u, w1, wout, al, *, chunk, n_chunks, n_col_blocks):
    n_oct, t_oct, ds5 = u.shape
    T = n_oct * t_oct
    assert n_oct * SUBLANES == chunk
    G, R1, R = w1.shape
    P4 = wout.shape[-1]
    n_grp = LANES // S5_GROUP
    n_q = ds5 // LANES
    tb = T // n_col_blocks
    C = tb // chunk
    kern = functools.partial(_s5_kernel, chunk=chunk, n_chunks=n_chunks)
    grp = lambda *s: pl.BlockSpec((n_grp,) + s, lambda q, i: (q,) + (0,) * len(s))
    octets = [pl.BlockSpec((None, C * SUBLANES, LANES), functools.partial(lambda q, i, o: (o, i, q), o=o))
              for o in range(n_oct)]
    return pl.pallas_call(
        kern,
        grid=(n_q, n_col_blocks),
        in_specs=octets + [grp(R1, R), grp(R, P4),
                           pl.BlockSpec((n_grp // 2, 2 * SUBLANES, P4 // 2), lambda q, i: (q, 0, 0))],
        out_specs=pl.BlockSpec((n_oct, C * SUBLANES, LANES), lambda q, i: (0, i, q)),
        out_shape=jax.ShapeDtypeStruct((n_oct, t_oct, ds5), F32),
        scratch_shapes=[pltpu.VMEM((n_grp, R, C), BF16), pltpu.VMEM((n_grp, R, C), F32)],
        compiler_params=pltpu.CompilerParams(dimension_semantics=("arbitrary", "arbitrary"),
                                             vmem_limit_bytes=BIG_VMEM_LIMIT),
        name="s5",
    )(*([u] * n_oct), w1, wout, al)


def _out_kernel(x_ref, hm_ref, y_ref, wglu_hbm, bglu_ref, wo_hbm, wgate_hbm, wup_hbm, wdown_hbm, gf_ref,
                out_ref, acc_ref, xn_ref, wglu_ref, wo_ref, wgate_ref, wup_ref, wdown_ref, sems,
                *, ff_chunk, final_norm):
    dm = hm_ref.shape[1]
    copies = [pltpu.make_async_copy(src, dst, sems.at[n]) for n, (src, dst) in enumerate(
        [(wglu_hbm, wglu_ref), (wo_hbm, wo_ref), (wgate_hbm, wgate_ref), (wup_hbm, wup_ref), (wdown_hbm, wdown_ref)])]
    first_step = pl.program_id(0) == 0

    @pl.when(first_step)
    def _():
        for cp in copies:
            cp.start(priority=1)
        copies[0].wait()
        copies[1].wait()

    n_oct, rows_oct, ds5 = y_ref.shape
    y = jnp.stack([y_ref[oc].reshape(rows_oct // SUBLANES, SUBLANES, ds5) for oc in range(n_oct)], axis=1)
    g = jax.nn.gelu(y.reshape(n_oct * rows_oct, ds5))
    z = jnp.dot(g.astype(BF16), wglu_ref[...], preferred_element_type=F32) + bglu_ref[...]
    gg = g * _sigmoid(z)
    hs = (gg * lax.rsqrt(jnp.mean(gg * gg, axis=-1, keepdims=True) + EPS)).astype(BF16)
    h1 = (x_ref[...] + jnp.dot(hm_ref[...], wo_ref[0:dm, :], preferred_element_type=F32)
          + jnp.dot(hs, wo_ref[dm:, :], preferred_element_type=F32))
    xn_ref[...] = (h1 * lax.rsqrt(jnp.mean(h1 * h1, axis=-1, keepdims=True) + EPS)).astype(BF16)
    acc_ref[...] = h1

    @pl.when(first_step)
    def _():
        for cp in copies[2:]:
            cp.wait()

    def ffn_chunk(c, carry):
        cs = pl.ds(pl.multiple_of(c * ff_chunk, ff_chunk), ff_chunk)
        xn2 = xn_ref[...]
        gate = jnp.dot(xn2, wgate_ref[:, cs], preferred_element_type=F32)
        up = jnp.dot(xn2, wup_ref[:, cs], preferred_element_type=F32)
        act = (_silu(gate) * up).astype(BF16)
        acc_ref[...] += jnp.dot(act, wdown_ref[cs, :], preferred_element_type=F32)
        return carry

    lax.fori_loop(0, wgate_ref.shape[1] // ff_chunk, ffn_chunk, 0, unroll=4)
    out_ref[...] = _rmsnorm(acc_ref[...], gf_ref[...]) if final_norm else acc_ref[...]


def _out_block(x2, hm, y, wglu, bglu, wo, wgate, wup, wdown, gf, *, tm, ff_chunk, final_norm):
    T, D = x2.shape
    dm = hm.shape[1]
    n_oct, _, ds5 = y.shape
    assert wgate.shape[1] % ff_chunk == 0
    kern = functools.partial(_out_kernel, ff_chunk=ff_chunk, final_norm=final_norm)
    tok = lambda w: pl.BlockSpec((tm, w), lambda i: (i, 0))
    in_hbm = pl.BlockSpec(memory_space=pl.ANY)
    weights = (wglu, wo, wgate, wup, wdown)
    return pl.pallas_call(
        kern,
        grid=(T // tm,),
        in_specs=[tok(D), tok(dm), pl.BlockSpec((n_oct, tm // n_oct, ds5), lambda i: (0, i, 0)),
                  in_hbm, _const_spec((1, ds5)), in_hbm, in_hbm, in_hbm, in_hbm, _const_spec((1, D))],
        out_specs=tok(D),
        out_shape=jax.ShapeDtypeStruct((T, D), F32),
        scratch_shapes=([pltpu.VMEM((tm, D), F32), pltpu.VMEM((tm, D), BF16)]
                        + [pltpu.VMEM(w.shape, w.dtype) for w in weights]
                        + [pltpu.SemaphoreType.DMA((len(weights),))]),
        compiler_params=pltpu.CompilerParams(dimension_semantics=("arbitrary",),
                                             vmem_limit_bytes=BIG_VMEM_LIMIT),
        name="out_block",
    )(x2, hm, y, wglu, bglu, wo, wgate, wup, wdown, gf)


def kernel(x, norm1_g, w_in, if_bias, conv_qk, mlstm_norm_g, a_re, a_im, log_dt, b_re, b_im, c_re, c_im,
           d_skip, w_glu, b_glu, s5_norm_g, w_out, norm2_g, w_gate, w_up, w_down, norm_f_g):
    B, S, D = x.shape
    depth = w_in.shape[0]
    H = N_MLSTM_HEADS
    dm = mlstm_norm_g.shape[1]
    T = B * S
    L5 = S5_CHUNK
    nc5 = S // L5
    row = lambda a: a.reshape(1, -1).astype(F32)

    h = x.reshape(T, D)
    for l in range(depth):
        last = l == depth - 1
        gb = jnp.zeros((1, LANES), F32).at[0, 0:2 * H].set(if_bias[l])

        q, kt, v, o, gatest, u = _inproj(h, row(norm1_g[l]), w_in[l].T.astype(F32), gb, conv_qk[l].astype(F32),
                                         seq=S, tm=min(INPROJ_TM, S), d_mlstm=dm, n_oct=L5 // SUBLANES)
        ones = lambda n: jnp.ones((n,), F32)
        g_mix = jnp.concatenate([mlstm_norm_g[l], s5_norm_g[l]]).astype(F32)
        g_ffn = norm2_g[l].astype(F32)
        hm, wglu, wo, wgate, wup, wdown = _mlstm(
            q, kt, v, o, gatest,
            [w.astype(F32) for w in (w_glu[l], w_out[l], w_gate[l], w_up[l], w_down[l])],
            [ones(w_glu.shape[1]), g_mix, g_ffn, g_ffn, ones(w_down.shape[1])], batch=B, seq=S)

        w1, wout, al = _s5_prep(a_re[l], a_im[l], log_dt[l], b_re[l], b_im[l], c_re[l], c_im[l],
                                d_skip[l], chunk=L5, n_chunks=nc5)
        y = _s5(u, w1, wout, al, chunk=L5, n_chunks=nc5, n_col_blocks=S5_COL_BLOCKS)

        h = _out_block(h, hm, y, wglu, row(b_glu[l]), wo, wgate, wup, wdown, row(norm_f_g),
                       tm=OUT_TM, ff_chunk=FF_CHUNK, final_norm=last)
    return h.reshape(B, S, D)
```
